```python
import math
import jax, jax.numpy as jnp
from jax import lax
import numpy as np


D_MODEL = 2048
BATCH = 4
SEQ = 2048
DEPTH = 1
DEC_BATCH = 32
DEC_SEQ = 1
PAST_LEN = 8192
PAGE_SIZE = 128

N_HEADS = 16
HEAD_DIM = D_MODEL // N_HEADS
N_KV_HEADS = 4
GROUP = N_HEADS // N_KV_HEADS
D_ATTN = N_HEADS * HEAD_DIM
D_KV = N_KV_HEADS * HEAD_DIM
IDX_HEADS = 16
IDX_DIM = 64
TOPK_MAX = 256
D_CONV = D_MODEL // 2
CONV_WIDTH = 31
D_FF = 5632
FFN_CONV_WIDTH = 3
N_BUCKETS = 32
MAX_DISTANCE = 128
Q_BLOCK = 128
EPS = 1e-6
IN_SIZES = (2 * D_CONV, D_ATTN, D_KV, D_KV, IDX_HEADS * IDX_DIM, IDX_DIM, IDX_HEADS, D_MODEL, D_MODEL)
N_IN = sum(IN_SIZES)

kernel_name = 'hybrid_conformer_dsa_decoder_step'


def rmsnorm(x, g):
    xf = x.astype(jnp.float32)
    y = xf * lax.rsqrt(jnp.mean(xf * xf, axis=-1, keepdims=True) + EPS) * g.astype(jnp.float32)
    return y.astype(x.dtype)


def layernorm(x, g, b):
    xf = x.astype(jnp.float32)
    mu = jnp.mean(xf, axis=-1, keepdims=True)
    var = jnp.mean(jnp.square(xf - mu), axis=-1, keepdims=True)
    y = (xf - mu) * lax.rsqrt(var + EPS) * g.astype(jnp.float32) + b.astype(jnp.float32)
    return y.astype(x.dtype)


def split_cols(a, sizes):
    outs, start = [], 0
    for s in sizes:
        outs.append(a[..., start:start + s])
        start += s
    return outs


def depthwise_conv(x_ext, w, b):
    c = x_ext.shape[-1]
    y = lax.conv_general_dilated(x_ext, w[:, None, :].astype(x_ext.dtype), window_strides=(1,),
                                 padding='VALID', dimension_numbers=('NWC', 'WIO', 'NWC'),
                                 feature_group_count=c)
    return y + b.astype(y.dtype)


def rel_bucket(dist):
    n = jnp.maximum(dist, 0)
    max_exact = N_BUCKETS // 2
    nf = jnp.maximum(n, 1).astype(jnp.float32)
    large = max_exact + (jnp.log(nf / max_exact) / math.log(MAX_DISTANCE / max_exact)
                         * (N_BUCKETS - max_exact)).astype(jnp.int32)
    large = jnp.minimum(large, N_BUCKETS - 1)
    return jnp.where(n < max_exact, n, large)


def indexer_select(qi, wi, kidx, qpos, kpos, topk):
    s = jnp.einsum('nthd,nsd->nths', qi, kidx, preferred_element_type=jnp.float32)
    score = jnp.einsum('nth,nths->nts', wi.astype(jnp.float32), jax.nn.relu(s))
    score = jnp.where(kpos[None, None, :] <= qpos[None, :, None], score, -jnp.inf)
    _, idx = lax.top_k(score, topk)
    return idx


def attend_selected(q, k_sel, v_sel, qpos, idx, rel_bias):
    n, t = q.shape[:2]
    ksel = idx.shape[-1]
    qg = q.reshape(n, t, N_KV_HEADS, GROUP, HEAD_DIM)
    logits = jnp.einsum('ntkgd,ntskd->ntkgs', qg, k_sel,
                        preferred_element_type=jnp.float32) * (HEAD_DIM ** -0.5)
    dist = qpos[None, :, None] - idx
    bias = rel_bias[rel_bucket(dist)].astype(jnp.float32)
    bias = bias.reshape(n, t, ksel, N_KV_HEADS, GROUP).transpose(0, 1, 3, 4, 2)
    logits = jnp.where((dist >= 0)[:, :, None, None, :], logits + bias, -jnp.inf)
    p = jax.nn.softmax(logits, axis=-1)
    o = jnp.einsum('ntkgs,ntskd->ntkgd', p.astype(v_sel.dtype), v_sel)
    return o.reshape(n, t, D_ATTN)


def prompt_sparse_attention(q, k, v, qi, wi, kidx, rel_bias):
    b, s = q.shape[:2]
    nb = s // Q_BLOCK
    topk = min(TOPK_MAX, s // 4)
    kpos = jnp.arange(s)
    bidx = jnp.arange(b)[:, None, None]

    def to_blocks(a):
        return a.reshape((b, nb, Q_BLOCK) + a.shape[2:]).swapaxes(0, 1)

    def block(args):
        q_b, qi_b, wi_b, start = args
        qpos = start + jnp.arange(Q_BLOCK)
        idx = indexer_select(qi_b, wi_b, kidx, qpos, kpos, topk)
        return attend_selected(q_b, k[bidx, idx], v[bidx, idx], qpos, idx, rel_bias)

    out = lax.map(block, (to_blocks(q), to_blocks(qi), to_blocks(wi), jnp.arange(nb) * Q_BLOCK))
    return out.swapaxes(0, 1).reshape(b, s, D_ATTN)


def sample_sparse_attention(q, k_new, v_new, qi, wi, kidx_new, cache_k, cache_v, cache_kidx,
                            page_table, rel_bias):
    bd, t = q.shape[:2]
    length = PAST_LEN + t
    topk = min(TOPK_MAX, length // 4)
    kidx_past = cache_kidx[page_table].reshape(bd, PAST_LEN, IDX_DIM)
    kidx_all = jnp.concatenate([kidx_past.astype(kidx_new.dtype), kidx_new], axis=1)
    qpos = PAST_LEN + jnp.arange(t)
    idx = indexer_select(qi, wi, kidx_all, qpos, jnp.arange(length), topk)
    bidx = jnp.arange(bd)[:, None, None]
    pidx = jnp.minimum(idx, PAST_LEN - 1)
    phys = page_table[bidx, pidx // PAGE_SIZE]
    off = pidx % PAGE_SIZE
    nidx = jnp.clip(idx - PAST_LEN, 0, t - 1)
    is_past = (idx < PAST_LEN)[..., None, None]
    k_sel = jnp.where(is_past, cache_k[phys, off].astype(k_new.dtype), k_new[bidx, nidx])
    v_sel = jnp.where(is_past, cache_v[phys, off].astype(v_new.dtype), v_new[bidx, nidx])
    return attend_selected(q, k_sel, v_sel, qpos, idx, rel_bias)


def conv_branch(glu_pre, prev, dw_conv, b_dw_conv, ln_g, ln_b, w_conv_out):
    a, gate = jnp.split(glu_pre, 2, axis=-1)
    u = a * jax.nn.sigmoid(gate)
    ext = jnp.concatenate([prev.astype(u.dtype), u], axis=1)
    h = depthwise_conv(ext, dw_conv, b_dw_conv)
    h = jax.nn.silu(layernorm(h, ln_g, ln_b))
    return h @ w_conv_out, ext[:, -(CONV_WIDTH - 1):]


def conv_ffn(x, prev, norm_ffn, w_up, dw_ffn, b_dw_ffn, w_down):
    u = rmsnorm(x, norm_ffn) @ w_up
    ext = jnp.concatenate([prev.astype(u.dtype), u], axis=1)
    h = depthwise_conv(ext, dw_ffn, b_dw_ffn)
    g, val = jnp.split(h, 2, axis=-1)
    return x + (jax.nn.silu(g) * val) @ w_down, ext[:, -(FFN_CONV_WIDTH - 1):]


def decoder_layer(x, conv_prev, ffn_prev, attend, norm_attn, w_in, dw_conv, b_dw_conv, ln_conv_g,
                  ln_conv_b, w_conv_out, w_o, norm_ffn, w_up, dw_ffn, b_dw_ffn, w_down):
    n, t = x.shape[:2]
    xn = rmsnorm(x, norm_attn)
    glu_pre, q, k, v, qi, ki, wi, ga, gb = split_cols(xn @ w_in, IN_SIZES)
    q = q.reshape(n, t, N_HEADS, HEAD_DIM)
    k = k.reshape(n, t, N_KV_HEADS, HEAD_DIM)
    v = v.reshape(n, t, N_KV_HEADS, HEAD_DIM)
    qi = qi.reshape(n, t, IDX_HEADS, IDX_DIM) * (IDX_DIM ** -0.5)
    wi = wi * (IDX_HEADS ** -0.5)
    conv_out, conv_state = conv_branch(glu_pre, conv_prev, dw_conv, b_dw_conv, ln_conv_g, ln_conv_b,
                                       w_conv_out)
    attn_out = attend(q, k, v, qi, wi, ki)
    mixed = jax.nn.sigmoid(ga) * conv_out + jax.nn.sigmoid(gb) * attn_out
    x = x + mixed @ w_o
    x, ffn_state = conv_ffn(x, ffn_prev, norm_ffn, w_up, dw_ffn, b_dw_ffn, w_down)
    return x, k, v, ki, conv_state, ffn_state


def setup_inputs(seed: int = 0) -> dict:
    key = jax.random.key(seed)
    ks = jax.random.split(key, 24)
    n_pages = PAST_LEN // PAGE_SIZE
    n_pool = (DEC_BATCH * n_pages * 5) // 4
    f32 = jnp.float32

    def nrm(k, shape, scale):
        return jax.random.normal(k, shape, f32) * scale

    page_table = jax.random.permutation(ks[0], n_pool)[:DEC_BATCH * n_pages]
    page_table = page_table.reshape(DEC_BATCH, n_pages).astype(jnp.int32)
    return {
        'x_prompt': nrm(ks[1], (BATCH, SEQ, D_MODEL), 1.0),
        'x_sample': nrm(ks[2], (DEC_BATCH, DEC_SEQ, D_MODEL), 1.0),
        'cache_k': nrm(ks[3], (DEPTH, n_pool, PAGE_SIZE, N_KV_HEADS, HEAD_DIM), 1.0),
        'cache_v': nrm(ks[4], (DEPTH, n_pool, PAGE_SIZE, N_KV_HEADS, HEAD_DIM), 1.0),
        'cache_kidx': nrm(ks[5], (DEPTH, n_pool, PAGE_SIZE, IDX_DIM), 1.0),
        'state_conv': nrm(ks[6], (DEPTH, DEC_BATCH, CONV_WIDTH - 1, D_CONV), 0.5),
        'state_ffn': nrm(ks[7], (DEPTH, DEC_BATCH, FFN_CONV_WIDTH - 1, 2 * D_FF), 1.0),
        'page_table': page_table,
        'rel_bias': nrm(ks[8], (N_BUCKETS, N_HEADS), 0.5),
        'norm_attn': 1.0 + nrm(ks[9], (DEPTH, D_MODEL), 0.02),
        'w_in': nrm(ks[10], (DEPTH, D_MODEL, N_IN), D_MODEL ** -0.5),
        'dw_conv': nrm(ks[11], (DEPTH, CONV_WIDTH, D_CONV), CONV_WIDTH ** -0.5),
        'b_dw_conv': nrm(ks[12], (DEPTH, D_CONV), 0.02),
        'ln_conv_g': 1.0 + nrm(ks[13], (DEPTH, D_CONV), 0.02),
        'ln_conv_b': nrm(ks[14], (DEPTH, D_CONV), 0.02),
        'w_conv_out': nrm(ks[15], (DEPTH, D_CONV, D_MODEL), D_CONV ** -0.5),
        'w_o': nrm(ks[16], (DEPTH, D_MODEL, D_MODEL), D_MODEL ** -0.5),
        'norm_ffn': 1.0 + nrm(ks[17], (DEPTH, D_MODEL), 0.02),
        'w_up': nrm(ks[18], (DEPTH, D_MODEL, 2 * D_FF), D_MODEL ** -0.5),
        'dw_ffn': nrm(ks[19], (DEPTH, FFN_CONV_WIDTH, 2 * D_FF), FFN_CONV_WIDTH ** -0.5),
        'b_dw_ffn': nrm(ks[20], (DEPTH, 2 * D_FF), 0.02),
        'w_down': nrm(ks[21], (DEPTH, D_FF, D_MODEL), D_FF ** -0.5),
        'norm_final': 1.0 + nrm(ks[22], (D_MODEL,), 0.02),
    }


def reference(x_prompt, x_sample, cache_k, cache_v, cache_kidx, state_conv, state_ffn, page_table,
              rel_bias, norm_attn, w_in, dw_conv, b_dw_conv, ln_conv_g, ln_conv_b, w_conv_out, w_o,
              norm_ffn, w_up, dw_ffn, b_dw_ffn, w_down, norm_final):
    xp, xs = x_prompt, x_sample
    bp = x_prompt.shape[0]
    kp_l, vp_l, ip_l, cp_l, fp_l = [], [], [], [], []
    ks_l, vs_l, is_l, cs_l, fs_l = [], [], [], [], []
    for l in range(DEPTH):
        weights = (norm_attn[l], w_in[l], dw_conv[l], b_dw_conv[l], ln_conv_g[l], ln_conv_b[l],
                   w_conv_out[l], w_o[l], norm_ffn[l], w_up[l], dw_ffn[l], b_dw_ffn[l], w_down[l])

        def attend_prompt(q, k, v, qi, wi, ki):
            return prompt_sparse_attention(q, k, v, qi, wi, ki, rel_bias)

        def attend_sample(q, k, v, qi, wi, ki, l=l):
            return sample_sparse_attention(q, k, v, qi, wi, ki, cache_k[l], cache_v[l], cache_kidx[l],
                                           page_table, rel_bias)

        conv0 = jnp.zeros((bp, CONV_WIDTH - 1, D_CONV), xp.dtype)
        ffn0 = jnp.zeros((bp, FFN_CONV_WIDTH - 1, 2 * D_FF), xp.dtype)
        xp, kp, vp, ip, cp, fp = decoder_layer(xp, conv0, ffn0, attend_prompt, *weights)
        xs, kd, vd, idd, cd, fd = decoder_layer(xs, state_conv[l], state_ffn[l], attend_sample, *weights)
        kp_l.append(kp); vp_l.append(vp); ip_l.append(ip); cp_l.append(cp); fp_l.append(fp)
        ks_l.append(kd); vs_l.append(vd); is_l.append(idd); cs_l.append(cd); fs_l.append(fd)
    y_prompt = rmsnorm(xp, norm_final)
    y_sample = rmsnorm(xs, norm_final)
    return (y_prompt, y_sample,
            jnp.stack(kp_l), jnp.stack(vp_l), jnp.stack(ip_l), jnp.stack(cp_l), jnp.stack(fp_l),
            jnp.stack(ks_l), jnp.stack(vs_l), jnp.stack(is_l), jnp.stack(cs_l), jnp.stack(fs_l))
```

```python
import functools
import math

import numpy as np
import jax
import jax.numpy as jnp
from jax import lax
from jax.experimental import pallas as pl
from jax.experimental.pallas import tpu as pltpu

F32 = jnp.float32
BF16 = jnp.bfloat16

EPS = 1e-6
TOPK_MAX = 256
N_BUCKETS = 32
MAX_DISTANCE = 128
QB = 128
CK = 256
LANES = 128
NEG = -1e30
VMEM_LIMIT = 56 * 1024 * 1024


def _cparams(n_axes, vmem=VMEM_LIMIT):
    return pltpu.CompilerParams(dimension_semantics=("arbitrary",) * n_axes, vmem_limit_bytes=vmem)


def _dot_nt(a, b):
    return lax.dot_general(a, b, (((1,), (1,)), ((), ())), preferred_element_type=F32)


def _sigmoid(x):
    return 1.0 / (1.0 + jnp.exp(-x))


def _rel_bucket_np(dist):
    n = np.maximum(dist, 0)
    max_exact = N_BUCKETS // 2
    nf = np.maximum(n, 1).astype(np.float32)
    large = max_exact + (np.log(nf / np.float32(max_exact)) / np.float32(math.log(MAX_DISTANCE / max_exact))
                         * np.float32(N_BUCKETS - max_exact)).astype(np.int32)
    large = np.minimum(large, N_BUCKETS - 1)
    return np.where(n < max_exact, n, large).astype(np.int32)


def _rms_kernel(x_ref, g_ref, o_ref):
    x = x_ref[...]
    y = x * lax.rsqrt(jnp.mean(x * x, axis=-1, keepdims=True) + EPS) * g_ref[...]
    o_ref[...] = y.astype(o_ref.dtype)


def rmsnorm_rows(x, g, out_dtype, tm):
    m, d = x.shape
    return pl.pallas_call(
        _rms_kernel,
        grid=(m // tm,),
        in_specs=[pl.BlockSpec((tm, d), lambda i: (i, 0)), pl.BlockSpec((1, d), lambda i: (0, 0))],
        out_specs=pl.BlockSpec((tm, d), lambda i: (i, 0)),
        out_shape=jax.ShapeDtypeStruct((m, d), out_dtype),
        compiler_params=_cparams(1),
    )(x, g.reshape(1, d))


def _mm_kernel(a_ref, w_ref, o_ref):
    o_ref[...] = jnp.dot(a_ref[...], w_ref[...].astype(BF16), preferred_element_type=F32)


def _mm_res_kernel(a_ref, w_ref, r_ref, o_ref):
    o_ref[...] = r_ref[...] + jnp.dot(a_ref[...], w_ref[...].astype(BF16), preferred_element_type=F32)


def matmul_w(a, w, col0, ncols, tm, tn, res=None):
    m, k = a.shape
    assert col0 % tn == 0 and ncols % tn == 0 and m % tm == 0
    cb = col0 // tn
    in_specs = [pl.BlockSpec((tm, k), lambda i, j: (i, 0)),
                pl.BlockSpec((k, tn), lambda i, j: (0, j + cb))]
    args = [a, w]
    kern = _mm_kernel
    if res is not None:
        in_specs.append(pl.BlockSpec((tm, tn), lambda i, j: (i, j)))
        args.append(res)
        kern = _mm_res_kernel
    return pl.pallas_call(
        kern,
        grid=(m // tm, ncols // tn),
        in_specs=in_specs,
        out_specs=pl.BlockSpec((tm, tn), lambda i, j: (i, j)),
        out_shape=jax.ShapeDtypeStruct((m, ncols), F32),
        compiler_params=_cparams(2),
    )(*args)


def _bias_kernel(rb_ref, bk3_ref, bks_ref, o3_ref, os_ref, *, n_heads):
    def head(h, carry):
        for t in range(3):
            bk = bk3_ref[t]
            acc = jnp.zeros(bk.shape, F32)
            for b in range(N_BUCKETS):
                acc = jnp.where(bk == b, rb_ref[b, h], acc)
            o3_ref[t, h] = acc
        for t in range(2):
            bk = bks_ref[t]
            acc = jnp.zeros(bk.shape, F32)
            for b in range(N_BUCKETS):
                acc = jnp.where(bk == b, rb_ref[b, h], acc)
            os_ref[t, pl.ds(h, 1), :] = acc
        return carry
    lax.fori_loop(0, n_heads, head, 0)


def bias_tables(rel_bias, past_len):
    n_heads = rel_bias.shape[1]
    i = np.arange(QB)[:, None]
    k = np.arange(QB)[None, :]
    bk3 = np.stack([_rel_bucket_np(i - k + 2 * QB), _rel_bucket_np(i - k + QB), _rel_bucket_np(i - k)])
    assert (_rel_bucket_np(np.arange(QB + 1, 1 << 20)) == N_BUCKETS - 1).all()
    assert (bk3[0] == N_BUCKETS - 1).all()
    bks = np.stack([np.full((1, LANES), N_BUCKETS - 1, np.int32),
                    _rel_bucket_np(LANES - np.arange(LANES))[None, :]])
    return pl.pallas_call(
        functools.partial(_bias_kernel, n_heads=n_heads),
        in_specs=[pl.BlockSpec(memory_space=pltpu.SMEM),
                  pl.BlockSpec(memory_space=pltpu.VMEM), pl.BlockSpec(memory_space=pltpu.VMEM)],
        out_specs=[pl.BlockSpec(memory_space=pltpu.VMEM), pl.BlockSpec(memory_space=pltpu.VMEM)],
        out_shape=[jax.ShapeDtypeStruct((3, n_heads, QB, QB), F32),
                   jax.ShapeDtypeStruct((2, n_heads, LANES), F32)],
    )(rel_bias, jnp.asarray(bk3), jnp.asarray(bks))


def _conv_kernel(glu_ref, prev_ref, dw_ref, bdw_ref, lng_ref, lnb_ref, wout_ref, o_ref, st_ref,
                 ext_ref, h_ref, wbf_ref, *, tt, width, dconv):
    b = pl.program_id(0)
    t = pl.program_id(1)
    pad = 32
    hist = width - 1

    @pl.when((b == 0) & (t == 0))
    def _():
        wbf_ref[...] = wout_ref[...].astype(BF16)

    @pl.when(t == 0)
    def _():
        ext_ref[pl.ds(pad - hist, hist), :] = prev_ref[...]

    @pl.when(t > 0)
    def _():
        ext_ref[pl.ds(0, pad), :] = ext_ref[pl.ds(tt, pad), :]

    glu = glu_ref[...]
    u = glu[:, :dconv] * _sigmoid(glu[:, dconv:])
    ext_ref[pl.ds(pad, tt), :] = u
    st_ref[...] = ext_ref[pl.ds(pad + tt - hist, hist), :]

    for c in range(dconv // LANES):
        cs = slice(c * LANES, (c + 1) * LANES)
        acc = jnp.zeros((tt, LANES), F32) + bdw_ref[:, cs]
        for j in range(width):
            acc = acc + dw_ref[pl.ds(j, 1), cs] * ext_ref[pl.ds(pad - hist + j, tt), cs]
        h_ref[:, cs] = acc

    h = h_ref[...]
    mu = jnp.mean(h, axis=-1, keepdims=True)
    var = jnp.mean(jnp.square(h - mu), axis=-1, keepdims=True)
    y = (h - mu) * lax.rsqrt(var + EPS) * lng_ref[...] + lnb_ref[...]
    y = y * _sigmoid(y)
    o_ref[...] = jnp.dot(y.astype(BF16), wbf_ref[...], preferred_element_type=F32)


def conv_branch_prompt(glu_pre, prev, dw, bdw, lng, lnb, wout, n_seq, seq, tt):
    width, dconv = dw.shape
    dm = wout.shape[1]
    nt = seq // tt
    kern = functools.partial(_conv_kernel, tt=tt, width=width, dconv=dconv)
    return pl.pallas_call(
        kern,
        grid=(n_seq, nt),
        in_specs=[pl.BlockSpec((tt, 2 * dconv), lambda b, t: (b * nt + t, 0)),
                  pl.BlockSpec((None, width - 1, dconv), lambda b, t: (b, 0, 0)),
                  pl.BlockSpec((width, dconv), lambda b, t: (0, 0)),
                  pl.BlockSpec((1, dconv), lambda b, t: (0, 0)),
                  pl.BlockSpec((1, dconv), lambda b, t: (0, 0)),
                  pl.BlockSpec((1, dconv), lambda b, t: (0, 0)),
                  pl.BlockSpec((dconv, dm), lambda b, t: (0, 0))],
        out_specs=[pl.BlockSpec((tt, dm), lambda b, t: (b * nt + t, 0)),
                   pl.BlockSpec((None, width - 1, dconv), lambda b, t: (b, 0, 0))],
        out_shape=[jax.ShapeDtypeStruct((n_seq * seq, dm), F32),
                   jax.ShapeDtypeStruct((n_seq, width - 1, dconv), F32)],
        scratch_shapes=[pltpu.VMEM((32 + tt, dconv), F32), pltpu.VMEM((tt, dconv), F32),
                        pltpu.VMEM((dconv, dm), BF16)],
        compiler_params=_cparams(2),
    )(glu_pre, prev, dw, bdw.reshape(1, dconv), lng.reshape(1, dconv), lnb.reshape(1, dconv), wout)


def _conv_step_kernel(glu_ref, prev_ref, dw_ref, bdw_ref, lng_ref, lnb_ref, wout_ref, o_ref, u_ref,
                      *, width, dconv):
    glu = glu_ref[...]
    u = glu[:, :dconv] * _sigmoid(glu[:, dconv:])
    u_ref[...] = u
    h = bdw_ref[...] + dw_ref[pl.ds(width - 1, 1), :] * u
    for j in range(width - 1):
        h = h + dw_ref[pl.ds(j, 1), :] * prev_ref[j]
    mu = jnp.mean(h, axis=-1, keepdims=True)
    var = jnp.mean(jnp.square(h - mu), axis=-1, keepdims=True)
    y = (h - mu) * lax.rsqrt(var + EPS) * lng_ref[...] + lnb_ref[...]
    y = y * _sigmoid(y)
    o_ref[...] = jnp.dot(y.astype(BF16), wout_ref[...].astype(BF16), preferred_element_type=F32)


def conv_branch_step(glu_pre, prev_t, dw, bdw, lng, lnb, wout):
    width, dconv = dw.shape
    n = glu_pre.shape[0]
    dm = wout.shape[1]
    kern = functools.partial(_conv_step_kernel, width=width, dconv=dconv)
    return pl.pallas_call(
        kern,
        out_shape=[jax.ShapeDtypeStruct((n, dm), F32), jax.ShapeDtypeStruct((n, dconv), F32)],
        compiler_params=pltpu.CompilerParams(vmem_limit_bytes=VMEM_LIMIT),
    )(glu_pre, prev_t, dw, bdw.reshape(1, dconv), lng.reshape(1, dconv), lnb.reshape(1, dconv), wout)


def _select_threshold(count_gt, row_min, row_max, n_adm, topk, any_fn):
    kf = jnp.float32(topk)
    full = n_adm <= kf
    lo0 = row_min - (1.0 + jnp.abs(row_min))
    hi0 = row_max
    flo0 = jnp.where(full, kf, n_adm)
    fhi0 = jnp.zeros_like(lo0)

    def active_rows(lo, hi, flo):
        mid = 0.5 * lo + 0.5 * hi
        return (flo != kf) & (lo < mid) & (mid < hi)

    def cond(st):
        lo, hi, flo, fhi = st
        return any_fn(active_rows(lo, hi, flo))

    def body(st):
        lo, hi, flo, fhi = st
        act = active_rows(lo, hi, flo)
        mid = 0.5 * lo + 0.5 * hi
        c = count_gt(mid)
        up = act & (c >= kf)
        dn = act & (c < kf)
        return (jnp.where(up, mid, lo), jnp.where(dn, mid, hi),
                jnp.where(up, c, flo), jnp.where(dn, c, fhi))

    lo, hi, flo, fhi = lax.while_loop(cond, body, (lo0, hi0, flo0, fhi0))
    lo = jnp.where(full, -jnp.inf, lo)
    return lo, hi, flo, fhi


def _attn_prompt_kernel(qi_ref, wi_ref, kw_ref, q_ref, k_ref, v_ref, co_ref, ga_ref, gb_ref, bias_ref,
                        o_ref, kd_ref, kb_ref, vb_ref, wib_ref, qib_ref, sc_ref, sel_ref, qs_ref,
                        m_ref, l_ref, acc_ref,
                        *, seq, n_heads, n_kv, idx_heads, idx_dim, topk, hd):
    j = pl.program_id(1)
    group = n_heads // n_kv
    nck = (j * QB + QB + CK - 1) // CK
    lane = lax.broadcasted_iota(jnp.int32, (1, LANES), 1)
    kf = jnp.float32(topk)

    @pl.when(j == 0)
    def _():
        kw = kw_ref[...]
        kd_ref[0] = jnp.where(lane < idx_dim, kw, 0.0).astype(BF16)
        kd_ref[1] = jnp.where(lane >= idx_dim, pltpu.roll(kw, idx_dim, 1), 0.0).astype(BF16)
        for g in range(n_kv):
            kb_ref[g] = k_ref[:, g * hd:(g + 1) * hd].astype(BF16)
            vb_ref[g] = v_ref[:, g * hd:(g + 1) * hd].astype(BF16)

    wi = wi_ref[...] * (idx_heads ** -0.5)
    for h in range(idx_heads):
        wib_ref[h] = jnp.broadcast_to(wi[:, idx_dim + h:idx_dim + h + 1], (QB, CK))
    qib_ref[...] = (qi_ref[...] * (idx_dim ** -0.5)).astype(BF16)
    qpos = j * QB + lax.broadcasted_iota(jnp.int32, (QB, CK), 0)
    kcol = lax.broadcasted_iota(jnp.int32, (QB, CK), 1)
    per_pair = LANES // idx_dim

    def score_chunk(c, carry):
        mn, mx = carry
        k0 = pl.multiple_of(c * CK, CK)
        acc = jnp.zeros((QB, CK), F32)
        for p in range(idx_heads // per_pair):
            lhs = qib_ref[:, p * LANES:(p + 1) * LANES]
            for r in range(per_pair):
                s = _dot_nt(lhs, kd_ref[r, pl.ds(k0, CK), :])
                acc = acc + wib_ref[p * per_pair + r] * jnp.maximum(s, 0.0)
        adm = (kcol + c * CK) <= qpos
        sc_ref[c] = jnp.where(adm, acc, -jnp.inf)
        mn = jnp.minimum(mn, jnp.where(adm, acc, jnp.inf))
        mx = jnp.maximum(mx, jnp.where(adm, acc, -jnp.inf))
        return mn, mx

    mn, mx = lax.fori_loop(0, nck, score_chunk,
                           (jnp.full((QB, CK), jnp.inf, F32), jnp.full((QB, CK), -jnp.inf, F32)))
    row_min = jnp.min(mn, axis=1, keepdims=True)
    row_max = jnp.max(mx, axis=1, keepdims=True)
    n_adm = (j * QB + 1 + lax.broadcasted_iota(jnp.int32, (QB, 1), 0)).astype(F32)

    def count_gt(t):
        def cbody(c, acc):
            return acc + jnp.where(sc_ref[c] > t, 1.0, 0.0)
        part = lax.fori_loop(0, nck, cbody, jnp.zeros((QB, CK), F32))
        return jnp.sum(part, axis=1, keepdims=True)

    def any_fn(mask):
        return jnp.max(jnp.where(mask, 1.0, 0.0)) > 0.0

    lo, hi, flo, fhi = _select_threshold(count_gt, row_min, row_max, n_adm, topk, any_fn)
    tie = flo != kf

    def sel_chunk(c, carry):
        sel_ref[c] = jnp.where(sc_ref[c] > lo, 1.0, 0.0)
        return carry
    lax.fori_loop(0, nck, sel_chunk, 0)

    @pl.when(any_fn(tie))
    def _():
        need = kf - fhi
        tri = (lax.broadcasted_iota(jnp.int32, (CK, CK), 0)
               < lax.broadcasted_iota(jnp.int32, (CK, CK), 1)).astype(BF16)

        def tie_chunk(c, before):
            s = sc_ref[c]
            eq = s == hi
            rank = before + jnp.dot(jnp.where(eq, 1.0, 0.0).astype(BF16), tri, preferred_element_type=F32)
            keep = (s > hi) | (eq & (rank < need))
            sel_ref[c] = jnp.where(tie, jnp.where(keep, 1.0, 0.0), sel_ref[c])
            return before + jnp.sum(jnp.where(eq, 1.0, 0.0), axis=1, keepdims=True)
        lax.fori_loop(0, nck, tie_chunk, jnp.zeros((QB, 1), F32))

    scale = hd ** -0.5
    q = q_ref[...]
    jb = j * (QB // LANES)
    for g in range(n_kv):
        for hh in range(group):
            h = g * group + hh
            qs_ref[hh] = q[:, h * hd:(h + 1) * hd].astype(BF16)
        m_ref[...] = jnp.full(m_ref.shape, NEG, F32)
        l_ref[...] = jnp.zeros(l_ref.shape, F32)
        acc_ref[...] = jnp.zeros(acc_ref.shape, F32)

        def att_chunk(c, carry):
            k0 = pl.multiple_of(c * CK, CK)
            kc = kb_ref[g, pl.ds(k0, CK), :]
            vc = vb_ref[g, pl.ds(k0, CK), :]
            lg = _dot_nt(qs_ref[...].reshape(group * QB, hd), kc).reshape(group, QB, CK) * scale
            tiles = []
            for s in range(CK // QB):
                ti = jnp.clip(c * (CK // QB) + s - jb + 2, 0, 2)
                tiles.append(bias_ref[ti, pl.ds(g * group, group)])
            lg = lg + jnp.concatenate(tiles, axis=-1)
            msk = sel_ref[c][None] > 0.0
            lg = jnp.where(msk, lg, NEG)
            m_old = m_ref[...]
            m_new = jnp.maximum(m_old, jnp.max(lg, axis=-1, keepdims=True))
            alpha = jnp.exp(m_old - m_new)
            p = jnp.where(msk, jnp.exp(lg - m_new), 0.0)
            l_ref[...] = alpha * l_ref[...] + jnp.sum(p, axis=-1, keepdims=True)
            pv = jnp.dot(p.reshape(group * QB, CK).astype(BF16), vc, preferred_element_type=F32)
            acc_ref[...] = alpha * acc_ref[...] + pv.reshape(group, QB, hd)
            m_ref[...] = m_new
            return carry
        lax.fori_loop(0, nck, att_chunk, 0)

        o = acc_ref[...] / l_ref[...]
        for hh in range(group):
            cs = slice((g * group + hh) * hd, (g * group + hh + 1) * hd)
            mixed = _sigmoid(ga_ref[:, cs]) * co_ref[:, cs] + _sigmoid(gb_ref[:, cs]) * o[hh]
            o_ref[:, cs] = mixed.astype(o_ref.dtype)


def attn_prompt(qi, kw, q, k, v, conv_out, gates, bias3, n_seq, seq, n_kv, idx_dim, idx_heads):
    m, dm = q.shape
    hd = k.shape[1] // n_kv
    n_heads = dm // hd
    group = n_heads // n_kv
    nb = seq // QB
    topk = min(TOPK_MAX, seq // 4)
    assert seq % CK == 0 and LANES % idx_dim == 0 and kw.shape[1] == LANES
    kern = functools.partial(_attn_prompt_kernel, seq=seq, n_heads=n_heads, n_kv=n_kv, idx_heads=idx_heads,
                             idx_dim=idx_dim, topk=topk, hd=hd)
    row = lambda b, j: (b * nb + j, 0)
    return pl.pallas_call(
        kern,
        grid=(n_seq, nb),
        in_specs=[pl.BlockSpec((QB, idx_heads * idx_dim), row),
                  pl.BlockSpec((QB, LANES), row),
                  pl.BlockSpec((seq, LANES), lambda b, j: (b, 0)),
                  pl.BlockSpec((QB, dm), row),
                  pl.BlockSpec((seq, n_kv * hd), lambda b, j: (b, 0)),
                  pl.BlockSpec((seq, n_kv * hd), lambda b, j: (b, 0)),
                  pl.BlockSpec((QB, dm), row),
                  pl.BlockSpec((QB, dm), lambda b, j: (b * nb + j, 0)),
                  pl.BlockSpec((QB, dm), lambda b, j: (b * nb + j, 1)),
                  pl.BlockSpec((3, n_heads, QB, QB), lambda b, j: (0, 0, 0, 0))],
        out_specs=pl.BlockSpec((QB, dm), row),
        out_shape=jax.ShapeDtypeStruct((m, dm), BF16),
        scratch_shapes=[pltpu.VMEM((2, seq, LANES), BF16),
                        pltpu.VMEM((n_kv, seq, hd), BF16),
                        pltpu.VMEM((n_kv, seq, hd), BF16),
                        pltpu.VMEM((idx_heads, QB, CK), F32),
                        pltpu.VMEM((QB, idx_heads * idx_dim), BF16),
                        pltpu.VMEM((seq // CK, QB, CK), F32),
                        pltpu.VMEM((seq // CK, QB, CK), F32),
                        pltpu.VMEM((group, QB, hd), BF16),
                        pltpu.VMEM((group, QB, 1), F32),
                        pltpu.VMEM((group, QB, 1), F32),
                        pltpu.VMEM((group, QB, hd), F32)],
        compiler_params=_cparams(2),
    )(qi, kw, kw, q, k, v, conv_out, gates, gates, bias3)


def _sample_score_kernel(pt_ref, qi_ref, wi_ref, kn_ref, *rest, pg, idx_heads, idx_dim):
    pages = rest[:pg]
    o_ref, self_ref = rest[pg], rest[pg + 1]
    qi = qi_ref[...] * (idx_dim ** -0.5)
    wi = wi_ref[...] * (idx_heads ** -0.5)
    qb = qi.astype(BF16)
    for i in range(pg):
        s = _dot_nt(qb, pages[i][...].astype(BF16))
        o_ref[pl.ds(i, 1), :] = jnp.sum(wi * jnp.maximum(s, 0.0), axis=0, keepdims=True)

    @pl.when(pl.program_id(1) == 0)
    def _():
        kn = kn_ref[...].astype(BF16).astype(F32)
        s = jnp.sum(qb.astype(F32) * kn, axis=1, keepdims=True)
        sself = jnp.sum(wi * jnp.maximum(s, 0.0), axis=0, keepdims=True)
        self_ref[...] = jnp.broadcast_to(sself, self_ref.shape)


def sample_scores(page_table, qi3, wi3, ki_new3, cache_kidx, pg):
    n, n_pages = page_table.shape
    idx_heads, idx_dim = qi3.shape[1:]
    page = cache_kidx.shape[1]
    kern = functools.partial(_sample_score_kernel, pg=pg, idx_heads=idx_heads, idx_dim=idx_dim)
    page_specs = [pl.BlockSpec((None, page, idx_dim), (lambda b, p, pt, i=i: (pt[b, p * pg + i], 0, 0)))
                  for i in range(pg)]
    grid_spec = pltpu.PrefetchScalarGridSpec(
        num_scalar_prefetch=1,
        grid=(n, n_pages // pg),
        in_specs=[pl.BlockSpec((None, idx_heads, idx_dim), lambda b, p, pt: (b, 0, 0)),
                  pl.BlockSpec((None, idx_heads, 1), lambda b, p, pt: (b, 0, 0)),
                  pl.BlockSpec((None, 1, idx_dim), lambda b, p, pt: (b, 0, 0))] + page_specs,
        out_specs=[pl.BlockSpec((None, pg, page), lambda b, p, pt: (b, p, 0)),
                   pl.BlockSpec((None, 1, LANES), lambda b, p, pt: (b, 0, 0))],
    )
    return pl.pallas_call(
        kern,
        grid_spec=grid_spec,
        out_shape=[jax.ShapeDtypeStruct((n, n_pages, page), F32),
                   jax.ShapeDtypeStruct((n, 1, LANES), F32)],
        compiler_params=_cparams(2),
    )(page_table, qi3, wi3, ki_new3, *([cache_kidx] * pg))


def _sample_select_kernel(sc_ref, self_ref, sel_ref, selself_ref, *, topk, past):
    sc = sc_ref[...]
    sself = self_ref[:, 0:1]
    n = sc.shape[0]
    kf = jnp.float32(topk)
    row_min = jnp.minimum(jnp.min(sc, axis=1, keepdims=True), sself)
    row_max = jnp.maximum(jnp.max(sc, axis=1, keepdims=True), sself)
    n_adm = jnp.full((n, 1), past + 1, F32)

    def count_gt(t):
        return (jnp.sum(jnp.where(sc > t, 1.0, 0.0), axis=1, keepdims=True)
                + jnp.where(sself > t, 1.0, 0.0))

    def any_fn(mask):
        return jnp.max(jnp.where(mask, 1.0, 0.0)) > 0.0

    lo, hi, flo, fhi = _select_threshold(count_gt, row_min, row_max, n_adm, topk, any_fn)
    tie = flo != kf
    sel_ref[...] = jnp.where(sc > lo, 1.0, 0.0)
    selself_ref[...] = jnp.broadcast_to(jnp.where(sself > lo, 1.0, 0.0), selself_ref.shape)

    @pl.when(any_fn(tie))
    def _():
        need = kf - fhi
        blk = 512
        tri = (lax.broadcasted_iota(jnp.int32, (blk, blk), 0)
               < lax.broadcasted_iota(jnp.int32, (blk, blk), 1)).astype(BF16)
        before = jnp.zeros((n, 1), F32)
        for c in range(past // blk):
            s = sc_ref[:, c * blk:(c + 1) * blk]
            eq = s == hi
            rank = before + jnp.dot(jnp.where(eq, 1.0, 0.0).astype(BF16), tri, preferred_element_type=F32)
            keep = (s > hi) | (eq & (rank < need))
            sel_ref[:, c * blk:(c + 1) * blk] = jnp.where(tie, jnp.where(keep, 1.0, 0.0),
                                                          sel_ref[:, c * blk:(c + 1) * blk])
            before = before + jnp.sum(jnp.where(eq, 1.0, 0.0), axis=1, keepdims=True)
        keep_self = (sself > hi) | ((sself == hi) & (before < need))
        selself_ref[...] = jnp.broadcast_to(
            jnp.where(tie, jnp.where(keep_self, 1.0, 0.0), jnp.where(sself > lo, 1.0, 0.0)), selself_ref.shape)


def sample_select(scores, sself, topk):
    n, past = scores.shape
    kern = functools.partial(_sample_select_kernel, topk=topk, past=past)
    return pl.pallas_call(
        kern,
        out_shape=[jax.ShapeDtypeStruct((n, past), F32), jax.ShapeDtypeStruct((n, LANES), F32)],
        compiler_params=pltpu.CompilerParams(vmem_limit_bytes=VMEM_LIMIT),
    )(scores, sself)


def _sample_attn_kernel(pt_ref, q_ref, kn_ref, vn_ref, sel_ref, selself_ref, bias_ref, rb0_ref, *rest,
                        pg, n_heads, n_kv, hd):
    kpages = rest[:pg]
    vpages = rest[pg:2 * pg]
    o_ref = rest[2 * pg]
    qbd_ref, m_ref, l_ref, acc_ref = rest[2 * pg + 1:]
    p = pl.program_id(1)
    n_steps = pl.num_programs(1)
    group = n_heads // n_kv
    scale = hd ** -0.5
    hrow = lax.broadcasted_iota(jnp.int32, (n_heads, n_kv * hd), 0) // group
    gcol = lax.broadcasted_iota(jnp.int32, (n_heads, n_kv * hd), 1) // hd
    own = hrow == gcol

    @pl.when(p == 0)
    def _():
        q = q_ref[...]
        qbd_ref[...] = jnp.where(own, jnp.concatenate([q] * n_kv, axis=1), 0.0).astype(BF16)
        m_ref[...] = jnp.full(m_ref.shape, NEG, F32)
        l_ref[...] = jnp.zeros(l_ref.shape, F32)
        acc_ref[...] = jnp.zeros(acc_ref.shape, F32)

    qbd = qbd_ref[...]
    last = p == n_steps - 1
    lgs, msks = [], []
    for i in range(pg):
        lg = _dot_nt(qbd, kpages[i][...].astype(BF16)) * scale
        if i == pg - 1:
            bias = jnp.where(last, bias_ref[1], bias_ref[0])
        else:
            bias = bias_ref[0]
        msk = jnp.broadcast_to(sel_ref[pl.ds(i, 1), :] > 0.0, lg.shape)
        lgs.append(jnp.where(msk, lg + bias, NEG))
        msks.append(msk)
    m_old = m_ref[...]
    m_new = m_old
    for lg in lgs:
        m_new = jnp.maximum(m_new, jnp.max(lg, axis=-1, keepdims=True))
    alpha = jnp.exp(m_old - m_new)
    l_new = alpha * l_ref[...]
    acc = alpha * acc_ref[...]
    for i in range(pg):
        pr = jnp.where(msks[i], jnp.exp(lgs[i] - m_new), 0.0)
        l_new = l_new + jnp.sum(pr, axis=-1, keepdims=True)
        acc = acc + jnp.dot(pr.astype(BF16), vpages[i][...].astype(BF16), preferred_element_type=F32)
    m_ref[...] = m_new
    l_ref[...] = l_new
    acc_ref[...] = acc

    @pl.when(last)
    def _():
        kn = kn_ref[...].astype(BF16).astype(F32)
        vn = vn_ref[...].astype(BF16).astype(F32)
        ls = jnp.sum(qbd.astype(F32) * kn, axis=-1, keepdims=True) * scale + rb0_ref[...]
        on = selself_ref[:, 0:1] > 0.0
        ls = jnp.where(on, ls, NEG)
        m_f = jnp.maximum(m_new, ls)
        a2 = jnp.exp(m_new - m_f)
        ps = jnp.where(on, jnp.exp(ls - m_f), 0.0)
        l_f = a2 * l_new + ps
        acc_f = a2 * acc + ps.astype(BF16).astype(F32) * vn
        o_full = jnp.where(own, acc_f / l_f, 0.0)
        o = o_full[:, 0:hd]
        for g in range(1, n_kv):
            o = o + o_full[:, g * hd:(g + 1) * hd]
        o_ref[...] = o


def sample_attention(page_table, q3, k_new3, v_new3, sel3, selself3, bias_s, rb0, cache_k, cache_v, pg,
                     n_kv):
    n, n_pages = page_table.shape
    n_heads, hd = q3.shape[1:]
    page = cache_k.shape[1]
    dkv = n_kv * hd
    kern = functools.partial(_sample_attn_kernel, pg=pg, n_heads=n_heads, n_kv=n_kv, hd=hd)
    kspecs = [pl.BlockSpec((None, page, dkv), (lambda b, p, pt, i=i: (pt[b, p * pg + i], 0, 0)))
              for i in range(pg)]
    grid_spec = pltpu.PrefetchScalarGridSpec(
        num_scalar_prefetch=1,
        grid=(n, n_pages // pg),
        in_specs=[pl.BlockSpec((None, n_heads, hd), lambda b, p, pt: (b, 0, 0)),
                  pl.BlockSpec((None, 1, dkv), lambda b, p, pt: (b, 0, 0)),
                  pl.BlockSpec((None, 1, dkv), lambda b, p, pt: (b, 0, 0)),
                  pl.BlockSpec((None, pg, page), lambda b, p, pt: (b, p, 0)),
                  pl.BlockSpec((None, 1, LANES), lambda b, p, pt: (b, 0, 0)),
                  pl.BlockSpec((2, n_heads, LANES), lambda b, p, pt: (0, 0, 0)),
                  pl.BlockSpec((n_heads, 1), lambda b, p, pt: (0, 0))] + kspecs + kspecs,
        out_specs=pl.BlockSpec((None, n_heads, hd), lambda b, p, pt: (b, 0, 0)),
        scratch_shapes=[pltpu.VMEM((n_heads, dkv), BF16), pltpu.VMEM((n_heads, 1), F32),
                        pltpu.VMEM((n_heads, 1), F32), pltpu.VMEM((n_heads, dkv), F32)],
    )
    return pl.pallas_call(
        kern,
        grid_spec=grid_spec,
        out_shape=jax.ShapeDtypeStruct((n, n_heads, hd), F32),
        compiler_params=_cparams(2),
    )(page_table, q3, k_new3, v_new3, sel3, selself3, bias_s, rb0, *([cache_k] * pg), *([cache_v] * pg))


def _mix_kernel(ga_ref, gb_ref, co_ref, at_ref, o_ref):
    o_ref[...] = (_sigmoid(ga_ref[...]) * co_ref[...] + _sigmoid(gb_ref[...]) * at_ref[...]).astype(o_ref.dtype)


def gated_mix(gates, conv_out, attn):
    n, dm = conv_out.shape
    return pl.pallas_call(
        _mix_kernel,
        grid=(1,),
        in_specs=[pl.BlockSpec((n, dm), lambda i: (0, 0)), pl.BlockSpec((n, dm), lambda i: (0, 1)),
                  pl.BlockSpec((n, dm), lambda i: (0, 0)), pl.BlockSpec((n, dm), lambda i: (0, 0))],
        out_specs=pl.BlockSpec((n, dm), lambda i: (0, 0)),
        out_shape=jax.ShapeDtypeStruct((n, dm), BF16),
    )(gates, gates, conv_out, attn)


def _ffn_act_kernel(ug_ref, uv_ref, hg_ref, hv_ref, pg_ref, pv_ref, wg_ref, wv_ref, bg_ref, bv_ref, o_ref,
                    eg_ref, ev_ref, *, tt, width):
    t = pl.program_id(1)
    hist = width - 1
    pad = 8
    first = t == 0

    def conv(u_ref, halo_ref, prev_ref, w_ref, b_ref, e_ref):
        e_ref[pl.ds(pad - hist, hist), :] = jnp.where(first, prev_ref[...], halo_ref[pl.ds(pad - hist, hist), :])
        e_ref[pl.ds(pad, tt), :] = u_ref[...]
        acc = b_ref[...] + w_ref[pl.ds(hist, 1), :] * u_ref[...]
        for j in range(hist):
            acc = acc + w_ref[pl.ds(j, 1), :] * e_ref[pl.ds(pad - hist + j, tt), :]
        return acc

    g = conv(ug_ref, hg_ref, pg_ref, wg_ref, bg_ref, eg_ref)
    v = conv(uv_ref, hv_ref, pv_ref, wv_ref, bv_ref, ev_ref)
    o_ref[...] = (g * _sigmoid(g) * v).astype(o_ref.dtype)


def ffn_act_prompt(u, prev, dw, bdw, n_seq, seq, tt, tc):
    m, f2 = u.shape
    f = f2 // 2
    width = dw.shape[0]
    nt = seq // tt
    ncb = f // tc
    hb = tt // 8
    kern = functools.partial(_ffn_act_kernel, tt=tt, width=width)
    halo = lambda off: (lambda b, t, c: (jnp.maximum((b * nt + t) * hb - 1, 0), c + off))
    return pl.pallas_call(
        kern,
        grid=(n_seq, nt, ncb),
        in_specs=[pl.BlockSpec((tt, tc), lambda b, t, c: (b * nt + t, c)),
                  pl.BlockSpec((tt, tc), lambda b, t, c: (b * nt + t, c + ncb)),
                  pl.BlockSpec((8, tc), halo(0)),
                  pl.BlockSpec((8, tc), halo(ncb)),
                  pl.BlockSpec((None, width - 1, tc), lambda b, t, c: (b, 0, c)),
                  pl.BlockSpec((None, width - 1, tc), lambda b, t, c: (b, 0, c + ncb)),
                  pl.BlockSpec((width, tc), lambda b, t, c: (0, c)),
                  pl.BlockSpec((width, tc), lambda b, t, c: (0, c + ncb)),
                  pl.BlockSpec((1, tc), lambda b, t, c: (0, c)),
                  pl.BlockSpec((1, tc), lambda b, t, c: (0, c + ncb))],
        out_specs=pl.BlockSpec((tt, tc), lambda b, t, c: (b * nt + t, c)),
        out_shape=jax.ShapeDtypeStruct((m, f), BF16),
        scratch_shapes=[pltpu.VMEM((8 + tt, tc), F32), pltpu.VMEM((8 + tt, tc), F32)],
        compiler_params=_cparams(3),
    )(u, u, u, u, prev, prev, dw, dw, bdw.reshape(1, f2), bdw.reshape(1, f2))


def _ffn_act_step_kernel(ug_ref, uv_ref, pg_ref, pv_ref, wg_ref, wv_ref, bg_ref, bv_ref, o_ref, *, width):
    def conv(u_ref, prev_ref, w_ref, b_ref):
        acc = b_ref[...] + w_ref[pl.ds(width - 1, 1), :] * u_ref[...]
        for j in range(width - 1):
            acc = acc + w_ref[pl.ds(j, 1), :] * prev_ref[j]
        return acc
    g = conv(ug_ref, pg_ref, wg_ref, bg_ref)
    v = conv(uv_ref, pv_ref, wv_ref, bv_ref)
    o_ref[...] = (g * _sigmoid(g) * v).astype(o_ref.dtype)


def ffn_act_step(u, prev_t, dw, bdw, tc):
    n, f2 = u.shape
    f = f2 // 2
    width = dw.shape[0]
    ncb = f // tc
    kern = functools.partial(_ffn_act_step_kernel, width=width)
    return pl.pallas_call(
        kern,
        grid=(ncb,),
        in_specs=[pl.BlockSpec((n, tc), lambda c: (0, c)),
                  pl.BlockSpec((n, tc), lambda c: (0, c + ncb)),
                  pl.BlockSpec((width - 1, n, tc), lambda c: (0, 0, c)),
                  pl.BlockSpec((width - 1, n, tc), lambda c: (0, 0, c + ncb)),
                  pl.BlockSpec((width, tc), lambda c: (0, c)),
                  pl.BlockSpec((width, tc), lambda c: (0, c + ncb)),
                  pl.BlockSpec((1, tc), lambda c: (0, c)),
                  pl.BlockSpec((1, tc), lambda c: (0, c + ncb))],
        out_specs=pl.BlockSpec((n, tc), lambda c: (0, c)),
        out_shape=jax.ShapeDtypeStruct((n, f), BF16),
        compiler_params=_cparams(1),
    )(u, u, prev_t, prev_t, dw, dw, bdw.reshape(1, f2), bdw.reshape(1, f2))


def _tiles(m):
    tm = 1024 if m % 1024 == 0 else m
    return tm


def _in_projection(xn, w_in, sizes, tm):
    glu_w, q_w, k_w, v_w, qi_w, ki_w, wi_w, ga_w, gb_w = sizes
    offs = np.concatenate([[0], np.cumsum(sizes)])
    tn = 512
    glu = matmul_w(xn, w_in, int(offs[0]), glu_w, tm, tn)
    q = matmul_w(xn, w_in, int(offs[1]), q_w, tm, tn)
    k = matmul_w(xn, w_in, int(offs[2]), k_w, tm, min(tn, k_w))
    v = matmul_w(xn, w_in, int(offs[3]), v_w, tm, min(tn, v_w))
    qi = matmul_w(xn, w_in, int(offs[4]), qi_w, tm, min(tn, qi_w))
    w_kw = jnp.pad(w_in[:, int(offs[5]):int(offs[7])], ((0, 0), (0, LANES - ki_w - wi_w)))
    kw = matmul_w(xn, w_kw, 0, LANES, tm, LANES)
    w_g = w_in[:, int(offs[7]):]
    gates = matmul_w(xn, w_g, 0, ga_w + gb_w, tm, tn)
    return glu, q, k, v, qi, kw, gates


def kernel(x_prompt, x_sample, cache_k, cache_v, cache_kidx, state_conv, state_ffn, page_table, rel_bias,
           norm_attn, w_in, dw_conv, b_dw_conv, ln_conv_g, ln_conv_b, w_conv_out, w_o, norm_ffn, w_up, dw_ffn,
           b_dw_ffn, w_down, norm_final):
    bsz, seq, dm = x_prompt.shape
    nd, dec_seq, _ = x_sample.shape
    depth, n_pool, page, n_kv, hd = cache_k.shape
    idx_dim = cache_kidx.shape[-1]
    n_pages = page_table.shape[1]
    past = n_pages * page
    width, dconv = dw_conv.shape[1:]
    fwidth = dw_ffn.shape[1]
    f = w_down.shape[1]
    n_heads = w_o.shape[1] // hd
    d_attn = n_heads * hd
    d_kv = n_kv * hd
    n_in = w_in.shape[2]
    idx_heads = (n_in - 2 * dconv - d_attn - 2 * d_kv - idx_dim - 2 * dm) // (idx_dim + 1)
    sizes = (2 * dconv, d_attn, d_kv, d_kv, idx_heads * idx_dim, idx_dim, idx_heads, dm, dm)
    assert sum(sizes) == n_in and depth == 1 and dec_seq == 1 and page == LANES and d_attn == dm

    mp = bsz * seq
    xp = x_prompt.reshape(mp, dm)
    xs = x_sample.reshape(nd, dm)
    bias3, bias_s = bias_tables(rel_bias, past)
    l = 0
    tmp = _tiles(mp)

    xn = rmsnorm_rows(xp, norm_attn[l], BF16, 512)
    glu, q, k, v, qi, kw, gates = _in_projection(xn, w_in[l], sizes, tmp)
    conv0 = jnp.zeros((bsz, width - 1, dconv), F32)
    conv_out, conv_state_p = conv_branch_prompt(glu, conv0, dw_conv[l], b_dw_conv[l], ln_conv_g[l], ln_conv_b[l],
                                                w_conv_out[l], bsz, seq, 256)
    mixed = attn_prompt(qi, kw, q, k, v, conv_out, gates, bias3, bsz, seq, n_kv, idx_dim, idx_heads)
    x2 = matmul_w(mixed, w_o[l], 0, dm, tmp, 512, res=xp)
    xn2 = rmsnorm_rows(x2, norm_ffn[l], BF16, 512)
    u = matmul_w(xn2, w_up[l], 0, 2 * f, tmp, 512)
    ffn0 = jnp.zeros((bsz, fwidth - 1, 2 * f), F32)
    act = ffn_act_prompt(u, ffn0, dw_ffn[l], b_dw_ffn[l], bsz, seq, 512, 512)
    x3 = matmul_w(act, w_down[l], 0, dm, tmp, 256, res=x2)
    y_prompt = rmsnorm_rows(x3, norm_final, F32, 512).reshape(bsz, seq, dm)
    ffn_state_p = u.reshape(bsz, seq, 2 * f)[:, seq - (fwidth - 1):, :]

    xns = rmsnorm_rows(xs, norm_attn[l], BF16, nd)
    glu_s, q_s, k_s, v_s, qi_s, kw_s, gates_s = _in_projection(xns, w_in[l], sizes, nd)
    ki_s = kw_s[:, :idx_dim]
    wi_s = kw_s[:, idx_dim:idx_dim + idx_heads]
    sc_prev_t = jnp.swapaxes(state_conv[l], 0, 1)
    conv_out_s, u_conv_s = conv_branch_step(glu_s, sc_prev_t, dw_conv[l], b_dw_conv[l], ln_conv_g[l],
                                            ln_conv_b[l], w_conv_out[l])
    conv_state_s = jnp.concatenate([state_conv[l][:, 1:], u_conv_s[:, None, :]], axis=1)

    pg = 8
    kidx_pool = cache_kidx[l]
    scores3, sself3 = sample_scores(page_table, qi_s.reshape(nd, idx_heads, idx_dim),
                                    wi_s.reshape(nd, idx_heads, 1), ki_s.reshape(nd, 1, idx_dim), kidx_pool, pg)
    topk_s = min(TOPK_MAX, (past + dec_seq) // 4)
    sel, selself = sample_select(scores3.reshape(nd, past), sself3.reshape(nd, LANES), topk_s)
    attn_s = sample_attention(page_table, q_s.reshape(nd, n_heads, hd), k_s.reshape(nd, 1, d_kv),
                              v_s.reshape(nd, 1, d_kv), sel.reshape(nd, n_pages, page),
                              selself.reshape(nd, 1, LANES), bias_s, rel_bias[0].reshape(n_heads, 1),
                              cache_k[l].reshape(n_pool, page, d_kv), cache_v[l].reshape(n_pool, page, d_kv),
                              pg, n_kv)
    mixed_s = gated_mix(gates_s, conv_out_s, attn_s.reshape(nd, dm))
    x2s = matmul_w(mixed_s, w_o[l], 0, dm, nd, 512, res=xs)
    xn2s = rmsnorm_rows(x2s, norm_ffn[l], BF16, nd)
    u_s = matmul_w(xn2s, w_up[l], 0, 2 * f, nd, 512)
    sf_prev_t = jnp.swapaxes(state_ffn[l], 0, 1)
    act_s = ffn_act_step(u_s, sf_prev_t, dw_ffn[l], b_dw_ffn[l], 512)
    x3s = matmul_w(act_s, w_down[l], 0, dm, nd, 256, res=x2s)
    y_sample = rmsnorm_rows(x3s, norm_final, F32, nd).reshape(nd, dec_seq, dm)
    ffn_state_s = jnp.concatenate([state_ffn[l][:, 1:], u_s[:, None, :]], axis=1)

    return (y_prompt, y_sample,
            k.reshape(1, bsz, seq, n_kv, hd), v.reshape(1, bsz, seq, n_kv, hd),
            kw.reshape(bsz, seq, LANES)[None, :, :, :idx_dim],
            conv_state_p[None], ffn_state_p[None],
            k_s.reshape(1, nd, dec_seq, n_kv, hd), v_s.reshape(1, nd, dec_seq, n_kv, hd),
            ki_s.reshape(1, nd, dec_seq, idx_dim),
            conv_state_s[None], ffn_state_s[None])
```

```python
import functools
import math

import numpy as np
import jax
import jax.numpy as jnp
from jax import lax
from jax.experimental import pallas as pl
from jax.experimental.pallas import tpu as pltpu

F32 = jnp.float32
BF16 = jnp.bfloat16

EPS = 1e-6
TOPK_MAX = 256
N_BUCKETS = 32
MAX_DISTANCE = 128
QB = 128
CK = 256
LANES = 128
NEG = -1e30
VMEM_LIMIT = 56 * 1024 * 1024


def _cparams(n_axes, vmem=VMEM_LIMIT):
    return pltpu.CompilerParams(dimension_semantics=("arbitrary",) * n_axes, vmem_limit_bytes=vmem)


def _dot_nt(a, b):
    return lax.dot_general(a, b, (((1,), (1,)), ((), ())), preferred_element_type=F32)


def _sigmoid(x):
    return 1.0 / (1.0 + jnp.exp(-x))


def _rel_bucket_np(dist):
    n = np.maximum(dist, 0)
    max_exact = N_BUCKETS // 2
    nf = np.maximum(n, 1).astype(np.float32)
    large = max_exact + (np.log(nf / np.float32(max_exact)) / np.float32(math.log(MAX_DISTANCE / max_exact))
                         * np.float32(N_BUCKETS - max_exact)).astype(np.int32)
    large = np.minimum(large, N_BUCKETS - 1)
    return np.where(n < max_exact, n, large).astype(np.int32)


def _rms_kernel(x_ref, g_ref, o_ref):
    x = x_ref[...]
    y = x * lax.rsqrt(jnp.mean(x * x, axis=-1, keepdims=True) + EPS) * g_ref[...]
    o_ref[...] = y.astype(o_ref.dtype)


def rmsnorm_rows(x, g, out_dtype, tm):
    m, d = x.shape
    return pl.pallas_call(
        _rms_kernel,
        grid=(m // tm,),
        in_specs=[pl.BlockSpec((tm, d), lambda i: (i, 0)), pl.BlockSpec((1, d), lambda i: (0, 0))],
        out_specs=pl.BlockSpec((tm, d), lambda i: (i, 0)),
        out_shape=jax.ShapeDtypeStruct((m, d), out_dtype),
        compiler_params=_cparams(1),
        name="rmsnorm",
    )(x, g.reshape(1, d))


def _mm_kernel(a_ref, w_ref, o_ref):
    o_ref[...] = jnp.dot(a_ref[...], w_ref[...].astype(BF16), preferred_element_type=F32)


def _mm_res_kernel(a_ref, w_ref, r_ref, o_ref):
    o_ref[...] = r_ref[...] + jnp.dot(a_ref[...], w_ref[...].astype(BF16), preferred_element_type=F32)


def matmul_w(a, w, col0, ncols, tm, tn, res=None, name="matmul"):
    m, k = a.shape
    assert col0 % tn == 0 and ncols % tn == 0 and m % tm == 0
    cb = col0 // tn
    in_specs = [pl.BlockSpec((tm, k), lambda i, j: (i, 0)),
                pl.BlockSpec((k, tn), lambda i, j: (0, j + cb))]
    args = [a, w]
    kern = _mm_kernel
    if res is not None:
        in_specs.append(pl.BlockSpec((tm, tn), lambda i, j: (i, j)))
        args.append(res)
        kern = _mm_res_kernel
    return pl.pallas_call(
        kern,
        grid=(m // tm, ncols // tn),
        in_specs=in_specs,
        out_specs=pl.BlockSpec((tm, tn), lambda i, j: (i, j)),
        out_shape=jax.ShapeDtypeStruct((m, ncols), F32),
        compiler_params=_cparams(2),
        name=name,
    )(*args)


def _bias_kernel(rb_ref, bk3_ref, bks_ref, o3_ref, os_ref, *, n_heads):
    def head(h, carry):
        for t in range(3):
            bk = bk3_ref[t]
            acc = jnp.zeros(bk.shape, F32)
            for b in range(N_BUCKETS):
                acc = jnp.where(bk == b, rb_ref[b, h], acc)
            o3_ref[t, h] = acc
        for t in range(2):
            bk = bks_ref[t]
            acc = jnp.zeros(bk.shape, F32)
            for b in range(N_BUCKETS):
                acc = jnp.where(bk == b, rb_ref[b, h], acc)
            os_ref[t, pl.ds(h, 1), :] = acc
        return carry
    lax.fori_loop(0, n_heads, head, 0)


def bias_tables(rel_bias, page, rep):
    n_heads = rel_bias.shape[1]
    cols = page * rep
    i = np.arange(QB)[:, None]
    k = np.arange(QB)[None, :]
    bk3 = np.stack([_rel_bucket_np(i - k + 2 * QB), _rel_bucket_np(i - k + QB), _rel_bucket_np(i - k)])
    assert (_rel_bucket_np(np.arange(QB + 1, 1 << 20)) == N_BUCKETS - 1).all()
    assert (bk3[0] == N_BUCKETS - 1).all()
    assert page >= QB
    bks = np.stack([np.full((1, cols), N_BUCKETS - 1, np.int32),
                    _rel_bucket_np(page - np.arange(cols) // rep)[None, :]])
    return pl.pallas_call(
        functools.partial(_bias_kernel, n_heads=n_heads),
        in_specs=[pl.BlockSpec(memory_space=pltpu.SMEM),
                  pl.BlockSpec(memory_space=pltpu.VMEM), pl.BlockSpec(memory_space=pltpu.VMEM)],
        out_specs=[pl.BlockSpec(memory_space=pltpu.VMEM), pl.BlockSpec(memory_space=pltpu.VMEM)],
        out_shape=[jax.ShapeDtypeStruct((3, n_heads, QB, QB), F32),
                   jax.ShapeDtypeStruct((2, n_heads, cols), F32)],
    )(rel_bias, jnp.asarray(bk3), jnp.asarray(bks))


def _conv_kernel(glu_ref, prev_ref, dw_ref, bdw_ref, lng_ref, lnb_ref, wout_ref, o_ref, st_ref,
                 ext_ref, h_ref, wbf_ref, *, tt, width, dconv):
    b = pl.program_id(0)
    t = pl.program_id(1)
    pad = 32
    hist = width - 1

    @pl.when((b == 0) & (t == 0))
    def _():
        wbf_ref[...] = wout_ref[...].astype(BF16)

    @pl.when(t == 0)
    def _():
        ext_ref[pl.ds(pad - hist, hist), :] = prev_ref[...]

    @pl.when(t > 0)
    def _():
        ext_ref[pl.ds(0, pad), :] = ext_ref[pl.ds(tt, pad), :]

    glu = glu_ref[...]
    u = glu[:, :dconv] * _sigmoid(glu[:, dconv:])
    ext_ref[pl.ds(pad, tt), :] = u
    st_ref[...] = ext_ref[pl.ds(pad + tt - hist, hist), :]

    for c in range(dconv // LANES):
        cs = slice(c * LANES, (c + 1) * LANES)
        acc = jnp.zeros((tt, LANES), F32) + bdw_ref[:, cs]
        for j in range(width):
            acc = acc + dw_ref[pl.ds(j, 1), cs] * ext_ref[pl.ds(pad - hist + j, tt), cs]
        h_ref[:, cs] = acc

    h = h_ref[...]
    mu = jnp.mean(h, axis=-1, keepdims=True)
    var = jnp.mean(jnp.square(h - mu), axis=-1, keepdims=True)
    y = (h - mu) * lax.rsqrt(var + EPS) * lng_ref[...] + lnb_ref[...]
    y = y * _sigmoid(y)
    o_ref[...] = jnp.dot(y.astype(BF16), wbf_ref[...], preferred_element_type=F32)


def conv_branch_prompt(glu_pre, prev, dw, bdw, lng, lnb, wout, n_seq, seq, tt):
    width, dconv = dw.shape
    dm = wout.shape[1]
    nt = seq // tt
    kern = functools.partial(_conv_kernel, tt=tt, width=width, dconv=dconv)
    return pl.pallas_call(
        kern,
        grid=(n_seq, nt),
        in_specs=[pl.BlockSpec((tt, 2 * dconv), lambda b, t: (b * nt + t, 0)),
                  pl.BlockSpec((None, width - 1, dconv), lambda b, t: (b, 0, 0)),
                  pl.BlockSpec((width, dconv), lambda b, t: (0, 0)),
                  pl.BlockSpec((1, dconv), lambda b, t: (0, 0)),
                  pl.BlockSpec((1, dconv), lambda b, t: (0, 0)),
                  pl.BlockSpec((1, dconv), lambda b, t: (0, 0)),
                  pl.BlockSpec((dconv, dm), lambda b, t: (0, 0))],
        out_specs=[pl.BlockSpec((tt, dm), lambda b, t: (b * nt + t, 0)),
                   pl.BlockSpec((None, width - 1, dconv), lambda b, t: (b, 0, 0))],
        out_shape=[jax.ShapeDtypeStruct((n_seq * seq, dm), F32),
                   jax.ShapeDtypeStruct((n_seq, width - 1, dconv), F32)],
        scratch_shapes=[pltpu.VMEM((32 + tt, dconv), F32), pltpu.VMEM((tt, dconv), F32),
                        pltpu.VMEM((dconv, dm), BF16)],
        compiler_params=_cparams(2),
        name="conv_branch",
    )(glu_pre, prev, dw, bdw.reshape(1, dconv), lng.reshape(1, dconv), lnb.reshape(1, dconv), wout)


def _conv_step_kernel(glu_ref, prev_ref, dw_ref, bdw_ref, lng_ref, lnb_ref, wout_ref, o_ref, u_ref,
                      *, width, dconv):
    glu = glu_ref[...]
    u = glu[:, :dconv] * _sigmoid(glu[:, dconv:])
    u_ref[...] = u
    h = bdw_ref[...] + dw_ref[pl.ds(width - 1, 1), :] * u
    for j in range(width - 1):
        h = h + dw_ref[pl.ds(j, 1), :] * prev_ref[j]
    mu = jnp.mean(h, axis=-1, keepdims=True)
    var = jnp.mean(jnp.square(h - mu), axis=-1, keepdims=True)
    y = (h - mu) * lax.rsqrt(var + EPS) * lng_ref[...] + lnb_ref[...]
    y = y * _sigmoid(y)
    o_ref[...] = jnp.dot(y.astype(BF16), wout_ref[...].astype(BF16), preferred_element_type=F32)


def conv_branch_step(glu_pre, prev_t, dw, bdw, lng, lnb, wout):
    width, dconv = dw.shape
    n = glu_pre.shape[0]
    dm = wout.shape[1]
    kern = functools.partial(_conv_step_kernel, width=width, dconv=dconv)
    return pl.pallas_call(
        kern,
        out_shape=[jax.ShapeDtypeStruct((n, dm), F32), jax.ShapeDtypeStruct((n, dconv), F32)],
        compiler_params=pltpu.CompilerParams(vmem_limit_bytes=VMEM_LIMIT),
    )(glu_pre, prev_t, dw, bdw.reshape(1, dconv), lng.reshape(1, dconv), lnb.reshape(1, dconv), wout)


def _select_threshold(count_gt, row_min, row_max, n_adm, topk, any_fn):
    kf = jnp.float32(topk)
    full = n_adm <= kf
    lo0 = row_min - (1.0 + jnp.abs(row_min))
    hi0 = row_max
    flo0 = jnp.where(full, kf, n_adm)
    fhi0 = jnp.zeros_like(lo0)

    def active_rows(lo, hi, flo):
        mid = 0.5 * lo + 0.5 * hi
        return (flo != kf) & (lo < mid) & (mid < hi)

    def cond(st):
        lo, hi, flo, fhi = st
        return any_fn(active_rows(lo, hi, flo))

    def body(st):
        lo, hi, flo, fhi = st
        act = active_rows(lo, hi, flo)
        mid = 0.5 * lo + 0.5 * hi
        c = count_gt(mid)
        up = act & (c >= kf)
        dn = act & (c < kf)
        return (jnp.where(up, mid, lo), jnp.where(dn, mid, hi),
                jnp.where(up, c, flo), jnp.where(dn, c, fhi))

    lo, hi, flo, fhi = lax.while_loop(cond, body, (lo0, hi0, flo0, fhi0))
    lo = jnp.where(full, -jnp.inf, lo)
    return lo, hi, flo, fhi


def _attn_prompt_kernel(qi_ref, wi_ref, kw_ref, q_ref, k_ref, v_ref, co_ref, ga_ref, gb_ref, bias_ref,
                        o_ref, kd_ref, kb_ref, vb_ref, wib_ref, qib_ref, sc_ref, sel_ref, qs_ref,
                        m_ref, l_ref, acc_ref,
                        *, seq, n_heads, n_kv, idx_heads, idx_dim, topk, hd):
    j = pl.program_id(1)
    group = n_heads // n_kv
    nck = (j * QB + QB + CK - 1) // CK
    lane = lax.broadcasted_iota(jnp.int32, (1, LANES), 1)
    kf = jnp.float32(topk)

    @pl.when(j == 0)
    def _():
        kw = kw_ref[...]
        kd_ref[0] = jnp.where(lane < idx_dim, kw, 0.0).astype(BF16)
        kd_ref[1] = jnp.where(lane >= idx_dim, pltpu.roll(kw, idx_dim, 1), 0.0).astype(BF16)
        for g in range(n_kv):
            kb_ref[g] = k_ref[:, g * hd:(g + 1) * hd].astype(BF16)
            vb_ref[g] = v_ref[:, g * hd:(g + 1) * hd].astype(BF16)

    wi = wi_ref[...] * (idx_heads ** -0.5)
    for h in range(idx_heads):
        wib_ref[h] = jnp.broadcast_to(wi[:, idx_dim + h:idx_dim + h + 1], (QB, CK))
    qib_ref[...] = (qi_ref[...] * (idx_dim ** -0.5)).astype(BF16)
    qpos = j * QB + lax.broadcasted_iota(jnp.int32, (QB, CK), 0)
    kcol = lax.broadcasted_iota(jnp.int32, (QB, CK), 1)
    per_pair = LANES // idx_dim

    def score_chunk(c, carry):
        mn, mx = carry
        k0 = pl.multiple_of(c * CK, CK)
        acc = jnp.zeros((QB, CK), F32)
        for p in range(idx_heads // per_pair):
            lhs = qib_ref[:, p * LANES:(p + 1) * LANES]
            for r in range(per_pair):
                s = _dot_nt(lhs, kd_ref[r, pl.ds(k0, CK), :])
                acc = acc + wib_ref[p * per_pair + r] * jnp.maximum(s, 0.0)
        adm = (kcol + c * CK) <= qpos
        sc_ref[c] = jnp.where(adm, acc, -jnp.inf)
        mn = jnp.minimum(mn, jnp.where(adm, acc, jnp.inf))
        mx = jnp.maximum(mx, jnp.where(adm, acc, -jnp.inf))
        return mn, mx

    mn, mx = lax.fori_loop(0, nck, score_chunk,
                           (jnp.full((QB, CK), jnp.inf, F32), jnp.full((QB, CK), -jnp.inf, F32)))
    row_min = jnp.min(mn, axis=1, keepdims=True)
    row_max = jnp.max(mx, axis=1, keepdims=True)
    n_adm = (j * QB + 1 + lax.broadcasted_iota(jnp.int32, (QB, 1), 0)).astype(F32)

    def count_gt(t):
        def cbody(c, acc):
            return acc + jnp.where(sc_ref[c] > t, 1.0, 0.0)
        part = lax.fori_loop(0, nck, cbody, jnp.zeros((QB, CK), F32))
        return jnp.sum(part, axis=1, keepdims=True)

    def any_fn(mask):
        return jnp.max(jnp.where(mask, 1.0, 0.0)) > 0.0

    lo, hi, flo, fhi = _select_threshold(count_gt, row_min, row_max, n_adm, topk, any_fn)
    tie = flo != kf

    def sel_chunk(c, carry):
        sel_ref[c] = jnp.where(sc_ref[c] > lo, 1.0, 0.0)
        return carry
    lax.fori_loop(0, nck, sel_chunk, 0)

    @pl.when(any_fn(tie))
    def _():
        need = kf - fhi
        tri = (lax.broadcasted_iota(jnp.int32, (CK, CK), 0)
               < lax.broadcasted_iota(jnp.int32, (CK, CK), 1)).astype(BF16)

        def tie_chunk(c, before):
            s = sc_ref[c]
            eq = s == hi
            rank = before + jnp.dot(jnp.where(eq, 1.0, 0.0).astype(BF16), tri, preferred_element_type=F32)
            keep = (s > hi) | (eq & (rank < need))
            sel_ref[c] = jnp.where(tie, jnp.where(keep, 1.0, 0.0), sel_ref[c])
            return before + jnp.sum(jnp.where(eq, 1.0, 0.0), axis=1, keepdims=True)
        lax.fori_loop(0, nck, tie_chunk, jnp.zeros((QB, 1), F32))

    scale = hd ** -0.5
    q = q_ref[...]
    jb = j * (QB // LANES)
    for g in range(n_kv):
        for hh in range(group):
            h = g * group + hh
            qs_ref[hh] = q[:, h * hd:(h + 1) * hd].astype(BF16)
        m_ref[...] = jnp.full(m_ref.shape, NEG, F32)
        l_ref[...] = jnp.zeros(l_ref.shape, F32)
        acc_ref[...] = jnp.zeros(acc_ref.shape, F32)

        def att_chunk(c, carry):
            k0 = pl.multiple_of(c * CK, CK)
            kc = kb_ref[g, pl.ds(k0, CK), :]
            vc = vb_ref[g, pl.ds(k0, CK), :]
            lg = _dot_nt(qs_ref[...].reshape(group * QB, hd), kc).reshape(group, QB, CK) * scale
            tiles = []
            for s in range(CK // QB):
                ti = jnp.clip(c * (CK // QB) + s - jb + 2, 0, 2)
                tiles.append(bias_ref[ti, pl.ds(g * group, group)])
            lg = lg + jnp.concatenate(tiles, axis=-1)
            msk = sel_ref[c][None] > 0.0
            lg = jnp.where(msk, lg, NEG)
            m_old = m_ref[...]
            m_new = jnp.maximum(m_old, jnp.max(lg, axis=-1, keepdims=True))
            alpha = jnp.exp(m_old - m_new)
            p = jnp.where(msk, jnp.exp(lg - m_new), 0.0)
            l_ref[...] = alpha * l_ref[...] + jnp.sum(p, axis=-1, keepdims=True)
            pv = jnp.dot(p.reshape(group * QB, CK).astype(BF16), vc, preferred_element_type=F32)
            acc_ref[...] = alpha * acc_ref[...] + pv.reshape(group, QB, hd)
            m_ref[...] = m_new
            return carry
        lax.fori_loop(0, nck, att_chunk, 0)

        o = acc_ref[...] / l_ref[...]
        for hh in range(group):
            cs = slice((g * group + hh) * hd, (g * group + hh + 1) * hd)
            mixed = _sigmoid(ga_ref[:, cs]) * co_ref[:, cs] + _sigmoid(gb_ref[:, cs]) * o[hh]
            o_ref[:, cs] = mixed.astype(o_ref.dtype)


def attn_prompt(qi, kw, q, k, v, conv_out, gates, bias3, n_seq, seq, n_kv, idx_dim, idx_heads):
    m, dm = q.shape
    hd = k.shape[1] // n_kv
    n_heads = dm // hd
    group = n_heads // n_kv
    nb = seq // QB
    topk = min(TOPK_MAX, seq // 4)
    assert seq % CK == 0 and LANES % idx_dim == 0 and kw.shape[1] == LANES
    kern = functools.partial(_attn_prompt_kernel, seq=seq, n_heads=n_heads, n_kv=n_kv, idx_heads=idx_heads,
                             idx_dim=idx_dim, topk=topk, hd=hd)
    row = lambda b, j: (b * nb + j, 0)
    return pl.pallas_call(
        kern,
        grid=(n_seq, nb),
        in_specs=[pl.BlockSpec((QB, idx_heads * idx_dim), row),
                  pl.BlockSpec((QB, LANES), row),
                  pl.BlockSpec((seq, LANES), lambda b, j: (b, 0)),
                  pl.BlockSpec((QB, dm), row),
                  pl.BlockSpec((seq, n_kv * hd), lambda b, j: (b, 0)),
                  pl.BlockSpec((seq, n_kv * hd), lambda b, j: (b, 0)),
                  pl.BlockSpec((QB, dm), row),
                  pl.BlockSpec((QB, dm), lambda b, j: (b * nb + j, 0)),
                  pl.BlockSpec((QB, dm), lambda b, j: (b * nb + j, 1)),
                  pl.BlockSpec((3, n_heads, QB, QB), lambda b, j: (0, 0, 0, 0))],
        out_specs=pl.BlockSpec((QB, dm), row),
        out_shape=jax.ShapeDtypeStruct((m, dm), BF16),
        scratch_shapes=[pltpu.VMEM((2, seq, LANES), BF16),
                        pltpu.VMEM((n_kv, seq, hd), BF16),
                        pltpu.VMEM((n_kv, seq, hd), BF16),
                        pltpu.VMEM((idx_heads, QB, CK), F32),
                        pltpu.VMEM((QB, idx_heads * idx_dim), BF16),
                        pltpu.VMEM((seq // CK, QB, CK), F32),
                        pltpu.VMEM((seq // CK, QB, CK), F32),
                        pltpu.VMEM((group, QB, hd), BF16),
                        pltpu.VMEM((group, QB, 1), F32),
                        pltpu.VMEM((group, QB, 1), F32),
                        pltpu.VMEM((group, QB, hd), F32)],
        compiler_params=_cparams(2),
        name="attn_prompt",
    )(qi, kw, kw, q, k, v, conv_out, gates, gates, bias3)


def _sample_score_kernel(pt_ref, qi_ref, wi_ref, kn_ref, *rest, pg, idx_heads, idx_dim):
    pages = rest[:pg]
    o_ref, self_ref = rest[pg], rest[pg + 1]
    qi = qi_ref[...] * (idx_dim ** -0.5)
    wi = wi_ref[...] * (idx_heads ** -0.5)
    qb = qi.astype(BF16)
    for i in range(pg):
        s = _dot_nt(qb, pages[i][...].astype(BF16))
        o_ref[pl.ds(i, 1), :] = jnp.sum(wi * jnp.maximum(s, 0.0), axis=0, keepdims=True)

    @pl.when(pl.program_id(1) == 0)
    def _():
        kn = kn_ref[...].astype(BF16).astype(F32)
        s = jnp.sum(qb.astype(F32) * kn, axis=1, keepdims=True)
        sself = jnp.sum(wi * jnp.maximum(s, 0.0), axis=0, keepdims=True)
        self_ref[...] = jnp.broadcast_to(sself, self_ref.shape)


def sample_scores(page_table, qi3, wi3, ki_new3, cache_kidx, pg):
    n, n_pages = page_table.shape
    idx_heads, idx_dim = qi3.shape[1:]
    page = cache_kidx.shape[1]
    kern = functools.partial(_sample_score_kernel, pg=pg, idx_heads=idx_heads, idx_dim=idx_dim)
    page_specs = [pl.BlockSpec((None, page, idx_dim), (lambda b, p, pt, i=i: (pt[b, p * pg + i], 0, 0)))
                  for i in range(pg)]
    grid_spec = pltpu.PrefetchScalarGridSpec(
        num_scalar_prefetch=1,
        grid=(n, n_pages // pg),
        in_specs=[pl.BlockSpec((None, idx_heads, idx_dim), lambda b, p, pt: (b, 0, 0)),
                  pl.BlockSpec((None, idx_heads, 1), lambda b, p, pt: (b, 0, 0)),
                  pl.BlockSpec((None, 1, idx_dim), lambda b, p, pt: (b, 0, 0))] + page_specs,
        out_specs=[pl.BlockSpec((None, pg, page), lambda b, p, pt: (b, p, 0)),
                   pl.BlockSpec((None, 1, LANES), lambda b, p, pt: (b, 0, 0))],
    )
    return pl.pallas_call(
        kern,
        grid_spec=grid_spec,
        out_shape=[jax.ShapeDtypeStruct((n, n_pages, page), F32),
                   jax.ShapeDtypeStruct((n, 1, LANES), F32)],
        compiler_params=_cparams(2),
        name="sample_scores",
    )(page_table, qi3, wi3, ki_new3, *([cache_kidx] * pg))


def _sample_select_kernel(sc_ref, self_ref, sel4_ref, selself_ref, sel_ref, *, topk, past, rep):
    sc = sc_ref[...]
    sself = self_ref[:, 0:1]
    n = sc.shape[0]
    kf = jnp.float32(topk)
    row_min = jnp.minimum(jnp.min(sc, axis=1, keepdims=True), sself)
    row_max = jnp.maximum(jnp.max(sc, axis=1, keepdims=True), sself)
    n_adm = jnp.full((n, 1), past + 1, F32)

    def count_gt(t):
        return (jnp.sum(jnp.where(sc > t, 1.0, 0.0), axis=1, keepdims=True)
                + jnp.where(sself > t, 1.0, 0.0))

    def any_fn(mask):
        return jnp.max(jnp.where(mask, 1.0, 0.0)) > 0.0

    lo, hi, flo, fhi = _select_threshold(count_gt, row_min, row_max, n_adm, topk, any_fn)
    tie = flo != kf
    sel_ref[...] = jnp.where(sc > lo, 1.0, 0.0)
    selself_ref[...] = jnp.broadcast_to(jnp.where(sself > lo, 1.0, 0.0), selself_ref.shape)

    @pl.when(any_fn(tie))
    def _():
        need = kf - fhi
        blk = 512
        tri = (lax.broadcasted_iota(jnp.int32, (blk, blk), 0)
               < lax.broadcasted_iota(jnp.int32, (blk, blk), 1)).astype(BF16)
        before = jnp.zeros((n, 1), F32)
        for c in range(past // blk):
            s = sc_ref[:, c * blk:(c + 1) * blk]
            eq = s == hi
            rank = before + jnp.dot(jnp.where(eq, 1.0, 0.0).astype(BF16), tri, preferred_element_type=F32)
            keep = (s > hi) | (eq & (rank < need))
            sel_ref[:, c * blk:(c + 1) * blk] = jnp.where(tie, jnp.where(keep, 1.0, 0.0),
                                                          sel_ref[:, c * blk:(c + 1) * blk])
            before = before + jnp.sum(jnp.where(eq, 1.0, 0.0), axis=1, keepdims=True)
        keep_self = (sself > hi) | ((sself == hi) & (before < need))
        selself_ref[...] = jnp.broadcast_to(
            jnp.where(tie, jnp.where(keep_self, 1.0, 0.0), jnp.where(sself > lo, 1.0, 0.0)), selself_ref.shape)

    blk = 512
    row_lo = lax.broadcasted_iota(jnp.int32, (blk, blk * rep), 0) * rep
    col = lax.broadcasted_iota(jnp.int32, (blk, blk * rep), 1)
    spread = jnp.where((col >= row_lo) & (col < row_lo + rep), 1.0, 0.0).astype(BF16)
    for c in range(past // blk):
        sel4_ref[:, c * blk * rep:(c + 1) * blk * rep] = jnp.dot(
            sel_ref[:, c * blk:(c + 1) * blk].astype(BF16), spread, preferred_element_type=F32)


def sample_select(scores, sself, topk, rep):
    n, past = scores.shape
    kern = functools.partial(_sample_select_kernel, topk=topk, past=past, rep=rep)
    return pl.pallas_call(
        kern,
        out_shape=[jax.ShapeDtypeStruct((n, past * rep), F32), jax.ShapeDtypeStruct((n, LANES), F32)],
        scratch_shapes=[pltpu.VMEM((n, past), F32)],
        compiler_params=pltpu.CompilerParams(vmem_limit_bytes=VMEM_LIMIT),
        name="sample_select",
    )(scores, sself)


def _sample_attn_kernel(pt_ref, q_ref, kn_ref, vn_ref, sel_ref, selself_ref, bias_ref, rb0_ref, own_ref, *rest,
                        pg, n_heads, n_kv, hd):
    kpages = rest[:pg]
    vpages = rest[pg:2 * pg]
    o_ref = rest[2 * pg]
    m_ref, l_ref, acc_ref = rest[2 * pg + 1:]
    p = pl.program_id(1)
    n_steps = pl.num_programs(1)
    group = n_heads // n_kv
    scale = hd ** -0.5
    own = own_ref[...] > 0.0

    @pl.when(p == 0)
    def _():
        m_ref[...] = jnp.full(m_ref.shape, NEG, F32)
        l_ref[...] = jnp.zeros(l_ref.shape, F32)
        acc_ref[...] = jnp.zeros(acc_ref.shape, F32)

    qb = q_ref[...].astype(BF16)
    last = p == n_steps - 1
    lgs, msks = [], []
    for i in range(pg):
        lg = _dot_nt(qb, kpages[i][...].astype(BF16)) * scale
        if i == pg - 1:
            bias = jnp.where(last, bias_ref[1], bias_ref[0])
        else:
            bias = bias_ref[0]
        msk = own & (sel_ref[pl.ds(i, 1), :] > 0.0)
        lgs.append(jnp.where(msk, lg + bias, NEG))
        msks.append(msk)
    m_old = m_ref[...]
    m_new = m_old
    for lg in lgs:
        m_new = jnp.maximum(m_new, jnp.max(lg, axis=-1, keepdims=True))
    alpha = jnp.exp(m_old - m_new)
    l_new = alpha * l_ref[...]
    acc = alpha * acc_ref[...]
    for i in range(pg):
        pr = jnp.where(msks[i], jnp.exp(lgs[i] - m_new), 0.0)
        l_new = l_new + jnp.sum(pr, axis=-1, keepdims=True)
        acc = acc + jnp.dot(pr.astype(BF16), vpages[i][...].astype(BF16), preferred_element_type=F32)
    m_ref[...] = m_new
    l_ref[...] = l_new
    acc_ref[...] = acc

    @pl.when(last)
    def _():
        kn = kn_ref[...].astype(BF16).astype(F32)
        vn = vn_ref[...].astype(BF16).astype(F32)
        ls = jnp.sum(qb.astype(F32) * kn, axis=-1, keepdims=True) * scale + rb0_ref[...]
        on = selself_ref[:, 0:1] > 0.0
        ls = jnp.where(on, ls, NEG)
        m_f = jnp.maximum(m_new, ls)
        a2 = jnp.exp(m_new - m_f)
        ps = jnp.where(on, jnp.exp(ls - m_f), 0.0)
        l_f = a2 * l_new + ps
        acc_f = a2 * acc + ps.astype(BF16).astype(F32) * vn
        o_ref[...] = acc_f / l_f


def sample_attention(page_table, q3, k_rep, v_rep, sel4, selself3, bias_s, rb0, cache_k, cache_v, pg, n_kv):
    n, n_pages = page_table.shape
    n_heads, hd = q3.shape[1:]
    rows = cache_k.shape[1]
    kern = functools.partial(_sample_attn_kernel, pg=pg, n_heads=n_heads, n_kv=n_kv, hd=hd)
    own = (np.arange(rows)[None, :] % n_kv == np.arange(n_heads)[:, None] // (n_heads // n_kv)).astype(np.float32)
    kspecs = [pl.BlockSpec((None, rows, hd), (lambda b, p, pt, i=i: (pt[b, p * pg + i], 0, 0)))
              for i in range(pg)]
    per_seq = lambda b, p, pt: (b, 0, 0)
    grid_spec = pltpu.PrefetchScalarGridSpec(
        num_scalar_prefetch=1,
        grid=(n, n_pages // pg),
        in_specs=[pl.BlockSpec((None, n_heads, hd), per_seq),
                  pl.BlockSpec((None, n_heads, hd), per_seq),
                  pl.BlockSpec((None, n_heads, hd), per_seq),
                  pl.BlockSpec((None, pg, rows), lambda b, p, pt: (b, p, 0)),
                  pl.BlockSpec((None, 1, LANES), per_seq),
                  pl.BlockSpec((2, n_heads, rows), lambda b, p, pt: (0, 0, 0)),
                  pl.BlockSpec((n_heads, 1), lambda b, p, pt: (0, 0)),
                  pl.BlockSpec((n_heads, rows), lambda b, p, pt: (0, 0))] + kspecs + kspecs,
        out_specs=pl.BlockSpec((None, n_heads, hd), per_seq),
        scratch_shapes=[pltpu.VMEM((n_heads, 1), F32), pltpu.VMEM((n_heads, 1), F32),
                        pltpu.VMEM((n_heads, hd), F32)],
    )
    return pl.pallas_call(
        kern,
        grid_spec=grid_spec,
        out_shape=jax.ShapeDtypeStruct((n, n_heads, hd), F32),
        compiler_params=_cparams(2),
        name="sample_attn",
    )(page_table, q3, k_rep, v_rep, sel4, selself3, bias_s, rb0, jnp.asarray(own), *([cache_k] * pg),
      *([cache_v] * pg))


def _mix_kernel(ga_ref, gb_ref, co_ref, at_ref, o_ref):
    o_ref[...] = (_sigmoid(ga_ref[...]) * co_ref[...] + _sigmoid(gb_ref[...]) * at_ref[...]).astype(o_ref.dtype)


def gated_mix(gates, conv_out, attn):
    n, dm = conv_out.shape
    return pl.pallas_call(
        _mix_kernel,
        grid=(1,),
        in_specs=[pl.BlockSpec((n, dm), lambda i: (0, 0)), pl.BlockSpec((n, dm), lambda i: (0, 1)),
                  pl.BlockSpec((n, dm), lambda i: (0, 0)), pl.BlockSpec((n, dm), lambda i: (0, 0))],
        out_specs=pl.BlockSpec((n, dm), lambda i: (0, 0)),
        out_shape=jax.ShapeDtypeStruct((n, dm), BF16),
    )(gates, gates, conv_out, attn)


def _ffn_act_kernel(ug_ref, uv_ref, hg_ref, hv_ref, pg_ref, pv_ref, wg_ref, wv_ref, bg_ref, bv_ref, o_ref,
                    eg_ref, ev_ref, *, tt, width):
    t = pl.program_id(1)
    hist = width - 1
    pad = 8
    first = t == 0

    def conv(u_ref, halo_ref, prev_ref, w_ref, b_ref, e_ref):
        e_ref[pl.ds(pad - hist, hist), :] = jnp.where(first, prev_ref[...], halo_ref[pl.ds(pad - hist, hist), :])
        e_ref[pl.ds(pad, tt), :] = u_ref[...]
        acc = b_ref[...] + w_ref[pl.ds(hist, 1), :] * u_ref[...]
        for j in range(hist):
            acc = acc + w_ref[pl.ds(j, 1), :] * e_ref[pl.ds(pad - hist + j, tt), :]
        return acc

    g = conv(ug_ref, hg_ref, pg_ref, wg_ref, bg_ref, eg_ref)
    v = conv(uv_ref, hv_ref, pv_ref, wv_ref, bv_ref, ev_ref)
    o_ref[...] = (g * _sigmoid(g) * v).astype(o_ref.dtype)


def ffn_act_prompt(u, prev, dw, bdw, n_seq, seq, tt, tc):
    m, f2 = u.shape
    f = f2 // 2
    width = dw.shape[0]
    nt = seq // tt
    ncb = f // tc
    hb = tt // 8
    kern = functools.partial(_ffn_act_kernel, tt=tt, width=width)
    halo = lambda off: (lambda b, t, c: (jnp.maximum((b * nt + t) * hb - 1, 0), c + off))
    return pl.pallas_call(
        kern,
        grid=(n_seq, nt, ncb),
        in_specs=[pl.BlockSpec((tt, tc), lambda b, t, c: (b * nt + t, c)),
                  pl.BlockSpec((tt, tc), lambda b, t, c: (b * nt + t, c + ncb)),
                  pl.BlockSpec((8, tc), halo(0)),
                  pl.BlockSpec((8, tc), halo(ncb)),
                  pl.BlockSpec((None, width - 1, tc), lambda b, t, c: (b, 0, c)),
                  pl.BlockSpec((None, width - 1, tc), lambda b, t, c: (b, 0, c + ncb)),
                  pl.BlockSpec((width, tc), lambda b, t, c: (0, c)),
                  pl.BlockSpec((width, tc), lambda b, t, c: (0, c + ncb)),
                  pl.BlockSpec((1, tc), lambda b, t, c: (0, c)),
                  pl.BlockSpec((1, tc), lambda b, t, c: (0, c + ncb))],
        out_specs=pl.BlockSpec((tt, tc), lambda b, t, c: (b * nt + t, c)),
        out_shape=jax.ShapeDtypeStruct((m, f), BF16),
        scratch_shapes=[pltpu.VMEM((8 + tt, tc), F32), pltpu.VMEM((8 + tt, tc), F32)],
        compiler_params=_cparams(3),
        name="ffn_act",
    )(u, u, u, u, prev, prev, dw, dw, bdw.reshape(1, f2), bdw.reshape(1, f2))


def _ffn_act_step_kernel(ug_ref, uv_ref, pg_ref, pv_ref, wg_ref, wv_ref, bg_ref, bv_ref, o_ref, *, width):
    def conv(u_ref, prev_ref, w_ref, b_ref):
        acc = b_ref[...] + w_ref[pl.ds(width - 1, 1), :] * u_ref[...]
        for j in range(width - 1):
            acc = acc + w_ref[pl.ds(j, 1), :] * prev_ref[j]
        return acc
    g = conv(ug_ref, pg_ref, wg_ref, bg_ref)
    v = conv(uv_ref, pv_ref, wv_ref, bv_ref)
    o_ref[...] = (g * _sigmoid(g) * v).astype(o_ref.dtype)


def ffn_act_step(u, prev_t, dw, bdw, tc):
    n, f2 = u.shape
    f = f2 // 2
    width = dw.shape[0]
    ncb = f // tc
    kern = functools.partial(_ffn_act_step_kernel, width=width)
    return pl.pallas_call(
        kern,
        grid=(ncb,),
        in_specs=[pl.BlockSpec((n, tc), lambda c: (0, c)),
                  pl.BlockSpec((n, tc), lambda c: (0, c + ncb)),
                  pl.BlockSpec((width - 1, n, tc), lambda c: (0, 0, c)),
                  pl.BlockSpec((width - 1, n, tc), lambda c: (0, 0, c + ncb)),
                  pl.BlockSpec((width, tc), lambda c: (0, c)),
                  pl.BlockSpec((width, tc), lambda c: (0, c + ncb)),
                  pl.BlockSpec((1, tc), lambda c: (0, c)),
                  pl.BlockSpec((1, tc), lambda c: (0, c + ncb))],
        out_specs=pl.BlockSpec((n, tc), lambda c: (0, c)),
        out_shape=jax.ShapeDtypeStruct((n, f), BF16),
        compiler_params=_cparams(1),
    )(u, u, prev_t, prev_t, dw, dw, bdw.reshape(1, f2), bdw.reshape(1, f2))


def _tiles(m):
    tm = 1024 if m % 1024 == 0 else m
    return tm


def _in_projection(xn, w_in, sizes, tm):
    glu_w, q_w, k_w, v_w, qi_w, ki_w, wi_w, ga_w, gb_w = sizes
    offs = np.concatenate([[0], np.cumsum(sizes)])
    tn = 512
    glu = matmul_w(xn, w_in, int(offs[0]), glu_w, tm, tn, name="mm_glu")
    q = matmul_w(xn, w_in, int(offs[1]), q_w, tm, tn, name="mm_q")
    k = matmul_w(xn, w_in, int(offs[2]), k_w, tm, min(tn, k_w), name="mm_k")
    v = matmul_w(xn, w_in, int(offs[3]), v_w, tm, min(tn, v_w), name="mm_v")
    qi = matmul_w(xn, w_in, int(offs[4]), qi_w, tm, min(tn, qi_w), name="mm_qi")
    w_kw = jnp.pad(w_in[:, int(offs[5]):int(offs[7])], ((0, 0), (0, LANES - ki_w - wi_w)))
    kw = matmul_w(xn, w_kw, 0, LANES, tm, LANES, name="mm_kw")
    w_g = w_in[:, int(offs[7]):]
    gates = matmul_w(xn, w_g, 0, ga_w + gb_w, tm, tn, name="mm_gates")
    return glu, q, k, v, qi, kw, gates


def kernel(x_prompt, x_sample, cache_k, cache_v, cache_kidx, state_conv, state_ffn, page_table, rel_bias,
           norm_attn, w_in, dw_conv, b_dw_conv, ln_conv_g, ln_conv_b, w_conv_out, w_o, norm_ffn, w_up, dw_ffn,
           b_dw_ffn, w_down, norm_final):
    bsz, seq, dm = x_prompt.shape
    nd, dec_seq, _ = x_sample.shape
    depth, n_pool, page, n_kv, hd = cache_k.shape
    idx_dim = cache_kidx.shape[-1]
    n_pages = page_table.shape[1]
    past = n_pages * page
    width, dconv = dw_conv.shape[1:]
    fwidth = dw_ffn.shape[1]
    f = w_down.shape[1]
    n_heads = w_o.shape[1] // hd
    d_attn = n_heads * hd
    d_kv = n_kv * hd
    n_in = w_in.shape[2]
    idx_heads = (n_in - 2 * dconv - d_attn - 2 * d_kv - idx_dim - 2 * dm) // (idx_dim + 1)
    sizes = (2 * dconv, d_attn, d_kv, d_kv, idx_heads * idx_dim, idx_dim, idx_heads, dm, dm)
    assert sum(sizes) == n_in and depth == 1 and dec_seq == 1 and page == LANES and d_attn == dm

    mp = bsz * seq
    xp = x_prompt.reshape(mp, dm)
    xs = x_sample.reshape(nd, dm)
    bias3, bias_s = bias_tables(rel_bias, page, n_kv)
    tmp = _tiles(mp)
    drop = lambda a: a.reshape(a.shape[1:])
    (norm_attn, w_in, dw_conv, b_dw_conv, ln_conv_g, ln_conv_b, w_conv_out, w_o, norm_ffn, w_up, dw_ffn,
     b_dw_ffn, w_down, state_conv, state_ffn) = map(drop, (
         norm_attn, w_in, dw_conv, b_dw_conv, ln_conv_g, ln_conv_b, w_conv_out, w_o, norm_ffn, w_up, dw_ffn,
         b_dw_ffn, w_down, state_conv, state_ffn))
    kidx_pool = cache_kidx.reshape(n_pool, page, idx_dim)
    k_pool = cache_k.reshape(n_pool, page * n_kv, hd)
    v_pool = cache_v.reshape(n_pool, page * n_kv, hd)

    xn = rmsnorm_rows(xp, norm_attn, BF16, 512)
    glu, q, k, v, qi, kw, gates = _in_projection(xn, w_in, sizes, tmp)
    conv0 = jnp.zeros((bsz, width - 1, dconv), F32)
    conv_out, conv_state_p = conv_branch_prompt(glu, conv0, dw_conv, b_dw_conv, ln_conv_g, ln_conv_b,
                                                w_conv_out, bsz, seq, 256)
    mixed = attn_prompt(qi, kw, q, k, v, conv_out, gates, bias3, bsz, seq, n_kv, idx_dim, idx_heads)
    x2 = matmul_w(mixed, w_o, 0, dm, tmp, 512, res=xp, name="mm_o")
    xn2 = rmsnorm_rows(x2, norm_ffn, BF16, 512)
    u = matmul_w(xn2, w_up, 0, 2 * f, tmp, 512, name="mm_up")
    ffn0 = jnp.zeros((bsz, fwidth - 1, 2 * f), F32)
    act = ffn_act_prompt(u, ffn0, dw_ffn, b_dw_ffn, bsz, seq, 512, 512)
    x3 = matmul_w(act, w_down, 0, dm, tmp, 256, res=x2, name="mm_down")
    y_prompt = rmsnorm_rows(x3, norm_final, F32, 512).reshape(bsz, seq, dm)
    ffn_state_p = u.reshape(bsz, seq, 2 * f)[:, seq - (fwidth - 1):, :]

    xns = rmsnorm_rows(xs, norm_attn, BF16, nd)
    glu_s, q_s, k_s, v_s, qi_s, kw_s, gates_s = _in_projection(xns, w_in, sizes, nd)
    ki_s = kw_s[:, :idx_dim]
    wi_s = kw_s[:, idx_dim:idx_dim + idx_heads]
    sc_prev_t = jnp.swapaxes(state_conv, 0, 1)
    conv_out_s, u_conv_s = conv_branch_step(glu_s, sc_prev_t, dw_conv, b_dw_conv, ln_conv_g, ln_conv_b,
                                            w_conv_out)
    conv_state_s = jnp.concatenate([state_conv[:, 1:], u_conv_s[:, None, :]], axis=1)

    pg = 8
    scores3, sself3 = sample_scores(page_table, qi_s.reshape(nd, idx_heads, idx_dim),
                                    wi_s.reshape(nd, idx_heads, 1), ki_s.reshape(nd, 1, idx_dim), kidx_pool, pg)
    topk_s = min(TOPK_MAX, (past + dec_seq) // 4)
    sel4, selself = sample_select(scores3.reshape(nd, past), sself3.reshape(nd, LANES), topk_s, n_kv)
    group = n_heads // n_kv
    k_rep = jnp.repeat(k_s.reshape(nd, n_kv, hd), group, axis=1)
    v_rep = jnp.repeat(v_s.reshape(nd, n_kv, hd), group, axis=1)
    attn_s = sample_attention(page_table, q_s.reshape(nd, n_heads, hd), k_rep, v_rep,
                              sel4.reshape(nd, n_pages, page * n_kv), selself.reshape(nd, 1, LANES), bias_s,
                              rel_bias[0].reshape(n_heads, 1), k_pool, v_pool, pg, n_kv)
    mixed_s = gated_mix(gates_s, conv_out_s, attn_s.reshape(nd, dm))
    x2s = matmul_w(mixed_s, w_o, 0, dm, nd, 512, res=xs)
    xn2s = rmsnorm_rows(x2s, norm_ffn, BF16, nd)
    u_s = matmul_w(xn2s, w_up, 0, 2 * f, nd, 512)
    sf_prev_t = jnp.swapaxes(state_ffn, 0, 1)
    act_s = ffn_act_step(u_s, sf_prev_t, dw_ffn, b_dw_ffn, 512)
    x3s = matmul_w(act_s, w_down, 0, dm, nd, 256, res=x2s)
    y_sample = rmsnorm_rows(x3s, norm_final, F32, nd).reshape(nd, dec_seq, dm)
    ffn_state_s = jnp.concatenate([state_ffn[:, 1:], u_s[:, None, :]], axis=1)

    return (y_prompt, y_sample,
            k.reshape(1, bsz, seq, n_kv, hd), v.reshape(1, bsz, seq, n_kv, hd),
            kw.reshape(bsz, seq, LANES)[None, :, :, :idx_dim],
            conv_state_p[None], ffn_state_p[None],
            k_s.reshape(1, nd, dec_seq, n_kv, hd), v_s.reshape(1, nd, dec_seq, n_kv, hd),
            ki_s.reshape(1, nd, dec_seq, idx_dim),
            conv_state_s[None], ffn_state_s[None])
```

```python
import functools
import math

import numpy as np
import jax
import jax.numpy as jnp
from jax import lax
from jax.experimental import pallas as pl
from jax.experimental.pallas import tpu as pltpu

F32 = jnp.float32
BF16 = jnp.bfloat16

EPS = 1e-6
TOPK_MAX = 256
N_BUCKETS = 32
MAX_DISTANCE = 128
QB = 128
CK = 256
LANES = 128
NEG = -1e30
LOG2E = math.log2(math.e)
VMEM_LIMIT = 56 * 1024 * 1024


def _cparams(n_axes, vmem=VMEM_LIMIT):
    return pltpu.CompilerParams(dimension_semantics=("arbitrary",) * n_axes, vmem_limit_bytes=vmem)


def _dot_nt(a, b):
    return lax.dot_general(a, b, (((1,), (1,)), ((), ())), preferred_element_type=F32)


def _sigmoid(x):
    return 1.0 / (1.0 + jnp.exp(-x))


def _fold_rows(x, op):
    while x.shape[0] > 8:
        half = x.shape[0] // 2
        x = op(x[:half], x[half:])
    return x


def _rel_bucket_np(dist):
    n = np.maximum(dist, 0)
    max_exact = N_BUCKETS // 2
    nf = np.maximum(n, 1).astype(np.float32)
    large = max_exact + (np.log(nf / np.float32(max_exact)) / np.float32(math.log(MAX_DISTANCE / max_exact))
                         * np.float32(N_BUCKETS - max_exact)).astype(np.int32)
    large = np.minimum(large, N_BUCKETS - 1)
    return np.where(n < max_exact, n, large).astype(np.int32)


def _rms_kernel(x_ref, g_ref, o_ref):
    x = x_ref[...]
    y = x * lax.rsqrt(jnp.mean(x * x, axis=-1, keepdims=True) + EPS) * g_ref[...]
    o_ref[...] = y.astype(o_ref.dtype)


def rmsnorm_rows(x, g, out_dtype, tm):
    m, d = x.shape
    return pl.pallas_call(
        _rms_kernel,
        grid=(m // tm,),
        in_specs=[pl.BlockSpec((tm, d), lambda i: (i, 0)), pl.BlockSpec((1, d), lambda i: (0, 0))],
        out_specs=pl.BlockSpec((tm, d), lambda i: (i, 0)),
        out_shape=jax.ShapeDtypeStruct((m, d), out_dtype),
        compiler_params=_cparams(1),
        name="rmsnorm",
    )(x, g.reshape(1, d))


def _mm_kernel(a_ref, w_ref, o_ref):
    o_ref[...] = jnp.dot(a_ref[...], w_ref[...].astype(BF16), preferred_element_type=F32)


def _mm_res_kernel(a_ref, w_ref, r_ref, o_ref):
    o_ref[...] = r_ref[...] + jnp.dot(a_ref[...], w_ref[...].astype(BF16), preferred_element_type=F32)


def matmul_w(a, w, col0, ncols, tm, tn, res=None, name="matmul"):
    m, k = a.shape
    assert col0 % tn == 0 and ncols % tn == 0 and m % tm == 0
    cb = col0 // tn
    in_specs = [pl.BlockSpec((tm, k), lambda i, j: (i, 0)),
                pl.BlockSpec((k, tn), lambda i, j: (0, j + cb))]
    args = [a, w]
    kern = _mm_kernel
    if res is not None:
        in_specs.append(pl.BlockSpec((tm, tn), lambda i, j: (i, j)))
        args.append(res)
        kern = _mm_res_kernel
    return pl.pallas_call(
        kern,
        grid=(m // tm, ncols // tn),
        in_specs=in_specs,
        out_specs=pl.BlockSpec((tm, tn), lambda i, j: (i, j)),
        out_shape=jax.ShapeDtypeStruct((m, ncols), F32),
        compiler_params=_cparams(2),
        name=name,
    )(*args)


def _bias_kernel(rb_ref, bk3_ref, bks_ref, o3_ref, os_ref, *, n_heads):
    def head(h, carry):
        for t in range(3):
            bk = bk3_ref[t]
            acc = jnp.zeros(bk.shape, F32)
            for b in range(N_BUCKETS):
                acc = jnp.where(bk == b, rb_ref[b, h], acc)
            o3_ref[t, h] = acc * LOG2E
        for t in range(2):
            bk = bks_ref[t]
            acc = jnp.zeros(bk.shape, F32)
            for b in range(N_BUCKETS):
                acc = jnp.where(bk == b, rb_ref[b, h], acc)
            os_ref[t, pl.ds(h, 1), :] = acc
        return carry
    lax.fori_loop(0, n_heads, head, 0)


def bias_tables(rel_bias, page, rep):
    n_heads = rel_bias.shape[1]
    cols = page * rep
    i = np.arange(QB)[None, :]
    k = np.arange(QB)[:, None]
    bk3 = np.stack([_rel_bucket_np(i - k + 2 * QB), _rel_bucket_np(i - k + QB), _rel_bucket_np(i - k)])
    assert (_rel_bucket_np(np.arange(QB + 1, 1 << 20)) == N_BUCKETS - 1).all()
    assert (bk3[0] == N_BUCKETS - 1).all()
    assert page >= QB
    bks = np.stack([np.full((1, cols), N_BUCKETS - 1, np.int32),
                    _rel_bucket_np(page - np.arange(cols) // rep)[None, :]])
    return pl.pallas_call(
        functools.partial(_bias_kernel, n_heads=n_heads),
        in_specs=[pl.BlockSpec(memory_space=pltpu.SMEM),
                  pl.BlockSpec(memory_space=pltpu.VMEM), pl.BlockSpec(memory_space=pltpu.VMEM)],
        out_specs=[pl.BlockSpec(memory_space=pltpu.VMEM), pl.BlockSpec(memory_space=pltpu.VMEM)],
        out_shape=[jax.ShapeDtypeStruct((3, n_heads, QB, QB), F32),
                   jax.ShapeDtypeStruct((2, n_heads, cols), F32)],
    )(rel_bias, jnp.asarray(bk3), jnp.asarray(bks))


def _conv_kernel(glu_ref, prev_ref, dw_ref, bdw_ref, lng_ref, lnb_ref, wout_ref, o_ref, st_ref,
                 ext_ref, h_ref, wbf_ref, *, tt, width, dconv):
    b = pl.program_id(0)
    t = pl.program_id(1)
    pad = 32
    hist = width - 1

    @pl.when((b == 0) & (t == 0))
    def _():
        wbf_ref[...] = wout_ref[...].astype(BF16)

    @pl.when(t == 0)
    def _():
        ext_ref[pl.ds(pad - hist, hist), :] = prev_ref[...]

    @pl.when(t > 0)
    def _():
        ext_ref[pl.ds(0, pad), :] = ext_ref[pl.ds(tt, pad), :]

    glu = glu_ref[...]
    u = glu[:, :dconv] * _sigmoid(glu[:, dconv:])
    ext_ref[pl.ds(pad, tt), :] = u
    st_ref[...] = ext_ref[pl.ds(pad + tt - hist, hist), :]

    for c in range(dconv // LANES):
        cs = slice(c * LANES, (c + 1) * LANES)
        acc = jnp.zeros((tt, LANES), F32) + bdw_ref[:, cs]
        for j in range(width):
            acc = acc + dw_ref[pl.ds(j, 1), cs] * ext_ref[pl.ds(pad - hist + j, tt), cs]
        h_ref[:, cs] = acc

    h = h_ref[...]
    mu = jnp.mean(h, axis=-1, keepdims=True)
    var = jnp.mean(jnp.square(h - mu), axis=-1, keepdims=True)
    y = (h - mu) * lax.rsqrt(var + EPS) * lng_ref[...] + lnb_ref[...]
    y = y * _sigmoid(y)
    o_ref[...] = jnp.dot(y.astype(BF16), wbf_ref[...], preferred_element_type=F32)


def conv_branch_prompt(glu_pre, prev, dw, bdw, lng, lnb, wout, n_seq, seq, tt):
    width, dconv = dw.shape
    dm = wout.shape[1]
    nt = seq // tt
    kern = functools.partial(_conv_kernel, tt=tt, width=width, dconv=dconv)
    return pl.pallas_call(
        kern,
        grid=(n_seq, nt),
        in_specs=[pl.BlockSpec((tt, 2 * dconv), lambda b, t: (b * nt + t, 0)),
                  pl.BlockSpec((None, width - 1, dconv), lambda b, t: (b, 0, 0)),
                  pl.BlockSpec((width, dconv), lambda b, t: (0, 0)),
                  pl.BlockSpec((1, dconv), lambda b, t: (0, 0)),
                  pl.BlockSpec((1, dconv), lambda b, t: (0, 0)),
                  pl.BlockSpec((1, dconv), lambda b, t: (0, 0)),
                  pl.BlockSpec((dconv, dm), lambda b, t: (0, 0))],
        out_specs=[pl.BlockSpec((tt, dm), lambda b, t: (b * nt + t, 0)),
                   pl.BlockSpec((None, width - 1, dconv), lambda b, t: (b, 0, 0))],
        out_shape=[jax.ShapeDtypeStruct((n_seq * seq, dm), F32),
                   jax.ShapeDtypeStruct((n_seq, width - 1, dconv), F32)],
        scratch_shapes=[pltpu.VMEM((32 + tt, dconv), F32), pltpu.VMEM((tt, dconv), F32),
                        pltpu.VMEM((dconv, dm), BF16)],
        compiler_params=_cparams(2),
        name="conv_branch",
    )(glu_pre, prev, dw, bdw.reshape(1, dconv), lng.reshape(1, dconv), lnb.reshape(1, dconv), wout)


def _conv_step_kernel(glu_ref, prev_ref, dw_ref, bdw_ref, lng_ref, lnb_ref, wout_ref, o_ref, u_ref,
                      *, width, dconv):
    glu = glu_ref[...]
    u = glu[:, :dconv] * _sigmoid(glu[:, dconv:])
    u_ref[...] = u
    h = bdw_ref[...] + dw_ref[pl.ds(width - 1, 1), :] * u
    for j in range(width - 1):
        h = h + dw_ref[pl.ds(j, 1), :] * prev_ref[j]
    mu = jnp.mean(h, axis=-1, keepdims=True)
    var = jnp.mean(jnp.square(h - mu), axis=-1, keepdims=True)
    y = (h - mu) * lax.rsqrt(var + EPS) * lng_ref[...] + lnb_ref[...]
    y = y * _sigmoid(y)
    o_ref[...] = jnp.dot(y.astype(BF16), wout_ref[...].astype(BF16), preferred_element_type=F32)


def conv_branch_step(glu_pre, prev_t, dw, bdw, lng, lnb, wout):
    width, dconv = dw.shape
    n = glu_pre.shape[0]
    dm = wout.shape[1]
    kern = functools.partial(_conv_step_kernel, width=width, dconv=dconv)
    return pl.pallas_call(
        kern,
        out_shape=[jax.ShapeDtypeStruct((n, dm), F32), jax.ShapeDtypeStruct((n, dconv), F32)],
        compiler_params=pltpu.CompilerParams(vmem_limit_bytes=VMEM_LIMIT),
    )(glu_pre, prev_t, dw, bdw.reshape(1, dconv), lng.reshape(1, dconv), lnb.reshape(1, dconv), wout)


def _select_threshold(count_gt, row_min, row_max, n_adm, topk, any_fn):
    kf = jnp.float32(topk)
    full = n_adm <= kf
    lo0 = row_min - (1.0 + jnp.abs(row_min))
    hi0 = row_max
    flo0 = jnp.where(full, kf, n_adm)
    fhi0 = jnp.zeros_like(lo0)

    def active_rows(lo, hi, flo):
        mid = 0.5 * lo + 0.5 * hi
        return (flo != kf) & (lo < mid) & (mid < hi)

    def cond(st):
        lo, hi, flo, fhi = st
        return any_fn(active_rows(lo, hi, flo))

    def step(st, interpolate):
        lo, hi, flo, fhi = st
        act = active_rows(lo, hi, flo)
        mid = 0.5 * lo + 0.5 * hi
        if interpolate:
            guess = lo + (hi - lo) * ((flo - kf) / (flo - fhi))
            mid = jnp.where((lo < guess) & (guess < hi), guess, mid)
        c = count_gt(mid)
        up = act & (c >= kf)
        dn = act & (c < kf)
        return (jnp.where(up, mid, lo), jnp.where(dn, mid, hi),
                jnp.where(up, c, flo), jnp.where(dn, c, fhi))

    def body(st):
        return step(step(st, True), False)

    lo, hi, flo, fhi = lax.while_loop(cond, body, (lo0, hi0, flo0, fhi0))
    lo = jnp.where(full, -jnp.inf, lo)
    return lo, hi, flo, fhi


def _attn_prompt_kernel(qi_ref, wi_ref, kw_ref, q_ref, k_ref, v_ref, co_ref, ga_ref, gb_ref, bias_ref,
                        o_ref, kd_ref, kb_ref, vt_ref, wit_ref, qib_ref, sc_ref, sel_ref, qs_ref,
                        m_ref, l_ref, acc_ref,
                        *, seq, n_heads, n_kv, idx_heads, idx_dim, topk, hd):
    j = pl.program_id(1)
    group = n_heads // n_kv
    nck = (j * QB + QB + CK - 1) // CK
    lane = lax.broadcasted_iota(jnp.int32, (1, LANES), 1)
    kf = jnp.float32(topk)

    @pl.when(j == 0)
    def _():
        kw = kw_ref[...]
        kd_ref[0] = jnp.where(lane < idx_dim, kw, 0.0).astype(BF16)
        kd_ref[1] = jnp.where(lane >= idx_dim, pltpu.roll(kw, idx_dim, 1), 0.0).astype(BF16)
        for g in range(n_kv):
            kb_ref[g] = k_ref[:, g * hd:(g + 1) * hd].astype(BF16)
            for c in range(seq // CK):
                vt_ref[g, c] = v_ref[c * CK:(c + 1) * CK, g * hd:(g + 1) * hd].T.astype(BF16)

    wit_ref[...] = (wi_ref[...] * (idx_heads ** -0.5)).T
    qib_ref[...] = (qi_ref[...] * (idx_dim ** -0.5)).astype(BF16)
    qpos = j * QB + lax.broadcasted_iota(jnp.int32, (CK, QB), 1)
    krow = lax.broadcasted_iota(jnp.int32, (CK, QB), 0)
    per_pair = LANES // idx_dim

    def score_chunk(c, carry):
        mn, mx = carry
        k0 = pl.multiple_of(c * CK, CK)
        acc = jnp.zeros((CK, QB), F32)
        for p in range(idx_heads // per_pair):
            rhs = qib_ref[:, p * LANES:(p + 1) * LANES]
            for r in range(per_pair):
                s = _dot_nt(kd_ref[r, pl.ds(k0, CK), :], rhs)
                acc = acc + wit_ref[pl.ds(idx_dim + p * per_pair + r, 1), :] * jnp.maximum(s, 0.0)
        adm = (krow + c * CK) <= qpos
        sc_ref[c] = jnp.where(adm, acc, -jnp.inf)
        mn = jnp.minimum(mn, _fold_rows(jnp.where(adm, acc, jnp.inf), jnp.minimum))
        mx = jnp.maximum(mx, _fold_rows(jnp.where(adm, acc, -jnp.inf), jnp.maximum))
        return mn, mx

    mn8, mx8 = lax.fori_loop(0, nck, score_chunk,
                             (jnp.full((8, QB), jnp.inf, F32), jnp.full((8, QB), -jnp.inf, F32)))
    row_min = jnp.min(mn8, axis=0, keepdims=True)
    row_max = jnp.max(mx8, axis=0, keepdims=True)
    n_adm = (j * QB + 1 + lax.broadcasted_iota(jnp.int32, (1, QB), 1)).astype(F32)

    def count_gt(t):
        def cbody(c, acc):
            return acc + _fold_rows(jnp.where(sc_ref[c] > t, 1.0, 0.0), jnp.add)
        part = lax.fori_loop(0, nck, cbody, jnp.zeros((8, QB), F32))
        return jnp.sum(part, axis=0, keepdims=True)

    def any_fn(mask):
        return jnp.max(jnp.where(mask, 1.0, 0.0)) > 0.0

    lo, hi, flo, fhi = _select_threshold(count_gt, row_min, row_max, n_adm, topk, any_fn)
    tie = flo != kf

    def sel_chunk(c, carry):
        sel_ref[c] = jnp.where(sc_ref[c] > lo, 1.0, 0.0)
        return carry
    lax.fori_loop(0, nck, sel_chunk, 0)

    @pl.when(any_fn(tie))
    def _():
        need = kf - fhi
        lower = (lax.broadcasted_iota(jnp.int32, (CK, CK), 1)
                 < lax.broadcasted_iota(jnp.int32, (CK, CK), 0)).astype(BF16)

        def tie_chunk(c, before):
            s = sc_ref[c]
            eq = s == hi
            eqf = jnp.where(eq, 1.0, 0.0)
            rank = before + jnp.dot(lower, eqf.astype(BF16), preferred_element_type=F32)
            keep = (s > hi) | (eq & (rank < need))
            sel_ref[c] = jnp.where(tie, jnp.where(keep, 1.0, 0.0), sel_ref[c])
            return before + jnp.sum(eqf, axis=0, keepdims=True)
        lax.fori_loop(0, nck, tie_chunk, jnp.zeros((1, QB), F32))

    scale2 = hd ** -0.5 * LOG2E
    for h in range(n_heads):
        qs_ref[h] = q_ref[:, h * hd:(h + 1) * hd].astype(BF16)
    m_ref[...] = jnp.full(m_ref.shape, NEG, F32)
    l_ref[...] = jnp.zeros(l_ref.shape, F32)
    acc_ref[...] = jnp.zeros(acc_ref.shape, F32)

    def att_chunk(c, carry):
        k0 = pl.multiple_of(c * CK, CK)
        msk = jnp.concatenate([sel_ref[c]] * group, axis=1) > 0.0
        tis = [jnp.clip(c * (CK // QB) + s - j + 2, 0, 2) for s in range(CK // QB)]
        for g in range(n_kv):
            kc = kb_ref[g, pl.ds(k0, CK), :]
            qg = qs_ref[pl.ds(g * group, group)].reshape(group * QB, hd)
            st = _dot_nt(kc, qg) * scale2
            rows = [jnp.concatenate([bias_ref[ti, g * group + hh] for hh in range(group)], axis=1)
                    for ti in tis]
            lg = jnp.where(msk, st + jnp.concatenate(rows, axis=0), -jnp.inf)
            m_old = m_ref[g]
            m_new = jnp.maximum(m_old, jnp.max(_fold_rows(lg, jnp.maximum), axis=0, keepdims=True))
            alpha = jnp.exp2(m_old - m_new)
            p = jnp.exp2(lg - m_new)
            l_ref[g] = alpha * l_ref[g] + jnp.sum(_fold_rows(p, jnp.add), axis=0, keepdims=True)
            pv = jnp.dot(vt_ref[g, c], p.astype(BF16), preferred_element_type=F32)
            acc_ref[g] = alpha * acc_ref[g] + pv
            m_ref[g] = m_new
        return carry
    lax.fori_loop(0, nck, att_chunk, 0)

    for g in range(n_kv):
        ot = acc_ref[g] / l_ref[g]
        for hh in range(group):
            cs = slice((g * group + hh) * hd, (g * group + hh + 1) * hd)
            o = ot[:, hh * QB:(hh + 1) * QB].T
            mixed = _sigmoid(ga_ref[:, cs]) * co_ref[:, cs] + _sigmoid(gb_ref[:, cs]) * o
            o_ref[:, cs] = mixed.astype(o_ref.dtype)


def attn_prompt(qi, kw, q, k, v, conv_out, gates, bias3, n_seq, seq, n_kv, idx_dim, idx_heads):
    m, dm = q.shape
    hd = k.shape[1] // n_kv
    n_heads = dm // hd
    group = n_heads // n_kv
    nb = seq // QB
    topk = min(TOPK_MAX, seq // 4)
    assert seq % CK == 0 and LANES % idx_dim == 0 and kw.shape[1] == LANES
    kern = functools.partial(_attn_prompt_kernel, seq=seq, n_heads=n_heads, n_kv=n_kv, idx_heads=idx_heads,
                             idx_dim=idx_dim, topk=topk, hd=hd)
    row = lambda b, j: (b * nb + j, 0)
    return pl.pallas_call(
        kern,
        grid=(n_seq, nb),
        in_specs=[pl.BlockSpec((QB, idx_heads * idx_dim), row),
                  pl.BlockSpec((QB, LANES), row),
                  pl.BlockSpec((seq, LANES), lambda b, j: (b, 0)),
                  pl.BlockSpec((QB, dm), row),
                  pl.BlockSpec((seq, n_kv * hd), lambda b, j: (b, 0)),
                  pl.BlockSpec((seq, n_kv * hd), lambda b, j: (b, 0)),
                  pl.BlockSpec((QB, dm), row),
                  pl.BlockSpec((QB, dm), lambda b, j: (b * nb + j, 0)),
                  pl.BlockSpec((QB, dm), lambda b, j: (b * nb + j, 1)),
                  pl.BlockSpec((3, n_heads, QB, QB), lambda b, j: (0, 0, 0, 0))],
        out_specs=pl.BlockSpec((QB, dm), row),
        out_shape=jax.ShapeDtypeStruct((m, dm), BF16),
        scratch_shapes=[pltpu.VMEM((2, seq, LANES), BF16),
                        pltpu.VMEM((n_kv, seq, hd), BF16),
                        pltpu.VMEM((n_kv, seq // CK, hd, CK), BF16),
                        pltpu.VMEM((LANES, QB), F32),
                        pltpu.VMEM((QB, idx_heads * idx_dim), BF16),
                        pltpu.VMEM((seq // CK, CK, QB), F32),
                        pltpu.VMEM((seq // CK, CK, QB), F32),
                        pltpu.VMEM((n_heads, QB, hd), BF16),
                        pltpu.VMEM((n_kv, 1, group * QB), F32),
                        pltpu.VMEM((n_kv, 1, group * QB), F32),
                        pltpu.VMEM((n_kv, hd, group * QB), F32)],
        compiler_params=_cparams(2),
        name="attn_prompt",
    )(qi, kw, kw, q, k, v, conv_out, gates, gates, bias3)


def _sample_score_kernel(pt_ref, qi_ref, wi_ref, kn_ref, *rest, pg, idx_heads, idx_dim):
    pages = rest[:pg]
    o_ref, self_ref = rest[pg], rest[pg + 1]
    qi = qi_ref[...] * (idx_dim ** -0.5)
    wi = wi_ref[...] * (idx_heads ** -0.5)
    qb = qi.astype(BF16)
    for i in range(pg):
        s = _dot_nt(qb, pages[i][...].astype(BF16))
        o_ref[pl.ds(i, 1), :] = jnp.sum(wi * jnp.maximum(s, 0.0), axis=0, keepdims=True)

    @pl.when(pl.program_id(1) == 0)
    def _():
        kn = kn_ref[...].astype(BF16).astype(F32)
        s = jnp.sum(qb.astype(F32) * kn, axis=1, keepdims=True)
        sself = jnp.sum(wi * jnp.maximum(s, 0.0), axis=0, keepdims=True)
        self_ref[...] = jnp.broadcast_to(sself, self_ref.shape)


def sample_scores(page_table, qi3, wi3, ki_new3, cache_kidx, pg):
    n, n_pages = page_table.shape
    idx_heads, idx_dim = qi3.shape[1:]
    page = cache_kidx.shape[1]
    kern = functools.partial(_sample_score_kernel, pg=pg, idx_heads=idx_heads, idx_dim=idx_dim)
    page_specs = [pl.BlockSpec((None, page, idx_dim), (lambda b, p, pt, i=i: (pt[b, p * pg + i], 0, 0)))
                  for i in range(pg)]
    grid_spec = pltpu.PrefetchScalarGridSpec(
        num_scalar_prefetch=1,
        grid=(n, n_pages // pg),
        in_specs=[pl.BlockSpec((None, idx_heads, idx_dim), lambda b, p, pt: (b, 0, 0)),
                  pl.BlockSpec((None, idx_heads, 1), lambda b, p, pt: (b, 0, 0)),
                  pl.BlockSpec((None, 1, idx_dim), lambda b, p, pt: (b, 0, 0))] + page_specs,
        out_specs=[pl.BlockSpec((None, pg, page), lambda b, p, pt: (b, p, 0)),
                   pl.BlockSpec((None, 1, LANES), lambda b, p, pt: (b, 0, 0))],
    )
    return pl.pallas_call(
        kern,
        grid_spec=grid_spec,
        out_shape=[jax.ShapeDtypeStruct((n, n_pages, page), F32),
                   jax.ShapeDtypeStruct((n, 1, LANES), F32)],
        compiler_params=_cparams(2),
        name="sample_scores",
    )(page_table, qi3, wi3, ki_new3, *([cache_kidx] * pg))


def _sample_select_kernel(sc_ref, self_ref, sel4_ref, selself_ref, sel_ref, *, topk, past, rep):
    sc = sc_ref[...]
    sself = self_ref[:, 0:1]
    n = sc.shape[0]
    kf = jnp.float32(topk)
    row_min = jnp.minimum(jnp.min(sc, axis=1, keepdims=True), sself)
    row_max = jnp.maximum(jnp.max(sc, axis=1, keepdims=True), sself)
    n_adm = jnp.full((n, 1), past + 1, F32)

    def count_gt(t):
        return (jnp.sum(jnp.where(sc > t, 1.0, 0.0), axis=1, keepdims=True)
                + jnp.where(sself > t, 1.0, 0.0))

    def any_fn(mask):
        return jnp.max(jnp.where(mask, 1.0, 0.0)) > 0.0

    lo, hi, flo, fhi = _select_threshold(count_gt, row_min, row_max, n_adm, topk, any_fn)
    tie = flo != kf
    sel_ref[...] = jnp.where(sc > lo, 1.0, 0.0)
    selself_ref[...] = jnp.broadcast_to(jnp.where(sself > lo, 1.0, 0.0), selself_ref.shape)

    @pl.when(any_fn(tie))
    def _():
        need = kf - fhi
        blk = 512
        tri = (lax.broadcasted_iota(jnp.int32, (blk, blk), 0)
               < lax.broadcasted_iota(jnp.int32, (blk, blk), 1)).astype(BF16)
        before = jnp.zeros((n, 1), F32)
        for c in range(past // blk):
            s = sc_ref[:, c * blk:(c + 1) * blk]
            eq = s == hi
            rank = before + jnp.dot(jnp.where(eq, 1.0, 0.0).astype(BF16), tri, preferred_element_type=F32)
            keep = (s > hi) | (eq & (rank < need))
            sel_ref[:, c * blk:(c + 1) * blk] = jnp.where(tie, jnp.where(keep, 1.0, 0.0),
                                                          sel_ref[:, c * blk:(c + 1) * blk])
            before = before + jnp.sum(jnp.where(eq, 1.0, 0.0), axis=1, keepdims=True)
        keep_self = (sself > hi) | ((sself == hi) & (before < need))
        selself_ref[...] = jnp.broadcast_to(
            jnp.where(tie, jnp.where(keep_self, 1.0, 0.0), jnp.where(sself > lo, 1.0, 0.0)), selself_ref.shape)

    blk = 512
    row_lo = lax.broadcasted_iota(jnp.int32, (blk, blk * rep), 0) * rep
    col = lax.broadcasted_iota(jnp.int32, (blk, blk * rep), 1)
    spread = jnp.where((col >= row_lo) & (col < row_lo + rep), 1.0, 0.0).astype(BF16)
    for c in range(past // blk):
        sel4_ref[:, c * blk * rep:(c + 1) * blk * rep] = jnp.dot(
            sel_ref[:, c * blk:(c + 1) * blk].astype(BF16), spread, preferred_element_type=F32)


def sample_select(scores, sself, topk, rep):
    n, past = scores.shape
    kern = functools.partial(_sample_select_kernel, topk=topk, past=past, rep=rep)
    return pl.pallas_call(
        kern,
        out_shape=[jax.ShapeDtypeStruct((n, past * rep), F32), jax.ShapeDtypeStruct((n, LANES), F32)],
        scratch_shapes=[pltpu.VMEM((n, past), F32)],
        compiler_params=pltpu.CompilerParams(vmem_limit_bytes=VMEM_LIMIT),
        name="sample_select",
    )(scores, sself)


def _sample_attn_kernel(pt_ref, q_ref, kn_ref, vn_ref, sel_ref, selself_ref, bias_ref, rb0_ref, own_ref, *rest,
                        pg, n_heads, n_kv, hd):
    kpages = rest[:pg]
    vpages = rest[pg:2 * pg]
    o_ref = rest[2 * pg]
    m_ref, l_ref, acc_ref = rest[2 * pg + 1:]
    p = pl.program_id(1)
    n_steps = pl.num_programs(1)
    group = n_heads // n_kv
    scale = hd ** -0.5
    own = own_ref[...] > 0.0

    @pl.when(p == 0)
    def _():
        m_ref[...] = jnp.full(m_ref.shape, NEG, F32)
        l_ref[...] = jnp.zeros(l_ref.shape, F32)
        acc_ref[...] = jnp.zeros(acc_ref.shape, F32)

    qb = q_ref[...].astype(BF16)
    last = p == n_steps - 1
    lgs, msks = [], []
    for i in range(pg):
        lg = _dot_nt(qb, kpages[i][...].astype(BF16)) * scale
        if i == pg - 1:
            bias = jnp.where(last, bias_ref[1], bias_ref[0])
        else:
            bias = bias_ref[0]
        msk = own & (sel_ref[pl.ds(i, 1), :] > 0.0)
        lgs.append(jnp.where(msk, lg + bias, NEG))
        msks.append(msk)
    m_old = m_ref[...]
    m_new = m_old
    for lg in lgs:
        m_new = jnp.maximum(m_new, jnp.max(lg, axis=-1, keepdims=True))
    alpha = jnp.exp(m_old - m_new)
    l_new = alpha * l_ref[...]
    acc = alpha * acc_ref[...]
    for i in range(pg):
        pr = jnp.where(msks[i], jnp.exp(lgs[i] - m_new), 0.0)
        l_new = l_new + jnp.sum(pr, axis=-1, keepdims=True)
        acc = acc + jnp.dot(pr.astype(BF16), vpages[i][...].astype(BF16), preferred_element_type=F32)
    m_ref[...] = m_new
    l_ref[...] = l_new
    acc_ref[...] = acc

    @pl.when(last)
    def _():
        kn = kn_ref[...].astype(BF16).astype(F32)
        vn = vn_ref[...].astype(BF16).astype(F32)
        ls = jnp.sum(qb.astype(F32) * kn, axis=-1, keepdims=True) * scale + rb0_ref[...]
        on = selself_ref[:, 0:1] > 0.0
        ls = jnp.where(on, ls, NEG)
        m_f = jnp.maximum(m_new, ls)
        a2 = jnp.exp(m_new - m_f)
        ps = jnp.where(on, jnp.exp(ls - m_f), 0.0)
        l_f = a2 * l_new + ps
        acc_f = a2 * acc + ps.astype(BF16).astype(F32) * vn
        o_ref[...] = acc_f / l_f


def sample_attention(page_table, q3, k_rep, v_rep, sel4, selself3, bias_s, rb0, cache_k, cache_v, pg, n_kv):
    n, n_pages = page_table.shape
    n_heads, hd = q3.shape[1:]
    rows = cache_k.shape[1]
    kern = functools.partial(_sample_attn_kernel, pg=pg, n_heads=n_heads, n_kv=n_kv, hd=hd)
    own = (np.arange(rows)[None, :] % n_kv == np.arange(n_heads)[:, None] // (n_heads // n_kv)).astype(np.float32)
    kspecs = [pl.BlockSpec((None, rows, hd), (lambda b, p, pt, i=i: (pt[b, p * pg + i], 0, 0)))
              for i in range(pg)]
    per_seq = lambda b, p, pt: (b, 0, 0)
    grid_spec = pltpu.PrefetchScalarGridSpec(
        num_scalar_prefetch=1,
        grid=(n, n_pages // pg),
        in_specs=[pl.BlockSpec((None, n_heads, hd), per_seq),
                  pl.BlockSpec((None, n_heads, hd), per_seq),
                  pl.BlockSpec((None, n_heads, hd), per_seq),
                  pl.BlockSpec((None, pg, rows), lambda b, p, pt: (b, p, 0)),
                  pl.BlockSpec((None, 1, LANES), per_seq),
                  pl.BlockSpec((2, n_heads, rows), lambda b, p, pt: (0, 0, 0)),
                  pl.BlockSpec((n_heads, 1), lambda b, p, pt: (0, 0)),
                  pl.BlockSpec((n_heads, rows), lambda b, p, pt: (0, 0))] + kspecs + kspecs,
        out_specs=pl.BlockSpec((None, n_heads, hd), per_seq),
        scratch_shapes=[pltpu.VMEM((n_heads, 1), F32), pltpu.VMEM((n_heads, 1), F32),
                        pltpu.VMEM((n_heads, hd), F32)],
    )
    return pl.pallas_call(
        kern,
        grid_spec=grid_spec,
        out_shape=jax.ShapeDtypeStruct((n, n_heads, hd), F32),
        compiler_params=_cparams(2),
        name="sample_attn",
    )(page_table, q3, k_rep, v_rep, sel4, selself3, bias_s, rb0, jnp.asarray(own), *([cache_k] * pg),
      *([cache_v] * pg))


def _mix_kernel(ga_ref, gb_ref, co_ref, at_ref, o_ref):
    o_ref[...] = (_sigmoid(ga_ref[...]) * co_ref[...] + _sigmoid(gb_ref[...]) * at_ref[...]).astype(o_ref.dtype)


def gated_mix(gates, conv_out, attn):
    n, dm = conv_out.shape
    return pl.pallas_call(
        _mix_kernel,
        grid=(1,),
        in_specs=[pl.BlockSpec((n, dm), lambda i: (0, 0)), pl.BlockSpec((n, dm), lambda i: (0, 1)),
                  pl.BlockSpec((n, dm), lambda i: (0, 0)), pl.BlockSpec((n, dm), lambda i: (0, 0))],
        out_specs=pl.BlockSpec((n, dm), lambda i: (0, 0)),
        out_shape=jax.ShapeDtypeStruct((n, dm), BF16),
    )(gates, gates, conv_out, attn)


def _ffn_act_kernel(ug_ref, uv_ref, hg_ref, hv_ref, pg_ref, pv_ref, wg_ref, wv_ref, bg_ref, bv_ref, o_ref,
                    eg_ref, ev_ref, *, tt, width):
    t = pl.program_id(1)
    hist = width - 1
    pad = 8
    first = t == 0

    def conv(u_ref, halo_ref, prev_ref, w_ref, b_ref, e_ref):
        e_ref[pl.ds(pad - hist, hist), :] = jnp.where(first, prev_ref[...], halo_ref[pl.ds(pad - hist, hist), :])
        e_ref[pl.ds(pad, tt), :] = u_ref[...]
        acc = b_ref[...] + w_ref[pl.ds(hist, 1), :] * u_ref[...]
        for j in range(hist):
            acc = acc + w_ref[pl.ds(j, 1), :] * e_ref[pl.ds(pad - hist + j, tt), :]
        return acc

    g = conv(ug_ref, hg_ref, pg_ref, wg_ref, bg_ref, eg_ref)
    v = conv(uv_ref, hv_ref, pv_ref, wv_ref, bv_ref, ev_ref)
    o_ref[...] = (g * _sigmoid(g) * v).astype(o_ref.dtype)


def ffn_act_prompt(u, prev, dw, bdw, n_seq, seq, tt, tc):
    m, f2 = u.shape
    f = f2 // 2
    width = dw.shape[0]
    nt = seq // tt
    ncb = f // tc
    hb = tt // 8
    kern = functools.partial(_ffn_act_kernel, tt=tt, width=width)
    halo = lambda off: (lambda b, t, c: (jnp.maximum((b * nt + t) * hb - 1, 0), c + off))
    return pl.pallas_call(
        kern,
        grid=(n_seq, nt, ncb),
        in_specs=[pl.BlockSpec((tt, tc), lambda b, t, c: (b * nt + t, c)),
                  pl.BlockSpec((tt, tc), lambda b, t, c: (b * nt + t, c + ncb)),
                  pl.BlockSpec((8, tc), halo(0)),
                  pl.BlockSpec((8, tc), halo(ncb)),
                  pl.BlockSpec((None, width - 1, tc), lambda b, t, c: (b, 0, c)),
                  pl.BlockSpec((None, width - 1, tc), lambda b, t, c: (b, 0, c + ncb)),
                  pl.BlockSpec((width, tc), lambda b, t, c: (0, c)),
                  pl.BlockSpec((width, tc), lambda b, t, c: (0, c + ncb)),
                  pl.BlockSpec((1, tc), lambda b, t, c: (0, c)),
                  pl.BlockSpec((1, tc), lambda b, t, c: (0, c + ncb))],
        out_specs=pl.BlockSpec((tt, tc), lambda b, t, c: (b * nt + t, c)),
        out_shape=jax.ShapeDtypeStruct((m, f), BF16),
        scratch_shapes=[pltpu.VMEM((8 + tt, tc), F32), pltpu.VMEM((8 + tt, tc), F32)],
        compiler_params=_cparams(3),
        name="ffn_act",
    )(u, u, u, u, prev, prev, dw, dw, bdw.reshape(1, f2), bdw.reshape(1, f2))


def _ffn_act_step_kernel(ug_ref, uv_ref, pg_ref, pv_ref, wg_ref, wv_ref, bg_ref, bv_ref, o_ref, *, width):
    def conv(u_ref, prev_ref, w_ref, b_ref):
        acc = b_ref[...] + w_ref[pl.ds(width - 1, 1), :] * u_ref[...]
        for j in range(width - 1):
            acc = acc + w_ref[pl.ds(j, 1), :] * prev_ref[j]
        return acc
    g = conv(ug_ref, pg_ref, wg_ref, bg_ref)
    v = conv(uv_ref, pv_ref, wv_ref, bv_ref)
    o_ref[...] = (g * _sigmoid(g) * v).astype(o_ref.dtype)


def ffn_act_step(u, prev_t, dw, bdw, tc):
    n, f2 = u.shape
    f = f2 // 2
    width = dw.shape[0]
    ncb = f // tc
    kern = functools.partial(_ffn_act_step_kernel, width=width)
    return pl.pallas_call(
        kern,
        grid=(ncb,),
        in_specs=[pl.BlockSpec((n, tc), lambda c: (0, c)),
                  pl.BlockSpec((n, tc), lambda c: (0, c + ncb)),
                  pl.BlockSpec((width - 1, n, tc), lambda c: (0, 0, c)),
                  pl.BlockSpec((width - 1, n, tc), lambda c: (0, 0, c + ncb)),
                  pl.BlockSpec((width, tc), lambda c: (0, c)),
                  pl.BlockSpec((width, tc), lambda c: (0, c + ncb)),
                  pl.BlockSpec((1, tc), lambda c: (0, c)),
                  pl.BlockSpec((1, tc), lambda c: (0, c + ncb))],
        out_specs=pl.BlockSpec((n, tc), lambda c: (0, c)),
        out_shape=jax.ShapeDtypeStruct((n, f), BF16),
        compiler_params=_cparams(1),
    )(u, u, prev_t, prev_t, dw, dw, bdw.reshape(1, f2), bdw.reshape(1, f2))


def _tiles(m):
    tm = 1024 if m % 1024 == 0 else m
    return tm


def _in_projection(xn, w_in, sizes, tm):
    glu_w, q_w, k_w, v_w, qi_w, ki_w, wi_w, ga_w, gb_w = sizes
    offs = np.concatenate([[0], np.cumsum(sizes)])
    tn = 512
    glu = matmul_w(xn, w_in, int(offs[0]), glu_w, tm, tn, name="mm_glu")
    q = matmul_w(xn, w_in, int(offs[1]), q_w, tm, tn, name="mm_q")
    k = matmul_w(xn, w_in, int(offs[2]), k_w, tm, min(tn, k_w), name="mm_k")
    v = matmul_w(xn, w_in, int(offs[3]), v_w, tm, min(tn, v_w), name="mm_v")
    qi = matmul_w(xn, w_in, int(offs[4]), qi_w, tm, min(tn, qi_w), name="mm_qi")
    w_kw = jnp.pad(w_in[:, int(offs[5]):int(offs[7])], ((0, 0), (0, LANES - ki_w - wi_w)))
    kw = matmul_w(xn, w_kw, 0, LANES, tm, LANES, name="mm_kw")
    w_g = w_in[:, int(offs[7]):]
    gates = matmul_w(xn, w_g, 0, ga_w + gb_w, tm, tn, name="mm_gates")
    return glu, q, k, v, qi, kw, gates


def kernel(x_prompt, x_sample, cache_k, cache_v, cache_kidx, state_conv, state_ffn, page_table, rel_bias,
           norm_attn, w_in, dw_conv, b_dw_conv, ln_conv_g, ln_conv_b, w_conv_out, w_o, norm_ffn, w_up, dw_ffn,
           b_dw_ffn, w_down, norm_final):
    bsz, seq, dm = x_prompt.shape
    nd, dec_seq, _ = x_sample.shape
    depth, n_pool, page, n_kv, hd = cache_k.shape
    idx_dim = cache_kidx.shape[-1]
    n_pages = page_table.shape[1]
    past = n_pages * page
    width, dconv = dw_conv.shape[1:]
    fwidth = dw_ffn.shape[1]
    f = w_down.shape[1]
    n_heads = w_o.shape[1] // hd
    d_attn = n_heads * hd
    d_kv = n_kv * hd
    n_in = w_in.shape[2]
    idx_heads = (n_in - 2 * dconv - d_attn - 2 * d_kv - idx_dim - 2 * dm) // (idx_dim + 1)
    sizes = (2 * dconv, d_attn, d_kv, d_kv, idx_heads * idx_dim, idx_dim, idx_heads, dm, dm)
    assert sum(sizes) == n_in and depth == 1 and dec_seq == 1 and page == LANES and d_attn == dm

    mp = bsz * seq
    xp = x_prompt.reshape(mp, dm)
    xs = x_sample.reshape(nd, dm)
    bias3, bias_s = bias_tables(rel_bias, page, n_kv)
    tmp = _tiles(mp)
    drop = lambda a: a.reshape(a.shape[1:])
    (norm_attn, w_in, dw_conv, b_dw_conv, ln_conv_g, ln_conv_b, w_conv_out, w_o, norm_ffn, w_up, dw_ffn,
     b_dw_ffn, w_down, state_conv, state_ffn) = map(drop, (
         norm_attn, w_in, dw_conv, b_dw_conv, ln_conv_g, ln_conv_b, w_conv_out, w_o, norm_ffn, w_up, dw_ffn,
         b_dw_ffn, w_down, state_conv, state_ffn))
    kidx_pool = cache_kidx.reshape(n_pool, page, idx_dim)
    k_pool = cache_k.reshape(n_pool, page * n_kv, hd)
    v_pool = cache_v.reshape(n_pool, page * n_kv, hd)

    xn = rmsnorm_rows(xp, norm_attn, BF16, 512)
    glu, q, k, v, qi, kw, gates = _in_projection(xn, w_in, sizes, tmp)
    conv0 = jnp.zeros((bsz, width - 1, dconv), F32)
    conv_out, conv_state_p = conv_branch_prompt(glu, conv0, dw_conv, b_dw_conv, ln_conv_g, ln_conv_b,
                                                w_conv_out, bsz, seq, 256)
    mixed = attn_prompt(qi, kw, q, k, v, conv_out, gates, bias3, bsz, seq, n_kv, idx_dim, idx_heads)
    x2 = matmul_w(mixed, w_o, 0, dm, tmp, 512, res=xp, name="mm_o")
    xn2 = rmsnorm_rows(x2, norm_ffn, BF16, 512)
    u = matmul_w(xn2, w_up, 0, 2 * f, tmp, 512, name="mm_up")
    ffn0 = jnp.zeros((bsz, fwidth - 1, 2 * f), F32)
    act = ffn_act_prompt(u, ffn0, dw_ffn, b_dw_ffn, bsz, seq, 512, 512)
    x3 = matmul_w(act, w_down, 0, dm, tmp, 256, res=x2, name="mm_down")
    y_prompt = rmsnorm_rows(x3, norm_final, F32, 512).reshape(bsz, seq, dm)
    ffn_state_p = u.reshape(bsz, seq, 2 * f)[:, seq - (fwidth - 1):, :]

    xns = rmsnorm_rows(xs, norm_attn, BF16, nd)
    glu_s, q_s, k_s, v_s, qi_s, kw_s, gates_s = _in_projection(xns, w_in, sizes, nd)
    ki_s = kw_s[:, :idx_dim]
    wi_s = kw_s[:, idx_dim:idx_dim + idx_heads]
    sc_prev_t = jnp.swapaxes(state_conv, 0, 1)
    conv_out_s, u_conv_s = conv_branch_step(glu_s, sc_prev_t, dw_conv, b_dw_conv, ln_conv_g, ln_conv_b,
                                            w_conv_out)
    conv_state_s = jnp.concatenate([state_conv[:, 1:], u_conv_s[:, None, :]], axis=1)

    pg = 8
    scores3, sself3 = sample_scores(page_table, qi_s.reshape(nd, idx_heads, idx_dim),
                                    wi_s.reshape(nd, idx_heads, 1), ki_s.reshape(nd, 1, idx_dim), kidx_pool, pg)
    topk_s = min(TOPK_MAX, (past + dec_seq) // 4)
    sel4, selself = sample_select(scores3.reshape(nd, past), sself3.reshape(nd, LANES), topk_s, n_kv)
    group = n_heads // n_kv
    k_rep = jnp.repeat(k_s.reshape(nd, n_kv, hd), group, axis=1)
    v_rep = jnp.repeat(v_s.reshape(nd, n_kv, hd), group, axis=1)
    attn_s = sample_attention(page_table, q_s.reshape(nd, n_heads, hd), k_rep, v_rep,
                              sel4.reshape(nd, n_pages, page * n_kv), selself.reshape(nd, 1, LANES), bias_s,
                              rel_bias[0].reshape(n_heads, 1), k_pool, v_pool, pg, n_kv)
    mixed_s = gated_mix(gates_s, conv_out_s, attn_s.reshape(nd, dm))
    x2s = matmul_w(mixed_s, w_o, 0, dm, nd, 512, res=xs)
    xn2s = rmsnorm_rows(x2s, norm_ffn, BF16, nd)
    u_s = matmul_w(xn2s, w_up, 0, 2 * f, nd, 512)
    sf_prev_t = jnp.swapaxes(state_ffn, 0, 1)
    act_s = ffn_act_step(u_s, sf_prev_t, dw_ffn, b_dw_ffn, 512)
    x3s = matmul_w(act_s, w_down, 0, dm, nd, 256, res=x2s)
    y_sample = rmsnorm_rows(x3s, norm_final, F32, nd).reshape(nd, dec_seq, dm)
    ffn_state_s = jnp.concatenate([state_ffn[:, 1:], u_s[:, None, :]], axis=1)

    return (y_prompt, y_sample,
            k.reshape(1, bsz, seq, n_kv, hd), v.reshape(1, bsz, seq, n_kv, hd),
            kw.reshape(bsz, seq, LANES)[None, :, :, :idx_dim],
            conv_state_p[None], ffn_state_p[None],
            k_s.reshape(1, nd, dec_seq, n_kv, hd), v_s.reshape(1, nd, dec_seq, n_kv, hd),
            ki_s.reshape(1, nd, dec_seq, idx_dim),
            conv_state_s[None], ffn_state_s[None])
```

```python
import functools
import math

import numpy as np
import jax
import jax.numpy as jnp
from jax import lax
from jax.experimental import pallas as pl
from jax.experimental.pallas import tpu as pltpu

F32 = jnp.float32
BF16 = jnp.bfloat16

EPS = 1e-6
TOPK_MAX = 256
N_BUCKETS = 32
MAX_DISTANCE = 128
QB = 128
CK = 256
LANES = 128
NEG = -1e30
LOG2E = math.log2(math.e)
VMEM_LIMIT = 56 * 1024 * 1024


def _cparams(n_axes, vmem=VMEM_LIMIT):
    return pltpu.CompilerParams(dimension_semantics=("arbitrary",) * n_axes, vmem_limit_bytes=vmem)


def _dot_nt(a, b):
    return lax.dot_general(a, b, (((1,), (1,)), ((), ())), preferred_element_type=F32)


def _sigmoid(x):
    return 1.0 / (1.0 + jnp.exp(-x))


def _fold_rows(x, op):
    while x.shape[0] > 8:
        half = x.shape[0] // 2
        x = op(x[:half], x[half:])
    return x


def _rel_bucket_np(dist):
    n = np.maximum(dist, 0)
    max_exact = N_BUCKETS // 2
    nf = np.maximum(n, 1).astype(np.float32)
    large = max_exact + (np.log(nf / np.float32(max_exact)) / np.float32(math.log(MAX_DISTANCE / max_exact))
                         * np.float32(N_BUCKETS - max_exact)).astype(np.int32)
    large = np.minimum(large, N_BUCKETS - 1)
    return np.where(n < max_exact, n, large).astype(np.int32)


def _rms_kernel(x_ref, g_ref, o_ref):
    x = x_ref[...]
    y = x * lax.rsqrt(jnp.mean(x * x, axis=-1, keepdims=True) + EPS) * g_ref[...]
    o_ref[...] = y.astype(o_ref.dtype)


def rmsnorm_rows(x, g, out_dtype, tm):
    m, d = x.shape
    return pl.pallas_call(
        _rms_kernel,
        grid=(m // tm,),
        in_specs=[pl.BlockSpec((tm, d), lambda i: (i, 0)), pl.BlockSpec((1, d), lambda i: (0, 0))],
        out_specs=pl.BlockSpec((tm, d), lambda i: (i, 0)),
        out_shape=jax.ShapeDtypeStruct((m, d), out_dtype),
        compiler_params=_cparams(1),
        name="rmsnorm",
    )(x, g.reshape(1, d))


def _mm_kernel(a_ref, w_ref, o_ref):
    o_ref[...] = jnp.dot(a_ref[...], w_ref[...].astype(BF16), preferred_element_type=F32)


def _mm_res_kernel(a_ref, w_ref, r_ref, o_ref):
    o_ref[...] = r_ref[...] + jnp.dot(a_ref[...], w_ref[...].astype(BF16), preferred_element_type=F32)


def matmul_w(a, w, col0, ncols, tm, tn, res=None, name="matmul"):
    m, k = a.shape
    assert col0 % tn == 0 and ncols % tn == 0 and m % tm == 0
    cb = col0 // tn
    in_specs = [pl.BlockSpec((tm, k), lambda i, j: (i, 0)),
                pl.BlockSpec((k, tn), lambda i, j: (0, j + cb))]
    args = [a, w]
    kern = _mm_kernel
    if res is not None:
        in_specs.append(pl.BlockSpec((tm, tn), lambda i, j: (i, j)))
        args.append(res)
        kern = _mm_res_kernel
    return pl.pallas_call(
        kern,
        grid=(m // tm, ncols // tn),
        in_specs=in_specs,
        out_specs=pl.BlockSpec((tm, tn), lambda i, j: (i, j)),
        out_shape=jax.ShapeDtypeStruct((m, ncols), F32),
        compiler_params=_cparams(2),
        name=name,
    )(*args)


def _bias_kernel(rb_ref, bk3_ref, bks_ref, o3_ref, os_ref, *, n_heads):
    def head(h, carry):
        for t in range(3):
            bk = bk3_ref[t]
            acc = jnp.zeros(bk.shape, F32)
            for b in range(N_BUCKETS):
                acc = jnp.where(bk == b, rb_ref[b, h], acc)
            o3_ref[t, h] = acc * LOG2E
        for t in range(2):
            bk = bks_ref[t]
            acc = jnp.zeros(bk.shape, F32)
            for b in range(N_BUCKETS):
                acc = jnp.where(bk == b, rb_ref[b, h], acc)
            os_ref[t, pl.ds(h, 1), :] = acc
        return carry
    lax.fori_loop(0, n_heads, head, 0)


def bias_tables(rel_bias, page, rep):
    n_heads = rel_bias.shape[1]
    cols = page * rep
    i = np.arange(QB)[None, :]
    k = np.arange(QB)[:, None]
    bk3 = np.stack([_rel_bucket_np(i - k + 2 * QB), _rel_bucket_np(i - k + QB), _rel_bucket_np(i - k)])
    assert (_rel_bucket_np(np.arange(QB + 1, 1 << 20)) == N_BUCKETS - 1).all()
    assert (bk3[0] == N_BUCKETS - 1).all()
    assert page >= QB
    bks = np.stack([np.full((1, cols), N_BUCKETS - 1, np.int32),
                    _rel_bucket_np(page - np.arange(cols) // rep)[None, :]])
    return pl.pallas_call(
        functools.partial(_bias_kernel, n_heads=n_heads),
        in_specs=[pl.BlockSpec(memory_space=pltpu.SMEM),
                  pl.BlockSpec(memory_space=pltpu.VMEM), pl.BlockSpec(memory_space=pltpu.VMEM)],
        out_specs=[pl.BlockSpec(memory_space=pltpu.VMEM), pl.BlockSpec(memory_space=pltpu.VMEM)],
        out_shape=[jax.ShapeDtypeStruct((3, n_heads, QB, QB), F32),
                   jax.ShapeDtypeStruct((2, n_heads, cols), F32)],
    )(rel_bias, jnp.asarray(bk3), jnp.asarray(bks))


def _conv_kernel(glu_ref, prev_ref, dw_ref, bdw_ref, lng_ref, lnb_ref, wout_ref, o_ref, st_ref,
                 ext_ref, h_ref, wbf_ref, *, tt, width, dconv):
    b = pl.program_id(0)
    t = pl.program_id(1)
    pad = 32
    hist = width - 1

    @pl.when((b == 0) & (t == 0))
    def _():
        wbf_ref[...] = wout_ref[...].astype(BF16)

    @pl.when(t == 0)
    def _():
        ext_ref[pl.ds(pad - hist, hist), :] = prev_ref[...]

    @pl.when(t > 0)
    def _():
        ext_ref[pl.ds(0, pad), :] = ext_ref[pl.ds(tt, pad), :]

    glu = glu_ref[...]
    u = glu[:, :dconv] * _sigmoid(glu[:, dconv:])
    ext_ref[pl.ds(pad, tt), :] = u
    st_ref[...] = ext_ref[pl.ds(pad + tt - hist, hist), :]

    for c in range(dconv // LANES):
        cs = slice(c * LANES, (c + 1) * LANES)
        acc = jnp.zeros((tt, LANES), F32) + bdw_ref[:, cs]
        for j in range(width):
            acc = acc + dw_ref[pl.ds(j, 1), cs] * ext_ref[pl.ds(pad - hist + j, tt), cs]
        h_ref[:, cs] = acc

    h = h_ref[...]
    mu = jnp.mean(h, axis=-1, keepdims=True)
    var = jnp.mean(jnp.square(h - mu), axis=-1, keepdims=True)
    y = (h - mu) * lax.rsqrt(var + EPS) * lng_ref[...] + lnb_ref[...]
    y = y * _sigmoid(y)
    o_ref[...] = jnp.dot(y.astype(BF16), wbf_ref[...], preferred_element_type=F32)


def conv_branch_prompt(glu_pre, prev, dw, bdw, lng, lnb, wout, n_seq, seq, tt):
    width, dconv = dw.shape
    dm = wout.shape[1]
    nt = seq // tt
    kern = functools.partial(_conv_kernel, tt=tt, width=width, dconv=dconv)
    return pl.pallas_call(
        kern,
        grid=(n_seq, nt),
        in_specs=[pl.BlockSpec((tt, 2 * dconv), lambda b, t: (b * nt + t, 0)),
                  pl.BlockSpec((None, width - 1, dconv), lambda b, t: (b, 0, 0)),
                  pl.BlockSpec((width, dconv), lambda b, t: (0, 0)),
                  pl.BlockSpec((1, dconv), lambda b, t: (0, 0)),
                  pl.BlockSpec((1, dconv), lambda b, t: (0, 0)),
                  pl.BlockSpec((1, dconv), lambda b, t: (0, 0)),
                  pl.BlockSpec((dconv, dm), lambda b, t: (0, 0))],
        out_specs=[pl.BlockSpec((tt, dm), lambda b, t: (b * nt + t, 0)),
                   pl.BlockSpec((None, width - 1, dconv), lambda b, t: (b, 0, 0))],
        out_shape=[jax.ShapeDtypeStruct((n_seq * seq, dm), F32),
                   jax.ShapeDtypeStruct((n_seq, width - 1, dconv), F32)],
        scratch_shapes=[pltpu.VMEM((32 + tt, dconv), F32), pltpu.VMEM((tt, dconv), F32),
                        pltpu.VMEM((dconv, dm), BF16)],
        compiler_params=_cparams(2),
        name="conv_branch",
    )(glu_pre, prev, dw, bdw.reshape(1, dconv), lng.reshape(1, dconv), lnb.reshape(1, dconv), wout)


def _conv_step_kernel(glu_ref, prev_ref, dw_ref, bdw_ref, lng_ref, lnb_ref, wout_ref, o_ref, u_ref,
                      *, width, dconv):
    glu = glu_ref[...]
    u = glu[:, :dconv] * _sigmoid(glu[:, dconv:])
    u_ref[...] = u
    h = bdw_ref[...] + dw_ref[pl.ds(width - 1, 1), :] * u
    for j in range(width - 1):
        h = h + dw_ref[pl.ds(j, 1), :] * prev_ref[j]
    mu = jnp.mean(h, axis=-1, keepdims=True)
    var = jnp.mean(jnp.square(h - mu), axis=-1, keepdims=True)
    y = (h - mu) * lax.rsqrt(var + EPS) * lng_ref[...] + lnb_ref[...]
    y = y * _sigmoid(y)
    o_ref[...] = jnp.dot(y.astype(BF16), wout_ref[...].astype(BF16), preferred_element_type=F32)


def conv_branch_step(glu_pre, prev_t, dw, bdw, lng, lnb, wout):
    width, dconv = dw.shape
    n = glu_pre.shape[0]
    dm = wout.shape[1]
    kern = functools.partial(_conv_step_kernel, width=width, dconv=dconv)
    return pl.pallas_call(
        kern,
        out_shape=[jax.ShapeDtypeStruct((n, dm), F32), jax.ShapeDtypeStruct((n, dconv), F32)],
        compiler_params=pltpu.CompilerParams(vmem_limit_bytes=VMEM_LIMIT),
    )(glu_pre, prev_t, dw, bdw.reshape(1, dconv), lng.reshape(1, dconv), lnb.reshape(1, dconv), wout)


def _select_threshold(count_gt, row_min, row_max, n_adm, topk, any_fn):
    kf = jnp.float32(topk)
    full = n_adm <= kf
    lo0 = row_min - (1.0 + jnp.abs(row_min))
    hi0 = row_max
    flo0 = jnp.where(full, kf, n_adm)
    fhi0 = jnp.zeros_like(lo0)

    def active_rows(lo, hi, flo):
        mid = 0.5 * lo + 0.5 * hi
        return (flo != kf) & (lo < mid) & (mid < hi)

    def cond(st):
        lo, hi, flo, fhi = st
        return any_fn(active_rows(lo, hi, flo))

    def step(st, interpolate):
        lo, hi, flo, fhi = st
        act = active_rows(lo, hi, flo)
        mid = 0.5 * lo + 0.5 * hi
        if interpolate:
            guess = lo + (hi - lo) * ((flo - kf) / (flo - fhi))
            mid = jnp.where((lo < guess) & (guess < hi), guess, mid)
        c = count_gt(mid)
        up = act & (c >= kf)
        dn = act & (c < kf)
        return (jnp.where(up, mid, lo), jnp.where(dn, mid, hi),
                jnp.where(up, c, flo), jnp.where(dn, c, fhi))

    def body(st):
        return step(step(st, True), False)

    lo, hi, flo, fhi = lax.while_loop(cond, body, (lo0, hi0, flo0, fhi0))
    lo = jnp.where(full, -jnp.inf, lo)
    return lo, hi, flo, fhi


def _attn_prompt_kernel(qi_ref, wi_ref, kw_ref, q_ref, k_ref, v_ref, co_ref, ga_ref, gb_ref, bias_ref,
                        o_ref, kd_ref, kb_ref, vt_ref, wit_ref, qib_ref, sc_ref, sel_ref, qs_ref,
                        m_ref, l_ref, acc_ref,
                        *, seq, n_heads, n_kv, idx_heads, idx_dim, topk, hd):
    j = pl.program_id(1)
    group = n_heads // n_kv
    nck = (j * QB + QB + CK - 1) // CK
    lane = lax.broadcasted_iota(jnp.int32, (1, LANES), 1)
    kf = jnp.float32(topk)

    @pl.when(j == 0)
    def _():
        kw = kw_ref[...]
        kd_ref[0] = jnp.where(lane < idx_dim, kw, 0.0).astype(BF16)
        kd_ref[1] = jnp.where(lane >= idx_dim, pltpu.roll(kw, idx_dim, 1), 0.0).astype(BF16)
        for g in range(n_kv):
            kb_ref[g] = k_ref[:, g * hd:(g + 1) * hd].astype(BF16)
            for c in range(seq // CK):
                vt_ref[g, c] = v_ref[c * CK:(c + 1) * CK, g * hd:(g + 1) * hd].T.astype(BF16)

    wit_ref[...] = (wi_ref[...] * (idx_heads ** -0.5)).T
    qib_ref[...] = (qi_ref[...] * (idx_dim ** -0.5)).astype(BF16)
    qpos = j * QB + lax.broadcasted_iota(jnp.int32, (CK, QB), 1)
    krow = lax.broadcasted_iota(jnp.int32, (CK, QB), 0)
    per_pair = LANES // idx_dim

    def score_chunk(c, carry):
        mn, mx = carry
        k0 = pl.multiple_of(c * CK, CK)
        acc = jnp.zeros((CK, QB), F32)
        for p in range(idx_heads // per_pair):
            rhs = qib_ref[:, p * LANES:(p + 1) * LANES]
            for r in range(per_pair):
                s = _dot_nt(kd_ref[r, pl.ds(k0, CK), :], rhs)
                acc = acc + wit_ref[pl.ds(idx_dim + p * per_pair + r, 1), :] * jnp.maximum(s, 0.0)
        adm = (krow + c * CK) <= qpos
        sc_ref[c] = jnp.where(adm, acc, -jnp.inf)
        mn = jnp.minimum(mn, _fold_rows(jnp.where(adm, acc, jnp.inf), jnp.minimum))
        mx = jnp.maximum(mx, _fold_rows(jnp.where(adm, acc, -jnp.inf), jnp.maximum))
        return mn, mx

    mn8, mx8 = lax.fori_loop(0, nck, score_chunk,
                             (jnp.full((8, QB), jnp.inf, F32), jnp.full((8, QB), -jnp.inf, F32)))
    row_min = jnp.min(mn8, axis=0, keepdims=True)
    row_max = jnp.max(mx8, axis=0, keepdims=True)
    n_adm = (j * QB + 1 + lax.broadcasted_iota(jnp.int32, (1, QB), 1)).astype(F32)

    def count_gt(t):
        def cbody(c, acc):
            return acc + _fold_rows(jnp.where(sc_ref[c] > t, 1.0, 0.0), jnp.add)
        part = lax.fori_loop(0, nck, cbody, jnp.zeros((8, QB), F32))
        return jnp.sum(part, axis=0, keepdims=True)

    def any_fn(mask):
        return jnp.max(jnp.where(mask, 1.0, 0.0)) > 0.0

    lo, hi, flo, fhi = _select_threshold(count_gt, row_min, row_max, n_adm, topk, any_fn)
    tie = flo != kf

    def sel_chunk(c, carry):
        sel_ref[c] = jnp.where(sc_ref[c] > lo, 1.0, 0.0)
        return carry
    lax.fori_loop(0, nck, sel_chunk, 0)

    @pl.when(any_fn(tie))
    def _():
        need = kf - fhi
        lower = (lax.broadcasted_iota(jnp.int32, (CK, CK), 1)
                 < lax.broadcasted_iota(jnp.int32, (CK, CK), 0)).astype(BF16)

        def tie_chunk(c, before):
            s = sc_ref[c]
            eq = s == hi
            eqf = jnp.where(eq, 1.0, 0.0)
            rank = before + jnp.dot(lower, eqf.astype(BF16), preferred_element_type=F32)
            keep = (s > hi) | (eq & (rank < need))
            sel_ref[c] = jnp.where(tie, jnp.where(keep, 1.0, 0.0), sel_ref[c])
            return before + jnp.sum(eqf, axis=0, keepdims=True)
        lax.fori_loop(0, nck, tie_chunk, jnp.zeros((1, QB), F32))

    scale2 = hd ** -0.5 * LOG2E
    for h in range(n_heads):
        qs_ref[h] = q_ref[:, h * hd:(h + 1) * hd].astype(BF16)
    m_ref[...] = jnp.full(m_ref.shape, NEG, F32)
    l_ref[...] = jnp.zeros(l_ref.shape, F32)
    acc_ref[...] = jnp.zeros(acc_ref.shape, F32)

    def att_chunk(c, carry):
        k0 = pl.multiple_of(c * CK, CK)
        msk = jnp.concatenate([sel_ref[c]] * group, axis=1) > 0.0
        tis = [jnp.clip(c * (CK // QB) + s - j + 2, 0, 2) for s in range(CK // QB)]
        for g in range(n_kv):
            kc = kb_ref[g, pl.ds(k0, CK), :]
            qg = qs_ref[pl.ds(g * group, group)].reshape(group * QB, hd)
            st = _dot_nt(kc, qg) * scale2
            rows = [jnp.concatenate([bias_ref[ti, g * group + hh] for hh in range(group)], axis=1)
                    for ti in tis]
            lg = jnp.where(msk, st + jnp.concatenate(rows, axis=0), -jnp.inf)
            m_old = m_ref[g]
            m_new = jnp.maximum(m_old, jnp.max(_fold_rows(lg, jnp.maximum), axis=0, keepdims=True))
            alpha = jnp.exp2(m_old - m_new)
            p = jnp.exp2(lg - m_new)
            l_ref[g] = alpha * l_ref[g] + jnp.sum(_fold_rows(p, jnp.add), axis=0, keepdims=True)
            pv = jnp.dot(vt_ref[g, c], p.astype(BF16), preferred_element_type=F32)
            acc_ref[g] = alpha * acc_ref[g] + pv
            m_ref[g] = m_new
        return carry
    lax.fori_loop(0, nck, att_chunk, 0)

    for g in range(n_kv):
        ot = acc_ref[g] / l_ref[g]
        for hh in range(group):
            cs = slice((g * group + hh) * hd, (g * group + hh + 1) * hd)
            o = ot[:, hh * QB:(hh + 1) * QB].T
            mixed = _sigmoid(ga_ref[:, cs]) * co_ref[:, cs] + _sigmoid(gb_ref[:, cs]) * o
            o_ref[:, cs] = mixed.astype(o_ref.dtype)


def attn_prompt(qi, kw, q, k, v, conv_out, gates, bias3, n_seq, seq, n_kv, idx_dim, idx_heads):
    m, dm = q.shape
    hd = k.shape[1] // n_kv
    n_heads = dm // hd
    group = n_heads // n_kv
    nb = seq // QB
    topk = min(TOPK_MAX, seq // 4)
    assert seq % CK == 0 and LANES % idx_dim == 0 and kw.shape[1] == LANES
    kern = functools.partial(_attn_prompt_kernel, seq=seq, n_heads=n_heads, n_kv=n_kv, idx_heads=idx_heads,
                             idx_dim=idx_dim, topk=topk, hd=hd)
    row = lambda b, j: (b * nb + j, 0)
    return pl.pallas_call(
        kern,
        grid=(n_seq, nb),
        in_specs=[pl.BlockSpec((QB, idx_heads * idx_dim), row),
                  pl.BlockSpec((QB, LANES), row),
                  pl.BlockSpec((seq, LANES), lambda b, j: (b, 0)),
                  pl.BlockSpec((QB, dm), row),
                  pl.BlockSpec((seq, n_kv * hd), lambda b, j: (b, 0)),
                  pl.BlockSpec((seq, n_kv * hd), lambda b, j: (b, 0)),
                  pl.BlockSpec((QB, dm), row),
                  pl.BlockSpec((QB, dm), lambda b, j: (b * nb + j, 0)),
                  pl.BlockSpec((QB, dm), lambda b, j: (b * nb + j, 1)),
                  pl.BlockSpec((3, n_heads, QB, QB), lambda b, j: (0, 0, 0, 0))],
        out_specs=pl.BlockSpec((QB, dm), row),
        out_shape=jax.ShapeDtypeStruct((m, dm), BF16),
        scratch_shapes=[pltpu.VMEM((2, seq, LANES), BF16),
                        pltpu.VMEM((n_kv, seq, hd), BF16),
                        pltpu.VMEM((n_kv, seq // CK, hd, CK), BF16),
                        pltpu.VMEM((LANES, QB), F32),
                        pltpu.VMEM((QB, idx_heads * idx_dim), BF16),
                        pltpu.VMEM((seq // CK, CK, QB), F32),
                        pltpu.VMEM((seq // CK, CK, QB), F32),
                        pltpu.VMEM((n_heads, QB, hd), BF16),
                        pltpu.VMEM((n_kv, 1, group * QB), F32),
                        pltpu.VMEM((n_kv, 1, group * QB), F32),
                        pltpu.VMEM((n_kv, hd, group * QB), F32)],
        compiler_params=_cparams(2),
        name="attn_prompt",
    )(qi, kw, kw, q, k, v, conv_out, gates, gates, bias3)


def _sample_score_kernel(pt_ref, qi_ref, wi_ref, kn_ref, *rest, pg, idx_heads, idx_dim):
    pages = rest[:pg]
    o_ref, self_ref, kcat_ref = rest[pg:]
    page = pages[0].shape[0]
    qi = qi_ref[...] * (idx_dim ** -0.5)
    wi = wi_ref[...] * (idx_heads ** -0.5)
    qb = qi.astype(BF16)
    for i in range(pg):
        kcat_ref[i * page:(i + 1) * page, :] = pages[i][...].astype(BF16)
    s = _dot_nt(qb, kcat_ref[...])
    o_ref[...] = jnp.sum(wi * jnp.maximum(s, 0.0), axis=0, keepdims=True)

    @pl.when(pl.program_id(1) == 0)
    def _():
        kn = kn_ref[...].astype(BF16).astype(F32)
        s = jnp.sum(qb.astype(F32) * kn, axis=1, keepdims=True)
        sself = jnp.sum(wi * jnp.maximum(s, 0.0), axis=0, keepdims=True)
        self_ref[...] = jnp.broadcast_to(sself, self_ref.shape)


def sample_scores(page_table, qi3, wi3, ki_new3, cache_kidx, pg):
    n, n_pages = page_table.shape
    idx_heads, idx_dim = qi3.shape[1:]
    page = cache_kidx.shape[1]
    kern = functools.partial(_sample_score_kernel, pg=pg, idx_heads=idx_heads, idx_dim=idx_dim)
    page_specs = [pl.BlockSpec((None, page, idx_dim), (lambda b, p, pt, i=i: (pt[b, p * pg + i], 0, 0)))
                  for i in range(pg)]
    grid_spec = pltpu.PrefetchScalarGridSpec(
        num_scalar_prefetch=1,
        grid=(n, n_pages // pg),
        in_specs=[pl.BlockSpec((None, idx_heads, idx_dim), lambda b, p, pt: (b, 0, 0)),
                  pl.BlockSpec((None, idx_heads, 1), lambda b, p, pt: (b, 0, 0)),
                  pl.BlockSpec((None, 1, idx_dim), lambda b, p, pt: (b, 0, 0))] + page_specs,
        out_specs=[pl.BlockSpec((None, None, 1, pg * page), lambda b, p, pt: (b, p, 0, 0)),
                   pl.BlockSpec((None, 1, LANES), lambda b, p, pt: (b, 0, 0))],
        scratch_shapes=[pltpu.VMEM((pg * page, idx_dim), BF16)],
    )
    return pl.pallas_call(
        kern,
        grid_spec=grid_spec,
        out_shape=[jax.ShapeDtypeStruct((n, n_pages // pg, 1, pg * page), F32),
                   jax.ShapeDtypeStruct((n, 1, LANES), F32)],
        compiler_params=_cparams(2),
        name="sample_scores",
    )(page_table, qi3, wi3, ki_new3, *([cache_kidx] * pg))


def _sample_select_kernel(sc_ref, self_ref, sel4_ref, selself_ref, sel_ref, *, topk, past, rep):
    sc = sc_ref[...]
    sself = self_ref[:, 0:1]
    n = sc.shape[0]
    kf = jnp.float32(topk)
    row_min = jnp.minimum(jnp.min(sc, axis=1, keepdims=True), sself)
    row_max = jnp.maximum(jnp.max(sc, axis=1, keepdims=True), sself)
    n_adm = jnp.full((n, 1), past + 1, F32)

    def count_gt(t):
        return (jnp.sum(jnp.where(sc > t, 1.0, 0.0), axis=1, keepdims=True)
                + jnp.where(sself > t, 1.0, 0.0))

    def any_fn(mask):
        return jnp.max(jnp.where(mask, 1.0, 0.0)) > 0.0

    lo, hi, flo, fhi = _select_threshold(count_gt, row_min, row_max, n_adm, topk, any_fn)
    tie = flo != kf
    sel_ref[...] = jnp.where(sc > lo, 1.0, 0.0)
    selself_ref[...] = jnp.broadcast_to(jnp.where(sself > lo, 1.0, 0.0), selself_ref.shape)

    @pl.when(any_fn(tie))
    def _():
        need = kf - fhi
        blk = 512
        tri = (lax.broadcasted_iota(jnp.int32, (blk, blk), 0)
               < lax.broadcasted_iota(jnp.int32, (blk, blk), 1)).astype(BF16)
        before = jnp.zeros((n, 1), F32)
        for c in range(past // blk):
            s = sc_ref[:, c * blk:(c + 1) * blk]
            eq = s == hi
            rank = before + jnp.dot(jnp.where(eq, 1.0, 0.0).astype(BF16), tri, preferred_element_type=F32)
            keep = (s > hi) | (eq & (rank < need))
            sel_ref[:, c * blk:(c + 1) * blk] = jnp.where(tie, jnp.where(keep, 1.0, 0.0),
                                                          sel_ref[:, c * blk:(c + 1) * blk])
            before = before + jnp.sum(jnp.where(eq, 1.0, 0.0), axis=1, keepdims=True)
        keep_self = (sself > hi) | ((sself == hi) & (before < need))
        selself_ref[...] = jnp.broadcast_to(
            jnp.where(tie, jnp.where(keep_self, 1.0, 0.0), jnp.where(sself > lo, 1.0, 0.0)), selself_ref.shape)

    blk = 512
    row_lo = lax.broadcasted_iota(jnp.int32, (blk, blk * rep), 0) * rep
    col = lax.broadcasted_iota(jnp.int32, (blk, blk * rep), 1)
    spread = jnp.where((col >= row_lo) & (col < row_lo + rep), 1.0, 0.0).astype(BF16)
    for c in range(past // blk):
        sel4_ref[:, c * blk * rep:(c + 1) * blk * rep] = jnp.dot(
            sel_ref[:, c * blk:(c + 1) * blk].astype(BF16), spread, preferred_element_type=F32)


def sample_select(scores, sself, topk, rep):
    n, past = scores.shape
    kern = functools.partial(_sample_select_kernel, topk=topk, past=past, rep=rep)
    return pl.pallas_call(
        kern,
        out_shape=[jax.ShapeDtypeStruct((n, past * rep), F32), jax.ShapeDtypeStruct((n, LANES), F32)],
        scratch_shapes=[pltpu.VMEM((n, past), F32)],
        compiler_params=pltpu.CompilerParams(vmem_limit_bytes=VMEM_LIMIT),
        name="sample_select",
    )(scores, sself)


def _sample_attn_kernel(pt_ref, q_ref, kn_ref, vn_ref, sel_ref, selself_ref, bias_ref, rb0_ref, own_ref, *rest,
                        pg, n_heads, n_kv, hd):
    kpages = rest[:pg]
    vpages = rest[pg:2 * pg]
    o_ref = rest[2 * pg]
    kcat_ref, vcat_ref, m_ref, l_ref, acc_ref = rest[2 * pg + 1:]
    p = pl.program_id(1)
    n_steps = pl.num_programs(1)
    scale = hd ** -0.5
    rows = kpages[0].shape[0]

    @pl.when(p == 0)
    def _():
        m_ref[...] = jnp.full(m_ref.shape, NEG, F32)
        l_ref[...] = jnp.zeros(l_ref.shape, F32)
        acc_ref[...] = jnp.zeros(acc_ref.shape, F32)

    for i in range(pg):
        kcat_ref[i * rows:(i + 1) * rows, :] = kpages[i][...].astype(BF16)
        vcat_ref[i * rows:(i + 1) * rows, :] = vpages[i][...].astype(BF16)
    qb = q_ref[...].astype(BF16)
    last = p == n_steps - 1
    bias = jnp.concatenate([bias_ref[0]] * (pg - 1) + [jnp.where(last, bias_ref[1], bias_ref[0])], axis=1)
    msk = (own_ref[...] > 0.0) & (sel_ref[...] > 0.0)
    lg = jnp.where(msk, _dot_nt(qb, kcat_ref[...]) * scale + bias, -jnp.inf)
    m_old = m_ref[...]
    m_new = jnp.maximum(m_old, jnp.max(lg, axis=-1, keepdims=True))
    alpha = jnp.exp(m_old - m_new)
    pr = jnp.exp(lg - m_new)
    l_new = alpha * l_ref[...] + jnp.sum(pr, axis=-1, keepdims=True)
    acc = alpha * acc_ref[...] + jnp.dot(pr.astype(BF16), vcat_ref[...], preferred_element_type=F32)
    m_ref[...] = m_new
    l_ref[...] = l_new
    acc_ref[...] = acc

    @pl.when(last)
    def _():
        kn = kn_ref[...].astype(BF16).astype(F32)
        vn = vn_ref[...].astype(BF16).astype(F32)
        ls = jnp.sum(qb.astype(F32) * kn, axis=-1, keepdims=True) * scale + rb0_ref[...]
        on = selself_ref[:, 0:1] > 0.0
        ls = jnp.where(on, ls, NEG)
        m_f = jnp.maximum(m_new, ls)
        a2 = jnp.exp(m_new - m_f)
        ps = jnp.where(on, jnp.exp(ls - m_f), 0.0)
        l_f = a2 * l_new + ps
        acc_f = a2 * acc + ps.astype(BF16).astype(F32) * vn
        o_ref[...] = acc_f / l_f


def sample_attention(page_table, q3, k_rep, v_rep, sel4, selself3, bias_s, rb0, cache_k, cache_v, pg, n_kv):
    n, n_pages = page_table.shape
    n_heads, hd = q3.shape[1:]
    rows = cache_k.shape[1]
    kern = functools.partial(_sample_attn_kernel, pg=pg, n_heads=n_heads, n_kv=n_kv, hd=hd)
    own = (np.arange(pg * rows)[None, :] % n_kv
           == np.arange(n_heads)[:, None] // (n_heads // n_kv)).astype(np.float32)
    kspecs = [pl.BlockSpec((None, rows, hd), (lambda b, p, pt, i=i: (pt[b, p * pg + i], 0, 0)))
              for i in range(pg)]
    per_seq = lambda b, p, pt: (b, 0, 0)
    grid_spec = pltpu.PrefetchScalarGridSpec(
        num_scalar_prefetch=1,
        grid=(n, n_pages // pg),
        in_specs=[pl.BlockSpec((None, n_heads, hd), per_seq),
                  pl.BlockSpec((None, n_heads, hd), per_seq),
                  pl.BlockSpec((None, n_heads, hd), per_seq),
                  pl.BlockSpec((None, None, 1, pg * rows), lambda b, p, pt: (b, p, 0, 0)),
                  pl.BlockSpec((None, 1, LANES), per_seq),
                  pl.BlockSpec((2, n_heads, rows), lambda b, p, pt: (0, 0, 0)),
                  pl.BlockSpec((n_heads, 1), lambda b, p, pt: (0, 0)),
                  pl.BlockSpec((n_heads, pg * rows), lambda b, p, pt: (0, 0))] + kspecs + kspecs,
        out_specs=pl.BlockSpec((None, n_heads, hd), per_seq),
        scratch_shapes=[pltpu.VMEM((pg * rows, hd), BF16), pltpu.VMEM((pg * rows, hd), BF16),
                        pltpu.VMEM((n_heads, 1), F32), pltpu.VMEM((n_heads, 1), F32),
                        pltpu.VMEM((n_heads, hd), F32)],
    )
    return pl.pallas_call(
        kern,
        grid_spec=grid_spec,
        out_shape=jax.ShapeDtypeStruct((n, n_heads, hd), F32),
        compiler_params=_cparams(2),
        name="sample_attn",
    )(page_table, q3, k_rep, v_rep, sel4, selself3, bias_s, rb0, jnp.asarray(own), *([cache_k] * pg),
      *([cache_v] * pg))


def _mix_kernel(ga_ref, gb_ref, co_ref, at_ref, o_ref):
    o_ref[...] = (_sigmoid(ga_ref[...]) * co_ref[...] + _sigmoid(gb_ref[...]) * at_ref[...]).astype(o_ref.dtype)


def gated_mix(gates, conv_out, attn):
    n, dm = conv_out.shape
    return pl.pallas_call(
        _mix_kernel,
        grid=(1,),
        in_specs=[pl.BlockSpec((n, dm), lambda i: (0, 0)), pl.BlockSpec((n, dm), lambda i: (0, 1)),
                  pl.BlockSpec((n, dm), lambda i: (0, 0)), pl.BlockSpec((n, dm), lambda i: (0, 0))],
        out_specs=pl.BlockSpec((n, dm), lambda i: (0, 0)),
        out_shape=jax.ShapeDtypeStruct((n, dm), BF16),
    )(gates, gates, conv_out, attn)


def _ffn_up_kernel(x_ref, xh_ref, wg_ref, wv_ref, dg_ref, dv_ref, bg_ref, bv_ref, pg_ref, pv_ref,
                   o_ref, sg_ref, sv_ref, eg_ref, ev_ref, *, tm, width, tiles_per_seq):
    hist = width - 1
    pad = xh_ref.shape[0]
    first = pl.program_id(0) % tiles_per_seq == 0

    def branch(w_ref, d_ref, b_ref, prev_ref, e_ref, s_ref):
        w = w_ref[...].astype(BF16)
        e_ref[pl.ds(0, pad), :] = jnp.dot(xh_ref[...], w, preferred_element_type=F32)
        e_ref[pl.ds(pad, tm), :] = jnp.dot(x_ref[...], w, preferred_element_type=F32)

        @pl.when(first)
        def _():
            e_ref[pl.ds(pad - hist, hist), :] = prev_ref[...]
        s_ref[...] = e_ref[pl.ds(pad + tm - hist, hist), :]
        acc = b_ref[...] + d_ref[pl.ds(hist, 1), :] * e_ref[pl.ds(pad, tm), :]
        for j in range(hist):
            acc = acc + d_ref[pl.ds(j, 1), :] * e_ref[pl.ds(pad - hist + j, tm), :]
        return acc

    g = branch(wg_ref, dg_ref, bg_ref, pg_ref, eg_ref, sg_ref)
    v = branch(wv_ref, dv_ref, bv_ref, pv_ref, ev_ref, sv_ref)
    o_ref[...] = (g * _sigmoid(g) * v).astype(o_ref.dtype)


def ffn_up_act(xn, w_up, prev, dw, bdw, n_seq, seq, tm, tn):
    m, k = xn.shape
    f2 = w_up.shape[1]
    f = f2 // 2
    width = dw.shape[0]
    ncb = f // tn
    pad = 16
    tps = seq // tm
    assert seq % tm == 0 and tm % pad == 0 and f % tn == 0 and width - 1 <= pad
    kern = functools.partial(_ffn_up_kernel, tm=tm, width=width, tiles_per_seq=tps)
    hb = tm // pad
    st_shape = jax.ShapeDtypeStruct((n_seq, width - 1, f), F32)
    act, sg, sv = pl.pallas_call(
        kern,
        grid=(m // tm, ncb),
        in_specs=[pl.BlockSpec((tm, k), lambda i, c: (i, 0)),
                  pl.BlockSpec((pad, k), lambda i, c: (jnp.maximum(i * hb - 1, 0), 0)),
                  pl.BlockSpec((k, tn), lambda i, c: (0, c)),
                  pl.BlockSpec((k, tn), lambda i, c: (0, c + ncb)),
                  pl.BlockSpec((width, tn), lambda i, c: (0, c)),
                  pl.BlockSpec((width, tn), lambda i, c: (0, c + ncb)),
                  pl.BlockSpec((1, tn), lambda i, c: (0, c)),
                  pl.BlockSpec((1, tn), lambda i, c: (0, c + ncb)),
                  pl.BlockSpec((None, width - 1, tn), lambda i, c: (i // tps, 0, c)),
                  pl.BlockSpec((None, width - 1, tn), lambda i, c: (i // tps, 0, c + ncb))],
        out_specs=[pl.BlockSpec((tm, tn), lambda i, c: (i, c)),
                   pl.BlockSpec((None, width - 1, tn), lambda i, c: (i // tps, 0, c)),
                   pl.BlockSpec((None, width - 1, tn), lambda i, c: (i // tps, 0, c))],
        out_shape=[jax.ShapeDtypeStruct((m, f), BF16), st_shape, st_shape],
        scratch_shapes=[pltpu.VMEM((pad + tm, tn), F32), pltpu.VMEM((pad + tm, tn), F32)],
        compiler_params=_cparams(2),
        name="ffn_up_act",
    )(xn, xn, w_up, w_up, dw, dw, bdw.reshape(1, f2), bdw.reshape(1, f2), prev, prev)
    return act, jnp.concatenate([sg, sv], axis=-1)


def _ffn_act_step_kernel(ug_ref, uv_ref, pg_ref, pv_ref, wg_ref, wv_ref, bg_ref, bv_ref, o_ref, *, width):
    def conv(u_ref, prev_ref, w_ref, b_ref):
        acc = b_ref[...] + w_ref[pl.ds(width - 1, 1), :] * u_ref[...]
        for j in range(width - 1):
            acc = acc + w_ref[pl.ds(j, 1), :] * prev_ref[j]
        return acc
    g = conv(ug_ref, pg_ref, wg_ref, bg_ref)
    v = conv(uv_ref, pv_ref, wv_ref, bv_ref)
    o_ref[...] = (g * _sigmoid(g) * v).astype(o_ref.dtype)


def ffn_act_step(u, prev_t, dw, bdw, tc):
    n, f2 = u.shape
    f = f2 // 2
    width = dw.shape[0]
    ncb = f // tc
    kern = functools.partial(_ffn_act_step_kernel, width=width)
    return pl.pallas_call(
        kern,
        grid=(ncb,),
        in_specs=[pl.BlockSpec((n, tc), lambda c: (0, c)),
                  pl.BlockSpec((n, tc), lambda c: (0, c + ncb)),
                  pl.BlockSpec((width - 1, n, tc), lambda c: (0, 0, c)),
                  pl.BlockSpec((width - 1, n, tc), lambda c: (0, 0, c + ncb)),
                  pl.BlockSpec((width, tc), lambda c: (0, c)),
                  pl.BlockSpec((width, tc), lambda c: (0, c + ncb)),
                  pl.BlockSpec((1, tc), lambda c: (0, c)),
                  pl.BlockSpec((1, tc), lambda c: (0, c + ncb))],
        out_specs=pl.BlockSpec((n, tc), lambda c: (0, c)),
        out_shape=jax.ShapeDtypeStruct((n, f), BF16),
        compiler_params=_cparams(1),
    )(u, u, prev_t, prev_t, dw, dw, bdw.reshape(1, f2), bdw.reshape(1, f2))


def _tiles(m):
    tm = 1024 if m % 1024 == 0 else m
    return tm


def _in_projection(xn, w_in, sizes, tm):
    glu_w, q_w, k_w, v_w, qi_w, ki_w, wi_w, ga_w, gb_w = sizes
    offs = np.concatenate([[0], np.cumsum(sizes)])
    tn = 512
    glu = matmul_w(xn, w_in, int(offs[0]), glu_w, tm, tn, name="mm_glu")
    q = matmul_w(xn, w_in, int(offs[1]), q_w, tm, tn, name="mm_q")
    k = matmul_w(xn, w_in, int(offs[2]), k_w, tm, min(tn, k_w), name="mm_k")
    v = matmul_w(xn, w_in, int(offs[3]), v_w, tm, min(tn, v_w), name="mm_v")
    qi = matmul_w(xn, w_in, int(offs[4]), qi_w, tm, min(tn, qi_w), name="mm_qi")
    w_kw = jnp.pad(w_in[:, int(offs[5]):int(offs[7])], ((0, 0), (0, LANES - ki_w - wi_w)))
    kw = matmul_w(xn, w_kw, 0, LANES, tm, LANES, name="mm_kw")
    w_g = w_in[:, int(offs[7]):]
    gates = matmul_w(xn, w_g, 0, ga_w + gb_w, tm, tn, name="mm_gates")
    return glu, q, k, v, qi, kw, gates


def kernel(x_prompt, x_sample, cache_k, cache_v, cache_kidx, state_conv, state_ffn, page_table, rel_bias,
           norm_attn, w_in, dw_conv, b_dw_conv, ln_conv_g, ln_conv_b, w_conv_out, w_o, norm_ffn, w_up, dw_ffn,
           b_dw_ffn, w_down, norm_final):
    bsz, seq, dm = x_prompt.shape
    nd, dec_seq, _ = x_sample.shape
    depth, n_pool, page, n_kv, hd = cache_k.shape
    idx_dim = cache_kidx.shape[-1]
    n_pages = page_table.shape[1]
    past = n_pages * page
    width, dconv = dw_conv.shape[1:]
    fwidth = dw_ffn.shape[1]
    f = w_down.shape[1]
    n_heads = w_o.shape[1] // hd
    d_attn = n_heads * hd
    d_kv = n_kv * hd
    n_in = w_in.shape[2]
    idx_heads = (n_in - 2 * dconv - d_attn - 2 * d_kv - idx_dim - 2 * dm) // (idx_dim + 1)
    sizes = (2 * dconv, d_attn, d_kv, d_kv, idx_heads * idx_dim, idx_dim, idx_heads, dm, dm)
    assert sum(sizes) == n_in and depth == 1 and dec_seq == 1 and page == LANES and d_attn == dm

    mp = bsz * seq
    xp = x_prompt.reshape(mp, dm)
    xs = x_sample.reshape(nd, dm)
    bias3, bias_s = bias_tables(rel_bias, page, n_kv)
    tmp = _tiles(mp)
    drop = lambda a: a.reshape(a.shape[1:])
    (norm_attn, w_in, dw_conv, b_dw_conv, ln_conv_g, ln_conv_b, w_conv_out, w_o, norm_ffn, w_up, dw_ffn,
     b_dw_ffn, w_down, state_conv, state_ffn) = map(drop, (
         norm_attn, w_in, dw_conv, b_dw_conv, ln_conv_g, ln_conv_b, w_conv_out, w_o, norm_ffn, w_up, dw_ffn,
         b_dw_ffn, w_down, state_conv, state_ffn))
    kidx_pool = cache_kidx.reshape(n_pool, page, idx_dim)
    k_pool = cache_k.reshape(n_pool, page * n_kv, hd)
    v_pool = cache_v.reshape(n_pool, page * n_kv, hd)

    xn = rmsnorm_rows(xp, norm_attn, BF16, 512)
    glu, q, k, v, qi, kw, gates = _in_projection(xn, w_in, sizes, tmp)
    conv0 = jnp.zeros((bsz, width - 1, dconv), F32)
    conv_out, conv_state_p = conv_branch_prompt(glu, conv0, dw_conv, b_dw_conv, ln_conv_g, ln_conv_b,
                                                w_conv_out, bsz, seq, 256)
    mixed = attn_prompt(qi, kw, q, k, v, conv_out, gates, bias3, bsz, seq, n_kv, idx_dim, idx_heads)
    x2 = matmul_w(mixed, w_o, 0, dm, tmp, 512, res=xp, name="mm_o")
    xn2 = rmsnorm_rows(x2, norm_ffn, BF16, 512)
    ffn0 = jnp.zeros((bsz, fwidth - 1, 2 * f), F32)
    act, ffn_state_p = ffn_up_act(xn2, w_up, ffn0, dw_ffn, b_dw_ffn, bsz, seq, min(tmp, seq // 2), 512)
    x3 = matmul_w(act, w_down, 0, dm, tmp, 256, res=x2, name="mm_down")
    y_prompt = rmsnorm_rows(x3, norm_final, F32, 512).reshape(bsz, seq, dm)

    xns = rmsnorm_rows(xs, norm_attn, BF16, nd)
    glu_s, q_s, k_s, v_s, qi_s, kw_s, gates_s = _in_projection(xns, w_in, sizes, nd)
    ki_s = kw_s[:, :idx_dim]
    wi_s = kw_s[:, idx_dim:idx_dim + idx_heads]
    sc_prev_t = jnp.swapaxes(state_conv, 0, 1)
    conv_out_s, u_conv_s = conv_branch_step(glu_s, sc_prev_t, dw_conv, b_dw_conv, ln_conv_g, ln_conv_b,
                                            w_conv_out)
    conv_state_s = jnp.concatenate([state_conv[:, 1:], u_conv_s[:, None, :]], axis=1)

    pg = 16 if n_pages % 16 == 0 else 8
    scores3, sself3 = sample_scores(page_table, qi_s.reshape(nd, idx_heads, idx_dim),
                                    wi_s.reshape(nd, idx_heads, 1), ki_s.reshape(nd, 1, idx_dim), kidx_pool, pg)
    topk_s = min(TOPK_MAX, (past + dec_seq) // 4)
    sel4, selself = sample_select(scores3.reshape(nd, past), sself3.reshape(nd, LANES), topk_s, n_kv)
    group = n_heads // n_kv
    k_rep = jnp.repeat(k_s.reshape(nd, n_kv, hd), group, axis=1)
    v_rep = jnp.repeat(v_s.reshape(nd, n_kv, hd), group, axis=1)
    attn_s = sample_attention(page_table, q_s.reshape(nd, n_heads, hd), k_rep, v_rep,
                              sel4.reshape(nd, n_pages // pg, 1, pg * page * n_kv), selself.reshape(nd, 1, LANES),
                              bias_s,
                              rel_bias[0].reshape(n_heads, 1), k_pool, v_pool, pg, n_kv)
    mixed_s = gated_mix(gates_s, conv_out_s, attn_s.reshape(nd, dm))
    x2s = matmul_w(mixed_s, w_o, 0, dm, nd, 512, res=xs)
    xn2s = rmsnorm_rows(x2s, norm_ffn, BF16, nd)
    u_s = matmul_w(xn2s, w_up, 0, 2 * f, nd, 512)
    sf_prev_t = jnp.swapaxes(state_ffn, 0, 1)
    act_s = ffn_act_step(u_s, sf_prev_t, dw_ffn, b_dw_ffn, 512)
    x3s = matmul_w(act_s, w_down, 0, dm, nd, 256, res=x2s)
    y_sample = rmsnorm_rows(x3s, norm_final, F32, nd).reshape(nd, dec_seq, dm)
    ffn_state_s = jnp.concatenate([state_ffn[:, 1:], u_s[:, None, :]], axis=1)

    return (y_prompt, y_sample,
            k.reshape(1, bsz, seq, n_kv, hd), v.reshape(1, bsz, seq, n_kv, hd),
            kw.reshape(bsz, seq, LANES)[None, :, :, :idx_dim],
            conv_state_p[None], ffn_state_p[None],
            k_s.reshape(1, nd, dec_seq, n_kv, hd), v_s.reshape(1, nd, dec_seq, n_kv, hd),
            ki_s.reshape(1, nd, dec_seq, idx_dim),
            conv_state_s[None], ffn_state_s[None])
```

```python
import functools
import math

import numpy as np
import jax
import jax.numpy as jnp
from jax import lax
from jax.experimental import pallas as pl
from jax.experimental.pallas import tpu as pltpu

F32 = jnp.float32
BF16 = jnp.bfloat16

EPS = 1e-6
TOPK_MAX = 256
N_BUCKETS = 32
MAX_DISTANCE = 128
QB = 128
CK = 256
LANES = 128
NEG = -1e30
LOG2E = math.log2(math.e)
VMEM_LIMIT = 56 * 1024 * 1024


def _cparams(n_axes, vmem=VMEM_LIMIT):
    return pltpu.CompilerParams(dimension_semantics=("arbitrary",) * n_axes, vmem_limit_bytes=vmem)


def _dot_nt(a, b):
    return lax.dot_general(a, b, (((1,), (1,)), ((), ())), preferred_element_type=F32)


def _sigmoid(x):
    return 1.0 / (1.0 + jnp.exp(-x))


def _fold_rows(x, op):
    while x.shape[0] > 8:
        half = x.shape[0] // 2
        x = op(x[:half], x[half:])
    return x


def _rel_bucket_np(dist):
    n = np.maximum(dist, 0)
    max_exact = N_BUCKETS // 2
    nf = np.maximum(n, 1).astype(np.float32)
    large = max_exact + (np.log(nf / np.float32(max_exact)) / np.float32(math.log(MAX_DISTANCE / max_exact))
                         * np.float32(N_BUCKETS - max_exact)).astype(np.int32)
    large = np.minimum(large, N_BUCKETS - 1)
    return np.where(n < max_exact, n, large).astype(np.int32)


def _rms_kernel(x_ref, g_ref, o_ref):
    x = x_ref[...]
    y = x * lax.rsqrt(jnp.mean(x * x, axis=-1, keepdims=True) + EPS) * g_ref[...]
    o_ref[...] = y.astype(o_ref.dtype)


def rmsnorm_rows(x, g, out_dtype, tm):
    m, d = x.shape
    return pl.pallas_call(
        _rms_kernel,
        grid=(m // tm,),
        in_specs=[pl.BlockSpec((tm, d), lambda i: (i, 0)), pl.BlockSpec((1, d), lambda i: (0, 0))],
        out_specs=pl.BlockSpec((tm, d), lambda i: (i, 0)),
        out_shape=jax.ShapeDtypeStruct((m, d), out_dtype),
        compiler_params=_cparams(1),
        name="rmsnorm",
    )(x, g.reshape(1, d))


def _mm_kernel(a_ref, w_ref, o_ref):
    o_ref[...] = jnp.dot(a_ref[...], w_ref[...].astype(BF16), preferred_element_type=F32)


def _mm_res_kernel(a_ref, w_ref, r_ref, o_ref):
    o_ref[...] = r_ref[...] + jnp.dot(a_ref[...], w_ref[...].astype(BF16), preferred_element_type=F32)


def matmul_w(a, w, col0, ncols, tm, tn, res=None, name="matmul"):
    m, k = a.shape
    assert col0 % tn == 0 and ncols % tn == 0 and m % tm == 0
    cb = col0 // tn
    in_specs = [pl.BlockSpec((tm, k), lambda i, j: (i, 0)),
                pl.BlockSpec((k, tn), lambda i, j: (0, j + cb))]
    args = [a, w]
    kern = _mm_kernel
    if res is not None:
        in_specs.append(pl.BlockSpec((tm, tn), lambda i, j: (i, j)))
        args.append(res)
        kern = _mm_res_kernel
    return pl.pallas_call(
        kern,
        grid=(m // tm, ncols // tn),
        in_specs=in_specs,
        out_specs=pl.BlockSpec((tm, tn), lambda i, j: (i, j)),
        out_shape=jax.ShapeDtypeStruct((m, ncols), F32),
        compiler_params=_cparams(2),
        name=name,
    )(*args)


def _bias_kernel(rb_ref, bk3_ref, bks_ref, o3_ref, os_ref, *, n_heads):
    def head(h, carry):
        for t in range(3):
            bk = bk3_ref[t]
            acc = jnp.zeros(bk.shape, F32)
            for b in range(N_BUCKETS):
                acc = jnp.where(bk == b, rb_ref[b, h], acc)
            o3_ref[t, h] = acc * LOG2E
        for t in range(2):
            bk = bks_ref[t]
            acc = jnp.zeros(bk.shape, F32)
            for b in range(N_BUCKETS):
                acc = jnp.where(bk == b, rb_ref[b, h], acc)
            os_ref[t, pl.ds(h, 1), :] = acc
        return carry
    lax.fori_loop(0, n_heads, head, 0)


def bias_tables(rel_bias, page, rep):
    n_heads = rel_bias.shape[1]
    cols = page * rep
    i = np.arange(QB)[None, :]
    k = np.arange(QB)[:, None]
    bk3 = np.stack([_rel_bucket_np(i - k + 2 * QB), _rel_bucket_np(i - k + QB), _rel_bucket_np(i - k)])
    assert (_rel_bucket_np(np.arange(QB + 1, 1 << 20)) == N_BUCKETS - 1).all()
    assert (bk3[0] == N_BUCKETS - 1).all()
    assert page >= QB
    bks = np.stack([np.full((1, cols), N_BUCKETS - 1, np.int32),
                    _rel_bucket_np(page - np.arange(cols) // rep)[None, :]])
    return pl.pallas_call(
        functools.partial(_bias_kernel, n_heads=n_heads),
        in_specs=[pl.BlockSpec(memory_space=pltpu.SMEM),
                  pl.BlockSpec(memory_space=pltpu.VMEM), pl.BlockSpec(memory_space=pltpu.VMEM)],
        out_specs=[pl.BlockSpec(memory_space=pltpu.VMEM), pl.BlockSpec(memory_space=pltpu.VMEM)],
        out_shape=[jax.ShapeDtypeStruct((3, n_heads, QB, QB), F32),
                   jax.ShapeDtypeStruct((2, n_heads, cols), F32)],
    )(rel_bias, jnp.asarray(bk3), jnp.asarray(bks))


def _conv_kernel(glu_ref, prev_ref, dw_ref, bdw_ref, lng_ref, lnb_ref, wout_ref, o_ref, st_ref,
                 ext_ref, h_ref, wbf_ref, *, tt, width, dconv):
    b = pl.program_id(0)
    t = pl.program_id(1)
    pad = 32
    hist = width - 1

    @pl.when((b == 0) & (t == 0))
    def _():
        wbf_ref[...] = wout_ref[...].astype(BF16)

    @pl.when(t == 0)
    def _():
        ext_ref[pl.ds(pad - hist, hist), :] = prev_ref[...]

    @pl.when(t > 0)
    def _():
        ext_ref[pl.ds(0, pad), :] = ext_ref[pl.ds(tt, pad), :]

    glu = glu_ref[...]
    u = glu[:, :dconv] * _sigmoid(glu[:, dconv:])
    ext_ref[pl.ds(pad, tt), :] = u
    st_ref[...] = ext_ref[pl.ds(pad + tt - hist, hist), :]

    for c in range(dconv // LANES):
        cs = slice(c * LANES, (c + 1) * LANES)
        acc = jnp.zeros((tt, LANES), F32) + bdw_ref[:, cs]
        for j in range(width):
            acc = acc + dw_ref[pl.ds(j, 1), cs] * ext_ref[pl.ds(pad - hist + j, tt), cs]
        h_ref[:, cs] = acc

    h = h_ref[...]
    mu = jnp.mean(h, axis=-1, keepdims=True)
    var = jnp.mean(jnp.square(h - mu), axis=-1, keepdims=True)
    y = (h - mu) * lax.rsqrt(var + EPS) * lng_ref[...] + lnb_ref[...]
    y = y * _sigmoid(y)
    o_ref[...] = jnp.dot(y.astype(BF16), wbf_ref[...], preferred_element_type=F32)


def conv_branch_prompt(glu_pre, prev, dw, bdw, lng, lnb, wout, n_seq, seq, tt):
    width, dconv = dw.shape
    dm = wout.shape[1]
    nt = seq // tt
    kern = functools.partial(_conv_kernel, tt=tt, width=width, dconv=dconv)
    return pl.pallas_call(
        kern,
        grid=(n_seq, nt),
        in_specs=[pl.BlockSpec((tt, 2 * dconv), lambda b, t: (b * nt + t, 0)),
                  pl.BlockSpec((None, width - 1, dconv), lambda b, t: (b, 0, 0)),
                  pl.BlockSpec((width, dconv), lambda b, t: (0, 0)),
                  pl.BlockSpec((1, dconv), lambda b, t: (0, 0)),
                  pl.BlockSpec((1, dconv), lambda b, t: (0, 0)),
                  pl.BlockSpec((1, dconv), lambda b, t: (0, 0)),
                  pl.BlockSpec((dconv, dm), lambda b, t: (0, 0))],
        out_specs=[pl.BlockSpec((tt, dm), lambda b, t: (b * nt + t, 0)),
                   pl.BlockSpec((None, width - 1, dconv), lambda b, t: (b, 0, 0))],
        out_shape=[jax.ShapeDtypeStruct((n_seq * seq, dm), F32),
                   jax.ShapeDtypeStruct((n_seq, width - 1, dconv), F32)],
        scratch_shapes=[pltpu.VMEM((32 + tt, dconv), F32), pltpu.VMEM((tt, dconv), F32),
                        pltpu.VMEM((dconv, dm), BF16)],
        compiler_params=_cparams(2),
        name="conv_branch",
    )(glu_pre, prev, dw, bdw.reshape(1, dconv), lng.reshape(1, dconv), lnb.reshape(1, dconv), wout)


def _conv_step_kernel(glu_ref, prev_ref, dw_ref, bdw_ref, lng_ref, lnb_ref, wout_ref, o_ref, u_ref,
                      *, width, dconv):
    glu = glu_ref[...]
    u = glu[:, :dconv] * _sigmoid(glu[:, dconv:])
    u_ref[...] = u
    h = bdw_ref[...] + dw_ref[pl.ds(width - 1, 1), :] * u
    for j in range(width - 1):
        h = h + dw_ref[pl.ds(j, 1), :] * prev_ref[j]
    mu = jnp.mean(h, axis=-1, keepdims=True)
    var = jnp.mean(jnp.square(h - mu), axis=-1, keepdims=True)
    y = (h - mu) * lax.rsqrt(var + EPS) * lng_ref[...] + lnb_ref[...]
    y = y * _sigmoid(y)
    o_ref[...] = jnp.dot(y.astype(BF16), wout_ref[...].astype(BF16), preferred_element_type=F32)


def conv_branch_step(glu_pre, prev_t, dw, bdw, lng, lnb, wout):
    width, dconv = dw.shape
    n = glu_pre.shape[0]
    dm = wout.shape[1]
    kern = functools.partial(_conv_step_kernel, width=width, dconv=dconv)
    return pl.pallas_call(
        kern,
        out_shape=[jax.ShapeDtypeStruct((n, dm), F32), jax.ShapeDtypeStruct((n, dconv), F32)],
        compiler_params=pltpu.CompilerParams(vmem_limit_bytes=VMEM_LIMIT),
    )(glu_pre, prev_t, dw, bdw.reshape(1, dconv), lng.reshape(1, dconv), lnb.reshape(1, dconv), wout)


def _select_threshold(count_gt, row_min, row_max, n_adm, topk, any_fn):
    kf = jnp.float32(topk)
    full = n_adm <= kf
    lo0 = row_min - (1.0 + jnp.abs(row_min))
    hi0 = row_max
    flo0 = jnp.where(full, kf, n_adm)
    fhi0 = jnp.zeros_like(lo0)

    def active_rows(lo, hi, flo):
        mid = 0.5 * lo + 0.5 * hi
        return (flo != kf) & (lo < mid) & (mid < hi)

    def cond(st):
        lo, hi, flo, fhi = st
        return any_fn(active_rows(lo, hi, flo))

    def step(st, interpolate):
        lo, hi, flo, fhi = st
        act = active_rows(lo, hi, flo)
        mid = 0.5 * lo + 0.5 * hi
        if interpolate:
            guess = lo + (hi - lo) * ((flo - kf) / (flo - fhi))
            mid = jnp.where((lo < guess) & (guess < hi), guess, mid)
        c = count_gt(mid)
        up = act & (c >= kf)
        dn = act & (c < kf)
        return (jnp.where(up, mid, lo), jnp.where(dn, mid, hi),
                jnp.where(up, c, flo), jnp.where(dn, c, fhi))

    def body(st):
        return step(step(st, True), False)

    lo, hi, flo, fhi = lax.while_loop(cond, body, (lo0, hi0, flo0, fhi0))
    lo = jnp.where(full, -jnp.inf, lo)
    return lo, hi, flo, fhi


def _attn_prompt_kernel(qi_ref, wi_ref, kw_ref, q_ref, k_ref, v_ref, co_ref, ga_ref, gb_ref, bias_ref,
                        o_ref, kd_ref, kb_ref, vt_ref, wit_ref, qib_ref, sc_ref, sel_ref, qs_ref,
                        m_ref, l_ref, acc_ref, st_ref,
                        *, seq, n_heads, n_kv, idx_heads, idx_dim, topk, hd):
    j = pl.program_id(1)
    group = n_heads // n_kv
    nck = (j * QB + QB + CK - 1) // CK
    lane = lax.broadcasted_iota(jnp.int32, (1, LANES), 1)
    kf = jnp.float32(topk)

    @pl.when(j == 0)
    def _():
        kw = kw_ref[...]
        kd_ref[0] = jnp.where(lane < idx_dim, kw, 0.0).astype(BF16)
        kd_ref[1] = jnp.where(lane >= idx_dim, pltpu.roll(kw, idx_dim, 1), 0.0).astype(BF16)
        for g in range(n_kv):
            kb_ref[g] = k_ref[:, g * hd:(g + 1) * hd].astype(BF16)
            for c in range(seq // CK):
                vt_ref[g, c] = v_ref[c * CK:(c + 1) * CK, g * hd:(g + 1) * hd].T.astype(BF16)

    wit_ref[...] = (wi_ref[...] * (idx_heads ** -0.5)).T
    qib_ref[...] = (qi_ref[...] * (idx_dim ** -0.5)).astype(BF16)
    qpos = j * QB + lax.broadcasted_iota(jnp.int32, (CK, QB), 1)
    krow = lax.broadcasted_iota(jnp.int32, (CK, QB), 0)
    per_pair = LANES // idx_dim

    def score_chunk(c, carry):
        mn, mx = carry
        k0 = pl.multiple_of(c * CK, CK)
        acc = jnp.zeros((CK, QB), F32)
        for p in range(idx_heads // per_pair):
            rhs = qib_ref[:, p * LANES:(p + 1) * LANES]
            for r in range(per_pair):
                s = _dot_nt(kd_ref[r, pl.ds(k0, CK), :], rhs)
                acc = acc + wit_ref[pl.ds(idx_dim + p * per_pair + r, 1), :] * jnp.maximum(s, 0.0)
        adm = (krow + c * CK) <= qpos
        sc_ref[c] = jnp.where(adm, acc, -jnp.inf)
        mn = jnp.minimum(mn, _fold_rows(jnp.where(adm, acc, jnp.inf), jnp.minimum))
        mx = jnp.maximum(mx, _fold_rows(jnp.where(adm, acc, -jnp.inf), jnp.maximum))
        return mn, mx

    mn8, mx8 = lax.fori_loop(0, nck, score_chunk,
                             (jnp.full((8, QB), jnp.inf, F32), jnp.full((8, QB), -jnp.inf, F32)))
    row_min = jnp.min(mn8, axis=0, keepdims=True)
    row_max = jnp.max(mx8, axis=0, keepdims=True)
    n_adm = (j * QB + 1 + lax.broadcasted_iota(jnp.int32, (1, QB), 1)).astype(F32)

    def count_gt(t):
        def cbody(c, acc):
            return acc + _fold_rows(jnp.where(sc_ref[c] > t, 1.0, 0.0), jnp.add)
        part = lax.fori_loop(0, nck, cbody, jnp.zeros((8, QB), F32))
        return jnp.sum(part, axis=0, keepdims=True)

    def any_fn(mask):
        return jnp.max(jnp.where(mask, 1.0, 0.0)) > 0.0

    lo, hi, flo, fhi = _select_threshold(count_gt, row_min, row_max, n_adm, topk, any_fn)
    tie = flo != kf

    def sel_chunk(c, carry):
        sel_ref[c] = jnp.where(sc_ref[c] > lo, 1.0, 0.0)
        return carry
    lax.fori_loop(0, nck, sel_chunk, 0)

    @pl.when(any_fn(tie))
    def _():
        need = kf - fhi
        lower = (lax.broadcasted_iota(jnp.int32, (CK, CK), 1)
                 < lax.broadcasted_iota(jnp.int32, (CK, CK), 0)).astype(BF16)

        def tie_chunk(c, before):
            s = sc_ref[c]
            eq = s == hi
            eqf = jnp.where(eq, 1.0, 0.0)
            rank = before + jnp.dot(lower, eqf.astype(BF16), preferred_element_type=F32)
            keep = (s > hi) | (eq & (rank < need))
            sel_ref[c] = jnp.where(tie, jnp.where(keep, 1.0, 0.0), sel_ref[c])
            return before + jnp.sum(eqf, axis=0, keepdims=True)
        lax.fori_loop(0, nck, tie_chunk, jnp.zeros((1, QB), F32))

    scale2 = hd ** -0.5 * LOG2E
    for h in range(n_heads):
        qs_ref[h] = q_ref[:, h * hd:(h + 1) * hd].astype(BF16)
    m_ref[...] = jnp.full(m_ref.shape, NEG, F32)
    l_ref[...] = jnp.zeros(l_ref.shape, F32)
    acc_ref[...] = jnp.zeros(acc_ref.shape, F32)

    def logits(g, c):
        kc = kb_ref[g, pl.ds(pl.multiple_of(c * CK, CK), CK), :]
        qg = qs_ref[pl.ds(g * group, group)].reshape(group * QB, hd)
        st_ref[g % 2] = _dot_nt(kc, qg)

    logits(0, 0)

    def att_chunk(c, carry):
        msk = jnp.concatenate([sel_ref[c]] * group, axis=1) > 0.0
        tis = [jnp.clip(c * (CK // QB) + s - j + 2, 0, 2) for s in range(CK // QB)]
        for g in range(n_kv):
            if g + 1 < n_kv:
                logits(g + 1, c)
            else:
                logits(0, jnp.minimum(c + 1, nck - 1))
            rows = [jnp.concatenate([bias_ref[ti, g * group + hh] for hh in range(group)], axis=1)
                    for ti in tis]
            lg = jnp.where(msk, st_ref[g % 2] * scale2 + jnp.concatenate(rows, axis=0), -jnp.inf)
            m_old = m_ref[g]
            m_new = jnp.maximum(m_old, jnp.max(_fold_rows(lg, jnp.maximum), axis=0, keepdims=True))
            alpha = jnp.exp2(m_old - m_new)
            p = jnp.exp2(lg - m_new)
            l_ref[g] = alpha * l_ref[g] + jnp.sum(_fold_rows(p, jnp.add), axis=0, keepdims=True)
            pv = jnp.dot(vt_ref[g, c], p.astype(BF16), preferred_element_type=F32)
            acc_ref[g] = alpha * acc_ref[g] + pv
            m_ref[g] = m_new
        return carry
    lax.fori_loop(0, nck, att_chunk, 0)

    for g in range(n_kv):
        ot = acc_ref[g] / l_ref[g]
        for hh in range(group):
            cs = slice((g * group + hh) * hd, (g * group + hh + 1) * hd)
            o = ot[:, hh * QB:(hh + 1) * QB].T
            mixed = _sigmoid(ga_ref[:, cs]) * co_ref[:, cs] + _sigmoid(gb_ref[:, cs]) * o
            o_ref[:, cs] = mixed.astype(o_ref.dtype)


def attn_prompt(qi, kw, q, k, v, conv_out, gates, bias3, n_seq, seq, n_kv, idx_dim, idx_heads):
    m, dm = q.shape
    hd = k.shape[1] // n_kv
    n_heads = dm // hd
    group = n_heads // n_kv
    nb = seq // QB
    topk = min(TOPK_MAX, seq // 4)
    assert seq % CK == 0 and LANES % idx_dim == 0 and kw.shape[1] == LANES
    kern = functools.partial(_attn_prompt_kernel, seq=seq, n_heads=n_heads, n_kv=n_kv, idx_heads=idx_heads,
                             idx_dim=idx_dim, topk=topk, hd=hd)
    row = lambda b, j: (b * nb + j, 0)
    return pl.pallas_call(
        kern,
        grid=(n_seq, nb),
        in_specs=[pl.BlockSpec((QB, idx_heads * idx_dim), row),
                  pl.BlockSpec((QB, LANES), row),
                  pl.BlockSpec((seq, LANES), lambda b, j: (b, 0)),
                  pl.BlockSpec((QB, dm), row),
                  pl.BlockSpec((seq, n_kv * hd), lambda b, j: (b, 0)),
                  pl.BlockSpec((seq, n_kv * hd), lambda b, j: (b, 0)),
                  pl.BlockSpec((QB, dm), row),
                  pl.BlockSpec((QB, dm), lambda b, j: (b * nb + j, 0)),
                  pl.BlockSpec((QB, dm), lambda b, j: (b * nb + j, 1)),
                  pl.BlockSpec((3, n_heads, QB, QB), lambda b, j: (0, 0, 0, 0))],
        out_specs=pl.BlockSpec((QB, dm), row),
        out_shape=jax.ShapeDtypeStruct((m, dm), BF16),
        scratch_shapes=[pltpu.VMEM((2, seq, LANES), BF16),
                        pltpu.VMEM((n_kv, seq, hd), BF16),
                        pltpu.VMEM((n_kv, seq // CK, hd, CK), BF16),
                        pltpu.VMEM((LANES, QB), F32),
                        pltpu.VMEM((QB, idx_heads * idx_dim), BF16),
                        pltpu.VMEM((seq // CK, CK, QB), F32),
                        pltpu.VMEM((seq // CK, CK, QB), F32),
                        pltpu.VMEM((n_heads, QB, hd), BF16),
                        pltpu.VMEM((n_kv, 1, group * QB), F32),
                        pltpu.VMEM((n_kv, 1, group * QB), F32),
                        pltpu.VMEM((n_kv, hd, group * QB), F32),
                        pltpu.VMEM((2, CK, group * QB), F32)],
        compiler_params=_cparams(2),
        name="attn_prompt",
    )(qi, kw, kw, q, k, v, conv_out, gates, gates, bias3)


def _sample_score_kernel(pt_ref, qi_ref, wi_ref, kn_ref, *rest, pg, idx_heads, idx_dim):
    pages = rest[:pg]
    o_ref, self_ref, kcat_ref = rest[pg:]
    page = pages[0].shape[0]
    qi = qi_ref[...] * (idx_dim ** -0.5)
    wi = wi_ref[...] * (idx_heads ** -0.5)
    qb = qi.astype(BF16)
    for i in range(pg):
        kcat_ref[i * page:(i + 1) * page, :] = pages[i][...].astype(BF16)
    s = _dot_nt(qb, kcat_ref[...])
    o_ref[...] = jnp.sum(wi * jnp.maximum(s, 0.0), axis=0, keepdims=True)

    @pl.when(pl.program_id(1) == 0)
    def _():
        kn = kn_ref[...].astype(BF16).astype(F32)
        s = jnp.sum(qb.astype(F32) * kn, axis=1, keepdims=True)
        sself = jnp.sum(wi * jnp.maximum(s, 0.0), axis=0, keepdims=True)
        self_ref[...] = jnp.broadcast_to(sself, self_ref.shape)


def sample_scores(page_table, qi3, wi3, ki_new3, cache_kidx, pg):
    n, n_pages = page_table.shape
    idx_heads, idx_dim = qi3.shape[1:]
    page = cache_kidx.shape[1]
    kern = functools.partial(_sample_score_kernel, pg=pg, idx_heads=idx_heads, idx_dim=idx_dim)
    page_specs = [pl.BlockSpec((None, page, idx_dim), (lambda b, p, pt, i=i: (pt[b, p * pg + i], 0, 0)))
                  for i in range(pg)]
    grid_spec = pltpu.PrefetchScalarGridSpec(
        num_scalar_prefetch=1,
        grid=(n, n_pages // pg),
        in_specs=[pl.BlockSpec((None, idx_heads, idx_dim), lambda b, p, pt: (b, 0, 0)),
                  pl.BlockSpec((None, idx_heads, 1), lambda b, p, pt: (b, 0, 0)),
                  pl.BlockSpec((None, 1, idx_dim), lambda b, p, pt: (b, 0, 0))] + page_specs,
        out_specs=[pl.BlockSpec((None, None, 1, pg * page), lambda b, p, pt: (b, p, 0, 0)),
                   pl.BlockSpec((None, 1, LANES), lambda b, p, pt: (b, 0, 0))],
        scratch_shapes=[pltpu.VMEM((pg * page, idx_dim), BF16)],
    )
    return pl.pallas_call(
        kern,
        grid_spec=grid_spec,
        out_shape=[jax.ShapeDtypeStruct((n, n_pages // pg, 1, pg * page), F32),
                   jax.ShapeDtypeStruct((n, 1, LANES), F32)],
        compiler_params=_cparams(2),
        name="sample_scores",
    )(page_table, qi3, wi3, ki_new3, *([cache_kidx] * pg))


def _sample_select_kernel(sc_ref, self_ref, sel4_ref, selself_ref, sel_ref, *, topk, past, rep):
    sc = sc_ref[...]
    sself = self_ref[:, 0:1]
    n = sc.shape[0]
    kf = jnp.float32(topk)
    row_min = jnp.minimum(jnp.min(sc, axis=1, keepdims=True), sself)
    row_max = jnp.maximum(jnp.max(sc, axis=1, keepdims=True), sself)
    n_adm = jnp.full((n, 1), past + 1, F32)

    def count_gt(t):
        return (jnp.sum(jnp.where(sc > t, 1.0, 0.0), axis=1, keepdims=True)
                + jnp.where(sself > t, 1.0, 0.0))

    def any_fn(mask):
        return jnp.max(jnp.where(mask, 1.0, 0.0)) > 0.0

    lo, hi, flo, fhi = _select_threshold(count_gt, row_min, row_max, n_adm, topk, any_fn)
    tie = flo != kf
    sel_ref[...] = jnp.where(sc > lo, 1.0, 0.0)
    selself_ref[...] = jnp.broadcast_to(jnp.where(sself > lo, 1.0, 0.0), selself_ref.shape)

    @pl.when(any_fn(tie))
    def _():
        need = kf - fhi
        blk = 512
        tri = (lax.broadcasted_iota(jnp.int32, (blk, blk), 0)
               < lax.broadcasted_iota(jnp.int32, (blk, blk), 1)).astype(BF16)
        before = jnp.zeros((n, 1), F32)
        for c in range(past // blk):
            s = sc_ref[:, c * blk:(c + 1) * blk]
            eq = s == hi
            rank = before + jnp.dot(jnp.where(eq, 1.0, 0.0).astype(BF16), tri, preferred_element_type=F32)
            keep = (s > hi) | (eq & (rank < need))
            sel_ref[:, c * blk:(c + 1) * blk] = jnp.where(tie, jnp.where(keep, 1.0, 0.0),
                                                          sel_ref[:, c * blk:(c + 1) * blk])
            before = before + jnp.sum(jnp.where(eq, 1.0, 0.0), axis=1, keepdims=True)
        keep_self = (sself > hi) | ((sself == hi) & (before < need))
        selself_ref[...] = jnp.broadcast_to(
            jnp.where(tie, jnp.where(keep_self, 1.0, 0.0), jnp.where(sself > lo, 1.0, 0.0)), selself_ref.shape)

    blk = 512
    row_lo = lax.broadcasted_iota(jnp.int32, (blk, blk * rep), 0) * rep
    col = lax.broadcasted_iota(jnp.int32, (blk, blk * rep), 1)
    spread = jnp.where((col >= row_lo) & (col < row_lo + rep), 1.0, 0.0).astype(BF16)
    for c in range(past // blk):
        sel4_ref[:, c * blk * rep:(c + 1) * blk * rep] = jnp.dot(
            sel_ref[:, c * blk:(c + 1) * blk].astype(BF16), spread, preferred_element_type=F32)


def sample_select(scores, sself, topk, rep):
    n, past = scores.shape
    kern = functools.partial(_sample_select_kernel, topk=topk, past=past, rep=rep)
    return pl.pallas_call(
        kern,
        out_shape=[jax.ShapeDtypeStruct((n, past * rep), F32), jax.ShapeDtypeStruct((n, LANES), F32)],
        scratch_shapes=[pltpu.VMEM((n, past), F32)],
        compiler_params=pltpu.CompilerParams(vmem_limit_bytes=VMEM_LIMIT),
        name="sample_select",
    )(scores, sself)


def _sample_attn_kernel(pt_ref, q_ref, kn_ref, vn_ref, sel_ref, selself_ref, bias_ref, rb0_ref, own_ref, *rest,
                        pg, n_heads, n_kv, hd):
    kpages = rest[:pg]
    vpages = rest[pg:2 * pg]
    o_ref = rest[2 * pg]
    kcat_ref, vcat_ref, m_ref, l_ref, acc_ref = rest[2 * pg + 1:]
    p = pl.program_id(1)
    n_steps = pl.num_programs(1)
    scale = hd ** -0.5
    rows = kpages[0].shape[0]

    @pl.when(p == 0)
    def _():
        m_ref[...] = jnp.full(m_ref.shape, NEG, F32)
        l_ref[...] = jnp.zeros(l_ref.shape, F32)
        acc_ref[...] = jnp.zeros(acc_ref.shape, F32)

    for i in range(pg):
        kcat_ref[i * rows:(i + 1) * rows, :] = kpages[i][...].astype(BF16)
        vcat_ref[i * rows:(i + 1) * rows, :] = vpages[i][...].astype(BF16)
    qb = q_ref[...].astype(BF16)
    last = p == n_steps - 1
    bias = jnp.concatenate([bias_ref[0]] * (pg - 1) + [jnp.where(last, bias_ref[1], bias_ref[0])], axis=1)
    msk = (own_ref[...] > 0.0) & (sel_ref[...] > 0.0)
    lg = jnp.where(msk, _dot_nt(qb, kcat_ref[...]) * scale + bias, -jnp.inf)
    m_old = m_ref[...]
    m_new = jnp.maximum(m_old, jnp.max(lg, axis=-1, keepdims=True))
    alpha = jnp.exp(m_old - m_new)
    pr = jnp.exp(lg - m_new)
    l_new = alpha * l_ref[...] + jnp.sum(pr, axis=-1, keepdims=True)
    acc = alpha * acc_ref[...] + jnp.dot(pr.astype(BF16), vcat_ref[...], preferred_element_type=F32)
    m_ref[...] = m_new
    l_ref[...] = l_new
    acc_ref[...] = acc

    @pl.when(last)
    def _():
        kn = kn_ref[...].astype(BF16).astype(F32)
        vn = vn_ref[...].astype(BF16).astype(F32)
        ls = jnp.sum(qb.astype(F32) * kn, axis=-1, keepdims=True) * scale + rb0_ref[...]
        on = selself_ref[:, 0:1] > 0.0
        ls = jnp.where(on, ls, NEG)
        m_f = jnp.maximum(m_new, ls)
        a2 = jnp.exp(m_new - m_f)
        ps = jnp.where(on, jnp.exp(ls - m_f), 0.0)
        l_f = a2 * l_new + ps
        acc_f = a2 * acc + ps.astype(BF16).astype(F32) * vn
        o_ref[...] = acc_f / l_f


def sample_attention(page_table, q3, k_rep, v_rep, sel4, selself3, bias_s, rb0, cache_k, cache_v, pg, n_kv):
    n, n_pages = page_table.shape
    n_heads, hd = q3.shape[1:]
    rows = cache_k.shape[1]
    kern = functools.partial(_sample_attn_kernel, pg=pg, n_heads=n_heads, n_kv=n_kv, hd=hd)
    own = (np.arange(pg * rows)[None, :] % n_kv
           == np.arange(n_heads)[:, None] // (n_heads // n_kv)).astype(np.float32)
    kspecs = [pl.BlockSpec((None, rows, hd), (lambda b, p, pt, i=i: (pt[b, p * pg + i], 0, 0)))
              for i in range(pg)]
    per_seq = lambda b, p, pt: (b, 0, 0)
    grid_spec = pltpu.PrefetchScalarGridSpec(
        num_scalar_prefetch=1,
        grid=(n, n_pages // pg),
        in_specs=[pl.BlockSpec((None, n_heads, hd), per_seq),
                  pl.BlockSpec((None, n_heads, hd), per_seq),
                  pl.BlockSpec((None, n_heads, hd), per_seq),
                  pl.BlockSpec((None, None, 1, pg * rows), lambda b, p, pt: (b, p, 0, 0)),
                  pl.BlockSpec((None, 1, LANES), per_seq),
                  pl.BlockSpec((2, n_heads, rows), lambda b, p, pt: (0, 0, 0)),
                  pl.BlockSpec((n_heads, 1), lambda b, p, pt: (0, 0)),
                  pl.BlockSpec((n_heads, pg * rows), lambda b, p, pt: (0, 0))] + kspecs + kspecs,
        out_specs=pl.BlockSpec((None, n_heads, hd), per_seq),
        scratch_shapes=[pltpu.VMEM((pg * rows, hd), BF16), pltpu.VMEM((pg * rows, hd), BF16),
                        pltpu.VMEM((n_heads, 1), F32), pltpu.VMEM((n_heads, 1), F32),
                        pltpu.VMEM((n_heads, hd), F32)],
    )
    return pl.pallas_call(
        kern,
        grid_spec=grid_spec,
        out_shape=jax.ShapeDtypeStruct((n, n_heads, hd), F32),
        compiler_params=_cparams(2),
        name="sample_attn",
    )(page_table, q3, k_rep, v_rep, sel4, selself3, bias_s, rb0, jnp.asarray(own), *([cache_k] * pg),
      *([cache_v] * pg))


def _mix_kernel(ga_ref, gb_ref, co_ref, at_ref, o_ref):
    o_ref[...] = (_sigmoid(ga_ref[...]) * co_ref[...] + _sigmoid(gb_ref[...]) * at_ref[...]).astype(o_ref.dtype)


def gated_mix(gates, conv_out, attn):
    n, dm = conv_out.shape
    return pl.pallas_call(
        _mix_kernel,
        grid=(1,),
        in_specs=[pl.BlockSpec((n, dm), lambda i: (0, 0)), pl.BlockSpec((n, dm), lambda i: (0, 1)),
                  pl.BlockSpec((n, dm), lambda i: (0, 0)), pl.BlockSpec((n, dm), lambda i: (0, 0))],
        out_specs=pl.BlockSpec((n, dm), lambda i: (0, 0)),
        out_shape=jax.ShapeDtypeStruct((n, dm), BF16),
    )(gates, gates, conv_out, attn)


def _ffn_up_kernel(x_ref, xh_ref, wg_ref, wv_ref, dg_ref, dv_ref, bg_ref, bv_ref, pg_ref, pv_ref,
                   o_ref, sg_ref, sv_ref, eg_ref, ev_ref, wgb_ref, wvb_ref, *, tm, rs, width, tiles_per_seq):
    hist = width - 1
    pad = xh_ref.shape[0]
    first = pl.program_id(0) % tiles_per_seq == 0
    branches = ((wg_ref, wgb_ref, dg_ref, bg_ref, pg_ref, eg_ref, sg_ref),
                (wv_ref, wvb_ref, dv_ref, bv_ref, pv_ref, ev_ref, sv_ref))
    kc = 256
    for w_ref, wb_ref, _, _, prev_ref, e_ref, _ in branches:
        for k0 in range(0, w_ref.shape[0], kc):
            wb_ref[k0:k0 + kc, :] = w_ref[k0:k0 + kc, :].astype(BF16)
        e_ref[pl.ds(0, pad), :] = jnp.dot(xh_ref[...], wb_ref[...], preferred_element_type=F32)

        @pl.when(first)
        def _():
            e_ref[pl.ds(pad - hist, hist), :] = prev_ref[...]

    def project(r):
        for _, wb_ref, _, _, _, e_ref, _ in branches:
            e_ref[pl.ds(pad + r * rs, rs), :] = jnp.dot(x_ref[pl.ds(r * rs, rs), :], wb_ref[...],
                                                        preferred_element_type=F32)

    def activate(r):
        outs = []
        for _, _, d_ref, b_ref, _, e_ref, _ in branches:
            acc = b_ref[...] + d_ref[pl.ds(hist, 1), :] * e_ref[pl.ds(pad + r * rs, rs), :]
            for j in range(hist):
                acc = acc + d_ref[pl.ds(j, 1), :] * e_ref[pl.ds(pad + r * rs - hist + j, rs), :]
            outs.append(acc)
        g, v = outs
        o_ref[pl.ds(r * rs, rs), :] = (g * _sigmoid(g) * v).astype(o_ref.dtype)

    n_sub = tm // rs
    project(0)
    for r in range(1, n_sub):
        project(r)
        activate(r - 1)
    activate(n_sub - 1)
    for _, _, _, _, _, e_ref, s_ref in branches:
        s_ref[...] = e_ref[pl.ds(pad + tm - hist, hist), :]


def ffn_up_act(xn, w_up, prev, dw, bdw, n_seq, seq, tm, tn):
    m, k = xn.shape
    f2 = w_up.shape[1]
    f = f2 // 2
    width = dw.shape[0]
    ncb = f // tn
    pad = 16
    tps = seq // tm
    assert seq % tm == 0 and tm % pad == 0 and f % tn == 0 and width - 1 <= pad
    rs = min(256, tm // 2)
    assert tm % rs == 0 and rs % 8 == 0 and k % 256 == 0
    kern = functools.partial(_ffn_up_kernel, tm=tm, rs=rs, width=width, tiles_per_seq=tps)
    hb = tm // pad
    st_shape = jax.ShapeDtypeStruct((m // tm, width - 1, f), F32)
    act, sg, sv = pl.pallas_call(
        kern,
        grid=(m // tm, ncb),
        in_specs=[pl.BlockSpec((tm, k), lambda i, c: (i, 0)),
                  pl.BlockSpec((pad, k), lambda i, c: (jnp.maximum(i * hb - 1, 0), 0)),
                  pl.BlockSpec((k, tn), lambda i, c: (0, c)),
                  pl.BlockSpec((k, tn), lambda i, c: (0, c + ncb)),
                  pl.BlockSpec((width, tn), lambda i, c: (0, c)),
                  pl.BlockSpec((width, tn), lambda i, c: (0, c + ncb)),
                  pl.BlockSpec((1, tn), lambda i, c: (0, c)),
                  pl.BlockSpec((1, tn), lambda i, c: (0, c + ncb)),
                  pl.BlockSpec((None, width - 1, tn), lambda i, c: (i // tps, 0, c)),
                  pl.BlockSpec((None, width - 1, tn), lambda i, c: (i // tps, 0, c + ncb))],
        out_specs=[pl.BlockSpec((tm, tn), lambda i, c: (i, c)),
                   pl.BlockSpec((None, width - 1, tn), lambda i, c: (i, 0, c)),
                   pl.BlockSpec((None, width - 1, tn), lambda i, c: (i, 0, c))],
        out_shape=[jax.ShapeDtypeStruct((m, f), BF16), st_shape, st_shape],
        scratch_shapes=[pltpu.VMEM((pad + tm, tn), F32), pltpu.VMEM((pad + tm, tn), F32),
                        pltpu.VMEM((k, tn), BF16), pltpu.VMEM((k, tn), BF16)],
        compiler_params=_cparams(2),
        name="ffn_up_act",
    )(xn, xn, w_up, w_up, dw, dw, bdw.reshape(1, f2), bdw.reshape(1, f2), prev, prev)
    state = jnp.concatenate([sg, sv], axis=-1).reshape(n_seq, tps, width - 1, f2)[:, tps - 1]
    return act, state


def _ffn_act_step_kernel(ug_ref, uv_ref, pg_ref, pv_ref, wg_ref, wv_ref, bg_ref, bv_ref, o_ref, *, width):
    def conv(u_ref, prev_ref, w_ref, b_ref):
        acc = b_ref[...] + w_ref[pl.ds(width - 1, 1), :] * u_ref[...]
        for j in range(width - 1):
            acc = acc + w_ref[pl.ds(j, 1), :] * prev_ref[j]
        return acc
    g = conv(ug_ref, pg_ref, wg_ref, bg_ref)
    v = conv(uv_ref, pv_ref, wv_ref, bv_ref)
    o_ref[...] = (g * _sigmoid(g) * v).astype(o_ref.dtype)


def ffn_act_step(u, prev_t, dw, bdw, tc):
    n, f2 = u.shape
    f = f2 // 2
    width = dw.shape[0]
    ncb = f // tc
    kern = functools.partial(_ffn_act_step_kernel, width=width)
    return pl.pallas_call(
        kern,
        grid=(ncb,),
        in_specs=[pl.BlockSpec((n, tc), lambda c: (0, c)),
                  pl.BlockSpec((n, tc), lambda c: (0, c + ncb)),
                  pl.BlockSpec((width - 1, n, tc), lambda c: (0, 0, c)),
                  pl.BlockSpec((width - 1, n, tc), lambda c: (0, 0, c + ncb)),
                  pl.BlockSpec((width, tc), lambda c: (0, c)),
                  pl.BlockSpec((width, tc), lambda c: (0, c + ncb)),
                  pl.BlockSpec((1, tc), lambda c: (0, c)),
                  pl.BlockSpec((1, tc), lambda c: (0, c + ncb))],
        out_specs=pl.BlockSpec((n, tc), lambda c: (0, c)),
        out_shape=jax.ShapeDtypeStruct((n, f), BF16),
        compiler_params=_cparams(1),
    )(u, u, prev_t, prev_t, dw, dw, bdw.reshape(1, f2), bdw.reshape(1, f2))


def _tiles(m):
    tm = 1024 if m % 1024 == 0 else m
    return tm


def _in_projection(xn, w_in, sizes, tm):
    glu_w, q_w, k_w, v_w, qi_w, ki_w, wi_w, ga_w, gb_w = sizes
    offs = np.concatenate([[0], np.cumsum(sizes)])
    tn = 512
    glu = matmul_w(xn, w_in, int(offs[0]), glu_w, tm, tn, name="mm_glu")
    q = matmul_w(xn, w_in, int(offs[1]), q_w, tm, tn, name="mm_q")
    k = matmul_w(xn, w_in, int(offs[2]), k_w, tm, min(tn, k_w), name="mm_k")
    v = matmul_w(xn, w_in, int(offs[3]), v_w, tm, min(tn, v_w), name="mm_v")
    qi = matmul_w(xn, w_in, int(offs[4]), qi_w, tm, min(tn, qi_w), name="mm_qi")
    w_kw = jnp.pad(w_in[:, int(offs[5]):int(offs[7])], ((0, 0), (0, LANES - ki_w - wi_w)))
    kw = matmul_w(xn, w_kw, 0, LANES, tm, LANES, name="mm_kw")
    w_g = w_in[:, int(offs[7]):]
    gates = matmul_w(xn, w_g, 0, ga_w + gb_w, tm, tn, name="mm_gates")
    return glu, q, k, v, qi, kw, gates


def kernel(x_prompt, x_sample, cache_k, cache_v, cache_kidx, state_conv, state_ffn, page_table, rel_bias,
           norm_attn, w_in, dw_conv, b_dw_conv, ln_conv_g, ln_conv_b, w_conv_out, w_o, norm_ffn, w_up, dw_ffn,
           b_dw_ffn, w_down, norm_final):
    bsz, seq, dm = x_prompt.shape
    nd, dec_seq, _ = x_sample.shape
    depth, n_pool, page, n_kv, hd = cache_k.shape
    idx_dim = cache_kidx.shape[-1]
    n_pages = page_table.shape[1]
    past = n_pages * page
    width, dconv = dw_conv.shape[1:]
    fwidth = dw_ffn.shape[1]
    f = w_down.shape[1]
    n_heads = w_o.shape[1] // hd
    d_attn = n_heads * hd
    d_kv = n_kv * hd
    n_in = w_in.shape[2]
    idx_heads = (n_in - 2 * dconv - d_attn - 2 * d_kv - idx_dim - 2 * dm) // (idx_dim + 1)
    sizes = (2 * dconv, d_attn, d_kv, d_kv, idx_heads * idx_dim, idx_dim, idx_heads, dm, dm)
    assert sum(sizes) == n_in and depth == 1 and dec_seq == 1 and page == LANES and d_attn == dm

    mp = bsz * seq
    xp = x_prompt.reshape(mp, dm)
    xs = x_sample.reshape(nd, dm)
    bias3, bias_s = bias_tables(rel_bias, page, n_kv)
    tmp = _tiles(mp)
    drop = lambda a: a.reshape(a.shape[1:])
    (norm_attn, w_in, dw_conv, b_dw_conv, ln_conv_g, ln_conv_b, w_conv_out, w_o, norm_ffn, w_up, dw_ffn,
     b_dw_ffn, w_down, state_conv, state_ffn) = map(drop, (
         norm_attn, w_in, dw_conv, b_dw_conv, ln_conv_g, ln_conv_b, w_conv_out, w_o, norm_ffn, w_up, dw_ffn,
         b_dw_ffn, w_down, state_conv, state_ffn))
    kidx_pool = cache_kidx.reshape(n_pool, page, idx_dim)
    k_pool = cache_k.reshape(n_pool, page * n_kv, hd)
    v_pool = cache_v.reshape(n_pool, page * n_kv, hd)

    xn = rmsnorm_rows(xp, norm_attn, BF16, 512)
    glu, q, k, v, qi, kw, gates = _in_projection(xn, w_in, sizes, tmp)
    conv0 = jnp.zeros((bsz, width - 1, dconv), F32)
    conv_out, conv_state_p = conv_branch_prompt(glu, conv0, dw_conv, b_dw_conv, ln_conv_g, ln_conv_b,
                                                w_conv_out, bsz, seq, 256)
    mixed = attn_prompt(qi, kw, q, k, v, conv_out, gates, bias3, bsz, seq, n_kv, idx_dim, idx_heads)
    x2 = matmul_w(mixed, w_o, 0, dm, tmp, 512, res=xp, name="mm_o")
    xn2 = rmsnorm_rows(x2, norm_ffn, BF16, 512)
    ffn0 = jnp.zeros((bsz, fwidth - 1, 2 * f), F32)
    act, ffn_state_p = ffn_up_act(xn2, w_up, ffn0, dw_ffn, b_dw_ffn, bsz, seq, min(tmp, seq // 2), 512)
    x3 = matmul_w(act, w_down, 0, dm, tmp, 256, res=x2, name="mm_down")
    y_prompt = rmsnorm_rows(x3, norm_final, F32, 512).reshape(bsz, seq, dm)

    xns = rmsnorm_rows(xs, norm_attn, BF16, nd)
    glu_s, q_s, k_s, v_s, qi_s, kw_s, gates_s = _in_projection(xns, w_in, sizes, nd)
    ki_s = kw_s[:, :idx_dim]
    wi_s = kw_s[:, idx_dim:idx_dim + idx_heads]
    sc_prev_t = jnp.swapaxes(state_conv, 0, 1)
    conv_out_s, u_conv_s = conv_branch_step(glu_s, sc_prev_t, dw_conv, b_dw_conv, ln_conv_g, ln_conv_b,
                                            w_conv_out)
    conv_state_s = jnp.concatenate([state_conv[:, 1:], u_conv_s[:, None, :]], axis=1)

    pg = 16 if n_pages % 16 == 0 else 8
    scores3, sself3 = sample_scores(page_table, qi_s.reshape(nd, idx_heads, idx_dim),
                                    wi_s.reshape(nd, idx_heads, 1), ki_s.reshape(nd, 1, idx_dim), kidx_pool, pg)
    topk_s = min(TOPK_MAX, (past + dec_seq) // 4)
    sel4, selself = sample_select(scores3.reshape(nd, past), sself3.reshape(nd, LANES), topk_s, n_kv)
    group = n_heads // n_kv
    k_rep = jnp.repeat(k_s.reshape(nd, n_kv, hd), group, axis=1)
    v_rep = jnp.repeat(v_s.reshape(nd, n_kv, hd), group, axis=1)
    attn_s = sample_attention(page_table, q_s.reshape(nd, n_heads, hd), k_rep, v_rep,
                              sel4.reshape(nd, n_pages // pg, 1, pg * page * n_kv), selself.reshape(nd, 1, LANES),
                              bias_s,
                              rel_bias[0].reshape(n_heads, 1), k_pool, v_pool, pg, n_kv)
    mixed_s = gated_mix(gates_s, conv_out_s, attn_s.reshape(nd, dm))
    x2s = matmul_w(mixed_s, w_o, 0, dm, nd, 512, res=xs)
    xn2s = rmsnorm_rows(x2s, norm_ffn, BF16, nd)
    u_s = matmul_w(xn2s, w_up, 0, 2 * f, nd, 512)
    sf_prev_t = jnp.swapaxes(state_ffn, 0, 1)
    act_s = ffn_act_step(u_s, sf_prev_t, dw_ffn, b_dw_ffn, 512)
    x3s = matmul_w(act_s, w_down, 0, dm, nd, 256, res=x2s)
    y_sample = rmsnorm_rows(x3s, norm_final, F32, nd).reshape(nd, dec_seq, dm)
    ffn_state_s = jnp.concatenate([state_ffn[:, 1:], u_s[:, None, :]], axis=1)

    return (y_prompt, y_sample,
            k.reshape(1, bsz, seq, n_kv, hd), v.reshape(1, bsz, seq, n_kv, hd),
            kw.reshape(bsz, seq, LANES)[None, :, :, :idx_dim],
            conv_state_p[None], ffn_state_p[None],
            k_s.reshape(1, nd, dec_seq, n_kv, hd), v_s.reshape(1, nd, dec_seq, n_kv, hd),
            ki_s.reshape(1, nd, dec_seq, idx_dim),
            conv_state_s[None], ffn_state_s[None])
```

```python
import functools
import math

import numpy as np
import jax
import jax.numpy as jnp
from jax import lax
from jax.experimental import pallas as pl
from jax.experimental.pallas import tpu as pltpu

F32 = jnp.float32
BF16 = jnp.bfloat16

EPS = 1e-6
TOPK_MAX = 256
N_BUCKETS = 32
MAX_DISTANCE = 128
QB = 128
CK = 256
LANES = 128
NEG = -1e30
LOG2E = math.log2(math.e)
VMEM_LIMIT = 56 * 1024 * 1024


def _cparams(n_axes, vmem=VMEM_LIMIT):
    return pltpu.CompilerParams(dimension_semantics=("arbitrary",) * n_axes, vmem_limit_bytes=vmem)


def _dot_nt(a, b):
    return lax.dot_general(a, b, (((1,), (1,)), ((), ())), preferred_element_type=F32)


def _sigmoid(x):
    return 1.0 / (1.0 + jnp.exp(-x))


def _fold_rows(x, op):
    while x.shape[0] > 8:
        half = x.shape[0] // 2
        x = op(x[:half], x[half:])
    return x


def _rel_bucket_np(dist):
    n = np.maximum(dist, 0)
    max_exact = N_BUCKETS // 2
    nf = np.maximum(n, 1).astype(np.float32)
    large = max_exact + (np.log(nf / np.float32(max_exact)) / np.float32(math.log(MAX_DISTANCE / max_exact))
                         * np.float32(N_BUCKETS - max_exact)).astype(np.int32)
    large = np.minimum(large, N_BUCKETS - 1)
    return np.where(n < max_exact, n, large).astype(np.int32)


def _rms_kernel(x_ref, g_ref, o_ref):
    x = x_ref[...]
    y = x * lax.rsqrt(jnp.mean(x * x, axis=-1, keepdims=True) + EPS) * g_ref[...]
    o_ref[...] = y.astype(o_ref.dtype)


def rmsnorm_rows(x, g, out_dtype, tm):
    m, d = x.shape
    return pl.pallas_call(
        _rms_kernel,
        grid=(m // tm,),
        in_specs=[pl.BlockSpec((tm, d), lambda i: (i, 0)), pl.BlockSpec((1, d), lambda i: (0, 0))],
        out_specs=pl.BlockSpec((tm, d), lambda i: (i, 0)),
        out_shape=jax.ShapeDtypeStruct((m, d), out_dtype),
        compiler_params=_cparams(1),
        name="rmsnorm",
    )(x, g.reshape(1, d))


def _cast_kernel(w_ref, o_ref):
    o_ref[...] = w_ref[...].astype(o_ref.dtype)


def cast_bf16(w, ncols=None):
    k, n = w.shape
    ncols = n if ncols is None else ncols
    tk = 512 if k % 512 == 0 else k
    tn = 1024 if ncols % 1024 == 0 else (512 if ncols % 512 == 0 else ncols)
    return pl.pallas_call(
        _cast_kernel,
        grid=(k // tk, ncols // tn),
        in_specs=[pl.BlockSpec((tk, tn), lambda i, j: (i, j))],
        out_specs=pl.BlockSpec((tk, tn), lambda i, j: (i, j)),
        out_shape=jax.ShapeDtypeStruct((k, ncols), BF16),
        compiler_params=_cparams(2),
        name="cast_bf16",
    )(w)


def _tail_kernel(*refs, shift, nblk):
    blocks = refs[:nblk]
    sel_ref, og_ref, okw_ref = refs[nblk:]
    lane = lax.broadcasted_iota(jnp.int32, (1, LANES), 1)
    parts = [blocks[i][...].astype(BF16) for i in range(nblk - 1)]
    parts.append(jnp.where(lane < shift, blocks[nblk - 1][...], 0.0).astype(BF16))
    og_ref[...] = jnp.dot(jnp.concatenate(parts, axis=1), sel_ref[...],
                          preferred_element_type=F32).astype(og_ref.dtype)
    okw_ref[...] = jnp.where(lane < shift, blocks[0][...], 0.0).astype(okw_ref.dtype)


def realign_tail(w_in, col0, n_small, n_big):
    k, n = w_in.shape
    tn = 512
    tk = 512 if k % 512 == 0 else k
    shift = n_small
    assert col0 % LANES == 0 and 0 < shift < LANES and n_big % tn == 0 and col0 + shift + n_big == n
    nblk = tn // LANES + 1
    cb = col0 // LANES
    sel = np.zeros((nblk * LANES, tn), np.float32)
    sel[np.arange(tn) + shift, np.arange(tn)] = 1.0
    in_specs = [pl.BlockSpec((tk, LANES), (lambda i, c, b=b: (i, cb + c * (tn // LANES) + b))) for b in range(nblk)]
    in_specs.append(pl.BlockSpec((nblk * LANES, tn), lambda i, c: (0, 0)))
    big, small = pl.pallas_call(
        functools.partial(_tail_kernel, shift=shift, nblk=nblk),
        grid=(k // tk, n_big // tn),
        in_specs=in_specs,
        out_specs=[pl.BlockSpec((tk, tn), lambda i, c: (i, c)),
                   pl.BlockSpec((None, tk, LANES), lambda i, c: (c, i, 0))],
        out_shape=[jax.ShapeDtypeStruct((k, n_big), BF16),
                   jax.ShapeDtypeStruct((n_big // tn, k, LANES), BF16)],
        compiler_params=_cparams(2),
        name="realign_tail",
    )(*([w_in] * nblk), jnp.asarray(sel, BF16))
    return small[0], big


def _mm_kernel(a_ref, w_ref, o_ref):
    o_ref[...] = jnp.dot(a_ref[...], w_ref[...], preferred_element_type=F32).astype(o_ref.dtype)


def _mm_res_kernel(a_ref, w_ref, r_ref, o_ref):
    o_ref[...] = r_ref[...] + jnp.dot(a_ref[...], w_ref[...], preferred_element_type=F32)


def matmul_w(a, w, col0, ncols, tm, tn, res=None, name="matmul", out_dtype=F32):
    m, k = a.shape
    assert col0 % tn == 0 and ncols % tn == 0 and m % tm == 0
    cb = col0 // tn
    in_specs = [pl.BlockSpec((tm, k), lambda i, j: (i, 0)),
                pl.BlockSpec((k, tn), lambda i, j: (0, j + cb))]
    args = [a, w]
    kern = _mm_kernel
    if res is not None:
        in_specs.append(pl.BlockSpec((tm, tn), lambda i, j: (i, j)))
        args.append(res)
        kern = _mm_res_kernel
    return pl.pallas_call(
        kern,
        grid=(m // tm, ncols // tn),
        in_specs=in_specs,
        out_specs=pl.BlockSpec((tm, tn), lambda i, j: (i, j)),
        out_shape=jax.ShapeDtypeStruct((m, ncols), out_dtype),
        compiler_params=_cparams(2),
        name=name,
    )(*args)


def _bias_kernel(rb_ref, bk3_ref, bks_ref, o3_ref, os_ref, *, n_heads):
    def head(h, carry):
        for t in range(3):
            bk = bk3_ref[t]
            acc = jnp.zeros(bk.shape, F32)
            for b in range(N_BUCKETS):
                acc = jnp.where(bk == b, rb_ref[b, h], acc)
            o3_ref[t, h] = acc * LOG2E
        for t in range(2):
            bk = bks_ref[t]
            acc = jnp.zeros(bk.shape, F32)
            for b in range(N_BUCKETS):
                acc = jnp.where(bk == b, rb_ref[b, h], acc)
            os_ref[t, pl.ds(h, 1), :] = acc
        return carry
    lax.fori_loop(0, n_heads, head, 0)


def bias_tables(rel_bias, page, rep):
    n_heads = rel_bias.shape[1]
    cols = page * rep
    i = np.arange(QB)[None, :]
    k = np.arange(QB)[:, None]
    bk3 = np.stack([_rel_bucket_np(i - k + 2 * QB), _rel_bucket_np(i - k + QB), _rel_bucket_np(i - k)])
    assert (_rel_bucket_np(np.arange(QB + 1, 1 << 20)) == N_BUCKETS - 1).all()
    assert (bk3[0] == N_BUCKETS - 1).all()
    assert page >= QB
    bks = np.stack([np.full((1, cols), N_BUCKETS - 1, np.int32),
                    _rel_bucket_np(page - np.arange(cols) // rep)[None, :]])
    return pl.pallas_call(
        functools.partial(_bias_kernel, n_heads=n_heads),
        in_specs=[pl.BlockSpec(memory_space=pltpu.SMEM),
                  pl.BlockSpec(memory_space=pltpu.VMEM), pl.BlockSpec(memory_space=pltpu.VMEM)],
        out_specs=[pl.BlockSpec(memory_space=pltpu.VMEM), pl.BlockSpec(memory_space=pltpu.VMEM)],
        out_shape=[jax.ShapeDtypeStruct((3, n_heads, QB, QB), F32),
                   jax.ShapeDtypeStruct((2, n_heads, cols), F32)],
    )(rel_bias, jnp.asarray(bk3), jnp.asarray(bks))


def _conv_kernel(glu_ref, prev_ref, dw_ref, bdw_ref, lng_ref, lnb_ref, wout_ref, o_ref, st_ref,
                 ext_ref, h_ref, wbf_ref, *, tt, width, dconv):
    b = pl.program_id(0)
    t = pl.program_id(1)
    pad = 32
    hist = width - 1

    @pl.when((b == 0) & (t == 0))
    def _():
        wbf_ref[...] = wout_ref[...].astype(BF16)

    @pl.when(t == 0)
    def _():
        ext_ref[pl.ds(pad - hist, hist), :] = prev_ref[...]

    @pl.when(t > 0)
    def _():
        ext_ref[pl.ds(0, pad), :] = ext_ref[pl.ds(tt, pad), :]

    glu = glu_ref[...]
    u = glu[:, :dconv] * _sigmoid(glu[:, dconv:])
    ext_ref[pl.ds(pad, tt), :] = u
    st_ref[...] = ext_ref[pl.ds(pad + tt - hist, hist), :]

    for c in range(dconv // LANES):
        cs = slice(c * LANES, (c + 1) * LANES)
        acc = jnp.zeros((tt, LANES), F32) + bdw_ref[:, cs]
        for j in range(width):
            acc = acc + dw_ref[pl.ds(j, 1), cs] * ext_ref[pl.ds(pad - hist + j, tt), cs]
        h_ref[:, cs] = acc

    h = h_ref[...]
    mu = jnp.mean(h, axis=-1, keepdims=True)
    var = jnp.mean(jnp.square(h - mu), axis=-1, keepdims=True)
    y = (h - mu) * lax.rsqrt(var + EPS) * lng_ref[...] + lnb_ref[...]
    y = y * _sigmoid(y)
    o_ref[...] = jnp.dot(y.astype(BF16), wbf_ref[...], preferred_element_type=F32)


def conv_branch_prompt(glu_pre, prev, dw, bdw, lng, lnb, wout, n_seq, seq, tt):
    width, dconv = dw.shape
    dm = wout.shape[1]
    nt = seq // tt
    kern = functools.partial(_conv_kernel, tt=tt, width=width, dconv=dconv)
    return pl.pallas_call(
        kern,
        grid=(n_seq, nt),
        in_specs=[pl.BlockSpec((tt, 2 * dconv), lambda b, t: (b * nt + t, 0)),
                  pl.BlockSpec((None, width - 1, dconv), lambda b, t: (b, 0, 0)),
                  pl.BlockSpec((width, dconv), lambda b, t: (0, 0)),
                  pl.BlockSpec((1, dconv), lambda b, t: (0, 0)),
                  pl.BlockSpec((1, dconv), lambda b, t: (0, 0)),
                  pl.BlockSpec((1, dconv), lambda b, t: (0, 0)),
                  pl.BlockSpec((dconv, dm), lambda b, t: (0, 0))],
        out_specs=[pl.BlockSpec((tt, dm), lambda b, t: (b * nt + t, 0)),
                   pl.BlockSpec((None, width - 1, dconv), lambda b, t: (b, 0, 0))],
        out_shape=[jax.ShapeDtypeStruct((n_seq * seq, dm), F32),
                   jax.ShapeDtypeStruct((n_seq, width - 1, dconv), F32)],
        scratch_shapes=[pltpu.VMEM((32 + tt, dconv), F32), pltpu.VMEM((tt, dconv), F32),
                        pltpu.VMEM((dconv, dm), BF16)],
        compiler_params=_cparams(2),
        name="conv_branch",
    )(glu_pre, prev, dw, bdw.reshape(1, dconv), lng.reshape(1, dconv), lnb.reshape(1, dconv), wout)


def _conv_step_kernel(glu_ref, prev_ref, dw_ref, bdw_ref, lng_ref, lnb_ref, wout_ref, o_ref, u_ref,
                      *, width, dconv):
    glu = glu_ref[...]
    u = glu[:, :dconv] * _sigmoid(glu[:, dconv:])
    u_ref[...] = u
    h = bdw_ref[...] + dw_ref[pl.ds(width - 1, 1), :] * u
    for j in range(width - 1):
        h = h + dw_ref[pl.ds(j, 1), :] * prev_ref[j]
    mu = jnp.mean(h, axis=-1, keepdims=True)
    var = jnp.mean(jnp.square(h - mu), axis=-1, keepdims=True)
    y = (h - mu) * lax.rsqrt(var + EPS) * lng_ref[...] + lnb_ref[...]
    y = y * _sigmoid(y)
    o_ref[...] = jnp.dot(y.astype(BF16), wout_ref[...].astype(BF16), preferred_element_type=F32)


def conv_branch_step(glu_pre, prev_t, dw, bdw, lng, lnb, wout):
    width, dconv = dw.shape
    n = glu_pre.shape[0]
    dm = wout.shape[1]
    kern = functools.partial(_conv_step_kernel, width=width, dconv=dconv)
    return pl.pallas_call(
        kern,
        out_shape=[jax.ShapeDtypeStruct((n, dm), F32), jax.ShapeDtypeStruct((n, dconv), F32)],
        compiler_params=pltpu.CompilerParams(vmem_limit_bytes=VMEM_LIMIT),
    )(glu_pre, prev_t, dw, bdw.reshape(1, dconv), lng.reshape(1, dconv), lnb.reshape(1, dconv), wout)


def _select_threshold(count_gt, row_min, row_max, n_adm, topk, any_fn):
    kf = jnp.float32(topk)
    full = n_adm <= kf
    lo0 = row_min - (1.0 + jnp.abs(row_min))
    hi0 = row_max
    flo0 = jnp.where(full, kf, n_adm)
    fhi0 = jnp.zeros_like(lo0)

    def active_rows(lo, hi, flo):
        mid = 0.5 * lo + 0.5 * hi
        return (flo != kf) & (lo < mid) & (mid < hi)

    def cond(st):
        lo, hi, flo, fhi = st
        return any_fn(active_rows(lo, hi, flo))

    def step(st, interpolate):
        lo, hi, flo, fhi = st
        act = active_rows(lo, hi, flo)
        mid = 0.5 * lo + 0.5 * hi
        if interpolate:
            guess = lo + (hi - lo) * ((flo - kf) / (flo - fhi))
            mid = jnp.where((lo < guess) & (guess < hi), guess, mid)
        c = count_gt(mid)
        up = act & (c >= kf)
        dn = act & (c < kf)
        return (jnp.where(up, mid, lo), jnp.where(dn, mid, hi),
                jnp.where(up, c, flo), jnp.where(dn, c, fhi))

    def body(st):
        return step(step(st, True), False)

    lo, hi, flo, fhi = lax.while_loop(cond, body, (lo0, hi0, flo0, fhi0))
    lo = jnp.where(full, -jnp.inf, lo)
    return lo, hi, flo, fhi


def _attn_prompt_kernel(qi_ref, wi_ref, kw_ref, q_ref, k_ref, v_ref, co_ref, ga_ref, gb_ref, bias_ref,
                        o_ref, kd_ref, kb_ref, vt_ref, wit_ref, qib_ref, sc_ref, sel_ref, qs_ref,
                        m_ref, l_ref, acc_ref, st_ref,
                        *, seq, n_heads, n_kv, idx_heads, idx_dim, topk, hd):
    j = pl.program_id(1)
    group = n_heads // n_kv
    nck = (j * QB + QB + CK - 1) // CK
    lane = lax.broadcasted_iota(jnp.int32, (1, LANES), 1)
    kf = jnp.float32(topk)

    @pl.when(j == 0)
    def _():
        kw = kw_ref[...]
        kd_ref[0] = jnp.where(lane < idx_dim, kw, 0.0).astype(BF16)
        kd_ref[1] = jnp.where(lane >= idx_dim, pltpu.roll(kw, idx_dim, 1), 0.0).astype(BF16)
        for g in range(n_kv):
            kb_ref[g] = k_ref[:, g * hd:(g + 1) * hd].astype(BF16)
            for c in range(seq // CK):
                vt_ref[g, c] = v_ref[c * CK:(c + 1) * CK, g * hd:(g + 1) * hd].T.astype(BF16)

    wit_ref[...] = (wi_ref[...] * (idx_heads ** -0.5)).T
    qib_ref[...] = (qi_ref[...] * (idx_dim ** -0.5)).astype(BF16)
    qpos = j * QB + lax.broadcasted_iota(jnp.int32, (CK, QB), 1)
    krow = lax.broadcasted_iota(jnp.int32, (CK, QB), 0)
    per_pair = LANES // idx_dim

    def score_chunk(c, carry):
        mn, mx = carry
        k0 = pl.multiple_of(c * CK, CK)
        acc = jnp.zeros((CK, QB), F32)
        for p in range(idx_heads // per_pair):
            rhs = qib_ref[:, p * LANES:(p + 1) * LANES]
            for r in range(per_pair):
                s = _dot_nt(kd_ref[r, pl.ds(k0, CK), :], rhs)
                acc = acc + wit_ref[pl.ds(idx_dim + p * per_pair + r, 1), :] * jnp.maximum(s, 0.0)
        adm = (krow + c * CK) <= qpos
        sc_ref[c] = jnp.where(adm, acc, -jnp.inf)
        mn = jnp.minimum(mn, _fold_rows(jnp.where(adm, acc, jnp.inf), jnp.minimum))
        mx = jnp.maximum(mx, _fold_rows(jnp.where(adm, acc, -jnp.inf), jnp.maximum))
        return mn, mx

    mn8, mx8 = lax.fori_loop(0, nck, score_chunk,
                             (jnp.full((8, QB), jnp.inf, F32), jnp.full((8, QB), -jnp.inf, F32)))
    row_min = jnp.min(mn8, axis=0, keepdims=True)
    row_max = jnp.max(mx8, axis=0, keepdims=True)
    n_adm = (j * QB + 1 + lax.broadcasted_iota(jnp.int32, (1, QB), 1)).astype(F32)

    def count_gt(t):
        def cbody(c, acc):
            return acc + _fold_rows(jnp.where(sc_ref[c] > t, 1.0, 0.0), jnp.add)
        part = lax.fori_loop(0, nck, cbody, jnp.zeros((8, QB), F32))
        return jnp.sum(part, axis=0, keepdims=True)

    def any_fn(mask):
        return jnp.max(jnp.where(mask, 1.0, 0.0)) > 0.0

    lo, hi, flo, fhi = _select_threshold(count_gt, row_min, row_max, n_adm, topk, any_fn)
    tie = flo != kf

    def sel_chunk(c, carry):
        sel_ref[c] = jnp.where(sc_ref[c] > lo, 1.0, 0.0)
        return carry
    lax.fori_loop(0, nck, sel_chunk, 0)

    @pl.when(any_fn(tie))
    def _():
        need = kf - fhi
        lower = (lax.broadcasted_iota(jnp.int32, (CK, CK), 1)
                 < lax.broadcasted_iota(jnp.int32, (CK, CK), 0)).astype(BF16)

        def tie_chunk(c, before):
            s = sc_ref[c]
            eq = s == hi
            eqf = jnp.where(eq, 1.0, 0.0)
            rank = before + jnp.dot(lower, eqf.astype(BF16), preferred_element_type=F32)
            keep = (s > hi) | (eq & (rank < need))
            sel_ref[c] = jnp.where(tie, jnp.where(keep, 1.0, 0.0), sel_ref[c])
            return before + jnp.sum(eqf, axis=0, keepdims=True)
        lax.fori_loop(0, nck, tie_chunk, jnp.zeros((1, QB), F32))

    scale2 = hd ** -0.5 * LOG2E
    for h in range(n_heads):
        qs_ref[h] = q_ref[:, h * hd:(h + 1) * hd].astype(BF16)
    m_ref[...] = jnp.full(m_ref.shape, NEG, F32)
    l_ref[...] = jnp.zeros(l_ref.shape, F32)
    acc_ref[...] = jnp.zeros(acc_ref.shape, F32)

    def logits(g, c):
        kc = kb_ref[g, pl.ds(pl.multiple_of(c * CK, CK), CK), :]
        qg = qs_ref[pl.ds(g * group, group)].reshape(group * QB, hd)
        st_ref[g % 2] = _dot_nt(kc, qg)

    logits(0, 0)

    def att_chunk(c, carry):
        msk = jnp.concatenate([sel_ref[c]] * group, axis=1) > 0.0
        tis = [jnp.clip(c * (CK // QB) + s - j + 2, 0, 2) for s in range(CK // QB)]
        for g in range(n_kv):
            if g + 1 < n_kv:
                logits(g + 1, c)
            else:
                logits(0, jnp.minimum(c + 1, nck - 1))
            rows = [jnp.concatenate([bias_ref[ti, g * group + hh] for hh in range(group)], axis=1)
                    for ti in tis]
            lg = jnp.where(msk, st_ref[g % 2] * scale2 + jnp.concatenate(rows, axis=0), -jnp.inf)
            m_old = m_ref[g]
            m_new = jnp.maximum(m_old, jnp.max(_fold_rows(lg, jnp.maximum), axis=0, keepdims=True))
            alpha = jnp.exp2(m_old - m_new)
            p = jnp.exp2(lg - m_new)
            l_ref[g] = alpha * l_ref[g] + jnp.sum(_fold_rows(p, jnp.add), axis=0, keepdims=True)
            pv = jnp.dot(vt_ref[g, c], p.astype(BF16), preferred_element_type=F32)
            acc_ref[g] = alpha * acc_ref[g] + pv
            m_ref[g] = m_new
        return carry
    lax.fori_loop(0, nck, att_chunk, 0)

    for g in range(n_kv):
        ot = acc_ref[g] / l_ref[g]
        for hh in range(group):
            cs = slice((g * group + hh) * hd, (g * group + hh + 1) * hd)
            o = ot[:, hh * QB:(hh + 1) * QB].T
            mixed = _sigmoid(ga_ref[:, cs]) * co_ref[:, cs] + _sigmoid(gb_ref[:, cs]) * o
            o_ref[:, cs] = mixed.astype(o_ref.dtype)


def attn_prompt(qi, kw, q, k, v, conv_out, gates, bias3, n_seq, seq, n_kv, idx_dim, idx_heads):
    m, dm = q.shape
    hd = k.shape[1] // n_kv
    n_heads = dm // hd
    group = n_heads // n_kv
    nb = seq // QB
    topk = min(TOPK_MAX, seq // 4)
    assert seq % CK == 0 and LANES % idx_dim == 0 and kw.shape[1] == LANES
    kern = functools.partial(_attn_prompt_kernel, seq=seq, n_heads=n_heads, n_kv=n_kv, idx_heads=idx_heads,
                             idx_dim=idx_dim, topk=topk, hd=hd)
    row = lambda b, j: (b * nb + j, 0)
    return pl.pallas_call(
        kern,
        grid=(n_seq, nb),
        in_specs=[pl.BlockSpec((QB, idx_heads * idx_dim), row),
                  pl.BlockSpec((QB, LANES), row),
                  pl.BlockSpec((seq, LANES), lambda b, j: (b, 0)),
                  pl.BlockSpec((QB, dm), row),
                  pl.BlockSpec((seq, n_kv * hd), lambda b, j: (b, 0)),
                  pl.BlockSpec((seq, n_kv * hd), lambda b, j: (b, 0)),
                  pl.BlockSpec((QB, dm), row),
                  pl.BlockSpec((QB, dm), lambda b, j: (b * nb + j, 0)),
                  pl.BlockSpec((QB, dm), lambda b, j: (b * nb + j, 1)),
                  pl.BlockSpec((3, n_heads, QB, QB), lambda b, j: (0, 0, 0, 0))],
        out_specs=pl.BlockSpec((QB, dm), row),
        out_shape=jax.ShapeDtypeStruct((m, dm), BF16),
        scratch_shapes=[pltpu.VMEM((2, seq, LANES), BF16),
                        pltpu.VMEM((n_kv, seq, hd), BF16),
                        pltpu.VMEM((n_kv, seq // CK, hd, CK), BF16),
                        pltpu.VMEM((LANES, QB), F32),
                        pltpu.VMEM((QB, idx_heads * idx_dim), BF16),
                        pltpu.VMEM((seq // CK, CK, QB), F32),
                        pltpu.VMEM((seq // CK, CK, QB), F32),
                        pltpu.VMEM((n_heads, QB, hd), BF16),
                        pltpu.VMEM((n_kv, 1, group * QB), F32),
                        pltpu.VMEM((n_kv, 1, group * QB), F32),
                        pltpu.VMEM((n_kv, hd, group * QB), F32),
                        pltpu.VMEM((2, CK, group * QB), F32)],
        compiler_params=_cparams(2),
        name="attn_prompt",
    )(qi, kw, kw, q, k, v, conv_out, gates, gates, bias3)


def _sample_score_kernel(pt_ref, qi_ref, wi_ref, kn_ref, *rest, pg, idx_heads, idx_dim):
    pages = rest[:pg]
    o_ref, self_ref, kcat_ref = rest[pg:]
    page = pages[0].shape[0]
    qi = qi_ref[...] * (idx_dim ** -0.5)
    wi = wi_ref[...] * (idx_heads ** -0.5)
    qb = qi.astype(BF16)
    for i in range(pg):
        kcat_ref[i * page:(i + 1) * page, :] = pages[i][...].astype(BF16)
    s = _dot_nt(qb, kcat_ref[...])
    o_ref[...] = jnp.sum(wi * jnp.maximum(s, 0.0), axis=0, keepdims=True)

    @pl.when(pl.program_id(1) == 0)
    def _():
        kn = kn_ref[...].astype(BF16).astype(F32)
        s = jnp.sum(qb.astype(F32) * kn, axis=1, keepdims=True)
        sself = jnp.sum(wi * jnp.maximum(s, 0.0), axis=0, keepdims=True)
        self_ref[...] = jnp.broadcast_to(sself, self_ref.shape)


def sample_scores(page_table, qi3, wi3, ki_new3, cache_kidx, pg):
    n, n_pages = page_table.shape
    idx_heads, idx_dim = qi3.shape[1:]
    page = cache_kidx.shape[1]
    kern = functools.partial(_sample_score_kernel, pg=pg, idx_heads=idx_heads, idx_dim=idx_dim)
    page_specs = [pl.BlockSpec((None, page, idx_dim), (lambda b, p, pt, i=i: (pt[b, p * pg + i], 0, 0)))
                  for i in range(pg)]
    grid_spec = pltpu.PrefetchScalarGridSpec(
        num_scalar_prefetch=1,
        grid=(n, n_pages // pg),
        in_specs=[pl.BlockSpec((None, idx_heads, idx_dim), lambda b, p, pt: (b, 0, 0)),
                  pl.BlockSpec((None, idx_heads, 1), lambda b, p, pt: (b, 0, 0)),
                  pl.BlockSpec((None, 1, idx_dim), lambda b, p, pt: (b, 0, 0))] + page_specs,
        out_specs=[pl.BlockSpec((None, None, 1, pg * page), lambda b, p, pt: (b, p, 0, 0)),
                   pl.BlockSpec((None, 1, LANES), lambda b, p, pt: (b, 0, 0))],
        scratch_shapes=[pltpu.VMEM((pg * page, idx_dim), BF16)],
    )
    return pl.pallas_call(
        kern,
        grid_spec=grid_spec,
        out_shape=[jax.ShapeDtypeStruct((n, n_pages // pg, 1, pg * page), F32),
                   jax.ShapeDtypeStruct((n, 1, LANES), F32)],
        compiler_params=_cparams(2),
        name="sample_scores",
    )(page_table, qi3, wi3, ki_new3, *([cache_kidx] * pg))


def _sample_select_kernel(sc_ref, self_ref, sel4_ref, selself_ref, sel_ref, *, topk, past, rep):
    sc = sc_ref[...]
    sself = self_ref[:, 0:1]
    n = sc.shape[0]
    kf = jnp.float32(topk)
    row_min = jnp.minimum(jnp.min(sc, axis=1, keepdims=True), sself)
    row_max = jnp.maximum(jnp.max(sc, axis=1, keepdims=True), sself)
    n_adm = jnp.full((n, 1), past + 1, F32)

    def count_gt(t):
        return (jnp.sum(jnp.where(sc > t, 1.0, 0.0), axis=1, keepdims=True)
                + jnp.where(sself > t, 1.0, 0.0))

    def any_fn(mask):
        return jnp.max(jnp.where(mask, 1.0, 0.0)) > 0.0

    lo, hi, flo, fhi = _select_threshold(count_gt, row_min, row_max, n_adm, topk, any_fn)
    tie = flo != kf
    sel_ref[...] = jnp.where(sc > lo, 1.0, 0.0)
    selself_ref[...] = jnp.broadcast_to(jnp.where(sself > lo, 1.0, 0.0), selself_ref.shape)

    @pl.when(any_fn(tie))
    def _():
        need = kf - fhi
        blk = 512
        tri = (lax.broadcasted_iota(jnp.int32, (blk, blk), 0)
               < lax.broadcasted_iota(jnp.int32, (blk, blk), 1)).astype(BF16)
        before = jnp.zeros((n, 1), F32)
        for c in range(past // blk):
            s = sc_ref[:, c * blk:(c + 1) * blk]
            eq = s == hi
            rank = before + jnp.dot(jnp.where(eq, 1.0, 0.0).astype(BF16), tri, preferred_element_type=F32)
            keep = (s > hi) | (eq & (rank < need))
            sel_ref[:, c * blk:(c + 1) * blk] = jnp.where(tie, jnp.where(keep, 1.0, 0.0),
                                                          sel_ref[:, c * blk:(c + 1) * blk])
            before = before + jnp.sum(jnp.where(eq, 1.0, 0.0), axis=1, keepdims=True)
        keep_self = (sself > hi) | ((sself == hi) & (before < need))
        selself_ref[...] = jnp.broadcast_to(
            jnp.where(tie, jnp.where(keep_self, 1.0, 0.0), jnp.where(sself > lo, 1.0, 0.0)), selself_ref.shape)

    blk = 512
    row_lo = lax.broadcasted_iota(jnp.int32, (blk, blk * rep), 0) * rep
    col = lax.broadcasted_iota(jnp.int32, (blk, blk * rep), 1)
    spread = jnp.where((col >= row_lo) & (col < row_lo + rep), 1.0, 0.0).astype(BF16)
    for c in range(past // blk):
        sel4_ref[:, c * blk * rep:(c + 1) * blk * rep] = jnp.dot(
            sel_ref[:, c * blk:(c + 1) * blk].astype(BF16), spread, preferred_element_type=F32)


def sample_select(scores, sself, topk, rep):
    n, past = scores.shape
    kern = functools.partial(_sample_select_kernel, topk=topk, past=past, rep=rep)
    return pl.pallas_call(
        kern,
        out_shape=[jax.ShapeDtypeStruct((n, past * rep), F32), jax.ShapeDtypeStruct((n, LANES), F32)],
        scratch_shapes=[pltpu.VMEM((n, past), F32)],
        compiler_params=pltpu.CompilerParams(vmem_limit_bytes=VMEM_LIMIT),
        name="sample_select",
    )(scores, sself)


def _sample_attn_kernel(pt_ref, q_ref, kn_ref, vn_ref, sel_ref, selself_ref, bias_ref, rb0_ref, own_ref, *rest,
                        pg, n_heads, n_kv, hd):
    kpages = rest[:pg]
    vpages = rest[pg:2 * pg]
    o_ref = rest[2 * pg]
    kcat_ref, vcat_ref, m_ref, l_ref, acc_ref = rest[2 * pg + 1:]
    p = pl.program_id(1)
    n_steps = pl.num_programs(1)
    scale = hd ** -0.5
    rows = kpages[0].shape[0]

    @pl.when(p == 0)
    def _():
        m_ref[...] = jnp.full(m_ref.shape, NEG, F32)
        l_ref[...] = jnp.zeros(l_ref.shape, F32)
        acc_ref[...] = jnp.zeros(acc_ref.shape, F32)

    for i in range(pg):
        kcat_ref[i * rows:(i + 1) * rows, :] = kpages[i][...].astype(BF16)
        vcat_ref[i * rows:(i + 1) * rows, :] = vpages[i][...].astype(BF16)
    qb = q_ref[...].astype(BF16)
    last = p == n_steps - 1
    bias = jnp.concatenate([bias_ref[0]] * (pg - 1) + [jnp.where(last, bias_ref[1], bias_ref[0])], axis=1)
    msk = (own_ref[...] > 0.0) & (sel_ref[...] > 0.0)
    lg = jnp.where(msk, _dot_nt(qb, kcat_ref[...]) * scale + bias, -jnp.inf)
    m_old = m_ref[...]
    m_new = jnp.maximum(m_old, jnp.max(lg, axis=-1, keepdims=True))
    alpha = jnp.exp(m_old - m_new)
    pr = jnp.exp(lg - m_new)
    l_new = alpha * l_ref[...] + jnp.sum(pr, axis=-1, keepdims=True)
    acc = alpha * acc_ref[...] + jnp.dot(pr.astype(BF16), vcat_ref[...], preferred_element_type=F32)
    m_ref[...] = m_new
    l_ref[...] = l_new
    acc_ref[...] = acc

    @pl.when(last)
    def _():
        kn = kn_ref[...].astype(BF16).astype(F32)
        vn = vn_ref[...].astype(BF16).astype(F32)
        ls = jnp.sum(qb.astype(F32) * kn, axis=-1, keepdims=True) * scale + rb0_ref[...]
        on = selself_ref[:, 0:1] > 0.0
        ls = jnp.where(on, ls, NEG)
        m_f = jnp.maximum(m_new, ls)
        a2 = jnp.exp(m_new - m_f)
        ps = jnp.where(on, jnp.exp(ls - m_f), 0.0)
        l_f = a2 * l_new + ps
        acc_f = a2 * acc + ps.astype(BF16).astype(F32) * vn
        o_ref[...] = acc_f / l_f


def sample_attention(page_table, q3, k_rep, v_rep, sel4, selself3, bias_s, rb0, cache_k, cache_v, pg, n_kv):
    n, n_pages = page_table.shape
    n_heads, hd = q3.shape[1:]
    rows = cache_k.shape[1]
    kern = functools.partial(_sample_attn_kernel, pg=pg, n_heads=n_heads, n_kv=n_kv, hd=hd)
    own = (np.arange(pg * rows)[None, :] % n_kv
           == np.arange(n_heads)[:, None] // (n_heads // n_kv)).astype(np.float32)
    kspecs = [pl.BlockSpec((None, rows, hd), (lambda b, p, pt, i=i: (pt[b, p * pg + i], 0, 0)))
              for i in range(pg)]
    per_seq = lambda b, p, pt: (b, 0, 0)
    grid_spec = pltpu.PrefetchScalarGridSpec(
        num_scalar_prefetch=1,
        grid=(n, n_pages // pg),
        in_specs=[pl.BlockSpec((None, n_heads, hd), per_seq),
                  pl.BlockSpec((None, n_heads, hd), per_seq),
                  pl.BlockSpec((None, n_heads, hd), per_seq),
                  pl.BlockSpec((None, None, 1, pg * rows), lambda b, p, pt: (b, p, 0, 0)),
                  pl.BlockSpec((None, 1, LANES), per_seq),
                  pl.BlockSpec((2, n_heads, rows), lambda b, p, pt: (0, 0, 0)),
                  pl.BlockSpec((n_heads, 1), lambda b, p, pt: (0, 0)),
                  pl.BlockSpec((n_heads, pg * rows), lambda b, p, pt: (0, 0))] + kspecs + kspecs,
        out_specs=pl.BlockSpec((None, n_heads, hd), per_seq),
        scratch_shapes=[pltpu.VMEM((pg * rows, hd), BF16), pltpu.VMEM((pg * rows, hd), BF16),
                        pltpu.VMEM((n_heads, 1), F32), pltpu.VMEM((n_heads, 1), F32),
                        pltpu.VMEM((n_heads, hd), F32)],
    )
    return pl.pallas_call(
        kern,
        grid_spec=grid_spec,
        out_shape=jax.ShapeDtypeStruct((n, n_heads, hd), F32),
        compiler_params=_cparams(2),
        name="sample_attn",
    )(page_table, q3, k_rep, v_rep, sel4, selself3, bias_s, rb0, jnp.asarray(own), *([cache_k] * pg),
      *([cache_v] * pg))


def _mix_kernel(ga_ref, gb_ref, co_ref, at_ref, o_ref):
    o_ref[...] = (_sigmoid(ga_ref[...]) * co_ref[...] + _sigmoid(gb_ref[...]) * at_ref[...]).astype(o_ref.dtype)


def gated_mix(gates, conv_out, attn):
    n, dm = conv_out.shape
    return pl.pallas_call(
        _mix_kernel,
        grid=(1,),
        in_specs=[pl.BlockSpec((n, dm), lambda i: (0, 0)), pl.BlockSpec((n, dm), lambda i: (0, 1)),
                  pl.BlockSpec((n, dm), lambda i: (0, 0)), pl.BlockSpec((n, dm), lambda i: (0, 0))],
        out_specs=pl.BlockSpec((n, dm), lambda i: (0, 0)),
        out_shape=jax.ShapeDtypeStruct((n, dm), BF16),
    )(gates, gates, conv_out, attn)


def _ffn_up_kernel(x_ref, xh_ref, wg_ref, wv_ref, dg_ref, dv_ref, bg_ref, bv_ref, pg_ref, pv_ref,
                   o_ref, sg_ref, sv_ref, eg_ref, ev_ref, *, tm, rs, width, tiles_per_seq):
    hist = width - 1
    pad = xh_ref.shape[0]
    first = pl.program_id(0) % tiles_per_seq == 0
    branches = ((wg_ref, dg_ref, bg_ref, pg_ref, eg_ref, sg_ref),
                (wv_ref, dv_ref, bv_ref, pv_ref, ev_ref, sv_ref))
    for w_ref, _, _, prev_ref, e_ref, _ in branches:
        e_ref[pl.ds(0, pad), :] = jnp.dot(xh_ref[...], w_ref[...], preferred_element_type=F32)

        @pl.when(first)
        def _():
            e_ref[pl.ds(pad - hist, hist), :] = prev_ref[...]

    def project(r):
        for w_ref, _, _, _, e_ref, _ in branches:
            e_ref[pl.ds(pad + r * rs, rs), :] = jnp.dot(x_ref[pl.ds(r * rs, rs), :], w_ref[...],
                                                        preferred_element_type=F32)

    def activate(r):
        outs = []
        for _, d_ref, b_ref, _, e_ref, _ in branches:
            acc = b_ref[...] + d_ref[pl.ds(hist, 1), :] * e_ref[pl.ds(pad + r * rs, rs), :]
            for j in range(hist):
                acc = acc + d_ref[pl.ds(j, 1), :] * e_ref[pl.ds(pad + r * rs - hist + j, rs), :]
            outs.append(acc)
        g, v = outs
        o_ref[pl.ds(r * rs, rs), :] = (g * _sigmoid(g) * v).astype(o_ref.dtype)

    n_sub = tm // rs
    project(0)
    for r in range(1, n_sub):
        project(r)
        activate(r - 1)
    activate(n_sub - 1)
    for _, _, _, _, e_ref, s_ref in branches:
        s_ref[...] = e_ref[pl.ds(pad + tm - hist, hist), :]


def ffn_up_act(xn, w_up, prev, dw, bdw, n_seq, seq, tm, tn):
    m, k = xn.shape
    f2 = w_up.shape[1]
    f = f2 // 2
    width = dw.shape[0]
    ncb = f // tn
    pad = 16
    tps = seq // tm
    assert seq % tm == 0 and tm % pad == 0 and f % tn == 0 and width - 1 <= pad
    rs = min(256, tm // 2)
    assert tm % rs == 0 and rs % 8 == 0 and k % 256 == 0
    kern = functools.partial(_ffn_up_kernel, tm=tm, rs=rs, width=width, tiles_per_seq=tps)
    hb = tm // pad
    st_shape = jax.ShapeDtypeStruct((m // tm, width - 1, f), F32)
    act, sg, sv = pl.pallas_call(
        kern,
        grid=(m // tm, ncb),
        in_specs=[pl.BlockSpec((tm, k), lambda i, c: (i, 0)),
                  pl.BlockSpec((pad, k), lambda i, c: (jnp.maximum(i * hb - 1, 0), 0)),
                  pl.BlockSpec((k, tn), lambda i, c: (0, c)),
                  pl.BlockSpec((k, tn), lambda i, c: (0, c + ncb)),
                  pl.BlockSpec((width, tn), lambda i, c: (0, c)),
                  pl.BlockSpec((width, tn), lambda i, c: (0, c + ncb)),
                  pl.BlockSpec((1, tn), lambda i, c: (0, c)),
                  pl.BlockSpec((1, tn), lambda i, c: (0, c + ncb)),
                  pl.BlockSpec((None, width - 1, tn), lambda i, c: (i // tps, 0, c)),
                  pl.BlockSpec((None, width - 1, tn), lambda i, c: (i // tps, 0, c + ncb))],
        out_specs=[pl.BlockSpec((tm, tn), lambda i, c: (i, c)),
                   pl.BlockSpec((None, width - 1, tn), lambda i, c: (i, 0, c)),
                   pl.BlockSpec((None, width - 1, tn), lambda i, c: (i, 0, c))],
        out_shape=[jax.ShapeDtypeStruct((m, f), BF16), st_shape, st_shape],
        scratch_shapes=[pltpu.VMEM((pad + tm, tn), F32), pltpu.VMEM((pad + tm, tn), F32)],
        compiler_params=_cparams(2),
        name="ffn_up_act",
    )(xn, xn, w_up, w_up, dw, dw, bdw.reshape(1, f2), bdw.reshape(1, f2), prev, prev)
    state = jnp.concatenate([sg, sv], axis=-1).reshape(n_seq, tps, width - 1, f2)[:, tps - 1]
    return act, state


def _ffn_act_step_kernel(ug_ref, uv_ref, pg_ref, pv_ref, wg_ref, wv_ref, bg_ref, bv_ref, o_ref, *, width):
    def conv(u_ref, prev_ref, w_ref, b_ref):
        acc = b_ref[...] + w_ref[pl.ds(width - 1, 1), :] * u_ref[...]
        for j in range(width - 1):
            acc = acc + w_ref[pl.ds(j, 1), :] * prev_ref[j]
        return acc
    g = conv(ug_ref, pg_ref, wg_ref, bg_ref)
    v = conv(uv_ref, pv_ref, wv_ref, bv_ref)
    o_ref[...] = (g * _sigmoid(g) * v).astype(o_ref.dtype)


def ffn_act_step(u, prev_t, dw, bdw, tc):
    n, f2 = u.shape
    f = f2 // 2
    width = dw.shape[0]
    ncb = f // tc
    kern = functools.partial(_ffn_act_step_kernel, width=width)
    return pl.pallas_call(
        kern,
        grid=(ncb,),
        in_specs=[pl.BlockSpec((n, tc), lambda c: (0, c)),
                  pl.BlockSpec((n, tc), lambda c: (0, c + ncb)),
                  pl.BlockSpec((width - 1, n, tc), lambda c: (0, 0, c)),
                  pl.BlockSpec((width - 1, n, tc), lambda c: (0, 0, c + ncb)),
                  pl.BlockSpec((width, tc), lambda c: (0, c)),
                  pl.BlockSpec((width, tc), lambda c: (0, c + ncb)),
                  pl.BlockSpec((1, tc), lambda c: (0, c)),
                  pl.BlockSpec((1, tc), lambda c: (0, c + ncb))],
        out_specs=pl.BlockSpec((n, tc), lambda c: (0, c)),
        out_shape=jax.ShapeDtypeStruct((n, f), BF16),
        compiler_params=_cparams(1),
    )(u, u, prev_t, prev_t, dw, dw, bdw.reshape(1, f2), bdw.reshape(1, f2))


def _row_tile(m, cap):
    tm = cap
    while m % tm:
        tm //= 2
    return tm if tm >= 16 else m


def _in_projection(xn, w_main, w_kw, w_g, sizes, tm):
    glu_w, q_w, k_w, v_w, qi_w, ki_w, wi_w, ga_w, gb_w = sizes
    offs = np.concatenate([[0], np.cumsum(sizes)])
    tn = 512
    glu = matmul_w(xn, w_main, int(offs[0]), glu_w, tm, tn, name="mm_glu")
    q = matmul_w(xn, w_main, int(offs[1]), q_w, tm, tn, name="mm_q", out_dtype=BF16)
    k = matmul_w(xn, w_main, int(offs[2]), k_w, tm, min(tn, k_w), name="mm_k")
    v = matmul_w(xn, w_main, int(offs[3]), v_w, tm, min(tn, v_w), name="mm_v")
    qi = matmul_w(xn, w_main, int(offs[4]), qi_w, tm, min(tn, qi_w), name="mm_qi", out_dtype=BF16)
    kw = matmul_w(xn, w_kw, 0, LANES, tm, LANES, name="mm_kw")
    gates = matmul_w(xn, w_g, 0, ga_w + gb_w, tm, tn, name="mm_gates")
    return glu, q, k, v, qi, kw, gates


def kernel(x_prompt, x_sample, cache_k, cache_v, cache_kidx, state_conv, state_ffn, page_table, rel_bias,
           norm_attn, w_in, dw_conv, b_dw_conv, ln_conv_g, ln_conv_b, w_conv_out, w_o, norm_ffn, w_up, dw_ffn,
           b_dw_ffn, w_down, norm_final):
    bsz, seq, dm = x_prompt.shape
    nd, dec_seq, _ = x_sample.shape
    depth, n_pool, page, n_kv, hd = cache_k.shape
    idx_dim = cache_kidx.shape[-1]
    n_pages = page_table.shape[1]
    past = n_pages * page
    width, dconv = dw_conv.shape[1:]
    fwidth = dw_ffn.shape[1]
    f = w_down.shape[1]
    n_heads = w_o.shape[1] // hd
    d_attn = n_heads * hd
    d_kv = n_kv * hd
    n_in = w_in.shape[2]
    idx_heads = (n_in - 2 * dconv - d_attn - 2 * d_kv - idx_dim - 2 * dm) // (idx_dim + 1)
    sizes = (2 * dconv, d_attn, d_kv, d_kv, idx_heads * idx_dim, idx_dim, idx_heads, dm, dm)
    assert sum(sizes) == n_in and depth == 1 and dec_seq == 1 and page == LANES and d_attn == dm

    mp = bsz * seq
    xp = x_prompt.reshape(mp, dm)
    xs = x_sample.reshape(nd, dm)
    bias3, bias_s = bias_tables(rel_bias, page, n_kv)
    tmp = _row_tile(mp, 2048)
    drop = lambda a: a.reshape(a.shape[1:])
    (norm_attn, w_in, dw_conv, b_dw_conv, ln_conv_g, ln_conv_b, w_conv_out, w_o, norm_ffn, w_up, dw_ffn,
     b_dw_ffn, w_down, state_conv, state_ffn) = map(drop, (
         norm_attn, w_in, dw_conv, b_dw_conv, ln_conv_g, ln_conv_b, w_conv_out, w_o, norm_ffn, w_up, dw_ffn,
         b_dw_ffn, w_down, state_conv, state_ffn))
    kidx_pool = cache_kidx.reshape(n_pool, page, idx_dim)
    k_pool = cache_k.reshape(n_pool, page * n_kv, hd)
    v_pool = cache_v.reshape(n_pool, page * n_kv, hd)

    n_aligned = sum(sizes[:5])
    w_main = cast_bf16(w_in, n_aligned)
    w_kw, w_g = realign_tail(w_in, n_aligned, idx_dim + idx_heads, 2 * dm)
    w_o, w_up, w_down = cast_bf16(w_o), cast_bf16(w_up), cast_bf16(w_down)

    xn = rmsnorm_rows(xp, norm_attn, BF16, 512)
    glu, q, k, v, qi, kw, gates = _in_projection(xn, w_main, w_kw, w_g, sizes, tmp)
    conv0 = jnp.zeros((bsz, width - 1, dconv), F32)
    conv_out, conv_state_p = conv_branch_prompt(glu, conv0, dw_conv, b_dw_conv, ln_conv_g, ln_conv_b,
                                                w_conv_out, bsz, seq, 256)
    mixed = attn_prompt(qi, kw, q, k, v, conv_out, gates, bias3, bsz, seq, n_kv, idx_dim, idx_heads)
    x2 = matmul_w(mixed, w_o, 0, dm, tmp, 512, res=xp, name="mm_o")
    xn2 = rmsnorm_rows(x2, norm_ffn, BF16, 512)
    ffn0 = jnp.zeros((bsz, fwidth - 1, 2 * f), F32)
    act, ffn_state_p = ffn_up_act(xn2, w_up, ffn0, dw_ffn, b_dw_ffn, bsz, seq, _row_tile(seq, 2048), 512)
    x3 = matmul_w(act, w_down, 0, dm, _row_tile(mp, 1024), 256, res=x2, name="mm_down")
    y_prompt = rmsnorm_rows(x3, norm_final, F32, 512).reshape(bsz, seq, dm)

    xns = rmsnorm_rows(xs, norm_attn, BF16, nd)
    glu_s, q_s, k_s, v_s, qi_s, kw_s, gates_s = _in_projection(xns, w_main, w_kw, w_g, sizes, nd)
    ki_s = kw_s[:, :idx_dim]
    wi_s = kw_s[:, idx_dim:idx_dim + idx_heads]
    sc_prev_t = jnp.swapaxes(state_conv, 0, 1)
    conv_out_s, u_conv_s = conv_branch_step(glu_s, sc_prev_t, dw_conv, b_dw_conv, ln_conv_g, ln_conv_b,
                                            w_conv_out)
    conv_state_s = jnp.concatenate([state_conv[:, 1:], u_conv_s[:, None, :]], axis=1)

    pg = 16 if n_pages % 16 == 0 else 8
    scores3, sself3 = sample_scores(page_table, qi_s.reshape(nd, idx_heads, idx_dim),
                                    wi_s.reshape(nd, idx_heads, 1), ki_s.reshape(nd, 1, idx_dim), kidx_pool, pg)
    topk_s = min(TOPK_MAX, (past + dec_seq) // 4)
    sel4, selself = sample_select(scores3.reshape(nd, past), sself3.reshape(nd, LANES), topk_s, n_kv)
    group = n_heads // n_kv
    k_rep = jnp.repeat(k_s.reshape(nd, n_kv, hd), group, axis=1)
    v_rep = jnp.repeat(v_s.reshape(nd, n_kv, hd), group, axis=1)
    attn_s = sample_attention(page_table, q_s.reshape(nd, n_heads, hd), k_rep, v_rep,
                              sel4.reshape(nd, n_pages // pg, 1, pg * page * n_kv), selself.reshape(nd, 1, LANES),
                              bias_s,
                              rel_bias[0].reshape(n_heads, 1), k_pool, v_pool, pg, n_kv)
    mixed_s = gated_mix(gates_s, conv_out_s, attn_s.reshape(nd, dm))
    x2s = matmul_w(mixed_s, w_o, 0, dm, nd, 512, res=xs)
    xn2s = rmsnorm_rows(x2s, norm_ffn, BF16, nd)
    u_s = matmul_w(xn2s, w_up, 0, 2 * f, nd, 512)
    sf_prev_t = jnp.swapaxes(state_ffn, 0, 1)
    act_s = ffn_act_step(u_s, sf_prev_t, dw_ffn, b_dw_ffn, 512)
    x3s = matmul_w(act_s, w_down, 0, dm, nd, 256, res=x2s)
    y_sample = rmsnorm_rows(x3s, norm_final, F32, nd).reshape(nd, dec_seq, dm)
    ffn_state_s = jnp.concatenate([state_ffn[:, 1:], u_s[:, None, :]], axis=1)

    return (y_prompt, y_sample,
            k.reshape(1, bsz, seq, n_kv, hd), v.reshape(1, bsz, seq, n_kv, hd),
            kw.reshape(bsz, seq, LANES)[None, :, :, :idx_dim],
            conv_state_p[None], ffn_state_p[None],
            k_s.reshape(1, nd, dec_seq, n_kv, hd), v_s.reshape(1, nd, dec_seq, n_kv, hd),
            ki_s.reshape(1, nd, dec_seq, idx_dim),
            conv_state_s[None], ffn_state_s[None])
```

```python
import functools
import math

import numpy as np
import jax
import jax.numpy as jnp
from jax import lax
from jax.experimental import pallas as pl
from jax.experimental.pallas import tpu as pltpu

F32 = jnp.float32
BF16 = jnp.bfloat16

EPS = 1e-6
TOPK_MAX = 256
N_BUCKETS = 32
MAX_DISTANCE = 128
QB = 128
CK = 256
LANES = 128
NEG = -1e30
LOG2E = math.log2(math.e)
VMEM_LIMIT = 56 * 1024 * 1024


def _cparams(n_axes, vmem=VMEM_LIMIT):
    return pltpu.CompilerParams(dimension_semantics=("arbitrary",) * n_axes, vmem_limit_bytes=vmem)


def _dot_nt(a, b):
    return lax.dot_general(a, b, (((1,), (1,)), ((), ())), preferred_element_type=F32)


def _sigmoid(x):
    return 1.0 / (1.0 + jnp.exp(-x))


def _fold_rows(x, op):
    while x.shape[0] > 8:
        half = x.shape[0] // 2
        x = op(x[:half], x[half:])
    return x


def _rel_bucket_np(dist):
    n = np.maximum(dist, 0)
    max_exact = N_BUCKETS // 2
    nf = np.maximum(n, 1).astype(np.float32)
    large = max_exact + (np.log(nf / np.float32(max_exact)) / np.float32(math.log(MAX_DISTANCE / max_exact))
                         * np.float32(N_BUCKETS - max_exact)).astype(np.int32)
    large = np.minimum(large, N_BUCKETS - 1)
    return np.where(n < max_exact, n, large).astype(np.int32)


def _rms_kernel(x_ref, g_ref, o_ref):
    x = x_ref[...]
    y = x * lax.rsqrt(jnp.mean(x * x, axis=-1, keepdims=True) + EPS) * g_ref[...]
    o_ref[...] = y.astype(o_ref.dtype)


def rmsnorm_rows(x, g, out_dtype, tm):
    m, d = x.shape
    return pl.pallas_call(
        _rms_kernel,
        grid=(m // tm,),
        in_specs=[pl.BlockSpec((tm, d), lambda i: (i, 0)), pl.BlockSpec((1, d), lambda i: (0, 0))],
        out_specs=pl.BlockSpec((tm, d), lambda i: (i, 0)),
        out_shape=jax.ShapeDtypeStruct((m, d), out_dtype),
        compiler_params=_cparams(1),
        name="rmsnorm",
    )(x, g.reshape(1, d))


def _cast_kernel(w_ref, o_ref):
    o_ref[...] = w_ref[...].astype(o_ref.dtype)


def cast_bf16(w, ncols=None):
    k, n = w.shape
    ncols = n if ncols is None else ncols
    tk = 512 if k % 512 == 0 else k
    tn = 1024 if ncols % 1024 == 0 else (512 if ncols % 512 == 0 else ncols)
    return pl.pallas_call(
        _cast_kernel,
        grid=(k // tk, ncols // tn),
        in_specs=[pl.BlockSpec((tk, tn), lambda i, j: (i, j))],
        out_specs=pl.BlockSpec((tk, tn), lambda i, j: (i, j)),
        out_shape=jax.ShapeDtypeStruct((k, ncols), BF16),
        compiler_params=_cparams(2),
        name="cast_bf16",
    )(w)


def _cast_rows_kernel(w_ref, o_ref, *, valid):
    w = w_ref[...]
    if valid < w.shape[0]:
        w = jnp.where(lax.broadcasted_iota(jnp.int32, w.shape, 0) < valid, w, 0.0)
    o_ref[...] = w.astype(o_ref.dtype)


def cast_rows_bf16(wt, row0, nrows, nvalid=None):
    n, k = wt.shape
    tr = 512 if nrows % 512 == 0 else nrows
    nvalid = nrows if nvalid is None else nvalid
    assert row0 % 8 == 0 and (nvalid == nrows or tr == nrows) and row0 + nrows <= n
    return pl.pallas_call(
        functools.partial(_cast_rows_kernel, valid=nvalid),
        grid=(nrows // tr,),
        in_specs=[pl.BlockSpec((pl.Element(tr), pl.Element(k)), lambda i: (pl.multiple_of(row0 + i * tr, 8), 0))],
        out_specs=pl.BlockSpec((tr, k), lambda i: (i, 0)),
        out_shape=jax.ShapeDtypeStruct((nrows, k), BF16),
        compiler_params=_cparams(1),
        name="cast_rows_bf16",
    )(wt)


def _mm_kernel(a_ref, w_ref, o_ref):
    o_ref[...] = jnp.dot(a_ref[...], w_ref[...], preferred_element_type=F32).astype(o_ref.dtype)


def _mm_res_kernel(a_ref, w_ref, r_ref, o_ref):
    o_ref[...] = r_ref[...] + jnp.dot(a_ref[...], w_ref[...], preferred_element_type=F32)


def _mm_nt_kernel(a_ref, wt_ref, o_ref):
    o_ref[...] = _dot_nt(a_ref[...], wt_ref[...]).astype(o_ref.dtype)


def matmul_w(a, w, col0, ncols, tm, tn, res=None, name="matmul", out_dtype=F32, w_transposed=False):
    m, k = a.shape
    assert col0 % tn == 0 and ncols % tn == 0 and m % tm == 0
    cb = col0 // tn
    if w_transposed:
        assert res is None
        w_spec = pl.BlockSpec((tn, k), lambda i, j: (j + cb, 0))
    else:
        w_spec = pl.BlockSpec((k, tn), lambda i, j: (0, j + cb))
    in_specs = [pl.BlockSpec((tm, k), lambda i, j: (i, 0)), w_spec]
    args = [a, w]
    kern = _mm_nt_kernel if w_transposed else _mm_kernel
    if res is not None:
        in_specs.append(pl.BlockSpec((tm, tn), lambda i, j: (i, j)))
        args.append(res)
        kern = _mm_res_kernel
    return pl.pallas_call(
        kern,
        grid=(m // tm, ncols // tn),
        in_specs=in_specs,
        out_specs=pl.BlockSpec((tm, tn), lambda i, j: (i, j)),
        out_shape=jax.ShapeDtypeStruct((m, ncols), out_dtype),
        compiler_params=_cparams(2),
        name=name,
    )(*args)


def _bias_kernel(rb_ref, bk3_ref, bks_ref, o3_ref, os_ref, *, n_heads):
    def head(h, carry):
        for t in range(3):
            bk = bk3_ref[t]
            acc = jnp.zeros(bk.shape, F32)
            for b in range(N_BUCKETS):
                acc = jnp.where(bk == b, rb_ref[b, h], acc)
            o3_ref[t, h] = acc * LOG2E
        for t in range(2):
            bk = bks_ref[t]
            acc = jnp.zeros(bk.shape, F32)
            for b in range(N_BUCKETS):
                acc = jnp.where(bk == b, rb_ref[b, h], acc)
            os_ref[t, pl.ds(h, 1), :] = acc
        return carry
    lax.fori_loop(0, n_heads, head, 0)


def bias_tables(rel_bias, page, rep):
    n_heads = rel_bias.shape[1]
    cols = page * rep
    i = np.arange(QB)[None, :]
    k = np.arange(QB)[:, None]
    bk3 = np.stack([_rel_bucket_np(i - k + 2 * QB), _rel_bucket_np(i - k + QB), _rel_bucket_np(i - k)])
    assert (_rel_bucket_np(np.arange(QB + 1, 1 << 20)) == N_BUCKETS - 1).all()
    assert (bk3[0] == N_BUCKETS - 1).all()
    assert page >= QB
    bks = np.stack([np.full((1, cols), N_BUCKETS - 1, np.int32),
                    _rel_bucket_np(page - np.arange(cols) // rep)[None, :]])
    return pl.pallas_call(
        functools.partial(_bias_kernel, n_heads=n_heads),
        in_specs=[pl.BlockSpec(memory_space=pltpu.SMEM),
                  pl.BlockSpec(memory_space=pltpu.VMEM), pl.BlockSpec(memory_space=pltpu.VMEM)],
        out_specs=[pl.BlockSpec(memory_space=pltpu.VMEM), pl.BlockSpec(memory_space=pltpu.VMEM)],
        out_shape=[jax.ShapeDtypeStruct((3, n_heads, QB, QB), F32),
                   jax.ShapeDtypeStruct((2, n_heads, cols), F32)],
    )(rel_bias, jnp.asarray(bk3), jnp.asarray(bks))


def _conv_kernel(glu_ref, prev_ref, dw_ref, bdw_ref, lng_ref, lnb_ref, wout_ref, o_ref, st_ref,
                 ext_ref, h_ref, wbf_ref, *, tt, width, dconv):
    b = pl.program_id(0)
    t = pl.program_id(1)
    pad = 32
    hist = width - 1

    @pl.when((b == 0) & (t == 0))
    def _():
        wbf_ref[...] = wout_ref[...].astype(BF16)

    @pl.when(t == 0)
    def _():
        ext_ref[pl.ds(pad - hist, hist), :] = prev_ref[...]

    @pl.when(t > 0)
    def _():
        ext_ref[pl.ds(0, pad), :] = ext_ref[pl.ds(tt, pad), :]

    glu = glu_ref[...]
    u = glu[:, :dconv] * _sigmoid(glu[:, dconv:])
    ext_ref[pl.ds(pad, tt), :] = u
    st_ref[...] = ext_ref[pl.ds(pad + tt - hist, hist), :]

    for c in range(dconv // LANES):
        cs = slice(c * LANES, (c + 1) * LANES)
        acc = jnp.zeros((tt, LANES), F32) + bdw_ref[:, cs]
        for j in range(width):
            acc = acc + dw_ref[pl.ds(j, 1), cs] * ext_ref[pl.ds(pad - hist + j, tt), cs]
        h_ref[:, cs] = acc

    h = h_ref[...]
    mu = jnp.mean(h, axis=-1, keepdims=True)
    var = jnp.mean(jnp.square(h - mu), axis=-1, keepdims=True)
    y = (h - mu) * lax.rsqrt(var + EPS) * lng_ref[...] + lnb_ref[...]
    y = y * _sigmoid(y)
    o_ref[...] = jnp.dot(y.astype(BF16), wbf_ref[...], preferred_element_type=F32)


def conv_branch_prompt(glu_pre, prev, dw, bdw, lng, lnb, wout, n_seq, seq, tt):
    width, dconv = dw.shape
    dm = wout.shape[1]
    nt = seq // tt
    kern = functools.partial(_conv_kernel, tt=tt, width=width, dconv=dconv)
    return pl.pallas_call(
        kern,
        grid=(n_seq, nt),
        in_specs=[pl.BlockSpec((tt, 2 * dconv), lambda b, t: (b * nt + t, 0)),
                  pl.BlockSpec((None, width - 1, dconv), lambda b, t: (b, 0, 0)),
                  pl.BlockSpec((width, dconv), lambda b, t: (0, 0)),
                  pl.BlockSpec((1, dconv), lambda b, t: (0, 0)),
                  pl.BlockSpec((1, dconv), lambda b, t: (0, 0)),
                  pl.BlockSpec((1, dconv), lambda b, t: (0, 0)),
                  pl.BlockSpec((dconv, dm), lambda b, t: (0, 0))],
        out_specs=[pl.BlockSpec((tt, dm), lambda b, t: (b * nt + t, 0)),
                   pl.BlockSpec((None, width - 1, dconv), lambda b, t: (b, 0, 0))],
        out_shape=[jax.ShapeDtypeStruct((n_seq * seq, dm), F32),
                   jax.ShapeDtypeStruct((n_seq, width - 1, dconv), F32)],
        scratch_shapes=[pltpu.VMEM((32 + tt, dconv), F32), pltpu.VMEM((tt, dconv), F32),
                        pltpu.VMEM((dconv, dm), BF16)],
        compiler_params=_cparams(2),
        name="conv_branch",
    )(glu_pre, prev, dw, bdw.reshape(1, dconv), lng.reshape(1, dconv), lnb.reshape(1, dconv), wout)


def _conv_step_kernel(glu_ref, prev_ref, dw_ref, bdw_ref, lng_ref, lnb_ref, wout_ref, o_ref, u_ref,
                      *, width, dconv):
    glu = glu_ref[...]
    u = glu[:, :dconv] * _sigmoid(glu[:, dconv:])
    u_ref[...] = u
    h = bdw_ref[...] + dw_ref[pl.ds(width - 1, 1), :] * u
    for j in range(width - 1):
        h = h + dw_ref[pl.ds(j, 1), :] * prev_ref[j]
    mu = jnp.mean(h, axis=-1, keepdims=True)
    var = jnp.mean(jnp.square(h - mu), axis=-1, keepdims=True)
    y = (h - mu) * lax.rsqrt(var + EPS) * lng_ref[...] + lnb_ref[...]
    y = y * _sigmoid(y)
    o_ref[...] = jnp.dot(y.astype(BF16), wout_ref[...].astype(BF16), preferred_element_type=F32)


def conv_branch_step(glu_pre, prev_t, dw, bdw, lng, lnb, wout):
    width, dconv = dw.shape
    n = glu_pre.shape[0]
    dm = wout.shape[1]
    kern = functools.partial(_conv_step_kernel, width=width, dconv=dconv)
    return pl.pallas_call(
        kern,
        out_shape=[jax.ShapeDtypeStruct((n, dm), F32), jax.ShapeDtypeStruct((n, dconv), F32)],
        compiler_params=pltpu.CompilerParams(vmem_limit_bytes=VMEM_LIMIT),
    )(glu_pre, prev_t, dw, bdw.reshape(1, dconv), lng.reshape(1, dconv), lnb.reshape(1, dconv), wout)


def _select_threshold(count_gt, row_min, row_max, n_adm, topk, any_fn):
    kf = jnp.float32(topk)
    full = n_adm <= kf
    lo0 = row_min - (1.0 + jnp.abs(row_min))
    hi0 = row_max
    flo0 = jnp.where(full, kf, n_adm)
    fhi0 = jnp.zeros_like(lo0)

    def active_rows(lo, hi, flo):
        mid = 0.5 * lo + 0.5 * hi
        return (flo != kf) & (lo < mid) & (mid < hi)

    def cond(st):
        lo, hi, flo, fhi = st
        return any_fn(active_rows(lo, hi, flo))

    def step(st, interpolate):
        lo, hi, flo, fhi = st
        act = active_rows(lo, hi, flo)
        mid = 0.5 * lo + 0.5 * hi
        if interpolate:
            guess = lo + (hi - lo) * ((flo - kf) / (flo - fhi))
            mid = jnp.where((lo < guess) & (guess < hi), guess, mid)
        c = count_gt(mid)
        up = act & (c >= kf)
        dn = act & (c < kf)
        return (jnp.where(up, mid, lo), jnp.where(dn, mid, hi),
                jnp.where(up, c, flo), jnp.where(dn, c, fhi))

    def body(st):
        return step(step(st, True), False)

    lo, hi, flo, fhi = lax.while_loop(cond, body, (lo0, hi0, flo0, fhi0))
    lo = jnp.where(full, -jnp.inf, lo)
    return lo, hi, flo, fhi


def _attn_prompt_kernel(qi_ref, wi_ref, kw_ref, q_ref, k_ref, v_ref, co_ref, ga_ref, gb_ref, bias_ref,
                        o_ref, kd_ref, kb_ref, vt_ref, wit_ref, qib_ref, sc_ref, sel_ref, qs_ref,
                        m_ref, l_ref, acc_ref, st_ref,
                        *, seq, n_heads, n_kv, idx_heads, idx_dim, topk, hd):
    j = pl.program_id(1)
    group = n_heads // n_kv
    nck = (j * QB + QB + CK - 1) // CK
    lane = lax.broadcasted_iota(jnp.int32, (1, LANES), 1)
    kf = jnp.float32(topk)

    @pl.when(j == 0)
    def _():
        kw = kw_ref[...]
        kd_ref[0] = jnp.where(lane < idx_dim, kw, 0.0).astype(BF16)
        kd_ref[1] = jnp.where(lane >= idx_dim, pltpu.roll(kw, idx_dim, 1), 0.0).astype(BF16)
        for g in range(n_kv):
            kb_ref[g] = k_ref[:, g * hd:(g + 1) * hd].astype(BF16)
            for c in range(seq // CK):
                vt_ref[g, c] = v_ref[c * CK:(c + 1) * CK, g * hd:(g + 1) * hd].T.astype(BF16)

    wit_ref[...] = (wi_ref[...] * (idx_heads ** -0.5)).T
    qib_ref[...] = (qi_ref[...] * (idx_dim ** -0.5)).astype(BF16)
    qpos = j * QB + lax.broadcasted_iota(jnp.int32, (CK, QB), 1)
    krow = lax.broadcasted_iota(jnp.int32, (CK, QB), 0)
    per_pair = LANES // idx_dim

    def score_chunk(c, carry):
        mn, mx = carry
        k0 = pl.multiple_of(c * CK, CK)
        acc = jnp.zeros((CK, QB), F32)
        for p in range(idx_heads // per_pair):
            rhs = qib_ref[:, p * LANES:(p + 1) * LANES]
            for r in range(per_pair):
                s = _dot_nt(kd_ref[r, pl.ds(k0, CK), :], rhs)
                acc = acc + wit_ref[pl.ds(idx_dim + p * per_pair + r, 1), :] * jnp.maximum(s, 0.0)
        adm = (krow + c * CK) <= qpos
        sc_ref[c] = jnp.where(adm, acc, -jnp.inf)
        mn = jnp.minimum(mn, _fold_rows(jnp.where(adm, acc, jnp.inf), jnp.minimum))
        mx = jnp.maximum(mx, _fold_rows(jnp.where(adm, acc, -jnp.inf), jnp.maximum))
        return mn, mx

    mn8, mx8 = lax.fori_loop(0, nck, score_chunk,
                             (jnp.full((8, QB), jnp.inf, F32), jnp.full((8, QB), -jnp.inf, F32)))
    row_min = jnp.min(mn8, axis=0, keepdims=True)
    row_max = jnp.max(mx8, axis=0, keepdims=True)
    n_adm = (j * QB + 1 + lax.broadcasted_iota(jnp.int32, (1, QB), 1)).astype(F32)

    def count_gt(t):
        def cbody(c, acc):
            return acc + _fold_rows(jnp.where(sc_ref[c] > t, 1.0, 0.0), jnp.add)
        part = lax.fori_loop(0, nck, cbody, jnp.zeros((8, QB), F32))
        return jnp.sum(part, axis=0, keepdims=True)

    def any_fn(mask):
        return jnp.max(jnp.where(mask, 1.0, 0.0)) > 0.0

    lo, hi, flo, fhi = _select_threshold(count_gt, row_min, row_max, n_adm, topk, any_fn)
    tie = flo != kf

    def sel_chunk(c, carry):
        sel_ref[c] = jnp.where(sc_ref[c] > lo, 1.0, 0.0)
        return carry
    lax.fori_loop(0, nck, sel_chunk, 0)

    @pl.when(any_fn(tie))
    def _():
        need = kf - fhi
        lower = (lax.broadcasted_iota(jnp.int32, (CK, CK), 1)
                 < lax.broadcasted_iota(jnp.int32, (CK, CK), 0)).astype(BF16)

        def tie_chunk(c, before):
            s = sc_ref[c]
            eq = s == hi
            eqf = jnp.where(eq, 1.0, 0.0)
            rank = before + jnp.dot(lower, eqf.astype(BF16), preferred_element_type=F32)
            keep = (s > hi) | (eq & (rank < need))
            sel_ref[c] = jnp.where(tie, jnp.where(keep, 1.0, 0.0), sel_ref[c])
            return before + jnp.sum(eqf, axis=0, keepdims=True)
        lax.fori_loop(0, nck, tie_chunk, jnp.zeros((1, QB), F32))

    scale2 = hd ** -0.5 * LOG2E
    for h in range(n_heads):
        qs_ref[h] = q_ref[:, h * hd:(h + 1) * hd].astype(BF16)
    m_ref[...] = jnp.full(m_ref.shape, NEG, F32)
    l_ref[...] = jnp.zeros(l_ref.shape, F32)
    acc_ref[...] = jnp.zeros(acc_ref.shape, F32)

    def logits(g, c):
        kc = kb_ref[g, pl.ds(pl.multiple_of(c * CK, CK), CK), :]
        qg = qs_ref[pl.ds(g * group, group)].reshape(group * QB, hd)
        st_ref[g % 2] = _dot_nt(kc, qg)

    logits(0, 0)

    def att_chunk(c, carry):
        msk = jnp.concatenate([sel_ref[c]] * group, axis=1) > 0.0
        tis = [jnp.clip(c * (CK // QB) + s - j + 2, 0, 2) for s in range(CK // QB)]
        for g in range(n_kv):
            if g + 1 < n_kv:
                logits(g + 1, c)
            else:
                logits(0, jnp.minimum(c + 1, nck - 1))
            rows = [jnp.concatenate([bias_ref[ti, g * group + hh] for hh in range(group)], axis=1)
                    for ti in tis]
            lg = jnp.where(msk, st_ref[g % 2] * scale2 + jnp.concatenate(rows, axis=0), -jnp.inf)
            m_old = m_ref[g]
            m_new = jnp.maximum(m_old, jnp.max(_fold_rows(lg, jnp.maximum), axis=0, keepdims=True))
            alpha = jnp.exp2(m_old - m_new)
            p = jnp.exp2(lg - m_new)
            l_ref[g] = alpha * l_ref[g] + jnp.sum(_fold_rows(p, jnp.add), axis=0, keepdims=True)
            pv = jnp.dot(vt_ref[g, c], p.astype(BF16), preferred_element_type=F32)
            acc_ref[g] = alpha * acc_ref[g] + pv
            m_ref[g] = m_new
        return carry
    lax.fori_loop(0, nck, att_chunk, 0)

    for g in range(n_kv):
        ot = acc_ref[g] / l_ref[g]
        for hh in range(group):
            cs = slice((g * group + hh) * hd, (g * group + hh + 1) * hd)
            o = ot[:, hh * QB:(hh + 1) * QB].T
            mixed = _sigmoid(ga_ref[:, cs]) * co_ref[:, cs] + _sigmoid(gb_ref[:, cs]) * o
            o_ref[:, cs] = mixed.astype(o_ref.dtype)


def attn_prompt(qi, kw, q, k, v, conv_out, gates, bias3, n_seq, seq, n_kv, idx_dim, idx_heads):
    m, dm = q.shape
    hd = k.shape[1] // n_kv
    n_heads = dm // hd
    group = n_heads // n_kv
    nb = seq // QB
    topk = min(TOPK_MAX, seq // 4)
    assert seq % CK == 0 and LANES % idx_dim == 0 and kw.shape[1] == LANES
    kern = functools.partial(_attn_prompt_kernel, seq=seq, n_heads=n_heads, n_kv=n_kv, idx_heads=idx_heads,
                             idx_dim=idx_dim, topk=topk, hd=hd)
    row = lambda b, j: (b * nb + j, 0)
    return pl.pallas_call(
        kern,
        grid=(n_seq, nb),
        in_specs=[pl.BlockSpec((QB, idx_heads * idx_dim), row),
                  pl.BlockSpec((QB, LANES), row),
                  pl.BlockSpec((seq, LANES), lambda b, j: (b, 0)),
                  pl.BlockSpec((QB, dm), row),
                  pl.BlockSpec((seq, n_kv * hd), lambda b, j: (b, 0)),
                  pl.BlockSpec((seq, n_kv * hd), lambda b, j: (b, 0)),
                  pl.BlockSpec((QB, dm), row),
                  pl.BlockSpec((QB, dm), lambda b, j: (b * nb + j, 0)),
                  pl.BlockSpec((QB, dm), lambda b, j: (b * nb + j, 1)),
                  pl.BlockSpec((3, n_heads, QB, QB), lambda b, j: (0, 0, 0, 0))],
        out_specs=pl.BlockSpec((QB, dm), row),
        out_shape=jax.ShapeDtypeStruct((m, dm), BF16),
        scratch_shapes=[pltpu.VMEM((2, seq, LANES), BF16),
                        pltpu.VMEM((n_kv, seq, hd), BF16),
                        pltpu.VMEM((n_kv, seq // CK, hd, CK), BF16),
                        pltpu.VMEM((LANES, QB), F32),
                        pltpu.VMEM((QB, idx_heads * idx_dim), BF16),
                        pltpu.VMEM((seq // CK, CK, QB), F32),
                        pltpu.VMEM((seq // CK, CK, QB), F32),
                        pltpu.VMEM((n_heads, QB, hd), BF16),
                        pltpu.VMEM((n_kv, 1, group * QB), F32),
                        pltpu.VMEM((n_kv, 1, group * QB), F32),
                        pltpu.VMEM((n_kv, hd, group * QB), F32),
                        pltpu.VMEM((2, CK, group * QB), F32)],
        compiler_params=_cparams(2),
        name="attn_prompt",
    )(qi, kw, kw, q, k, v, conv_out, gates, gates, bias3)


def _sample_score_kernel(pt_ref, qi_ref, wi_ref, kn_ref, *rest, pg, idx_heads, idx_dim):
    pages = rest[:pg]
    o_ref, self_ref, kcat_ref = rest[pg:]
    page = pages[0].shape[1]
    qi = qi_ref[...] * (idx_dim ** -0.5)
    wi = wi_ref[...] * (idx_heads ** -0.5)
    qb = qi.astype(BF16)
    for i in range(pg):
        kcat_ref[:, i * page:(i + 1) * page] = pages[i][...].astype(BF16)
    s = jnp.dot(qb, kcat_ref[...], preferred_element_type=F32)
    o_ref[...] = jnp.sum(wi * jnp.maximum(s, 0.0), axis=0, keepdims=True)

    @pl.when(pl.program_id(1) == 0)
    def _():
        kn = kn_ref[...].astype(BF16).astype(F32)
        s = jnp.sum(qb.astype(F32) * kn, axis=1, keepdims=True)
        sself = jnp.sum(wi * jnp.maximum(s, 0.0), axis=0, keepdims=True)
        self_ref[...] = jnp.broadcast_to(sself, self_ref.shape)


def sample_scores(page_table, qi3, wi3, ki_new3, cache_kidx_t, pg):
    n, n_pages = page_table.shape
    idx_heads, idx_dim = qi3.shape[1:]
    page = cache_kidx_t.shape[2]
    kern = functools.partial(_sample_score_kernel, pg=pg, idx_heads=idx_heads, idx_dim=idx_dim)
    page_specs = [pl.BlockSpec((None, idx_dim, page), (lambda b, p, pt, i=i: (pt[b, p * pg + i], 0, 0)))
                  for i in range(pg)]
    grid_spec = pltpu.PrefetchScalarGridSpec(
        num_scalar_prefetch=1,
        grid=(n, n_pages // pg),
        in_specs=[pl.BlockSpec((None, idx_heads, idx_dim), lambda b, p, pt: (b, 0, 0)),
                  pl.BlockSpec((None, idx_heads, 1), lambda b, p, pt: (b, 0, 0)),
                  pl.BlockSpec((None, 1, idx_dim), lambda b, p, pt: (b, 0, 0))] + page_specs,
        out_specs=[pl.BlockSpec((None, None, 1, pg * page), lambda b, p, pt: (b, p, 0, 0)),
                   pl.BlockSpec((None, 1, LANES), lambda b, p, pt: (b, 0, 0))],
        scratch_shapes=[pltpu.VMEM((idx_dim, pg * page), BF16)],
    )
    return pl.pallas_call(
        kern,
        grid_spec=grid_spec,
        out_shape=[jax.ShapeDtypeStruct((n, n_pages // pg, 1, pg * page), F32),
                   jax.ShapeDtypeStruct((n, 1, LANES), F32)],
        compiler_params=_cparams(2),
        name="sample_scores",
    )(page_table, qi3, wi3, ki_new3, *([cache_kidx_t] * pg))


def _sample_select_kernel(sc_ref, self_ref, sel4_ref, selself_ref, sel_ref, *, topk, past, rep):
    sc = sc_ref[...]
    sself = self_ref[:, 0:1]
    n = sc.shape[0]
    kf = jnp.float32(topk)
    row_min = jnp.minimum(jnp.min(sc, axis=1, keepdims=True), sself)
    row_max = jnp.maximum(jnp.max(sc, axis=1, keepdims=True), sself)
    n_adm = jnp.full((n, 1), past + 1, F32)

    def count_gt(t):
        return (jnp.sum(jnp.where(sc > t, 1.0, 0.0), axis=1, keepdims=True)
                + jnp.where(sself > t, 1.0, 0.0))

    def any_fn(mask):
        return jnp.max(jnp.where(mask, 1.0, 0.0)) > 0.0

    lo, hi, flo, fhi = _select_threshold(count_gt, row_min, row_max, n_adm, topk, any_fn)
    tie = flo != kf
    sel_ref[...] = jnp.where(sc > lo, 1.0, 0.0)
    selself_ref[...] = jnp.broadcast_to(jnp.where(sself > lo, 1.0, 0.0), selself_ref.shape)

    @pl.when(any_fn(tie))
    def _():
        need = kf - fhi
        blk = 512
        tri = (lax.broadcasted_iota(jnp.int32, (blk, blk), 0)
               < lax.broadcasted_iota(jnp.int32, (blk, blk), 1)).astype(BF16)
        before = jnp.zeros((n, 1), F32)
        for c in range(past // blk):
            s = sc_ref[:, c * blk:(c + 1) * blk]
            eq = s == hi
            rank = before + jnp.dot(jnp.where(eq, 1.0, 0.0).astype(BF16), tri, preferred_element_type=F32)
            keep = (s > hi) | (eq & (rank < need))
            sel_ref[:, c * blk:(c + 1) * blk] = jnp.where(tie, jnp.where(keep, 1.0, 0.0),
                                                          sel_ref[:, c * blk:(c + 1) * blk])
            before = before + jnp.sum(jnp.where(eq, 1.0, 0.0), axis=1, keepdims=True)
        keep_self = (sself > hi) | ((sself == hi) & (before < need))
        selself_ref[...] = jnp.broadcast_to(
            jnp.where(tie, jnp.where(keep_self, 1.0, 0.0), jnp.where(sself > lo, 1.0, 0.0)), selself_ref.shape)

    blk = 512
    row_lo = lax.broadcasted_iota(jnp.int32, (blk, blk * rep), 0) * rep
    col = lax.broadcasted_iota(jnp.int32, (blk, blk * rep), 1)
    spread = jnp.where((col >= row_lo) & (col < row_lo + rep), 1.0, 0.0).astype(BF16)
    for c in range(past // blk):
        sel4_ref[:, c * blk * rep:(c + 1) * blk * rep] = jnp.dot(
            sel_ref[:, c * blk:(c + 1) * blk].astype(BF16), spread, preferred_element_type=F32)


def sample_select(scores, sself, topk, rep):
    n, past = scores.shape
    kern = functools.partial(_sample_select_kernel, topk=topk, past=past, rep=rep)
    return pl.pallas_call(
        kern,
        out_shape=[jax.ShapeDtypeStruct((n, past * rep), F32), jax.ShapeDtypeStruct((n, LANES), F32)],
        scratch_shapes=[pltpu.VMEM((n, past), F32)],
        compiler_params=pltpu.CompilerParams(vmem_limit_bytes=VMEM_LIMIT),
        name="sample_select",
    )(scores, sself)


def _sample_attn_kernel(pt_ref, q_ref, kn_ref, vn_ref, sel_ref, selself_ref, bias_ref, rb0_ref, own_ref, *rest,
                        pg, n_heads, n_kv, hd):
    kpages = rest[:pg]
    vpages = rest[pg:2 * pg]
    o_ref = rest[2 * pg]
    kcat_ref, vcat_ref, m_ref, l_ref, acc_ref = rest[2 * pg + 1:]
    p = pl.program_id(1)
    n_steps = pl.num_programs(1)
    scale = hd ** -0.5
    rows = kpages[0].shape[0]

    @pl.when(p == 0)
    def _():
        m_ref[...] = jnp.full(m_ref.shape, NEG, F32)
        l_ref[...] = jnp.zeros(l_ref.shape, F32)
        acc_ref[...] = jnp.zeros(acc_ref.shape, F32)

    for i in range(pg):
        kcat_ref[i * rows:(i + 1) * rows, :] = kpages[i][...].astype(BF16)
        vcat_ref[i * rows:(i + 1) * rows, :] = vpages[i][...].astype(BF16)
    qb = q_ref[...].astype(BF16)
    last = p == n_steps - 1
    bias = jnp.concatenate([bias_ref[0]] * (pg - 1) + [jnp.where(last, bias_ref[1], bias_ref[0])], axis=1)
    msk = (own_ref[...] > 0.0) & (sel_ref[...] > 0.0)
    lg = jnp.where(msk, _dot_nt(qb, kcat_ref[...]) * scale + bias, -jnp.inf)
    m_old = m_ref[...]
    m_new = jnp.maximum(m_old, jnp.max(lg, axis=-1, keepdims=True))
    alpha = jnp.exp(m_old - m_new)
    pr = jnp.exp(lg - m_new)
    l_new = alpha * l_ref[...] + jnp.sum(pr, axis=-1, keepdims=True)
    acc = alpha * acc_ref[...] + jnp.dot(pr.astype(BF16), vcat_ref[...], preferred_element_type=F32)
    m_ref[...] = m_new
    l_ref[...] = l_new
    acc_ref[...] = acc

    @pl.when(last)
    def _():
        kn = kn_ref[...].astype(BF16).astype(F32)
        vn = vn_ref[...].astype(BF16).astype(F32)
        ls = jnp.sum(qb.astype(F32) * kn, axis=-1, keepdims=True) * scale + rb0_ref[...]
        on = selself_ref[:, 0:1] > 0.0
        ls = jnp.where(on, ls, NEG)
        m_f = jnp.maximum(m_new, ls)
        a2 = jnp.exp(m_new - m_f)
        ps = jnp.where(on, jnp.exp(ls - m_f), 0.0)
        l_f = a2 * l_new + ps
        acc_f = a2 * acc + ps.astype(BF16).astype(F32) * vn
        o_ref[...] = acc_f / l_f


def sample_attention(page_table, q3, k_rep, v_rep, sel4, selself3, bias_s, rb0, cache_k, cache_v, pg, n_kv):
    n, n_pages = page_table.shape
    n_heads, hd = q3.shape[1:]
    rows = cache_k.shape[1]
    kern = functools.partial(_sample_attn_kernel, pg=pg, n_heads=n_heads, n_kv=n_kv, hd=hd)
    own = (np.arange(pg * rows)[None, :] % n_kv
           == np.arange(n_heads)[:, None] // (n_heads // n_kv)).astype(np.float32)
    kspecs = [pl.BlockSpec((None, rows, hd), (lambda b, p, pt, i=i: (pt[b, p * pg + i], 0, 0)))
              for i in range(pg)]
    per_seq = lambda b, p, pt: (b, 0, 0)
    grid_spec = pltpu.PrefetchScalarGridSpec(
        num_scalar_prefetch=1,
        grid=(n, n_pages // pg),
        in_specs=[pl.BlockSpec((None, n_heads, hd), per_seq),
                  pl.BlockSpec((None, n_heads, hd), per_seq),
                  pl.BlockSpec((None, n_heads, hd), per_seq),
                  pl.BlockSpec((None, None, 1, pg * rows), lambda b, p, pt: (b, p, 0, 0)),
                  pl.BlockSpec((None, 1, LANES), per_seq),
                  pl.BlockSpec((2, n_heads, rows), lambda b, p, pt: (0, 0, 0)),
                  pl.BlockSpec((n_heads, 1), lambda b, p, pt: (0, 0)),
                  pl.BlockSpec((n_heads, pg * rows), lambda b, p, pt: (0, 0))] + kspecs + kspecs,
        out_specs=pl.BlockSpec((None, n_heads, hd), per_seq),
        scratch_shapes=[pltpu.VMEM((pg * rows, hd), BF16), pltpu.VMEM((pg * rows, hd), BF16),
                        pltpu.VMEM((n_heads, 1), F32), pltpu.VMEM((n_heads, 1), F32),
                        pltpu.VMEM((n_heads, hd), F32)],
    )
    return pl.pallas_call(
        kern,
        grid_spec=grid_spec,
        out_shape=jax.ShapeDtypeStruct((n, n_heads, hd), F32),
        compiler_params=_cparams(2),
        name="sample_attn",
    )(page_table, q3, k_rep, v_rep, sel4, selself3, bias_s, rb0, jnp.asarray(own), *([cache_k] * pg),
      *([cache_v] * pg))


def _mix_kernel(ga_ref, gb_ref, co_ref, at_ref, o_ref):
    o_ref[...] = (_sigmoid(ga_ref[...]) * co_ref[...] + _sigmoid(gb_ref[...]) * at_ref[...]).astype(o_ref.dtype)


def gated_mix(gates, conv_out, attn):
    n, dm = conv_out.shape
    return pl.pallas_call(
        _mix_kernel,
        grid=(1,),
        in_specs=[pl.BlockSpec((n, dm), lambda i: (0, 0)), pl.BlockSpec((n, dm), lambda i: (0, 1)),
                  pl.BlockSpec((n, dm), lambda i: (0, 0)), pl.BlockSpec((n, dm), lambda i: (0, 0))],
        out_specs=pl.BlockSpec((n, dm), lambda i: (0, 0)),
        out_shape=jax.ShapeDtypeStruct((n, dm), BF16),
    )(gates, gates, conv_out, attn)


def _ffn_up_kernel(x_ref, xh_ref, wg_ref, wv_ref, dg_ref, dv_ref, bg_ref, bv_ref, pg_ref, pv_ref,
                   o_ref, sg_ref, sv_ref, eg_ref, ev_ref, *, tm, rs, width, tiles_per_seq):
    hist = width - 1
    pad = xh_ref.shape[0]
    first = pl.program_id(0) % tiles_per_seq == 0
    branches = ((wg_ref, dg_ref, bg_ref, pg_ref, eg_ref, sg_ref),
                (wv_ref, dv_ref, bv_ref, pv_ref, ev_ref, sv_ref))
    for w_ref, _, _, prev_ref, e_ref, _ in branches:
        e_ref[pl.ds(0, pad), :] = jnp.dot(xh_ref[...], w_ref[...], preferred_element_type=F32)

        @pl.when(first)
        def _():
            e_ref[pl.ds(pad - hist, hist), :] = prev_ref[...]

    def project(r):
        for w_ref, _, _, _, e_ref, _ in branches:
            e_ref[pl.ds(pad + r * rs, rs), :] = jnp.dot(x_ref[pl.ds(r * rs, rs), :], w_ref[...],
                                                        preferred_element_type=F32)

    def activate(r):
        outs = []
        for _, d_ref, b_ref, _, e_ref, _ in branches:
            acc = b_ref[...] + d_ref[pl.ds(hist, 1), :] * e_ref[pl.ds(pad + r * rs, rs), :]
            for j in range(hist):
                acc = acc + d_ref[pl.ds(j, 1), :] * e_ref[pl.ds(pad + r * rs - hist + j, rs), :]
            outs.append(acc)
        g, v = outs
        o_ref[pl.ds(r * rs, rs), :] = (g * _sigmoid(g) * v).astype(o_ref.dtype)

    n_sub = tm // rs
    project(0)
    for r in range(1, n_sub):
        project(r)
        activate(r - 1)
    activate(n_sub - 1)
    for _, _, _, _, e_ref, s_ref in branches:
        s_ref[...] = e_ref[pl.ds(pad + tm - hist, hist), :]


def ffn_up_act(xn, w_up, prev, dw, bdw, n_seq, seq, tm, tn):
    m, k = xn.shape
    f2 = w_up.shape[1]
    f = f2 // 2
    width = dw.shape[0]
    ncb = f // tn
    pad = 16
    tps = seq // tm
    assert seq % tm == 0 and tm % pad == 0 and f % tn == 0 and width - 1 <= pad
    rs = min(256, tm // 2)
    assert tm % rs == 0 and rs % 8 == 0 and k % 256 == 0
    kern = functools.partial(_ffn_up_kernel, tm=tm, rs=rs, width=width, tiles_per_seq=tps)
    hb = tm // pad
    st_shape = jax.ShapeDtypeStruct((m // tm, width - 1, f), F32)
    act, sg, sv = pl.pallas_call(
        kern,
        grid=(m // tm, ncb),
        in_specs=[pl.BlockSpec((tm, k), lambda i, c: (i, 0)),
                  pl.BlockSpec((pad, k), lambda i, c: (jnp.maximum(i * hb - 1, 0), 0)),
                  pl.BlockSpec((k, tn), lambda i, c: (0, c)),
                  pl.BlockSpec((k, tn), lambda i, c: (0, c + ncb)),
                  pl.BlockSpec((width, tn), lambda i, c: (0, c)),
                  pl.BlockSpec((width, tn), lambda i, c: (0, c + ncb)),
                  pl.BlockSpec((1, tn), lambda i, c: (0, c)),
                  pl.BlockSpec((1, tn), lambda i, c: (0, c + ncb)),
                  pl.BlockSpec((None, width - 1, tn), lambda i, c: (i // tps, 0, c)),
                  pl.BlockSpec((None, width - 1, tn), lambda i, c: (i // tps, 0, c + ncb))],
        out_specs=[pl.BlockSpec((tm, tn), lambda i, c: (i, c)),
                   pl.BlockSpec((None, width - 1, tn), lambda i, c: (i, 0, c)),
                   pl.BlockSpec((None, width - 1, tn), lambda i, c: (i, 0, c))],
        out_shape=[jax.ShapeDtypeStruct((m, f), BF16), st_shape, st_shape],
        scratch_shapes=[pltpu.VMEM((pad + tm, tn), F32), pltpu.VMEM((pad + tm, tn), F32)],
        compiler_params=_cparams(2),
        name="ffn_up_act",
    )(xn, xn, w_up, w_up, dw, dw, bdw.reshape(1, f2), bdw.reshape(1, f2), prev, prev)
    state = jnp.concatenate([sg, sv], axis=-1).reshape(n_seq, tps, width - 1, f2)[:, tps - 1]
    return act, state


def _ffn_act_step_kernel(ug_ref, uv_ref, pg_ref, pv_ref, wg_ref, wv_ref, bg_ref, bv_ref, o_ref, *, width):
    def conv(u_ref, prev_ref, w_ref, b_ref):
        acc = b_ref[...] + w_ref[pl.ds(width - 1, 1), :] * u_ref[...]
        for j in range(width - 1):
            acc = acc + w_ref[pl.ds(j, 1), :] * prev_ref[j]
        return acc
    g = conv(ug_ref, pg_ref, wg_ref, bg_ref)
    v = conv(uv_ref, pv_ref, wv_ref, bv_ref)
    o_ref[...] = (g * _sigmoid(g) * v).astype(o_ref.dtype)


def ffn_act_step(u, prev_t, dw, bdw, tc):
    n, f2 = u.shape
    f = f2 // 2
    width = dw.shape[0]
    ncb = f // tc
    kern = functools.partial(_ffn_act_step_kernel, width=width)
    return pl.pallas_call(
        kern,
        grid=(ncb,),
        in_specs=[pl.BlockSpec((n, tc), lambda c: (0, c)),
                  pl.BlockSpec((n, tc), lambda c: (0, c + ncb)),
                  pl.BlockSpec((width - 1, n, tc), lambda c: (0, 0, c)),
                  pl.BlockSpec((width - 1, n, tc), lambda c: (0, 0, c + ncb)),
                  pl.BlockSpec((width, tc), lambda c: (0, c)),
                  pl.BlockSpec((width, tc), lambda c: (0, c + ncb)),
                  pl.BlockSpec((1, tc), lambda c: (0, c)),
                  pl.BlockSpec((1, tc), lambda c: (0, c + ncb))],
        out_specs=pl.BlockSpec((n, tc), lambda c: (0, c)),
        out_shape=jax.ShapeDtypeStruct((n, f), BF16),
        compiler_params=_cparams(1),
    )(u, u, prev_t, prev_t, dw, dw, bdw.reshape(1, f2), bdw.reshape(1, f2))


def _row_tile(m, cap):
    tm = cap
    while m % tm:
        tm //= 2
    return tm if tm >= 16 else m


def _in_projection(xn, w_main, w_kw, w_g, sizes, tm):
    glu_w, q_w, k_w, v_w, qi_w, ki_w, wi_w, ga_w, gb_w = sizes
    offs = np.concatenate([[0], np.cumsum(sizes)])
    tn = 512
    t = dict(w_transposed=True)
    glu = matmul_w(xn, w_main, int(offs[0]), glu_w, tm, tn, name="mm_glu", **t)
    q = matmul_w(xn, w_main, int(offs[1]), q_w, tm, tn, name="mm_q", out_dtype=BF16, **t)
    k = matmul_w(xn, w_main, int(offs[2]), k_w, tm, min(tn, k_w), name="mm_k", **t)
    v = matmul_w(xn, w_main, int(offs[3]), v_w, tm, min(tn, v_w), name="mm_v", **t)
    qi = matmul_w(xn, w_main, int(offs[4]), qi_w, tm, min(tn, qi_w), name="mm_qi", out_dtype=BF16, **t)
    kw = matmul_w(xn, w_kw, 0, LANES, tm, LANES, name="mm_kw", **t)
    gates = matmul_w(xn, w_g, 0, ga_w + gb_w, tm, tn, name="mm_gates", **t)
    return glu, q, k, v, qi, kw, gates


def kernel(x_prompt, x_sample, cache_k, cache_v, cache_kidx, state_conv, state_ffn, page_table, rel_bias,
           norm_attn, w_in, dw_conv, b_dw_conv, ln_conv_g, ln_conv_b, w_conv_out, w_o, norm_ffn, w_up, dw_ffn,
           b_dw_ffn, w_down, norm_final):
    bsz, seq, dm = x_prompt.shape
    nd, dec_seq, _ = x_sample.shape
    depth, n_pool, page, n_kv, hd = cache_k.shape
    idx_dim = cache_kidx.shape[-1]
    n_pages = page_table.shape[1]
    past = n_pages * page
    width, dconv = dw_conv.shape[1:]
    fwidth = dw_ffn.shape[1]
    f = w_down.shape[1]
    n_heads = w_o.shape[1] // hd
    d_attn = n_heads * hd
    d_kv = n_kv * hd
    n_in = w_in.shape[2]
    idx_heads = (n_in - 2 * dconv - d_attn - 2 * d_kv - idx_dim - 2 * dm) // (idx_dim + 1)
    sizes = (2 * dconv, d_attn, d_kv, d_kv, idx_heads * idx_dim, idx_dim, idx_heads, dm, dm)
    assert sum(sizes) == n_in and depth == 1 and dec_seq == 1 and page == LANES and d_attn == dm

    mp = bsz * seq
    xp = x_prompt.reshape(mp, dm)
    xs = x_sample.reshape(nd, dm)
    bias3, bias_s = bias_tables(rel_bias, page, n_kv)
    tmp = _row_tile(mp, 2048)
    drop = lambda a: a.reshape(a.shape[1:])
    (norm_attn, w_in, dw_conv, b_dw_conv, ln_conv_g, ln_conv_b, w_conv_out, w_o, norm_ffn, w_up, dw_ffn,
     b_dw_ffn, w_down, state_conv, state_ffn) = map(drop, (
         norm_attn, w_in, dw_conv, b_dw_conv, ln_conv_g, ln_conv_b, w_conv_out, w_o, norm_ffn, w_up, dw_ffn,
         b_dw_ffn, w_down, state_conv, state_ffn))
    kidx_pool = jnp.swapaxes(cache_kidx.reshape(n_pool, page, idx_dim), 1, 2)
    k_pool = cache_k.reshape(n_pool, page * n_kv, hd)
    v_pool = cache_v.reshape(n_pool, page * n_kv, hd)

    n_aligned = sum(sizes[:5])
    n_small = idx_dim + idx_heads
    w_in_t = jnp.swapaxes(w_in, 0, 1)
    w_main = cast_rows_bf16(w_in_t, 0, n_aligned)
    w_kw = cast_rows_bf16(w_in_t, n_aligned, LANES, n_small)
    w_g = cast_rows_bf16(w_in_t, n_aligned + n_small, 2 * dm)
    w_o, w_up, w_down = cast_bf16(w_o), cast_bf16(w_up), cast_bf16(w_down)

    xn = rmsnorm_rows(xp, norm_attn, BF16, 512)
    glu, q, k, v, qi, kw, gates = _in_projection(xn, w_main, w_kw, w_g, sizes, tmp)
    conv0 = jnp.zeros((bsz, width - 1, dconv), F32)
    conv_out, conv_state_p = conv_branch_prompt(glu, conv0, dw_conv, b_dw_conv, ln_conv_g, ln_conv_b,
                                                w_conv_out, bsz, seq, 256)
    mixed = attn_prompt(qi, kw, q, k, v, conv_out, gates, bias3, bsz, seq, n_kv, idx_dim, idx_heads)
    x2 = matmul_w(mixed, w_o, 0, dm, tmp, 512, res=xp, name="mm_o")
    xn2 = rmsnorm_rows(x2, norm_ffn, BF16, 512)
    ffn0 = jnp.zeros((bsz, fwidth - 1, 2 * f), F32)
    act, ffn_state_p = ffn_up_act(xn2, w_up, ffn0, dw_ffn, b_dw_ffn, bsz, seq, _row_tile(seq, 2048), 512)
    x3 = matmul_w(act, w_down, 0, dm, _row_tile(mp, 1024), 256, res=x2, name="mm_down")
    y_prompt = rmsnorm_rows(x3, norm_final, F32, 512).reshape(bsz, seq, dm)

    xns = rmsnorm_rows(xs, norm_attn, BF16, nd)
    glu_s, q_s, k_s, v_s, qi_s, kw_s, gates_s = _in_projection(xns, w_main, w_kw, w_g, sizes, nd)
    ki_s = kw_s[:, :idx_dim]
    wi_s = kw_s[:, idx_dim:idx_dim + idx_heads]
    sc_prev_t = jnp.swapaxes(state_conv, 0, 1)
    conv_out_s, u_conv_s = conv_branch_step(glu_s, sc_prev_t, dw_conv, b_dw_conv, ln_conv_g, ln_conv_b,
                                            w_conv_out)
    conv_state_s = jnp.concatenate([state_conv[:, 1:], u_conv_s[:, None, :]], axis=1)

    pg = 16 if n_pages % 16 == 0 else 8
    scores3, sself3 = sample_scores(page_table, qi_s.reshape(nd, idx_heads, idx_dim),
                                    wi_s.reshape(nd, idx_heads, 1), ki_s.reshape(nd, 1, idx_dim), kidx_pool, pg)
    topk_s = min(TOPK_MAX, (past + dec_seq) // 4)
    sel4, selself = sample_select(scores3.reshape(nd, past), sself3.reshape(nd, LANES), topk_s, n_kv)
    group = n_heads // n_kv
    k_rep = jnp.repeat(k_s.reshape(nd, n_kv, hd), group, axis=1)
    v_rep = jnp.repeat(v_s.reshape(nd, n_kv, hd), group, axis=1)
    attn_s = sample_attention(page_table, q_s.reshape(nd, n_heads, hd), k_rep, v_rep,
                              sel4.reshape(nd, n_pages // pg, 1, pg * page * n_kv), selself.reshape(nd, 1, LANES),
                              bias_s,
                              rel_bias[0].reshape(n_heads, 1), k_pool, v_pool, pg, n_kv)
    mixed_s = gated_mix(gates_s, conv_out_s, attn_s.reshape(nd, dm))
    x2s = matmul_w(mixed_s, w_o, 0, dm, nd, 512, res=xs)
    xn2s = rmsnorm_rows(x2s, norm_ffn, BF16, nd)
    u_s = matmul_w(xn2s, w_up, 0, 2 * f, nd, 512)
    sf_prev_t = jnp.swapaxes(state_ffn, 0, 1)
    act_s = ffn_act_step(u_s, sf_prev_t, dw_ffn, b_dw_ffn, 512)
    x3s = matmul_w(act_s, w_down, 0, dm, nd, 256, res=x2s)
    y_sample = rmsnorm_rows(x3s, norm_final, F32, nd).reshape(nd, dec_seq, dm)
    ffn_state_s = jnp.concatenate([state_ffn[:, 1:], u_s[:, None, :]], axis=1)

    return (y_prompt, y_sample,
            k.reshape(1, bsz, seq, n_kv, hd), v.reshape(1, bsz, seq, n_kv, hd),
            kw.reshape(bsz, seq, LANES)[None, :, :, :idx_dim],
            conv_state_p[None], ffn_state_p[None],
            k_s.reshape(1, nd, dec_seq, n_kv, hd), v_s.reshape(1, nd, dec_seq, n_kv, hd),
            ki_s.reshape(1, nd, dec_seq, idx_dim),
            conv_state_s[None], ffn_state_s[None])
```

```python
import functools
import math

import numpy as np
import jax
import jax.numpy as jnp
from jax import lax
from jax.experimental import pallas as pl
from jax.experimental.pallas import tpu as pltpu

F32 = jnp.float32
BF16 = jnp.bfloat16

EPS = 1e-6
TOPK_MAX = 256
N_BUCKETS = 32
MAX_DISTANCE = 128
QB = 128
CK = 256
LANES = 128
NEG = -1e30
LOG2E = math.log2(math.e)
VMEM_LIMIT = 56 * 1024 * 1024


def _cparams(n_axes, vmem=VMEM_LIMIT):
    return pltpu.CompilerParams(dimension_semantics=("arbitrary",) * n_axes, vmem_limit_bytes=vmem)


def _dot_nt(a, b):
    return lax.dot_general(a, b, (((1,), (1,)), ((), ())), preferred_element_type=F32)


def _sigmoid(x):
    return 1.0 / (1.0 + jnp.exp(-x))


def _fold_rows(x, op):
    while x.shape[0] > 8:
        half = x.shape[0] // 2
        x = op(x[:half], x[half:])
    return x


def _rel_bucket_np(dist):
    n = np.maximum(dist, 0)
    max_exact = N_BUCKETS // 2
    nf = np.maximum(n, 1).astype(np.float32)
    large = max_exact + (np.log(nf / np.float32(max_exact)) / np.float32(math.log(MAX_DISTANCE / max_exact))
                         * np.float32(N_BUCKETS - max_exact)).astype(np.int32)
    large = np.minimum(large, N_BUCKETS - 1)
    return np.where(n < max_exact, n, large).astype(np.int32)


def _rms_kernel(x_ref, g_ref, o_ref):
    x = x_ref[...]
    y = x * lax.rsqrt(jnp.mean(x * x, axis=-1, keepdims=True) + EPS) * g_ref[...]
    o_ref[...] = y.astype(o_ref.dtype)


def rmsnorm_rows(x, g, out_dtype, tm):
    m, d = x.shape
    return pl.pallas_call(
        _rms_kernel,
        grid=(m // tm,),
        in_specs=[pl.BlockSpec((tm, d), lambda i: (i, 0)), pl.BlockSpec((1, d), lambda i: (0, 0))],
        out_specs=pl.BlockSpec((tm, d), lambda i: (i, 0)),
        out_shape=jax.ShapeDtypeStruct((m, d), out_dtype),
        compiler_params=_cparams(1),
        name="rmsnorm",
    )(x, g.reshape(1, d))


def _cast_kernel(w_ref, o_ref):
    o_ref[...] = w_ref[...].astype(o_ref.dtype)


def cast_bf16(w, ncols=None):
    k, n = w.shape
    ncols = n if ncols is None else ncols
    tk = 512 if k % 512 == 0 else k
    tn = 1024 if ncols % 1024 == 0 else (512 if ncols % 512 == 0 else ncols)
    return pl.pallas_call(
        _cast_kernel,
        grid=(k // tk, ncols // tn),
        in_specs=[pl.BlockSpec((tk, tn), lambda i, j: (i, j))],
        out_specs=pl.BlockSpec((tk, tn), lambda i, j: (i, j)),
        out_shape=jax.ShapeDtypeStruct((k, ncols), BF16),
        compiler_params=_cparams(2),
        name="cast_bf16",
    )(w)


def _cast_rows_kernel(w_ref, o_ref, *, valid):
    w = w_ref[...]
    if valid < w.shape[0]:
        w = jnp.where(lax.broadcasted_iota(jnp.int32, w.shape, 0) < valid, w, 0.0)
    o_ref[...] = w.astype(o_ref.dtype)


def cast_rows_bf16(wt, row0, nrows, nvalid=None):
    n, k = wt.shape
    tr = 512 if nrows % 512 == 0 else nrows
    nvalid = nrows if nvalid is None else nvalid
    assert row0 % 8 == 0 and (nvalid == nrows or tr == nrows) and row0 + nrows <= n
    return pl.pallas_call(
        functools.partial(_cast_rows_kernel, valid=nvalid),
        grid=(nrows // tr,),
        in_specs=[pl.BlockSpec((pl.Element(tr), pl.Element(k)), lambda i: (pl.multiple_of(row0 + i * tr, 8), 0))],
        out_specs=pl.BlockSpec((tr, k), lambda i: (i, 0)),
        out_shape=jax.ShapeDtypeStruct((nrows, k), BF16),
        compiler_params=_cparams(1),
        name="cast_rows_bf16",
    )(wt)


def _mm_kernel(a_ref, w_ref, o_ref):
    o_ref[...] = jnp.dot(a_ref[...], w_ref[...], preferred_element_type=F32).astype(o_ref.dtype)


def _mm_res_kernel(a_ref, w_ref, r_ref, o_ref):
    o_ref[...] = r_ref[...] + jnp.dot(a_ref[...], w_ref[...], preferred_element_type=F32)


def _mm_nt_kernel(a_ref, wt_ref, o_ref):
    o_ref[...] = _dot_nt(a_ref[...], wt_ref[...]).astype(o_ref.dtype)


def matmul_w(a, w, col0, ncols, tm, tn, res=None, name="matmul", out_dtype=F32, w_transposed=False):
    m, k = a.shape
    assert col0 % tn == 0 and ncols % tn == 0 and m % tm == 0
    cb = col0 // tn
    if w_transposed:
        assert res is None
        w_spec = pl.BlockSpec((tn, k), lambda i, j: (j + cb, 0))
    else:
        w_spec = pl.BlockSpec((k, tn), lambda i, j: (0, j + cb))
    in_specs = [pl.BlockSpec((tm, k), lambda i, j: (i, 0)), w_spec]
    args = [a, w]
    kern = _mm_nt_kernel if w_transposed else _mm_kernel
    if res is not None:
        in_specs.append(pl.BlockSpec((tm, tn), lambda i, j: (i, j)))
        args.append(res)
        kern = _mm_res_kernel
    return pl.pallas_call(
        kern,
        grid=(m // tm, ncols // tn),
        in_specs=in_specs,
        out_specs=pl.BlockSpec((tm, tn), lambda i, j: (i, j)),
        out_shape=jax.ShapeDtypeStruct((m, ncols), out_dtype),
        compiler_params=_cparams(2),
        name=name,
    )(*args)


def _bias_kernel(rb_ref, bk3_ref, bks_ref, o3_ref, os_ref, *, n_heads):
    def head(h, carry):
        for t in range(3):
            bk = bk3_ref[t]
            acc = jnp.zeros(bk.shape, F32)
            for b in range(N_BUCKETS):
                acc = jnp.where(bk == b, rb_ref[b, h], acc)
            o3_ref[t, h] = acc * LOG2E
        for t in range(2):
            bk = bks_ref[t]
            acc = jnp.zeros(bk.shape, F32)
            for b in range(N_BUCKETS):
                acc = jnp.where(bk == b, rb_ref[b, h], acc)
            os_ref[t, pl.ds(h, 1), :] = acc
        return carry
    lax.fori_loop(0, n_heads, head, 0)


def bias_tables(rel_bias, page, rep):
    n_heads = rel_bias.shape[1]
    cols = page * rep
    i = np.arange(QB)[None, :]
    k = np.arange(QB)[:, None]
    bk3 = np.stack([_rel_bucket_np(i - k + 2 * QB), _rel_bucket_np(i - k + QB), _rel_bucket_np(i - k)])
    assert (_rel_bucket_np(np.arange(QB + 1, 1 << 20)) == N_BUCKETS - 1).all()
    assert (bk3[0] == N_BUCKETS - 1).all()
    assert page >= QB
    bks = np.stack([np.full((1, cols), N_BUCKETS - 1, np.int32),
                    _rel_bucket_np(page - np.arange(cols) // rep)[None, :]])
    return pl.pallas_call(
        functools.partial(_bias_kernel, n_heads=n_heads),
        in_specs=[pl.BlockSpec(memory_space=pltpu.SMEM),
                  pl.BlockSpec(memory_space=pltpu.VMEM), pl.BlockSpec(memory_space=pltpu.VMEM)],
        out_specs=[pl.BlockSpec(memory_space=pltpu.VMEM), pl.BlockSpec(memory_space=pltpu.VMEM)],
        out_shape=[jax.ShapeDtypeStruct((3, n_heads, QB, QB), F32),
                   jax.ShapeDtypeStruct((2, n_heads, cols), F32)],
    )(rel_bias, jnp.asarray(bk3), jnp.asarray(bks))


def _conv_kernel(glu_ref, prev_ref, dw_ref, bdw_ref, lng_ref, lnb_ref, wout_ref, o_ref, st_ref,
                 ext_ref, h_ref, wbf_ref, sh_ref, *, tt, width, dconv):
    b = pl.program_id(0)
    t = pl.program_id(1)
    pad = 32
    hist = width - 1

    @pl.when((b == 0) & (t == 0))
    def _():
        wbf_ref[...] = wout_ref[...].astype(BF16)

    @pl.when(t == 0)
    def _():
        ext_ref[pl.ds(pad - hist, hist), :] = prev_ref[...]

    @pl.when(t > 0)
    def _():
        ext_ref[pl.ds(0, pad), :] = ext_ref[pl.ds(tt, pad), :]

    glu = glu_ref[...]
    u = glu[:, :dconv] * _sigmoid(glu[:, dconv:])
    ext_ref[pl.ds(pad, tt), :] = u
    st_ref[...] = ext_ref[pl.ds(pad + tt - hist, hist), :]

    span = tt + pad - 8
    for s in range(1, 8):
        sh_ref[s - 1] = ext_ref[pl.ds(s, span), :]

    for c in range(dconv // LANES):
        cs = slice(c * LANES, (c + 1) * LANES)
        acc = jnp.zeros((tt, LANES), F32) + bdw_ref[:, cs]
        for j in range(width):
            a, s = divmod(pad - hist + j, 8)
            rows = ext_ref[pl.ds(8 * a, tt), cs] if s == 0 else sh_ref[s - 1, pl.ds(8 * a, tt), cs]
            acc = acc + dw_ref[pl.ds(j, 1), cs] * rows
        h_ref[:, cs] = acc

    h = h_ref[...]
    mu = jnp.mean(h, axis=-1, keepdims=True)
    var = jnp.mean(jnp.square(h - mu), axis=-1, keepdims=True)
    y = (h - mu) * lax.rsqrt(var + EPS) * lng_ref[...] + lnb_ref[...]
    y = y * _sigmoid(y)
    o_ref[...] = jnp.dot(y.astype(BF16), wbf_ref[...], preferred_element_type=F32)


def conv_branch_prompt(glu_pre, prev, dw, bdw, lng, lnb, wout, n_seq, seq, tt):
    width, dconv = dw.shape
    dm = wout.shape[1]
    nt = seq // tt
    kern = functools.partial(_conv_kernel, tt=tt, width=width, dconv=dconv)
    return pl.pallas_call(
        kern,
        grid=(n_seq, nt),
        in_specs=[pl.BlockSpec((tt, 2 * dconv), lambda b, t: (b * nt + t, 0)),
                  pl.BlockSpec((None, width - 1, dconv), lambda b, t: (b, 0, 0)),
                  pl.BlockSpec((width, dconv), lambda b, t: (0, 0)),
                  pl.BlockSpec((1, dconv), lambda b, t: (0, 0)),
                  pl.BlockSpec((1, dconv), lambda b, t: (0, 0)),
                  pl.BlockSpec((1, dconv), lambda b, t: (0, 0)),
                  pl.BlockSpec((dconv, dm), lambda b, t: (0, 0))],
        out_specs=[pl.BlockSpec((tt, dm), lambda b, t: (b * nt + t, 0)),
                   pl.BlockSpec((None, width - 1, dconv), lambda b, t: (b, 0, 0))],
        out_shape=[jax.ShapeDtypeStruct((n_seq * seq, dm), F32),
                   jax.ShapeDtypeStruct((n_seq, width - 1, dconv), F32)],
        scratch_shapes=[pltpu.VMEM((32 + tt, dconv), F32), pltpu.VMEM((tt, dconv), F32),
                        pltpu.VMEM((dconv, dm), BF16), pltpu.VMEM((7, 24 + tt, dconv), F32)],
        compiler_params=_cparams(2),
        name="conv_branch",
    )(glu_pre, prev, dw, bdw.reshape(1, dconv), lng.reshape(1, dconv), lnb.reshape(1, dconv), wout)


def _conv_step_kernel(glu_ref, prev_ref, dw_ref, bdw_ref, lng_ref, lnb_ref, wout_ref, o_ref, u_ref,
                      *, width, dconv):
    glu = glu_ref[...]
    u = glu[:, :dconv] * _sigmoid(glu[:, dconv:])
    u_ref[...] = u
    h = bdw_ref[...] + dw_ref[pl.ds(width - 1, 1), :] * u
    for j in range(width - 1):
        h = h + dw_ref[pl.ds(j, 1), :] * prev_ref[j]
    mu = jnp.mean(h, axis=-1, keepdims=True)
    var = jnp.mean(jnp.square(h - mu), axis=-1, keepdims=True)
    y = (h - mu) * lax.rsqrt(var + EPS) * lng_ref[...] + lnb_ref[...]
    y = y * _sigmoid(y)
    o_ref[...] = jnp.dot(y.astype(BF16), wout_ref[...].astype(BF16), preferred_element_type=F32)


def conv_branch_step(glu_pre, prev_t, dw, bdw, lng, lnb, wout):
    width, dconv = dw.shape
    n = glu_pre.shape[0]
    dm = wout.shape[1]
    kern = functools.partial(_conv_step_kernel, width=width, dconv=dconv)
    return pl.pallas_call(
        kern,
        out_shape=[jax.ShapeDtypeStruct((n, dm), F32), jax.ShapeDtypeStruct((n, dconv), F32)],
        compiler_params=pltpu.CompilerParams(vmem_limit_bytes=VMEM_LIMIT),
    )(glu_pre, prev_t, dw, bdw.reshape(1, dconv), lng.reshape(1, dconv), lnb.reshape(1, dconv), wout)


def _select_threshold(count_gt, row_min, row_max, n_adm, topk, any_fn):
    kf = jnp.float32(topk)
    full = n_adm <= kf
    lo0 = row_min - (1.0 + jnp.abs(row_min))
    hi0 = row_max
    flo0 = jnp.where(full, kf, n_adm)
    fhi0 = jnp.zeros_like(lo0)

    def active_rows(lo, hi, flo):
        mid = 0.5 * lo + 0.5 * hi
        return (flo != kf) & (lo < mid) & (mid < hi)

    def cond(st):
        lo, hi, flo, fhi = st
        return any_fn(active_rows(lo, hi, flo))

    def step(st, interpolate):
        lo, hi, flo, fhi = st
        act = active_rows(lo, hi, flo)
        mid = 0.5 * lo + 0.5 * hi
        if interpolate:
            guess = lo + (hi - lo) * ((flo - kf) / (flo - fhi))
            mid = jnp.where((lo < guess) & (guess < hi), guess, mid)
        c = count_gt(mid)
        up = act & (c >= kf)
        dn = act & (c < kf)
        return (jnp.where(up, mid, lo), jnp.where(dn, mid, hi),
                jnp.where(up, c, flo), jnp.where(dn, c, fhi))

    def body(st):
        return step(step(st, True), False)

    lo, hi, flo, fhi = lax.while_loop(cond, body, (lo0, hi0, flo0, fhi0))
    lo = jnp.where(full, -jnp.inf, lo)
    return lo, hi, flo, fhi


def _attn_prompt_kernel(qi_ref, wi_ref, kw_ref, q_ref, k_ref, v_ref, co_ref, ga_ref, gb_ref, bias_ref,
                        o_ref, kd_ref, kb_ref, vt_ref, wit_ref, qib_ref, sc_ref, sel_ref, qs_ref,
                        m_ref, l_ref, acc_ref, st_ref,
                        *, seq, n_heads, n_kv, idx_heads, idx_dim, topk, hd):
    j = pl.program_id(1)
    group = n_heads // n_kv
    nck = (j * QB + QB + CK - 1) // CK
    kf = jnp.float32(topk)

    @pl.when(j == 0)
    def _():
        kd_ref[...] = kw_ref[:, :idx_dim].astype(BF16)
        for g in range(n_kv):
            kb_ref[g] = k_ref[:, g * hd:(g + 1) * hd].astype(BF16)
            for c in range(seq // CK):
                vt_ref[g, c] = v_ref[c * CK:(c + 1) * CK, g * hd:(g + 1) * hd].T.astype(BF16)

    wit_ref[...] = (wi_ref[...] * (idx_heads ** -0.5)).T
    qi = qi_ref[...].astype(F32) * (idx_dim ** -0.5)
    for h in range(idx_heads):
        qib_ref[h] = qi[:, h * idx_dim:(h + 1) * idx_dim].astype(BF16)
    qpos = j * QB + lax.broadcasted_iota(jnp.int32, (CK, QB), 1)
    krow = lax.broadcasted_iota(jnp.int32, (CK, QB), 0)
    per_dot = 2 * LANES // QB

    def score_chunk(c, carry):
        mn, mx = carry
        k0 = pl.multiple_of(c * CK, CK)
        kc = kd_ref[pl.ds(k0, CK), :]
        acc = jnp.zeros((CK, QB), F32)
        for p in range(idx_heads // per_dot):
            rhs = qib_ref[pl.ds(p * per_dot, per_dot)].reshape(per_dot * QB, idx_dim)
            s = _dot_nt(kc, rhs)
            for r in range(per_dot):
                acc = acc + (wit_ref[pl.ds(idx_dim + p * per_dot + r, 1), :]
                             * jnp.maximum(s[:, r * QB:(r + 1) * QB], 0.0))
        adm = (krow + c * CK) <= qpos
        sc_ref[c] = jnp.where(adm, acc, -jnp.inf)
        mn = jnp.minimum(mn, _fold_rows(jnp.where(adm, acc, jnp.inf), jnp.minimum))
        mx = jnp.maximum(mx, _fold_rows(jnp.where(adm, acc, -jnp.inf), jnp.maximum))
        return mn, mx

    mn8, mx8 = lax.fori_loop(0, nck, score_chunk,
                             (jnp.full((8, QB), jnp.inf, F32), jnp.full((8, QB), -jnp.inf, F32)))
    row_min = jnp.min(mn8, axis=0, keepdims=True)
    row_max = jnp.max(mx8, axis=0, keepdims=True)
    n_adm = (j * QB + 1 + lax.broadcasted_iota(jnp.int32, (1, QB), 1)).astype(F32)

    def count_gt(t):
        def cbody(c, acc):
            return acc + _fold_rows(jnp.where(sc_ref[c] > t, 1.0, 0.0), jnp.add)
        part = lax.fori_loop(0, nck, cbody, jnp.zeros((8, QB), F32))
        return jnp.sum(part, axis=0, keepdims=True)

    def any_fn(mask):
        return jnp.max(jnp.where(mask, 1.0, 0.0)) > 0.0

    lo, hi, flo, fhi = _select_threshold(count_gt, row_min, row_max, n_adm, topk, any_fn)
    tie = flo != kf

    def sel_chunk(c, carry):
        sel_ref[c] = jnp.where(sc_ref[c] > lo, 1.0, 0.0)
        return carry
    lax.fori_loop(0, nck, sel_chunk, 0)

    @pl.when(any_fn(tie))
    def _():
        need = kf - fhi
        lower = (lax.broadcasted_iota(jnp.int32, (CK, CK), 1)
                 < lax.broadcasted_iota(jnp.int32, (CK, CK), 0)).astype(BF16)

        def tie_chunk(c, before):
            s = sc_ref[c]
            eq = s == hi
            eqf = jnp.where(eq, 1.0, 0.0)
            rank = before + jnp.dot(lower, eqf.astype(BF16), preferred_element_type=F32)
            keep = (s > hi) | (eq & (rank < need))
            sel_ref[c] = jnp.where(tie, jnp.where(keep, 1.0, 0.0), sel_ref[c])
            return before + jnp.sum(eqf, axis=0, keepdims=True)
        lax.fori_loop(0, nck, tie_chunk, jnp.zeros((1, QB), F32))

    scale2 = hd ** -0.5 * LOG2E
    for h in range(n_heads):
        qs_ref[h] = q_ref[:, h * hd:(h + 1) * hd].astype(BF16)
    m_ref[...] = jnp.full(m_ref.shape, NEG, F32)
    l_ref[...] = jnp.zeros(l_ref.shape, F32)
    acc_ref[...] = jnp.zeros(acc_ref.shape, F32)

    def logits(g, c):
        kc = kb_ref[g, pl.ds(pl.multiple_of(c * CK, CK), CK), :]
        qg = qs_ref[pl.ds(g * group, group)].reshape(group * QB, hd)
        st_ref[g % 2] = _dot_nt(kc, qg)

    logits(0, 0)

    def att_chunk(c, carry):
        msk = jnp.concatenate([sel_ref[c]] * group, axis=1) > 0.0
        tis = [jnp.clip(c * (CK // QB) + s - j + 2, 0, 2) for s in range(CK // QB)]
        for g in range(n_kv):
            if g + 1 < n_kv:
                logits(g + 1, c)
            else:
                logits(0, jnp.minimum(c + 1, nck - 1))
            rows = [jnp.concatenate([bias_ref[ti, g * group + hh] for hh in range(group)], axis=1)
                    for ti in tis]
            lg = jnp.where(msk, st_ref[g % 2] * scale2 + jnp.concatenate(rows, axis=0), -jnp.inf)
            m_old = m_ref[g]
            m_new = jnp.maximum(m_old, jnp.max(_fold_rows(lg, jnp.maximum), axis=0, keepdims=True))
            alpha = jnp.exp2(m_old - m_new)
            p = jnp.exp2(lg - m_new)
            l_ref[g] = alpha * l_ref[g] + jnp.sum(_fold_rows(p, jnp.add), axis=0, keepdims=True)
            pv = jnp.dot(vt_ref[g, c], p.astype(BF16), preferred_element_type=F32)
            acc_ref[g] = alpha * acc_ref[g] + pv
            m_ref[g] = m_new
        return carry
    lax.fori_loop(0, nck, att_chunk, 0)

    for g in range(n_kv):
        ot = acc_ref[g] / l_ref[g]
        for hh in range(group):
            cs = slice((g * group + hh) * hd, (g * group + hh + 1) * hd)
            o = ot[:, hh * QB:(hh + 1) * QB].T
            mixed = _sigmoid(ga_ref[:, cs]) * co_ref[:, cs] + _sigmoid(gb_ref[:, cs]) * o
            o_ref[:, cs] = mixed.astype(o_ref.dtype)


def attn_prompt(qi, kw, q, k, v, conv_out, gates, bias3, n_seq, seq, n_kv, idx_dim, idx_heads):
    m, dm = q.shape
    hd = k.shape[1] // n_kv
    n_heads = dm // hd
    group = n_heads // n_kv
    nb = seq // QB
    topk = min(TOPK_MAX, seq // 4)
    assert seq % CK == 0 and LANES % idx_dim == 0 and kw.shape[1] == LANES
    kern = functools.partial(_attn_prompt_kernel, seq=seq, n_heads=n_heads, n_kv=n_kv, idx_heads=idx_heads,
                             idx_dim=idx_dim, topk=topk, hd=hd)
    row = lambda b, j: (b * nb + j, 0)
    return pl.pallas_call(
        kern,
        grid=(n_seq, nb),
        in_specs=[pl.BlockSpec((QB, idx_heads * idx_dim), row),
                  pl.BlockSpec((QB, LANES), row),
                  pl.BlockSpec((seq, LANES), lambda b, j: (b, 0)),
                  pl.BlockSpec((QB, dm), row),
                  pl.BlockSpec((seq, n_kv * hd), lambda b, j: (b, 0)),
                  pl.BlockSpec((seq, n_kv * hd), lambda b, j: (b, 0)),
                  pl.BlockSpec((QB, dm), row),
                  pl.BlockSpec((QB, dm), lambda b, j: (b * nb + j, 0)),
                  pl.BlockSpec((QB, dm), lambda b, j: (b * nb + j, 1)),
                  pl.BlockSpec((3, n_heads, QB, QB), lambda b, j: (0, 0, 0, 0))],
        out_specs=pl.BlockSpec((QB, dm), row),
        out_shape=jax.ShapeDtypeStruct((m, dm), BF16),
        scratch_shapes=[pltpu.VMEM((seq, idx_dim), BF16),
                        pltpu.VMEM((n_kv, seq, hd), BF16),
                        pltpu.VMEM((n_kv, seq // CK, hd, CK), BF16),
                        pltpu.VMEM((LANES, QB), F32),
                        pltpu.VMEM((idx_heads, QB, idx_dim), BF16),
                        pltpu.VMEM((seq // CK, CK, QB), F32),
                        pltpu.VMEM((seq // CK, CK, QB), F32),
                        pltpu.VMEM((n_heads, QB, hd), BF16),
                        pltpu.VMEM((n_kv, 1, group * QB), F32),
                        pltpu.VMEM((n_kv, 1, group * QB), F32),
                        pltpu.VMEM((n_kv, hd, group * QB), F32),
                        pltpu.VMEM((2, CK, group * QB), F32)],
        compiler_params=_cparams(2),
        name="attn_prompt",
    )(qi, kw, kw, q, k, v, conv_out, gates, gates, bias3)


def _sample_score_kernel(pt_ref, qi_ref, wi_ref, kn_ref, *rest, pg, idx_heads, idx_dim):
    pages = rest[:pg]
    o_ref, self_ref, kcat_ref = rest[pg:]
    page = pages[0].shape[1]
    qi = qi_ref[...] * (idx_dim ** -0.5)
    wi = wi_ref[...] * (idx_heads ** -0.5)
    qb = qi.astype(BF16)
    for i in range(pg):
        kcat_ref[:, i * page:(i + 1) * page] = pages[i][...].astype(BF16)
    s = jnp.dot(qb, kcat_ref[...], preferred_element_type=F32)
    o_ref[...] = jnp.sum(wi * jnp.maximum(s, 0.0), axis=0, keepdims=True)

    @pl.when(pl.program_id(1) == 0)
    def _():
        kn = kn_ref[...].astype(BF16).astype(F32)
        s = jnp.sum(qb.astype(F32) * kn, axis=1, keepdims=True)
        sself = jnp.sum(wi * jnp.maximum(s, 0.0), axis=0, keepdims=True)
        self_ref[...] = jnp.broadcast_to(sself, self_ref.shape)


def sample_scores(page_table, qi3, wi3, ki_new3, cache_kidx_t, pg):
    n, n_pages = page_table.shape
    idx_heads, idx_dim = qi3.shape[1:]
    page = cache_kidx_t.shape[2]
    kern = functools.partial(_sample_score_kernel, pg=pg, idx_heads=idx_heads, idx_dim=idx_dim)
    page_specs = [pl.BlockSpec((None, idx_dim, page), (lambda b, p, pt, i=i: (pt[b, p * pg + i], 0, 0)))
                  for i in range(pg)]
    grid_spec = pltpu.PrefetchScalarGridSpec(
        num_scalar_prefetch=1,
        grid=(n, n_pages // pg),
        in_specs=[pl.BlockSpec((None, idx_heads, idx_dim), lambda b, p, pt: (b, 0, 0)),
                  pl.BlockSpec((None, idx_heads, 1), lambda b, p, pt: (b, 0, 0)),
                  pl.BlockSpec((None, 1, idx_dim), lambda b, p, pt: (b, 0, 0))] + page_specs,
        out_specs=[pl.BlockSpec((None, None, 1, pg * page), lambda b, p, pt: (b, p, 0, 0)),
                   pl.BlockSpec((None, 1, LANES), lambda b, p, pt: (b, 0, 0))],
        scratch_shapes=[pltpu.VMEM((idx_dim, pg * page), BF16)],
    )
    return pl.pallas_call(
        kern,
        grid_spec=grid_spec,
        out_shape=[jax.ShapeDtypeStruct((n, n_pages // pg, 1, pg * page), F32),
                   jax.ShapeDtypeStruct((n, 1, LANES), F32)],
        compiler_params=_cparams(2),
        name="sample_scores",
    )(page_table, qi3, wi3, ki_new3, *([cache_kidx_t] * pg))


def _sample_select_kernel(sc_ref, self_ref, sel4_ref, selself_ref, sel_ref, *, topk, past, rep):
    sc = sc_ref[...]
    sself = self_ref[:, 0:1]
    n = sc.shape[0]
    kf = jnp.float32(topk)
    row_min = jnp.minimum(jnp.min(sc, axis=1, keepdims=True), sself)
    row_max = jnp.maximum(jnp.max(sc, axis=1, keepdims=True), sself)
    n_adm = jnp.full((n, 1), past + 1, F32)

    def count_gt(t):
        return (jnp.sum(jnp.where(sc > t, 1.0, 0.0), axis=1, keepdims=True)
                + jnp.where(sself > t, 1.0, 0.0))

    def any_fn(mask):
        return jnp.max(jnp.where(mask, 1.0, 0.0)) > 0.0

    lo, hi, flo, fhi = _select_threshold(count_gt, row_min, row_max, n_adm, topk, any_fn)
    tie = flo != kf
    sel_ref[...] = jnp.where(sc > lo, 1.0, 0.0)
    selself_ref[...] = jnp.broadcast_to(jnp.where(sself > lo, 1.0, 0.0), selself_ref.shape)

    @pl.when(any_fn(tie))
    def _():
        need = kf - fhi
        blk = 512
        tri = (lax.broadcasted_iota(jnp.int32, (blk, blk), 0)
               < lax.broadcasted_iota(jnp.int32, (blk, blk), 1)).astype(BF16)
        before = jnp.zeros((n, 1), F32)
        for c in range(past // blk):
            s = sc_ref[:, c * blk:(c + 1) * blk]
            eq = s == hi
            rank = before + jnp.dot(jnp.where(eq, 1.0, 0.0).astype(BF16), tri, preferred_element_type=F32)
            keep = (s > hi) | (eq & (rank < need))
            sel_ref[:, c * blk:(c + 1) * blk] = jnp.where(tie, jnp.where(keep, 1.0, 0.0),
                                                          sel_ref[:, c * blk:(c + 1) * blk])
            before = before + jnp.sum(jnp.where(eq, 1.0, 0.0), axis=1, keepdims=True)
        keep_self = (sself > hi) | ((sself == hi) & (before < need))
        selself_ref[...] = jnp.broadcast_to(
            jnp.where(tie, jnp.where(keep_self, 1.0, 0.0), jnp.where(sself > lo, 1.0, 0.0)), selself_ref.shape)

    blk = 512
    row_lo = lax.broadcasted_iota(jnp.int32, (blk, blk * rep), 0) * rep
    col = lax.broadcasted_iota(jnp.int32, (blk, blk * rep), 1)
    spread = jnp.where((col >= row_lo) & (col < row_lo + rep), 1.0, 0.0).astype(BF16)
    for c in range(past // blk):
        sel4_ref[:, c * blk * rep:(c + 1) * blk * rep] = jnp.dot(
            sel_ref[:, c * blk:(c + 1) * blk].astype(BF16), spread, preferred_element_type=F32)


def sample_select(scores, sself, topk, rep):
    n, past = scores.shape
    kern = functools.partial(_sample_select_kernel, topk=topk, past=past, rep=rep)
    return pl.pallas_call(
        kern,
        out_shape=[jax.ShapeDtypeStruct((n, past * rep), F32), jax.ShapeDtypeStruct((n, LANES), F32)],
        scratch_shapes=[pltpu.VMEM((n, past), F32)],
        compiler_params=pltpu.CompilerParams(vmem_limit_bytes=VMEM_LIMIT),
        name="sample_select",
    )(scores, sself)


def _sample_attn_kernel(pt_ref, q_ref, kn_ref, vn_ref, sel_ref, selself_ref, bias_ref, rb0_ref, own_ref, *rest,
                        pg, n_heads, n_kv, hd):
    kpages = rest[:pg]
    vpages = rest[pg:2 * pg]
    o_ref = rest[2 * pg]
    kcat_ref, vcat_ref, m_ref, l_ref, acc_ref = rest[2 * pg + 1:]
    p = pl.program_id(1)
    n_steps = pl.num_programs(1)
    scale = hd ** -0.5
    rows = kpages[0].shape[0]

    @pl.when(p == 0)
    def _():
        m_ref[...] = jnp.full(m_ref.shape, NEG, F32)
        l_ref[...] = jnp.zeros(l_ref.shape, F32)
        acc_ref[...] = jnp.zeros(acc_ref.shape, F32)

    for i in range(pg):
        kcat_ref[i * rows:(i + 1) * rows, :] = kpages[i][...].astype(BF16)
        vcat_ref[i * rows:(i + 1) * rows, :] = vpages[i][...].astype(BF16)
    qb = q_ref[...].astype(BF16)
    last = p == n_steps - 1
    bias = jnp.concatenate([bias_ref[0]] * (pg - 1) + [jnp.where(last, bias_ref[1], bias_ref[0])], axis=1)
    msk = (own_ref[...] > 0.0) & (sel_ref[...] > 0.0)
    lg = jnp.where(msk, _dot_nt(qb, kcat_ref[...]) * scale + bias, -jnp.inf)
    m_old = m_ref[...]
    m_new = jnp.maximum(m_old, jnp.max(lg, axis=-1, keepdims=True))
    alpha = jnp.exp(m_old - m_new)
    pr = jnp.exp(lg - m_new)
    l_new = alpha * l_ref[...] + jnp.sum(pr, axis=-1, keepdims=True)
    acc = alpha * acc_ref[...] + jnp.dot(pr.astype(BF16), vcat_ref[...], preferred_element_type=F32)
    m_ref[...] = m_new
    l_ref[...] = l_new
    acc_ref[...] = acc

    @pl.when(last)
    def _():
        kn = kn_ref[...].astype(BF16).astype(F32)
        vn = vn_ref[...].astype(BF16).astype(F32)
        ls = jnp.sum(qb.astype(F32) * kn, axis=-1, keepdims=True) * scale + rb0_ref[...]
        on = selself_ref[:, 0:1] > 0.0
        ls = jnp.where(on, ls, NEG)
        m_f = jnp.maximum(m_new, ls)
        a2 = jnp.exp(m_new - m_f)
        ps = jnp.where(on, jnp.exp(ls - m_f), 0.0)
        l_f = a2 * l_new + ps
        acc_f = a2 * acc + ps.astype(BF16).astype(F32) * vn
        o_ref[...] = acc_f / l_f


def sample_attention(page_table, q3, k_rep, v_rep, sel4, selself3, bias_s, rb0, cache_k, cache_v, pg, n_kv):
    n, n_pages = page_table.shape
    n_heads, hd = q3.shape[1:]
    rows = cache_k.shape[1]
    kern = functools.partial(_sample_attn_kernel, pg=pg, n_heads=n_heads, n_kv=n_kv, hd=hd)
    own = (np.arange(pg * rows)[None, :] % n_kv
           == np.arange(n_heads)[:, None] // (n_heads // n_kv)).astype(np.float32)
    kspecs = [pl.BlockSpec((None, rows, hd), (lambda b, p, pt, i=i: (pt[b, p * pg + i], 0, 0)))
              for i in range(pg)]
    per_seq = lambda b, p, pt: (b, 0, 0)
    grid_spec = pltpu.PrefetchScalarGridSpec(
        num_scalar_prefetch=1,
        grid=(n, n_pages // pg),
        in_specs=[pl.BlockSpec((None, n_heads, hd), per_seq),
                  pl.BlockSpec((None, n_heads, hd), per_seq),
                  pl.BlockSpec((None, n_heads, hd), per_seq),
                  pl.BlockSpec((None, None, 1, pg * rows), lambda b, p, pt: (b, p, 0, 0)),
                  pl.BlockSpec((None, 1, LANES), per_seq),
                  pl.BlockSpec((2, n_heads, rows), lambda b, p, pt: (0, 0, 0)),
                  pl.BlockSpec((n_heads, 1), lambda b, p, pt: (0, 0)),
                  pl.BlockSpec((n_heads, pg * rows), lambda b, p, pt: (0, 0))] + kspecs + kspecs,
        out_specs=pl.BlockSpec((None, n_heads, hd), per_seq),
        scratch_shapes=[pltpu.VMEM((pg * rows, hd), BF16), pltpu.VMEM((pg * rows, hd), BF16),
                        pltpu.VMEM((n_heads, 1), F32), pltpu.VMEM((n_heads, 1), F32),
                        pltpu.VMEM((n_heads, hd), F32)],
    )
    return pl.pallas_call(
        kern,
        grid_spec=grid_spec,
        out_shape=jax.ShapeDtypeStruct((n, n_heads, hd), F32),
        compiler_params=_cparams(2),
        name="sample_attn",
    )(page_table, q3, k_rep, v_rep, sel4, selself3, bias_s, rb0, jnp.asarray(own), *([cache_k] * pg),
      *([cache_v] * pg))


def _mix_kernel(ga_ref, gb_ref, co_ref, at_ref, o_ref):
    o_ref[...] = (_sigmoid(ga_ref[...]) * co_ref[...] + _sigmoid(gb_ref[...]) * at_ref[...]).astype(o_ref.dtype)


def gated_mix(gates, conv_out, attn):
    n, dm = conv_out.shape
    return pl.pallas_call(
        _mix_kernel,
        grid=(1,),
        in_specs=[pl.BlockSpec((n, dm), lambda i: (0, 0)), pl.BlockSpec((n, dm), lambda i: (0, 1)),
                  pl.BlockSpec((n, dm), lambda i: (0, 0)), pl.BlockSpec((n, dm), lambda i: (0, 0))],
        out_specs=pl.BlockSpec((n, dm), lambda i: (0, 0)),
        out_shape=jax.ShapeDtypeStruct((n, dm), BF16),
    )(gates, gates, conv_out, attn)


def _ffn_up_kernel(x_ref, xh_ref, wg_ref, wv_ref, dg_ref, dv_ref, bg_ref, bv_ref, pg_ref, pv_ref,
                   o_ref, sg_ref, sv_ref, eg_ref, ev_ref, *, tm, rs, width, tiles_per_seq):
    hist = width - 1
    pad = xh_ref.shape[0]
    first = pl.program_id(0) % tiles_per_seq == 0
    branches = ((wg_ref, dg_ref, bg_ref, pg_ref, eg_ref, sg_ref),
                (wv_ref, dv_ref, bv_ref, pv_ref, ev_ref, sv_ref))
    for w_ref, _, _, prev_ref, e_ref, _ in branches:
        e_ref[pl.ds(0, pad), :] = jnp.dot(xh_ref[...], w_ref[...], preferred_element_type=F32)

        @pl.when(first)
        def _():
            e_ref[pl.ds(pad - hist, hist), :] = prev_ref[...]

    def project(r):
        for w_ref, _, _, _, e_ref, _ in branches:
            e_ref[pl.ds(pad + r * rs, rs), :] = jnp.dot(x_ref[pl.ds(r * rs, rs), :], w_ref[...],
                                                        preferred_element_type=F32)

    def activate(r):
        outs = []
        for _, d_ref, b_ref, _, e_ref, _ in branches:
            acc = b_ref[...] + d_ref[pl.ds(hist, 1), :] * e_ref[pl.ds(pad + r * rs, rs), :]
            for j in range(hist):
                acc = acc + d_ref[pl.ds(j, 1), :] * e_ref[pl.ds(pad + r * rs - hist + j, rs), :]
            outs.append(acc)
        g, v = outs
        o_ref[pl.ds(r * rs, rs), :] = (g * _sigmoid(g) * v).astype(o_ref.dtype)

    n_sub = tm // rs
    project(0)
    for r in range(1, n_sub):
        project(r)
        activate(r - 1)
    activate(n_sub - 1)
    for _, _, _, _, e_ref, s_ref in branches:
        s_ref[...] = e_ref[pl.ds(pad + tm - hist, hist), :]


def ffn_up_act(xn, w_up, prev, dw, bdw, n_seq, seq, tm, tn):
    m, k = xn.shape
    f2 = w_up.shape[1]
    f = f2 // 2
    width = dw.shape[0]
    ncb = f // tn
    pad = 16
    tps = seq // tm
    assert seq % tm == 0 and tm % pad == 0 and f % tn == 0 and width - 1 <= pad
    rs = min(256, tm // 2)
    assert tm % rs == 0 and rs % 8 == 0 and k % 256 == 0
    kern = functools.partial(_ffn_up_kernel, tm=tm, rs=rs, width=width, tiles_per_seq=tps)
    hb = tm // pad
    st_shape = jax.ShapeDtypeStruct((m // tm, width - 1, f), F32)
    act, sg, sv = pl.pallas_call(
        kern,
        grid=(m // tm, ncb),
        in_specs=[pl.BlockSpec((tm, k), lambda i, c: (i, 0)),
                  pl.BlockSpec((pad, k), lambda i, c: (jnp.maximum(i * hb - 1, 0), 0)),
                  pl.BlockSpec((k, tn), lambda i, c: (0, c)),
                  pl.BlockSpec((k, tn), lambda i, c: (0, c + ncb)),
                  pl.BlockSpec((width, tn), lambda i, c: (0, c)),
                  pl.BlockSpec((width, tn), lambda i, c: (0, c + ncb)),
                  pl.BlockSpec((1, tn), lambda i, c: (0, c)),
                  pl.BlockSpec((1, tn), lambda i, c: (0, c + ncb)),
                  pl.BlockSpec((None, width - 1, tn), lambda i, c: (i // tps, 0, c)),
                  pl.BlockSpec((None, width - 1, tn), lambda i, c: (i // tps, 0, c + ncb))],
        out_specs=[pl.BlockSpec((tm, tn), lambda i, c: (i, c)),
                   pl.BlockSpec((None, width - 1, tn), lambda i, c: (i, 0, c)),
                   pl.BlockSpec((None, width - 1, tn), lambda i, c: (i, 0, c))],
        out_shape=[jax.ShapeDtypeStruct((m, f), BF16), st_shape, st_shape],
        scratch_shapes=[pltpu.VMEM((pad + tm, tn), F32), pltpu.VMEM((pad + tm, tn), F32)],
        compiler_params=_cparams(2),
        name="ffn_up_act",
    )(xn, xn, w_up, w_up, dw, dw, bdw.reshape(1, f2), bdw.reshape(1, f2), prev, prev)
    state = jnp.concatenate([sg, sv], axis=-1).reshape(n_seq, tps, width - 1, f2)[:, tps - 1]
    return act, state


def _ffn_act_step_kernel(ug_ref, uv_ref, pg_ref, pv_ref, wg_ref, wv_ref, bg_ref, bv_ref, o_ref, *, width):
    def conv(u_ref, prev_ref, w_ref, b_ref):
        acc = b_ref[...] + w_ref[pl.ds(width - 1, 1), :] * u_ref[...]
        for j in range(width - 1):
            acc = acc + w_ref[pl.ds(j, 1), :] * prev_ref[j]
        return acc
    g = conv(ug_ref, pg_ref, wg_ref, bg_ref)
    v = conv(uv_ref, pv_ref, wv_ref, bv_ref)
    o_ref[...] = (g * _sigmoid(g) * v).astype(o_ref.dtype)


def ffn_act_step(u, prev_t, dw, bdw, tc):
    n, f2 = u.shape
    f = f2 // 2
    width = dw.shape[0]
    ncb = f // tc
    kern = functools.partial(_ffn_act_step_kernel, width=width)
    return pl.pallas_call(
        kern,
        grid=(ncb,),
        in_specs=[pl.BlockSpec((n, tc), lambda c: (0, c)),
                  pl.BlockSpec((n, tc), lambda c: (0, c + ncb)),
                  pl.BlockSpec((width - 1, n, tc), lambda c: (0, 0, c)),
                  pl.BlockSpec((width - 1, n, tc), lambda c: (0, 0, c + ncb)),
                  pl.BlockSpec((width, tc), lambda c: (0, c)),
                  pl.BlockSpec((width, tc), lambda c: (0, c + ncb)),
                  pl.BlockSpec((1, tc), lambda c: (0, c)),
                  pl.BlockSpec((1, tc), lambda c: (0, c + ncb))],
        out_specs=pl.BlockSpec((n, tc), lambda c: (0, c)),
        out_shape=jax.ShapeDtypeStruct((n, f), BF16),
        compiler_params=_cparams(1),
    )(u, u, prev_t, prev_t, dw, dw, bdw.reshape(1, f2), bdw.reshape(1, f2))


def _row_tile(m, cap):
    tm = cap
    while m % tm:
        tm //= 2
    return tm if tm >= 16 else m


def _in_projection(xn, w_main, w_kw, w_g, sizes, tm):
    glu_w, q_w, k_w, v_w, qi_w, ki_w, wi_w, ga_w, gb_w = sizes
    offs = np.concatenate([[0], np.cumsum(sizes)])
    tn = 512
    t = dict(w_transposed=True)
    glu = matmul_w(xn, w_main, int(offs[0]), glu_w, tm, tn, name="mm_glu", **t)
    q = matmul_w(xn, w_main, int(offs[1]), q_w, tm, tn, name="mm_q", out_dtype=BF16, **t)
    k = matmul_w(xn, w_main, int(offs[2]), k_w, tm, min(tn, k_w), name="mm_k", **t)
    v = matmul_w(xn, w_main, int(offs[3]), v_w, tm, min(tn, v_w), name="mm_v", **t)
    qi = matmul_w(xn, w_main, int(offs[4]), qi_w, tm, min(tn, qi_w), name="mm_qi", out_dtype=BF16, **t)
    kw = matmul_w(xn, w_kw, 0, LANES, tm, LANES, name="mm_kw", **t)
    gates = matmul_w(xn, w_g, 0, ga_w + gb_w, tm, tn, name="mm_gates", **t)
    return glu, q, k, v, qi, kw, gates


def kernel(x_prompt, x_sample, cache_k, cache_v, cache_kidx, state_conv, state_ffn, page_table, rel_bias,
           norm_attn, w_in, dw_conv, b_dw_conv, ln_conv_g, ln_conv_b, w_conv_out, w_o, norm_ffn, w_up, dw_ffn,
           b_dw_ffn, w_down, norm_final):
    bsz, seq, dm = x_prompt.shape
    nd, dec_seq, _ = x_sample.shape
    depth, n_pool, page, n_kv, hd = cache_k.shape
    idx_dim = cache_kidx.shape[-1]
    n_pages = page_table.shape[1]
    past = n_pages * page
    width, dconv = dw_conv.shape[1:]
    fwidth = dw_ffn.shape[1]
    f = w_down.shape[1]
    n_heads = w_o.shape[1] // hd
    d_attn = n_heads * hd
    d_kv = n_kv * hd
    n_in = w_in.shape[2]
    idx_heads = (n_in - 2 * dconv - d_attn - 2 * d_kv - idx_dim - 2 * dm) // (idx_dim + 1)
    sizes = (2 * dconv, d_attn, d_kv, d_kv, idx_heads * idx_dim, idx_dim, idx_heads, dm, dm)
    assert sum(sizes) == n_in and depth == 1 and dec_seq == 1 and page == LANES and d_attn == dm

    mp = bsz * seq
    xp = x_prompt.reshape(mp, dm)
    xs = x_sample.reshape(nd, dm)
    bias3, bias_s = bias_tables(rel_bias, page, n_kv)
    tmp = _row_tile(mp, 2048)
    drop = lambda a: a.reshape(a.shape[1:])
    (norm_attn, w_in, dw_conv, b_dw_conv, ln_conv_g, ln_conv_b, w_conv_out, w_o, norm_ffn, w_up, dw_ffn,
     b_dw_ffn, w_down, state_conv, state_ffn) = map(drop, (
         norm_attn, w_in, dw_conv, b_dw_conv, ln_conv_g, ln_conv_b, w_conv_out, w_o, norm_ffn, w_up, dw_ffn,
         b_dw_ffn, w_down, state_conv, state_ffn))
    kidx_pool = jnp.swapaxes(cache_kidx.reshape(n_pool, page, idx_dim), 1, 2)
    k_pool = cache_k.reshape(n_pool, page * n_kv, hd)
    v_pool = cache_v.reshape(n_pool, page * n_kv, hd)

    n_aligned = sum(sizes[:5])
    n_small = idx_dim + idx_heads
    w_in_t = jnp.swapaxes(w_in, 0, 1)
    w_main = cast_rows_bf16(w_in_t, 0, n_aligned)
    w_kw = cast_rows_bf16(w_in_t, n_aligned, LANES, n_small)
    w_g = cast_rows_bf16(w_in_t, n_aligned + n_small, 2 * dm)
    w_o, w_up, w_down = cast_bf16(w_o), cast_bf16(w_up), cast_bf16(w_down)

    xn = rmsnorm_rows(xp, norm_attn, BF16, 512)
    glu, q, k, v, qi, kw, gates = _in_projection(xn, w_main, w_kw, w_g, sizes, tmp)
    conv0 = jnp.zeros((bsz, width - 1, dconv), F32)
    conv_out, conv_state_p = conv_branch_prompt(glu, conv0, dw_conv, b_dw_conv, ln_conv_g, ln_conv_b,
                                                w_conv_out, bsz, seq, 256)
    mixed = attn_prompt(qi, kw, q, k, v, conv_out, gates, bias3, bsz, seq, n_kv, idx_dim, idx_heads)
    x2 = matmul_w(mixed, w_o, 0, dm, tmp, 512, res=xp, name="mm_o")
    xn2 = rmsnorm_rows(x2, norm_ffn, BF16, 512)
    ffn0 = jnp.zeros((bsz, fwidth - 1, 2 * f), F32)
    act, ffn_state_p = ffn_up_act(xn2, w_up, ffn0, dw_ffn, b_dw_ffn, bsz, seq, _row_tile(seq, 2048), 512)
    x3 = matmul_w(act, w_down, 0, dm, _row_tile(mp, 1024), 256, res=x2, name="mm_down")
    y_prompt = rmsnorm_rows(x3, norm_final, F32, 512).reshape(bsz, seq, dm)

    xns = rmsnorm_rows(xs, norm_attn, BF16, nd)
    glu_s, q_s, k_s, v_s, qi_s, kw_s, gates_s = _in_projection(xns, w_main, w_kw, w_g, sizes, nd)
    ki_s = kw_s[:, :idx_dim]
    wi_s = kw_s[:, idx_dim:idx_dim + idx_heads]
    sc_prev_t = jnp.swapaxes(state_conv, 0, 1)
    conv_out_s, u_conv_s = conv_branch_step(glu_s, sc_prev_t, dw_conv, b_dw_conv, ln_conv_g, ln_conv_b,
                                            w_conv_out)
    conv_state_s = jnp.concatenate([state_conv[:, 1:], u_conv_s[:, None, :]], axis=1)

    pg = 16 if n_pages % 16 == 0 else 8
    scores3, sself3 = sample_scores(page_table, qi_s.reshape(nd, idx_heads, idx_dim),
                                    wi_s.reshape(nd, idx_heads, 1), ki_s.reshape(nd, 1, idx_dim), kidx_pool, pg)
    topk_s = min(TOPK_MAX, (past + dec_seq) // 4)
    sel4, selself = sample_select(scores3.reshape(nd, past), sself3.reshape(nd, LANES), topk_s, n_kv)
    group = n_heads // n_kv
    k_rep = jnp.repeat(k_s.reshape(nd, n_kv, hd), group, axis=1)
    v_rep = jnp.repeat(v_s.reshape(nd, n_kv, hd), group, axis=1)
    attn_s = sample_attention(page_table, q_s.reshape(nd, n_heads, hd), k_rep, v_rep,
                              sel4.reshape(nd, n_pages // pg, 1, pg * page * n_kv), selself.reshape(nd, 1, LANES),
                              bias_s,
                              rel_bias[0].reshape(n_heads, 1), k_pool, v_pool, pg, n_kv)
    mixed_s = gated_mix(gates_s, conv_out_s, attn_s.reshape(nd, dm))
    x2s = matmul_w(mixed_s, w_o, 0, dm, nd, 512, res=xs)
    xn2s = rmsnorm_rows(x2s, norm_ffn, BF16, nd)
    u_s = matmul_w(xn2s, w_up, 0, 2 * f, nd, 512)
    sf_prev_t = jnp.swapaxes(state_ffn, 0, 1)
    act_s = ffn_act_step(u_s, sf_prev_t, dw_ffn, b_dw_ffn, 512)
    x3s = matmul_w(act_s, w_down, 0, dm, nd, 256, res=x2s)
    y_sample = rmsnorm_rows(x3s, norm_final, F32, nd).reshape(nd, dec_seq, dm)
    ffn_state_s = jnp.concatenate([state_ffn[:, 1:], u_s[:, None, :]], axis=1)

    return (y_prompt, y_sample,
            k.reshape(1, bsz, seq, n_kv, hd), v.reshape(1, bsz, seq, n_kv, hd),
            kw.reshape(bsz, seq, LANES)[None, :, :, :idx_dim],
            conv_state_p[None], ffn_state_p[None],
            k_s.reshape(1, nd, dec_seq, n_kv, hd), v_s.reshape(1, nd, dec_seq, n_kv, hd),
            ki_s.reshape(1, nd, dec_seq, idx_dim),
            conv_state_s[None], ffn_state_s[None])
```

```python
import functools
import math

import numpy as np
import jax
import jax.numpy as jnp
from jax import lax
from jax.experimental import pallas as pl
from jax.experimental.pallas import tpu as pltpu

F32 = jnp.float32
BF16 = jnp.bfloat16

EPS = 1e-6
TOPK_MAX = 256
N_BUCKETS = 32
MAX_DISTANCE = 128
QB = 128
CK = 256
LANES = 128
NEG = -1e30
LOG2E = math.log2(math.e)
VMEM_LIMIT = 56 * 1024 * 1024


def _cparams(n_axes, vmem=VMEM_LIMIT):
    return pltpu.CompilerParams(dimension_semantics=("arbitrary",) * n_axes, vmem_limit_bytes=vmem)


def _dot_nt(a, b):
    return lax.dot_general(a, b, (((1,), (1,)), ((), ())), preferred_element_type=F32)


def _sigmoid(x):
    return 1.0 / (1.0 + jnp.exp(-x))


def _fold_rows(x, op):
    while x.shape[0] > 8:
        half = x.shape[0] // 2
        x = op(x[:half], x[half:])
    return x


def _rel_bucket_np(dist):
    n = np.maximum(dist, 0)
    max_exact = N_BUCKETS // 2
    nf = np.maximum(n, 1).astype(np.float32)
    large = max_exact + (np.log(nf / np.float32(max_exact)) / np.float32(math.log(MAX_DISTANCE / max_exact))
                         * np.float32(N_BUCKETS - max_exact)).astype(np.int32)
    large = np.minimum(large, N_BUCKETS - 1)
    return np.where(n < max_exact, n, large).astype(np.int32)


def _rms_kernel(x_ref, g_ref, o_ref):
    x = x_ref[...]
    y = x * lax.rsqrt(jnp.mean(x * x, axis=-1, keepdims=True) + EPS) * g_ref[...]
    o_ref[...] = y.astype(o_ref.dtype)


def rmsnorm_rows(x, g, out_dtype, tm):
    m, d = x.shape
    return pl.pallas_call(
        _rms_kernel,
        grid=(m // tm,),
        in_specs=[pl.BlockSpec((tm, d), lambda i: (i, 0)), pl.BlockSpec((1, d), lambda i: (0, 0))],
        out_specs=pl.BlockSpec((tm, d), lambda i: (i, 0)),
        out_shape=jax.ShapeDtypeStruct((m, d), out_dtype),
        compiler_params=_cparams(1),
        name="rmsnorm",
    )(x, g.reshape(1, d))


def _cast_kernel(w_ref, o_ref):
    o_ref[...] = w_ref[...].astype(o_ref.dtype)


def cast_bf16(w, ncols=None):
    k, n = w.shape
    ncols = n if ncols is None else ncols
    tk = 512 if k % 512 == 0 else k
    tn = 1024 if ncols % 1024 == 0 else (512 if ncols % 512 == 0 else ncols)
    return pl.pallas_call(
        _cast_kernel,
        grid=(k // tk, ncols // tn),
        in_specs=[pl.BlockSpec((tk, tn), lambda i, j: (i, j))],
        out_specs=pl.BlockSpec((tk, tn), lambda i, j: (i, j)),
        out_shape=jax.ShapeDtypeStruct((k, ncols), BF16),
        compiler_params=_cparams(2),
        name="cast_bf16",
    )(w)


def _cast_rows_kernel(w_ref, o_ref, *, valid):
    w = w_ref[...]
    if valid < w.shape[0]:
        w = jnp.where(lax.broadcasted_iota(jnp.int32, w.shape, 0) < valid, w, 0.0)
    o_ref[...] = w.astype(o_ref.dtype)


def cast_rows_bf16(wt, row0, nrows, nvalid=None):
    n, k = wt.shape
    tr = 512 if nrows % 512 == 0 else nrows
    nvalid = nrows if nvalid is None else nvalid
    assert row0 % 8 == 0 and (nvalid == nrows or tr == nrows) and row0 + nrows <= n
    return pl.pallas_call(
        functools.partial(_cast_rows_kernel, valid=nvalid),
        grid=(nrows // tr,),
        in_specs=[pl.BlockSpec((pl.Element(tr), pl.Element(k)), lambda i: (pl.multiple_of(row0 + i * tr, 8), 0))],
        out_specs=pl.BlockSpec((tr, k), lambda i: (i, 0)),
        out_shape=jax.ShapeDtypeStruct((nrows, k), BF16),
        compiler_params=_cparams(1),
        name="cast_rows_bf16",
    )(wt)


def _mm_kernel(a_ref, w_ref, o_ref):
    o_ref[...] = jnp.dot(a_ref[...], w_ref[...], preferred_element_type=F32).astype(o_ref.dtype)


def _mm_res_kernel(a_ref, w_ref, r_ref, o_ref):
    o_ref[...] = r_ref[...] + jnp.dot(a_ref[...], w_ref[...], preferred_element_type=F32)


def _mm_nt_kernel(a_ref, wt_ref, o_ref):
    o_ref[...] = _dot_nt(a_ref[...], wt_ref[...]).astype(o_ref.dtype)


def matmul_w(a, w, col0, ncols, tm, tn, res=None, name="matmul", out_dtype=F32, w_transposed=False):
    m, k = a.shape
    assert col0 % tn == 0 and ncols % tn == 0 and m % tm == 0
    cb = col0 // tn
    if w_transposed:
        assert res is None
        w_spec = pl.BlockSpec((tn, k), lambda i, j: (j + cb, 0))
    else:
        w_spec = pl.BlockSpec((k, tn), lambda i, j: (0, j + cb))
    in_specs = [pl.BlockSpec((tm, k), lambda i, j: (i, 0)), w_spec]
    args = [a, w]
    kern = _mm_nt_kernel if w_transposed else _mm_kernel
    if res is not None:
        in_specs.append(pl.BlockSpec((tm, tn), lambda i, j: (i, j)))
        args.append(res)
        kern = _mm_res_kernel
    return pl.pallas_call(
        kern,
        grid=(m // tm, ncols // tn),
        in_specs=in_specs,
        out_specs=pl.BlockSpec((tm, tn), lambda i, j: (i, j)),
        out_shape=jax.ShapeDtypeStruct((m, ncols), out_dtype),
        compiler_params=_cparams(2),
        name=name,
    )(*args)


def _bias_kernel(rb_ref, bk3_ref, bks_ref, o3_ref, os_ref, *, n_heads):
    def head(h, carry):
        for t in range(3):
            bk = bk3_ref[t]
            acc = jnp.zeros(bk.shape, F32)
            for b in range(N_BUCKETS):
                acc = jnp.where(bk == b, rb_ref[b, h], acc)
            o3_ref[t, h] = acc * LOG2E
        for t in range(2):
            bk = bks_ref[t]
            acc = jnp.zeros(bk.shape, F32)
            for b in range(N_BUCKETS):
                acc = jnp.where(bk == b, rb_ref[b, h], acc)
            os_ref[t, pl.ds(h, 1), :] = acc
        return carry
    lax.fori_loop(0, n_heads, head, 0)


def bias_tables(rel_bias, page, rep):
    n_heads = rel_bias.shape[1]
    cols = page * rep
    i = np.arange(QB)[None, :]
    k = np.arange(QB)[:, None]
    bk3 = np.stack([_rel_bucket_np(i - k + 2 * QB), _rel_bucket_np(i - k + QB), _rel_bucket_np(i - k)])
    assert (_rel_bucket_np(np.arange(QB + 1, 1 << 20)) == N_BUCKETS - 1).all()
    assert (bk3[0] == N_BUCKETS - 1).all()
    assert page >= QB
    bks = np.stack([np.full((1, cols), N_BUCKETS - 1, np.int32),
                    _rel_bucket_np(page - np.arange(cols) // rep)[None, :]])
    return pl.pallas_call(
        functools.partial(_bias_kernel, n_heads=n_heads),
        in_specs=[pl.BlockSpec(memory_space=pltpu.SMEM),
                  pl.BlockSpec(memory_space=pltpu.VMEM), pl.BlockSpec(memory_space=pltpu.VMEM)],
        out_specs=[pl.BlockSpec(memory_space=pltpu.VMEM), pl.BlockSpec(memory_space=pltpu.VMEM)],
        out_shape=[jax.ShapeDtypeStruct((3, n_heads, QB, QB), F32),
                   jax.ShapeDtypeStruct((2, n_heads, cols), F32)],
    )(rel_bias, jnp.asarray(bk3), jnp.asarray(bks))


def _conv_kernel(glu_ref, prev_ref, dw_ref, bdw_ref, lng_ref, lnb_ref, wout_ref, o_ref, st_ref,
                 ext_ref, h_ref, wbf_ref, sh_ref, *, tt, width, dconv):
    b = pl.program_id(0)
    t = pl.program_id(1)
    pad = 32
    hist = width - 1

    @pl.when((b == 0) & (t == 0))
    def _():
        wbf_ref[...] = wout_ref[...].astype(BF16)

    @pl.when(t == 0)
    def _():
        ext_ref[pl.ds(pad - hist, hist), :] = prev_ref[...]

    @pl.when(t > 0)
    def _():
        ext_ref[pl.ds(0, pad), :] = ext_ref[pl.ds(tt, pad), :]

    glu = glu_ref[...]
    u = glu[:, :dconv] * _sigmoid(glu[:, dconv:])
    ext_ref[pl.ds(pad, tt), :] = u
    st_ref[...] = ext_ref[pl.ds(pad + tt - hist, hist), :]

    span = tt + pad - 8
    for s in range(1, 8):
        sh_ref[s - 1] = ext_ref[pl.ds(s, span), :]

    for c in range(dconv // LANES):
        cs = slice(c * LANES, (c + 1) * LANES)
        acc = jnp.zeros((tt, LANES), F32) + bdw_ref[:, cs]
        for j in range(width):
            a, s = divmod(pad - hist + j, 8)
            rows = ext_ref[pl.ds(8 * a, tt), cs] if s == 0 else sh_ref[s - 1, pl.ds(8 * a, tt), cs]
            acc = acc + dw_ref[pl.ds(j, 1), cs] * rows
        h_ref[:, cs] = acc

    h = h_ref[...]
    mu = jnp.mean(h, axis=-1, keepdims=True)
    var = jnp.mean(jnp.square(h - mu), axis=-1, keepdims=True)
    y = (h - mu) * lax.rsqrt(var + EPS) * lng_ref[...] + lnb_ref[...]
    y = y * _sigmoid(y)
    o_ref[...] = jnp.dot(y.astype(BF16), wbf_ref[...], preferred_element_type=F32)


def conv_branch_prompt(glu_pre, prev, dw, bdw, lng, lnb, wout, n_seq, seq, tt):
    width, dconv = dw.shape
    dm = wout.shape[1]
    nt = seq // tt
    kern = functools.partial(_conv_kernel, tt=tt, width=width, dconv=dconv)
    return pl.pallas_call(
        kern,
        grid=(n_seq, nt),
        in_specs=[pl.BlockSpec((tt, 2 * dconv), lambda b, t: (b * nt + t, 0)),
                  pl.BlockSpec((None, width - 1, dconv), lambda b, t: (b, 0, 0)),
                  pl.BlockSpec((width, dconv), lambda b, t: (0, 0)),
                  pl.BlockSpec((1, dconv), lambda b, t: (0, 0)),
                  pl.BlockSpec((1, dconv), lambda b, t: (0, 0)),
                  pl.BlockSpec((1, dconv), lambda b, t: (0, 0)),
                  pl.BlockSpec((dconv, dm), lambda b, t: (0, 0))],
        out_specs=[pl.BlockSpec((tt, dm), lambda b, t: (b * nt + t, 0)),
                   pl.BlockSpec((None, width - 1, dconv), lambda b, t: (b, 0, 0))],
        out_shape=[jax.ShapeDtypeStruct((n_seq * seq, dm), F32),
                   jax.ShapeDtypeStruct((n_seq, width - 1, dconv), F32)],
        scratch_shapes=[pltpu.VMEM((32 + tt, dconv), F32), pltpu.VMEM((tt, dconv), F32),
                        pltpu.VMEM((dconv, dm), BF16), pltpu.VMEM((7, 24 + tt, dconv), F32)],
        compiler_params=_cparams(2),
        name="conv_branch",
    )(glu_pre, prev, dw, bdw.reshape(1, dconv), lng.reshape(1, dconv), lnb.reshape(1, dconv), wout)


def _conv_step_kernel(glu_ref, prev_ref, dw_ref, bdw_ref, lng_ref, lnb_ref, wout_ref, o_ref, u_ref,
                      *, width, dconv):
    glu = glu_ref[...]
    u = glu[:, :dconv] * _sigmoid(glu[:, dconv:])
    u_ref[...] = u
    h = bdw_ref[...] + dw_ref[pl.ds(width - 1, 1), :] * u
    for j in range(width - 1):
        h = h + dw_ref[pl.ds(j, 1), :] * prev_ref[j]
    mu = jnp.mean(h, axis=-1, keepdims=True)
    var = jnp.mean(jnp.square(h - mu), axis=-1, keepdims=True)
    y = (h - mu) * lax.rsqrt(var + EPS) * lng_ref[...] + lnb_ref[...]
    y = y * _sigmoid(y)
    o_ref[...] = jnp.dot(y.astype(BF16), wout_ref[...].astype(BF16), preferred_element_type=F32)


def conv_branch_step(glu_pre, prev_t, dw, bdw, lng, lnb, wout):
    width, dconv = dw.shape
    n = glu_pre.shape[0]
    dm = wout.shape[1]
    kern = functools.partial(_conv_step_kernel, width=width, dconv=dconv)
    return pl.pallas_call(
        kern,
        out_shape=[jax.ShapeDtypeStruct((n, dm), F32), jax.ShapeDtypeStruct((n, dconv), F32)],
        compiler_params=pltpu.CompilerParams(vmem_limit_bytes=VMEM_LIMIT),
    )(glu_pre, prev_t, dw, bdw.reshape(1, dconv), lng.reshape(1, dconv), lnb.reshape(1, dconv), wout)


def _select_threshold(count_gt, row_min, row_max, n_adm, topk, any_fn):
    kf = jnp.float32(topk)
    full = n_adm <= kf
    lo0 = row_min - (1.0 + jnp.abs(row_min))
    hi0 = row_max
    flo0 = jnp.where(full, kf, n_adm)
    fhi0 = jnp.zeros_like(lo0)

    def active_rows(lo, hi, flo):
        mid = 0.5 * lo + 0.5 * hi
        return (flo != kf) & (lo < mid) & (mid < hi)

    def cond(st):
        lo, hi, flo, fhi = st
        return any_fn(active_rows(lo, hi, flo))

    def step(st):
        lo, hi, flo, fhi = st
        act = active_rows(lo, hi, flo)
        mid = 0.5 * lo + 0.5 * hi
        c = count_gt(mid)
        up = act & (c >= kf)
        dn = act & (c < kf)
        return (jnp.where(up, mid, lo), jnp.where(dn, mid, hi),
                jnp.where(up, c, flo), jnp.where(dn, c, fhi))

    def body(st):
        for _ in range(4):
            st = step(st)
        return st

    lo, hi, flo, fhi = lax.while_loop(cond, body, (lo0, hi0, flo0, fhi0))
    lo = jnp.where(full, -jnp.inf, lo)
    return lo, hi, flo, fhi


def _attn_prompt_kernel(qi_ref, wi_ref, kw_ref, q_ref, k_ref, v_ref, co_ref, ga_ref, gb_ref, bias_ref,
                        o_ref, kd_ref, kb_ref, vt_ref, wit_ref, qib_ref, sc_ref, sel_ref, qs_ref,
                        m_ref, l_ref, acc_ref, st_ref,
                        *, seq, n_heads, n_kv, idx_heads, idx_dim, topk, hd):
    j = pl.program_id(1)
    group = n_heads // n_kv
    nck = (j * QB + QB + CK - 1) // CK
    kf = jnp.float32(topk)

    @pl.when(j == 0)
    def _():
        kd_ref[...] = kw_ref[:, :idx_dim].astype(BF16)
        for g in range(n_kv):
            kb_ref[g] = k_ref[:, g * hd:(g + 1) * hd].astype(BF16)
            for c in range(seq // CK):
                vt_ref[g, c] = v_ref[c * CK:(c + 1) * CK, g * hd:(g + 1) * hd].T.astype(BF16)

    wit_ref[...] = (wi_ref[...] * (idx_heads ** -0.5)).T
    qi = qi_ref[...].astype(F32) * (idx_dim ** -0.5)
    for h in range(idx_heads):
        qib_ref[h] = qi[:, h * idx_dim:(h + 1) * idx_dim].astype(BF16)
    qpos = j * QB + lax.broadcasted_iota(jnp.int32, (CK, QB), 1)
    krow = lax.broadcasted_iota(jnp.int32, (CK, QB), 0)
    per_dot = 2 * LANES // QB

    def score_chunk(c, carry):
        mn, mx = carry
        k0 = pl.multiple_of(c * CK, CK)
        kc = kd_ref[pl.ds(k0, CK), :]
        acc = jnp.zeros((CK, QB), F32)
        for p in range(idx_heads // per_dot):
            rhs = qib_ref[pl.ds(p * per_dot, per_dot)].reshape(per_dot * QB, idx_dim)
            s = _dot_nt(kc, rhs)
            for r in range(per_dot):
                acc = acc + (wit_ref[pl.ds(idx_dim + p * per_dot + r, 1), :]
                             * jnp.maximum(s[:, r * QB:(r + 1) * QB], 0.0))
        adm = (krow + c * CK) <= qpos
        sc_ref[c] = jnp.where(adm, acc, -jnp.inf)
        mn = jnp.minimum(mn, _fold_rows(jnp.where(adm, acc, jnp.inf), jnp.minimum))
        mx = jnp.maximum(mx, _fold_rows(jnp.where(adm, acc, -jnp.inf), jnp.maximum))
        return mn, mx

    mn8, mx8 = lax.fori_loop(0, nck, score_chunk,
                             (jnp.full((8, QB), jnp.inf, F32), jnp.full((8, QB), -jnp.inf, F32)))
    row_min = jnp.min(mn8, axis=0, keepdims=True)
    row_max = jnp.max(mx8, axis=0, keepdims=True)
    n_adm = (j * QB + 1 + lax.broadcasted_iota(jnp.int32, (1, QB), 1)).astype(F32)

    def count_gt(t):
        def cbody(c, acc):
            return acc + _fold_rows(jnp.where(sc_ref[c] > t, 1.0, 0.0), jnp.add)
        part = lax.fori_loop(0, nck, cbody, jnp.zeros((8, QB), F32))
        return jnp.sum(part, axis=0, keepdims=True)

    def any_fn(mask):
        return jnp.max(jnp.where(mask, 1.0, 0.0)) > 0.0

    lo, hi, flo, fhi = _select_threshold(count_gt, row_min, row_max, n_adm, topk, any_fn)
    tie = flo != kf

    def sel_chunk(c, carry):
        sel_ref[c] = jnp.where(sc_ref[c] > lo, 1.0, 0.0)
        return carry
    lax.fori_loop(0, nck, sel_chunk, 0)

    @pl.when(any_fn(tie))
    def _():
        need = kf - fhi
        lower = (lax.broadcasted_iota(jnp.int32, (CK, CK), 1)
                 < lax.broadcasted_iota(jnp.int32, (CK, CK), 0)).astype(BF16)

        def tie_chunk(c, before):
            s = sc_ref[c]
            eq = s == hi
            eqf = jnp.where(eq, 1.0, 0.0)
            rank = before + jnp.dot(lower, eqf.astype(BF16), preferred_element_type=F32)
            keep = (s > hi) | (eq & (rank < need))
            sel_ref[c] = jnp.where(tie, jnp.where(keep, 1.0, 0.0), sel_ref[c])
            return before + jnp.sum(eqf, axis=0, keepdims=True)
        lax.fori_loop(0, nck, tie_chunk, jnp.zeros((1, QB), F32))

    scale2 = hd ** -0.5 * LOG2E
    for h in range(n_heads):
        qs_ref[h] = q_ref[:, h * hd:(h + 1) * hd].astype(BF16)
    m_ref[...] = jnp.full(m_ref.shape, NEG, F32)
    l_ref[...] = jnp.zeros(l_ref.shape, F32)
    acc_ref[...] = jnp.zeros(acc_ref.shape, F32)

    def logits(g, c):
        kc = kb_ref[g, pl.ds(pl.multiple_of(c * CK, CK), CK), :]
        qg = qs_ref[pl.ds(g * group, group)].reshape(group * QB, hd)
        st_ref[g % 2] = _dot_nt(kc, qg)

    logits(0, 0)

    def att_chunk(c, carry):
        msk = jnp.concatenate([sel_ref[c]] * group, axis=1) > 0.0
        tis = [jnp.clip(c * (CK // QB) + s - j + 2, 0, 2) for s in range(CK // QB)]
        for g in range(n_kv):
            if g + 1 < n_kv:
                logits(g + 1, c)
            else:
                logits(0, jnp.minimum(c + 1, nck - 1))
            rows = [jnp.concatenate([bias_ref[ti, g * group + hh] for hh in range(group)], axis=1)
                    for ti in tis]
            lg = jnp.where(msk, st_ref[g % 2] * scale2 + jnp.concatenate(rows, axis=0), -jnp.inf)
            m_old = m_ref[g]
            m_new = jnp.maximum(m_old, jnp.max(_fold_rows(lg, jnp.maximum), axis=0, keepdims=True))
            alpha = jnp.exp2(m_old - m_new)
            p = jnp.exp2(lg - m_new)
            l_ref[g] = alpha * l_ref[g] + jnp.sum(_fold_rows(p, jnp.add), axis=0, keepdims=True)
            pv = jnp.dot(vt_ref[g, c], p.astype(BF16), preferred_element_type=F32)
            acc_ref[g] = alpha * acc_ref[g] + pv
            m_ref[g] = m_new
        return carry
    lax.fori_loop(0, nck, att_chunk, 0)

    for g in range(n_kv):
        ot = acc_ref[g] / l_ref[g]
        for hh in range(group):
            cs = slice((g * group + hh) * hd, (g * group + hh + 1) * hd)
            o = ot[:, hh * QB:(hh + 1) * QB].T
            mixed = _sigmoid(ga_ref[:, cs]) * co_ref[:, cs] + _sigmoid(gb_ref[:, cs]) * o
            o_ref[:, cs] = mixed.astype(o_ref.dtype)


def attn_prompt(qi, kw, q, k, v, conv_out, gates, bias3, n_seq, seq, n_kv, idx_dim, idx_heads):
    m, dm = q.shape
    hd = k.shape[1] // n_kv
    n_heads = dm // hd
    group = n_heads // n_kv
    nb = seq // QB
    topk = min(TOPK_MAX, seq // 4)
    assert seq % CK == 0 and LANES % idx_dim == 0 and kw.shape[1] == LANES
    kern = functools.partial(_attn_prompt_kernel, seq=seq, n_heads=n_heads, n_kv=n_kv, idx_heads=idx_heads,
                             idx_dim=idx_dim, topk=topk, hd=hd)
    row = lambda b, j: (b * nb + j, 0)
    return pl.pallas_call(
        kern,
        grid=(n_seq, nb),
        in_specs=[pl.BlockSpec((QB, idx_heads * idx_dim), row),
                  pl.BlockSpec((QB, LANES), row),
                  pl.BlockSpec((seq, LANES), lambda b, j: (b, 0)),
                  pl.BlockSpec((QB, dm), row),
                  pl.BlockSpec((seq, n_kv * hd), lambda b, j: (b, 0)),
                  pl.BlockSpec((seq, n_kv * hd), lambda b, j: (b, 0)),
                  pl.BlockSpec((QB, dm), row),
                  pl.BlockSpec((QB, dm), lambda b, j: (b * nb + j, 0)),
                  pl.BlockSpec((QB, dm), lambda b, j: (b * nb + j, 1)),
                  pl.BlockSpec((3, n_heads, QB, QB), lambda b, j: (0, 0, 0, 0))],
        out_specs=pl.BlockSpec((QB, dm), row),
        out_shape=jax.ShapeDtypeStruct((m, dm), BF16),
        scratch_shapes=[pltpu.VMEM((seq, idx_dim), BF16),
                        pltpu.VMEM((n_kv, seq, hd), BF16),
                        pltpu.VMEM((n_kv, seq // CK, hd, CK), BF16),
                        pltpu.VMEM((LANES, QB), F32),
                        pltpu.VMEM((idx_heads, QB, idx_dim), BF16),
                        pltpu.VMEM((seq // CK, CK, QB), F32),
                        pltpu.VMEM((seq // CK, CK, QB), F32),
                        pltpu.VMEM((n_heads, QB, hd), BF16),
                        pltpu.VMEM((n_kv, 1, group * QB), F32),
                        pltpu.VMEM((n_kv, 1, group * QB), F32),
                        pltpu.VMEM((n_kv, hd, group * QB), F32),
                        pltpu.VMEM((2, CK, group * QB), F32)],
        compiler_params=_cparams(2),
        name="attn_prompt",
    )(qi, kw, kw, q, k, v, conv_out, gates, gates, bias3)


def _sample_score_kernel(pt_ref, qi_ref, wi_ref, kn_ref, *rest, pg, idx_heads, idx_dim):
    pages = rest[:pg]
    o_ref, self_ref, kcat_ref = rest[pg:]
    page = pages[0].shape[1]
    qi = qi_ref[...] * (idx_dim ** -0.5)
    wi = wi_ref[...] * (idx_heads ** -0.5)
    qb = qi.astype(BF16)
    for i in range(pg):
        kcat_ref[:, i * page:(i + 1) * page] = pages[i][...].astype(BF16)
    s = jnp.dot(qb, kcat_ref[...], preferred_element_type=F32)
    o_ref[...] = jnp.sum(wi * jnp.maximum(s, 0.0), axis=0, keepdims=True)

    @pl.when(pl.program_id(1) == 0)
    def _():
        kn = kn_ref[...].astype(BF16).astype(F32)
        s = jnp.sum(qb.astype(F32) * kn, axis=1, keepdims=True)
        sself = jnp.sum(wi * jnp.maximum(s, 0.0), axis=0, keepdims=True)
        self_ref[...] = jnp.broadcast_to(sself, self_ref.shape)


def sample_scores(page_table, qi3, wi3, ki_new3, cache_kidx_t, pg):
    n, n_pages = page_table.shape
    idx_heads, idx_dim = qi3.shape[1:]
    page = cache_kidx_t.shape[2]
    kern = functools.partial(_sample_score_kernel, pg=pg, idx_heads=idx_heads, idx_dim=idx_dim)
    page_specs = [pl.BlockSpec((None, idx_dim, page), (lambda b, p, pt, i=i: (pt[b, p * pg + i], 0, 0)))
                  for i in range(pg)]
    grid_spec = pltpu.PrefetchScalarGridSpec(
        num_scalar_prefetch=1,
        grid=(n, n_pages // pg),
        in_specs=[pl.BlockSpec((None, idx_heads, idx_dim), lambda b, p, pt: (b, 0, 0)),
                  pl.BlockSpec((None, idx_heads, 1), lambda b, p, pt: (b, 0, 0)),
                  pl.BlockSpec((None, 1, idx_dim), lambda b, p, pt: (b, 0, 0))] + page_specs,
        out_specs=[pl.BlockSpec((None, None, 1, pg * page), lambda b, p, pt: (b, p, 0, 0)),
                   pl.BlockSpec((None, 1, LANES), lambda b, p, pt: (b, 0, 0))],
        scratch_shapes=[pltpu.VMEM((idx_dim, pg * page), BF16)],
    )
    return pl.pallas_call(
        kern,
        grid_spec=grid_spec,
        out_shape=[jax.ShapeDtypeStruct((n, n_pages // pg, 1, pg * page), F32),
                   jax.ShapeDtypeStruct((n, 1, LANES), F32)],
        compiler_params=_cparams(2),
        name="sample_scores",
    )(page_table, qi3, wi3, ki_new3, *([cache_kidx_t] * pg))


def _sample_select_kernel(sc_ref, self_ref, sel4_ref, selself_ref, sel_ref, *, topk, past, rep):
    sc = sc_ref[...]
    sself = self_ref[:, 0:1]
    n = sc.shape[0]
    kf = jnp.float32(topk)
    row_min = jnp.minimum(jnp.min(sc, axis=1, keepdims=True), sself)
    row_max = jnp.maximum(jnp.max(sc, axis=1, keepdims=True), sself)
    n_adm = jnp.full((n, 1), past + 1, F32)

    def count_gt(t):
        return (jnp.sum(jnp.where(sc > t, 1.0, 0.0), axis=1, keepdims=True)
                + jnp.where(sself > t, 1.0, 0.0))

    def any_fn(mask):
        return jnp.max(jnp.where(mask, 1.0, 0.0)) > 0.0

    lo, hi, flo, fhi = _select_threshold(count_gt, row_min, row_max, n_adm, topk, any_fn)
    tie = flo != kf
    sel_ref[...] = jnp.where(sc > lo, 1.0, 0.0)
    selself_ref[...] = jnp.broadcast_to(jnp.where(sself > lo, 1.0, 0.0), selself_ref.shape)

    @pl.when(any_fn(tie))
    def _():
        need = kf - fhi
        blk = 512
        tri = (lax.broadcasted_iota(jnp.int32, (blk, blk), 0)
               < lax.broadcasted_iota(jnp.int32, (blk, blk), 1)).astype(BF16)
        before = jnp.zeros((n, 1), F32)
        for c in range(past // blk):
            s = sc_ref[:, c * blk:(c + 1) * blk]
            eq = s == hi
            rank = before + jnp.dot(jnp.where(eq, 1.0, 0.0).astype(BF16), tri, preferred_element_type=F32)
            keep = (s > hi) | (eq & (rank < need))
            sel_ref[:, c * blk:(c + 1) * blk] = jnp.where(tie, jnp.where(keep, 1.0, 0.0),
                                                          sel_ref[:, c * blk:(c + 1) * blk])
            before = before + jnp.sum(jnp.where(eq, 1.0, 0.0), axis=1, keepdims=True)
        keep_self = (sself > hi) | ((sself == hi) & (before < need))
        selself_ref[...] = jnp.broadcast_to(
            jnp.where(tie, jnp.where(keep_self, 1.0, 0.0), jnp.where(sself > lo, 1.0, 0.0)), selself_ref.shape)

    blk = 512
    row_lo = lax.broadcasted_iota(jnp.int32, (blk, blk * rep), 0) * rep
    col = lax.broadcasted_iota(jnp.int32, (blk, blk * rep), 1)
    spread = jnp.where((col >= row_lo) & (col < row_lo + rep), 1.0, 0.0).astype(BF16)
    for c in range(past // blk):
        sel4_ref[:, c * blk * rep:(c + 1) * blk * rep] = jnp.dot(
            sel_ref[:, c * blk:(c + 1) * blk].astype(BF16), spread, preferred_element_type=F32)


def sample_select(scores, sself, topk, rep):
    n, past = scores.shape
    kern = functools.partial(_sample_select_kernel, topk=topk, past=past, rep=rep)
    return pl.pallas_call(
        kern,
        out_shape=[jax.ShapeDtypeStruct((n, past * rep), F32), jax.ShapeDtypeStruct((n, LANES), F32)],
        scratch_shapes=[pltpu.VMEM((n, past), F32)],
        compiler_params=pltpu.CompilerParams(vmem_limit_bytes=VMEM_LIMIT),
        name="sample_select",
    )(scores, sself)


def _sample_attn_kernel(pt_ref, q_ref, kn_ref, vn_ref, sel_ref, selself_ref, bias_ref, rb0_ref, own_ref, *rest,
                        pg, n_heads, n_kv, hd):
    k_hbm, v_hbm, o_ref, kbuf_ref, vbuf_ref, sem, kcat_ref, vcat_ref, m_ref, l_ref, acc_ref = rest
    b = pl.program_id(0)
    p = pl.program_id(1)
    n_steps = pl.num_programs(1)
    scale = hd ** -0.5
    rows = k_hbm.shape[1]
    step = b * n_steps + p
    slot = step % 2

    def page_copies(bb, pp, sl):
        copies = []
        for i in range(pg):
            pid = pt_ref[bb, pp * pg + i]
            dst = pl.ds(i * rows, rows)
            copies.append(pltpu.make_async_copy(k_hbm.at[pid], kbuf_ref.at[sl, dst], sem.at[0, sl]))
            copies.append(pltpu.make_async_copy(v_hbm.at[pid], vbuf_ref.at[sl, dst], sem.at[1, sl]))
        return copies

    @pl.when(step == 0)
    def _():
        for c in page_copies(0, 0, 0):
            c.start()

    nxt = step + 1

    @pl.when(nxt < pl.num_programs(0) * n_steps)
    def _():
        for c in page_copies(nxt // n_steps, nxt % n_steps, 1 - slot):
            c.start()

    for c in page_copies(b, p, slot):
        c.wait()

    @pl.when(p == 0)
    def _():
        m_ref[...] = jnp.full(m_ref.shape, NEG, F32)
        l_ref[...] = jnp.zeros(l_ref.shape, F32)
        acc_ref[...] = jnp.zeros(acc_ref.shape, F32)

    for i in range(pg):
        kcat_ref[i * rows:(i + 1) * rows, :] = kbuf_ref[slot, pl.ds(i * rows, rows), :].astype(BF16)
        vcat_ref[i * rows:(i + 1) * rows, :] = vbuf_ref[slot, pl.ds(i * rows, rows), :].astype(BF16)
    qb = q_ref[...].astype(BF16)
    last = p == n_steps - 1
    bias = jnp.concatenate([bias_ref[0]] * (pg - 1) + [jnp.where(last, bias_ref[1], bias_ref[0])], axis=1)
    msk = (own_ref[...] > 0.0) & (sel_ref[...] > 0.0)
    lg = jnp.where(msk, _dot_nt(qb, kcat_ref[...]) * scale + bias, -jnp.inf)
    m_old = m_ref[...]
    m_new = jnp.maximum(m_old, jnp.max(lg, axis=-1, keepdims=True))
    alpha = jnp.exp(m_old - m_new)
    pr = jnp.exp(lg - m_new)
    l_new = alpha * l_ref[...] + jnp.sum(pr, axis=-1, keepdims=True)
    acc = alpha * acc_ref[...] + jnp.dot(pr.astype(BF16), vcat_ref[...], preferred_element_type=F32)
    m_ref[...] = m_new
    l_ref[...] = l_new
    acc_ref[...] = acc

    @pl.when(last)
    def _():
        kn = kn_ref[...].astype(BF16).astype(F32)
        vn = vn_ref[...].astype(BF16).astype(F32)
        ls = jnp.sum(qb.astype(F32) * kn, axis=-1, keepdims=True) * scale + rb0_ref[...]
        on = selself_ref[:, 0:1] > 0.0
        ls = jnp.where(on, ls, NEG)
        m_f = jnp.maximum(m_new, ls)
        a2 = jnp.exp(m_new - m_f)
        ps = jnp.where(on, jnp.exp(ls - m_f), 0.0)
        l_f = a2 * l_new + ps
        acc_f = a2 * acc + ps.astype(BF16).astype(F32) * vn
        o_ref[...] = acc_f / l_f


def sample_attention(page_table, q3, k_rep, v_rep, sel4, selself3, bias_s, rb0, cache_k, cache_v, pg, n_kv):
    n, n_pages = page_table.shape
    n_heads, hd = q3.shape[1:]
    rows = cache_k.shape[1]
    kern = functools.partial(_sample_attn_kernel, pg=pg, n_heads=n_heads, n_kv=n_kv, hd=hd)
    own = (np.arange(pg * rows)[None, :] % n_kv
           == np.arange(n_heads)[:, None] // (n_heads // n_kv)).astype(np.float32)
    hbm = pl.BlockSpec(memory_space=pl.ANY)
    per_seq = lambda b, p, pt: (b, 0, 0)
    grid_spec = pltpu.PrefetchScalarGridSpec(
        num_scalar_prefetch=1,
        grid=(n, n_pages // pg),
        in_specs=[pl.BlockSpec((None, n_heads, hd), per_seq),
                  pl.BlockSpec((None, n_heads, hd), per_seq),
                  pl.BlockSpec((None, n_heads, hd), per_seq),
                  pl.BlockSpec((None, None, 1, pg * rows), lambda b, p, pt: (b, p, 0, 0)),
                  pl.BlockSpec((None, 1, LANES), per_seq),
                  pl.BlockSpec((2, n_heads, rows), lambda b, p, pt: (0, 0, 0)),
                  pl.BlockSpec((n_heads, 1), lambda b, p, pt: (0, 0)),
                  pl.BlockSpec((n_heads, pg * rows), lambda b, p, pt: (0, 0)), hbm, hbm],
        out_specs=pl.BlockSpec((None, n_heads, hd), per_seq),
        scratch_shapes=[pltpu.VMEM((2, pg * rows, hd), F32), pltpu.VMEM((2, pg * rows, hd), F32),
                        pltpu.SemaphoreType.DMA((2, 2)),
                        pltpu.VMEM((pg * rows, hd), BF16), pltpu.VMEM((pg * rows, hd), BF16),
                        pltpu.VMEM((n_heads, 1), F32), pltpu.VMEM((n_heads, 1), F32),
                        pltpu.VMEM((n_heads, hd), F32)],
    )
    return pl.pallas_call(
        kern,
        grid_spec=grid_spec,
        out_shape=jax.ShapeDtypeStruct((n, n_heads, hd), F32),
        compiler_params=_cparams(2),
        name="sample_attn",
    )(page_table, q3, k_rep, v_rep, sel4, selself3, bias_s, rb0, jnp.asarray(own), cache_k, cache_v)


def _mix_kernel(ga_ref, gb_ref, co_ref, at_ref, o_ref):
    o_ref[...] = (_sigmoid(ga_ref[...]) * co_ref[...] + _sigmoid(gb_ref[...]) * at_ref[...]).astype(o_ref.dtype)


def gated_mix(gates, conv_out, attn):
    n, dm = conv_out.shape
    return pl.pallas_call(
        _mix_kernel,
        grid=(1,),
        in_specs=[pl.BlockSpec((n, dm), lambda i: (0, 0)), pl.BlockSpec((n, dm), lambda i: (0, 1)),
                  pl.BlockSpec((n, dm), lambda i: (0, 0)), pl.BlockSpec((n, dm), lambda i: (0, 0))],
        out_specs=pl.BlockSpec((n, dm), lambda i: (0, 0)),
        out_shape=jax.ShapeDtypeStruct((n, dm), BF16),
    )(gates, gates, conv_out, attn)


def _ffn_up_kernel(x_ref, xh_ref, wg_ref, wv_ref, dg_ref, dv_ref, bg_ref, bv_ref, pg_ref, pv_ref,
                   o_ref, sg_ref, sv_ref, eg_ref, ev_ref, *, tm, rs, width, tiles_per_seq):
    hist = width - 1
    pad = xh_ref.shape[0]
    first = pl.program_id(0) % tiles_per_seq == 0
    branches = ((wg_ref, dg_ref, bg_ref, pg_ref, eg_ref, sg_ref),
                (wv_ref, dv_ref, bv_ref, pv_ref, ev_ref, sv_ref))
    for w_ref, _, _, prev_ref, e_ref, _ in branches:
        e_ref[pl.ds(0, pad), :] = jnp.dot(xh_ref[...], w_ref[...], preferred_element_type=F32)

        @pl.when(first)
        def _():
            e_ref[pl.ds(pad - hist, hist), :] = prev_ref[...]

    def project(r):
        for w_ref, _, _, _, e_ref, _ in branches:
            e_ref[pl.ds(pad + r * rs, rs), :] = jnp.dot(x_ref[pl.ds(r * rs, rs), :], w_ref[...],
                                                        preferred_element_type=F32)

    def activate(r):
        outs = []
        for _, d_ref, b_ref, _, e_ref, _ in branches:
            acc = b_ref[...] + d_ref[pl.ds(hist, 1), :] * e_ref[pl.ds(pad + r * rs, rs), :]
            for j in range(hist):
                acc = acc + d_ref[pl.ds(j, 1), :] * e_ref[pl.ds(pad + r * rs - hist + j, rs), :]
            outs.append(acc)
        g, v = outs
        o_ref[pl.ds(r * rs, rs), :] = (g * _sigmoid(g) * v).astype(o_ref.dtype)

    n_sub = tm // rs
    project(0)
    for r in range(1, n_sub):
        project(r)
        activate(r - 1)
    activate(n_sub - 1)
    for _, _, _, _, e_ref, s_ref in branches:
        s_ref[...] = e_ref[pl.ds(pad + tm - hist, hist), :]


def ffn_up_act(xn, w_up, prev, dw, bdw, n_seq, seq, tm, tn):
    m, k = xn.shape
    f2 = w_up.shape[1]
    f = f2 // 2
    width = dw.shape[0]
    ncb = f // tn
    pad = 16
    tps = seq // tm
    assert seq % tm == 0 and tm % pad == 0 and f % tn == 0 and width - 1 <= pad
    rs = min(256, tm // 2)
    assert tm % rs == 0 and rs % 8 == 0 and k % 256 == 0
    kern = functools.partial(_ffn_up_kernel, tm=tm, rs=rs, width=width, tiles_per_seq=tps)
    hb = tm // pad
    st_shape = jax.ShapeDtypeStruct((m // tm, width - 1, f), F32)
    act, sg, sv = pl.pallas_call(
        kern,
        grid=(m // tm, ncb),
        in_specs=[pl.BlockSpec((tm, k), lambda i, c: (i, 0)),
                  pl.BlockSpec((pad, k), lambda i, c: (jnp.maximum(i * hb - 1, 0), 0)),
                  pl.BlockSpec((k, tn), lambda i, c: (0, c)),
                  pl.BlockSpec((k, tn), lambda i, c: (0, c + ncb)),
                  pl.BlockSpec((width, tn), lambda i, c: (0, c)),
                  pl.BlockSpec((width, tn), lambda i, c: (0, c + ncb)),
                  pl.BlockSpec((1, tn), lambda i, c: (0, c)),
                  pl.BlockSpec((1, tn), lambda i, c: (0, c + ncb)),
                  pl.BlockSpec((None, width - 1, tn), lambda i, c: (i // tps, 0, c)),
                  pl.BlockSpec((None, width - 1, tn), lambda i, c: (i // tps, 0, c + ncb))],
        out_specs=[pl.BlockSpec((tm, tn), lambda i, c: (i, c)),
                   pl.BlockSpec((None, width - 1, tn), lambda i, c: (i, 0, c)),
                   pl.BlockSpec((None, width - 1, tn), lambda i, c: (i, 0, c))],
        out_shape=[jax.ShapeDtypeStruct((m, f), BF16), st_shape, st_shape],
        scratch_shapes=[pltpu.VMEM((pad + tm, tn), F32), pltpu.VMEM((pad + tm, tn), F32)],
        compiler_params=_cparams(2),
        name="ffn_up_act",
    )(xn, xn, w_up, w_up, dw, dw, bdw.reshape(1, f2), bdw.reshape(1, f2), prev, prev)
    state = jnp.concatenate([sg, sv], axis=-1).reshape(n_seq, tps, width - 1, f2)[:, tps - 1]
    return act, state


def _ffn_act_step_kernel(ug_ref, uv_ref, pg_ref, pv_ref, wg_ref, wv_ref, bg_ref, bv_ref, o_ref, *, width):
    def conv(u_ref, prev_ref, w_ref, b_ref):
        acc = b_ref[...] + w_ref[pl.ds(width - 1, 1), :] * u_ref[...]
        for j in range(width - 1):
            acc = acc + w_ref[pl.ds(j, 1), :] * prev_ref[j]
        return acc
    g = conv(ug_ref, pg_ref, wg_ref, bg_ref)
    v = conv(uv_ref, pv_ref, wv_ref, bv_ref)
    o_ref[...] = (g * _sigmoid(g) * v).astype(o_ref.dtype)


def ffn_act_step(u, prev_t, dw, bdw, tc):
    n, f2 = u.shape
    f = f2 // 2
    width = dw.shape[0]
    ncb = f // tc
    kern = functools.partial(_ffn_act_step_kernel, width=width)
    return pl.pallas_call(
        kern,
        grid=(ncb,),
        in_specs=[pl.BlockSpec((n, tc), lambda c: (0, c)),
                  pl.BlockSpec((n, tc), lambda c: (0, c + ncb)),
                  pl.BlockSpec((width - 1, n, tc), lambda c: (0, 0, c)),
                  pl.BlockSpec((width - 1, n, tc), lambda c: (0, 0, c + ncb)),
                  pl.BlockSpec((width, tc), lambda c: (0, c)),
                  pl.BlockSpec((width, tc), lambda c: (0, c + ncb)),
                  pl.BlockSpec((1, tc), lambda c: (0, c)),
                  pl.BlockSpec((1, tc), lambda c: (0, c + ncb))],
        out_specs=pl.BlockSpec((n, tc), lambda c: (0, c)),
        out_shape=jax.ShapeDtypeStruct((n, f), BF16),
        compiler_params=_cparams(1),
    )(u, u, prev_t, prev_t, dw, dw, bdw.reshape(1, f2), bdw.reshape(1, f2))


def _row_tile(m, cap):
    tm = cap
    while m % tm:
        tm //= 2
    return tm if tm >= 16 else m


def _in_projection(xn, w_main, w_kw, w_g, sizes, tm):
    glu_w, q_w, k_w, v_w, qi_w, ki_w, wi_w, ga_w, gb_w = sizes
    offs = np.concatenate([[0], np.cumsum(sizes)])
    tn = 512
    t = dict(w_transposed=True)
    glu = matmul_w(xn, w_main, int(offs[0]), glu_w, tm, tn, name="mm_glu", **t)
    q = matmul_w(xn, w_main, int(offs[1]), q_w, tm, tn, name="mm_q", out_dtype=BF16, **t)
    k = matmul_w(xn, w_main, int(offs[2]), k_w, tm, min(tn, k_w), name="mm_k", **t)
    v = matmul_w(xn, w_main, int(offs[3]), v_w, tm, min(tn, v_w), name="mm_v", **t)
    qi = matmul_w(xn, w_main, int(offs[4]), qi_w, tm, min(tn, qi_w), name="mm_qi", out_dtype=BF16, **t)
    kw = matmul_w(xn, w_kw, 0, LANES, tm, LANES, name="mm_kw", **t)
    gates = matmul_w(xn, w_g, 0, ga_w + gb_w, tm, tn, name="mm_gates", **t)
    return glu, q, k, v, qi, kw, gates


def kernel(x_prompt, x_sample, cache_k, cache_v, cache_kidx, state_conv, state_ffn, page_table, rel_bias,
           norm_attn, w_in, dw_conv, b_dw_conv, ln_conv_g, ln_conv_b, w_conv_out, w_o, norm_ffn, w_up, dw_ffn,
           b_dw_ffn, w_down, norm_final):
    bsz, seq, dm = x_prompt.shape
    nd, dec_seq, _ = x_sample.shape
    depth, n_pool, page, n_kv, hd = cache_k.shape
    idx_dim = cache_kidx.shape[-1]
    n_pages = page_table.shape[1]
    past = n_pages * page
    width, dconv = dw_conv.shape[1:]
    fwidth = dw_ffn.shape[1]
    f = w_down.shape[1]
    n_heads = w_o.shape[1] // hd
    d_attn = n_heads * hd
    d_kv = n_kv * hd
    n_in = w_in.shape[2]
    idx_heads = (n_in - 2 * dconv - d_attn - 2 * d_kv - idx_dim - 2 * dm) // (idx_dim + 1)
    sizes = (2 * dconv, d_attn, d_kv, d_kv, idx_heads * idx_dim, idx_dim, idx_heads, dm, dm)
    assert sum(sizes) == n_in and depth == 1 and dec_seq == 1 and page == LANES and d_attn == dm

    mp = bsz * seq
    xp = x_prompt.reshape(mp, dm)
    xs = x_sample.reshape(nd, dm)
    bias3, bias_s = bias_tables(rel_bias, page, n_kv)
    tmp = _row_tile(mp, 2048)
    drop = lambda a: a.reshape(a.shape[1:])
    (norm_attn, w_in, dw_conv, b_dw_conv, ln_conv_g, ln_conv_b, w_conv_out, w_o, norm_ffn, w_up, dw_ffn,
     b_dw_ffn, w_down, state_conv, state_ffn) = map(drop, (
         norm_attn, w_in, dw_conv, b_dw_conv, ln_conv_g, ln_conv_b, w_conv_out, w_o, norm_ffn, w_up, dw_ffn,
         b_dw_ffn, w_down, state_conv, state_ffn))
    kidx_pool = jnp.swapaxes(cache_kidx.reshape(n_pool, page, idx_dim), 1, 2)
    k_pool = cache_k.reshape(n_pool, page * n_kv, hd)
    v_pool = cache_v.reshape(n_pool, page * n_kv, hd)

    n_aligned = sum(sizes[:5])
    n_small = idx_dim + idx_heads
    w_in_t = jnp.swapaxes(w_in, 0, 1)
    w_main = cast_rows_bf16(w_in_t, 0, n_aligned)
    w_kw = cast_rows_bf16(w_in_t, n_aligned, LANES, n_small)
    w_g = cast_rows_bf16(w_in_t, n_aligned + n_small, 2 * dm)
    w_o, w_up, w_down = cast_bf16(w_o), cast_bf16(w_up), cast_bf16(w_down)

    xn = rmsnorm_rows(xp, norm_attn, BF16, 512)
    glu, q, k, v, qi, kw, gates = _in_projection(xn, w_main, w_kw, w_g, sizes, tmp)
    conv0 = jnp.zeros((bsz, width - 1, dconv), F32)
    conv_out, conv_state_p = conv_branch_prompt(glu, conv0, dw_conv, b_dw_conv, ln_conv_g, ln_conv_b,
                                                w_conv_out, bsz, seq, 256)
    mixed = attn_prompt(qi, kw, q, k, v, conv_out, gates, bias3, bsz, seq, n_kv, idx_dim, idx_heads)
    x2 = matmul_w(mixed, w_o, 0, dm, tmp, 512, res=xp, name="mm_o")
    xn2 = rmsnorm_rows(x2, norm_ffn, BF16, 512)
    ffn0 = jnp.zeros((bsz, fwidth - 1, 2 * f), F32)
    act, ffn_state_p = ffn_up_act(xn2, w_up, ffn0, dw_ffn, b_dw_ffn, bsz, seq, _row_tile(seq, 2048), 512)
    x3 = matmul_w(act, w_down, 0, dm, _row_tile(mp, 1024), 256, res=x2, name="mm_down")
    y_prompt = rmsnorm_rows(x3, norm_final, F32, 512).reshape(bsz, seq, dm)

    xns = rmsnorm_rows(xs, norm_attn, BF16, nd)
    glu_s, q_s, k_s, v_s, qi_s, kw_s, gates_s = _in_projection(xns, w_main, w_kw, w_g, sizes, nd)
    ki_s = kw_s[:, :idx_dim]
    wi_s = kw_s[:, idx_dim:idx_dim + idx_heads]
    sc_prev_t = jnp.swapaxes(state_conv, 0, 1)
    conv_out_s, u_conv_s = conv_branch_step(glu_s, sc_prev_t, dw_conv, b_dw_conv, ln_conv_g, ln_conv_b,
                                            w_conv_out)
    conv_state_s = jnp.concatenate([state_conv[:, 1:], u_conv_s[:, None, :]], axis=1)

    pg = 16 if n_pages % 16 == 0 else 8
    scores3, sself3 = sample_scores(page_table, qi_s.reshape(nd, idx_heads, idx_dim),
                                    wi_s.reshape(nd, idx_heads, 1), ki_s.reshape(nd, 1, idx_dim), kidx_pool, pg)
    topk_s = min(TOPK_MAX, (past + dec_seq) // 4)
    sel4, selself = sample_select(scores3.reshape(nd, past), sself3.reshape(nd, LANES), topk_s, n_kv)
    group = n_heads // n_kv
    k_rep = jnp.repeat(k_s.reshape(nd, n_kv, hd), group, axis=1)
    v_rep = jnp.repeat(v_s.reshape(nd, n_kv, hd), group, axis=1)
    attn_s = sample_attention(page_table, q_s.reshape(nd, n_heads, hd), k_rep, v_rep,
                              sel4.reshape(nd, n_pages // pg, 1, pg * page * n_kv), selself.reshape(nd, 1, LANES),
                              bias_s,
                              rel_bias[0].reshape(n_heads, 1), k_pool, v_pool, pg, n_kv)
    mixed_s = gated_mix(gates_s, conv_out_s, attn_s.reshape(nd, dm))
    x2s = matmul_w(mixed_s, w_o, 0, dm, nd, 512, res=xs)
    xn2s = rmsnorm_rows(x2s, norm_ffn, BF16, nd)
    u_s = matmul_w(xn2s, w_up, 0, 2 * f, nd, 512)
    sf_prev_t = jnp.swapaxes(state_ffn, 0, 1)
    act_s = ffn_act_step(u_s, sf_prev_t, dw_ffn, b_dw_ffn, 512)
    x3s = matmul_w(act_s, w_down, 0, dm, nd, 256, res=x2s)
    y_sample = rmsnorm_rows(x3s, norm_final, F32, nd).reshape(nd, dec_seq, dm)
    ffn_state_s = jnp.concatenate([state_ffn[:, 1:], u_s[:, None, :]], axis=1)

    return (y_prompt, y_sample,
            k.reshape(1, bsz, seq, n_kv, hd), v.reshape(1, bsz, seq, n_kv, hd),
            kw.reshape(bsz, seq, LANES)[None, :, :, :idx_dim],
            conv_state_p[None], ffn_state_p[None],
            k_s.reshape(1, nd, dec_seq, n_kv, hd), v_s.reshape(1, nd, dec_seq, n_kv, hd),
            ki_s.reshape(1, nd, dec_seq, idx_dim),
            conv_state_s[None], ffn_state_s[None])
```

```python
import functools
import math

import numpy as np
import jax
import jax.numpy as jnp
from jax import lax
from jax.experimental import pallas as pl
from jax.experimental.pallas import tpu as pltpu

F32 = jnp.float32
BF16 = jnp.bfloat16

EPS = 1e-6
TOPK_MAX = 256
N_BUCKETS = 32
MAX_DISTANCE = 128
QB = 128
CK = 256
LANES = 128
NEG = -1e30
LOG2E = math.log2(math.e)
VMEM_LIMIT = 56 * 1024 * 1024


def _cparams(n_axes, vmem=VMEM_LIMIT):
    return pltpu.CompilerParams(dimension_semantics=("arbitrary",) * n_axes, vmem_limit_bytes=vmem)


def _dot_nt(a, b):
    return lax.dot_general(a, b, (((1,), (1,)), ((), ())), preferred_element_type=F32)


def _sigmoid(x):
    return 1.0 / (1.0 + jnp.exp(-x))


def _fold_rows(x, op):
    while x.shape[0] > 8:
        half = x.shape[0] // 2
        x = op(x[:half], x[half:])
    return x


def _rel_bucket_np(dist):
    n = np.maximum(dist, 0)
    max_exact = N_BUCKETS // 2
    nf = np.maximum(n, 1).astype(np.float32)
    large = max_exact + (np.log(nf / np.float32(max_exact)) / np.float32(math.log(MAX_DISTANCE / max_exact))
                         * np.float32(N_BUCKETS - max_exact)).astype(np.int32)
    large = np.minimum(large, N_BUCKETS - 1)
    return np.where(n < max_exact, n, large).astype(np.int32)


def _rms_kernel(x_ref, g_ref, o_ref):
    x = x_ref[...]
    y = x * lax.rsqrt(jnp.mean(x * x, axis=-1, keepdims=True) + EPS) * g_ref[...]
    o_ref[...] = y.astype(o_ref.dtype)


def rmsnorm_rows(x, g, out_dtype, tm):
    m, d = x.shape
    return pl.pallas_call(
        _rms_kernel,
        grid=(m // tm,),
        in_specs=[pl.BlockSpec((tm, d), lambda i: (i, 0)), pl.BlockSpec((1, d), lambda i: (0, 0))],
        out_specs=pl.BlockSpec((tm, d), lambda i: (i, 0)),
        out_shape=jax.ShapeDtypeStruct((m, d), out_dtype),
        compiler_params=_cparams(1),
        name="rmsnorm",
    )(x, g.reshape(1, d))


def _cast_kernel(w_ref, o_ref):
    o_ref[...] = w_ref[...].astype(o_ref.dtype)


def cast_bf16(w, ncols=None):
    k, n = w.shape
    ncols = n if ncols is None else ncols
    tk = 512 if k % 512 == 0 else k
    tn = 1024 if ncols % 1024 == 0 else (512 if ncols % 512 == 0 else ncols)
    return pl.pallas_call(
        _cast_kernel,
        grid=(k // tk, ncols // tn),
        in_specs=[pl.BlockSpec((tk, tn), lambda i, j: (i, j))],
        out_specs=pl.BlockSpec((tk, tn), lambda i, j: (i, j)),
        out_shape=jax.ShapeDtypeStruct((k, ncols), BF16),
        compiler_params=_cparams(2),
        name="cast_bf16",
    )(w)


def _cast_rows_kernel(w_ref, o_ref, *, valid):
    w = w_ref[...]
    if valid < w.shape[0]:
        w = jnp.where(lax.broadcasted_iota(jnp.int32, w.shape, 0) < valid, w, 0.0)
    o_ref[...] = w.astype(o_ref.dtype)


def cast_rows_bf16(wt, row0, nrows, nvalid=None):
    n, k = wt.shape
    tr = 512 if nrows % 512 == 0 else nrows
    nvalid = nrows if nvalid is None else nvalid
    assert row0 % 8 == 0 and (nvalid == nrows or tr == nrows) and row0 + nrows <= n
    return pl.pallas_call(
        functools.partial(_cast_rows_kernel, valid=nvalid),
        grid=(nrows // tr,),
        in_specs=[pl.BlockSpec((pl.Element(tr), pl.Element(k)), lambda i: (pl.multiple_of(row0 + i * tr, 8), 0))],
        out_specs=pl.BlockSpec((tr, k), lambda i: (i, 0)),
        out_shape=jax.ShapeDtypeStruct((nrows, k), BF16),
        compiler_params=_cparams(1),
        name="cast_rows_bf16",
    )(wt)


def _mm_kernel(a_ref, w_ref, o_ref):
    o_ref[...] = jnp.dot(a_ref[...], w_ref[...], preferred_element_type=F32).astype(o_ref.dtype)


def _mm_nt_kernel(a_ref, wt_ref, o_ref):
    o_ref[...] = _dot_nt(a_ref[...], wt_ref[...]).astype(o_ref.dtype)


def matmul_w(a, w, col0, ncols, tm, tn, name="matmul", out_dtype=F32, w_transposed=False):
    m, k = a.shape
    assert col0 % tn == 0 and ncols % tn == 0 and m % tm == 0
    cb = col0 // tn
    if w_transposed:
        w_spec = pl.BlockSpec((tn, k), lambda i, j: (j + cb, 0))
    else:
        w_spec = pl.BlockSpec((k, tn), lambda i, j: (0, j + cb))
    return pl.pallas_call(
        _mm_nt_kernel if w_transposed else _mm_kernel,
        grid=(m // tm, ncols // tn),
        in_specs=[pl.BlockSpec((tm, k), lambda i, j: (i, 0)), w_spec],
        out_specs=pl.BlockSpec((tm, tn), lambda i, j: (i, j)),
        out_shape=jax.ShapeDtypeStruct((m, ncols), out_dtype),
        compiler_params=_cparams(2),
        name=name,
    )(a, w)


def _rms(x, g):
    return x * lax.rsqrt(jnp.mean(x * x, axis=-1, keepdims=True) + EPS) * g


def _mm_res_norm_kernel(a_ref, w_ref, r_ref, g_ref, o_ref, on_ref):
    x = r_ref[...] + jnp.dot(a_ref[...], w_ref[...], preferred_element_type=F32)
    o_ref[...] = x
    on_ref[...] = _rms(x, g_ref[...]).astype(on_ref.dtype)


def matmul_res_norm(a, w, res, g, tm):
    m, k = a.shape
    n = w.shape[1]
    return pl.pallas_call(
        _mm_res_norm_kernel,
        grid=(m // tm,),
        in_specs=[pl.BlockSpec((tm, k), lambda i: (i, 0)), pl.BlockSpec((k, n), lambda i: (0, 0)),
                  pl.BlockSpec((tm, n), lambda i: (i, 0)), pl.BlockSpec((1, n), lambda i: (0, 0))],
        out_specs=[pl.BlockSpec((tm, n), lambda i: (i, 0)), pl.BlockSpec((tm, n), lambda i: (i, 0))],
        out_shape=[jax.ShapeDtypeStruct((m, n), F32), jax.ShapeDtypeStruct((m, n), BF16)],
        compiler_params=_cparams(1),
        name="mm_o_norm",
    )(a, w, res, g.reshape(1, n))


def _mm_ksplit_norm_kernel(a_ref, w_ref, r_ref, g_ref, o_ref, acc_ref):
    kk = pl.program_id(1)

    @pl.when(kk == 0)
    def _():
        acc_ref[...] = r_ref[...]
    acc_ref[...] += jnp.dot(a_ref[...], w_ref[...], preferred_element_type=F32)

    @pl.when(kk == pl.num_programs(1) - 1)
    def _():
        o_ref[...] = _rms(acc_ref[...], g_ref[...])


def matmul_res_norm_out(a, w, res, g, tm, tk):
    m, k = a.shape
    n = w.shape[1]
    assert k % tk == 0 and m % tm == 0
    return pl.pallas_call(
        _mm_ksplit_norm_kernel,
        grid=(m // tm, k // tk),
        in_specs=[pl.BlockSpec((tm, tk), lambda i, kk: (i, kk)), pl.BlockSpec((tk, n), lambda i, kk: (kk, 0)),
                  pl.BlockSpec((tm, n), lambda i, kk: (i, 0)), pl.BlockSpec((1, n), lambda i, kk: (0, 0))],
        out_specs=pl.BlockSpec((tm, n), lambda i, kk: (i, 0)),
        out_shape=jax.ShapeDtypeStruct((m, n), F32),
        scratch_shapes=[pltpu.VMEM((tm, n), F32)],
        compiler_params=_cparams(2),
        name="mm_down_norm",
    )(a, w, res, g.reshape(1, n))


def _bias_kernel(rb_ref, bk3_ref, bks_ref, o3_ref, os_ref, *, n_heads):
    def head(h, carry):
        for t in range(3):
            bk = bk3_ref[t]
            acc = jnp.zeros(bk.shape, F32)
            for b in range(N_BUCKETS):
                acc = jnp.where(bk == b, rb_ref[b, h], acc)
            o3_ref[t, h] = acc * LOG2E
        for t in range(2):
            bk = bks_ref[t]
            acc = jnp.zeros(bk.shape, F32)
            for b in range(N_BUCKETS):
                acc = jnp.where(bk == b, rb_ref[b, h], acc)
            os_ref[t, pl.ds(h, 1), :] = acc
        return carry
    lax.fori_loop(0, n_heads, head, 0)


def bias_tables(rel_bias, page, rep):
    n_heads = rel_bias.shape[1]
    cols = page * rep
    i = np.arange(QB)[None, :]
    k = np.arange(QB)[:, None]
    bk3 = np.stack([_rel_bucket_np(i - k + 2 * QB), _rel_bucket_np(i - k + QB), _rel_bucket_np(i - k)])
    assert (_rel_bucket_np(np.arange(QB + 1, 1 << 20)) == N_BUCKETS - 1).all()
    assert (bk3[0] == N_BUCKETS - 1).all()
    assert page >= QB
    bks = np.stack([np.full((1, cols), N_BUCKETS - 1, np.int32),
                    _rel_bucket_np(page - np.arange(cols) // rep)[None, :]])
    return pl.pallas_call(
        functools.partial(_bias_kernel, n_heads=n_heads),
        in_specs=[pl.BlockSpec(memory_space=pltpu.SMEM),
                  pl.BlockSpec(memory_space=pltpu.VMEM), pl.BlockSpec(memory_space=pltpu.VMEM)],
        out_specs=[pl.BlockSpec(memory_space=pltpu.VMEM), pl.BlockSpec(memory_space=pltpu.VMEM)],
        out_shape=[jax.ShapeDtypeStruct((3, n_heads, QB, QB), F32),
                   jax.ShapeDtypeStruct((2, n_heads, cols), F32)],
    )(rel_bias, jnp.asarray(bk3), jnp.asarray(bks))


def _conv_kernel(glu_ref, prev_ref, dw_ref, bdw_ref, lng_ref, lnb_ref, wout_ref, o_ref, st_ref,
                 ext_ref, h_ref, wbf_ref, sh_ref, *, tt, width, dconv):
    b = pl.program_id(0)
    t = pl.program_id(1)
    pad = 32
    hist = width - 1

    @pl.when((b == 0) & (t == 0))
    def _():
        wbf_ref[...] = wout_ref[...].astype(BF16)

    @pl.when(t == 0)
    def _():
        ext_ref[pl.ds(pad - hist, hist), :] = prev_ref[...]

    @pl.when(t > 0)
    def _():
        ext_ref[pl.ds(0, pad), :] = ext_ref[pl.ds(tt, pad), :]

    glu = glu_ref[...]
    u = glu[:, :dconv] * _sigmoid(glu[:, dconv:])
    ext_ref[pl.ds(pad, tt), :] = u
    st_ref[...] = ext_ref[pl.ds(pad + tt - hist, hist), :]

    span = tt + pad - 8
    for s in range(1, 8):
        sh_ref[s - 1] = ext_ref[pl.ds(s, span), :]

    for c in range(dconv // LANES):
        cs = slice(c * LANES, (c + 1) * LANES)
        acc = jnp.zeros((tt, LANES), F32) + bdw_ref[:, cs]
        for j in range(width):
            a, s = divmod(pad - hist + j, 8)
            rows = ext_ref[pl.ds(8 * a, tt), cs] if s == 0 else sh_ref[s - 1, pl.ds(8 * a, tt), cs]
            acc = acc + dw_ref[pl.ds(j, 1), cs] * rows
        h_ref[:, cs] = acc

    h = h_ref[...]
    mu = jnp.mean(h, axis=-1, keepdims=True)
    var = jnp.mean(jnp.square(h - mu), axis=-1, keepdims=True)
    y = (h - mu) * lax.rsqrt(var + EPS) * lng_ref[...] + lnb_ref[...]
    y = y * _sigmoid(y)
    o_ref[...] = jnp.dot(y.astype(BF16), wbf_ref[...], preferred_element_type=F32)


def conv_branch_prompt(glu_pre, prev, dw, bdw, lng, lnb, wout, n_seq, seq, tt):
    width, dconv = dw.shape
    dm = wout.shape[1]
    nt = seq // tt
    kern = functools.partial(_conv_kernel, tt=tt, width=width, dconv=dconv)
    return pl.pallas_call(
        kern,
        grid=(n_seq, nt),
        in_specs=[pl.BlockSpec((tt, 2 * dconv), lambda b, t: (b * nt + t, 0)),
                  pl.BlockSpec((None, width - 1, dconv), lambda b, t: (b, 0, 0)),
                  pl.BlockSpec((width, dconv), lambda b, t: (0, 0)),
                  pl.BlockSpec((1, dconv), lambda b, t: (0, 0)),
                  pl.BlockSpec((1, dconv), lambda b, t: (0, 0)),
                  pl.BlockSpec((1, dconv), lambda b, t: (0, 0)),
                  pl.BlockSpec((dconv, dm), lambda b, t: (0, 0))],
        out_specs=[pl.BlockSpec((tt, dm), lambda b, t: (b * nt + t, 0)),
                   pl.BlockSpec((None, width - 1, dconv), lambda b, t: (b, 0, 0))],
        out_shape=[jax.ShapeDtypeStruct((n_seq * seq, dm), F32),
                   jax.ShapeDtypeStruct((n_seq, width - 1, dconv), F32)],
        scratch_shapes=[pltpu.VMEM((32 + tt, dconv), F32), pltpu.VMEM((tt, dconv), F32),
                        pltpu.VMEM((dconv, dm), BF16), pltpu.VMEM((7, 24 + tt, dconv), F32)],
        compiler_params=_cparams(2),
        name="conv_branch",
    )(glu_pre, prev, dw, bdw.reshape(1, dconv), lng.reshape(1, dconv), lnb.reshape(1, dconv), wout)


def _conv_step_kernel(glu_ref, prev_ref, dw_ref, bdw_ref, lng_ref, lnb_ref, wout_ref, o_ref, u_ref,
                      *, width, dconv):
    glu = glu_ref[...]
    u = glu[:, :dconv] * _sigmoid(glu[:, dconv:])
    u_ref[...] = u
    h = bdw_ref[...] + dw_ref[pl.ds(width - 1, 1), :] * u
    for j in range(width - 1):
        h = h + dw_ref[pl.ds(j, 1), :] * prev_ref[j]
    mu = jnp.mean(h, axis=-1, keepdims=True)
    var = jnp.mean(jnp.square(h - mu), axis=-1, keepdims=True)
    y = (h - mu) * lax.rsqrt(var + EPS) * lng_ref[...] + lnb_ref[...]
    y = y * _sigmoid(y)
    o_ref[...] = jnp.dot(y.astype(BF16), wout_ref[...].astype(BF16), preferred_element_type=F32)


def conv_branch_step(glu_pre, prev_t, dw, bdw, lng, lnb, wout):
    width, dconv = dw.shape
    n = glu_pre.shape[0]
    dm = wout.shape[1]
    kern = functools.partial(_conv_step_kernel, width=width, dconv=dconv)
    return pl.pallas_call(
        kern,
        out_shape=[jax.ShapeDtypeStruct((n, dm), F32), jax.ShapeDtypeStruct((n, dconv), F32)],
        compiler_params=pltpu.CompilerParams(vmem_limit_bytes=VMEM_LIMIT),
    )(glu_pre, prev_t, dw, bdw.reshape(1, dconv), lng.reshape(1, dconv), lnb.reshape(1, dconv), wout)


def _select_threshold(count_gt, row_min, row_max, n_adm, topk, any_fn):
    kf = jnp.float32(topk)
    full = n_adm <= kf
    lo0 = row_min - (1.0 + jnp.abs(row_min))
    hi0 = row_max
    flo0 = jnp.where(full, kf, n_adm)
    fhi0 = jnp.zeros_like(lo0)

    def active_rows(lo, hi, flo):
        mid = 0.5 * lo + 0.5 * hi
        return (flo != kf) & (lo < mid) & (mid < hi)

    def cond(st):
        lo, hi, flo, fhi = st
        return any_fn(active_rows(lo, hi, flo))

    def step(st):
        lo, hi, flo, fhi = st
        act = active_rows(lo, hi, flo)
        mid = 0.5 * lo + 0.5 * hi
        c = count_gt(mid)
        up = act & (c >= kf)
        dn = act & (c < kf)
        return (jnp.where(up, mid, lo), jnp.where(dn, mid, hi),
                jnp.where(up, c, flo), jnp.where(dn, c, fhi))

    def body(st):
        for _ in range(4):
            st = step(st)
        return st

    lo, hi, flo, fhi = lax.while_loop(cond, body, (lo0, hi0, flo0, fhi0))
    lo = jnp.where(full, -jnp.inf, lo)
    return lo, hi, flo, fhi


def _attn_prompt_kernel(qi_ref, wi_ref, kw_ref, q_ref, k_ref, v_ref, co_ref, ga_ref, gb_ref, bias_ref,
                        o_ref, kd_ref, kb_ref, vt_ref, wit_ref, qib_ref, sc_ref, sel_ref, qs_ref,
                        m_ref, l_ref, acc_ref, st_ref,
                        *, seq, n_heads, n_kv, idx_heads, idx_dim, topk, hd):
    j = pl.program_id(1)
    group = n_heads // n_kv
    nck = (j * QB + QB + CK - 1) // CK
    kf = jnp.float32(topk)

    @pl.when(j == 0)
    def _():
        kd_ref[...] = kw_ref[:, :idx_dim].astype(BF16)
        for g in range(n_kv):
            kb_ref[g] = k_ref[:, g * hd:(g + 1) * hd].astype(BF16)
            for c in range(seq // CK):
                vt_ref[g, c] = v_ref[c * CK:(c + 1) * CK, g * hd:(g + 1) * hd].T.astype(BF16)

    wit_ref[...] = (wi_ref[...] * (idx_heads ** -0.5)).T
    qi = qi_ref[...].astype(F32) * (idx_dim ** -0.5)
    for h in range(idx_heads):
        qib_ref[h] = qi[:, h * idx_dim:(h + 1) * idx_dim].astype(BF16)
    qpos = j * QB + lax.broadcasted_iota(jnp.int32, (CK, QB), 1)
    krow = lax.broadcasted_iota(jnp.int32, (CK, QB), 0)
    per_dot = 2 * LANES // QB

    def score_chunk(c, carry):
        mn, mx = carry
        k0 = pl.multiple_of(c * CK, CK)
        kc = kd_ref[pl.ds(k0, CK), :]
        acc = jnp.zeros((CK, QB), F32)
        for p in range(idx_heads // per_dot):
            rhs = qib_ref[pl.ds(p * per_dot, per_dot)].reshape(per_dot * QB, idx_dim)
            s = _dot_nt(kc, rhs)
            for r in range(per_dot):
                acc = acc + (wit_ref[pl.ds(idx_dim + p * per_dot + r, 1), :]
                             * jnp.maximum(s[:, r * QB:(r + 1) * QB], 0.0))
        adm = (krow + c * CK) <= qpos
        sc_ref[c] = jnp.where(adm, acc, -jnp.inf)
        mn = jnp.minimum(mn, _fold_rows(jnp.where(adm, acc, jnp.inf), jnp.minimum))
        mx = jnp.maximum(mx, _fold_rows(jnp.where(adm, acc, -jnp.inf), jnp.maximum))
        return mn, mx

    mn8, mx8 = lax.fori_loop(0, nck, score_chunk,
                             (jnp.full((8, QB), jnp.inf, F32), jnp.full((8, QB), -jnp.inf, F32)))
    row_min = jnp.min(mn8, axis=0, keepdims=True)
    row_max = jnp.max(mx8, axis=0, keepdims=True)
    n_adm = (j * QB + 1 + lax.broadcasted_iota(jnp.int32, (1, QB), 1)).astype(F32)

    def count_gt(t):
        def cbody(c, acc):
            return acc + _fold_rows(jnp.where(sc_ref[c] > t, 1.0, 0.0), jnp.add)
        part = lax.fori_loop(0, nck, cbody, jnp.zeros((8, QB), F32))
        return jnp.sum(part, axis=0, keepdims=True)

    def any_fn(mask):
        return jnp.max(jnp.where(mask, 1.0, 0.0)) > 0.0

    lo, hi, flo, fhi = _select_threshold(count_gt, row_min, row_max, n_adm, topk, any_fn)
    tie = flo != kf

    def sel_chunk(c, carry):
        sel_ref[c] = jnp.where(sc_ref[c] > lo, 1.0, 0.0)
        return carry
    lax.fori_loop(0, nck, sel_chunk, 0)

    @pl.when(any_fn(tie))
    def _():
        need = kf - fhi
        lower = (lax.broadcasted_iota(jnp.int32, (CK, CK), 1)
                 < lax.broadcasted_iota(jnp.int32, (CK, CK), 0)).astype(BF16)

        def tie_chunk(c, before):
            s = sc_ref[c]
            eq = s == hi
            eqf = jnp.where(eq, 1.0, 0.0)
            rank = before + jnp.dot(lower, eqf.astype(BF16), preferred_element_type=F32)
            keep = (s > hi) | (eq & (rank < need))
            sel_ref[c] = jnp.where(tie, jnp.where(keep, 1.0, 0.0), sel_ref[c])
            return before + jnp.sum(eqf, axis=0, keepdims=True)
        lax.fori_loop(0, nck, tie_chunk, jnp.zeros((1, QB), F32))

    scale2 = hd ** -0.5 * LOG2E
    for h in range(n_heads):
        qs_ref[h] = q_ref[:, h * hd:(h + 1) * hd].astype(BF16)
    m_ref[...] = jnp.full(m_ref.shape, NEG, F32)
    l_ref[...] = jnp.zeros(l_ref.shape, F32)
    acc_ref[...] = jnp.zeros(acc_ref.shape, F32)

    def logits(g, c):
        kc = kb_ref[g, pl.ds(pl.multiple_of(c * CK, CK), CK), :]
        qg = qs_ref[pl.ds(g * group, group)].reshape(group * QB, hd)
        st_ref[g % 2] = _dot_nt(kc, qg)

    logits(0, 0)

    def att_chunk(c, carry):
        msk = jnp.concatenate([sel_ref[c]] * group, axis=1) > 0.0
        tis = [jnp.clip(c * (CK // QB) + s - j + 2, 0, 2) for s in range(CK // QB)]
        for g in range(n_kv):
            if g + 1 < n_kv:
                logits(g + 1, c)
            else:
                logits(0, jnp.minimum(c + 1, nck - 1))
            rows = [jnp.concatenate([bias_ref[ti, g * group + hh] for hh in range(group)], axis=1)
                    for ti in tis]
            lg = jnp.where(msk, st_ref[g % 2] * scale2 + jnp.concatenate(rows, axis=0), -jnp.inf)
            m_old = m_ref[g]
            m_new = jnp.maximum(m_old, jnp.max(_fold_rows(lg, jnp.maximum), axis=0, keepdims=True))
            alpha = jnp.exp2(m_old - m_new)
            p = jnp.exp2(lg - m_new)
            l_ref[g] = alpha * l_ref[g] + jnp.sum(_fold_rows(p, jnp.add), axis=0, keepdims=True)
            pv = jnp.dot(vt_ref[g, c], p.astype(BF16), preferred_element_type=F32)
            acc_ref[g] = alpha * acc_ref[g] + pv
            m_ref[g] = m_new
        return carry
    lax.fori_loop(0, nck, att_chunk, 0)

    for g in range(n_kv):
        ot = acc_ref[g] / l_ref[g]
        for hh in range(group):
            cs = slice((g * group + hh) * hd, (g * group + hh + 1) * hd)
            o = ot[:, hh * QB:(hh + 1) * QB].T
            mixed = _sigmoid(ga_ref[:, cs]) * co_ref[:, cs] + _sigmoid(gb_ref[:, cs]) * o
            o_ref[:, cs] = mixed.astype(o_ref.dtype)


def attn_prompt(qi, kw, q, k, v, conv_out, gates, bias3, n_seq, seq, n_kv, idx_dim, idx_heads):
    m, dm = q.shape
    hd = k.shape[1] // n_kv
    n_heads = dm // hd
    group = n_heads // n_kv
    nb = seq // QB
    topk = min(TOPK_MAX, seq // 4)
    assert seq % CK == 0 and LANES % idx_dim == 0 and kw.shape[1] == LANES
    kern = functools.partial(_attn_prompt_kernel, seq=seq, n_heads=n_heads, n_kv=n_kv, idx_heads=idx_heads,
                             idx_dim=idx_dim, topk=topk, hd=hd)
    row = lambda b, j: (b * nb + j, 0)
    return pl.pallas_call(
        kern,
        grid=(n_seq, nb),
        in_specs=[pl.BlockSpec((QB, idx_heads * idx_dim), row),
                  pl.BlockSpec((QB, LANES), row),
                  pl.BlockSpec((seq, LANES), lambda b, j: (b, 0)),
                  pl.BlockSpec((QB, dm), row),
                  pl.BlockSpec((seq, n_kv * hd), lambda b, j: (b, 0)),
                  pl.BlockSpec((seq, n_kv * hd), lambda b, j: (b, 0)),
                  pl.BlockSpec((QB, dm), row),
                  pl.BlockSpec((QB, dm), lambda b, j: (b * nb + j, 0)),
                  pl.BlockSpec((QB, dm), lambda b, j: (b * nb + j, 1)),
                  pl.BlockSpec((3, n_heads, QB, QB), lambda b, j: (0, 0, 0, 0))],
        out_specs=pl.BlockSpec((QB, dm), row),
        out_shape=jax.ShapeDtypeStruct((m, dm), BF16),
        scratch_shapes=[pltpu.VMEM((seq, idx_dim), BF16),
                        pltpu.VMEM((n_kv, seq, hd), BF16),
                        pltpu.VMEM((n_kv, seq // CK, hd, CK), BF16),
                        pltpu.VMEM((LANES, QB), F32),
                        pltpu.VMEM((idx_heads, QB, idx_dim), BF16),
                        pltpu.VMEM((seq // CK, CK, QB), F32),
                        pltpu.VMEM((seq // CK, CK, QB), F32),
                        pltpu.VMEM((n_heads, QB, hd), BF16),
                        pltpu.VMEM((n_kv, 1, group * QB), F32),
                        pltpu.VMEM((n_kv, 1, group * QB), F32),
                        pltpu.VMEM((n_kv, hd, group * QB), F32),
                        pltpu.VMEM((2, CK, group * QB), F32)],
        compiler_params=_cparams(2),
        name="attn_prompt",
    )(qi, kw, kw, q, k, v, conv_out, gates, gates, bias3)


def _sample_score_kernel(pt_ref, qi_ref, wi_ref, kn_ref, *rest, pg, idx_heads, idx_dim):
    k_hbm, o_ref, self_ref, kbuf_ref, sem, kcat_ref = rest
    page = k_hbm.shape[2]
    b = pl.program_id(0)
    p = pl.program_id(1)
    n_steps = pl.num_programs(1)
    step = b * n_steps + p
    slot = step % 2

    def page_copies(bb, pp, sl):
        return [pltpu.make_async_copy(k_hbm.at[pt_ref[bb, pp * pg + i]], kbuf_ref.at[sl, i], sem.at[sl])
                for i in range(pg)]

    @pl.when(step == 0)
    def _():
        for c in page_copies(0, 0, 0):
            c.start()

    nxt = step + 1

    @pl.when(nxt < pl.num_programs(0) * n_steps)
    def _():
        for c in page_copies(nxt // n_steps, nxt % n_steps, 1 - slot):
            c.start()

    for c in page_copies(b, p, slot):
        c.wait()

    qi = qi_ref[...] * (idx_dim ** -0.5)
    wi = wi_ref[...] * (idx_heads ** -0.5)
    qb = qi.astype(BF16)
    for i in range(pg):
        kcat_ref[:, i * page:(i + 1) * page] = kbuf_ref[slot, i].astype(BF16)
    s = jnp.dot(qb, kcat_ref[...], preferred_element_type=F32)
    o_ref[...] = jnp.sum(wi * jnp.maximum(s, 0.0), axis=0, keepdims=True)

    @pl.when(pl.program_id(1) == 0)
    def _():
        kn = kn_ref[...].astype(BF16).astype(F32)
        s = jnp.sum(qb.astype(F32) * kn, axis=1, keepdims=True)
        sself = jnp.sum(wi * jnp.maximum(s, 0.0), axis=0, keepdims=True)
        self_ref[...] = jnp.broadcast_to(sself, self_ref.shape)


def sample_scores(page_table, qi3, wi3, ki_new3, cache_kidx_t, pg):
    n, n_pages = page_table.shape
    idx_heads, idx_dim = qi3.shape[1:]
    page = cache_kidx_t.shape[2]
    kern = functools.partial(_sample_score_kernel, pg=pg, idx_heads=idx_heads, idx_dim=idx_dim)
    grid_spec = pltpu.PrefetchScalarGridSpec(
        num_scalar_prefetch=1,
        grid=(n, n_pages // pg),
        in_specs=[pl.BlockSpec((None, idx_heads, idx_dim), lambda b, p, pt: (b, 0, 0)),
                  pl.BlockSpec((None, idx_heads, 1), lambda b, p, pt: (b, 0, 0)),
                  pl.BlockSpec((None, 1, idx_dim), lambda b, p, pt: (b, 0, 0)),
                  pl.BlockSpec(memory_space=pl.ANY)],
        out_specs=[pl.BlockSpec((None, None, 1, pg * page), lambda b, p, pt: (b, p, 0, 0)),
                   pl.BlockSpec((None, 1, LANES), lambda b, p, pt: (b, 0, 0))],
        scratch_shapes=[pltpu.VMEM((2, pg, idx_dim, page), F32), pltpu.SemaphoreType.DMA((2,)),
                        pltpu.VMEM((idx_dim, pg * page), BF16)],
    )
    return pl.pallas_call(
        kern,
        grid_spec=grid_spec,
        out_shape=[jax.ShapeDtypeStruct((n, n_pages // pg, 1, pg * page), F32),
                   jax.ShapeDtypeStruct((n, 1, LANES), F32)],
        compiler_params=_cparams(2),
        name="sample_scores",
    )(page_table, qi3, wi3, ki_new3, cache_kidx_t)


def _sample_select_kernel(sc_ref, self_ref, sel4_ref, selself_ref, sel_ref, *, topk, past, rep):
    sc = sc_ref[...]
    sself = self_ref[:, 0:1]
    n = sc.shape[0]
    kf = jnp.float32(topk)
    row_min = jnp.minimum(jnp.min(sc, axis=1, keepdims=True), sself)
    row_max = jnp.maximum(jnp.max(sc, axis=1, keepdims=True), sself)
    n_adm = jnp.full((n, 1), past + 1, F32)

    def count_gt(t):
        return (jnp.sum(jnp.where(sc > t, 1.0, 0.0), axis=1, keepdims=True)
                + jnp.where(sself > t, 1.0, 0.0))

    def any_fn(mask):
        return jnp.max(jnp.where(mask, 1.0, 0.0)) > 0.0

    lo, hi, flo, fhi = _select_threshold(count_gt, row_min, row_max, n_adm, topk, any_fn)
    tie = flo != kf
    sel_ref[...] = jnp.where(sc > lo, 1.0, 0.0)
    selself_ref[...] = jnp.broadcast_to(jnp.where(sself > lo, 1.0, 0.0), selself_ref.shape)

    @pl.when(any_fn(tie))
    def _():
        need = kf - fhi
        blk = 512
        tri = (lax.broadcasted_iota(jnp.int32, (blk, blk), 0)
               < lax.broadcasted_iota(jnp.int32, (blk, blk), 1)).astype(BF16)
        before = jnp.zeros((n, 1), F32)
        for c in range(past // blk):
            s = sc_ref[:, c * blk:(c + 1) * blk]
            eq = s == hi
            rank = before + jnp.dot(jnp.where(eq, 1.0, 0.0).astype(BF16), tri, preferred_element_type=F32)
            keep = (s > hi) | (eq & (rank < need))
            sel_ref[:, c * blk:(c + 1) * blk] = jnp.where(tie, jnp.where(keep, 1.0, 0.0),
                                                          sel_ref[:, c * blk:(c + 1) * blk])
            before = before + jnp.sum(jnp.where(eq, 1.0, 0.0), axis=1, keepdims=True)
        keep_self = (sself > hi) | ((sself == hi) & (before < need))
        selself_ref[...] = jnp.broadcast_to(
            jnp.where(tie, jnp.where(keep_self, 1.0, 0.0), jnp.where(sself > lo, 1.0, 0.0)), selself_ref.shape)

    blk = 512
    row_lo = lax.broadcasted_iota(jnp.int32, (blk, blk * rep), 0) * rep
    col = lax.broadcasted_iota(jnp.int32, (blk, blk * rep), 1)
    spread = jnp.where((col >= row_lo) & (col < row_lo + rep), 1.0, 0.0).astype(BF16)
    for c in range(past // blk):
        sel4_ref[:, c * blk * rep:(c + 1) * blk * rep] = jnp.dot(
            sel_ref[:, c * blk:(c + 1) * blk].astype(BF16), spread, preferred_element_type=F32)


def sample_select(scores, sself, topk, rep):
    n, past = scores.shape
    kern = functools.partial(_sample_select_kernel, topk=topk, past=past, rep=rep)
    return pl.pallas_call(
        kern,
        out_shape=[jax.ShapeDtypeStruct((n, past * rep), F32), jax.ShapeDtypeStruct((n, LANES), F32)],
        scratch_shapes=[pltpu.VMEM((n, past), F32)],
        compiler_params=pltpu.CompilerParams(vmem_limit_bytes=VMEM_LIMIT),
        name="sample_select",
    )(scores, sself)


def _sample_attn_kernel(pt_ref, q_ref, kn_ref, vn_ref, sel_ref, selself_ref, bias_ref, rb0_ref, own_ref, *rest,
                        pg, n_heads, n_kv, hd):
    k_hbm, v_hbm, o_ref, kbuf_ref, vbuf_ref, sem, kcat_ref, vcat_ref, m_ref, l_ref, acc_ref = rest
    b = pl.program_id(0)
    p = pl.program_id(1)
    n_steps = pl.num_programs(1)
    scale = hd ** -0.5
    rows = k_hbm.shape[1]
    step = b * n_steps + p
    slot = step % 2

    def page_copies(bb, pp, sl):
        copies = []
        for i in range(pg):
            pid = pt_ref[bb, pp * pg + i]
            dst = pl.ds(i * rows, rows)
            copies.append(pltpu.make_async_copy(k_hbm.at[pid], kbuf_ref.at[sl, dst], sem.at[0, sl]))
            copies.append(pltpu.make_async_copy(v_hbm.at[pid], vbuf_ref.at[sl, dst], sem.at[1, sl]))
        return copies

    @pl.when(step == 0)
    def _():
        for c in page_copies(0, 0, 0):
            c.start()

    nxt = step + 1

    @pl.when(nxt < pl.num_programs(0) * n_steps)
    def _():
        for c in page_copies(nxt // n_steps, nxt % n_steps, 1 - slot):
            c.start()

    for c in page_copies(b, p, slot):
        c.wait()

    @pl.when(p == 0)
    def _():
        m_ref[...] = jnp.full(m_ref.shape, NEG, F32)
        l_ref[...] = jnp.zeros(l_ref.shape, F32)
        acc_ref[...] = jnp.zeros(acc_ref.shape, F32)

    for i in range(pg):
        kcat_ref[i * rows:(i + 1) * rows, :] = kbuf_ref[slot, pl.ds(i * rows, rows), :].astype(BF16)
        vcat_ref[i * rows:(i + 1) * rows, :] = vbuf_ref[slot, pl.ds(i * rows, rows), :].astype(BF16)
    qb = q_ref[...].astype(BF16)
    last = p == n_steps - 1
    bias = jnp.concatenate([bias_ref[0]] * (pg - 1) + [jnp.where(last, bias_ref[1], bias_ref[0])], axis=1)
    msk = (own_ref[...] > 0.0) & (sel_ref[...] > 0.0)
    lg = jnp.where(msk, _dot_nt(qb, kcat_ref[...]) * scale + bias, -jnp.inf)
    m_old = m_ref[...]
    m_new = jnp.maximum(m_old, jnp.max(lg, axis=-1, keepdims=True))
    alpha = jnp.exp(m_old - m_new)
    pr = jnp.exp(lg - m_new)
    l_new = alpha * l_ref[...] + jnp.sum(pr, axis=-1, keepdims=True)
    acc = alpha * acc_ref[...] + jnp.dot(pr.astype(BF16), vcat_ref[...], preferred_element_type=F32)
    m_ref[...] = m_new
    l_ref[...] = l_new
    acc_ref[...] = acc

    @pl.when(last)
    def _():
        kn = kn_ref[...].astype(BF16).astype(F32)
        vn = vn_ref[...].astype(BF16).astype(F32)
        ls = jnp.sum(qb.astype(F32) * kn, axis=-1, keepdims=True) * scale + rb0_ref[...]
        on = selself_ref[:, 0:1] > 0.0
        ls = jnp.where(on, ls, NEG)
        m_f = jnp.maximum(m_new, ls)
        a2 = jnp.exp(m_new - m_f)
        ps = jnp.where(on, jnp.exp(ls - m_f), 0.0)
        l_f = a2 * l_new + ps
        acc_f = a2 * acc + ps.astype(BF16).astype(F32) * vn
        o_ref[...] = acc_f / l_f


def sample_attention(page_table, q3, k_rep, v_rep, sel4, selself3, bias_s, rb0, cache_k, cache_v, pg, n_kv):
    n, n_pages = page_table.shape
    n_heads, hd = q3.shape[1:]
    rows = cache_k.shape[1]
    kern = functools.partial(_sample_attn_kernel, pg=pg, n_heads=n_heads, n_kv=n_kv, hd=hd)
    own = (np.arange(pg * rows)[None, :] % n_kv
           == np.arange(n_heads)[:, None] // (n_heads // n_kv)).astype(np.float32)
    hbm = pl.BlockSpec(memory_space=pl.ANY)
    per_seq = lambda b, p, pt: (b, 0, 0)
    grid_spec = pltpu.PrefetchScalarGridSpec(
        num_scalar_prefetch=1,
        grid=(n, n_pages // pg),
        in_specs=[pl.BlockSpec((None, n_heads, hd), per_seq),
                  pl.BlockSpec((None, n_heads, hd), per_seq),
                  pl.BlockSpec((None, n_heads, hd), per_seq),
                  pl.BlockSpec((None, None, 1, pg * rows), lambda b, p, pt: (b, p, 0, 0)),
                  pl.BlockSpec((None, 1, LANES), per_seq),
                  pl.BlockSpec((2, n_heads, rows), lambda b, p, pt: (0, 0, 0)),
                  pl.BlockSpec((n_heads, 1), lambda b, p, pt: (0, 0)),
                  pl.BlockSpec((n_heads, pg * rows), lambda b, p, pt: (0, 0)), hbm, hbm],
        out_specs=pl.BlockSpec((None, n_heads, hd), per_seq),
        scratch_shapes=[pltpu.VMEM((2, pg * rows, hd), F32), pltpu.VMEM((2, pg * rows, hd), F32),
                        pltpu.SemaphoreType.DMA((2, 2)),
                        pltpu.VMEM((pg * rows, hd), BF16), pltpu.VMEM((pg * rows, hd), BF16),
                        pltpu.VMEM((n_heads, 1), F32), pltpu.VMEM((n_heads, 1), F32),
                        pltpu.VMEM((n_heads, hd), F32)],
    )
    return pl.pallas_call(
        kern,
        grid_spec=grid_spec,
        out_shape=jax.ShapeDtypeStruct((n, n_heads, hd), F32),
        compiler_params=_cparams(2),
        name="sample_attn",
    )(page_table, q3, k_rep, v_rep, sel4, selself3, bias_s, rb0, jnp.asarray(own), cache_k, cache_v)


def _mix_kernel(ga_ref, gb_ref, co_ref, at_ref, o_ref):
    o_ref[...] = (_sigmoid(ga_ref[...]) * co_ref[...] + _sigmoid(gb_ref[...]) * at_ref[...]).astype(o_ref.dtype)


def gated_mix(gates, conv_out, attn):
    n, dm = conv_out.shape
    return pl.pallas_call(
        _mix_kernel,
        grid=(1,),
        in_specs=[pl.BlockSpec((n, dm), lambda i: (0, 0)), pl.BlockSpec((n, dm), lambda i: (0, 1)),
                  pl.BlockSpec((n, dm), lambda i: (0, 0)), pl.BlockSpec((n, dm), lambda i: (0, 0))],
        out_specs=pl.BlockSpec((n, dm), lambda i: (0, 0)),
        out_shape=jax.ShapeDtypeStruct((n, dm), BF16),
    )(gates, gates, conv_out, attn)


def _ffn_up_kernel(x_ref, xh_ref, wg_ref, wv_ref, dg_ref, dv_ref, bg_ref, bv_ref, pg_ref, pv_ref,
                   o_ref, sg_ref, sv_ref, eg_ref, ev_ref, *, tm, rs, width, tiles_per_seq):
    hist = width - 1
    pad = xh_ref.shape[0]
    first = pl.program_id(0) % tiles_per_seq == 0
    branches = ((wg_ref, dg_ref, bg_ref, pg_ref, eg_ref, sg_ref),
                (wv_ref, dv_ref, bv_ref, pv_ref, ev_ref, sv_ref))
    for w_ref, _, _, prev_ref, e_ref, _ in branches:
        e_ref[pl.ds(0, pad), :] = jnp.dot(xh_ref[...], w_ref[...], preferred_element_type=F32)

        @pl.when(first)
        def _():
            e_ref[pl.ds(pad - hist, hist), :] = prev_ref[...]

    def project(r):
        for w_ref, _, _, _, e_ref, _ in branches:
            e_ref[pl.ds(pad + r * rs, rs), :] = jnp.dot(x_ref[pl.ds(r * rs, rs), :], w_ref[...],
                                                        preferred_element_type=F32)

    def activate(r):
        outs = []
        for _, d_ref, b_ref, _, e_ref, _ in branches:
            acc = b_ref[...] + d_ref[pl.ds(hist, 1), :] * e_ref[pl.ds(pad + r * rs, rs), :]
            for j in range(hist):
                acc = acc + d_ref[pl.ds(j, 1), :] * e_ref[pl.ds(pad + r * rs - hist + j, rs), :]
            outs.append(acc)
        g, v = outs
        o_ref[pl.ds(r * rs, rs), :] = (g * _sigmoid(g) * v).astype(o_ref.dtype)

    n_sub = tm // rs
    project(0)
    for r in range(1, n_sub):
        project(r)
        activate(r - 1)
    activate(n_sub - 1)
    for _, _, _, _, e_ref, s_ref in branches:
        s_ref[...] = e_ref[pl.ds(pad + tm - hist, hist), :]


def ffn_up_act(xn, w_up, prev, dw, bdw, n_seq, seq, tm, tn):
    m, k = xn.shape
    f2 = w_up.shape[1]
    f = f2 // 2
    width = dw.shape[0]
    ncb = f // tn
    pad = 16
    tps = seq // tm
    assert seq % tm == 0 and tm % pad == 0 and f % tn == 0 and width - 1 <= pad
    rs = min(256, tm // 2)
    assert tm % rs == 0 and rs % 8 == 0 and k % 256 == 0
    kern = functools.partial(_ffn_up_kernel, tm=tm, rs=rs, width=width, tiles_per_seq=tps)
    hb = tm // pad
    st_shape = jax.ShapeDtypeStruct((m // tm, width - 1, f), F32)
    act, sg, sv = pl.pallas_call(
        kern,
        grid=(m // tm, ncb),
        in_specs=[pl.BlockSpec((tm, k), lambda i, c: (i, 0)),
                  pl.BlockSpec((pad, k), lambda i, c: (jnp.maximum(i * hb - 1, 0), 0)),
                  pl.BlockSpec((k, tn), lambda i, c: (0, c)),
                  pl.BlockSpec((k, tn), lambda i, c: (0, c + ncb)),
                  pl.BlockSpec((width, tn), lambda i, c: (0, c)),
                  pl.BlockSpec((width, tn), lambda i, c: (0, c + ncb)),
                  pl.BlockSpec((1, tn), lambda i, c: (0, c)),
                  pl.BlockSpec((1, tn), lambda i, c: (0, c + ncb)),
                  pl.BlockSpec((None, width - 1, tn), lambda i, c: (i // tps, 0, c)),
                  pl.BlockSpec((None, width - 1, tn), lambda i, c: (i // tps, 0, c + ncb))],
        out_specs=[pl.BlockSpec((tm, tn), lambda i, c: (i, c)),
                   pl.BlockSpec((None, width - 1, tn), lambda i, c: (i, 0, c)),
                   pl.BlockSpec((None, width - 1, tn), lambda i, c: (i, 0, c))],
        out_shape=[jax.ShapeDtypeStruct((m, f), BF16), st_shape, st_shape],
        scratch_shapes=[pltpu.VMEM((pad + tm, tn), F32), pltpu.VMEM((pad + tm, tn), F32)],
        compiler_params=_cparams(2),
        name="ffn_up_act",
    )(xn, xn, w_up, w_up, dw, dw, bdw.reshape(1, f2), bdw.reshape(1, f2), prev, prev)
    state = jnp.concatenate([sg, sv], axis=-1).reshape(n_seq, tps, width - 1, f2)[:, tps - 1]
    return act, state


def _ffn_act_step_kernel(ug_ref, uv_ref, pg_ref, pv_ref, wg_ref, wv_ref, bg_ref, bv_ref, o_ref, *, width):
    def conv(u_ref, prev_ref, w_ref, b_ref):
        acc = b_ref[...] + w_ref[pl.ds(width - 1, 1), :] * u_ref[...]
        for j in range(width - 1):
            acc = acc + w_ref[pl.ds(j, 1), :] * prev_ref[j]
        return acc
    g = conv(ug_ref, pg_ref, wg_ref, bg_ref)
    v = conv(uv_ref, pv_ref, wv_ref, bv_ref)
    o_ref[...] = (g * _sigmoid(g) * v).astype(o_ref.dtype)


def ffn_act_step(u, prev_t, dw, bdw, tc):
    n, f2 = u.shape
    f = f2 // 2
    width = dw.shape[0]
    ncb = f // tc
    kern = functools.partial(_ffn_act_step_kernel, width=width)
    return pl.pallas_call(
        kern,
        grid=(ncb,),
        in_specs=[pl.BlockSpec((n, tc), lambda c: (0, c)),
                  pl.BlockSpec((n, tc), lambda c: (0, c + ncb)),
                  pl.BlockSpec((width - 1, n, tc), lambda c: (0, 0, c)),
                  pl.BlockSpec((width - 1, n, tc), lambda c: (0, 0, c + ncb)),
                  pl.BlockSpec((width, tc), lambda c: (0, c)),
                  pl.BlockSpec((width, tc), lambda c: (0, c + ncb)),
                  pl.BlockSpec((1, tc), lambda c: (0, c)),
                  pl.BlockSpec((1, tc), lambda c: (0, c + ncb))],
        out_specs=pl.BlockSpec((n, tc), lambda c: (0, c)),
        out_shape=jax.ShapeDtypeStruct((n, f), BF16),
        compiler_params=_cparams(1),
    )(u, u, prev_t, prev_t, dw, dw, bdw.reshape(1, f2), bdw.reshape(1, f2))


def _row_tile(m, cap):
    tm = cap
    while m % tm:
        tm //= 2
    return tm if tm >= 16 else m


def _in_projection(xn, w_main, w_kw, w_g, sizes, tm):
    glu_w, q_w, k_w, v_w, qi_w, ki_w, wi_w, ga_w, gb_w = sizes
    offs = np.concatenate([[0], np.cumsum(sizes)])
    tn = 512
    t = dict(w_transposed=True)
    glu = matmul_w(xn, w_main, int(offs[0]), glu_w, tm, tn, name="mm_glu", **t)
    q = matmul_w(xn, w_main, int(offs[1]), q_w, tm, tn, name="mm_q", out_dtype=BF16, **t)
    k = matmul_w(xn, w_main, int(offs[2]), k_w, tm, min(tn, k_w), name="mm_k", **t)
    v = matmul_w(xn, w_main, int(offs[3]), v_w, tm, min(tn, v_w), name="mm_v", **t)
    qi = matmul_w(xn, w_main, int(offs[4]), qi_w, tm, min(tn, qi_w), name="mm_qi", out_dtype=BF16, **t)
    kw = matmul_w(xn, w_kw, 0, LANES, tm, LANES, name="mm_kw", **t)
    gates = matmul_w(xn, w_g, 0, ga_w + gb_w, tm, tn, name="mm_gates", **t)
    return glu, q, k, v, qi, kw, gates


def kernel(x_prompt, x_sample, cache_k, cache_v, cache_kidx, state_conv, state_ffn, page_table, rel_bias,
           norm_attn, w_in, dw_conv, b_dw_conv, ln_conv_g, ln_conv_b, w_conv_out, w_o, norm_ffn, w_up, dw_ffn,
           b_dw_ffn, w_down, norm_final):
    bsz, seq, dm = x_prompt.shape
    nd, dec_seq, _ = x_sample.shape
    depth, n_pool, page, n_kv, hd = cache_k.shape
    idx_dim = cache_kidx.shape[-1]
    n_pages = page_table.shape[1]
    past = n_pages * page
    width, dconv = dw_conv.shape[1:]
    fwidth = dw_ffn.shape[1]
    f = w_down.shape[1]
    n_heads = w_o.shape[1] // hd
    d_attn = n_heads * hd
    d_kv = n_kv * hd
    n_in = w_in.shape[2]
    idx_heads = (n_in - 2 * dconv - d_attn - 2 * d_kv - idx_dim - 2 * dm) // (idx_dim + 1)
    sizes = (2 * dconv, d_attn, d_kv, d_kv, idx_heads * idx_dim, idx_dim, idx_heads, dm, dm)
    assert sum(sizes) == n_in and depth == 1 and dec_seq == 1 and page == LANES and d_attn == dm

    mp = bsz * seq
    xp = x_prompt.reshape(mp, dm)
    xs = x_sample.reshape(nd, dm)
    bias3, bias_s = bias_tables(rel_bias, page, n_kv)
    tmp = _row_tile(mp, 2048)
    drop = lambda a: a.reshape(a.shape[1:])
    (norm_attn, w_in, dw_conv, b_dw_conv, ln_conv_g, ln_conv_b, w_conv_out, w_o, norm_ffn, w_up, dw_ffn,
     b_dw_ffn, w_down, state_conv, state_ffn) = map(drop, (
         norm_attn, w_in, dw_conv, b_dw_conv, ln_conv_g, ln_conv_b, w_conv_out, w_o, norm_ffn, w_up, dw_ffn,
         b_dw_ffn, w_down, state_conv, state_ffn))
    kidx_pool = jnp.swapaxes(cache_kidx.reshape(n_pool, page, idx_dim), 1, 2)
    k_pool = cache_k.reshape(n_pool, page * n_kv, hd)
    v_pool = cache_v.reshape(n_pool, page * n_kv, hd)

    n_aligned = sum(sizes[:5])
    n_small = idx_dim + idx_heads
    w_in_t = jnp.swapaxes(w_in, 0, 1)
    w_main = cast_rows_bf16(w_in_t, 0, n_aligned)
    w_kw = cast_rows_bf16(w_in_t, n_aligned, LANES, n_small)
    w_g = cast_rows_bf16(w_in_t, n_aligned + n_small, 2 * dm)
    w_o, w_up, w_down = cast_bf16(w_o), cast_bf16(w_up), cast_bf16(w_down)

    xn = rmsnorm_rows(xp, norm_attn, BF16, 512)
    glu, q, k, v, qi, kw, gates = _in_projection(xn, w_main, w_kw, w_g, sizes, tmp)
    conv0 = jnp.zeros((bsz, width - 1, dconv), F32)
    conv_out, conv_state_p = conv_branch_prompt(glu, conv0, dw_conv, b_dw_conv, ln_conv_g, ln_conv_b,
                                                w_conv_out, bsz, seq, 256)
    mixed = attn_prompt(qi, kw, q, k, v, conv_out, gates, bias3, bsz, seq, n_kv, idx_dim, idx_heads)
    x2, xn2 = matmul_res_norm(mixed, w_o, xp, norm_ffn, _row_tile(mp, 512))
    ffn0 = jnp.zeros((bsz, fwidth - 1, 2 * f), F32)
    act, ffn_state_p = ffn_up_act(xn2, w_up, ffn0, dw_ffn, b_dw_ffn, bsz, seq, _row_tile(seq, 2048), 512)
    tk_down = f // 4 if f % (4 * LANES) == 0 else f
    y_prompt = matmul_res_norm_out(act, w_down, x2, norm_final, _row_tile(mp, 512), tk_down).reshape(bsz, seq, dm)

    xns = rmsnorm_rows(xs, norm_attn, BF16, nd)
    glu_s, q_s, k_s, v_s, qi_s, kw_s, gates_s = _in_projection(xns, w_main, w_kw, w_g, sizes, nd)
    ki_s = kw_s[:, :idx_dim]
    wi_s = kw_s[:, idx_dim:idx_dim + idx_heads]
    sc_prev_t = jnp.swapaxes(state_conv, 0, 1)
    conv_out_s, u_conv_s = conv_branch_step(glu_s, sc_prev_t, dw_conv, b_dw_conv, ln_conv_g, ln_conv_b,
                                            w_conv_out)
    conv_state_s = jnp.concatenate([state_conv[:, 1:], u_conv_s[:, None, :]], axis=1)

    pg = 16 if n_pages % 16 == 0 else 8
    pg_idx = 32 if n_pages % 32 == 0 else pg
    scores3, sself3 = sample_scores(page_table, qi_s.reshape(nd, idx_heads, idx_dim),
                                    wi_s.reshape(nd, idx_heads, 1), ki_s.reshape(nd, 1, idx_dim), kidx_pool,
                                    pg_idx)
    topk_s = min(TOPK_MAX, (past + dec_seq) // 4)
    sel4, selself = sample_select(scores3.reshape(nd, past), sself3.reshape(nd, LANES), topk_s, n_kv)
    group = n_heads // n_kv
    k_rep = jnp.repeat(k_s.reshape(nd, n_kv, hd), group, axis=1)
    v_rep = jnp.repeat(v_s.reshape(nd, n_kv, hd), group, axis=1)
    attn_s = sample_attention(page_table, q_s.reshape(nd, n_heads, hd), k_rep, v_rep,
                              sel4.reshape(nd, n_pages // pg, 1, pg * page * n_kv), selself.reshape(nd, 1, LANES),
                              bias_s,
                              rel_bias[0].reshape(n_heads, 1), k_pool, v_pool, pg, n_kv)
    mixed_s = gated_mix(gates_s, conv_out_s, attn_s.reshape(nd, dm))
    x2s, xn2s = matmul_res_norm(mixed_s, w_o, xs, norm_ffn, nd)
    u_s = matmul_w(xn2s, w_up, 0, 2 * f, nd, 512)
    sf_prev_t = jnp.swapaxes(state_ffn, 0, 1)
    act_s = ffn_act_step(u_s, sf_prev_t, dw_ffn, b_dw_ffn, 512)
    y_sample = matmul_res_norm_out(act_s, w_down, x2s, norm_final, nd, tk_down).reshape(nd, dec_seq, dm)
    ffn_state_s = jnp.concatenate([state_ffn[:, 1:], u_s[:, None, :]], axis=1)

    return (y_prompt, y_sample,
            k.reshape(1, bsz, seq, n_kv, hd), v.reshape(1, bsz, seq, n_kv, hd),
            kw.reshape(bsz, seq, LANES)[None, :, :, :idx_dim],
            conv_state_p[None], ffn_state_p[None],
            k_s.reshape(1, nd, dec_seq, n_kv, hd), v_s.reshape(1, nd, dec_seq, n_kv, hd),
            ki_s.reshape(1, nd, dec_seq, idx_dim),
            conv_state_s[None], ffn_state_s[None])
```

```python
import functools
import math

import numpy as np
import jax
import jax.numpy as jnp
from jax import lax
from jax.experimental import pallas as pl
from jax.experimental.pallas import tpu as pltpu

F32 = jnp.float32
BF16 = jnp.bfloat16

EPS = 1e-6
TOPK_MAX = 256
N_BUCKETS = 32
MAX_DISTANCE = 128
QB = 128
CK = 256
LANES = 128
NEG = -1e30
LOG2E = math.log2(math.e)
VMEM_LIMIT = 56 * 1024 * 1024


def _cparams(n_axes, vmem=VMEM_LIMIT):
    return pltpu.CompilerParams(dimension_semantics=("arbitrary",) * n_axes, vmem_limit_bytes=vmem)


def _dot_nt(a, b):
    return lax.dot_general(a, b, (((1,), (1,)), ((), ())), preferred_element_type=F32)


def _sigmoid(x):
    return 1.0 / (1.0 + jnp.exp(-x))


def _fold_rows(x, op):
    while x.shape[0] > 8:
        half = x.shape[0] // 2
        x = op(x[:half], x[half:])
    return x


def _rel_bucket_np(dist):
    n = np.maximum(dist, 0)
    max_exact = N_BUCKETS // 2
    nf = np.maximum(n, 1).astype(np.float32)
    large = max_exact + (np.log(nf / np.float32(max_exact)) / np.float32(math.log(MAX_DISTANCE / max_exact))
                         * np.float32(N_BUCKETS - max_exact)).astype(np.int32)
    large = np.minimum(large, N_BUCKETS - 1)
    return np.where(n < max_exact, n, large).astype(np.int32)


def _rms_kernel(x_ref, g_ref, o_ref):
    x = x_ref[...]
    y = x * lax.rsqrt(jnp.mean(x * x, axis=-1, keepdims=True) + EPS) * g_ref[...]
    o_ref[...] = y.astype(o_ref.dtype)


def rmsnorm_rows(x, g, out_dtype, tm):
    m, d = x.shape
    return pl.pallas_call(
        _rms_kernel,
        grid=(m // tm,),
        in_specs=[pl.BlockSpec((tm, d), lambda i: (i, 0)), pl.BlockSpec((1, d), lambda i: (0, 0))],
        out_specs=pl.BlockSpec((tm, d), lambda i: (i, 0)),
        out_shape=jax.ShapeDtypeStruct((m, d), out_dtype),
        compiler_params=_cparams(1),
        name="rmsnorm",
    )(x, g.reshape(1, d))


def _cast_kernel(w_ref, o_ref):
    o_ref[...] = w_ref[...].astype(o_ref.dtype)


def cast_bf16(w, ncols=None):
    k, n = w.shape
    ncols = n if ncols is None else ncols
    tk = 512 if k % 512 == 0 else k
    tn = 1024 if ncols % 1024 == 0 else (512 if ncols % 512 == 0 else ncols)
    return pl.pallas_call(
        _cast_kernel,
        grid=(k // tk, ncols // tn),
        in_specs=[pl.BlockSpec((tk, tn), lambda i, j: (i, j))],
        out_specs=pl.BlockSpec((tk, tn), lambda i, j: (i, j)),
        out_shape=jax.ShapeDtypeStruct((k, ncols), BF16),
        compiler_params=_cparams(2),
        name="cast_bf16",
    )(w)


def _cast_rows_kernel(w_ref, o_ref, *, valid):
    w = w_ref[...]
    if valid < w.shape[0]:
        w = jnp.where(lax.broadcasted_iota(jnp.int32, w.shape, 0) < valid, w, 0.0)
    o_ref[...] = w.astype(o_ref.dtype)


def cast_rows_bf16(wt, row0, nrows, nvalid=None):
    n, k = wt.shape
    tr = 512 if nrows % 512 == 0 else nrows
    nvalid = nrows if nvalid is None else nvalid
    assert row0 % 8 == 0 and (nvalid == nrows or tr == nrows) and row0 + nrows <= n
    return pl.pallas_call(
        functools.partial(_cast_rows_kernel, valid=nvalid),
        grid=(nrows // tr,),
        in_specs=[pl.BlockSpec((pl.Element(tr), pl.Element(k)), lambda i: (pl.multiple_of(row0 + i * tr, 8), 0))],
        out_specs=pl.BlockSpec((tr, k), lambda i: (i, 0)),
        out_shape=jax.ShapeDtypeStruct((nrows, k), BF16),
        compiler_params=_cparams(1),
        name="cast_rows_bf16",
    )(wt)


def _mm_kernel(a_ref, w_ref, o_ref):
    o_ref[...] = jnp.dot(a_ref[...], w_ref[...].astype(BF16), preferred_element_type=F32).astype(o_ref.dtype)


def _mm_nt_kernel(a_ref, wt_ref, o_ref, *, sigmoid):
    y = _dot_nt(a_ref[...], wt_ref[...])
    o_ref[...] = (_sigmoid(y) if sigmoid else y).astype(o_ref.dtype)


def matmul_w(a, w, col0, ncols, tm, tn, name="matmul", out_dtype=F32, w_transposed=False, sigmoid=False):
    m, k = a.shape
    assert col0 % tn == 0 and ncols % tn == 0 and m % tm == 0 and (w_transposed or not sigmoid)
    cb = col0 // tn
    if w_transposed:
        w_spec = pl.BlockSpec((tn, k), lambda i, j: (j + cb, 0))
    else:
        w_spec = pl.BlockSpec((k, tn), lambda i, j: (0, j + cb))
    return pl.pallas_call(
        functools.partial(_mm_nt_kernel, sigmoid=sigmoid) if w_transposed else _mm_kernel,
        grid=(m // tm, ncols // tn),
        in_specs=[pl.BlockSpec((tm, k), lambda i, j: (i, 0)), w_spec],
        out_specs=pl.BlockSpec((tm, tn), lambda i, j: (i, j)),
        out_shape=jax.ShapeDtypeStruct((m, ncols), out_dtype),
        compiler_params=_cparams(2),
        name=name,
    )(a, w)


def _rms(x, g):
    return x * lax.rsqrt(jnp.mean(x * x, axis=-1, keepdims=True) + EPS) * g


def _mm_res_norm_kernel(a_ref, w_ref, r_ref, g_ref, o_ref, on_ref):
    x = r_ref[...] + jnp.dot(a_ref[...], w_ref[...], preferred_element_type=F32)
    o_ref[...] = x
    on_ref[...] = _rms(x, g_ref[...]).astype(on_ref.dtype)


def matmul_res_norm(a, w, res, g, tm):
    m, k = a.shape
    n = w.shape[1]
    return pl.pallas_call(
        _mm_res_norm_kernel,
        grid=(m // tm,),
        in_specs=[pl.BlockSpec((tm, k), lambda i: (i, 0)), pl.BlockSpec((k, n), lambda i: (0, 0)),
                  pl.BlockSpec((tm, n), lambda i: (i, 0)), pl.BlockSpec((1, n), lambda i: (0, 0))],
        out_specs=[pl.BlockSpec((tm, n), lambda i: (i, 0)), pl.BlockSpec((tm, n), lambda i: (i, 0))],
        out_shape=[jax.ShapeDtypeStruct((m, n), F32), jax.ShapeDtypeStruct((m, n), BF16)],
        compiler_params=_cparams(1),
        name="mm_o_norm",
    )(a, w, res, g.reshape(1, n))


def _mm_ksplit_norm_kernel(a_ref, w_ref, r_ref, g_ref, o_ref, acc_ref):
    kk = pl.program_id(1)

    @pl.when(kk == 0)
    def _():
        acc_ref[...] = r_ref[...]
    acc_ref[...] += jnp.dot(a_ref[...], w_ref[...], preferred_element_type=F32)

    @pl.when(kk == pl.num_programs(1) - 1)
    def _():
        o_ref[...] = _rms(acc_ref[...], g_ref[...])


def matmul_res_norm_out(a, w, res, g, tm, tk):
    m, k = a.shape
    n = w.shape[1]
    assert k % tk == 0 and m % tm == 0
    return pl.pallas_call(
        _mm_ksplit_norm_kernel,
        grid=(m // tm, k // tk),
        in_specs=[pl.BlockSpec((tm, tk), lambda i, kk: (i, kk)), pl.BlockSpec((tk, n), lambda i, kk: (kk, 0)),
                  pl.BlockSpec((tm, n), lambda i, kk: (i, 0)), pl.BlockSpec((1, n), lambda i, kk: (0, 0))],
        out_specs=pl.BlockSpec((tm, n), lambda i, kk: (i, 0)),
        out_shape=jax.ShapeDtypeStruct((m, n), F32),
        scratch_shapes=[pltpu.VMEM((tm, n), F32)],
        compiler_params=_cparams(2),
        name="mm_down_norm",
    )(a, w, res, g.reshape(1, n))


def _bias_kernel(rb_ref, bk3_ref, bks_ref, o3_ref, os_ref, *, n_heads):
    def head(h, carry):
        for t in range(3):
            bk = bk3_ref[t]
            acc = jnp.zeros(bk.shape, F32)
            for b in range(N_BUCKETS):
                acc = jnp.where(bk == b, rb_ref[b, h], acc)
            o3_ref[t, h] = acc * LOG2E
        for t in range(2):
            bk = bks_ref[t]
            acc = jnp.zeros(bk.shape, F32)
            for b in range(N_BUCKETS):
                acc = jnp.where(bk == b, rb_ref[b, h], acc)
            os_ref[t, pl.ds(h, 1), :] = acc
        return carry
    lax.fori_loop(0, n_heads, head, 0)


def bias_tables(rel_bias, page, rep):
    n_heads = rel_bias.shape[1]
    cols = page * rep
    i = np.arange(QB)[None, :]
    k = np.arange(QB)[:, None]
    bk3 = np.stack([_rel_bucket_np(i - k + 2 * QB), _rel_bucket_np(i - k + QB), _rel_bucket_np(i - k)])
    assert (_rel_bucket_np(np.arange(QB + 1, 1 << 20)) == N_BUCKETS - 1).all()
    assert (bk3[0] == N_BUCKETS - 1).all()
    assert page >= QB
    bks = np.stack([np.full((1, cols), N_BUCKETS - 1, np.int32),
                    _rel_bucket_np(page - np.arange(cols) // rep)[None, :]])
    return pl.pallas_call(
        functools.partial(_bias_kernel, n_heads=n_heads),
        in_specs=[pl.BlockSpec(memory_space=pltpu.SMEM),
                  pl.BlockSpec(memory_space=pltpu.VMEM), pl.BlockSpec(memory_space=pltpu.VMEM)],
        out_specs=[pl.BlockSpec(memory_space=pltpu.VMEM), pl.BlockSpec(memory_space=pltpu.VMEM)],
        out_shape=[jax.ShapeDtypeStruct((3, n_heads, QB, QB), F32),
                   jax.ShapeDtypeStruct((2, n_heads, cols), F32)],
    )(rel_bias, jnp.asarray(bk3), jnp.asarray(bks))


def _conv_kernel(glu_ref, prev_ref, dw_ref, bdw_ref, lng_ref, lnb_ref, wout_ref, o_ref, st_ref,
                 ext_ref, h_ref, wbf_ref, sh_ref, *, tt, width, dconv):
    b = pl.program_id(0)
    t = pl.program_id(1)
    pad = 32
    hist = width - 1

    @pl.when((b == 0) & (t == 0))
    def _():
        wbf_ref[...] = wout_ref[...].astype(BF16)

    @pl.when(t == 0)
    def _():
        ext_ref[pl.ds(pad - hist, hist), :] = prev_ref[...]

    @pl.when(t > 0)
    def _():
        ext_ref[pl.ds(0, pad), :] = ext_ref[pl.ds(tt, pad), :]

    glu = glu_ref[...]
    u = glu[:, :dconv] * _sigmoid(glu[:, dconv:])
    ext_ref[pl.ds(pad, tt), :] = u
    st_ref[...] = ext_ref[pl.ds(pad + tt - hist, hist), :]

    span = tt + pad - 8
    for s in range(1, 8):
        sh_ref[s - 1] = ext_ref[pl.ds(s, span), :]

    for c in range(dconv // LANES):
        cs = slice(c * LANES, (c + 1) * LANES)
        acc = jnp.zeros((tt, LANES), F32) + bdw_ref[:, cs]
        for j in range(width):
            a, s = divmod(pad - hist + j, 8)
            rows = ext_ref[pl.ds(8 * a, tt), cs] if s == 0 else sh_ref[s - 1, pl.ds(8 * a, tt), cs]
            acc = acc + dw_ref[pl.ds(j, 1), cs] * rows
        h_ref[:, cs] = acc

    h = h_ref[...]
    mu = jnp.mean(h, axis=-1, keepdims=True)
    var = jnp.mean(jnp.square(h - mu), axis=-1, keepdims=True)
    y = (h - mu) * lax.rsqrt(var + EPS) * lng_ref[...] + lnb_ref[...]
    y = y * _sigmoid(y)
    o_ref[...] = jnp.dot(y.astype(BF16), wbf_ref[...], preferred_element_type=F32)


def conv_branch_prompt(glu_pre, prev, dw, bdw, lng, lnb, wout, n_seq, seq, tt):
    width, dconv = dw.shape
    dm = wout.shape[1]
    nt = seq // tt
    kern = functools.partial(_conv_kernel, tt=tt, width=width, dconv=dconv)
    return pl.pallas_call(
        kern,
        grid=(n_seq, nt),
        in_specs=[pl.BlockSpec((tt, 2 * dconv), lambda b, t: (b * nt + t, 0)),
                  pl.BlockSpec((None, width - 1, dconv), lambda b, t: (b, 0, 0)),
                  pl.BlockSpec((width, dconv), lambda b, t: (0, 0)),
                  pl.BlockSpec((1, dconv), lambda b, t: (0, 0)),
                  pl.BlockSpec((1, dconv), lambda b, t: (0, 0)),
                  pl.BlockSpec((1, dconv), lambda b, t: (0, 0)),
                  pl.BlockSpec((dconv, dm), lambda b, t: (0, 0))],
        out_specs=[pl.BlockSpec((tt, dm), lambda b, t: (b * nt + t, 0)),
                   pl.BlockSpec((None, width - 1, dconv), lambda b, t: (b, 0, 0))],
        out_shape=[jax.ShapeDtypeStruct((n_seq * seq, dm), F32),
                   jax.ShapeDtypeStruct((n_seq, width - 1, dconv), F32)],
        scratch_shapes=[pltpu.VMEM((32 + tt, dconv), F32), pltpu.VMEM((tt, dconv), F32),
                        pltpu.VMEM((dconv, dm), BF16), pltpu.VMEM((7, 24 + tt, dconv), F32)],
        compiler_params=_cparams(2),
        name="conv_branch",
    )(glu_pre, prev, dw, bdw.reshape(1, dconv), lng.reshape(1, dconv), lnb.reshape(1, dconv), wout)


def _conv_step_kernel(glu_ref, prev_ref, dw_ref, bdw_ref, lng_ref, lnb_ref, wout_ref, o_ref, u_ref,
                      *, width, dconv):
    glu = glu_ref[...]
    u = glu[:, :dconv] * _sigmoid(glu[:, dconv:])
    u_ref[...] = u
    h = bdw_ref[...] + dw_ref[pl.ds(width - 1, 1), :] * u
    for j in range(width - 1):
        h = h + dw_ref[pl.ds(j, 1), :] * prev_ref[j]
    mu = jnp.mean(h, axis=-1, keepdims=True)
    var = jnp.mean(jnp.square(h - mu), axis=-1, keepdims=True)
    y = (h - mu) * lax.rsqrt(var + EPS) * lng_ref[...] + lnb_ref[...]
    y = y * _sigmoid(y)
    o_ref[...] = jnp.dot(y.astype(BF16), wout_ref[...].astype(BF16), preferred_element_type=F32)


def conv_branch_step(glu_pre, prev_t, dw, bdw, lng, lnb, wout):
    width, dconv = dw.shape
    n = glu_pre.shape[0]
    dm = wout.shape[1]
    kern = functools.partial(_conv_step_kernel, width=width, dconv=dconv)
    return pl.pallas_call(
        kern,
        out_shape=[jax.ShapeDtypeStruct((n, dm), F32), jax.ShapeDtypeStruct((n, dconv), F32)],
        compiler_params=pltpu.CompilerParams(vmem_limit_bytes=VMEM_LIMIT),
    )(glu_pre, prev_t, dw, bdw.reshape(1, dconv), lng.reshape(1, dconv), lnb.reshape(1, dconv), wout)


def _select_threshold(count_gt, row_min, row_max, n_adm, topk, any_fn):
    kf = jnp.float32(topk)
    full = n_adm <= kf
    lo0 = row_min - (1.0 + jnp.abs(row_min))
    hi0 = row_max
    flo0 = jnp.where(full, kf, n_adm)
    fhi0 = jnp.zeros_like(lo0)

    def active_rows(lo, hi, flo):
        mid = 0.5 * lo + 0.5 * hi
        return (flo != kf) & (lo < mid) & (mid < hi)

    def cond(st):
        lo, hi, flo, fhi = st
        return any_fn(active_rows(lo, hi, flo))

    def step(st):
        lo, hi, flo, fhi = st
        act = active_rows(lo, hi, flo)
        mid = 0.5 * lo + 0.5 * hi
        c = count_gt(mid)
        up = act & (c >= kf)
        dn = act & (c < kf)
        return (jnp.where(up, mid, lo), jnp.where(dn, mid, hi),
                jnp.where(up, c, flo), jnp.where(dn, c, fhi))

    def body(st):
        for _ in range(4):
            st = step(st)
        return st

    lo, hi, flo, fhi = lax.while_loop(cond, body, (lo0, hi0, flo0, fhi0))
    lo = jnp.where(full, -jnp.inf, lo)
    return lo, hi, flo, fhi


def _attn_prompt_kernel(qi_ref, wi_ref, kw_ref, q_ref, k_ref, v_ref, co_ref, ga_ref, gb_ref, bias_ref,
                        o_ref, kd_ref, kb_ref, vt_ref, wit_ref, qib_ref, sc_ref, sel_ref, qs_ref,
                        m_ref, l_ref, acc_ref, st_ref,
                        *, seq, n_heads, n_kv, idx_heads, idx_dim, topk, hd):
    j = pl.program_id(1)
    group = n_heads // n_kv
    nck = (j * QB + QB + CK - 1) // CK
    kf = jnp.float32(topk)

    @pl.when(j == 0)
    def _():
        kd_ref[...] = kw_ref[:, :idx_dim].astype(BF16)
        for g in range(n_kv):
            kb_ref[g] = k_ref[:, g * hd:(g + 1) * hd].astype(BF16)
            for c in range(seq // CK):
                vt_ref[g, c] = v_ref[c * CK:(c + 1) * CK, g * hd:(g + 1) * hd].T.astype(BF16)

    wit_ref[...] = (wi_ref[...] * (idx_heads ** -0.5)).T
    qi = qi_ref[...].astype(F32) * (idx_dim ** -0.5)
    for h in range(idx_heads):
        qib_ref[h] = qi[:, h * idx_dim:(h + 1) * idx_dim].astype(BF16)
    qpos = j * QB + lax.broadcasted_iota(jnp.int32, (CK, QB), 1)
    krow = lax.broadcasted_iota(jnp.int32, (CK, QB), 0)
    per_dot = 2 * LANES // QB

    def score_chunk(c, carry):
        mn, mx = carry
        k0 = pl.multiple_of(c * CK, CK)
        kc = kd_ref[pl.ds(k0, CK), :]
        acc = jnp.zeros((CK, QB), F32)
        for p in range(idx_heads // per_dot):
            rhs = qib_ref[pl.ds(p * per_dot, per_dot)].reshape(per_dot * QB, idx_dim)
            s = _dot_nt(kc, rhs)
            for r in range(per_dot):
                acc = acc + (wit_ref[pl.ds(idx_dim + p * per_dot + r, 1), :]
                             * jnp.maximum(s[:, r * QB:(r + 1) * QB], 0.0))
        adm = (krow + c * CK) <= qpos
        sc_ref[c] = jnp.where(adm, acc, -jnp.inf)
        mn = jnp.minimum(mn, _fold_rows(jnp.where(adm, acc, jnp.inf), jnp.minimum))
        mx = jnp.maximum(mx, _fold_rows(jnp.where(adm, acc, -jnp.inf), jnp.maximum))
        return mn, mx

    mn8, mx8 = lax.fori_loop(0, nck, score_chunk,
                             (jnp.full((8, QB), jnp.inf, F32), jnp.full((8, QB), -jnp.inf, F32)))
    row_min = jnp.min(mn8, axis=0, keepdims=True)
    row_max = jnp.max(mx8, axis=0, keepdims=True)
    n_adm = (j * QB + 1 + lax.broadcasted_iota(jnp.int32, (1, QB), 1)).astype(F32)

    def count_gt(t):
        def cbody(c, acc):
            return acc + _fold_rows(jnp.where(sc_ref[c] > t, 1.0, 0.0), jnp.add)
        part = lax.fori_loop(0, nck, cbody, jnp.zeros((8, QB), F32))
        return jnp.sum(part, axis=0, keepdims=True)

    def any_fn(mask):
        return jnp.max(jnp.where(mask, 1.0, 0.0)) > 0.0

    lo, hi, flo, fhi = _select_threshold(count_gt, row_min, row_max, n_adm, topk, any_fn)
    tie = flo != kf

    def sel_chunk(c, carry):
        sel_ref[c] = jnp.where(sc_ref[c] > lo, 1.0, 0.0)
        return carry
    lax.fori_loop(0, nck, sel_chunk, 0)

    @pl.when(any_fn(tie))
    def _():
        need = kf - fhi
        lower = (lax.broadcasted_iota(jnp.int32, (CK, CK), 1)
                 < lax.broadcasted_iota(jnp.int32, (CK, CK), 0)).astype(BF16)

        def tie_chunk(c, before):
            s = sc_ref[c]
            eq = s == hi
            eqf = jnp.where(eq, 1.0, 0.0)
            rank = before + jnp.dot(lower, eqf.astype(BF16), preferred_element_type=F32)
            keep = (s > hi) | (eq & (rank < need))
            sel_ref[c] = jnp.where(tie, jnp.where(keep, 1.0, 0.0), sel_ref[c])
            return before + jnp.sum(eqf, axis=0, keepdims=True)
        lax.fori_loop(0, nck, tie_chunk, jnp.zeros((1, QB), F32))

    scale2 = hd ** -0.5 * LOG2E
    for h in range(n_heads):
        qs_ref[h] = q_ref[:, h * hd:(h + 1) * hd].astype(BF16)
    m_ref[...] = jnp.full(m_ref.shape, NEG, F32)
    l_ref[...] = jnp.zeros(l_ref.shape, F32)
    acc_ref[...] = jnp.zeros(acc_ref.shape, F32)

    def logits(g, c):
        kc = kb_ref[g, pl.ds(pl.multiple_of(c * CK, CK), CK), :]
        qg = qs_ref[pl.ds(g * group, group)].reshape(group * QB, hd)
        st_ref[g % 2] = _dot_nt(kc, qg)

    logits(0, 0)

    def att_chunk(c, carry):
        msk = jnp.concatenate([sel_ref[c]] * group, axis=1) > 0.0
        tis = [jnp.clip(c * (CK // QB) + s - j + 2, 0, 2) for s in range(CK // QB)]
        for g in range(n_kv):
            if g + 1 < n_kv:
                logits(g + 1, c)
            else:
                logits(0, jnp.minimum(c + 1, nck - 1))
            rows = [jnp.concatenate([bias_ref[ti, g * group + hh] for hh in range(group)], axis=1)
                    for ti in tis]
            lg = jnp.where(msk, st_ref[g % 2] * scale2 + jnp.concatenate(rows, axis=0), -jnp.inf)
            m_old = m_ref[g]
            m_new = jnp.maximum(m_old, jnp.max(_fold_rows(lg, jnp.maximum), axis=0, keepdims=True))
            alpha = jnp.exp2(m_old - m_new)
            p = jnp.exp2(lg - m_new)
            l_ref[g] = alpha * l_ref[g] + jnp.sum(_fold_rows(p, jnp.add), axis=0, keepdims=True)
            pv = jnp.dot(vt_ref[g, c], p.astype(BF16), preferred_element_type=F32)
            acc_ref[g] = alpha * acc_ref[g] + pv
            m_ref[g] = m_new
        return carry
    lax.fori_loop(0, nck, att_chunk, 0)

    for g in range(n_kv):
        ot = acc_ref[g] / l_ref[g]
        for hh in range(group):
            cs = slice((g * group + hh) * hd, (g * group + hh + 1) * hd)
            o = ot[:, hh * QB:(hh + 1) * QB].T
            mixed = ga_ref[:, cs] * co_ref[:, cs] + gb_ref[:, cs] * o
            o_ref[:, cs] = mixed.astype(o_ref.dtype)


def attn_prompt(qi, kw, q, k, v, conv_out, gates, bias3, n_seq, seq, n_kv, idx_dim, idx_heads):
    m, dm = q.shape
    hd = k.shape[1] // n_kv
    n_heads = dm // hd
    group = n_heads // n_kv
    nb = seq // QB
    topk = min(TOPK_MAX, seq // 4)
    assert seq % CK == 0 and LANES % idx_dim == 0 and kw.shape[1] == LANES
    kern = functools.partial(_attn_prompt_kernel, seq=seq, n_heads=n_heads, n_kv=n_kv, idx_heads=idx_heads,
                             idx_dim=idx_dim, topk=topk, hd=hd)
    row = lambda b, j: (b * nb + j, 0)
    return pl.pallas_call(
        kern,
        grid=(n_seq, nb),
        in_specs=[pl.BlockSpec((QB, idx_heads * idx_dim), row),
                  pl.BlockSpec((QB, LANES), row),
                  pl.BlockSpec((seq, LANES), lambda b, j: (b, 0)),
                  pl.BlockSpec((QB, dm), row),
                  pl.BlockSpec((seq, n_kv * hd), lambda b, j: (b, 0)),
                  pl.BlockSpec((seq, n_kv * hd), lambda b, j: (b, 0)),
                  pl.BlockSpec((QB, dm), row),
                  pl.BlockSpec((QB, dm), lambda b, j: (b * nb + j, 0)),
                  pl.BlockSpec((QB, dm), lambda b, j: (b * nb + j, 1)),
                  pl.BlockSpec((3, n_heads, QB, QB), lambda b, j: (0, 0, 0, 0))],
        out_specs=pl.BlockSpec((QB, dm), row),
        out_shape=jax.ShapeDtypeStruct((m, dm), BF16),
        scratch_shapes=[pltpu.VMEM((seq, idx_dim), BF16),
                        pltpu.VMEM((n_kv, seq, hd), BF16),
                        pltpu.VMEM((n_kv, seq // CK, hd, CK), BF16),
                        pltpu.VMEM((LANES, QB), F32),
                        pltpu.VMEM((idx_heads, QB, idx_dim), BF16),
                        pltpu.VMEM((seq // CK, CK, QB), F32),
                        pltpu.VMEM((seq // CK, CK, QB), F32),
                        pltpu.VMEM((n_heads, QB, hd), BF16),
                        pltpu.VMEM((n_kv, 1, group * QB), F32),
                        pltpu.VMEM((n_kv, 1, group * QB), F32),
                        pltpu.VMEM((n_kv, hd, group * QB), F32),
                        pltpu.VMEM((2, CK, group * QB), F32)],
        compiler_params=_cparams(2),
        name="attn_prompt",
    )(qi, kw, kw, q, k, v, conv_out, gates, gates, bias3)


def _sample_score_kernel(pt_ref, qi_ref, wi_ref, kn_ref, *rest, pg, idx_heads, idx_dim):
    k_hbm, o_ref, self_ref, kbuf_ref, sem, kcat_ref = rest
    page = k_hbm.shape[2]
    b = pl.program_id(0)
    p = pl.program_id(1)
    n_steps = pl.num_programs(1)
    step = b * n_steps + p
    slot = step % 2

    def page_copies(bb, pp, sl):
        return [pltpu.make_async_copy(k_hbm.at[pt_ref[bb, pp * pg + i]], kbuf_ref.at[sl, i], sem.at[sl])
                for i in range(pg)]

    @pl.when(step == 0)
    def _():
        for c in page_copies(0, 0, 0):
            c.start()

    nxt = step + 1

    @pl.when(nxt < pl.num_programs(0) * n_steps)
    def _():
        for c in page_copies(nxt // n_steps, nxt % n_steps, 1 - slot):
            c.start()

    for c in page_copies(b, p, slot):
        c.wait()

    qi = qi_ref[...] * (idx_dim ** -0.5)
    wi = wi_ref[...] * (idx_heads ** -0.5)
    qb = qi.astype(BF16)
    for i in range(pg):
        kcat_ref[:, i * page:(i + 1) * page] = kbuf_ref[slot, i].astype(BF16)
    s = jnp.dot(qb, kcat_ref[...], preferred_element_type=F32)
    o_ref[...] = jnp.sum(wi * jnp.maximum(s, 0.0), axis=0, keepdims=True)

    @pl.when(pl.program_id(1) == 0)
    def _():
        kn = kn_ref[...].astype(BF16).astype(F32)
        s = jnp.sum(qb.astype(F32) * kn, axis=1, keepdims=True)
        sself = jnp.sum(wi * jnp.maximum(s, 0.0), axis=0, keepdims=True)
        self_ref[...] = jnp.broadcast_to(sself, self_ref.shape)


def sample_scores(page_table, qi3, wi3, ki_new3, cache_kidx_t, pg):
    n, n_pages = page_table.shape
    idx_heads, idx_dim = qi3.shape[1:]
    page = cache_kidx_t.shape[2]
    kern = functools.partial(_sample_score_kernel, pg=pg, idx_heads=idx_heads, idx_dim=idx_dim)
    grid_spec = pltpu.PrefetchScalarGridSpec(
        num_scalar_prefetch=1,
        grid=(n, n_pages // pg),
        in_specs=[pl.BlockSpec((None, idx_heads, idx_dim), lambda b, p, pt: (b, 0, 0)),
                  pl.BlockSpec((None, idx_heads, 1), lambda b, p, pt: (b, 0, 0)),
                  pl.BlockSpec((None, 1, idx_dim), lambda b, p, pt: (b, 0, 0)),
                  pl.BlockSpec(memory_space=pl.ANY)],
        out_specs=[pl.BlockSpec((None, None, 1, pg * page), lambda b, p, pt: (b, p, 0, 0)),
                   pl.BlockSpec((None, 1, LANES), lambda b, p, pt: (b, 0, 0))],
        scratch_shapes=[pltpu.VMEM((2, pg, idx_dim, page), F32), pltpu.SemaphoreType.DMA((2,)),
                        pltpu.VMEM((idx_dim, pg * page), BF16)],
    )
    return pl.pallas_call(
        kern,
        grid_spec=grid_spec,
        out_shape=[jax.ShapeDtypeStruct((n, n_pages // pg, 1, pg * page), F32),
                   jax.ShapeDtypeStruct((n, 1, LANES), F32)],
        compiler_params=_cparams(2),
        name="sample_scores",
    )(page_table, qi3, wi3, ki_new3, cache_kidx_t)


def _sample_select_kernel(sc_ref, self_ref, sel4_ref, selself_ref, sel_ref, *, topk, past, rep):
    sc = sc_ref[...]
    sself = self_ref[:, 0:1]
    n = sc.shape[0]
    kf = jnp.float32(topk)
    row_min = jnp.minimum(jnp.min(sc, axis=1, keepdims=True), sself)
    row_max = jnp.maximum(jnp.max(sc, axis=1, keepdims=True), sself)
    n_adm = jnp.full((n, 1), past + 1, F32)

    def count_gt(t):
        return (jnp.sum(jnp.where(sc > t, 1.0, 0.0), axis=1, keepdims=True)
                + jnp.where(sself > t, 1.0, 0.0))

    def any_fn(mask):
        return jnp.max(jnp.where(mask, 1.0, 0.0)) > 0.0

    lo, hi, flo, fhi = _select_threshold(count_gt, row_min, row_max, n_adm, topk, any_fn)
    tie = flo != kf
    sel_ref[...] = jnp.where(sc > lo, 1.0, 0.0)
    selself_ref[...] = jnp.broadcast_to(jnp.where(sself > lo, 1.0, 0.0), selself_ref.shape)

    @pl.when(any_fn(tie))
    def _():
        need = kf - fhi
        blk = 512
        tri = (lax.broadcasted_iota(jnp.int32, (blk, blk), 0)
               < lax.broadcasted_iota(jnp.int32, (blk, blk), 1)).astype(BF16)
        before = jnp.zeros((n, 1), F32)
        for c in range(past // blk):
            s = sc_ref[:, c * blk:(c + 1) * blk]
            eq = s == hi
            rank = before + jnp.dot(jnp.where(eq, 1.0, 0.0).astype(BF16), tri, preferred_element_type=F32)
            keep = (s > hi) | (eq & (rank < need))
            sel_ref[:, c * blk:(c + 1) * blk] = jnp.where(tie, jnp.where(keep, 1.0, 0.0),
                                                          sel_ref[:, c * blk:(c + 1) * blk])
            before = before + jnp.sum(jnp.where(eq, 1.0, 0.0), axis=1, keepdims=True)
        keep_self = (sself > hi) | ((sself == hi) & (before < need))
        selself_ref[...] = jnp.broadcast_to(
            jnp.where(tie, jnp.where(keep_self, 1.0, 0.0), jnp.where(sself > lo, 1.0, 0.0)), selself_ref.shape)

    blk = 512
    row_lo = lax.broadcasted_iota(jnp.int32, (blk, blk * rep), 0) * rep
    col = lax.broadcasted_iota(jnp.int32, (blk, blk * rep), 1)
    spread = jnp.where((col >= row_lo) & (col < row_lo + rep), 1.0, 0.0).astype(BF16)
    for c in range(past // blk):
        sel4_ref[:, c * blk * rep:(c + 1) * blk * rep] = jnp.dot(
            sel_ref[:, c * blk:(c + 1) * blk].astype(BF16), spread, preferred_element_type=F32)


def sample_select(scores, sself, topk, rep):
    n, past = scores.shape
    kern = functools.partial(_sample_select_kernel, topk=topk, past=past, rep=rep)
    return pl.pallas_call(
        kern,
        out_shape=[jax.ShapeDtypeStruct((n, past * rep), F32), jax.ShapeDtypeStruct((n, LANES), F32)],
        scratch_shapes=[pltpu.VMEM((n, past), F32)],
        compiler_params=pltpu.CompilerParams(vmem_limit_bytes=VMEM_LIMIT),
        name="sample_select",
    )(scores, sself)


def _sample_attn_kernel(pt_ref, q_ref, kn_ref, vn_ref, sel_ref, selself_ref, bias_ref, rb0_ref, own_ref, *rest,
                        pg, n_heads, n_kv, hd):
    k_hbm, v_hbm, o_ref, kbuf_ref, vbuf_ref, sem, kcat_ref, vcat_ref, m_ref, l_ref, acc_ref = rest
    b = pl.program_id(0)
    p = pl.program_id(1)
    n_steps = pl.num_programs(1)
    scale = hd ** -0.5
    rows = k_hbm.shape[1]
    step = b * n_steps + p
    slot = step % 2

    def page_copies(bb, pp, sl):
        copies = []
        for i in range(pg):
            pid = pt_ref[bb, pp * pg + i]
            dst = pl.ds(i * rows, rows)
            copies.append(pltpu.make_async_copy(k_hbm.at[pid], kbuf_ref.at[sl, dst], sem.at[0, sl]))
            copies.append(pltpu.make_async_copy(v_hbm.at[pid], vbuf_ref.at[sl, dst], sem.at[1, sl]))
        return copies

    @pl.when(step == 0)
    def _():
        for c in page_copies(0, 0, 0):
            c.start()

    nxt = step + 1

    @pl.when(nxt < pl.num_programs(0) * n_steps)
    def _():
        for c in page_copies(nxt // n_steps, nxt % n_steps, 1 - slot):
            c.start()

    for c in page_copies(b, p, slot):
        c.wait()

    @pl.when(p == 0)
    def _():
        m_ref[...] = jnp.full(m_ref.shape, NEG, F32)
        l_ref[...] = jnp.zeros(l_ref.shape, F32)
        acc_ref[...] = jnp.zeros(acc_ref.shape, F32)

    for i in range(pg):
        kcat_ref[i * rows:(i + 1) * rows, :] = kbuf_ref[slot, pl.ds(i * rows, rows), :].astype(BF16)
        vcat_ref[i * rows:(i + 1) * rows, :] = vbuf_ref[slot, pl.ds(i * rows, rows), :].astype(BF16)
    qb = q_ref[...].astype(BF16)
    last = p == n_steps - 1
    bias = jnp.concatenate([bias_ref[0]] * (pg - 1) + [jnp.where(last, bias_ref[1], bias_ref[0])], axis=1)
    msk = (own_ref[...] > 0.0) & (sel_ref[...] > 0.0)
    lg = jnp.where(msk, _dot_nt(qb, kcat_ref[...]) * scale + bias, -jnp.inf)
    m_old = m_ref[...]
    m_new = jnp.maximum(m_old, jnp.max(lg, axis=-1, keepdims=True))
    alpha = jnp.exp(m_old - m_new)
    pr = jnp.exp(lg - m_new)
    l_new = alpha * l_ref[...] + jnp.sum(pr, axis=-1, keepdims=True)
    acc = alpha * acc_ref[...] + jnp.dot(pr.astype(BF16), vcat_ref[...], preferred_element_type=F32)
    m_ref[...] = m_new
    l_ref[...] = l_new
    acc_ref[...] = acc

    @pl.when(last)
    def _():
        kn = kn_ref[...].astype(BF16).astype(F32)
        vn = vn_ref[...].astype(BF16).astype(F32)
        ls = jnp.sum(qb.astype(F32) * kn, axis=-1, keepdims=True) * scale + rb0_ref[...]
        on = selself_ref[:, 0:1] > 0.0
        ls = jnp.where(on, ls, NEG)
        m_f = jnp.maximum(m_new, ls)
        a2 = jnp.exp(m_new - m_f)
        ps = jnp.where(on, jnp.exp(ls - m_f), 0.0)
        l_f = a2 * l_new + ps
        acc_f = a2 * acc + ps.astype(BF16).astype(F32) * vn
        o_ref[...] = acc_f / l_f


def sample_attention(page_table, q3, k_rep, v_rep, sel4, selself3, bias_s, rb0, cache_k, cache_v, pg, n_kv):
    n, n_pages = page_table.shape
    n_heads, hd = q3.shape[1:]
    rows = cache_k.shape[1]
    kern = functools.partial(_sample_attn_kernel, pg=pg, n_heads=n_heads, n_kv=n_kv, hd=hd)
    own = (np.arange(pg * rows)[None, :] % n_kv
           == np.arange(n_heads)[:, None] // (n_heads // n_kv)).astype(np.float32)
    hbm = pl.BlockSpec(memory_space=pl.ANY)
    per_seq = lambda b, p, pt: (b, 0, 0)
    grid_spec = pltpu.PrefetchScalarGridSpec(
        num_scalar_prefetch=1,
        grid=(n, n_pages // pg),
        in_specs=[pl.BlockSpec((None, n_heads, hd), per_seq),
                  pl.BlockSpec((None, n_heads, hd), per_seq),
                  pl.BlockSpec((None, n_heads, hd), per_seq),
                  pl.BlockSpec((None, None, 1, pg * rows), lambda b, p, pt: (b, p, 0, 0)),
                  pl.BlockSpec((None, 1, LANES), per_seq),
                  pl.BlockSpec((2, n_heads, rows), lambda b, p, pt: (0, 0, 0)),
                  pl.BlockSpec((n_heads, 1), lambda b, p, pt: (0, 0)),
                  pl.BlockSpec((n_heads, pg * rows), lambda b, p, pt: (0, 0)), hbm, hbm],
        out_specs=pl.BlockSpec((None, n_heads, hd), per_seq),
        scratch_shapes=[pltpu.VMEM((2, pg * rows, hd), F32), pltpu.VMEM((2, pg * rows, hd), F32),
                        pltpu.SemaphoreType.DMA((2, 2)),
                        pltpu.VMEM((pg * rows, hd), BF16), pltpu.VMEM((pg * rows, hd), BF16),
                        pltpu.VMEM((n_heads, 1), F32), pltpu.VMEM((n_heads, 1), F32),
                        pltpu.VMEM((n_heads, hd), F32)],
    )
    return pl.pallas_call(
        kern,
        grid_spec=grid_spec,
        out_shape=jax.ShapeDtypeStruct((n, n_heads, hd), F32),
        compiler_params=_cparams(2),
        name="sample_attn",
    )(page_table, q3, k_rep, v_rep, sel4, selself3, bias_s, rb0, jnp.asarray(own), cache_k, cache_v)


def _mix_kernel(ga_ref, gb_ref, co_ref, at_ref, o_ref):
    o_ref[...] = (ga_ref[...] * co_ref[...] + gb_ref[...] * at_ref[...]).astype(o_ref.dtype)


def gated_mix(gates, conv_out, attn):
    n, dm = conv_out.shape
    return pl.pallas_call(
        _mix_kernel,
        grid=(1,),
        in_specs=[pl.BlockSpec((n, dm), lambda i: (0, 0)), pl.BlockSpec((n, dm), lambda i: (0, 1)),
                  pl.BlockSpec((n, dm), lambda i: (0, 0)), pl.BlockSpec((n, dm), lambda i: (0, 0))],
        out_specs=pl.BlockSpec((n, dm), lambda i: (0, 0)),
        out_shape=jax.ShapeDtypeStruct((n, dm), BF16),
    )(gates, gates, conv_out, attn)


def _ffn_up_kernel(x_ref, xh_ref, wg_ref, wv_ref, dg_ref, dv_ref, bg_ref, bv_ref, pg_ref, pv_ref,
                   o_ref, sg_ref, sv_ref, eg_ref, ev_ref, wgb_ref, wvb_ref, *, tm, rs, width, tiles_per_seq):
    hist = width - 1
    pad = xh_ref.shape[0]
    first = pl.program_id(0) % tiles_per_seq == 0
    kc = 256
    for wf_ref, wb_ref in ((wg_ref, wgb_ref), (wv_ref, wvb_ref)):
        for k0 in range(0, wf_ref.shape[0], kc):
            wb_ref[k0:k0 + kc, :] = wf_ref[k0:k0 + kc, :].astype(BF16)
    branches = ((wgb_ref, dg_ref, bg_ref, pg_ref, eg_ref, sg_ref),
                (wvb_ref, dv_ref, bv_ref, pv_ref, ev_ref, sv_ref))
    for w_ref, _, _, prev_ref, e_ref, _ in branches:
        e_ref[pl.ds(0, pad), :] = jnp.dot(xh_ref[...], w_ref[...], preferred_element_type=F32)

        @pl.when(first)
        def _():
            e_ref[pl.ds(pad - hist, hist), :] = prev_ref[...]

    def project(r):
        for w_ref, _, _, _, e_ref, _ in branches:
            e_ref[pl.ds(pad + r * rs, rs), :] = jnp.dot(x_ref[pl.ds(r * rs, rs), :], w_ref[...],
                                                        preferred_element_type=F32)

    def activate(r):
        outs = []
        for _, d_ref, b_ref, _, e_ref, _ in branches:
            acc = b_ref[...] + d_ref[pl.ds(hist, 1), :] * e_ref[pl.ds(pad + r * rs, rs), :]
            for j in range(hist):
                acc = acc + d_ref[pl.ds(j, 1), :] * e_ref[pl.ds(pad + r * rs - hist + j, rs), :]
            outs.append(acc)
        g, v = outs
        o_ref[pl.ds(r * rs, rs), :] = (g * _sigmoid(g) * v).astype(o_ref.dtype)

    n_sub = tm // rs
    project(0)
    for r in range(1, n_sub):
        project(r)
        activate(r - 1)
    activate(n_sub - 1)
    for _, _, _, _, e_ref, s_ref in branches:
        s_ref[...] = e_ref[pl.ds(pad + tm - hist, hist), :]


def ffn_up_act(xn, w_up, prev, dw, bdw, n_seq, seq, tm, tn):
    m, k = xn.shape
    f2 = w_up.shape[1]
    f = f2 // 2
    width = dw.shape[0]
    ncb = f // tn
    pad = 16
    tps = seq // tm
    assert seq % tm == 0 and tm % pad == 0 and f % tn == 0 and width - 1 <= pad
    rs = min(256, tm // 2)
    assert tm % rs == 0 and rs % 8 == 0 and k % 256 == 0
    kern = functools.partial(_ffn_up_kernel, tm=tm, rs=rs, width=width, tiles_per_seq=tps)
    hb = tm // pad
    st_shape = jax.ShapeDtypeStruct((m // tm, width - 1, f), F32)
    act, sg, sv = pl.pallas_call(
        kern,
        grid=(m // tm, ncb),
        in_specs=[pl.BlockSpec((tm, k), lambda i, c: (i, 0)),
                  pl.BlockSpec((pad, k), lambda i, c: (jnp.maximum(i * hb - 1, 0), 0)),
                  pl.BlockSpec((k, tn), lambda i, c: (0, c)),
                  pl.BlockSpec((k, tn), lambda i, c: (0, c + ncb)),
                  pl.BlockSpec((width, tn), lambda i, c: (0, c)),
                  pl.BlockSpec((width, tn), lambda i, c: (0, c + ncb)),
                  pl.BlockSpec((1, tn), lambda i, c: (0, c)),
                  pl.BlockSpec((1, tn), lambda i, c: (0, c + ncb)),
                  pl.BlockSpec((None, width - 1, tn), lambda i, c: (i // tps, 0, c)),
                  pl.BlockSpec((None, width - 1, tn), lambda i, c: (i // tps, 0, c + ncb))],
        out_specs=[pl.BlockSpec((tm, tn), lambda i, c: (i, c)),
                   pl.BlockSpec((None, width - 1, tn), lambda i, c: (i, 0, c)),
                   pl.BlockSpec((None, width - 1, tn), lambda i, c: (i, 0, c))],
        out_shape=[jax.ShapeDtypeStruct((m, f), BF16), st_shape, st_shape],
        scratch_shapes=[pltpu.VMEM((pad + tm, tn), F32), pltpu.VMEM((pad + tm, tn), F32),
                        pltpu.VMEM((k, tn), BF16), pltpu.VMEM((k, tn), BF16)],
        compiler_params=_cparams(2),
        name="ffn_up_act",
    )(xn, xn, w_up, w_up, dw, dw, bdw.reshape(1, f2), bdw.reshape(1, f2), prev, prev)
    state = jnp.concatenate([sg, sv], axis=-1).reshape(n_seq, tps, width - 1, f2)[:, tps - 1]
    return act, state


def _ffn_act_step_kernel(ug_ref, uv_ref, pg_ref, pv_ref, wg_ref, wv_ref, bg_ref, bv_ref, o_ref, *, width):
    def conv(u_ref, prev_ref, w_ref, b_ref):
        acc = b_ref[...] + w_ref[pl.ds(width - 1, 1), :] * u_ref[...]
        for j in range(width - 1):
            acc = acc + w_ref[pl.ds(j, 1), :] * prev_ref[j]
        return acc
    g = conv(ug_ref, pg_ref, wg_ref, bg_ref)
    v = conv(uv_ref, pv_ref, wv_ref, bv_ref)
    o_ref[...] = (g * _sigmoid(g) * v).astype(o_ref.dtype)


def ffn_act_step(u, prev_t, dw, bdw, tc):
    n, f2 = u.shape
    f = f2 // 2
    width = dw.shape[0]
    ncb = f // tc
    kern = functools.partial(_ffn_act_step_kernel, width=width)
    return pl.pallas_call(
        kern,
        grid=(ncb,),
        in_specs=[pl.BlockSpec((n, tc), lambda c: (0, c)),
                  pl.BlockSpec((n, tc), lambda c: (0, c + ncb)),
                  pl.BlockSpec((width - 1, n, tc), lambda c: (0, 0, c)),
                  pl.BlockSpec((width - 1, n, tc), lambda c: (0, 0, c + ncb)),
                  pl.BlockSpec((width, tc), lambda c: (0, c)),
                  pl.BlockSpec((width, tc), lambda c: (0, c + ncb)),
                  pl.BlockSpec((1, tc), lambda c: (0, c)),
                  pl.BlockSpec((1, tc), lambda c: (0, c + ncb))],
        out_specs=pl.BlockSpec((n, tc), lambda c: (0, c)),
        out_shape=jax.ShapeDtypeStruct((n, f), BF16),
        compiler_params=_cparams(1),
    )(u, u, prev_t, prev_t, dw, dw, bdw.reshape(1, f2), bdw.reshape(1, f2))


def _row_tile(m, cap):
    tm = cap
    while m % tm:
        tm //= 2
    return tm if tm >= 16 else m


def _in_projection(xn, w_main, w_kw, w_g, sizes, tm):
    glu_w, q_w, k_w, v_w, qi_w, ki_w, wi_w, ga_w, gb_w = sizes
    offs = np.concatenate([[0], np.cumsum(sizes)])
    tn = 512
    t = dict(w_transposed=True)
    glu = matmul_w(xn, w_main, int(offs[0]), glu_w, tm, tn, name="mm_glu", **t)
    q = matmul_w(xn, w_main, int(offs[1]), q_w, tm, tn, name="mm_q", out_dtype=BF16, **t)
    k = matmul_w(xn, w_main, int(offs[2]), k_w, tm, min(tn, k_w), name="mm_k", **t)
    v = matmul_w(xn, w_main, int(offs[3]), v_w, tm, min(tn, v_w), name="mm_v", **t)
    qi = matmul_w(xn, w_main, int(offs[4]), qi_w, tm, min(tn, qi_w), name="mm_qi", out_dtype=BF16, **t)
    kw = matmul_w(xn, w_kw, 0, LANES, tm, LANES, name="mm_kw", **t)
    gates = matmul_w(xn, w_g, 0, ga_w + gb_w, tm, tn, name="mm_gates", sigmoid=True, **t)
    return glu, q, k, v, qi, kw, gates


def kernel(x_prompt, x_sample, cache_k, cache_v, cache_kidx, state_conv, state_ffn, page_table, rel_bias,
           norm_attn, w_in, dw_conv, b_dw_conv, ln_conv_g, ln_conv_b, w_conv_out, w_o, norm_ffn, w_up, dw_ffn,
           b_dw_ffn, w_down, norm_final):
    bsz, seq, dm = x_prompt.shape
    nd, dec_seq, _ = x_sample.shape
    depth, n_pool, page, n_kv, hd = cache_k.shape
    idx_dim = cache_kidx.shape[-1]
    n_pages = page_table.shape[1]
    past = n_pages * page
    width, dconv = dw_conv.shape[1:]
    fwidth = dw_ffn.shape[1]
    f = w_down.shape[1]
    n_heads = w_o.shape[1] // hd
    d_attn = n_heads * hd
    d_kv = n_kv * hd
    n_in = w_in.shape[2]
    idx_heads = (n_in - 2 * dconv - d_attn - 2 * d_kv - idx_dim - 2 * dm) // (idx_dim + 1)
    sizes = (2 * dconv, d_attn, d_kv, d_kv, idx_heads * idx_dim, idx_dim, idx_heads, dm, dm)
    assert sum(sizes) == n_in and depth == 1 and dec_seq == 1 and page == LANES and d_attn == dm

    mp = bsz * seq
    xp = x_prompt.reshape(mp, dm)
    xs = x_sample.reshape(nd, dm)
    bias3, bias_s = bias_tables(rel_bias, page, n_kv)
    tmp = _row_tile(mp, 2048)
    drop = lambda a: a.reshape(a.shape[1:])
    (norm_attn, w_in, dw_conv, b_dw_conv, ln_conv_g, ln_conv_b, w_conv_out, w_o, norm_ffn, w_up, dw_ffn,
     b_dw_ffn, w_down, state_conv, state_ffn) = map(drop, (
         norm_attn, w_in, dw_conv, b_dw_conv, ln_conv_g, ln_conv_b, w_conv_out, w_o, norm_ffn, w_up, dw_ffn,
         b_dw_ffn, w_down, state_conv, state_ffn))
    kidx_pool = jnp.swapaxes(cache_kidx.reshape(n_pool, page, idx_dim), 1, 2)
    k_pool = cache_k.reshape(n_pool, page * n_kv, hd)
    v_pool = cache_v.reshape(n_pool, page * n_kv, hd)

    n_aligned = sum(sizes[:5])
    n_small = idx_dim + idx_heads
    w_in_t = jnp.swapaxes(w_in, 0, 1)
    w_main = cast_rows_bf16(w_in_t, 0, n_aligned)
    w_kw = cast_rows_bf16(w_in_t, n_aligned, LANES, n_small)
    w_g = cast_rows_bf16(w_in_t, n_aligned + n_small, 2 * dm)
    w_o, w_down = cast_bf16(w_o), cast_bf16(w_down)

    xn = rmsnorm_rows(xp, norm_attn, BF16, 512)
    glu, q, k, v, qi, kw, gates = _in_projection(xn, w_main, w_kw, w_g, sizes, tmp)
    conv0 = jnp.zeros((bsz, width - 1, dconv), F32)
    conv_out, conv_state_p = conv_branch_prompt(glu, conv0, dw_conv, b_dw_conv, ln_conv_g, ln_conv_b,
                                                w_conv_out, bsz, seq, 256)
    mixed = attn_prompt(qi, kw, q, k, v, conv_out, gates, bias3, bsz, seq, n_kv, idx_dim, idx_heads)
    x2, xn2 = matmul_res_norm(mixed, w_o, xp, norm_ffn, _row_tile(mp, 512))
    ffn0 = jnp.zeros((bsz, fwidth - 1, 2 * f), F32)
    act, ffn_state_p = ffn_up_act(xn2, w_up, ffn0, dw_ffn, b_dw_ffn, bsz, seq, _row_tile(seq, 2048), 512)
    tk_down = f // 4 if f % (4 * LANES) == 0 else f
    y_prompt = matmul_res_norm_out(act, w_down, x2, norm_final, _row_tile(mp, 512), tk_down).reshape(bsz, seq, dm)

    xns = rmsnorm_rows(xs, norm_attn, BF16, nd)
    glu_s, q_s, k_s, v_s, qi_s, kw_s, gates_s = _in_projection(xns, w_main, w_kw, w_g, sizes, nd)
    ki_s = kw_s[:, :idx_dim]
    wi_s = kw_s[:, idx_dim:idx_dim + idx_heads]
    sc_prev_t = jnp.swapaxes(state_conv, 0, 1)
    conv_out_s, u_conv_s = conv_branch_step(glu_s, sc_prev_t, dw_conv, b_dw_conv, ln_conv_g, ln_conv_b,
                                            w_conv_out)
    conv_state_s = jnp.concatenate([state_conv[:, 1:], u_conv_s[:, None, :]], axis=1)

    pg = 16 if n_pages % 16 == 0 else 8
    pg_idx = 32 if n_pages % 32 == 0 else pg
    scores3, sself3 = sample_scores(page_table, qi_s.reshape(nd, idx_heads, idx_dim),
                                    wi_s.reshape(nd, idx_heads, 1), ki_s.reshape(nd, 1, idx_dim), kidx_pool,
                                    pg_idx)
    topk_s = min(TOPK_MAX, (past + dec_seq) // 4)
    sel4, selself = sample_select(scores3.reshape(nd, past), sself3.reshape(nd, LANES), topk_s, n_kv)
    group = n_heads // n_kv
    k_rep = jnp.repeat(k_s.reshape(nd, n_kv, hd), group, axis=1)
    v_rep = jnp.repeat(v_s.reshape(nd, n_kv, hd), group, axis=1)
    attn_s = sample_attention(page_table, q_s.reshape(nd, n_heads, hd), k_rep, v_rep,
                              sel4.reshape(nd, n_pages // pg, 1, pg * page * n_kv), selself.reshape(nd, 1, LANES),
                              bias_s,
                              rel_bias[0].reshape(n_heads, 1), k_pool, v_pool, pg, n_kv)
    mixed_s = gated_mix(gates_s, conv_out_s, attn_s.reshape(nd, dm))
    x2s, xn2s = matmul_res_norm(mixed_s, w_o, xs, norm_ffn, nd)
    u_s = matmul_w(xn2s, w_up, 0, 2 * f, nd, 512)
    sf_prev_t = jnp.swapaxes(state_ffn, 0, 1)
    act_s = ffn_act_step(u_s, sf_prev_t, dw_ffn, b_dw_ffn, 512)
    y_sample = matmul_res_norm_out(act_s, w_down, x2s, norm_final, nd, tk_down).reshape(nd, dec_seq, dm)
    ffn_state_s = jnp.concatenate([state_ffn[:, 1:], u_s[:, None, :]], axis=1)

    return (y_prompt, y_sample,
            k.reshape(1, bsz, seq, n_kv, hd), v.reshape(1, bsz, seq, n_kv, hd),
            kw.reshape(bsz, seq, LANES)[None, :, :, :idx_dim],
            conv_state_p[None], ffn_state_p[None],
            k_s.reshape(1, nd, dec_seq, n_kv, hd), v_s.reshape(1, nd, dec_seq, n_kv, hd),
            ki_s.reshape(1, nd, dec_seq, idx_dim),
            conv_state_s[None], ffn_state_s[None])
```

```python
import functools
import math

import numpy as np
import jax
import jax.numpy as jnp
from jax import lax
from jax.experimental import pallas as pl
from jax.experimental.pallas import tpu as pltpu

F32 = jnp.float32
BF16 = jnp.bfloat16

EPS = 1e-6
TOPK_MAX = 256
N_BUCKETS = 32
MAX_DISTANCE = 128
QB = 128
CK = 256
LANES = 128
NEG = -1e30
LOG2E = math.log2(math.e)
VMEM_LIMIT = 56 * 1024 * 1024


def _cparams(n_axes, vmem=VMEM_LIMIT):
    return pltpu.CompilerParams(dimension_semantics=("arbitrary",) * n_axes, vmem_limit_bytes=vmem)


def _dot_nt(a, b):
    return lax.dot_general(a, b, (((1,), (1,)), ((), ())), preferred_element_type=F32)


def _sigmoid(x):
    return 1.0 / (1.0 + jnp.exp(-x))


def _fold_rows(x, op):
    while x.shape[0] > 8:
        half = x.shape[0] // 2
        x = op(x[:half], x[half:])
    return x


def _rel_bucket_np(dist):
    n = np.maximum(dist, 0)
    max_exact = N_BUCKETS // 2
    nf = np.maximum(n, 1).astype(np.float32)
    large = max_exact + (np.log(nf / np.float32(max_exact)) / np.float32(math.log(MAX_DISTANCE / max_exact))
                         * np.float32(N_BUCKETS - max_exact)).astype(np.int32)
    large = np.minimum(large, N_BUCKETS - 1)
    return np.where(n < max_exact, n, large).astype(np.int32)


def _rms_kernel(x_ref, g_ref, o_ref):
    x = x_ref[...]
    y = x * lax.rsqrt(jnp.mean(x * x, axis=-1, keepdims=True) + EPS) * g_ref[...]
    o_ref[...] = y.astype(o_ref.dtype)


def rmsnorm_rows(x, g, out_dtype, tm):
    m, d = x.shape
    return pl.pallas_call(
        _rms_kernel,
        grid=(m // tm,),
        in_specs=[pl.BlockSpec((tm, d), lambda i: (i, 0)), pl.BlockSpec((1, d), lambda i: (0, 0))],
        out_specs=pl.BlockSpec((tm, d), lambda i: (i, 0)),
        out_shape=jax.ShapeDtypeStruct((m, d), out_dtype),
        compiler_params=_cparams(1),
        name="rmsnorm",
    )(x, g.reshape(1, d))


def _cast_kernel(w_ref, o_ref):
    o_ref[...] = w_ref[...].astype(o_ref.dtype)


def cast_bf16(w, ncols=None):
    k, n = w.shape
    ncols = n if ncols is None else ncols
    tk = 512 if k % 512 == 0 else k
    tn = 1024 if ncols % 1024 == 0 else (512 if ncols % 512 == 0 else ncols)
    return pl.pallas_call(
        _cast_kernel,
        grid=(k // tk, ncols // tn),
        in_specs=[pl.BlockSpec((tk, tn), lambda i, j: (i, j))],
        out_specs=pl.BlockSpec((tk, tn), lambda i, j: (i, j)),
        out_shape=jax.ShapeDtypeStruct((k, ncols), BF16),
        compiler_params=_cparams(2),
        name="cast_bf16",
    )(w)


def _cast_rows_kernel(w_ref, o_ref, *, valid):
    w = w_ref[...]
    if valid < w.shape[0]:
        w = jnp.where(lax.broadcasted_iota(jnp.int32, w.shape, 0) < valid, w, 0.0)
    o_ref[...] = w.astype(o_ref.dtype)


def cast_rows_bf16(wt, row0, nrows, nvalid=None):
    n, k = wt.shape
    tr = 512 if nrows % 512 == 0 else nrows
    nvalid = nrows if nvalid is None else nvalid
    assert row0 % 8 == 0 and (nvalid == nrows or tr == nrows) and row0 + nrows <= n
    return pl.pallas_call(
        functools.partial(_cast_rows_kernel, valid=nvalid),
        grid=(nrows // tr,),
        in_specs=[pl.BlockSpec((pl.Element(tr), pl.Element(k)), lambda i: (pl.multiple_of(row0 + i * tr, 8), 0))],
        out_specs=pl.BlockSpec((tr, k), lambda i: (i, 0)),
        out_shape=jax.ShapeDtypeStruct((nrows, k), BF16),
        compiler_params=_cparams(1),
        name="cast_rows_bf16",
    )(wt)


def _mm_kernel(a_ref, w_ref, o_ref):
    o_ref[...] = jnp.dot(a_ref[...], w_ref[...].astype(BF16), preferred_element_type=F32).astype(o_ref.dtype)


def _mm_nt_kernel(a_ref, wt_ref, o_ref, *, sigmoid):
    y = _dot_nt(a_ref[...], wt_ref[...])
    o_ref[...] = (_sigmoid(y) if sigmoid else y).astype(o_ref.dtype)


def matmul_w(a, w, col0, ncols, tm, tn, name="matmul", out_dtype=F32, w_transposed=False, sigmoid=False):
    m, k = a.shape
    assert col0 % tn == 0 and ncols % tn == 0 and m % tm == 0 and (w_transposed or not sigmoid)
    cb = col0 // tn
    if w_transposed:
        w_spec = pl.BlockSpec((tn, k), lambda i, j: (j + cb, 0))
    else:
        w_spec = pl.BlockSpec((k, tn), lambda i, j: (0, j + cb))
    return pl.pallas_call(
        functools.partial(_mm_nt_kernel, sigmoid=sigmoid) if w_transposed else _mm_kernel,
        grid=(m // tm, ncols // tn),
        in_specs=[pl.BlockSpec((tm, k), lambda i, j: (i, 0)), w_spec],
        out_specs=pl.BlockSpec((tm, tn), lambda i, j: (i, j)),
        out_shape=jax.ShapeDtypeStruct((m, ncols), out_dtype),
        compiler_params=_cparams(2),
        name=name,
    )(a, w)


def _rms(x, g):
    return x * lax.rsqrt(jnp.mean(x * x, axis=-1, keepdims=True) + EPS) * g


def _mm_res_norm_kernel(a_ref, w_ref, r_ref, g_ref, o_ref, on_ref):
    x = r_ref[...] + jnp.dot(a_ref[...], w_ref[...], preferred_element_type=F32)
    o_ref[...] = x
    on_ref[...] = _rms(x, g_ref[...]).astype(on_ref.dtype)


def matmul_res_norm(a, w, res, g, tm):
    m, k = a.shape
    n = w.shape[1]
    return pl.pallas_call(
        _mm_res_norm_kernel,
        grid=(m // tm,),
        in_specs=[pl.BlockSpec((tm, k), lambda i: (i, 0)), pl.BlockSpec((k, n), lambda i: (0, 0)),
                  pl.BlockSpec((tm, n), lambda i: (i, 0)), pl.BlockSpec((1, n), lambda i: (0, 0))],
        out_specs=[pl.BlockSpec((tm, n), lambda i: (i, 0)), pl.BlockSpec((tm, n), lambda i: (i, 0))],
        out_shape=[jax.ShapeDtypeStruct((m, n), F32), jax.ShapeDtypeStruct((m, n), BF16)],
        compiler_params=_cparams(1),
        name="mm_o_norm",
    )(a, w, res, g.reshape(1, n))


def _mm_ksplit_norm_kernel(a_ref, w_ref, r_ref, g_ref, o_ref, acc_ref):
    kk = pl.program_id(1)

    @pl.when(kk == 0)
    def _():
        acc_ref[...] = r_ref[...]
    acc_ref[...] += jnp.dot(a_ref[...], w_ref[...], preferred_element_type=F32)

    @pl.when(kk == pl.num_programs(1) - 1)
    def _():
        o_ref[...] = _rms(acc_ref[...], g_ref[...])


def matmul_res_norm_out(a, w, res, g, tm, tk):
    m, k = a.shape
    n = w.shape[1]
    assert k % tk == 0 and m % tm == 0
    return pl.pallas_call(
        _mm_ksplit_norm_kernel,
        grid=(m // tm, k // tk),
        in_specs=[pl.BlockSpec((tm, tk), lambda i, kk: (i, kk)), pl.BlockSpec((tk, n), lambda i, kk: (kk, 0)),
                  pl.BlockSpec((tm, n), lambda i, kk: (i, 0)), pl.BlockSpec((1, n), lambda i, kk: (0, 0))],
        out_specs=pl.BlockSpec((tm, n), lambda i, kk: (i, 0)),
        out_shape=jax.ShapeDtypeStruct((m, n), F32),
        scratch_shapes=[pltpu.VMEM((tm, n), F32)],
        compiler_params=_cparams(2),
        name="mm_down_norm",
    )(a, w, res, g.reshape(1, n))


def _bias_kernel(rb_ref, bk3_ref, bks_ref, o3_ref, os_ref, *, n_heads):
    def head(h, carry):
        for t in range(3):
            bk = bk3_ref[t]
            acc = jnp.zeros(bk.shape, F32)
            for b in range(N_BUCKETS):
                acc = jnp.where(bk == b, rb_ref[b, h], acc)
            o3_ref[t, h] = acc * LOG2E
        for t in range(2):
            bk = bks_ref[t]
            acc = jnp.zeros(bk.shape, F32)
            for b in range(N_BUCKETS):
                acc = jnp.where(bk == b, rb_ref[b, h], acc)
            os_ref[t, pl.ds(h, 1), :] = acc
        return carry
    lax.fori_loop(0, n_heads, head, 0)


def bias_tables(rel_bias, page, rep):
    n_heads = rel_bias.shape[1]
    cols = page * rep
    i = np.arange(QB)[None, :]
    k = np.arange(QB)[:, None]
    bk3 = np.stack([_rel_bucket_np(i - k + 2 * QB), _rel_bucket_np(i - k + QB), _rel_bucket_np(i - k)])
    assert (_rel_bucket_np(np.arange(QB + 1, 1 << 20)) == N_BUCKETS - 1).all()
    assert (bk3[0] == N_BUCKETS - 1).all()
    assert page >= QB
    bks = np.stack([np.full((1, cols), N_BUCKETS - 1, np.int32),
                    _rel_bucket_np(page - np.arange(cols) // rep)[None, :]])
    return pl.pallas_call(
        functools.partial(_bias_kernel, n_heads=n_heads),
        in_specs=[pl.BlockSpec(memory_space=pltpu.SMEM),
                  pl.BlockSpec(memory_space=pltpu.VMEM), pl.BlockSpec(memory_space=pltpu.VMEM)],
        out_specs=[pl.BlockSpec(memory_space=pltpu.VMEM), pl.BlockSpec(memory_space=pltpu.VMEM)],
        out_shape=[jax.ShapeDtypeStruct((3, n_heads, QB, QB), F32),
                   jax.ShapeDtypeStruct((2, n_heads, cols), F32)],
    )(rel_bias, jnp.asarray(bk3), jnp.asarray(bks))


def _conv_kernel(glu_ref, prev_ref, dw_ref, bdw_ref, lng_ref, lnb_ref, wout_ref, o_ref, st_ref,
                 ext_ref, h_ref, wbf_ref, sh_ref, *, tt, width, dconv):
    b = pl.program_id(0)
    t = pl.program_id(1)
    pad = 32
    hist = width - 1

    @pl.when((b == 0) & (t == 0))
    def _():
        wbf_ref[...] = wout_ref[...].astype(BF16)

    @pl.when(t == 0)
    def _():
        ext_ref[pl.ds(pad - hist, hist), :] = prev_ref[...]

    @pl.when(t > 0)
    def _():
        ext_ref[pl.ds(0, pad), :] = ext_ref[pl.ds(tt, pad), :]

    glu = glu_ref[...]
    u = glu[:, :dconv] * _sigmoid(glu[:, dconv:])
    ext_ref[pl.ds(pad, tt), :] = u
    st_ref[...] = ext_ref[pl.ds(pad + tt - hist, hist), :]

    span = tt + pad - 8
    for s in range(1, 8):
        sh_ref[s - 1] = ext_ref[pl.ds(s, span), :]

    for c in range(dconv // LANES):
        cs = slice(c * LANES, (c + 1) * LANES)
        acc = jnp.zeros((tt, LANES), F32) + bdw_ref[:, cs]
        for j in range(width):
            a, s = divmod(pad - hist + j, 8)
            rows = ext_ref[pl.ds(8 * a, tt), cs] if s == 0 else sh_ref[s - 1, pl.ds(8 * a, tt), cs]
            acc = acc + dw_ref[pl.ds(j, 1), cs] * rows
        h_ref[:, cs] = acc

    h = h_ref[...]
    mu = jnp.mean(h, axis=-1, keepdims=True)
    var = jnp.mean(jnp.square(h - mu), axis=-1, keepdims=True)
    y = (h - mu) * lax.rsqrt(var + EPS) * lng_ref[...] + lnb_ref[...]
    y = y * _sigmoid(y)
    o_ref[...] = jnp.dot(y.astype(BF16), wbf_ref[...], preferred_element_type=F32)


def conv_branch_prompt(glu_pre, prev, dw, bdw, lng, lnb, wout, n_seq, seq, tt):
    width, dconv = dw.shape
    dm = wout.shape[1]
    nt = seq // tt
    kern = functools.partial(_conv_kernel, tt=tt, width=width, dconv=dconv)
    return pl.pallas_call(
        kern,
        grid=(n_seq, nt),
        in_specs=[pl.BlockSpec((tt, 2 * dconv), lambda b, t: (b * nt + t, 0)),
                  pl.BlockSpec((None, width - 1, dconv), lambda b, t: (b, 0, 0)),
                  pl.BlockSpec((width, dconv), lambda b, t: (0, 0)),
                  pl.BlockSpec((1, dconv), lambda b, t: (0, 0)),
                  pl.BlockSpec((1, dconv), lambda b, t: (0, 0)),
                  pl.BlockSpec((1, dconv), lambda b, t: (0, 0)),
                  pl.BlockSpec((dconv, dm), lambda b, t: (0, 0))],
        out_specs=[pl.BlockSpec((tt, dm), lambda b, t: (b * nt + t, 0)),
                   pl.BlockSpec((None, width - 1, dconv), lambda b, t: (b, 0, 0))],
        out_shape=[jax.ShapeDtypeStruct((n_seq * seq, dm), F32),
                   jax.ShapeDtypeStruct((n_seq, width - 1, dconv), F32)],
        scratch_shapes=[pltpu.VMEM((32 + tt, dconv), F32), pltpu.VMEM((tt, dconv), F32),
                        pltpu.VMEM((dconv, dm), BF16), pltpu.VMEM((7, 24 + tt, dconv), F32)],
        compiler_params=_cparams(2),
        name="conv_branch",
    )(glu_pre, prev, dw, bdw.reshape(1, dconv), lng.reshape(1, dconv), lnb.reshape(1, dconv), wout)


def _conv_step_kernel(glu_ref, prev_ref, dw_ref, bdw_ref, lng_ref, lnb_ref, wout_ref, o_ref, u_ref,
                      *, width, dconv):
    glu = glu_ref[...]
    u = glu[:, :dconv] * _sigmoid(glu[:, dconv:])
    u_ref[...] = u
    h = bdw_ref[...] + dw_ref[pl.ds(width - 1, 1), :] * u
    for j in range(width - 1):
        h = h + dw_ref[pl.ds(j, 1), :] * prev_ref[j]
    mu = jnp.mean(h, axis=-1, keepdims=True)
    var = jnp.mean(jnp.square(h - mu), axis=-1, keepdims=True)
    y = (h - mu) * lax.rsqrt(var + EPS) * lng_ref[...] + lnb_ref[...]
    y = y * _sigmoid(y)
    o_ref[...] = jnp.dot(y.astype(BF16), wout_ref[...].astype(BF16), preferred_element_type=F32)


def conv_branch_step(glu_pre, prev_t, dw, bdw, lng, lnb, wout):
    width, dconv = dw.shape
    n = glu_pre.shape[0]
    dm = wout.shape[1]
    kern = functools.partial(_conv_step_kernel, width=width, dconv=dconv)
    return pl.pallas_call(
        kern,
        out_shape=[jax.ShapeDtypeStruct((n, dm), F32), jax.ShapeDtypeStruct((n, dconv), F32)],
        compiler_params=pltpu.CompilerParams(vmem_limit_bytes=VMEM_LIMIT),
    )(glu_pre, prev_t, dw, bdw.reshape(1, dconv), lng.reshape(1, dconv), lnb.reshape(1, dconv), wout)


def _select_threshold(count_gt, row_min, row_max, n_adm, topk, any_fn):
    kf = jnp.float32(topk)
    full = n_adm <= kf
    lo0 = row_min - (1.0 + jnp.abs(row_min))
    hi0 = row_max
    flo0 = jnp.where(full, kf, n_adm)
    fhi0 = jnp.zeros_like(lo0)

    def active_rows(lo, hi, flo):
        mid = 0.5 * lo + 0.5 * hi
        return (flo != kf) & (lo < mid) & (mid < hi)

    def cond(st):
        lo, hi, flo, fhi = st
        return any_fn(active_rows(lo, hi, flo))

    def step(st):
        lo, hi, flo, fhi = st
        act = active_rows(lo, hi, flo)
        mid = 0.5 * lo + 0.5 * hi
        c = count_gt(mid)
        up = act & (c >= kf)
        dn = act & (c < kf)
        return (jnp.where(up, mid, lo), jnp.where(dn, mid, hi),
                jnp.where(up, c, flo), jnp.where(dn, c, fhi))

    def body(st):
        for _ in range(4):
            st = step(st)
        return st

    lo, hi, flo, fhi = lax.while_loop(cond, body, (lo0, hi0, flo0, fhi0))
    lo = jnp.where(full, -jnp.inf, lo)
    return lo, hi, flo, fhi


def _attn_prompt_kernel(qi_ref, wi_ref, kw_ref, q_ref, k_ref, v_ref, co_ref, ga_ref, gb_ref, bias_ref,
                        o_ref, kd_ref, kb_ref, vt_ref, wit_ref, qib_ref, sc_ref, sel_ref, qs_ref, *state,
                        seq, n_heads, n_kv, idx_heads, idx_dim, topk, hd):
    m_refs, l_refs, acc_refs = state[:n_kv], state[n_kv:2 * n_kv], state[2 * n_kv:3 * n_kv]
    st_refs = state[3 * n_kv:]
    j = pl.program_id(1)
    group = n_heads // n_kv
    nck = (j * QB + QB + CK - 1) // CK
    kf = jnp.float32(topk)

    @pl.when(j == 0)
    def _():
        kd_ref[...] = kw_ref[:, :idx_dim].astype(BF16)
        for g in range(n_kv):
            kb_ref[g] = k_ref[:, g * hd:(g + 1) * hd].astype(BF16)
            for c in range(seq // CK):
                vt_ref[g, c] = v_ref[c * CK:(c + 1) * CK, g * hd:(g + 1) * hd].T.astype(BF16)

    wit_ref[...] = (wi_ref[...] * (idx_heads ** -0.5)).T
    qi = qi_ref[...].astype(F32) * (idx_dim ** -0.5)
    for h in range(idx_heads):
        qib_ref[h] = qi[:, h * idx_dim:(h + 1) * idx_dim].astype(BF16)
    qpos = j * QB + lax.broadcasted_iota(jnp.int32, (CK, QB), 1)
    krow = lax.broadcasted_iota(jnp.int32, (CK, QB), 0)
    per_dot = 2 * LANES // QB

    def score_chunk(c, carry):
        mn, mx = carry
        k0 = pl.multiple_of(c * CK, CK)
        kc = kd_ref[pl.ds(k0, CK), :]
        acc = jnp.zeros((CK, QB), F32)
        for p in range(idx_heads // per_dot):
            rhs = qib_ref[pl.ds(p * per_dot, per_dot)].reshape(per_dot * QB, idx_dim)
            s = _dot_nt(kc, rhs)
            for r in range(per_dot):
                acc = acc + (wit_ref[pl.ds(idx_dim + p * per_dot + r, 1), :]
                             * jnp.maximum(s[:, r * QB:(r + 1) * QB], 0.0))
        adm = (krow + c * CK) <= qpos
        sc_ref[c] = jnp.where(adm, acc, -jnp.inf)
        mn = jnp.minimum(mn, _fold_rows(jnp.where(adm, acc, jnp.inf), jnp.minimum))
        mx = jnp.maximum(mx, _fold_rows(jnp.where(adm, acc, -jnp.inf), jnp.maximum))
        return mn, mx

    mn8, mx8 = lax.fori_loop(0, nck, score_chunk,
                             (jnp.full((8, QB), jnp.inf, F32), jnp.full((8, QB), -jnp.inf, F32)))
    row_min = jnp.min(mn8, axis=0, keepdims=True)
    row_max = jnp.max(mx8, axis=0, keepdims=True)
    n_adm = (j * QB + 1 + lax.broadcasted_iota(jnp.int32, (1, QB), 1)).astype(F32)

    def count_gt(t):
        def cbody(c, acc):
            return acc + _fold_rows(jnp.where(sc_ref[c] > t, 1.0, 0.0), jnp.add)
        part = lax.fori_loop(0, nck, cbody, jnp.zeros((8, QB), F32))
        return jnp.sum(part, axis=0, keepdims=True)

    def any_fn(mask):
        return jnp.max(jnp.where(mask, 1.0, 0.0)) > 0.0

    lo, hi, flo, fhi = _select_threshold(count_gt, row_min, row_max, n_adm, topk, any_fn)
    tie = flo != kf

    def sel_chunk(c, carry):
        sel_ref[c] = jnp.where(sc_ref[c] > lo, 1.0, 0.0)
        return carry
    lax.fori_loop(0, nck, sel_chunk, 0)

    @pl.when(any_fn(tie))
    def _():
        need = kf - fhi
        lower = (lax.broadcasted_iota(jnp.int32, (CK, CK), 1)
                 < lax.broadcasted_iota(jnp.int32, (CK, CK), 0)).astype(BF16)

        def tie_chunk(c, before):
            s = sc_ref[c]
            eq = s == hi
            eqf = jnp.where(eq, 1.0, 0.0)
            rank = before + jnp.dot(lower, eqf.astype(BF16), preferred_element_type=F32)
            keep = (s > hi) | (eq & (rank < need))
            sel_ref[c] = jnp.where(tie, jnp.where(keep, 1.0, 0.0), sel_ref[c])
            return before + jnp.sum(eqf, axis=0, keepdims=True)
        lax.fori_loop(0, nck, tie_chunk, jnp.zeros((1, QB), F32))

    scale2 = hd ** -0.5 * LOG2E
    for h in range(n_heads):
        qs_ref[h] = q_ref[:, h * hd:(h + 1) * hd].astype(BF16)
    for g in range(n_kv):
        m_refs[g][...] = jnp.full(m_refs[g].shape, NEG, F32)
        l_refs[g][...] = jnp.zeros(l_refs[g].shape, F32)
        acc_refs[g][...] = jnp.zeros(acc_refs[g].shape, F32)

    def logits(g, c):
        kc = kb_ref[g, pl.ds(pl.multiple_of(c * CK, CK), CK), :]
        qg = qs_ref[pl.ds(g * group, group)].reshape(group * QB, hd)
        st_refs[g % 2][...] = _dot_nt(kc, qg)

    logits(0, 0)

    def att_chunk(c, carry):
        msk = jnp.concatenate([sel_ref[c]] * group, axis=1) > 0.0
        tis = [jnp.clip(c * (CK // QB) + s - j + 2, 0, 2) for s in range(CK // QB)]
        for g in range(n_kv):
            if g + 1 < n_kv:
                logits(g + 1, c)
            else:
                logits(0, jnp.minimum(c + 1, nck - 1))
            rows = [jnp.concatenate([bias_ref[ti, g * group + hh] for hh in range(group)], axis=1)
                    for ti in tis]
            lg = jnp.where(msk, st_refs[g % 2][...] * scale2 + jnp.concatenate(rows, axis=0), -jnp.inf)
            m_old = m_refs[g][...]
            m_new = jnp.maximum(m_old, jnp.max(_fold_rows(lg, jnp.maximum), axis=0, keepdims=True))
            alpha = jnp.exp2(m_old - m_new)
            p = jnp.exp2(lg - m_new)
            l_refs[g][...] = alpha * l_refs[g][...] + jnp.sum(_fold_rows(p, jnp.add), axis=0, keepdims=True)
            pv = jnp.dot(vt_ref[g, c], p.astype(BF16), preferred_element_type=F32)
            acc_refs[g][...] = alpha * acc_refs[g][...] + pv
            m_refs[g][...] = m_new
        return carry
    lax.fori_loop(0, nck, att_chunk, 0)

    for g in range(n_kv):
        ot = acc_refs[g][...] / l_refs[g][...]
        for hh in range(group):
            cs = slice((g * group + hh) * hd, (g * group + hh + 1) * hd)
            o = ot[:, hh * QB:(hh + 1) * QB].T
            mixed = ga_ref[:, cs] * co_ref[:, cs] + gb_ref[:, cs] * o
            o_ref[:, cs] = mixed.astype(o_ref.dtype)


def attn_prompt(qi, kw, q, k, v, conv_out, gates, bias3, n_seq, seq, n_kv, idx_dim, idx_heads):
    m, dm = q.shape
    hd = k.shape[1] // n_kv
    n_heads = dm // hd
    group = n_heads // n_kv
    nb = seq // QB
    topk = min(TOPK_MAX, seq // 4)
    assert seq % CK == 0 and LANES % idx_dim == 0 and kw.shape[1] == LANES
    kern = functools.partial(_attn_prompt_kernel, seq=seq, n_heads=n_heads, n_kv=n_kv, idx_heads=idx_heads,
                             idx_dim=idx_dim, topk=topk, hd=hd)
    row = lambda b, j: (b * nb + j, 0)
    return pl.pallas_call(
        kern,
        grid=(n_seq, nb),
        in_specs=[pl.BlockSpec((QB, idx_heads * idx_dim), row),
                  pl.BlockSpec((QB, LANES), row),
                  pl.BlockSpec((seq, LANES), lambda b, j: (b, 0)),
                  pl.BlockSpec((QB, dm), row),
                  pl.BlockSpec((seq, n_kv * hd), lambda b, j: (b, 0)),
                  pl.BlockSpec((seq, n_kv * hd), lambda b, j: (b, 0)),
                  pl.BlockSpec((QB, dm), row),
                  pl.BlockSpec((QB, dm), lambda b, j: (b * nb + j, 0)),
                  pl.BlockSpec((QB, dm), lambda b, j: (b * nb + j, 1)),
                  pl.BlockSpec((3, n_heads, QB, QB), lambda b, j: (0, 0, 0, 0))],
        out_specs=pl.BlockSpec((QB, dm), row),
        out_shape=jax.ShapeDtypeStruct((m, dm), BF16),
        scratch_shapes=[pltpu.VMEM((seq, idx_dim), BF16),
                        pltpu.VMEM((n_kv, seq, hd), BF16),
                        pltpu.VMEM((n_kv, seq // CK, hd, CK), BF16),
                        pltpu.VMEM((LANES, QB), F32),
                        pltpu.VMEM((idx_heads, QB, idx_dim), BF16),
                        pltpu.VMEM((seq // CK, CK, QB), F32),
                        pltpu.VMEM((seq // CK, CK, QB), F32),
                        pltpu.VMEM((n_heads, QB, hd), BF16)]
                       + [pltpu.VMEM((1, group * QB), F32)] * (2 * n_kv)
                       + [pltpu.VMEM((hd, group * QB), F32)] * n_kv
                       + [pltpu.VMEM((CK, group * QB), F32)] * 2,
        compiler_params=_cparams(2),
        name="attn_prompt",
    )(qi, kw, kw, q, k, v, conv_out, gates, gates, bias3)


def _sample_score_kernel(pt_ref, qi_ref, wi_ref, kn_ref, *rest, pg, idx_heads, idx_dim):
    k_hbm, o_ref, self_ref, kbuf_ref, sem, kcat_ref = rest
    page = k_hbm.shape[2]
    b = pl.program_id(0)
    p = pl.program_id(1)
    n_steps = pl.num_programs(1)
    step = b * n_steps + p
    slot = step % 2

    def page_copies(bb, pp, sl):
        return [pltpu.make_async_copy(k_hbm.at[pt_ref[bb, pp * pg + i]], kbuf_ref.at[sl, i], sem.at[sl])
                for i in range(pg)]

    @pl.when(step == 0)
    def _():
        for c in page_copies(0, 0, 0):
            c.start()

    nxt = step + 1

    @pl.when(nxt < pl.num_programs(0) * n_steps)
    def _():
        for c in page_copies(nxt // n_steps, nxt % n_steps, 1 - slot):
            c.start()

    for c in page_copies(b, p, slot):
        c.wait()

    qi = qi_ref[...] * (idx_dim ** -0.5)
    wi = wi_ref[...] * (idx_heads ** -0.5)
    qb = qi.astype(BF16)
    for i in range(pg):
        kcat_ref[:, i * page:(i + 1) * page] = kbuf_ref[slot, i].astype(BF16)
    s = jnp.dot(qb, kcat_ref[...], preferred_element_type=F32)
    o_ref[...] = jnp.sum(wi * jnp.maximum(s, 0.0), axis=0, keepdims=True)

    @pl.when(pl.program_id(1) == 0)
    def _():
        kn = kn_ref[...].astype(BF16).astype(F32)
        s = jnp.sum(qb.astype(F32) * kn, axis=1, keepdims=True)
        sself = jnp.sum(wi * jnp.maximum(s, 0.0), axis=0, keepdims=True)
        self_ref[...] = jnp.broadcast_to(sself, self_ref.shape)


def sample_scores(page_table, qi3, wi3, ki_new3, cache_kidx_t, pg):
    n, n_pages = page_table.shape
    idx_heads, idx_dim = qi3.shape[1:]
    page = cache_kidx_t.shape[2]
    kern = functools.partial(_sample_score_kernel, pg=pg, idx_heads=idx_heads, idx_dim=idx_dim)
    grid_spec = pltpu.PrefetchScalarGridSpec(
        num_scalar_prefetch=1,
        grid=(n, n_pages // pg),
        in_specs=[pl.BlockSpec((None, idx_heads, idx_dim), lambda b, p, pt: (b, 0, 0)),
                  pl.BlockSpec((None, idx_heads, 1), lambda b, p, pt: (b, 0, 0)),
                  pl.BlockSpec((None, 1, idx_dim), lambda b, p, pt: (b, 0, 0)),
                  pl.BlockSpec(memory_space=pl.ANY)],
        out_specs=[pl.BlockSpec((None, None, 1, pg * page), lambda b, p, pt: (b, p, 0, 0)),
                   pl.BlockSpec((None, 1, LANES), lambda b, p, pt: (b, 0, 0))],
        scratch_shapes=[pltpu.VMEM((2, pg, idx_dim, page), F32), pltpu.SemaphoreType.DMA((2,)),
                        pltpu.VMEM((idx_dim, pg * page), BF16)],
    )
    return pl.pallas_call(
        kern,
        grid_spec=grid_spec,
        out_shape=[jax.ShapeDtypeStruct((n, n_pages // pg, 1, pg * page), F32),
                   jax.ShapeDtypeStruct((n, 1, LANES), F32)],
        compiler_params=_cparams(2),
        name="sample_scores",
    )(page_table, qi3, wi3, ki_new3, cache_kidx_t)


def _sample_select_kernel(sc_ref, self_ref, sel4_ref, selself_ref, sel_ref, *, topk, past, rep):
    sc = sc_ref[...]
    sself = self_ref[:, 0:1]
    n = sc.shape[0]
    kf = jnp.float32(topk)
    row_min = jnp.minimum(jnp.min(sc, axis=1, keepdims=True), sself)
    row_max = jnp.maximum(jnp.max(sc, axis=1, keepdims=True), sself)
    n_adm = jnp.full((n, 1), past + 1, F32)

    def count_gt(t):
        return (jnp.sum(jnp.where(sc > t, 1.0, 0.0), axis=1, keepdims=True)
                + jnp.where(sself > t, 1.0, 0.0))

    def any_fn(mask):
        return jnp.max(jnp.where(mask, 1.0, 0.0)) > 0.0

    lo, hi, flo, fhi = _select_threshold(count_gt, row_min, row_max, n_adm, topk, any_fn)
    tie = flo != kf
    sel_ref[...] = jnp.where(sc > lo, 1.0, 0.0)
    selself_ref[...] = jnp.broadcast_to(jnp.where(sself > lo, 1.0, 0.0), selself_ref.shape)

    @pl.when(any_fn(tie))
    def _():
        need = kf - fhi
        blk = 512
        tri = (lax.broadcasted_iota(jnp.int32, (blk, blk), 0)
               < lax.broadcasted_iota(jnp.int32, (blk, blk), 1)).astype(BF16)
        before = jnp.zeros((n, 1), F32)
        for c in range(past // blk):
            s = sc_ref[:, c * blk:(c + 1) * blk]
            eq = s == hi
            rank = before + jnp.dot(jnp.where(eq, 1.0, 0.0).astype(BF16), tri, preferred_element_type=F32)
            keep = (s > hi) | (eq & (rank < need))
            sel_ref[:, c * blk:(c + 1) * blk] = jnp.where(tie, jnp.where(keep, 1.0, 0.0),
                                                          sel_ref[:, c * blk:(c + 1) * blk])
            before = before + jnp.sum(jnp.where(eq, 1.0, 0.0), axis=1, keepdims=True)
        keep_self = (sself > hi) | ((sself == hi) & (before < need))
        selself_ref[...] = jnp.broadcast_to(
            jnp.where(tie, jnp.where(keep_self, 1.0, 0.0), jnp.where(sself > lo, 1.0, 0.0)), selself_ref.shape)

    blk = 512
    row_lo = lax.broadcasted_iota(jnp.int32, (blk, blk * rep), 0) * rep
    col = lax.broadcasted_iota(jnp.int32, (blk, blk * rep), 1)
    spread = jnp.where((col >= row_lo) & (col < row_lo + rep), 1.0, 0.0).astype(BF16)
    for c in range(past // blk):
        sel4_ref[:, c * blk * rep:(c + 1) * blk * rep] = jnp.dot(
            sel_ref[:, c * blk:(c + 1) * blk].astype(BF16), spread, preferred_element_type=F32)


def sample_select(scores, sself, topk, rep):
    n, past = scores.shape
    kern = functools.partial(_sample_select_kernel, topk=topk, past=past, rep=rep)
    return pl.pallas_call(
        kern,
        out_shape=[jax.ShapeDtypeStruct((n, past * rep), F32), jax.ShapeDtypeStruct((n, LANES), F32)],
        scratch_shapes=[pltpu.VMEM((n, past), F32)],
        compiler_params=pltpu.CompilerParams(vmem_limit_bytes=VMEM_LIMIT),
        name="sample_select",
    )(scores, sself)


def _sample_attn_kernel(pt_ref, q_ref, kn_ref, vn_ref, sel_ref, selself_ref, bias_ref, rb0_ref, own_ref, *rest,
                        pg, n_heads, n_kv, hd):
    k_hbm, v_hbm, o_ref, kbuf_ref, vbuf_ref, sem, kcat_ref, vcat_ref, m_ref, l_ref, acc_ref = rest
    b = pl.program_id(0)
    p = pl.program_id(1)
    n_steps = pl.num_programs(1)
    scale = hd ** -0.5
    rows = k_hbm.shape[1]
    step = b * n_steps + p
    slot = step % 2

    def page_copies(bb, pp, sl):
        copies = []
        for i in range(pg):
            pid = pt_ref[bb, pp * pg + i]
            dst = pl.ds(i * rows, rows)
            copies.append(pltpu.make_async_copy(k_hbm.at[pid], kbuf_ref.at[sl, dst], sem.at[0, sl]))
            copies.append(pltpu.make_async_copy(v_hbm.at[pid], vbuf_ref.at[sl, dst], sem.at[1, sl]))
        return copies

    @pl.when(step == 0)
    def _():
        for c in page_copies(0, 0, 0):
            c.start()

    nxt = step + 1

    @pl.when(nxt < pl.num_programs(0) * n_steps)
    def _():
        for c in page_copies(nxt // n_steps, nxt % n_steps, 1 - slot):
            c.start()

    for c in page_copies(b, p, slot):
        c.wait()

    @pl.when(p == 0)
    def _():
        m_ref[...] = jnp.full(m_ref.shape, NEG, F32)
        l_ref[...] = jnp.zeros(l_ref.shape, F32)
        acc_ref[...] = jnp.zeros(acc_ref.shape, F32)

    for i in range(pg):
        kcat_ref[i * rows:(i + 1) * rows, :] = kbuf_ref[slot, pl.ds(i * rows, rows), :].astype(BF16)
        vcat_ref[i * rows:(i + 1) * rows, :] = vbuf_ref[slot, pl.ds(i * rows, rows), :].astype(BF16)
    qb = q_ref[...].astype(BF16)
    last = p == n_steps - 1
    bias = jnp.concatenate([bias_ref[0]] * (pg - 1) + [jnp.where(last, bias_ref[1], bias_ref[0])], axis=1)
    msk = (own_ref[...] > 0.0) & (sel_ref[...] > 0.0)
    lg = jnp.where(msk, _dot_nt(qb, kcat_ref[...]) * scale + bias, -jnp.inf)
    m_old = m_ref[...]
    m_new = jnp.maximum(m_old, jnp.max(lg, axis=-1, keepdims=True))
    alpha = jnp.exp(m_old - m_new)
    pr = jnp.exp(lg - m_new)
    l_new = alpha * l_ref[...] + jnp.sum(pr, axis=-1, keepdims=True)
    acc = alpha * acc_ref[...] + jnp.dot(pr.astype(BF16), vcat_ref[...], preferred_element_type=F32)
    m_ref[...] = m_new
    l_ref[...] = l_new
    acc_ref[...] = acc

    @pl.when(last)
    def _():
        kn = kn_ref[...].astype(BF16).astype(F32)
        vn = vn_ref[...].astype(BF16).astype(F32)
        ls = jnp.sum(qb.astype(F32) * kn, axis=-1, keepdims=True) * scale + rb0_ref[...]
        on = selself_ref[:, 0:1] > 0.0
        ls = jnp.where(on, ls, NEG)
        m_f = jnp.maximum(m_new, ls)
        a2 = jnp.exp(m_new - m_f)
        ps = jnp.where(on, jnp.exp(ls - m_f), 0.0)
        l_f = a2 * l_new + ps
        acc_f = a2 * acc + ps.astype(BF16).astype(F32) * vn
        o_ref[...] = acc_f / l_f


def sample_attention(page_table, q3, k_rep, v_rep, sel4, selself3, bias_s, rb0, cache_k, cache_v, pg, n_kv):
    n, n_pages = page_table.shape
    n_heads, hd = q3.shape[1:]
    rows = cache_k.shape[1]
    kern = functools.partial(_sample_attn_kernel, pg=pg, n_heads=n_heads, n_kv=n_kv, hd=hd)
    own = (np.arange(pg * rows)[None, :] % n_kv
           == np.arange(n_heads)[:, None] // (n_heads // n_kv)).astype(np.float32)
    hbm = pl.BlockSpec(memory_space=pl.ANY)
    per_seq = lambda b, p, pt: (b, 0, 0)
    grid_spec = pltpu.PrefetchScalarGridSpec(
        num_scalar_prefetch=1,
        grid=(n, n_pages // pg),
        in_specs=[pl.BlockSpec((None, n_heads, hd), per_seq),
                  pl.BlockSpec((None, n_heads, hd), per_seq),
                  pl.BlockSpec((None, n_heads, hd), per_seq),
                  pl.BlockSpec((None, None, 1, pg * rows), lambda b, p, pt: (b, p, 0, 0)),
                  pl.BlockSpec((None, 1, LANES), per_seq),
                  pl.BlockSpec((2, n_heads, rows), lambda b, p, pt: (0, 0, 0)),
                  pl.BlockSpec((n_heads, 1), lambda b, p, pt: (0, 0)),
                  pl.BlockSpec((n_heads, pg * rows), lambda b, p, pt: (0, 0)), hbm, hbm],
        out_specs=pl.BlockSpec((None, n_heads, hd), per_seq),
        scratch_shapes=[pltpu.VMEM((2, pg * rows, hd), F32), pltpu.VMEM((2, pg * rows, hd), F32),
                        pltpu.SemaphoreType.DMA((2, 2)),
                        pltpu.VMEM((pg * rows, hd), BF16), pltpu.VMEM((pg * rows, hd), BF16),
                        pltpu.VMEM((n_heads, 1), F32), pltpu.VMEM((n_heads, 1), F32),
                        pltpu.VMEM((n_heads, hd), F32)],
    )
    return pl.pallas_call(
        kern,
        grid_spec=grid_spec,
        out_shape=jax.ShapeDtypeStruct((n, n_heads, hd), F32),
        compiler_params=_cparams(2),
        name="sample_attn",
    )(page_table, q3, k_rep, v_rep, sel4, selself3, bias_s, rb0, jnp.asarray(own), cache_k, cache_v)


def _mix_kernel(ga_ref, gb_ref, co_ref, at_ref, o_ref):
    o_ref[...] = (ga_ref[...] * co_ref[...] + gb_ref[...] * at_ref[...]).astype(o_ref.dtype)


def gated_mix(gates, conv_out, attn):
    n, dm = conv_out.shape
    return pl.pallas_call(
        _mix_kernel,
        grid=(1,),
        in_specs=[pl.BlockSpec((n, dm), lambda i: (0, 0)), pl.BlockSpec((n, dm), lambda i: (0, 1)),
                  pl.BlockSpec((n, dm), lambda i: (0, 0)), pl.BlockSpec((n, dm), lambda i: (0, 0))],
        out_specs=pl.BlockSpec((n, dm), lambda i: (0, 0)),
        out_shape=jax.ShapeDtypeStruct((n, dm), BF16),
    )(gates, gates, conv_out, attn)


def _ffn_up_kernel(x_ref, xh_ref, wg_ref, wv_ref, dg_ref, dv_ref, bg_ref, bv_ref, pg_ref, pv_ref,
                   o_ref, sg_ref, sv_ref, eg_ref, ev_ref, wgb_ref, wvb_ref, *, tm, rs, width, tiles_per_seq):
    hist = width - 1
    pad = xh_ref.shape[0]
    first = pl.program_id(0) % tiles_per_seq == 0
    kc = 256
    for wf_ref, wb_ref in ((wg_ref, wgb_ref), (wv_ref, wvb_ref)):
        for k0 in range(0, wf_ref.shape[0], kc):
            wb_ref[k0:k0 + kc, :] = wf_ref[k0:k0 + kc, :].astype(BF16)
    branches = ((wgb_ref, dg_ref, bg_ref, pg_ref, eg_ref, sg_ref),
                (wvb_ref, dv_ref, bv_ref, pv_ref, ev_ref, sv_ref))
    for w_ref, _, _, prev_ref, e_ref, _ in branches:
        e_ref[0, pl.ds(0, pad), :] = jnp.dot(xh_ref[...], w_ref[...], preferred_element_type=F32)

        @pl.when(first)
        def _():
            e_ref[0, pl.ds(pad - hist, hist), :] = prev_ref[...]

    def project(r):
        for w_ref, _, _, _, e_ref, _ in branches:
            e_ref[r % 2, pl.ds(pad, rs), :] = jnp.dot(x_ref[pl.ds(r * rs, rs), :], w_ref[...],
                                                      preferred_element_type=F32)

    def hand_over(r):
        for _, _, _, _, e_ref, _ in branches:
            e_ref[(r + 1) % 2, pl.ds(0, pad), :] = e_ref[r % 2, pl.ds(rs, pad), :]

    def activate(r):
        outs = []
        for _, d_ref, b_ref, _, e_ref, _ in branches:
            acc = b_ref[...] + d_ref[pl.ds(hist, 1), :] * e_ref[r % 2, pl.ds(pad, rs), :]
            for j in range(hist):
                acc = acc + d_ref[pl.ds(j, 1), :] * e_ref[r % 2, pl.ds(pad - hist + j, rs), :]
            outs.append(acc)
        g, v = outs
        o_ref[pl.ds(r * rs, rs), :] = (g * _sigmoid(g) * v).astype(o_ref.dtype)

    n_sub = tm // rs
    project(0)
    hand_over(0)
    for r in range(1, n_sub):
        project(r)
        activate(r - 1)
        hand_over(r)
    activate(n_sub - 1)
    for _, _, _, _, e_ref, s_ref in branches:
        s_ref[...] = e_ref[(n_sub - 1) % 2, pl.ds(pad + rs - hist, hist), :]


def ffn_up_act(xn, w_up, prev, dw, bdw, n_seq, seq, tm, tn):
    m, k = xn.shape
    f2 = w_up.shape[1]
    f = f2 // 2
    width = dw.shape[0]
    ncb = f // tn
    pad = 16
    tps = seq // tm
    assert seq % tm == 0 and tm % pad == 0 and f % tn == 0 and width - 1 <= pad
    rs = min(256, tm // 2)
    assert tm % rs == 0 and rs % 8 == 0 and k % 256 == 0
    kern = functools.partial(_ffn_up_kernel, tm=tm, rs=rs, width=width, tiles_per_seq=tps)
    hb = tm // pad
    st_shape = jax.ShapeDtypeStruct((m // tm, width - 1, f), F32)
    act, sg, sv = pl.pallas_call(
        kern,
        grid=(m // tm, ncb),
        in_specs=[pl.BlockSpec((tm, k), lambda i, c: (i, 0)),
                  pl.BlockSpec((pad, k), lambda i, c: (jnp.maximum(i * hb - 1, 0), 0)),
                  pl.BlockSpec((k, tn), lambda i, c: (0, c)),
                  pl.BlockSpec((k, tn), lambda i, c: (0, c + ncb)),
                  pl.BlockSpec((width, tn), lambda i, c: (0, c)),
                  pl.BlockSpec((width, tn), lambda i, c: (0, c + ncb)),
                  pl.BlockSpec((1, tn), lambda i, c: (0, c)),
                  pl.BlockSpec((1, tn), lambda i, c: (0, c + ncb)),
                  pl.BlockSpec((None, width - 1, tn), lambda i, c: (i // tps, 0, c)),
                  pl.BlockSpec((None, width - 1, tn), lambda i, c: (i // tps, 0, c + ncb))],
        out_specs=[pl.BlockSpec((tm, tn), lambda i, c: (i, c)),
                   pl.BlockSpec((None, width - 1, tn), lambda i, c: (i, 0, c)),
                   pl.BlockSpec((None, width - 1, tn), lambda i, c: (i, 0, c))],
        out_shape=[jax.ShapeDtypeStruct((m, f), BF16), st_shape, st_shape],
        scratch_shapes=[pltpu.VMEM((2, pad + rs, tn), F32), pltpu.VMEM((2, pad + rs, tn), F32),
                        pltpu.VMEM((k, tn), BF16), pltpu.VMEM((k, tn), BF16)],
        compiler_params=_cparams(2),
        name="ffn_up_act",
    )(xn, xn, w_up, w_up, dw, dw, bdw.reshape(1, f2), bdw.reshape(1, f2), prev, prev)
    state = jnp.concatenate([sg, sv], axis=-1).reshape(n_seq, tps, width - 1, f2)[:, tps - 1]
    return act, state


def _ffn_act_step_kernel(ug_ref, uv_ref, pg_ref, pv_ref, wg_ref, wv_ref, bg_ref, bv_ref, o_ref, *, width):
    def conv(u_ref, prev_ref, w_ref, b_ref):
        acc = b_ref[...] + w_ref[pl.ds(width - 1, 1), :] * u_ref[...]
        for j in range(width - 1):
            acc = acc + w_ref[pl.ds(j, 1), :] * prev_ref[j]
        return acc
    g = conv(ug_ref, pg_ref, wg_ref, bg_ref)
    v = conv(uv_ref, pv_ref, wv_ref, bv_ref)
    o_ref[...] = (g * _sigmoid(g) * v).astype(o_ref.dtype)


def ffn_act_step(u, prev_t, dw, bdw, tc):
    n, f2 = u.shape
    f = f2 // 2
    width = dw.shape[0]
    ncb = f // tc
    kern = functools.partial(_ffn_act_step_kernel, width=width)
    return pl.pallas_call(
        kern,
        grid=(ncb,),
        in_specs=[pl.BlockSpec((n, tc), lambda c: (0, c)),
                  pl.BlockSpec((n, tc), lambda c: (0, c + ncb)),
                  pl.BlockSpec((width - 1, n, tc), lambda c: (0, 0, c)),
                  pl.BlockSpec((width - 1, n, tc), lambda c: (0, 0, c + ncb)),
                  pl.BlockSpec((width, tc), lambda c: (0, c)),
                  pl.BlockSpec((width, tc), lambda c: (0, c + ncb)),
                  pl.BlockSpec((1, tc), lambda c: (0, c)),
                  pl.BlockSpec((1, tc), lambda c: (0, c + ncb))],
        out_specs=pl.BlockSpec((n, tc), lambda c: (0, c)),
        out_shape=jax.ShapeDtypeStruct((n, f), BF16),
        compiler_params=_cparams(1),
    )(u, u, prev_t, prev_t, dw, dw, bdw.reshape(1, f2), bdw.reshape(1, f2))


def _row_tile(m, cap):
    tm = cap
    while m % tm:
        tm //= 2
    return tm if tm >= 16 else m


def _in_projection(xn, w_main, w_kw, w_g, sizes, tm):
    glu_w, q_w, k_w, v_w, qi_w, ki_w, wi_w, ga_w, gb_w = sizes
    offs = np.concatenate([[0], np.cumsum(sizes)])
    tn = 512
    t = dict(w_transposed=True)
    glu = matmul_w(xn, w_main, int(offs[0]), glu_w, tm, tn, name="mm_glu", **t)
    q = matmul_w(xn, w_main, int(offs[1]), q_w, tm, tn, name="mm_q", out_dtype=BF16, **t)
    k = matmul_w(xn, w_main, int(offs[2]), k_w, tm, min(tn, k_w), name="mm_k", **t)
    v = matmul_w(xn, w_main, int(offs[3]), v_w, tm, min(tn, v_w), name="mm_v", **t)
    qi = matmul_w(xn, w_main, int(offs[4]), qi_w, tm, min(tn, qi_w), name="mm_qi", out_dtype=BF16, **t)
    kw = matmul_w(xn, w_kw, 0, LANES, tm, LANES, name="mm_kw", **t)
    gates = matmul_w(xn, w_g, 0, ga_w + gb_w, tm, tn, name="mm_gates", sigmoid=True, **t)
    return glu, q, k, v, qi, kw, gates


def kernel(x_prompt, x_sample, cache_k, cache_v, cache_kidx, state_conv, state_ffn, page_table, rel_bias,
           norm_attn, w_in, dw_conv, b_dw_conv, ln_conv_g, ln_conv_b, w_conv_out, w_o, norm_ffn, w_up, dw_ffn,
           b_dw_ffn, w_down, norm_final):
    bsz, seq, dm = x_prompt.shape
    nd, dec_seq, _ = x_sample.shape
    depth, n_pool, page, n_kv, hd = cache_k.shape
    idx_dim = cache_kidx.shape[-1]
    n_pages = page_table.shape[1]
    past = n_pages * page
    width, dconv = dw_conv.shape[1:]
    fwidth = dw_ffn.shape[1]
    f = w_down.shape[1]
    n_heads = w_o.shape[1] // hd
    d_attn = n_heads * hd
    d_kv = n_kv * hd
    n_in = w_in.shape[2]
    idx_heads = (n_in - 2 * dconv - d_attn - 2 * d_kv - idx_dim - 2 * dm) // (idx_dim + 1)
    sizes = (2 * dconv, d_attn, d_kv, d_kv, idx_heads * idx_dim, idx_dim, idx_heads, dm, dm)
    assert sum(sizes) == n_in and depth == 1 and dec_seq == 1 and page == LANES and d_attn == dm

    mp = bsz * seq
    xp = x_prompt.reshape(mp, dm)
    xs = x_sample.reshape(nd, dm)
    bias3, bias_s = bias_tables(rel_bias, page, n_kv)
    tmp = _row_tile(mp, 2048)
    drop = lambda a: a.reshape(a.shape[1:])
    (norm_attn, w_in, dw_conv, b_dw_conv, ln_conv_g, ln_conv_b, w_conv_out, w_o, norm_ffn, w_up, dw_ffn,
     b_dw_ffn, w_down, state_conv, state_ffn) = map(drop, (
         norm_attn, w_in, dw_conv, b_dw_conv, ln_conv_g, ln_conv_b, w_conv_out, w_o, norm_ffn, w_up, dw_ffn,
         b_dw_ffn, w_down, state_conv, state_ffn))
    kidx_pool = jnp.swapaxes(cache_kidx.reshape(n_pool, page, idx_dim), 1, 2)
    k_pool = cache_k.reshape(n_pool, page * n_kv, hd)
    v_pool = cache_v.reshape(n_pool, page * n_kv, hd)

    n_aligned = sum(sizes[:5])
    n_small = idx_dim + idx_heads
    w_in_t = jnp.swapaxes(w_in, 0, 1)
    w_main = cast_rows_bf16(w_in_t, 0, n_aligned)
    w_kw = cast_rows_bf16(w_in_t, n_aligned, LANES, n_small)
    w_g = cast_rows_bf16(w_in_t, n_aligned + n_small, 2 * dm)
    w_o, w_down = cast_bf16(w_o), cast_bf16(w_down)

    xn = rmsnorm_rows(xp, norm_attn, BF16, 512)
    glu, q, k, v, qi, kw, gates = _in_projection(xn, w_main, w_kw, w_g, sizes, tmp)
    conv0 = jnp.zeros((bsz, width - 1, dconv), F32)
    conv_out, conv_state_p = conv_branch_prompt(glu, conv0, dw_conv, b_dw_conv, ln_conv_g, ln_conv_b,
                                                w_conv_out, bsz, seq, 256)
    mixed = attn_prompt(qi, kw, q, k, v, conv_out, gates, bias3, bsz, seq, n_kv, idx_dim, idx_heads)
    x2, xn2 = matmul_res_norm(mixed, w_o, xp, norm_ffn, _row_tile(mp, 512))
    ffn0 = jnp.zeros((bsz, fwidth - 1, 2 * f), F32)
    act, ffn_state_p = ffn_up_act(xn2, w_up, ffn0, dw_ffn, b_dw_ffn, bsz, seq, _row_tile(seq, 2048), 512)
    tk_down = f // 4 if f % (4 * LANES) == 0 else f
    y_prompt = matmul_res_norm_out(act, w_down, x2, norm_final, _row_tile(mp, 512), tk_down).reshape(bsz, seq, dm)

    xns = rmsnorm_rows(xs, norm_attn, BF16, nd)
    glu_s, q_s, k_s, v_s, qi_s, kw_s, gates_s = _in_projection(xns, w_main, w_kw, w_g, sizes, nd)
    ki_s = kw_s[:, :idx_dim]
    wi_s = kw_s[:, idx_dim:idx_dim + idx_heads]
    sc_prev_t = jnp.swapaxes(state_conv, 0, 1)
    conv_out_s, u_conv_s = conv_branch_step(glu_s, sc_prev_t, dw_conv, b_dw_conv, ln_conv_g, ln_conv_b,
                                            w_conv_out)
    conv_state_s = jnp.concatenate([state_conv[:, 1:], u_conv_s[:, None, :]], axis=1)

    pg = 16 if n_pages % 16 == 0 else 8
    pg_idx = 32 if n_pages % 32 == 0 else pg
    scores3, sself3 = sample_scores(page_table, qi_s.reshape(nd, idx_heads, idx_dim),
                                    wi_s.reshape(nd, idx_heads, 1), ki_s.reshape(nd, 1, idx_dim), kidx_pool,
                                    pg_idx)
    topk_s = min(TOPK_MAX, (past + dec_seq) // 4)
    sel4, selself = sample_select(scores3.reshape(nd, past), sself3.reshape(nd, LANES), topk_s, n_kv)
    group = n_heads // n_kv
    k_rep = jnp.repeat(k_s.reshape(nd, n_kv, hd), group, axis=1)
    v_rep = jnp.repeat(v_s.reshape(nd, n_kv, hd), group, axis=1)
    attn_s = sample_attention(page_table, q_s.reshape(nd, n_heads, hd), k_rep, v_rep,
                              sel4.reshape(nd, n_pages // pg, 1, pg * page * n_kv), selself.reshape(nd, 1, LANES),
                              bias_s,
                              rel_bias[0].reshape(n_heads, 1), k_pool, v_pool, pg, n_kv)
    mixed_s = gated_mix(gates_s, conv_out_s, attn_s.reshape(nd, dm))
    x2s, xn2s = matmul_res_norm(mixed_s, w_o, xs, norm_ffn, nd)
    u_s = matmul_w(xn2s, w_up, 0, 2 * f, nd, 512)
    sf_prev_t = jnp.swapaxes(state_ffn, 0, 1)
    act_s = ffn_act_step(u_s, sf_prev_t, dw_ffn, b_dw_ffn, 512)
    y_sample = matmul_res_norm_out(act_s, w_down, x2s, norm_final, nd, tk_down).reshape(nd, dec_seq, dm)
    ffn_state_s = jnp.concatenate([state_ffn[:, 1:], u_s[:, None, :]], axis=1)

    return (y_prompt, y_sample,
            k.reshape(1, bsz, seq, n_kv, hd), v.reshape(1, bsz, seq, n_kv, hd),
            kw.reshape(bsz, seq, LANES)[None, :, :, :idx_dim],
            conv_state_p[None], ffn_state_p[None],
            k_s.reshape(1, nd, dec_seq, n_kv, hd), v_s.reshape(1, nd, dec_seq, n_kv, hd),
            ki_s.reshape(1, nd, dec_seq, idx_dim),
            conv_state_s[None], ffn_state_s[None])
```

```python
import functools
import math

import numpy as np
import jax
import jax.numpy as jnp
from jax import lax
from jax.experimental import pallas as pl
from jax.experimental.pallas import tpu as pltpu

F32 = jnp.float32
BF16 = jnp.bfloat16

EPS = 1e-6
TOPK_MAX = 256
N_BUCKETS = 32
MAX_DISTANCE = 128
QB = 128
CK = 256
LANES = 128
NEG = -1e30
LOG2E = math.log2(math.e)
VMEM_LIMIT = 56 * 1024 * 1024


def _cparams(n_axes, vmem=VMEM_LIMIT):
    return pltpu.CompilerParams(dimension_semantics=("arbitrary",) * n_axes, vmem_limit_bytes=vmem)


def _dot_nt(a, b):
    return lax.dot_general(a, b, (((1,), (1,)), ((), ())), preferred_element_type=F32)


def _sigmoid(x):
    return 1.0 / (1.0 + jnp.exp(-x))


def _fold_rows(x, op):
    while x.shape[0] > 8:
        half = x.shape[0] // 2
        x = op(x[:half], x[half:])
    return x


def _rel_bucket_np(dist):
    n = np.maximum(dist, 0)
    max_exact = N_BUCKETS // 2
    nf = np.maximum(n, 1).astype(np.float32)
    large = max_exact + (np.log(nf / np.float32(max_exact)) / np.float32(math.log(MAX_DISTANCE / max_exact))
                         * np.float32(N_BUCKETS - max_exact)).astype(np.int32)
    large = np.minimum(large, N_BUCKETS - 1)
    return np.where(n < max_exact, n, large).astype(np.int32)


def _rms_kernel(x_ref, g_ref, o_ref):
    x = x_ref[...]
    y = x * lax.rsqrt(jnp.mean(x * x, axis=-1, keepdims=True) + EPS) * g_ref[...]
    o_ref[...] = y.astype(o_ref.dtype)


def rmsnorm_rows(x, g, out_dtype, tm):
    m, d = x.shape
    return pl.pallas_call(
        _rms_kernel,
        grid=(m // tm,),
        in_specs=[pl.BlockSpec((tm, d), lambda i: (i, 0)), pl.BlockSpec((1, d), lambda i: (0, 0))],
        out_specs=pl.BlockSpec((tm, d), lambda i: (i, 0)),
        out_shape=jax.ShapeDtypeStruct((m, d), out_dtype),
        compiler_params=_cparams(1),
        name="rmsnorm",
    )(x, g.reshape(1, d))


def _cast_kernel(w_ref, o_ref):
    o_ref[...] = w_ref[...].astype(o_ref.dtype)


def cast_bf16(w, ncols=None):
    k, n = w.shape
    ncols = n if ncols is None else ncols
    tk = 512 if k % 512 == 0 else k
    tn = 1024 if ncols % 1024 == 0 else (512 if ncols % 512 == 0 else ncols)
    return pl.pallas_call(
        _cast_kernel,
        grid=(k // tk, ncols // tn),
        in_specs=[pl.BlockSpec((tk, tn), lambda i, j: (i, j))],
        out_specs=pl.BlockSpec((tk, tn), lambda i, j: (i, j)),
        out_shape=jax.ShapeDtypeStruct((k, ncols), BF16),
        compiler_params=_cparams(2),
        name="cast_bf16",
    )(w)


def _cast_rows_kernel(w_ref, o_ref, *, valid):
    w = w_ref[...]
    if valid < w.shape[0]:
        w = jnp.where(lax.broadcasted_iota(jnp.int32, w.shape, 0) < valid, w, 0.0)
    o_ref[...] = w.astype(o_ref.dtype)


def cast_rows_bf16(wt, row0, nrows, nvalid=None):
    n, k = wt.shape
    tr = 512 if nrows % 512 == 0 else nrows
    nvalid = nrows if nvalid is None else nvalid
    assert row0 % 8 == 0 and (nvalid == nrows or tr == nrows) and row0 + nrows <= n
    return pl.pallas_call(
        functools.partial(_cast_rows_kernel, valid=nvalid),
        grid=(nrows // tr,),
        in_specs=[pl.BlockSpec((pl.Element(tr), pl.Element(k)), lambda i: (pl.multiple_of(row0 + i * tr, 8), 0))],
        out_specs=pl.BlockSpec((tr, k), lambda i: (i, 0)),
        out_shape=jax.ShapeDtypeStruct((nrows, k), BF16),
        compiler_params=_cparams(1),
        name="cast_rows_bf16",
    )(wt)


def _mm_kernel(a_ref, w_ref, o_ref):
    o_ref[...] = jnp.dot(a_ref[...], w_ref[...].astype(BF16), preferred_element_type=F32).astype(o_ref.dtype)


def _mm_nt_kernel(a_ref, wt_ref, o_ref, *, sigmoid):
    y = _dot_nt(a_ref[...], wt_ref[...])
    o_ref[...] = (_sigmoid(y) if sigmoid else y).astype(o_ref.dtype)


def matmul_w(a, w, col0, ncols, tm, tn, name="matmul", out_dtype=F32, w_transposed=False, sigmoid=False):
    m, k = a.shape
    assert col0 % tn == 0 and ncols % tn == 0 and m % tm == 0 and (w_transposed or not sigmoid)
    cb = col0 // tn
    if w_transposed:
        w_spec = pl.BlockSpec((tn, k), lambda i, j: (j + cb, 0))
    else:
        w_spec = pl.BlockSpec((k, tn), lambda i, j: (0, j + cb))
    return pl.pallas_call(
        functools.partial(_mm_nt_kernel, sigmoid=sigmoid) if w_transposed else _mm_kernel,
        grid=(m // tm, ncols // tn),
        in_specs=[pl.BlockSpec((tm, k), lambda i, j: (i, 0)), w_spec],
        out_specs=pl.BlockSpec((tm, tn), lambda i, j: (i, j)),
        out_shape=jax.ShapeDtypeStruct((m, ncols), out_dtype),
        compiler_params=_cparams(2),
        name=name,
    )(a, w)


def _rms(x, g):
    return x * lax.rsqrt(jnp.mean(x * x, axis=-1, keepdims=True) + EPS) * g


def _mm_res_norm_kernel(a_ref, w_ref, r_ref, g_ref, o_ref, on_ref):
    x = r_ref[...] + jnp.dot(a_ref[...], w_ref[...], preferred_element_type=F32)
    o_ref[...] = x
    on_ref[...] = _rms(x, g_ref[...]).astype(on_ref.dtype)


def matmul_res_norm(a, w, res, g, tm):
    m, k = a.shape
    n = w.shape[1]
    return pl.pallas_call(
        _mm_res_norm_kernel,
        grid=(m // tm,),
        in_specs=[pl.BlockSpec((tm, k), lambda i: (i, 0)), pl.BlockSpec((k, n), lambda i: (0, 0)),
                  pl.BlockSpec((tm, n), lambda i: (i, 0)), pl.BlockSpec((1, n), lambda i: (0, 0))],
        out_specs=[pl.BlockSpec((tm, n), lambda i: (i, 0)), pl.BlockSpec((tm, n), lambda i: (i, 0))],
        out_shape=[jax.ShapeDtypeStruct((m, n), F32), jax.ShapeDtypeStruct((m, n), BF16)],
        compiler_params=_cparams(1),
        name="mm_o_norm",
    )(a, w, res, g.reshape(1, n))


def _mm_ksplit_norm_kernel(a_ref, w_ref, r_ref, g_ref, o_ref, acc_ref):
    kk = pl.program_id(1)

    @pl.when(kk == 0)
    def _():
        acc_ref[...] = r_ref[...]
    acc_ref[...] += jnp.dot(a_ref[...], w_ref[...], preferred_element_type=F32)

    @pl.when(kk == pl.num_programs(1) - 1)
    def _():
        o_ref[...] = _rms(acc_ref[...], g_ref[...])


def matmul_res_norm_out(a, w, res, g, tm, tk):
    m, k = a.shape
    n = w.shape[1]
    assert k % tk == 0 and m % tm == 0
    return pl.pallas_call(
        _mm_ksplit_norm_kernel,
        grid=(m // tm, k // tk),
        in_specs=[pl.BlockSpec((tm, tk), lambda i, kk: (i, kk)), pl.BlockSpec((tk, n), lambda i, kk: (kk, 0)),
                  pl.BlockSpec((tm, n), lambda i, kk: (i, 0)), pl.BlockSpec((1, n), lambda i, kk: (0, 0))],
        out_specs=pl.BlockSpec((tm, n), lambda i, kk: (i, 0)),
        out_shape=jax.ShapeDtypeStruct((m, n), F32),
        scratch_shapes=[pltpu.VMEM((tm, n), F32)],
        compiler_params=_cparams(2),
        name="mm_down_norm",
    )(a, w, res, g.reshape(1, n))


def _bias_kernel(rb_ref, bk3_ref, bks_ref, o3_ref, os_ref, *, n_heads):
    def head(h, carry):
        for t in range(3):
            bk = bk3_ref[t]
            acc = jnp.zeros(bk.shape, F32)
            for b in range(N_BUCKETS):
                acc = jnp.where(bk == b, rb_ref[b, h], acc)
            o3_ref[t, h] = acc * LOG2E
        for t in range(2):
            bk = bks_ref[t]
            acc = jnp.zeros(bk.shape, F32)
            for b in range(N_BUCKETS):
                acc = jnp.where(bk == b, rb_ref[b, h], acc)
            os_ref[t, pl.ds(h, 1), :] = acc
        return carry
    lax.fori_loop(0, n_heads, head, 0)


def bias_tables(rel_bias, page, rep):
    n_heads = rel_bias.shape[1]
    cols = page * rep
    i = np.arange(QB)[None, :]
    k = np.arange(QB)[:, None]
    bk3 = np.stack([_rel_bucket_np(i - k + 2 * QB), _rel_bucket_np(i - k + QB), _rel_bucket_np(i - k)])
    assert (_rel_bucket_np(np.arange(QB + 1, 1 << 20)) == N_BUCKETS - 1).all()
    assert (bk3[0] == N_BUCKETS - 1).all()
    assert page >= QB
    bks = np.stack([np.full((1, cols), N_BUCKETS - 1, np.int32),
                    _rel_bucket_np(page - np.arange(cols) // rep)[None, :]])
    return pl.pallas_call(
        functools.partial(_bias_kernel, n_heads=n_heads),
        in_specs=[pl.BlockSpec(memory_space=pltpu.SMEM),
                  pl.BlockSpec(memory_space=pltpu.VMEM), pl.BlockSpec(memory_space=pltpu.VMEM)],
        out_specs=[pl.BlockSpec(memory_space=pltpu.VMEM), pl.BlockSpec(memory_space=pltpu.VMEM)],
        out_shape=[jax.ShapeDtypeStruct((3, n_heads, QB, QB), F32),
                   jax.ShapeDtypeStruct((2, n_heads, cols), F32)],
    )(rel_bias, jnp.asarray(bk3), jnp.asarray(bks))


def _conv_kernel(glu_ref, prev_ref, dw_ref, bdw_ref, lng_ref, lnb_ref, wout_ref, o_ref, st_ref,
                 ext_ref, h_ref, wbf_ref, sh_ref, h2_ref, *, tt, width, dconv):
    b = pl.program_id(0)
    t = pl.program_id(1)
    pad = 32
    hist = width - 1

    @pl.when((b == 0) & (t == 0))
    def _():
        wbf_ref[...] = wout_ref[...].astype(BF16)

    @pl.when(t == 0)
    def _():
        ext_ref[pl.ds(pad - hist, hist), :] = prev_ref[...]

    @pl.when(t > 0)
    def _():
        ext_ref[pl.ds(0, pad), :] = ext_ref[pl.ds(tt, pad), :]

    glu = glu_ref[...]
    u = glu[:, :dconv] * _sigmoid(glu[:, dconv:])
    ext_ref[pl.ds(pad, tt), :] = u
    st_ref[...] = ext_ref[pl.ds(pad + tt - hist, hist), :]

    span = tt + pad - 8
    for s in range(1, 8):
        sh_ref[s - 1] = ext_ref[pl.ds(s, span), :]

    half = tt // 2
    for r, hh_ref in enumerate((h_ref, h2_ref)):
        r0 = r * half
        for c in range(dconv // LANES):
            cs = slice(c * LANES, (c + 1) * LANES)
            acc = jnp.zeros((half, LANES), F32) + bdw_ref[:, cs]
            for j in range(width):
                a, s = divmod(pad - hist + j, 8)
                src = ext_ref if s == 0 else sh_ref.at[s - 1]
                acc = acc + dw_ref[pl.ds(j, 1), cs] * src[pl.ds(8 * a + r0, half), cs]
            hh_ref[:, cs] = acc
        h = hh_ref[...]
        mu = jnp.mean(h, axis=-1, keepdims=True)
        var = jnp.mean(jnp.square(h - mu), axis=-1, keepdims=True)
        y = (h - mu) * lax.rsqrt(var + EPS) * lng_ref[...] + lnb_ref[...]
        y = y * _sigmoid(y)
        o_ref[pl.ds(r0, half), :] = jnp.dot(y.astype(BF16), wbf_ref[...], preferred_element_type=F32)


def conv_branch_prompt(glu_pre, prev, dw, bdw, lng, lnb, wout, n_seq, seq, tt):
    width, dconv = dw.shape
    dm = wout.shape[1]
    nt = seq // tt
    kern = functools.partial(_conv_kernel, tt=tt, width=width, dconv=dconv)
    return pl.pallas_call(
        kern,
        grid=(n_seq, nt),
        in_specs=[pl.BlockSpec((tt, 2 * dconv), lambda b, t: (b * nt + t, 0)),
                  pl.BlockSpec((None, width - 1, dconv), lambda b, t: (b, 0, 0)),
                  pl.BlockSpec((width, dconv), lambda b, t: (0, 0)),
                  pl.BlockSpec((1, dconv), lambda b, t: (0, 0)),
                  pl.BlockSpec((1, dconv), lambda b, t: (0, 0)),
                  pl.BlockSpec((1, dconv), lambda b, t: (0, 0)),
                  pl.BlockSpec((dconv, dm), lambda b, t: (0, 0))],
        out_specs=[pl.BlockSpec((tt, dm), lambda b, t: (b * nt + t, 0)),
                   pl.BlockSpec((None, width - 1, dconv), lambda b, t: (b, 0, 0))],
        out_shape=[jax.ShapeDtypeStruct((n_seq * seq, dm), F32),
                   jax.ShapeDtypeStruct((n_seq, width - 1, dconv), F32)],
        scratch_shapes=[pltpu.VMEM((32 + tt, dconv), F32), pltpu.VMEM((tt // 2, dconv), F32),
                        pltpu.VMEM((dconv, dm), BF16), pltpu.VMEM((7, 24 + tt, dconv), F32),
                        pltpu.VMEM((tt // 2, dconv), F32)],
        compiler_params=_cparams(2),
        name="conv_branch",
    )(glu_pre, prev, dw, bdw.reshape(1, dconv), lng.reshape(1, dconv), lnb.reshape(1, dconv), wout)


def _conv_step_kernel(glu_ref, prev_ref, dw_ref, bdw_ref, lng_ref, lnb_ref, wout_ref, o_ref, u_ref,
                      *, width, dconv):
    glu = glu_ref[...]
    u = glu[:, :dconv] * _sigmoid(glu[:, dconv:])
    u_ref[...] = u
    h = bdw_ref[...] + dw_ref[pl.ds(width - 1, 1), :] * u
    for j in range(width - 1):
        h = h + dw_ref[pl.ds(j, 1), :] * prev_ref[j]
    mu = jnp.mean(h, axis=-1, keepdims=True)
    var = jnp.mean(jnp.square(h - mu), axis=-1, keepdims=True)
    y = (h - mu) * lax.rsqrt(var + EPS) * lng_ref[...] + lnb_ref[...]
    y = y * _sigmoid(y)
    o_ref[...] = jnp.dot(y.astype(BF16), wout_ref[...].astype(BF16), preferred_element_type=F32)


def conv_branch_step(glu_pre, prev_t, dw, bdw, lng, lnb, wout):
    width, dconv = dw.shape
    n = glu_pre.shape[0]
    dm = wout.shape[1]
    kern = functools.partial(_conv_step_kernel, width=width, dconv=dconv)
    return pl.pallas_call(
        kern,
        out_shape=[jax.ShapeDtypeStruct((n, dm), F32), jax.ShapeDtypeStruct((n, dconv), F32)],
        compiler_params=pltpu.CompilerParams(vmem_limit_bytes=VMEM_LIMIT),
    )(glu_pre, prev_t, dw, bdw.reshape(1, dconv), lng.reshape(1, dconv), lnb.reshape(1, dconv), wout)


def _select_threshold(count_gt, row_min, row_max, n_adm, topk, any_fn):
    kf = jnp.float32(topk)
    full = n_adm <= kf
    lo0 = row_min - (1.0 + jnp.abs(row_min))
    hi0 = row_max
    flo0 = jnp.where(full, kf, n_adm)
    fhi0 = jnp.zeros_like(lo0)

    def active_rows(lo, hi, flo):
        mid = 0.5 * lo + 0.5 * hi
        return (flo != kf) & (lo < mid) & (mid < hi)

    def cond(st):
        lo, hi, flo, fhi = st
        return any_fn(active_rows(lo, hi, flo))

    def step(st):
        lo, hi, flo, fhi = st
        act = active_rows(lo, hi, flo)
        mid = 0.5 * lo + 0.5 * hi
        c = count_gt(mid)
        up = act & (c >= kf)
        dn = act & (c < kf)
        return (jnp.where(up, mid, lo), jnp.where(dn, mid, hi),
                jnp.where(up, c, flo), jnp.where(dn, c, fhi))

    def body(st):
        for _ in range(4):
            st = step(st)
        return st

    lo, hi, flo, fhi = lax.while_loop(cond, body, (lo0, hi0, flo0, fhi0))
    lo = jnp.where(full, -jnp.inf, lo)
    return lo, hi, flo, fhi


def _attn_prompt_kernel(qi_ref, wi_ref, kw_ref, q_ref, k_ref, v_ref, co_ref, ga_ref, gb_ref, bias_ref,
                        o_ref, kd_ref, kb_ref, vt_ref, wit_ref, qib_ref, sc_ref, sel_ref, qs_ref, *state,
                        seq, n_heads, n_kv, idx_heads, idx_dim, topk, hd):
    m_refs, l_refs, acc_refs = state[:n_kv], state[n_kv:2 * n_kv], state[2 * n_kv:3 * n_kv]
    st_refs = state[3 * n_kv:]
    j = pl.program_id(1)
    group = n_heads // n_kv
    nck = (j * QB + QB + CK - 1) // CK
    kf = jnp.float32(topk)

    @pl.when(j == 0)
    def _():
        kd_ref[...] = kw_ref[:, :idx_dim].astype(BF16)
        for g in range(n_kv):
            kb_ref[g] = k_ref[:, g * hd:(g + 1) * hd].astype(BF16)
            for c in range(seq // CK):
                vt_ref[g, c] = v_ref[c * CK:(c + 1) * CK, g * hd:(g + 1) * hd].T.astype(BF16)

    wit_ref[...] = (wi_ref[...] * (idx_heads ** -0.5)).T
    qi = qi_ref[...].astype(F32) * (idx_dim ** -0.5)
    for h in range(idx_heads):
        qib_ref[h] = qi[:, h * idx_dim:(h + 1) * idx_dim].astype(BF16)
    qpos = j * QB + lax.broadcasted_iota(jnp.int32, (CK, QB), 1)
    krow = lax.broadcasted_iota(jnp.int32, (CK, QB), 0)
    per_dot = 2 * LANES // QB

    def score_chunk(c, carry):
        mn, mx = carry
        k0 = pl.multiple_of(c * CK, CK)
        kc = kd_ref[pl.ds(k0, CK), :]
        acc = jnp.zeros((CK, QB), F32)
        for p in range(idx_heads // per_dot):
            rhs = qib_ref[pl.ds(p * per_dot, per_dot)].reshape(per_dot * QB, idx_dim)
            s = _dot_nt(kc, rhs)
            for r in range(per_dot):
                acc = acc + (wit_ref[pl.ds(idx_dim + p * per_dot + r, 1), :]
                             * jnp.maximum(s[:, r * QB:(r + 1) * QB], 0.0))
        adm = (krow + c * CK) <= qpos
        sc_ref[c] = jnp.where(adm, acc, -jnp.inf)
        mn = jnp.minimum(mn, _fold_rows(jnp.where(adm, acc, jnp.inf), jnp.minimum))
        mx = jnp.maximum(mx, _fold_rows(jnp.where(adm, acc, -jnp.inf), jnp.maximum))
        return mn, mx

    mn8, mx8 = lax.fori_loop(0, nck, score_chunk,
                             (jnp.full((8, QB), jnp.inf, F32), jnp.full((8, QB), -jnp.inf, F32)))
    row_min = jnp.min(mn8, axis=0, keepdims=True)
    row_max = jnp.max(mx8, axis=0, keepdims=True)
    n_adm = (j * QB + 1 + lax.broadcasted_iota(jnp.int32, (1, QB), 1)).astype(F32)

    def count_gt(t):
        def cbody(c, acc):
            return acc + _fold_rows(jnp.where(sc_ref[c] > t, 1.0, 0.0), jnp.add)
        part = lax.fori_loop(0, nck, cbody, jnp.zeros((8, QB), F32))
        return jnp.sum(part, axis=0, keepdims=True)

    def any_fn(mask):
        return jnp.max(jnp.where(mask, 1.0, 0.0)) > 0.0

    lo, hi, flo, fhi = _select_threshold(count_gt, row_min, row_max, n_adm, topk, any_fn)
    tie = flo != kf

    def sel_chunk(c, carry):
        sel_ref[c] = jnp.where(sc_ref[c] > lo, 1.0, 0.0)
        return carry
    lax.fori_loop(0, nck, sel_chunk, 0)

    @pl.when(any_fn(tie))
    def _():
        need = kf - fhi
        lower = (lax.broadcasted_iota(jnp.int32, (CK, CK), 1)
                 < lax.broadcasted_iota(jnp.int32, (CK, CK), 0)).astype(BF16)

        def tie_chunk(c, before):
            s = sc_ref[c]
            eq = s == hi
            eqf = jnp.where(eq, 1.0, 0.0)
            rank = before + jnp.dot(lower, eqf.astype(BF16), preferred_element_type=F32)
            keep = (s > hi) | (eq & (rank < need))
            sel_ref[c] = jnp.where(tie, jnp.where(keep, 1.0, 0.0), sel_ref[c])
            return before + jnp.sum(eqf, axis=0, keepdims=True)
        lax.fori_loop(0, nck, tie_chunk, jnp.zeros((1, QB), F32))

    scale2 = hd ** -0.5 * LOG2E
    for h in range(n_heads):
        qs_ref[h] = q_ref[:, h * hd:(h + 1) * hd].astype(BF16)
    for g in range(n_kv):
        m_refs[g][...] = jnp.full(m_refs[g].shape, NEG, F32)
        l_refs[g][...] = jnp.zeros(l_refs[g].shape, F32)
        acc_refs[g][...] = jnp.zeros(acc_refs[g].shape, F32)

    def logits(g, c):
        kc = kb_ref[g, pl.ds(pl.multiple_of(c * CK, CK), CK), :]
        qg = qs_ref[pl.ds(g * group, group)].reshape(group * QB, hd)
        st_refs[g % 2][...] = _dot_nt(kc, qg)

    logits(0, 0)

    def att_chunk(c, carry):
        msk = jnp.concatenate([sel_ref[c]] * group, axis=1) > 0.0
        tis = [jnp.clip(c * (CK // QB) + s - j + 2, 0, 2) for s in range(CK // QB)]
        for g in range(n_kv):
            if g + 1 < n_kv:
                logits(g + 1, c)
            else:
                logits(0, jnp.minimum(c + 1, nck - 1))
            rows = [jnp.concatenate([bias_ref[ti, g * group + hh] for hh in range(group)], axis=1)
                    for ti in tis]
            lg = jnp.where(msk, st_refs[g % 2][...] * scale2 + jnp.concatenate(rows, axis=0), -jnp.inf)
            m_old = m_refs[g][...]
            m_new = jnp.maximum(m_old, jnp.max(_fold_rows(lg, jnp.maximum), axis=0, keepdims=True))
            alpha = jnp.exp2(m_old - m_new)
            p = jnp.exp2(lg - m_new)
            l_refs[g][...] = alpha * l_refs[g][...] + jnp.sum(_fold_rows(p, jnp.add), axis=0, keepdims=True)
            pv = jnp.dot(vt_ref[g, c], p.astype(BF16), preferred_element_type=F32)
            acc_refs[g][...] = alpha * acc_refs[g][...] + pv
            m_refs[g][...] = m_new
        return carry
    lax.fori_loop(0, nck, att_chunk, 0)

    for g in range(n_kv):
        ot = acc_refs[g][...] / l_refs[g][...]
        for hh in range(group):
            cs = slice((g * group + hh) * hd, (g * group + hh + 1) * hd)
            o = ot[:, hh * QB:(hh + 1) * QB].T
            mixed = ga_ref[:, cs] * co_ref[:, cs] + gb_ref[:, cs] * o
            o_ref[:, cs] = mixed.astype(o_ref.dtype)


def attn_prompt(qi, kw, q, k, v, conv_out, gates, bias3, n_seq, seq, n_kv, idx_dim, idx_heads):
    m, dm = q.shape
    hd = k.shape[1] // n_kv
    n_heads = dm // hd
    group = n_heads // n_kv
    nb = seq // QB
    topk = min(TOPK_MAX, seq // 4)
    assert seq % CK == 0 and LANES % idx_dim == 0 and kw.shape[1] == LANES
    kern = functools.partial(_attn_prompt_kernel, seq=seq, n_heads=n_heads, n_kv=n_kv, idx_heads=idx_heads,
                             idx_dim=idx_dim, topk=topk, hd=hd)
    row = lambda b, j: (b * nb + j, 0)
    return pl.pallas_call(
        kern,
        grid=(n_seq, nb),
        in_specs=[pl.BlockSpec((QB, idx_heads * idx_dim), row),
                  pl.BlockSpec((QB, LANES), row),
                  pl.BlockSpec((seq, LANES), lambda b, j: (b, 0)),
                  pl.BlockSpec((QB, dm), row),
                  pl.BlockSpec((seq, n_kv * hd), lambda b, j: (b, 0)),
                  pl.BlockSpec((seq, n_kv * hd), lambda b, j: (b, 0)),
                  pl.BlockSpec((QB, dm), row),
                  pl.BlockSpec((QB, dm), lambda b, j: (b * nb + j, 0)),
                  pl.BlockSpec((QB, dm), lambda b, j: (b * nb + j, 1)),
                  pl.BlockSpec((3, n_heads, QB, QB), lambda b, j: (0, 0, 0, 0))],
        out_specs=pl.BlockSpec((QB, dm), row),
        out_shape=jax.ShapeDtypeStruct((m, dm), BF16),
        scratch_shapes=[pltpu.VMEM((seq, idx_dim), BF16),
                        pltpu.VMEM((n_kv, seq, hd), BF16),
                        pltpu.VMEM((n_kv, seq // CK, hd, CK), BF16),
                        pltpu.VMEM((LANES, QB), F32),
                        pltpu.VMEM((idx_heads, QB, idx_dim), BF16),
                        pltpu.VMEM((seq // CK, CK, QB), F32),
                        pltpu.VMEM((seq // CK, CK, QB), F32),
                        pltpu.VMEM((n_heads, QB, hd), BF16)]
                       + [pltpu.VMEM((1, group * QB), F32)] * (2 * n_kv)
                       + [pltpu.VMEM((hd, group * QB), F32)] * n_kv
                       + [pltpu.VMEM((CK, group * QB), F32)] * 2,
        compiler_params=_cparams(2),
        name="attn_prompt",
    )(qi, kw, kw, q, k, v, conv_out, gates, gates, bias3)


def _sample_score_kernel(pt_ref, qi_ref, wi_ref, kn_ref, *rest, pg, idx_heads, idx_dim):
    k_hbm, o_ref, self_ref, kbuf_ref, sem, kcat_ref = rest
    page = k_hbm.shape[2]
    b = pl.program_id(0)
    p = pl.program_id(1)
    n_steps = pl.num_programs(1)
    step = b * n_steps + p
    slot = step % 2

    def page_copies(bb, pp, sl):
        return [pltpu.make_async_copy(k_hbm.at[pt_ref[bb, pp * pg + i]], kbuf_ref.at[sl, i], sem.at[sl])
                for i in range(pg)]

    @pl.when(step == 0)
    def _():
        for c in page_copies(0, 0, 0):
            c.start()

    nxt = step + 1

    @pl.when(nxt < pl.num_programs(0) * n_steps)
    def _():
        for c in page_copies(nxt // n_steps, nxt % n_steps, 1 - slot):
            c.start()

    for c in page_copies(b, p, slot):
        c.wait()

    qi = qi_ref[...] * (idx_dim ** -0.5)
    wi = wi_ref[...] * (idx_heads ** -0.5)
    qb = qi.astype(BF16)
    for i in range(pg):
        kcat_ref[:, i * page:(i + 1) * page] = kbuf_ref[slot, i].astype(BF16)
    s = jnp.dot(qb, kcat_ref[...], preferred_element_type=F32)
    o_ref[...] = jnp.sum(wi * jnp.maximum(s, 0.0), axis=0, keepdims=True)

    @pl.when(pl.program_id(1) == 0)
    def _():
        kn = kn_ref[...].astype(BF16).astype(F32)
        s = jnp.sum(qb.astype(F32) * kn, axis=1, keepdims=True)
        sself = jnp.sum(wi * jnp.maximum(s, 0.0), axis=0, keepdims=True)
        self_ref[...] = jnp.broadcast_to(sself, self_ref.shape)


def sample_scores(page_table, qi3, wi3, ki_new3, cache_kidx_t, pg):
    n, n_pages = page_table.shape
    idx_heads, idx_dim = qi3.shape[1:]
    page = cache_kidx_t.shape[2]
    kern = functools.partial(_sample_score_kernel, pg=pg, idx_heads=idx_heads, idx_dim=idx_dim)
    grid_spec = pltpu.PrefetchScalarGridSpec(
        num_scalar_prefetch=1,
        grid=(n, n_pages // pg),
        in_specs=[pl.BlockSpec((None, idx_heads, idx_dim), lambda b, p, pt: (b, 0, 0)),
                  pl.BlockSpec((None, idx_heads, 1), lambda b, p, pt: (b, 0, 0)),
                  pl.BlockSpec((None, 1, idx_dim), lambda b, p, pt: (b, 0, 0)),
                  pl.BlockSpec(memory_space=pl.ANY)],
        out_specs=[pl.BlockSpec((None, None, 1, pg * page), lambda b, p, pt: (b, p, 0, 0)),
                   pl.BlockSpec((None, 1, LANES), lambda b, p, pt: (b, 0, 0))],
        scratch_shapes=[pltpu.VMEM((2, pg, idx_dim, page), F32), pltpu.SemaphoreType.DMA((2,)),
                        pltpu.VMEM((idx_dim, pg * page), BF16)],
    )
    return pl.pallas_call(
        kern,
        grid_spec=grid_spec,
        out_shape=[jax.ShapeDtypeStruct((n, n_pages // pg, 1, pg * page), F32),
                   jax.ShapeDtypeStruct((n, 1, LANES), F32)],
        compiler_params=_cparams(2),
        name="sample_scores",
    )(page_table, qi3, wi3, ki_new3, cache_kidx_t)


def _sample_select_kernel(sc_ref, self_ref, sel4_ref, selself_ref, sel_ref, *, topk, past, rep):
    sc = sc_ref[...]
    sself = self_ref[:, 0:1]
    n = sc.shape[0]
    kf = jnp.float32(topk)
    row_min = jnp.minimum(jnp.min(sc, axis=1, keepdims=True), sself)
    row_max = jnp.maximum(jnp.max(sc, axis=1, keepdims=True), sself)
    n_adm = jnp.full((n, 1), past + 1, F32)

    def count_gt(t):
        return (jnp.sum(jnp.where(sc > t, 1.0, 0.0), axis=1, keepdims=True)
                + jnp.where(sself > t, 1.0, 0.0))

    def any_fn(mask):
        return jnp.max(jnp.where(mask, 1.0, 0.0)) > 0.0

    lo, hi, flo, fhi = _select_threshold(count_gt, row_min, row_max, n_adm, topk, any_fn)
    tie = flo != kf
    sel_ref[...] = jnp.where(sc > lo, 1.0, 0.0)
    selself_ref[...] = jnp.broadcast_to(jnp.where(sself > lo, 1.0, 0.0), selself_ref.shape)

    @pl.when(any_fn(tie))
    def _():
        need = kf - fhi
        blk = 512
        tri = (lax.broadcasted_iota(jnp.int32, (blk, blk), 0)
               < lax.broadcasted_iota(jnp.int32, (blk, blk), 1)).astype(BF16)
        before = jnp.zeros((n, 1), F32)
        for c in range(past // blk):
            s = sc_ref[:, c * blk:(c + 1) * blk]
            eq = s == hi
            rank = before + jnp.dot(jnp.where(eq, 1.0, 0.0).astype(BF16), tri, preferred_element_type=F32)
            keep = (s > hi) | (eq & (rank < need))
            sel_ref[:, c * blk:(c + 1) * blk] = jnp.where(tie, jnp.where(keep, 1.0, 0.0),
                                                          sel_ref[:, c * blk:(c + 1) * blk])
            before = before + jnp.sum(jnp.where(eq, 1.0, 0.0), axis=1, keepdims=True)
        keep_self = (sself > hi) | ((sself == hi) & (before < need))
        selself_ref[...] = jnp.broadcast_to(
            jnp.where(tie, jnp.where(keep_self, 1.0, 0.0), jnp.where(sself > lo, 1.0, 0.0)), selself_ref.shape)

    blk = 512
    row_lo = lax.broadcasted_iota(jnp.int32, (blk, blk * rep), 0) * rep
    col = lax.broadcasted_iota(jnp.int32, (blk, blk * rep), 1)
    spread = jnp.where((col >= row_lo) & (col < row_lo + rep), 1.0, 0.0).astype(BF16)
    for c in range(past // blk):
        sel4_ref[:, c * blk * rep:(c + 1) * blk * rep] = jnp.dot(
            sel_ref[:, c * blk:(c + 1) * blk].astype(BF16), spread, preferred_element_type=F32)


def sample_select(scores, sself, topk, rep):
    n, past = scores.shape
    kern = functools.partial(_sample_select_kernel, topk=topk, past=past, rep=rep)
    return pl.pallas_call(
        kern,
        out_shape=[jax.ShapeDtypeStruct((n, past * rep), F32), jax.ShapeDtypeStruct((n, LANES), F32)],
        scratch_shapes=[pltpu.VMEM((n, past), F32)],
        compiler_params=pltpu.CompilerParams(vmem_limit_bytes=VMEM_LIMIT),
        name="sample_select",
    )(scores, sself)


def _sample_attn_kernel(pt_ref, q_ref, kn_ref, vn_ref, sel_ref, selself_ref, bias_ref, rb0_ref, own_ref, *rest,
                        pg, n_heads, n_kv, hd):
    k_hbm, v_hbm, o_ref, kbuf_ref, vbuf_ref, sem, kcat_ref, vcat_ref, m_ref, l_ref, acc_ref = rest
    b = pl.program_id(0)
    p = pl.program_id(1)
    n_steps = pl.num_programs(1)
    scale = hd ** -0.5
    rows = k_hbm.shape[1]
    step = b * n_steps + p
    slot = step % 2

    def page_copies(bb, pp, sl):
        copies = []
        for i in range(pg):
            pid = pt_ref[bb, pp * pg + i]
            dst = pl.ds(i * rows, rows)
            copies.append(pltpu.make_async_copy(k_hbm.at[pid], kbuf_ref.at[sl, dst], sem.at[0, sl]))
            copies.append(pltpu.make_async_copy(v_hbm.at[pid], vbuf_ref.at[sl, dst], sem.at[1, sl]))
        return copies

    @pl.when(step == 0)
    def _():
        for c in page_copies(0, 0, 0):
            c.start()

    nxt = step + 1

    @pl.when(nxt < pl.num_programs(0) * n_steps)
    def _():
        for c in page_copies(nxt // n_steps, nxt % n_steps, 1 - slot):
            c.start()

    for c in page_copies(b, p, slot):
        c.wait()

    @pl.when(p == 0)
    def _():
        m_ref[...] = jnp.full(m_ref.shape, NEG, F32)
        l_ref[...] = jnp.zeros(l_ref.shape, F32)
        acc_ref[...] = jnp.zeros(acc_ref.shape, F32)

    for i in range(pg):
        kcat_ref[i * rows:(i + 1) * rows, :] = kbuf_ref[slot, pl.ds(i * rows, rows), :].astype(BF16)
        vcat_ref[i * rows:(i + 1) * rows, :] = vbuf_ref[slot, pl.ds(i * rows, rows), :].astype(BF16)
    qb = q_ref[...].astype(BF16)
    last = p == n_steps - 1
    bias = jnp.concatenate([bias_ref[0]] * (pg - 1) + [jnp.where(last, bias_ref[1], bias_ref[0])], axis=1)
    msk = (own_ref[...] > 0.0) & (sel_ref[...] > 0.0)
    lg = jnp.where(msk, _dot_nt(qb, kcat_ref[...]) * scale + bias, -jnp.inf)
    m_old = m_ref[...]
    m_new = jnp.maximum(m_old, jnp.max(lg, axis=-1, keepdims=True))
    alpha = jnp.exp(m_old - m_new)
    pr = jnp.exp(lg - m_new)
    l_new = alpha * l_ref[...] + jnp.sum(pr, axis=-1, keepdims=True)
    acc = alpha * acc_ref[...] + jnp.dot(pr.astype(BF16), vcat_ref[...], preferred_element_type=F32)
    m_ref[...] = m_new
    l_ref[...] = l_new
    acc_ref[...] = acc

    @pl.when(last)
    def _():
        kn = kn_ref[...].astype(BF16).astype(F32)
        vn = vn_ref[...].astype(BF16).astype(F32)
        ls = jnp.sum(qb.astype(F32) * kn, axis=-1, keepdims=True) * scale + rb0_ref[...]
        on = selself_ref[:, 0:1] > 0.0
        ls = jnp.where(on, ls, NEG)
        m_f = jnp.maximum(m_new, ls)
        a2 = jnp.exp(m_new - m_f)
        ps = jnp.where(on, jnp.exp(ls - m_f), 0.0)
        l_f = a2 * l_new + ps
        acc_f = a2 * acc + ps.astype(BF16).astype(F32) * vn
        o_ref[...] = acc_f / l_f


def sample_attention(page_table, q3, k_rep, v_rep, sel4, selself3, bias_s, rb0, cache_k, cache_v, pg, n_kv):
    n, n_pages = page_table.shape
    n_heads, hd = q3.shape[1:]
    rows = cache_k.shape[1]
    kern = functools.partial(_sample_attn_kernel, pg=pg, n_heads=n_heads, n_kv=n_kv, hd=hd)
    own = (np.arange(pg * rows)[None, :] % n_kv
           == np.arange(n_heads)[:, None] // (n_heads // n_kv)).astype(np.float32)
    hbm = pl.BlockSpec(memory_space=pl.ANY)
    per_seq = lambda b, p, pt: (b, 0, 0)
    grid_spec = pltpu.PrefetchScalarGridSpec(
        num_scalar_prefetch=1,
        grid=(n, n_pages // pg),
        in_specs=[pl.BlockSpec((None, n_heads, hd), per_seq),
                  pl.BlockSpec((None, n_heads, hd), per_seq),
                  pl.BlockSpec((None, n_heads, hd), per_seq),
                  pl.BlockSpec((None, None, 1, pg * rows), lambda b, p, pt: (b, p, 0, 0)),
                  pl.BlockSpec((None, 1, LANES), per_seq),
                  pl.BlockSpec((2, n_heads, rows), lambda b, p, pt: (0, 0, 0)),
                  pl.BlockSpec((n_heads, 1), lambda b, p, pt: (0, 0)),
                  pl.BlockSpec((n_heads, pg * rows), lambda b, p, pt: (0, 0)), hbm, hbm],
        out_specs=pl.BlockSpec((None, n_heads, hd), per_seq),
        scratch_shapes=[pltpu.VMEM((2, pg * rows, hd), F32), pltpu.VMEM((2, pg * rows, hd), F32),
                        pltpu.SemaphoreType.DMA((2, 2)),
                        pltpu.VMEM((pg * rows, hd), BF16), pltpu.VMEM((pg * rows, hd), BF16),
                        pltpu.VMEM((n_heads, 1), F32), pltpu.VMEM((n_heads, 1), F32),
                        pltpu.VMEM((n_heads, hd), F32)],
    )
    return pl.pallas_call(
        kern,
        grid_spec=grid_spec,
        out_shape=jax.ShapeDtypeStruct((n, n_heads, hd), F32),
        compiler_params=_cparams(2),
        name="sample_attn",
    )(page_table, q3, k_rep, v_rep, sel4, selself3, bias_s, rb0, jnp.asarray(own), cache_k, cache_v)


def _mix_kernel(ga_ref, gb_ref, co_ref, at_ref, o_ref):
    o_ref[...] = (ga_ref[...] * co_ref[...] + gb_ref[...] * at_ref[...]).astype(o_ref.dtype)


def gated_mix(gates, conv_out, attn):
    n, dm = conv_out.shape
    return pl.pallas_call(
        _mix_kernel,
        grid=(1,),
        in_specs=[pl.BlockSpec((n, dm), lambda i: (0, 0)), pl.BlockSpec((n, dm), lambda i: (0, 1)),
                  pl.BlockSpec((n, dm), lambda i: (0, 0)), pl.BlockSpec((n, dm), lambda i: (0, 0))],
        out_specs=pl.BlockSpec((n, dm), lambda i: (0, 0)),
        out_shape=jax.ShapeDtypeStruct((n, dm), BF16),
    )(gates, gates, conv_out, attn)


def _ffn_up_kernel(x_ref, xh_ref, wg_ref, wv_ref, dg_ref, dv_ref, bg_ref, bv_ref, pg_ref, pv_ref,
                   o_ref, sg_ref, sv_ref, eg_ref, ev_ref, wgb_ref, wvb_ref, *, tm, rs, width, tiles_per_seq):
    hist = width - 1
    pad = xh_ref.shape[0]
    first = pl.program_id(0) % tiles_per_seq == 0
    kc = 256
    for wf_ref, wb_ref in ((wg_ref, wgb_ref), (wv_ref, wvb_ref)):
        for k0 in range(0, wf_ref.shape[0], kc):
            wb_ref[k0:k0 + kc, :] = wf_ref[k0:k0 + kc, :].astype(BF16)
    branches = ((wgb_ref, dg_ref, bg_ref, pg_ref, eg_ref, sg_ref),
                (wvb_ref, dv_ref, bv_ref, pv_ref, ev_ref, sv_ref))
    for w_ref, _, _, prev_ref, e_ref, _ in branches:
        e_ref[0, pl.ds(0, pad), :] = jnp.dot(xh_ref[...], w_ref[...], preferred_element_type=F32)

        @pl.when(first)
        def _():
            e_ref[0, pl.ds(pad - hist, hist), :] = prev_ref[...]

    def project(r):
        for w_ref, _, _, _, e_ref, _ in branches:
            e_ref[r % 2, pl.ds(pad, rs), :] = jnp.dot(x_ref[pl.ds(r * rs, rs), :], w_ref[...],
                                                      preferred_element_type=F32)

    def hand_over(r):
        for _, _, _, _, e_ref, _ in branches:
            e_ref[(r + 1) % 2, pl.ds(0, pad), :] = e_ref[r % 2, pl.ds(rs, pad), :]

    def activate(r):
        outs = []
        for _, d_ref, b_ref, _, e_ref, _ in branches:
            acc = b_ref[...] + d_ref[pl.ds(hist, 1), :] * e_ref[r % 2, pl.ds(pad, rs), :]
            for j in range(hist):
                acc = acc + d_ref[pl.ds(j, 1), :] * e_ref[r % 2, pl.ds(pad - hist + j, rs), :]
            outs.append(acc)
        g, v = outs
        o_ref[pl.ds(r * rs, rs), :] = (g * _sigmoid(g) * v).astype(o_ref.dtype)

    n_sub = tm // rs
    project(0)
    hand_over(0)
    for r in range(1, n_sub):
        project(r)
        activate(r - 1)
        hand_over(r)
    activate(n_sub - 1)
    for _, _, _, _, e_ref, s_ref in branches:
        s_ref[...] = e_ref[(n_sub - 1) % 2, pl.ds(pad + rs - hist, hist), :]


def ffn_up_act(xn, w_up, prev, dw, bdw, n_seq, seq, tm, tn):
    m, k = xn.shape
    f2 = w_up.shape[1]
    f = f2 // 2
    width = dw.shape[0]
    ncb = f // tn
    pad = 16
    tps = seq // tm
    assert seq % tm == 0 and tm % pad == 0 and f % tn == 0 and width - 1 <= pad
    rs = min(256, tm // 2)
    assert tm % rs == 0 and rs % 8 == 0 and k % 256 == 0
    kern = functools.partial(_ffn_up_kernel, tm=tm, rs=rs, width=width, tiles_per_seq=tps)
    hb = tm // pad
    st_shape = jax.ShapeDtypeStruct((m // tm, width - 1, f), F32)
    act, sg, sv = pl.pallas_call(
        kern,
        grid=(m // tm, ncb),
        in_specs=[pl.BlockSpec((tm, k), lambda i, c: (i, 0)),
                  pl.BlockSpec((pad, k), lambda i, c: (jnp.maximum(i * hb - 1, 0), 0)),
                  pl.BlockSpec((k, tn), lambda i, c: (0, c)),
                  pl.BlockSpec((k, tn), lambda i, c: (0, c + ncb)),
                  pl.BlockSpec((width, tn), lambda i, c: (0, c)),
                  pl.BlockSpec((width, tn), lambda i, c: (0, c + ncb)),
                  pl.BlockSpec((1, tn), lambda i, c: (0, c)),
                  pl.BlockSpec((1, tn), lambda i, c: (0, c + ncb)),
                  pl.BlockSpec((None, width - 1, tn), lambda i, c: (i // tps, 0, c)),
                  pl.BlockSpec((None, width - 1, tn), lambda i, c: (i // tps, 0, c + ncb))],
        out_specs=[pl.BlockSpec((tm, tn), lambda i, c: (i, c)),
                   pl.BlockSpec((None, width - 1, tn), lambda i, c: (i, 0, c)),
                   pl.BlockSpec((None, width - 1, tn), lambda i, c: (i, 0, c))],
        out_shape=[jax.ShapeDtypeStruct((m, f), BF16), st_shape, st_shape],
        scratch_shapes=[pltpu.VMEM((2, pad + rs, tn), F32), pltpu.VMEM((2, pad + rs, tn), F32),
                        pltpu.VMEM((k, tn), BF16), pltpu.VMEM((k, tn), BF16)],
        compiler_params=_cparams(2),
        name="ffn_up_act",
    )(xn, xn, w_up, w_up, dw, dw, bdw.reshape(1, f2), bdw.reshape(1, f2), prev, prev)
    state = jnp.concatenate([sg, sv], axis=-1).reshape(n_seq, tps, width - 1, f2)[:, tps - 1]
    return act, state


def _ffn_act_step_kernel(ug_ref, uv_ref, pg_ref, pv_ref, wg_ref, wv_ref, bg_ref, bv_ref, o_ref, *, width):
    def conv(u_ref, prev_ref, w_ref, b_ref):
        acc = b_ref[...] + w_ref[pl.ds(width - 1, 1), :] * u_ref[...]
        for j in range(width - 1):
            acc = acc + w_ref[pl.ds(j, 1), :] * prev_ref[j]
        return acc
    g = conv(ug_ref, pg_ref, wg_ref, bg_ref)
    v = conv(uv_ref, pv_ref, wv_ref, bv_ref)
    o_ref[...] = (g * _sigmoid(g) * v).astype(o_ref.dtype)


def ffn_act_step(u, prev_t, dw, bdw, tc):
    n, f2 = u.shape
    f = f2 // 2
    width = dw.shape[0]
    ncb = f // tc
    kern = functools.partial(_ffn_act_step_kernel, width=width)
    return pl.pallas_call(
        kern,
        grid=(ncb,),
        in_specs=[pl.BlockSpec((n, tc), lambda c: (0, c)),
                  pl.BlockSpec((n, tc), lambda c: (0, c + ncb)),
                  pl.BlockSpec((width - 1, n, tc), lambda c: (0, 0, c)),
                  pl.BlockSpec((width - 1, n, tc), lambda c: (0, 0, c + ncb)),
                  pl.BlockSpec((width, tc), lambda c: (0, c)),
                  pl.BlockSpec((width, tc), lambda c: (0, c + ncb)),
                  pl.BlockSpec((1, tc), lambda c: (0, c)),
                  pl.BlockSpec((1, tc), lambda c: (0, c + ncb))],
        out_specs=pl.BlockSpec((n, tc), lambda c: (0, c)),
        out_shape=jax.ShapeDtypeStruct((n, f), BF16),
        compiler_params=_cparams(1),
    )(u, u, prev_t, prev_t, dw, dw, bdw.reshape(1, f2), bdw.reshape(1, f2))


def _row_tile(m, cap):
    tm = cap
    while m % tm:
        tm //= 2
    return tm if tm >= 16 else m


def _in_projection(xn, w_main, w_kw, w_g, sizes, tm):
    glu_w, q_w, k_w, v_w, qi_w, ki_w, wi_w, ga_w, gb_w = sizes
    offs = np.concatenate([[0], np.cumsum(sizes)])
    tn = 512
    t = dict(w_transposed=True)
    glu = matmul_w(xn, w_main, int(offs[0]), glu_w, tm, tn, name="mm_glu", **t)
    q = matmul_w(xn, w_main, int(offs[1]), q_w, tm, tn, name="mm_q", out_dtype=BF16, **t)
    k = matmul_w(xn, w_main, int(offs[2]), k_w, tm, min(tn, k_w), name="mm_k", **t)
    v = matmul_w(xn, w_main, int(offs[3]), v_w, tm, min(tn, v_w), name="mm_v", **t)
    qi = matmul_w(xn, w_main, int(offs[4]), qi_w, tm, min(tn, qi_w), name="mm_qi", out_dtype=BF16, **t)
    kw = matmul_w(xn, w_kw, 0, LANES, tm, LANES, name="mm_kw", **t)
    gates = matmul_w(xn, w_g, 0, ga_w + gb_w, tm, tn, name="mm_gates", sigmoid=True, **t)
    return glu, q, k, v, qi, kw, gates


def kernel(x_prompt, x_sample, cache_k, cache_v, cache_kidx, state_conv, state_ffn, page_table, rel_bias,
           norm_attn, w_in, dw_conv, b_dw_conv, ln_conv_g, ln_conv_b, w_conv_out, w_o, norm_ffn, w_up, dw_ffn,
           b_dw_ffn, w_down, norm_final):
    bsz, seq, dm = x_prompt.shape
    nd, dec_seq, _ = x_sample.shape
    depth, n_pool, page, n_kv, hd = cache_k.shape
    idx_dim = cache_kidx.shape[-1]
    n_pages = page_table.shape[1]
    past = n_pages * page
    width, dconv = dw_conv.shape[1:]
    fwidth = dw_ffn.shape[1]
    f = w_down.shape[1]
    n_heads = w_o.shape[1] // hd
    d_attn = n_heads * hd
    d_kv = n_kv * hd
    n_in = w_in.shape[2]
    idx_heads = (n_in - 2 * dconv - d_attn - 2 * d_kv - idx_dim - 2 * dm) // (idx_dim + 1)
    sizes = (2 * dconv, d_attn, d_kv, d_kv, idx_heads * idx_dim, idx_dim, idx_heads, dm, dm)
    assert sum(sizes) == n_in and depth == 1 and dec_seq == 1 and page == LANES and d_attn == dm

    mp = bsz * seq
    xp = x_prompt.reshape(mp, dm)
    xs = x_sample.reshape(nd, dm)
    bias3, bias_s = bias_tables(rel_bias, page, n_kv)
    tmp = _row_tile(mp, 2048)
    drop = lambda a: a.reshape(a.shape[1:])
    (norm_attn, w_in, dw_conv, b_dw_conv, ln_conv_g, ln_conv_b, w_conv_out, w_o, norm_ffn, w_up, dw_ffn,
     b_dw_ffn, w_down, state_conv, state_ffn) = map(drop, (
         norm_attn, w_in, dw_conv, b_dw_conv, ln_conv_g, ln_conv_b, w_conv_out, w_o, norm_ffn, w_up, dw_ffn,
         b_dw_ffn, w_down, state_conv, state_ffn))
    kidx_pool = jnp.swapaxes(cache_kidx.reshape(n_pool, page, idx_dim), 1, 2)
    k_pool = cache_k.reshape(n_pool, page * n_kv, hd)
    v_pool = cache_v.reshape(n_pool, page * n_kv, hd)

    n_aligned = sum(sizes[:5])
    n_small = idx_dim + idx_heads
    w_in_t = jnp.swapaxes(w_in, 0, 1)
    w_main = cast_rows_bf16(w_in_t, 0, n_aligned)
    w_kw = cast_rows_bf16(w_in_t, n_aligned, LANES, n_small)
    w_g = cast_rows_bf16(w_in_t, n_aligned + n_small, 2 * dm)
    w_o, w_down = cast_bf16(w_o), cast_bf16(w_down)

    xn = rmsnorm_rows(xp, norm_attn, BF16, 512)
    glu, q, k, v, qi, kw, gates = _in_projection(xn, w_main, w_kw, w_g, sizes, tmp)
    conv0 = jnp.zeros((bsz, width - 1, dconv), F32)
    conv_out, conv_state_p = conv_branch_prompt(glu, conv0, dw_conv, b_dw_conv, ln_conv_g, ln_conv_b,
                                                w_conv_out, bsz, seq, 256)
    mixed = attn_prompt(qi, kw, q, k, v, conv_out, gates, bias3, bsz, seq, n_kv, idx_dim, idx_heads)
    x2, xn2 = matmul_res_norm(mixed, w_o, xp, norm_ffn, _row_tile(mp, 512))
    ffn0 = jnp.zeros((bsz, fwidth - 1, 2 * f), F32)
    act, ffn_state_p = ffn_up_act(xn2, w_up, ffn0, dw_ffn, b_dw_ffn, bsz, seq, _row_tile(seq, 2048), 512)
    tk_down = f // 4 if f % (4 * LANES) == 0 else f
    tk_p = 512 if f % 512 == 0 else tk_down
    y_prompt = matmul_res_norm_out(act, w_down, x2, norm_final, _row_tile(mp, 1024), tk_p).reshape(bsz, seq, dm)

    xns = rmsnorm_rows(xs, norm_attn, BF16, nd)
    glu_s, q_s, k_s, v_s, qi_s, kw_s, gates_s = _in_projection(xns, w_main, w_kw, w_g, sizes, nd)
    ki_s = kw_s[:, :idx_dim]
    wi_s = kw_s[:, idx_dim:idx_dim + idx_heads]
    sc_prev_t = jnp.swapaxes(state_conv, 0, 1)
    conv_out_s, u_conv_s = conv_branch_step(glu_s, sc_prev_t, dw_conv, b_dw_conv, ln_conv_g, ln_conv_b,
                                            w_conv_out)
    conv_state_s = jnp.concatenate([state_conv[:, 1:], u_conv_s[:, None, :]], axis=1)

    pg = 16 if n_pages % 16 == 0 else 8
    pg_idx = 32 if n_pages % 32 == 0 else pg
    scores3, sself3 = sample_scores(page_table, qi_s.reshape(nd, idx_heads, idx_dim),
                                    wi_s.reshape(nd, idx_heads, 1), ki_s.reshape(nd, 1, idx_dim), kidx_pool,
                                    pg_idx)
    topk_s = min(TOPK_MAX, (past + dec_seq) // 4)
    sel4, selself = sample_select(scores3.reshape(nd, past), sself3.reshape(nd, LANES), topk_s, n_kv)
    group = n_heads // n_kv
    k_rep = jnp.repeat(k_s.reshape(nd, n_kv, hd), group, axis=1)
    v_rep = jnp.repeat(v_s.reshape(nd, n_kv, hd), group, axis=1)
    attn_s = sample_attention(page_table, q_s.reshape(nd, n_heads, hd), k_rep, v_rep,
                              sel4.reshape(nd, n_pages // pg, 1, pg * page * n_kv), selself.reshape(nd, 1, LANES),
                              bias_s,
                              rel_bias[0].reshape(n_heads, 1), k_pool, v_pool, pg, n_kv)
    mixed_s = gated_mix(gates_s, conv_out_s, attn_s.reshape(nd, dm))
    x2s, xn2s = matmul_res_norm(mixed_s, w_o, xs, norm_ffn, nd)
    u_s = matmul_w(xn2s, w_up, 0, 2 * f, nd, 512)
    sf_prev_t = jnp.swapaxes(state_ffn, 0, 1)
    act_s = ffn_act_step(u_s, sf_prev_t, dw_ffn, b_dw_ffn, 512)
    y_sample = matmul_res_norm_out(act_s, w_down, x2s, norm_final, nd, tk_down).reshape(nd, dec_seq, dm)
    ffn_state_s = jnp.concatenate([state_ffn[:, 1:], u_s[:, None, :]], axis=1)

    return (y_prompt, y_sample,
            k.reshape(1, bsz, seq, n_kv, hd), v.reshape(1, bsz, seq, n_kv, hd),
            kw.reshape(bsz, seq, LANES)[None, :, :, :idx_dim],
            conv_state_p[None], ffn_state_p[None],
            k_s.reshape(1, nd, dec_seq, n_kv, hd), v_s.reshape(1, nd, dec_seq, n_kv, hd),
            ki_s.reshape(1, nd, dec_seq, idx_dim),
            conv_state_s[None], ffn_state_s[None])
```

```python
import functools
import math

import numpy as np
import jax
import jax.numpy as jnp
from jax import lax
from jax.experimental import pallas as pl
from jax.experimental.pallas import tpu as pltpu

F32 = jnp.float32
BF16 = jnp.bfloat16

EPS = 1e-6
TOPK_MAX = 256
N_BUCKETS = 32
MAX_DISTANCE = 128
QB = 128
CK = 256
LANES = 128
NEG = -1e30
LOG2E = math.log2(math.e)
VMEM_LIMIT = 56 * 1024 * 1024


def _cparams(n_axes, vmem=VMEM_LIMIT):
    return pltpu.CompilerParams(dimension_semantics=("arbitrary",) * n_axes, vmem_limit_bytes=vmem)


def _dot_nt(a, b):
    return lax.dot_general(a, b, (((1,), (1,)), ((), ())), preferred_element_type=F32)


def _sigmoid(x):
    return 1.0 / (1.0 + jnp.exp(-x))


def _fold_rows(x, op):
    while x.shape[0] > 8:
        half = x.shape[0] // 2
        x = op(x[:half], x[half:])
    return x


def _rel_bucket_np(dist):
    n = np.maximum(dist, 0)
    max_exact = N_BUCKETS // 2
    nf = np.maximum(n, 1).astype(np.float32)
    large = max_exact + (np.log(nf / np.float32(max_exact)) / np.float32(math.log(MAX_DISTANCE / max_exact))
                         * np.float32(N_BUCKETS - max_exact)).astype(np.int32)
    large = np.minimum(large, N_BUCKETS - 1)
    return np.where(n < max_exact, n, large).astype(np.int32)


def _rms_kernel(x_ref, g_ref, o_ref):
    x = x_ref[...]
    y = x * lax.rsqrt(jnp.mean(x * x, axis=-1, keepdims=True) + EPS) * g_ref[...]
    o_ref[...] = y.astype(o_ref.dtype)


def rmsnorm_rows(x, g, out_dtype, tm):
    m, d = x.shape
    return pl.pallas_call(
        _rms_kernel,
        grid=(m // tm,),
        in_specs=[pl.BlockSpec((tm, d), lambda i: (i, 0)), pl.BlockSpec((1, d), lambda i: (0, 0))],
        out_specs=pl.BlockSpec((tm, d), lambda i: (i, 0)),
        out_shape=jax.ShapeDtypeStruct((m, d), out_dtype),
        compiler_params=_cparams(1),
        name="rmsnorm",
    )(x, g.reshape(1, d))


def _cast_kernel(w_ref, o_ref):
    o_ref[...] = w_ref[...].astype(o_ref.dtype)


def cast_bf16(w, ncols=None):
    k, n = w.shape
    ncols = n if ncols is None else ncols
    tk = 512 if k % 512 == 0 else k
    tn = 1024 if ncols % 1024 == 0 else (512 if ncols % 512 == 0 else ncols)
    return pl.pallas_call(
        _cast_kernel,
        grid=(k // tk, ncols // tn),
        in_specs=[pl.BlockSpec((tk, tn), lambda i, j: (i, j))],
        out_specs=pl.BlockSpec((tk, tn), lambda i, j: (i, j)),
        out_shape=jax.ShapeDtypeStruct((k, ncols), BF16),
        compiler_params=_cparams(2),
        name="cast_bf16",
    )(w)


def _cast_rows_kernel(w_ref, o_ref, *, valid):
    w = w_ref[...]
    if valid < w.shape[0]:
        w = jnp.where(lax.broadcasted_iota(jnp.int32, w.shape, 0) < valid, w, 0.0)
    o_ref[...] = w.astype(o_ref.dtype)


def cast_rows_bf16(wt, row0, nrows, nvalid=None):
    n, k = wt.shape
    tr = 512 if nrows % 512 == 0 else nrows
    nvalid = nrows if nvalid is None else nvalid
    assert row0 % 8 == 0 and (nvalid == nrows or tr == nrows) and row0 + nrows <= n
    return pl.pallas_call(
        functools.partial(_cast_rows_kernel, valid=nvalid),
        grid=(nrows // tr,),
        in_specs=[pl.BlockSpec((pl.Element(tr), pl.Element(k)), lambda i: (pl.multiple_of(row0 + i * tr, 8), 0))],
        out_specs=pl.BlockSpec((tr, k), lambda i: (i, 0)),
        out_shape=jax.ShapeDtypeStruct((nrows, k), BF16),
        compiler_params=_cparams(1),
        name="cast_rows_bf16",
    )(wt)


def _mm_kernel(a_ref, w_ref, o_ref):
    o_ref[...] = jnp.dot(a_ref[...], w_ref[...].astype(BF16), preferred_element_type=F32).astype(o_ref.dtype)


def _mm_nt_kernel(a_ref, wt_ref, o_ref, *, sigmoid):
    y = _dot_nt(a_ref[...], wt_ref[...])
    o_ref[...] = (_sigmoid(y) if sigmoid else y).astype(o_ref.dtype)


def matmul_w(a, w, col0, ncols, tm, tn, name="matmul", out_dtype=F32, w_transposed=False, sigmoid=False):
    m, k = a.shape
    assert col0 % tn == 0 and ncols % tn == 0 and m % tm == 0 and (w_transposed or not sigmoid)
    cb = col0 // tn
    if w_transposed:
        w_spec = pl.BlockSpec((tn, k), lambda i, j: (j + cb, 0))
    else:
        w_spec = pl.BlockSpec((k, tn), lambda i, j: (0, j + cb))
    return pl.pallas_call(
        functools.partial(_mm_nt_kernel, sigmoid=sigmoid) if w_transposed else _mm_kernel,
        grid=(m // tm, ncols // tn),
        in_specs=[pl.BlockSpec((tm, k), lambda i, j: (i, 0)), w_spec],
        out_specs=pl.BlockSpec((tm, tn), lambda i, j: (i, j)),
        out_shape=jax.ShapeDtypeStruct((m, ncols), out_dtype),
        compiler_params=_cparams(2),
        name=name,
    )(a, w)


def _rms(x, g):
    return x * lax.rsqrt(jnp.mean(x * x, axis=-1, keepdims=True) + EPS) * g


def _mm_res_norm_kernel(a_ref, w_ref, r_ref, g_ref, o_ref, on_ref):
    x = r_ref[...] + jnp.dot(a_ref[...], w_ref[...], preferred_element_type=F32)
    o_ref[...] = x
    on_ref[...] = _rms(x, g_ref[...]).astype(on_ref.dtype)


def matmul_res_norm(a, w, res, g, tm):
    m, k = a.shape
    n = w.shape[1]
    return pl.pallas_call(
        _mm_res_norm_kernel,
        grid=(m // tm,),
        in_specs=[pl.BlockSpec((tm, k), lambda i: (i, 0)), pl.BlockSpec((k, n), lambda i: (0, 0)),
                  pl.BlockSpec((tm, n), lambda i: (i, 0)), pl.BlockSpec((1, n), lambda i: (0, 0))],
        out_specs=[pl.BlockSpec((tm, n), lambda i: (i, 0)), pl.BlockSpec((tm, n), lambda i: (i, 0))],
        out_shape=[jax.ShapeDtypeStruct((m, n), F32), jax.ShapeDtypeStruct((m, n), BF16)],
        compiler_params=_cparams(1),
        name="mm_o_norm",
    )(a, w, res, g.reshape(1, n))


def _mm_ksplit_norm_kernel(a_ref, w_ref, r_ref, g_ref, o_ref, acc_ref):
    kk = pl.program_id(1)

    @pl.when(kk == 0)
    def _():
        acc_ref[...] = r_ref[...]
    acc_ref[...] += jnp.dot(a_ref[...], w_ref[...], preferred_element_type=F32)

    @pl.when(kk == pl.num_programs(1) - 1)
    def _():
        o_ref[...] = _rms(acc_ref[...], g_ref[...])


def matmul_res_norm_out(a, w, res, g, tm, tk):
    m, k = a.shape
    n = w.shape[1]
    assert k % tk == 0 and m % tm == 0
    return pl.pallas_call(
        _mm_ksplit_norm_kernel,
        grid=(m // tm, k // tk),
        in_specs=[pl.BlockSpec((tm, tk), lambda i, kk: (i, kk)), pl.BlockSpec((tk, n), lambda i, kk: (kk, 0)),
                  pl.BlockSpec((tm, n), lambda i, kk: (i, 0)), pl.BlockSpec((1, n), lambda i, kk: (0, 0))],
        out_specs=pl.BlockSpec((tm, n), lambda i, kk: (i, 0)),
        out_shape=jax.ShapeDtypeStruct((m, n), F32),
        scratch_shapes=[pltpu.VMEM((tm, n), F32)],
        compiler_params=_cparams(2),
        name="mm_down_norm",
    )(a, w, res, g.reshape(1, n))


def _bias_kernel(rb_ref, bk3_ref, bks_ref, o3_ref, os_ref, *, n_heads):
    def head(h, carry):
        far = rb_ref[N_BUCKETS - 1, h]
        for t in range(3):
            bk = bk3_ref[t]
            acc = jnp.zeros(bk.shape, F32)
            for b in range(N_BUCKETS):
                acc = jnp.where(bk == b, rb_ref[b, h], acc)
            o3_ref[t, h] = (acc - far) * LOG2E
        for t in range(2):
            bk = bks_ref[t]
            acc = jnp.zeros(bk.shape, F32)
            for b in range(N_BUCKETS):
                acc = jnp.where(bk == b, rb_ref[b, h], acc)
            os_ref[t, pl.ds(h, 1), :] = acc
        return carry
    lax.fori_loop(0, n_heads, head, 0)


def bias_tables(rel_bias, page, rep):
    n_heads = rel_bias.shape[1]
    cols = page * rep
    i = np.arange(QB)[None, :]
    k = np.arange(QB)[:, None]
    bk3 = np.stack([_rel_bucket_np(i - k + 2 * QB), _rel_bucket_np(i - k + QB), _rel_bucket_np(i - k)])
    assert (_rel_bucket_np(np.arange(QB + 1, 1 << 20)) == N_BUCKETS - 1).all()
    assert (bk3[0] == N_BUCKETS - 1).all()
    assert page >= QB
    bks = np.stack([np.full((1, cols), N_BUCKETS - 1, np.int32),
                    _rel_bucket_np(page - np.arange(cols) // rep)[None, :]])
    return pl.pallas_call(
        functools.partial(_bias_kernel, n_heads=n_heads),
        in_specs=[pl.BlockSpec(memory_space=pltpu.SMEM),
                  pl.BlockSpec(memory_space=pltpu.VMEM), pl.BlockSpec(memory_space=pltpu.VMEM)],
        out_specs=[pl.BlockSpec(memory_space=pltpu.VMEM), pl.BlockSpec(memory_space=pltpu.VMEM)],
        out_shape=[jax.ShapeDtypeStruct((3, n_heads, QB, QB), F32),
                   jax.ShapeDtypeStruct((2, n_heads, cols), F32)],
    )(rel_bias, jnp.asarray(bk3), jnp.asarray(bks))


def _conv_kernel(glu_ref, prev_ref, dw_ref, bdw_ref, lng_ref, lnb_ref, wout_ref, o_ref, st_ref,
                 ext_ref, h_ref, wbf_ref, sh_ref, h2_ref, *, tt, width, dconv):
    b = pl.program_id(0)
    t = pl.program_id(1)
    pad = 32
    hist = width - 1

    @pl.when((b == 0) & (t == 0))
    def _():
        wbf_ref[...] = wout_ref[...].astype(BF16)

    @pl.when(t == 0)
    def _():
        ext_ref[pl.ds(pad - hist, hist), :] = prev_ref[...]

    @pl.when(t > 0)
    def _():
        ext_ref[pl.ds(0, pad), :] = ext_ref[pl.ds(tt, pad), :]

    glu = glu_ref[...]
    u = glu[:, :dconv] * _sigmoid(glu[:, dconv:])
    ext_ref[pl.ds(pad, tt), :] = u
    st_ref[...] = ext_ref[pl.ds(pad + tt - hist, hist), :]

    span = tt + pad - 8
    for s in range(1, 8):
        sh_ref[s - 1] = ext_ref[pl.ds(s, span), :]

    half = tt // 2
    for r, hh_ref in enumerate((h_ref, h2_ref)):
        r0 = r * half
        for c in range(dconv // LANES):
            cs = slice(c * LANES, (c + 1) * LANES)
            acc = jnp.zeros((half, LANES), F32) + bdw_ref[:, cs]
            for j in range(width):
                a, s = divmod(pad - hist + j, 8)
                src = ext_ref if s == 0 else sh_ref.at[s - 1]
                acc = acc + dw_ref[pl.ds(j, 1), cs] * src[pl.ds(8 * a + r0, half), cs]
            hh_ref[:, cs] = acc
        h = hh_ref[...]
        mu = jnp.mean(h, axis=-1, keepdims=True)
        var = jnp.mean(jnp.square(h - mu), axis=-1, keepdims=True)
        y = (h - mu) * lax.rsqrt(var + EPS) * lng_ref[...] + lnb_ref[...]
        y = y * _sigmoid(y)
        o_ref[pl.ds(r0, half), :] = jnp.dot(y.astype(BF16), wbf_ref[...], preferred_element_type=F32)


def conv_branch_prompt(glu_pre, prev, dw, bdw, lng, lnb, wout, n_seq, seq, tt):
    width, dconv = dw.shape
    dm = wout.shape[1]
    nt = seq // tt
    kern = functools.partial(_conv_kernel, tt=tt, width=width, dconv=dconv)
    return pl.pallas_call(
        kern,
        grid=(n_seq, nt),
        in_specs=[pl.BlockSpec((tt, 2 * dconv), lambda b, t: (b * nt + t, 0)),
                  pl.BlockSpec((None, width - 1, dconv), lambda b, t: (b, 0, 0)),
                  pl.BlockSpec((width, dconv), lambda b, t: (0, 0)),
                  pl.BlockSpec((1, dconv), lambda b, t: (0, 0)),
                  pl.BlockSpec((1, dconv), lambda b, t: (0, 0)),
                  pl.BlockSpec((1, dconv), lambda b, t: (0, 0)),
                  pl.BlockSpec((dconv, dm), lambda b, t: (0, 0))],
        out_specs=[pl.BlockSpec((tt, dm), lambda b, t: (b * nt + t, 0)),
                   pl.BlockSpec((None, width - 1, dconv), lambda b, t: (b, 0, 0))],
        out_shape=[jax.ShapeDtypeStruct((n_seq * seq, dm), F32),
                   jax.ShapeDtypeStruct((n_seq, width - 1, dconv), F32)],
        scratch_shapes=[pltpu.VMEM((32 + tt, dconv), F32), pltpu.VMEM((tt // 2, dconv), F32),
                        pltpu.VMEM((dconv, dm), BF16), pltpu.VMEM((7, 24 + tt, dconv), F32),
                        pltpu.VMEM((tt // 2, dconv), F32)],
        compiler_params=_cparams(2),
        name="conv_branch",
    )(glu_pre, prev, dw, bdw.reshape(1, dconv), lng.reshape(1, dconv), lnb.reshape(1, dconv), wout)


def _conv_step_kernel(glu_ref, prev_ref, dw_ref, bdw_ref, lng_ref, lnb_ref, wout_ref, o_ref, u_ref,
                      *, width, dconv):
    glu = glu_ref[...]
    u = glu[:, :dconv] * _sigmoid(glu[:, dconv:])
    u_ref[...] = u
    h = bdw_ref[...] + dw_ref[pl.ds(width - 1, 1), :] * u
    for j in range(width - 1):
        h = h + dw_ref[pl.ds(j, 1), :] * prev_ref[j]
    mu = jnp.mean(h, axis=-1, keepdims=True)
    var = jnp.mean(jnp.square(h - mu), axis=-1, keepdims=True)
    y = (h - mu) * lax.rsqrt(var + EPS) * lng_ref[...] + lnb_ref[...]
    y = y * _sigmoid(y)
    o_ref[...] = jnp.dot(y.astype(BF16), wout_ref[...].astype(BF16), preferred_element_type=F32)


def conv_branch_step(glu_pre, prev_t, dw, bdw, lng, lnb, wout):
    width, dconv = dw.shape
    n = glu_pre.shape[0]
    dm = wout.shape[1]
    kern = functools.partial(_conv_step_kernel, width=width, dconv=dconv)
    return pl.pallas_call(
        kern,
        out_shape=[jax.ShapeDtypeStruct((n, dm), F32), jax.ShapeDtypeStruct((n, dconv), F32)],
        compiler_params=pltpu.CompilerParams(vmem_limit_bytes=VMEM_LIMIT),
    )(glu_pre, prev_t, dw, bdw.reshape(1, dconv), lng.reshape(1, dconv), lnb.reshape(1, dconv), wout)


def _select_threshold(count_gt, row_min, row_max, n_adm, topk, any_fn):
    kf = jnp.float32(topk)
    full = n_adm <= kf
    lo0 = row_min - (1.0 + jnp.abs(row_min))
    hi0 = row_max
    flo0 = jnp.where(full, kf, n_adm)
    fhi0 = jnp.zeros_like(lo0)

    def active_rows(lo, hi, flo):
        mid = 0.5 * lo + 0.5 * hi
        return (flo != kf) & (lo < mid) & (mid < hi)

    def cond(st):
        lo, hi, flo, fhi = st
        return any_fn(active_rows(lo, hi, flo))

    def step(st):
        lo, hi, flo, fhi = st
        act = active_rows(lo, hi, flo)
        mid = 0.5 * lo + 0.5 * hi
        c = count_gt(mid)
        up = act & (c >= kf)
        dn = act & (c < kf)
        return (jnp.where(up, mid, lo), jnp.where(dn, mid, hi),
                jnp.where(up, c, flo), jnp.where(dn, c, fhi))

    def body(st):
        for _ in range(4):
            st = step(st)
        return st

    lo, hi, flo, fhi = lax.while_loop(cond, body, (lo0, hi0, flo0, fhi0))
    lo = jnp.where(full, -jnp.inf, lo)
    return lo, hi, flo, fhi


def _attn_prompt_kernel(qi_ref, wi_ref, kw_ref, q_ref, k_ref, v_ref, co_ref, ga_ref, gb_ref, bias_ref,
                        o_ref, kd_ref, kb_ref, vt_ref, wit_ref, qib_ref, sc_ref, sel_ref, qs_ref, *state,
                        seq, n_heads, n_kv, idx_heads, idx_dim, topk, hd):
    m_refs, l_refs, acc_refs = state[:n_kv], state[n_kv:2 * n_kv], state[2 * n_kv:3 * n_kv]
    st_refs = state[3 * n_kv:]
    j = pl.program_id(1)
    group = n_heads // n_kv
    nck = (j * QB + QB + CK - 1) // CK
    kf = jnp.float32(topk)

    @pl.when(j == 0)
    def _():
        kd_ref[...] = kw_ref[:, :idx_dim].astype(BF16)
        for g in range(n_kv):
            kb_ref[g] = k_ref[:, g * hd:(g + 1) * hd].astype(BF16)
            for c in range(seq // CK):
                vt_ref[g, c] = v_ref[c * CK:(c + 1) * CK, g * hd:(g + 1) * hd].T.astype(BF16)

    wit_ref[...] = (wi_ref[...] * (idx_heads ** -0.5)).T
    qi = qi_ref[...].astype(F32) * (idx_dim ** -0.5)
    for h in range(idx_heads):
        qib_ref[h] = qi[:, h * idx_dim:(h + 1) * idx_dim].astype(BF16)
    qpos = j * QB + lax.broadcasted_iota(jnp.int32, (CK, QB), 1)
    krow = lax.broadcasted_iota(jnp.int32, (CK, QB), 0)
    per_dot = 2 * LANES // QB

    def score_chunk(c, carry):
        mn, mx = carry
        k0 = pl.multiple_of(c * CK, CK)
        kc = kd_ref[pl.ds(k0, CK), :]
        acc = jnp.zeros((CK, QB), F32)
        for p in range(idx_heads // per_dot):
            rhs = qib_ref[pl.ds(p * per_dot, per_dot)].reshape(per_dot * QB, idx_dim)
            s = _dot_nt(kc, rhs)
            for r in range(per_dot):
                acc = acc + (wit_ref[pl.ds(idx_dim + p * per_dot + r, 1), :]
                             * jnp.maximum(s[:, r * QB:(r + 1) * QB], 0.0))
        adm = (krow + c * CK) <= qpos
        sc_ref[c] = jnp.where(adm, acc, -jnp.inf)
        mn = jnp.minimum(mn, _fold_rows(jnp.where(adm, acc, jnp.inf), jnp.minimum))
        mx = jnp.maximum(mx, _fold_rows(jnp.where(adm, acc, -jnp.inf), jnp.maximum))
        return mn, mx

    mn8, mx8 = lax.fori_loop(0, nck, score_chunk,
                             (jnp.full((8, QB), jnp.inf, F32), jnp.full((8, QB), -jnp.inf, F32)))
    row_min = jnp.min(mn8, axis=0, keepdims=True)
    row_max = jnp.max(mx8, axis=0, keepdims=True)
    n_adm = (j * QB + 1 + lax.broadcasted_iota(jnp.int32, (1, QB), 1)).astype(F32)

    def count_gt(t):
        def cbody(c, acc):
            return acc + _fold_rows(jnp.where(sc_ref[c] > t, 1.0, 0.0), jnp.add)
        part = lax.fori_loop(0, nck, cbody, jnp.zeros((8, QB), F32))
        return jnp.sum(part, axis=0, keepdims=True)

    def any_fn(mask):
        return jnp.max(jnp.where(mask, 1.0, 0.0)) > 0.0

    lo, hi, flo, fhi = _select_threshold(count_gt, row_min, row_max, n_adm, topk, any_fn)
    tie = flo != kf

    def sel_chunk(c, carry):
        sel_ref[c] = jnp.where(sc_ref[c] > lo, 1.0, 0.0)
        return carry
    lax.fori_loop(0, nck, sel_chunk, 0)

    @pl.when(any_fn(tie))
    def _():
        need = kf - fhi
        lower = (lax.broadcasted_iota(jnp.int32, (CK, CK), 1)
                 < lax.broadcasted_iota(jnp.int32, (CK, CK), 0)).astype(BF16)

        def tie_chunk(c, before):
            s = sc_ref[c]
            eq = s == hi
            eqf = jnp.where(eq, 1.0, 0.0)
            rank = before + jnp.dot(lower, eqf.astype(BF16), preferred_element_type=F32)
            keep = (s > hi) | (eq & (rank < need))
            sel_ref[c] = jnp.where(tie, jnp.where(keep, 1.0, 0.0), sel_ref[c])
            return before + jnp.sum(eqf, axis=0, keepdims=True)
        lax.fori_loop(0, nck, tie_chunk, jnp.zeros((1, QB), F32))

    scale2 = hd ** -0.5 * LOG2E
    for h in range(n_heads):
        qs_ref[h] = q_ref[:, h * hd:(h + 1) * hd].astype(BF16)
    for g in range(n_kv):
        m_refs[g][...] = jnp.full(m_refs[g].shape, NEG, F32)
        l_refs[g][...] = jnp.zeros(l_refs[g].shape, F32)
        acc_refs[g][...] = jnp.zeros(acc_refs[g].shape, F32)

    def logits(g, c):
        kc = kb_ref[g, pl.ds(pl.multiple_of(c * CK, CK), CK), :]
        qg = qs_ref[pl.ds(g * group, group)].reshape(group * QB, hd)
        st_refs[g % 2][...] = _dot_nt(kc, qg)

    logits(0, 0)

    def att_chunk(c, carry, near):
        msk = jnp.concatenate([sel_ref[c]] * group, axis=1) > 0.0
        tis = [jnp.clip(c * (CK // QB) + s - j + 2, 0, 2) for s in range(CK // QB)]
        for g in range(n_kv):
            if g + 1 < n_kv:
                logits(g + 1, c)
            else:
                logits(0, jnp.minimum(c + 1, nck - 1))
            lg = st_refs[g % 2][...] * scale2
            if near:
                rows = [jnp.concatenate([bias_ref[ti, g * group + hh] for hh in range(group)], axis=1)
                        for ti in tis]
                lg = lg + jnp.concatenate(rows, axis=0)
            lg = jnp.where(msk, lg, -jnp.inf)
            m_old = m_refs[g][...]
            m_new = jnp.maximum(m_old, jnp.max(_fold_rows(lg, jnp.maximum), axis=0, keepdims=True))
            alpha = jnp.exp2(m_old - m_new)
            p = jnp.exp2(lg - m_new)
            l_refs[g][...] = alpha * l_refs[g][...] + jnp.sum(_fold_rows(p, jnp.add), axis=0, keepdims=True)
            pv = jnp.dot(vt_ref[g, c], p.astype(BF16), preferred_element_type=F32)
            acc_refs[g][...] = alpha * acc_refs[g][...] + pv
            m_refs[g][...] = m_new
        return carry
    n_far = jnp.maximum((j * (QB // LANES) - 1) // (CK // QB), 0)
    lax.fori_loop(0, n_far, functools.partial(att_chunk, near=False), 0)
    lax.fori_loop(n_far, nck, functools.partial(att_chunk, near=True), 0)

    for g in range(n_kv):
        ot = acc_refs[g][...] / l_refs[g][...]
        for hh in range(group):
            cs = slice((g * group + hh) * hd, (g * group + hh + 1) * hd)
            o = ot[:, hh * QB:(hh + 1) * QB].T
            mixed = ga_ref[:, cs] * co_ref[:, cs] + gb_ref[:, cs] * o
            o_ref[:, cs] = mixed.astype(o_ref.dtype)


def attn_prompt(qi, kw, q, k, v, conv_out, gates, bias3, n_seq, seq, n_kv, idx_dim, idx_heads):
    m, dm = q.shape
    hd = k.shape[1] // n_kv
    n_heads = dm // hd
    group = n_heads // n_kv
    nb = seq // QB
    topk = min(TOPK_MAX, seq // 4)
    assert seq % CK == 0 and LANES % idx_dim == 0 and kw.shape[1] == LANES
    kern = functools.partial(_attn_prompt_kernel, seq=seq, n_heads=n_heads, n_kv=n_kv, idx_heads=idx_heads,
                             idx_dim=idx_dim, topk=topk, hd=hd)
    row = lambda b, j: (b * nb + j, 0)
    return pl.pallas_call(
        kern,
        grid=(n_seq, nb),
        in_specs=[pl.BlockSpec((QB, idx_heads * idx_dim), row),
                  pl.BlockSpec((QB, LANES), row),
                  pl.BlockSpec((seq, LANES), lambda b, j: (b, 0)),
                  pl.BlockSpec((QB, dm), row),
                  pl.BlockSpec((seq, n_kv * hd), lambda b, j: (b, 0)),
                  pl.BlockSpec((seq, n_kv * hd), lambda b, j: (b, 0)),
                  pl.BlockSpec((QB, dm), row),
                  pl.BlockSpec((QB, dm), lambda b, j: (b * nb + j, 0)),
                  pl.BlockSpec((QB, dm), lambda b, j: (b * nb + j, 1)),
                  pl.BlockSpec((3, n_heads, QB, QB), lambda b, j: (0, 0, 0, 0))],
        out_specs=pl.BlockSpec((QB, dm), row),
        out_shape=jax.ShapeDtypeStruct((m, dm), BF16),
        scratch_shapes=[pltpu.VMEM((seq, idx_dim), BF16),
                        pltpu.VMEM((n_kv, seq, hd), BF16),
                        pltpu.VMEM((n_kv, seq // CK, hd, CK), BF16),
                        pltpu.VMEM((LANES, QB), F32),
                        pltpu.VMEM((idx_heads, QB, idx_dim), BF16),
                        pltpu.VMEM((seq // CK, CK, QB), F32),
                        pltpu.VMEM((seq // CK, CK, QB), F32),
                        pltpu.VMEM((n_heads, QB, hd), BF16)]
                       + [pltpu.VMEM((1, group * QB), F32)] * (2 * n_kv)
                       + [pltpu.VMEM((hd, group * QB), F32)] * n_kv
                       + [pltpu.VMEM((CK, group * QB), F32)] * 2,
        compiler_params=_cparams(2),
        name="attn_prompt",
    )(qi, kw, kw, q, k, v, conv_out, gates, gates, bias3)


def _sample_score_kernel(pt_ref, qi_ref, wi_ref, kn_ref, *rest, pg, idx_heads, idx_dim):
    k_hbm, o_ref, self_ref, kbuf_ref, sem, kcat_ref = rest
    page = k_hbm.shape[2]
    b = pl.program_id(0)
    p = pl.program_id(1)
    n_steps = pl.num_programs(1)
    step = b * n_steps + p
    slot = step % 2

    def page_copies(bb, pp, sl):
        return [pltpu.make_async_copy(k_hbm.at[pt_ref[bb, pp * pg + i]], kbuf_ref.at[sl, i], sem.at[sl])
                for i in range(pg)]

    @pl.when(step == 0)
    def _():
        for c in page_copies(0, 0, 0):
            c.start()

    nxt = step + 1

    @pl.when(nxt < pl.num_programs(0) * n_steps)
    def _():
        for c in page_copies(nxt // n_steps, nxt % n_steps, 1 - slot):
            c.start()

    for c in page_copies(b, p, slot):
        c.wait()

    qi = qi_ref[...] * (idx_dim ** -0.5)
    wi = wi_ref[...] * (idx_heads ** -0.5)
    qb = qi.astype(BF16)
    for i in range(pg):
        kcat_ref[:, i * page:(i + 1) * page] = kbuf_ref[slot, i].astype(BF16)
    s = jnp.dot(qb, kcat_ref[...], preferred_element_type=F32)
    o_ref[...] = jnp.sum(wi * jnp.maximum(s, 0.0), axis=0, keepdims=True)

    @pl.when(pl.program_id(1) == 0)
    def _():
        kn = kn_ref[...].astype(BF16).astype(F32)
        s = jnp.sum(qb.astype(F32) * kn, axis=1, keepdims=True)
        sself = jnp.sum(wi * jnp.maximum(s, 0.0), axis=0, keepdims=True)
        self_ref[...] = jnp.broadcast_to(sself, self_ref.shape)


def sample_scores(page_table, qi3, wi3, ki_new3, cache_kidx_t, pg):
    n, n_pages = page_table.shape
    idx_heads, idx_dim = qi3.shape[1:]
    page = cache_kidx_t.shape[2]
    kern = functools.partial(_sample_score_kernel, pg=pg, idx_heads=idx_heads, idx_dim=idx_dim)
    grid_spec = pltpu.PrefetchScalarGridSpec(
        num_scalar_prefetch=1,
        grid=(n, n_pages // pg),
        in_specs=[pl.BlockSpec((None, idx_heads, idx_dim), lambda b, p, pt: (b, 0, 0)),
                  pl.BlockSpec((None, idx_heads, 1), lambda b, p, pt: (b, 0, 0)),
                  pl.BlockSpec((None, 1, idx_dim), lambda b, p, pt: (b, 0, 0)),
                  pl.BlockSpec(memory_space=pl.ANY)],
        out_specs=[pl.BlockSpec((None, None, 1, pg * page), lambda b, p, pt: (b, p, 0, 0)),
                   pl.BlockSpec((None, 1, LANES), lambda b, p, pt: (b, 0, 0))],
        scratch_shapes=[pltpu.VMEM((2, pg, idx_dim, page), F32), pltpu.SemaphoreType.DMA((2,)),
                        pltpu.VMEM((idx_dim, pg * page), BF16)],
    )
    return pl.pallas_call(
        kern,
        grid_spec=grid_spec,
        out_shape=[jax.ShapeDtypeStruct((n, n_pages // pg, 1, pg * page), F32),
                   jax.ShapeDtypeStruct((n, 1, LANES), F32)],
        compiler_params=_cparams(2),
        name="sample_scores",
    )(page_table, qi3, wi3, ki_new3, cache_kidx_t)


def _sample_select_kernel(sc_ref, self_ref, sel4_ref, selself_ref, sel_ref, *, topk, past, rep):
    sc = sc_ref[...]
    sself = self_ref[:, 0:1]
    n = sc.shape[0]
    kf = jnp.float32(topk)
    row_min = jnp.minimum(jnp.min(sc, axis=1, keepdims=True), sself)
    row_max = jnp.maximum(jnp.max(sc, axis=1, keepdims=True), sself)
    n_adm = jnp.full((n, 1), past + 1, F32)

    def count_gt(t):
        return (jnp.sum(jnp.where(sc > t, 1.0, 0.0), axis=1, keepdims=True)
                + jnp.where(sself > t, 1.0, 0.0))

    def any_fn(mask):
        return jnp.max(jnp.where(mask, 1.0, 0.0)) > 0.0

    lo, hi, flo, fhi = _select_threshold(count_gt, row_min, row_max, n_adm, topk, any_fn)
    tie = flo != kf
    sel_ref[...] = jnp.where(sc > lo, 1.0, 0.0)
    selself_ref[...] = jnp.broadcast_to(jnp.where(sself > lo, 1.0, 0.0), selself_ref.shape)

    @pl.when(any_fn(tie))
    def _():
        need = kf - fhi
        blk = 512
        tri = (lax.broadcasted_iota(jnp.int32, (blk, blk), 0)
               < lax.broadcasted_iota(jnp.int32, (blk, blk), 1)).astype(BF16)
        before = jnp.zeros((n, 1), F32)
        for c in range(past // blk):
            s = sc_ref[:, c * blk:(c + 1) * blk]
            eq = s == hi
            rank = before + jnp.dot(jnp.where(eq, 1.0, 0.0).astype(BF16), tri, preferred_element_type=F32)
            keep = (s > hi) | (eq & (rank < need))
            sel_ref[:, c * blk:(c + 1) * blk] = jnp.where(tie, jnp.where(keep, 1.0, 0.0),
                                                          sel_ref[:, c * blk:(c + 1) * blk])
            before = before + jnp.sum(jnp.where(eq, 1.0, 0.0), axis=1, keepdims=True)
        keep_self = (sself > hi) | ((sself == hi) & (before < need))
        selself_ref[...] = jnp.broadcast_to(
            jnp.where(tie, jnp.where(keep_self, 1.0, 0.0), jnp.where(sself > lo, 1.0, 0.0)), selself_ref.shape)

    blk = 512
    row_lo = lax.broadcasted_iota(jnp.int32, (blk, blk * rep), 0) * rep
    col = lax.broadcasted_iota(jnp.int32, (blk, blk * rep), 1)
    spread = jnp.where((col >= row_lo) & (col < row_lo + rep), 1.0, 0.0).astype(BF16)
    for c in range(past // blk):
        sel4_ref[:, c * blk * rep:(c + 1) * blk * rep] = jnp.dot(
            sel_ref[:, c * blk:(c + 1) * blk].astype(BF16), spread, preferred_element_type=F32)


def sample_select(scores, sself, topk, rep):
    n, past = scores.shape
    kern = functools.partial(_sample_select_kernel, topk=topk, past=past, rep=rep)
    return pl.pallas_call(
        kern,
        out_shape=[jax.ShapeDtypeStruct((n, past * rep), F32), jax.ShapeDtypeStruct((n, LANES), F32)],
        scratch_shapes=[pltpu.VMEM((n, past), F32)],
        compiler_params=pltpu.CompilerParams(vmem_limit_bytes=VMEM_LIMIT),
        name="sample_select",
    )(scores, sself)


def _sample_attn_kernel(pt_ref, q_ref, kn_ref, vn_ref, sel_ref, selself_ref, bias_ref, rb0_ref, own_ref, *rest,
                        pg, n_heads, n_kv, hd):
    k_hbm, v_hbm, o_ref, kbuf_ref, vbuf_ref, sem, kcat_ref, vcat_ref, m_ref, l_ref, acc_ref = rest
    b = pl.program_id(0)
    p = pl.program_id(1)
    n_steps = pl.num_programs(1)
    scale = hd ** -0.5
    rows = k_hbm.shape[1]
    step = b * n_steps + p
    slot = step % 2

    def page_copies(bb, pp, sl):
        copies = []
        for i in range(pg):
            pid = pt_ref[bb, pp * pg + i]
            dst = pl.ds(i * rows, rows)
            copies.append(pltpu.make_async_copy(k_hbm.at[pid], kbuf_ref.at[sl, dst], sem.at[0, sl]))
            copies.append(pltpu.make_async_copy(v_hbm.at[pid], vbuf_ref.at[sl, dst], sem.at[1, sl]))
        return copies

    @pl.when(step == 0)
    def _():
        for c in page_copies(0, 0, 0):
            c.start()

    nxt = step + 1

    @pl.when(nxt < pl.num_programs(0) * n_steps)
    def _():
        for c in page_copies(nxt // n_steps, nxt % n_steps, 1 - slot):
            c.start()

    for c in page_copies(b, p, slot):
        c.wait()

    @pl.when(p == 0)
    def _():
        m_ref[...] = jnp.full(m_ref.shape, NEG, F32)
        l_ref[...] = jnp.zeros(l_ref.shape, F32)
        acc_ref[...] = jnp.zeros(acc_ref.shape, F32)

    for i in range(pg):
        kcat_ref[i * rows:(i + 1) * rows, :] = kbuf_ref[slot, pl.ds(i * rows, rows), :].astype(BF16)
        vcat_ref[i * rows:(i + 1) * rows, :] = vbuf_ref[slot, pl.ds(i * rows, rows), :].astype(BF16)
    qb = q_ref[...].astype(BF16)
    last = p == n_steps - 1
    bias = jnp.concatenate([bias_ref[0]] * (pg - 1) + [jnp.where(last, bias_ref[1], bias_ref[0])], axis=1)
    msk = (own_ref[...] > 0.0) & (sel_ref[...] > 0.0)
    lg = jnp.where(msk, _dot_nt(qb, kcat_ref[...]) * scale + bias, -jnp.inf)
    m_old = m_ref[...]
    m_new = jnp.maximum(m_old, jnp.max(lg, axis=-1, keepdims=True))
    alpha = jnp.exp(m_old - m_new)
    pr = jnp.exp(lg - m_new)
    l_new = alpha * l_ref[...] + jnp.sum(pr, axis=-1, keepdims=True)
    acc = alpha * acc_ref[...] + jnp.dot(pr.astype(BF16), vcat_ref[...], preferred_element_type=F32)
    m_ref[...] = m_new
    l_ref[...] = l_new
    acc_ref[...] = acc

    @pl.when(last)
    def _():
        kn = kn_ref[...].astype(BF16).astype(F32)
        vn = vn_ref[...].astype(BF16).astype(F32)
        ls = jnp.sum(qb.astype(F32) * kn, axis=-1, keepdims=True) * scale + rb0_ref[...]
        on = selself_ref[:, 0:1] > 0.0
        ls = jnp.where(on, ls, NEG)
        m_f = jnp.maximum(m_new, ls)
        a2 = jnp.exp(m_new - m_f)
        ps = jnp.where(on, jnp.exp(ls - m_f), 0.0)
        l_f = a2 * l_new + ps
        acc_f = a2 * acc + ps.astype(BF16).astype(F32) * vn
        o_ref[...] = acc_f / l_f


def sample_attention(page_table, q3, k_rep, v_rep, sel4, selself3, bias_s, rb0, cache_k, cache_v, pg, n_kv):
    n, n_pages = page_table.shape
    n_heads, hd = q3.shape[1:]
    rows = cache_k.shape[1]
    kern = functools.partial(_sample_attn_kernel, pg=pg, n_heads=n_heads, n_kv=n_kv, hd=hd)
    own = (np.arange(pg * rows)[None, :] % n_kv
           == np.arange(n_heads)[:, None] // (n_heads // n_kv)).astype(np.float32)
    hbm = pl.BlockSpec(memory_space=pl.ANY)
    per_seq = lambda b, p, pt: (b, 0, 0)
    grid_spec = pltpu.PrefetchScalarGridSpec(
        num_scalar_prefetch=1,
        grid=(n, n_pages // pg),
        in_specs=[pl.BlockSpec((None, n_heads, hd), per_seq),
                  pl.BlockSpec((None, n_heads, hd), per_seq),
                  pl.BlockSpec((None, n_heads, hd), per_seq),
                  pl.BlockSpec((None, None, 1, pg * rows), lambda b, p, pt: (b, p, 0, 0)),
                  pl.BlockSpec((None, 1, LANES), per_seq),
                  pl.BlockSpec((2, n_heads, rows), lambda b, p, pt: (0, 0, 0)),
                  pl.BlockSpec((n_heads, 1), lambda b, p, pt: (0, 0)),
                  pl.BlockSpec((n_heads, pg * rows), lambda b, p, pt: (0, 0)), hbm, hbm],
        out_specs=pl.BlockSpec((None, n_heads, hd), per_seq),
        scratch_shapes=[pltpu.VMEM((2, pg * rows, hd), F32), pltpu.VMEM((2, pg * rows, hd), F32),
                        pltpu.SemaphoreType.DMA((2, 2)),
                        pltpu.VMEM((pg * rows, hd), BF16), pltpu.VMEM((pg * rows, hd), BF16),
                        pltpu.VMEM((n_heads, 1), F32), pltpu.VMEM((n_heads, 1), F32),
                        pltpu.VMEM((n_heads, hd), F32)],
    )
    return pl.pallas_call(
        kern,
        grid_spec=grid_spec,
        out_shape=jax.ShapeDtypeStruct((n, n_heads, hd), F32),
        compiler_params=_cparams(2),
        name="sample_attn",
    )(page_table, q3, k_rep, v_rep, sel4, selself3, bias_s, rb0, jnp.asarray(own), cache_k, cache_v)


def _mix_kernel(ga_ref, gb_ref, co_ref, at_ref, o_ref):
    o_ref[...] = (ga_ref[...] * co_ref[...] + gb_ref[...] * at_ref[...]).astype(o_ref.dtype)


def gated_mix(gates, conv_out, attn):
    n, dm = conv_out.shape
    return pl.pallas_call(
        _mix_kernel,
        grid=(1,),
        in_specs=[pl.BlockSpec((n, dm), lambda i: (0, 0)), pl.BlockSpec((n, dm), lambda i: (0, 1)),
                  pl.BlockSpec((n, dm), lambda i: (0, 0)), pl.BlockSpec((n, dm), lambda i: (0, 0))],
        out_specs=pl.BlockSpec((n, dm), lambda i: (0, 0)),
        out_shape=jax.ShapeDtypeStruct((n, dm), BF16),
    )(gates, gates, conv_out, attn)


def _ffn_up_kernel(x_ref, xh_ref, wg_ref, wv_ref, dg_ref, dv_ref, bg_ref, bv_ref, pg_ref, pv_ref,
                   o_ref, sg_ref, sv_ref, eg_ref, ev_ref, wgb_ref, wvb_ref, *, tm, rs, width, tiles_per_seq):
    hist = width - 1
    pad = xh_ref.shape[0]
    first = pl.program_id(0) % tiles_per_seq == 0
    kc = 256
    for wf_ref, wb_ref in ((wg_ref, wgb_ref), (wv_ref, wvb_ref)):
        for k0 in range(0, wf_ref.shape[0], kc):
            wb_ref[k0:k0 + kc, :] = wf_ref[k0:k0 + kc, :].astype(BF16)
    branches = ((wgb_ref, dg_ref, bg_ref, pg_ref, eg_ref, sg_ref),
                (wvb_ref, dv_ref, bv_ref, pv_ref, ev_ref, sv_ref))
    for w_ref, _, _, prev_ref, e_ref, _ in branches:
        e_ref[0, pl.ds(0, pad), :] = jnp.dot(xh_ref[...], w_ref[...], preferred_element_type=F32)

        @pl.when(first)
        def _():
            e_ref[0, pl.ds(pad - hist, hist), :] = prev_ref[...]

    def project(r):
        for w_ref, _, _, _, e_ref, _ in branches:
            e_ref[r % 2, pl.ds(pad, rs), :] = jnp.dot(x_ref[pl.ds(r * rs, rs), :], w_ref[...],
                                                      preferred_element_type=F32)

    def hand_over(r):
        for _, _, _, _, e_ref, _ in branches:
            e_ref[(r + 1) % 2, pl.ds(0, pad), :] = e_ref[r % 2, pl.ds(rs, pad), :]

    def activate(r):
        outs = []
        for _, d_ref, b_ref, _, e_ref, _ in branches:
            acc = b_ref[...] + d_ref[pl.ds(hist, 1), :] * e_ref[r % 2, pl.ds(pad, rs), :]
            for j in range(hist):
                acc = acc + d_ref[pl.ds(j, 1), :] * e_ref[r % 2, pl.ds(pad - hist + j, rs), :]
            outs.append(acc)
        g, v = outs
        o_ref[pl.ds(r * rs, rs), :] = (g * _sigmoid(g) * v).astype(o_ref.dtype)

    n_sub = tm // rs
    project(0)
    hand_over(0)
    for r in range(1, n_sub):
        project(r)
        activate(r - 1)
        hand_over(r)
    activate(n_sub - 1)
    for _, _, _, _, e_ref, s_ref in branches:
        s_ref[...] = e_ref[(n_sub - 1) % 2, pl.ds(pad + rs - hist, hist), :]


def ffn_up_act(xn, w_up, prev, dw, bdw, n_seq, seq, tm, tn):
    m, k = xn.shape
    f2 = w_up.shape[1]
    f = f2 // 2
    width = dw.shape[0]
    ncb = f // tn
    pad = 16
    tps = seq // tm
    assert seq % tm == 0 and tm % pad == 0 and f % tn == 0 and width - 1 <= pad
    rs = min(256, tm // 2)
    assert tm % rs == 0 and rs % 8 == 0 and k % 256 == 0
    kern = functools.partial(_ffn_up_kernel, tm=tm, rs=rs, width=width, tiles_per_seq=tps)
    hb = tm // pad
    st_shape = jax.ShapeDtypeStruct((m // tm, width - 1, f), F32)
    act, sg, sv = pl.pallas_call(
        kern,
        grid=(m // tm, ncb),
        in_specs=[pl.BlockSpec((tm, k), lambda i, c: (i, 0)),
                  pl.BlockSpec((pad, k), lambda i, c: (jnp.maximum(i * hb - 1, 0), 0)),
                  pl.BlockSpec((k, tn), lambda i, c: (0, c)),
                  pl.BlockSpec((k, tn), lambda i, c: (0, c + ncb)),
                  pl.BlockSpec((width, tn), lambda i, c: (0, c)),
                  pl.BlockSpec((width, tn), lambda i, c: (0, c + ncb)),
                  pl.BlockSpec((1, tn), lambda i, c: (0, c)),
                  pl.BlockSpec((1, tn), lambda i, c: (0, c + ncb)),
                  pl.BlockSpec((None, width - 1, tn), lambda i, c: (i // tps, 0, c)),
                  pl.BlockSpec((None, width - 1, tn), lambda i, c: (i // tps, 0, c + ncb))],
        out_specs=[pl.BlockSpec((tm, tn), lambda i, c: (i, c)),
                   pl.BlockSpec((None, width - 1, tn), lambda i, c: (i, 0, c)),
                   pl.BlockSpec((None, width - 1, tn), lambda i, c: (i, 0, c))],
        out_shape=[jax.ShapeDtypeStruct((m, f), BF16), st_shape, st_shape],
        scratch_shapes=[pltpu.VMEM((2, pad + rs, tn), F32), pltpu.VMEM((2, pad + rs, tn), F32),
                        pltpu.VMEM((k, tn), BF16), pltpu.VMEM((k, tn), BF16)],
        compiler_params=_cparams(2),
        name="ffn_up_act",
    )(xn, xn, w_up, w_up, dw, dw, bdw.reshape(1, f2), bdw.reshape(1, f2), prev, prev)
    state = jnp.concatenate([sg, sv], axis=-1).reshape(n_seq, tps, width - 1, f2)[:, tps - 1]
    return act, state


def _ffn_act_step_kernel(ug_ref, uv_ref, pg_ref, pv_ref, wg_ref, wv_ref, bg_ref, bv_ref, o_ref, *, width):
    def conv(u_ref, prev_ref, w_ref, b_ref):
        acc = b_ref[...] + w_ref[pl.ds(width - 1, 1), :] * u_ref[...]
        for j in range(width - 1):
            acc = acc + w_ref[pl.ds(j, 1), :] * prev_ref[j]
        return acc
    g = conv(ug_ref, pg_ref, wg_ref, bg_ref)
    v = conv(uv_ref, pv_ref, wv_ref, bv_ref)
    o_ref[...] = (g * _sigmoid(g) * v).astype(o_ref.dtype)


def ffn_act_step(u, prev_t, dw, bdw, tc):
    n, f2 = u.shape
    f = f2 // 2
    width = dw.shape[0]
    ncb = f // tc
    kern = functools.partial(_ffn_act_step_kernel, width=width)
    return pl.pallas_call(
        kern,
        grid=(ncb,),
        in_specs=[pl.BlockSpec((n, tc), lambda c: (0, c)),
                  pl.BlockSpec((n, tc), lambda c: (0, c + ncb)),
                  pl.BlockSpec((width - 1, n, tc), lambda c: (0, 0, c)),
                  pl.BlockSpec((width - 1, n, tc), lambda c: (0, 0, c + ncb)),
                  pl.BlockSpec((width, tc), lambda c: (0, c)),
                  pl.BlockSpec((width, tc), lambda c: (0, c + ncb)),
                  pl.BlockSpec((1, tc), lambda c: (0, c)),
                  pl.BlockSpec((1, tc), lambda c: (0, c + ncb))],
        out_specs=pl.BlockSpec((n, tc), lambda c: (0, c)),
        out_shape=jax.ShapeDtypeStruct((n, f), BF16),
        compiler_params=_cparams(1),
    )(u, u, prev_t, prev_t, dw, dw, bdw.reshape(1, f2), bdw.reshape(1, f2))


def _row_tile(m, cap):
    tm = cap
    while m % tm:
        tm //= 2
    return tm if tm >= 16 else m


def _in_projection(xn, w_main, w_kw, w_g, sizes, tm):
    glu_w, q_w, k_w, v_w, qi_w, ki_w, wi_w, ga_w, gb_w = sizes
    offs = np.concatenate([[0], np.cumsum(sizes)])
    tn = 512
    t = dict(w_transposed=True)
    glu = matmul_w(xn, w_main, int(offs[0]), glu_w, tm, tn, name="mm_glu", **t)
    q = matmul_w(xn, w_main, int(offs[1]), q_w, tm, tn, name="mm_q", out_dtype=BF16, **t)
    k = matmul_w(xn, w_main, int(offs[2]), k_w, tm, min(tn, k_w), name="mm_k", **t)
    v = matmul_w(xn, w_main, int(offs[3]), v_w, tm, min(tn, v_w), name="mm_v", **t)
    qi = matmul_w(xn, w_main, int(offs[4]), qi_w, tm, min(tn, qi_w), name="mm_qi", out_dtype=BF16, **t)
    kw = matmul_w(xn, w_kw, 0, LANES, tm, LANES, name="mm_kw", **t)
    gates = matmul_w(xn, w_g, 0, ga_w + gb_w, tm, tn, name="mm_gates", sigmoid=True, **t)
    return glu, q, k, v, qi, kw, gates


def kernel(x_prompt, x_sample, cache_k, cache_v, cache_kidx, state_conv, state_ffn, page_table, rel_bias,
           norm_attn, w_in, dw_conv, b_dw_conv, ln_conv_g, ln_conv_b, w_conv_out, w_o, norm_ffn, w_up, dw_ffn,
           b_dw_ffn, w_down, norm_final):
    bsz, seq, dm = x_prompt.shape
    nd, dec_seq, _ = x_sample.shape
    depth, n_pool, page, n_kv, hd = cache_k.shape
    idx_dim = cache_kidx.shape[-1]
    n_pages = page_table.shape[1]
    past = n_pages * page
    width, dconv = dw_conv.shape[1:]
    fwidth = dw_ffn.shape[1]
    f = w_down.shape[1]
    n_heads = w_o.shape[1] // hd
    d_attn = n_heads * hd
    d_kv = n_kv * hd
    n_in = w_in.shape[2]
    idx_heads = (n_in - 2 * dconv - d_attn - 2 * d_kv - idx_dim - 2 * dm) // (idx_dim + 1)
    sizes = (2 * dconv, d_attn, d_kv, d_kv, idx_heads * idx_dim, idx_dim, idx_heads, dm, dm)
    assert sum(sizes) == n_in and depth == 1 and dec_seq == 1 and page == LANES and d_attn == dm

    mp = bsz * seq
    xp = x_prompt.reshape(mp, dm)
    xs = x_sample.reshape(nd, dm)
    bias3, bias_s = bias_tables(rel_bias, page, n_kv)
    tmp = _row_tile(mp, 2048)
    drop = lambda a: a.reshape(a.shape[1:])
    (norm_attn, w_in, dw_conv, b_dw_conv, ln_conv_g, ln_conv_b, w_conv_out, w_o, norm_ffn, w_up, dw_ffn,
     b_dw_ffn, w_down, state_conv, state_ffn) = map(drop, (
         norm_attn, w_in, dw_conv, b_dw_conv, ln_conv_g, ln_conv_b, w_conv_out, w_o, norm_ffn, w_up, dw_ffn,
         b_dw_ffn, w_down, state_conv, state_ffn))
    kidx_pool = jnp.swapaxes(cache_kidx.reshape(n_pool, page, idx_dim), 1, 2)
    k_pool = cache_k.reshape(n_pool, page * n_kv, hd)
    v_pool = cache_v.reshape(n_pool, page * n_kv, hd)

    n_aligned = sum(sizes[:5])
    n_small = idx_dim + idx_heads
    w_in_t = jnp.swapaxes(w_in, 0, 1)
    w_main = cast_rows_bf16(w_in_t, 0, n_aligned)
    w_kw = cast_rows_bf16(w_in_t, n_aligned, LANES, n_small)
    w_g = cast_rows_bf16(w_in_t, n_aligned + n_small, 2 * dm)
    w_o, w_down = cast_bf16(w_o), cast_bf16(w_down)

    xn = rmsnorm_rows(xp, norm_attn, BF16, 512)
    glu, q, k, v, qi, kw, gates = _in_projection(xn, w_main, w_kw, w_g, sizes, tmp)
    conv0 = jnp.zeros((bsz, width - 1, dconv), F32)
    conv_out, conv_state_p = conv_branch_prompt(glu, conv0, dw_conv, b_dw_conv, ln_conv_g, ln_conv_b,
                                                w_conv_out, bsz, seq, 256)
    mixed = attn_prompt(qi, kw, q, k, v, conv_out, gates, bias3, bsz, seq, n_kv, idx_dim, idx_heads)
    x2, xn2 = matmul_res_norm(mixed, w_o, xp, norm_ffn, _row_tile(mp, 512))
    ffn0 = jnp.zeros((bsz, fwidth - 1, 2 * f), F32)
    act, ffn_state_p = ffn_up_act(xn2, w_up, ffn0, dw_ffn, b_dw_ffn, bsz, seq, _row_tile(seq, 2048), 512)
    tk_down = f // 4 if f % (4 * LANES) == 0 else f
    tk_p = 512 if f % 512 == 0 else tk_down
    y_prompt = matmul_res_norm_out(act, w_down, x2, norm_final, _row_tile(mp, 1024), tk_p).reshape(bsz, seq, dm)

    xns = rmsnorm_rows(xs, norm_attn, BF16, nd)
    glu_s, q_s, k_s, v_s, qi_s, kw_s, gates_s = _in_projection(xns, w_main, w_kw, w_g, sizes, nd)
    ki_s = kw_s[:, :idx_dim]
    wi_s = kw_s[:, idx_dim:idx_dim + idx_heads]
    sc_prev_t = jnp.swapaxes(state_conv, 0, 1)
    conv_out_s, u_conv_s = conv_branch_step(glu_s, sc_prev_t, dw_conv, b_dw_conv, ln_conv_g, ln_conv_b,
                                            w_conv_out)
    conv_state_s = jnp.concatenate([state_conv[:, 1:], u_conv_s[:, None, :]], axis=1)

    pg = 16 if n_pages % 16 == 0 else 8
    pg_idx = 32 if n_pages % 32 == 0 else pg
    scores3, sself3 = sample_scores(page_table, qi_s.reshape(nd, idx_heads, idx_dim),
                                    wi_s.reshape(nd, idx_heads, 1), ki_s.reshape(nd, 1, idx_dim), kidx_pool,
                                    pg_idx)
    topk_s = min(TOPK_MAX, (past + dec_seq) // 4)
    sel4, selself = sample_select(scores3.reshape(nd, past), sself3.reshape(nd, LANES), topk_s, n_kv)
    group = n_heads // n_kv
    k_rep = jnp.repeat(k_s.reshape(nd, n_kv, hd), group, axis=1)
    v_rep = jnp.repeat(v_s.reshape(nd, n_kv, hd), group, axis=1)
    attn_s = sample_attention(page_table, q_s.reshape(nd, n_heads, hd), k_rep, v_rep,
                              sel4.reshape(nd, n_pages // pg, 1, pg * page * n_kv), selself.reshape(nd, 1, LANES),
                              bias_s,
                              rel_bias[0].reshape(n_heads, 1), k_pool, v_pool, pg, n_kv)
    mixed_s = gated_mix(gates_s, conv_out_s, attn_s.reshape(nd, dm))
    x2s, xn2s = matmul_res_norm(mixed_s, w_o, xs, norm_ffn, nd)
    u_s = matmul_w(xn2s, w_up, 0, 2 * f, nd, 512)
    sf_prev_t = jnp.swapaxes(state_ffn, 0, 1)
    act_s = ffn_act_step(u_s, sf_prev_t, dw_ffn, b_dw_ffn, 512)
    y_sample = matmul_res_norm_out(act_s, w_down, x2s, norm_final, nd, tk_down).reshape(nd, dec_seq, dm)
    ffn_state_s = jnp.concatenate([state_ffn[:, 1:], u_s[:, None, :]], axis=1)

    return (y_prompt, y_sample,
            k.reshape(1, bsz, seq, n_kv, hd), v.reshape(1, bsz, seq, n_kv, hd),
            kw.reshape(bsz, seq, LANES)[None, :, :, :idx_dim],
            conv_state_p[None], ffn_state_p[None],
            k_s.reshape(1, nd, dec_seq, n_kv, hd), v_s.reshape(1, nd, dec_seq, n_kv, hd),
            ki_s.reshape(1, nd, dec_seq, idx_dim),
            conv_state_s[None], ffn_state_s[None])
```

```python
import functools
import math

import numpy as np
import jax
import jax.numpy as jnp
from jax import lax
from jax.experimental import pallas as pl
from jax.experimental.pallas import tpu as pltpu

F32 = jnp.float32
BF16 = jnp.bfloat16

EPS = 1e-6
TOPK_MAX = 256
N_BUCKETS = 32
MAX_DISTANCE = 128
QB = 128
CK = 256
LANES = 128
NEG = -1e30
LOG2E = math.log2(math.e)
VMEM_LIMIT = 56 * 1024 * 1024


def _cparams(n_axes, vmem=VMEM_LIMIT):
    return pltpu.CompilerParams(dimension_semantics=("arbitrary",) * n_axes, vmem_limit_bytes=vmem)


def _dot_nt(a, b):
    return lax.dot_general(a, b, (((1,), (1,)), ((), ())), preferred_element_type=F32)


def _sigmoid(x):
    return 1.0 / (1.0 + jnp.exp(-x))


def _fold_rows(x, op):
    while x.shape[0] > 8:
        half = x.shape[0] // 2
        x = op(x[:half], x[half:])
    return x


def _rel_bucket_np(dist):
    n = np.maximum(dist, 0)
    max_exact = N_BUCKETS // 2
    nf = np.maximum(n, 1).astype(np.float32)
    large = max_exact + (np.log(nf / np.float32(max_exact)) / np.float32(math.log(MAX_DISTANCE / max_exact))
                         * np.float32(N_BUCKETS - max_exact)).astype(np.int32)
    large = np.minimum(large, N_BUCKETS - 1)
    return np.where(n < max_exact, n, large).astype(np.int32)


def _rms_kernel(x_ref, g_ref, o_ref):
    x = x_ref[...]
    y = x * lax.rsqrt(jnp.mean(x * x, axis=-1, keepdims=True) + EPS) * g_ref[...]
    o_ref[...] = y.astype(o_ref.dtype)


def rmsnorm_rows(x, g, out_dtype, tm):
    m, d = x.shape
    return pl.pallas_call(
        _rms_kernel,
        grid=(m // tm,),
        in_specs=[pl.BlockSpec((tm, d), lambda i: (i, 0)), pl.BlockSpec((1, d), lambda i: (0, 0))],
        out_specs=pl.BlockSpec((tm, d), lambda i: (i, 0)),
        out_shape=jax.ShapeDtypeStruct((m, d), out_dtype),
        compiler_params=_cparams(1),
        name="rmsnorm",
    )(x, g.reshape(1, d))


def _cast_kernel(w_ref, o_ref):
    o_ref[...] = w_ref[...].astype(o_ref.dtype)


def cast_bf16(w, ncols=None):
    k, n = w.shape
    ncols = n if ncols is None else ncols
    tk = 512 if k % 512 == 0 else k
    tn = 1024 if ncols % 1024 == 0 else (512 if ncols % 512 == 0 else ncols)
    return pl.pallas_call(
        _cast_kernel,
        grid=(k // tk, ncols // tn),
        in_specs=[pl.BlockSpec((tk, tn), lambda i, j: (i, j))],
        out_specs=pl.BlockSpec((tk, tn), lambda i, j: (i, j)),
        out_shape=jax.ShapeDtypeStruct((k, ncols), BF16),
        compiler_params=_cparams(2),
        name="cast_bf16",
    )(w)


def _cast_rows_kernel(w_ref, o_ref, *, valid):
    w = w_ref[...]
    if valid < w.shape[0]:
        w = jnp.where(lax.broadcasted_iota(jnp.int32, w.shape, 0) < valid, w, 0.0)
    o_ref[...] = w.astype(o_ref.dtype)


def cast_rows_bf16(wt, row0, nrows, nvalid=None):
    n, k = wt.shape
    tr = 512 if nrows % 512 == 0 else nrows
    nvalid = nrows if nvalid is None else nvalid
    assert row0 % 8 == 0 and (nvalid == nrows or tr == nrows) and row0 + nrows <= n
    return pl.pallas_call(
        functools.partial(_cast_rows_kernel, valid=nvalid),
        grid=(nrows // tr,),
        in_specs=[pl.BlockSpec((pl.Element(tr), pl.Element(k)), lambda i: (pl.multiple_of(row0 + i * tr, 8), 0))],
        out_specs=pl.BlockSpec((tr, k), lambda i: (i, 0)),
        out_shape=jax.ShapeDtypeStruct((nrows, k), BF16),
        compiler_params=_cparams(1),
        name="cast_rows_bf16",
    )(wt)


def _mm_kernel(a_ref, w_ref, o_ref):
    o_ref[...] = jnp.dot(a_ref[...], w_ref[...].astype(BF16), preferred_element_type=F32).astype(o_ref.dtype)


def _mm_nt_kernel(a_ref, wt_ref, o_ref, *, sigmoid):
    y = _dot_nt(a_ref[...], wt_ref[...])
    o_ref[...] = (_sigmoid(y) if sigmoid else y).astype(o_ref.dtype)


def matmul_w(a, w, col0, ncols, tm, tn, name="matmul", out_dtype=F32, w_transposed=False, sigmoid=False):
    m, k = a.shape
    assert col0 % tn == 0 and ncols % tn == 0 and m % tm == 0 and (w_transposed or not sigmoid)
    cb = col0 // tn
    if w_transposed:
        w_spec = pl.BlockSpec((tn, k), lambda i, j: (j + cb, 0))
    else:
        w_spec = pl.BlockSpec((k, tn), lambda i, j: (0, j + cb))
    return pl.pallas_call(
        functools.partial(_mm_nt_kernel, sigmoid=sigmoid) if w_transposed else _mm_kernel,
        grid=(m // tm, ncols // tn),
        in_specs=[pl.BlockSpec((tm, k), lambda i, j: (i, 0)), w_spec],
        out_specs=pl.BlockSpec((tm, tn), lambda i, j: (i, j)),
        out_shape=jax.ShapeDtypeStruct((m, ncols), out_dtype),
        compiler_params=_cparams(2),
        name=name,
    )(a, w)


def _rms(x, g):
    return x * lax.rsqrt(jnp.mean(x * x, axis=-1, keepdims=True) + EPS) * g


def _mm_res_norm_kernel(a_ref, w_ref, r_ref, g_ref, o_ref, on_ref):
    x = r_ref[...] + jnp.dot(a_ref[...], w_ref[...], preferred_element_type=F32)
    o_ref[...] = x
    on_ref[...] = _rms(x, g_ref[...]).astype(on_ref.dtype)


def matmul_res_norm(a, w, res, g, tm):
    m, k = a.shape
    n = w.shape[1]
    return pl.pallas_call(
        _mm_res_norm_kernel,
        grid=(m // tm,),
        in_specs=[pl.BlockSpec((tm, k), lambda i: (i, 0)), pl.BlockSpec((k, n), lambda i: (0, 0)),
                  pl.BlockSpec((tm, n), lambda i: (i, 0)), pl.BlockSpec((1, n), lambda i: (0, 0))],
        out_specs=[pl.BlockSpec((tm, n), lambda i: (i, 0)), pl.BlockSpec((tm, n), lambda i: (i, 0))],
        out_shape=[jax.ShapeDtypeStruct((m, n), F32), jax.ShapeDtypeStruct((m, n), BF16)],
        compiler_params=_cparams(1),
        name="mm_o_norm",
    )(a, w, res, g.reshape(1, n))


def _mm_ksplit_norm_kernel(a_ref, w_ref, r_ref, g_ref, o_ref, acc_ref):
    kk = pl.program_id(1)

    @pl.when(kk == 0)
    def _():
        acc_ref[...] = r_ref[...]
    acc_ref[...] += jnp.dot(a_ref[...], w_ref[...], preferred_element_type=F32)

    @pl.when(kk == pl.num_programs(1) - 1)
    def _():
        o_ref[...] = _rms(acc_ref[...], g_ref[...])


def matmul_res_norm_out(a, w, res, g, tm, tk):
    m, k = a.shape
    n = w.shape[1]
    assert k % tk == 0 and m % tm == 0
    return pl.pallas_call(
        _mm_ksplit_norm_kernel,
        grid=(m // tm, k // tk),
        in_specs=[pl.BlockSpec((tm, tk), lambda i, kk: (i, kk)), pl.BlockSpec((tk, n), lambda i, kk: (kk, 0)),
                  pl.BlockSpec((tm, n), lambda i, kk: (i, 0)), pl.BlockSpec((1, n), lambda i, kk: (0, 0))],
        out_specs=pl.BlockSpec((tm, n), lambda i, kk: (i, 0)),
        out_shape=jax.ShapeDtypeStruct((m, n), F32),
        scratch_shapes=[pltpu.VMEM((tm, n), F32)],
        compiler_params=_cparams(2),
        name="mm_down_norm",
    )(a, w, res, g.reshape(1, n))


def _bias_kernel(rb_ref, bk3_ref, bks_ref, o3_ref, os_ref, *, n_heads):
    def head(h, carry):
        far = rb_ref[N_BUCKETS - 1, h]
        for t in range(3):
            bk = bk3_ref[t]
            acc = jnp.zeros(bk.shape, F32)
            for b in range(N_BUCKETS):
                acc = jnp.where(bk == b, rb_ref[b, h], acc)
            o3_ref[t, h] = (acc - far) * LOG2E
        for t in range(2):
            bk = bks_ref[t]
            acc = jnp.zeros(bk.shape, F32)
            for b in range(N_BUCKETS):
                acc = jnp.where(bk == b, rb_ref[b, h], acc)
            os_ref[t, pl.ds(h, 1), :] = acc
        return carry
    lax.fori_loop(0, n_heads, head, 0)


def bias_tables(rel_bias, page, rep):
    n_heads = rel_bias.shape[1]
    cols = page * rep
    i = np.arange(QB)[None, :]
    k = np.arange(QB)[:, None]
    bk3 = np.stack([_rel_bucket_np(i - k + 2 * QB), _rel_bucket_np(i - k + QB), _rel_bucket_np(i - k)])
    assert (_rel_bucket_np(np.arange(QB + 1, 1 << 20)) == N_BUCKETS - 1).all()
    assert (bk3[0] == N_BUCKETS - 1).all()
    assert page >= QB
    bks = np.stack([np.full((1, cols), N_BUCKETS - 1, np.int32),
                    _rel_bucket_np(page - np.arange(cols) // rep)[None, :]])
    return pl.pallas_call(
        functools.partial(_bias_kernel, n_heads=n_heads),
        in_specs=[pl.BlockSpec(memory_space=pltpu.SMEM),
                  pl.BlockSpec(memory_space=pltpu.VMEM), pl.BlockSpec(memory_space=pltpu.VMEM)],
        out_specs=[pl.BlockSpec(memory_space=pltpu.VMEM), pl.BlockSpec(memory_space=pltpu.VMEM)],
        out_shape=[jax.ShapeDtypeStruct((3, n_heads, QB, QB), F32),
                   jax.ShapeDtypeStruct((2, n_heads, cols), F32)],
    )(rel_bias, jnp.asarray(bk3), jnp.asarray(bks))


def _conv_kernel(glu_ref, prev_ref, dw_ref, bdw_ref, lng_ref, lnb_ref, wout_ref, o_ref, st_ref,
                 ext_ref, h_ref, wbf_ref, sh_ref, h2_ref, *, tt, width, dconv):
    b = pl.program_id(0)
    t = pl.program_id(1)
    pad = 32
    hist = width - 1

    @pl.when((b == 0) & (t == 0))
    def _():
        wbf_ref[...] = wout_ref[...].astype(BF16)

    @pl.when(t == 0)
    def _():
        ext_ref[pl.ds(pad - hist, hist), :] = prev_ref[...]

    @pl.when(t > 0)
    def _():
        ext_ref[pl.ds(0, pad), :] = ext_ref[pl.ds(tt, pad), :]

    glu = glu_ref[...]
    u = glu[:, :dconv] * _sigmoid(glu[:, dconv:])
    ext_ref[pl.ds(pad, tt), :] = u
    st_ref[...] = ext_ref[pl.ds(pad + tt - hist, hist), :]

    span = tt + pad - 8
    for s in range(1, 8):
        sh_ref[s - 1] = ext_ref[pl.ds(s, span), :]

    half = tt // 2
    for r, hh_ref in enumerate((h_ref, h2_ref)):
        r0 = r * half
        for c in range(dconv // LANES):
            cs = slice(c * LANES, (c + 1) * LANES)
            acc = jnp.zeros((half, LANES), F32) + bdw_ref[:, cs]
            for j in range(width):
                a, s = divmod(pad - hist + j, 8)
                src = ext_ref if s == 0 else sh_ref.at[s - 1]
                acc = acc + dw_ref[pl.ds(j, 1), cs] * src[pl.ds(8 * a + r0, half), cs]
            hh_ref[:, cs] = acc
        h = hh_ref[...]
        mu = jnp.mean(h, axis=-1, keepdims=True)
        var = jnp.mean(jnp.square(h - mu), axis=-1, keepdims=True)
        y = (h - mu) * lax.rsqrt(var + EPS) * lng_ref[...] + lnb_ref[...]
        y = y * _sigmoid(y)
        o_ref[pl.ds(r0, half), :] = jnp.dot(y.astype(BF16), wbf_ref[...], preferred_element_type=F32)


def conv_branch_prompt(glu_pre, prev, dw, bdw, lng, lnb, wout, n_seq, seq, tt):
    width, dconv = dw.shape
    dm = wout.shape[1]
    nt = seq // tt
    kern = functools.partial(_conv_kernel, tt=tt, width=width, dconv=dconv)
    return pl.pallas_call(
        kern,
        grid=(n_seq, nt),
        in_specs=[pl.BlockSpec((tt, 2 * dconv), lambda b, t: (b * nt + t, 0)),
                  pl.BlockSpec((None, width - 1, dconv), lambda b, t: (b, 0, 0)),
                  pl.BlockSpec((width, dconv), lambda b, t: (0, 0)),
                  pl.BlockSpec((1, dconv), lambda b, t: (0, 0)),
                  pl.BlockSpec((1, dconv), lambda b, t: (0, 0)),
                  pl.BlockSpec((1, dconv), lambda b, t: (0, 0)),
                  pl.BlockSpec((dconv, dm), lambda b, t: (0, 0))],
        out_specs=[pl.BlockSpec((tt, dm), lambda b, t: (b * nt + t, 0)),
                   pl.BlockSpec((None, width - 1, dconv), lambda b, t: (b, 0, 0))],
        out_shape=[jax.ShapeDtypeStruct((n_seq * seq, dm), F32),
                   jax.ShapeDtypeStruct((n_seq, width - 1, dconv), F32)],
        scratch_shapes=[pltpu.VMEM((32 + tt, dconv), F32), pltpu.VMEM((tt // 2, dconv), F32),
                        pltpu.VMEM((dconv, dm), BF16), pltpu.VMEM((7, 24 + tt, dconv), F32),
                        pltpu.VMEM((tt // 2, dconv), F32)],
        compiler_params=_cparams(2),
        name="conv_branch",
    )(glu_pre, prev, dw, bdw.reshape(1, dconv), lng.reshape(1, dconv), lnb.reshape(1, dconv), wout)


def _conv_step_kernel(glu_ref, prev_ref, dw_ref, bdw_ref, lng_ref, lnb_ref, wout_ref, o_ref, u_ref,
                      *, width, dconv):
    glu = glu_ref[...]
    u = glu[:, :dconv] * _sigmoid(glu[:, dconv:])
    u_ref[...] = u
    h = bdw_ref[...] + dw_ref[pl.ds(width - 1, 1), :] * u
    for j in range(width - 1):
        h = h + dw_ref[pl.ds(j, 1), :] * prev_ref[j]
    mu = jnp.mean(h, axis=-1, keepdims=True)
    var = jnp.mean(jnp.square(h - mu), axis=-1, keepdims=True)
    y = (h - mu) * lax.rsqrt(var + EPS) * lng_ref[...] + lnb_ref[...]
    y = y * _sigmoid(y)
    o_ref[...] = jnp.dot(y.astype(BF16), wout_ref[...].astype(BF16), preferred_element_type=F32)


def conv_branch_step(glu_pre, prev_t, dw, bdw, lng, lnb, wout):
    width, dconv = dw.shape
    n = glu_pre.shape[0]
    dm = wout.shape[1]
    kern = functools.partial(_conv_step_kernel, width=width, dconv=dconv)
    return pl.pallas_call(
        kern,
        out_shape=[jax.ShapeDtypeStruct((n, dm), F32), jax.ShapeDtypeStruct((n, dconv), F32)],
        compiler_params=pltpu.CompilerParams(vmem_limit_bytes=VMEM_LIMIT),
    )(glu_pre, prev_t, dw, bdw.reshape(1, dconv), lng.reshape(1, dconv), lnb.reshape(1, dconv), wout)


def _select_threshold(count_gt, row_min, row_max, n_adm, topk, any_fn):
    kf = jnp.float32(topk)
    full = n_adm <= kf
    lo0 = row_min - (1.0 + jnp.abs(row_min))
    hi0 = row_max
    flo0 = jnp.where(full, kf, n_adm)
    fhi0 = jnp.zeros_like(lo0)

    def active_rows(lo, hi, flo):
        mid = 0.5 * lo + 0.5 * hi
        return (flo != kf) & (lo < mid) & (mid < hi)

    def cond(st):
        lo, hi, flo, fhi = st
        return any_fn(active_rows(lo, hi, flo))

    def step(st):
        lo, hi, flo, fhi = st
        act = active_rows(lo, hi, flo)
        mid = 0.5 * lo + 0.5 * hi
        c = count_gt(mid)
        up = act & (c >= kf)
        dn = act & (c < kf)
        return (jnp.where(up, mid, lo), jnp.where(dn, mid, hi),
                jnp.where(up, c, flo), jnp.where(dn, c, fhi))

    def body(st):
        for _ in range(4):
            st = step(st)
        return st

    lo, hi, flo, fhi = lax.while_loop(cond, body, (lo0, hi0, flo0, fhi0))
    lo = jnp.where(full, -jnp.inf, lo)
    return lo, hi, flo, fhi


def _attn_prompt_kernel(qi_ref, wi_ref, kw_ref, q_ref, k_ref, v_ref, co_ref, ga_ref, gb_ref, bias_ref,
                        o_ref, kd_ref, kb_ref, vt_ref, wit_ref, qib_ref, sc_ref, sel_ref, qs_ref, *state,
                        seq, n_heads, n_kv, idx_heads, idx_dim, topk, hd):
    m_refs, l_refs, acc_refs = state[:n_kv], state[n_kv:2 * n_kv], state[2 * n_kv:3 * n_kv]
    st_refs = state[3 * n_kv:]
    j = pl.program_id(1)
    group = n_heads // n_kv
    nck = (j * QB + QB + CK - 1) // CK
    kf = jnp.float32(topk)

    @pl.when(j == 0)
    def _():
        kd_ref[...] = kw_ref[:, :idx_dim].astype(BF16)
        for g in range(n_kv):
            kb_ref[g] = k_ref[:, g * hd:(g + 1) * hd].astype(BF16)
            for c in range(seq // CK):
                vt_ref[g, c] = v_ref[c * CK:(c + 1) * CK, g * hd:(g + 1) * hd].T.astype(BF16)

    wit_ref[...] = (wi_ref[...] * (idx_heads ** -0.5)).T
    qi = qi_ref[...].astype(F32) * (idx_dim ** -0.5)
    for h in range(idx_heads):
        qib_ref[h] = qi[:, h * idx_dim:(h + 1) * idx_dim].astype(BF16)
    qpos = j * QB + lax.broadcasted_iota(jnp.int32, (CK, QB), 1)
    krow = lax.broadcasted_iota(jnp.int32, (CK, QB), 0)
    per_dot = 2 * LANES // QB

    def score_chunk(c, carry):
        mn, mx = carry
        k0 = pl.multiple_of(c * CK, CK)
        kc = kd_ref[pl.ds(k0, CK), :]
        acc = jnp.zeros((CK, QB), F32)
        for p in range(idx_heads // per_dot):
            rhs = qib_ref[pl.ds(p * per_dot, per_dot)].reshape(per_dot * QB, idx_dim)
            s = _dot_nt(kc, rhs)
            for r in range(per_dot):
                acc = acc + (wit_ref[pl.ds(idx_dim + p * per_dot + r, 1), :]
                             * jnp.maximum(s[:, r * QB:(r + 1) * QB], 0.0))
        adm = (krow + c * CK) <= qpos
        sc_ref[c] = jnp.where(adm, acc, -jnp.inf)
        mn = jnp.minimum(mn, _fold_rows(jnp.where(adm, acc, jnp.inf), jnp.minimum))
        mx = jnp.maximum(mx, _fold_rows(jnp.where(adm, acc, -jnp.inf), jnp.maximum))
        return mn, mx

    mn8, mx8 = lax.fori_loop(0, nck, score_chunk,
                             (jnp.full((8, QB), jnp.inf, F32), jnp.full((8, QB), -jnp.inf, F32)))
    row_min = jnp.min(mn8, axis=0, keepdims=True)
    row_max = jnp.max(mx8, axis=0, keepdims=True)
    n_adm = (j * QB + 1 + lax.broadcasted_iota(jnp.int32, (1, QB), 1)).astype(F32)

    def count_gt(t):
        def cbody(c, acc):
            return acc + _fold_rows(jnp.where(sc_ref[c] > t, 1.0, 0.0), jnp.add)
        part = lax.fori_loop(0, nck, cbody, jnp.zeros((8, QB), F32))
        return jnp.sum(part, axis=0, keepdims=True)

    def any_fn(mask):
        return jnp.max(jnp.where(mask, 1.0, 0.0)) > 0.0

    lo, hi, flo, fhi = _select_threshold(count_gt, row_min, row_max, n_adm, topk, any_fn)
    tie = flo != kf

    def sel_chunk(c, carry):
        sel_ref[c] = jnp.where(sc_ref[c] > lo, 1.0, 0.0)
        return carry
    lax.fori_loop(0, nck, sel_chunk, 0)

    @pl.when(any_fn(tie))
    def _():
        need = kf - fhi
        lower = (lax.broadcasted_iota(jnp.int32, (CK, CK), 1)
                 < lax.broadcasted_iota(jnp.int32, (CK, CK), 0)).astype(BF16)

        def tie_chunk(c, before):
            s = sc_ref[c]
            eq = s == hi
            eqf = jnp.where(eq, 1.0, 0.0)
            rank = before + jnp.dot(lower, eqf.astype(BF16), preferred_element_type=F32)
            keep = (s > hi) | (eq & (rank < need))
            sel_ref[c] = jnp.where(tie, jnp.where(keep, 1.0, 0.0), sel_ref[c])
            return before + jnp.sum(eqf, axis=0, keepdims=True)
        lax.fori_loop(0, nck, tie_chunk, jnp.zeros((1, QB), F32))

    scale2 = hd ** -0.5 * LOG2E
    for h in range(n_heads):
        qs_ref[h] = q_ref[:, h * hd:(h + 1) * hd].astype(BF16)
    for g in range(n_kv):
        m_refs[g][...] = jnp.full(m_refs[g].shape, NEG, F32)
        l_refs[g][...] = jnp.zeros(l_refs[g].shape, F32)
        acc_refs[g][...] = jnp.zeros(acc_refs[g].shape, F32)

    def logits(g, c):
        kc = kb_ref[g, pl.ds(pl.multiple_of(c * CK, CK), CK), :]
        qg = qs_ref[pl.ds(g * group, group)].reshape(group * QB, hd)
        st_refs[g % 2][...] = _dot_nt(kc, qg)

    logits(0, 0)

    def att_chunk(c, carry, near):
        msk = jnp.concatenate([sel_ref[c]] * group, axis=1) > 0.0
        tis = [jnp.clip(c * (CK // QB) + s - j + 2, 0, 2) for s in range(CK // QB)]
        for g in range(n_kv):
            if g + 1 < n_kv:
                logits(g + 1, c)
            else:
                logits(0, jnp.minimum(c + 1, nck - 1))
            lg = st_refs[g % 2][...] * scale2
            if near:
                rows = [jnp.concatenate([bias_ref[ti, g * group + hh] for hh in range(group)], axis=1)
                        for ti in tis]
                lg = lg + jnp.concatenate(rows, axis=0)
            lg = jnp.where(msk, lg, -jnp.inf)
            m_old = m_refs[g][...]
            m_new = jnp.maximum(m_old, jnp.max(_fold_rows(lg, jnp.maximum), axis=0, keepdims=True))
            alpha = jnp.exp2(m_old - m_new)
            p = jnp.exp2(lg - m_new)
            l_refs[g][...] = alpha * l_refs[g][...] + jnp.sum(_fold_rows(p, jnp.add), axis=0, keepdims=True)
            pv = jnp.dot(vt_ref[g, c], p.astype(BF16), preferred_element_type=F32)
            acc_refs[g][...] = alpha * acc_refs[g][...] + pv
            m_refs[g][...] = m_new
        return carry
    n_far = jnp.maximum((j * (QB // LANES) - 1) // (CK // QB), 0)
    lax.fori_loop(0, n_far, functools.partial(att_chunk, near=False), 0)
    lax.fori_loop(n_far, nck, functools.partial(att_chunk, near=True), 0)

    for g in range(n_kv):
        ot = acc_refs[g][...] / l_refs[g][...]
        for hh in range(group):
            cs = slice((g * group + hh) * hd, (g * group + hh + 1) * hd)
            o = ot[:, hh * QB:(hh + 1) * QB].T
            mixed = ga_ref[:, cs] * co_ref[:, cs] + gb_ref[:, cs] * o
            o_ref[:, cs] = mixed.astype(o_ref.dtype)


def attn_prompt(qi, kw, q, k, v, conv_out, gates, bias3, n_seq, seq, n_kv, idx_dim, idx_heads):
    m, dm = q.shape
    hd = k.shape[1] // n_kv
    n_heads = dm // hd
    group = n_heads // n_kv
    nb = seq // QB
    topk = min(TOPK_MAX, seq // 4)
    assert seq % CK == 0 and LANES % idx_dim == 0 and kw.shape[1] == LANES
    kern = functools.partial(_attn_prompt_kernel, seq=seq, n_heads=n_heads, n_kv=n_kv, idx_heads=idx_heads,
                             idx_dim=idx_dim, topk=topk, hd=hd)
    row = lambda b, j: (b * nb + j, 0)
    return pl.pallas_call(
        kern,
        grid=(n_seq, nb),
        in_specs=[pl.BlockSpec((QB, idx_heads * idx_dim), row),
                  pl.BlockSpec((QB, LANES), row),
                  pl.BlockSpec((seq, LANES), lambda b, j: (b, 0)),
                  pl.BlockSpec((QB, dm), row),
                  pl.BlockSpec((seq, n_kv * hd), lambda b, j: (b, 0)),
                  pl.BlockSpec((seq, n_kv * hd), lambda b, j: (b, 0)),
                  pl.BlockSpec((QB, dm), row),
                  pl.BlockSpec((QB, dm), lambda b, j: (b * nb + j, 0)),
                  pl.BlockSpec((QB, dm), lambda b, j: (b * nb + j, 1)),
                  pl.BlockSpec((3, n_heads, QB, QB), lambda b, j: (0, 0, 0, 0))],
        out_specs=pl.BlockSpec((QB, dm), row),
        out_shape=jax.ShapeDtypeStruct((m, dm), BF16),
        scratch_shapes=[pltpu.VMEM((seq, idx_dim), BF16),
                        pltpu.VMEM((n_kv, seq, hd), BF16),
                        pltpu.VMEM((n_kv, seq // CK, hd, CK), BF16),
                        pltpu.VMEM((LANES, QB), F32),
                        pltpu.VMEM((idx_heads, QB, idx_dim), BF16),
                        pltpu.VMEM((seq // CK, CK, QB), F32),
                        pltpu.VMEM((seq // CK, CK, QB), F32),
                        pltpu.VMEM((n_heads, QB, hd), BF16)]
                       + [pltpu.VMEM((1, group * QB), F32)] * (2 * n_kv)
                       + [pltpu.VMEM((hd, group * QB), F32)] * n_kv
                       + [pltpu.VMEM((CK, group * QB), F32)] * 2,
        compiler_params=_cparams(2),
        name="attn_prompt",
    )(qi, kw, kw, q, k, v, conv_out, gates, gates, bias3)


def _sample_score_kernel(pt_ref, qi_ref, wi_ref, kn_ref, *rest, pg, idx_heads, idx_dim):
    k_hbm, o_ref, self_ref, kbuf_ref, sem, kcat_ref = rest
    page = k_hbm.shape[2]
    b = pl.program_id(0)
    p = pl.program_id(1)
    n_steps = pl.num_programs(1)
    step = b * n_steps + p
    slot = step % 2

    def page_copies(bb, pp, sl):
        return [pltpu.make_async_copy(k_hbm.at[pt_ref[bb, pp * pg + i]], kbuf_ref.at[sl, i], sem.at[sl])
                for i in range(pg)]

    @pl.when(step == 0)
    def _():
        for c in page_copies(0, 0, 0):
            c.start()

    nxt = step + 1

    @pl.when(nxt < pl.num_programs(0) * n_steps)
    def _():
        for c in page_copies(nxt // n_steps, nxt % n_steps, 1 - slot):
            c.start()

    for c in page_copies(b, p, slot):
        c.wait()

    qi = qi_ref[...] * (idx_dim ** -0.5)
    wi = wi_ref[...] * (idx_heads ** -0.5)
    qb = qi.astype(BF16)
    for i in range(pg):
        kcat_ref[:, i * page:(i + 1) * page] = kbuf_ref[slot, i].astype(BF16)
    s = jnp.dot(qb, kcat_ref[...], preferred_element_type=F32)
    o_ref[...] = jnp.sum(wi * jnp.maximum(s, 0.0), axis=0, keepdims=True)

    @pl.when(pl.program_id(1) == 0)
    def _():
        kn = kn_ref[...].astype(BF16).astype(F32)
        s = jnp.sum(qb.astype(F32) * kn, axis=1, keepdims=True)
        sself = jnp.sum(wi * jnp.maximum(s, 0.0), axis=0, keepdims=True)
        self_ref[...] = jnp.broadcast_to(sself, self_ref.shape)


def sample_scores(page_table, qi3, wi3, ki_new3, cache_kidx_t, pg):
    n, n_pages = page_table.shape
    idx_heads, idx_dim = qi3.shape[1:]
    page = cache_kidx_t.shape[2]
    kern = functools.partial(_sample_score_kernel, pg=pg, idx_heads=idx_heads, idx_dim=idx_dim)
    grid_spec = pltpu.PrefetchScalarGridSpec(
        num_scalar_prefetch=1,
        grid=(n, n_pages // pg),
        in_specs=[pl.BlockSpec((None, idx_heads, idx_dim), lambda b, p, pt: (b, 0, 0)),
                  pl.BlockSpec((None, idx_heads, 1), lambda b, p, pt: (b, 0, 0)),
                  pl.BlockSpec((None, 1, idx_dim), lambda b, p, pt: (b, 0, 0)),
                  pl.BlockSpec(memory_space=pl.ANY)],
        out_specs=[pl.BlockSpec((None, None, 1, pg * page), lambda b, p, pt: (b, p, 0, 0)),
                   pl.BlockSpec((None, 1, LANES), lambda b, p, pt: (b, 0, 0))],
        scratch_shapes=[pltpu.VMEM((2, pg, idx_dim, page), F32), pltpu.SemaphoreType.DMA((2,)),
                        pltpu.VMEM((idx_dim, pg * page), BF16)],
    )
    return pl.pallas_call(
        kern,
        grid_spec=grid_spec,
        out_shape=[jax.ShapeDtypeStruct((n, n_pages // pg, 1, pg * page), F32),
                   jax.ShapeDtypeStruct((n, 1, LANES), F32)],
        compiler_params=_cparams(2),
        name="sample_scores",
    )(page_table, qi3, wi3, ki_new3, cache_kidx_t)


def _sample_select_kernel(sc_ref, self_ref, sel4_ref, selself_ref, sel_ref, *, topk, past, rep):
    sc = sc_ref[...]
    sself = self_ref[:, 0:1]
    n = sc.shape[0]
    kf = jnp.float32(topk)
    row_min = jnp.minimum(jnp.min(sc, axis=1, keepdims=True), sself)
    row_max = jnp.maximum(jnp.max(sc, axis=1, keepdims=True), sself)
    n_adm = jnp.full((n, 1), past + 1, F32)

    def count_gt(t):
        return (jnp.sum(jnp.where(sc > t, 1.0, 0.0), axis=1, keepdims=True)
                + jnp.where(sself > t, 1.0, 0.0))

    def any_fn(mask):
        return jnp.max(jnp.where(mask, 1.0, 0.0)) > 0.0

    lo, hi, flo, fhi = _select_threshold(count_gt, row_min, row_max, n_adm, topk, any_fn)
    tie = flo != kf
    sel_ref[...] = jnp.where(sc > lo, 1.0, 0.0)
    selself_ref[...] = jnp.broadcast_to(jnp.where(sself > lo, 1.0, 0.0), selself_ref.shape)

    @pl.when(any_fn(tie))
    def _():
        need = kf - fhi
        blk = 512
        tri = (lax.broadcasted_iota(jnp.int32, (blk, blk), 0)
               < lax.broadcasted_iota(jnp.int32, (blk, blk), 1)).astype(BF16)
        before = jnp.zeros((n, 1), F32)
        for c in range(past // blk):
            s = sc_ref[:, c * blk:(c + 1) * blk]
            eq = s == hi
            rank = before + jnp.dot(jnp.where(eq, 1.0, 0.0).astype(BF16), tri, preferred_element_type=F32)
            keep = (s > hi) | (eq & (rank < need))
            sel_ref[:, c * blk:(c + 1) * blk] = jnp.where(tie, jnp.where(keep, 1.0, 0.0),
                                                          sel_ref[:, c * blk:(c + 1) * blk])
            before = before + jnp.sum(jnp.where(eq, 1.0, 0.0), axis=1, keepdims=True)
        keep_self = (sself > hi) | ((sself == hi) & (before < need))
        selself_ref[...] = jnp.broadcast_to(
            jnp.where(tie, jnp.where(keep_self, 1.0, 0.0), jnp.where(sself > lo, 1.0, 0.0)), selself_ref.shape)

    blk = 512
    row_lo = lax.broadcasted_iota(jnp.int32, (blk, blk * rep), 0) * rep
    col = lax.broadcasted_iota(jnp.int32, (blk, blk * rep), 1)
    spread = jnp.where((col >= row_lo) & (col < row_lo + rep), 1.0, 0.0).astype(BF16)
    for c in range(past // blk):
        sel4_ref[:, c * blk * rep:(c + 1) * blk * rep] = jnp.dot(
            sel_ref[:, c * blk:(c + 1) * blk].astype(BF16), spread, preferred_element_type=F32)


def sample_select(scores, sself, topk, rep):
    n, past = scores.shape
    kern = functools.partial(_sample_select_kernel, topk=topk, past=past, rep=rep)
    return pl.pallas_call(
        kern,
        out_shape=[jax.ShapeDtypeStruct((n, past * rep), F32), jax.ShapeDtypeStruct((n, LANES), F32)],
        scratch_shapes=[pltpu.VMEM((n, past), F32)],
        compiler_params=pltpu.CompilerParams(vmem_limit_bytes=VMEM_LIMIT),
        name="sample_select",
    )(scores, sself)


def _sample_attn_kernel(pt_ref, q_ref, kn_ref, vn_ref, sel_ref, selself_ref, bias_ref, rb0_ref, own_ref, *rest,
                        pg, n_heads, n_kv, hd):
    k_hbm, v_hbm, o_ref, kbuf_ref, vbuf_ref, sem, kcat_ref, vcat_ref, m_ref, l_ref, acc_ref = rest
    b = pl.program_id(0)
    p = pl.program_id(1)
    n_steps = pl.num_programs(1)
    scale = hd ** -0.5
    rows = k_hbm.shape[1]
    step = b * n_steps + p
    slot = step % 2

    def page_copies(bb, pp, sl):
        copies = []
        for i in range(pg):
            pid = pt_ref[bb, pp * pg + i]
            dst = pl.ds(i * rows, rows)
            copies.append(pltpu.make_async_copy(k_hbm.at[pid], kbuf_ref.at[sl, dst], sem.at[0, sl]))
            copies.append(pltpu.make_async_copy(v_hbm.at[pid], vbuf_ref.at[sl, dst], sem.at[1, sl]))
        return copies

    @pl.when(step == 0)
    def _():
        for c in page_copies(0, 0, 0):
            c.start()

    nxt = step + 1

    @pl.when(nxt < pl.num_programs(0) * n_steps)
    def _():
        for c in page_copies(nxt // n_steps, nxt % n_steps, 1 - slot):
            c.start()

    for c in page_copies(b, p, slot):
        c.wait()

    @pl.when(p == 0)
    def _():
        m_ref[...] = jnp.full(m_ref.shape, NEG, F32)
        l_ref[...] = jnp.zeros(l_ref.shape, F32)
        acc_ref[...] = jnp.zeros(acc_ref.shape, F32)

    for i in range(pg):
        kcat_ref[i * rows:(i + 1) * rows, :] = kbuf_ref[slot, pl.ds(i * rows, rows), :].astype(BF16)
        vcat_ref[i * rows:(i + 1) * rows, :] = vbuf_ref[slot, pl.ds(i * rows, rows), :].astype(BF16)
    qb = q_ref[...].astype(BF16)
    last = p == n_steps - 1
    bias = jnp.concatenate([bias_ref[0]] * (pg - 1) + [jnp.where(last, bias_ref[1], bias_ref[0])], axis=1)
    msk = (own_ref[...] > 0.0) & (sel_ref[...] > 0.0)
    lg = jnp.where(msk, _dot_nt(qb, kcat_ref[...]) * scale + bias, -jnp.inf)
    m_old = m_ref[...]
    m_new = jnp.maximum(m_old, jnp.max(lg, axis=-1, keepdims=True))
    alpha = jnp.exp(m_old - m_new)
    pr = jnp.exp(lg - m_new)
    l_new = alpha * l_ref[...] + jnp.sum(pr, axis=-1, keepdims=True)
    acc = alpha * acc_ref[...] + jnp.dot(pr.astype(BF16), vcat_ref[...], preferred_element_type=F32)
    m_ref[...] = m_new
    l_ref[...] = l_new
    acc_ref[...] = acc

    @pl.when(last)
    def _():
        kn = kn_ref[...].astype(BF16).astype(F32)
        vn = vn_ref[...].astype(BF16).astype(F32)
        ls = jnp.sum(qb.astype(F32) * kn, axis=-1, keepdims=True) * scale + rb0_ref[...]
        on = selself_ref[:, 0:1] > 0.0
        ls = jnp.where(on, ls, NEG)
        m_f = jnp.maximum(m_new, ls)
        a2 = jnp.exp(m_new - m_f)
        ps = jnp.where(on, jnp.exp(ls - m_f), 0.0)
        l_f = a2 * l_new + ps
        acc_f = a2 * acc + ps.astype(BF16).astype(F32) * vn
        o_ref[...] = acc_f / l_f


def sample_attention(page_table, q3, k_rep, v_rep, sel4, selself3, bias_s, rb0, cache_k, cache_v, pg, n_kv):
    n, n_pages = page_table.shape
    n_heads, hd = q3.shape[1:]
    rows = cache_k.shape[1]
    kern = functools.partial(_sample_attn_kernel, pg=pg, n_heads=n_heads, n_kv=n_kv, hd=hd)
    own = (np.arange(pg * rows)[None, :] % n_kv
           == np.arange(n_heads)[:, None] // (n_heads // n_kv)).astype(np.float32)
    hbm = pl.BlockSpec(memory_space=pl.ANY)
    per_seq = lambda b, p, pt: (b, 0, 0)
    grid_spec = pltpu.PrefetchScalarGridSpec(
        num_scalar_prefetch=1,
        grid=(n, n_pages // pg),
        in_specs=[pl.BlockSpec((None, n_heads, hd), per_seq),
                  pl.BlockSpec((None, n_heads, hd), per_seq),
                  pl.BlockSpec((None, n_heads, hd), per_seq),
                  pl.BlockSpec((None, None, 1, pg * rows), lambda b, p, pt: (b, p, 0, 0)),
                  pl.BlockSpec((None, 1, LANES), per_seq),
                  pl.BlockSpec((2, n_heads, rows), lambda b, p, pt: (0, 0, 0)),
                  pl.BlockSpec((n_heads, 1), lambda b, p, pt: (0, 0)),
                  pl.BlockSpec((n_heads, pg * rows), lambda b, p, pt: (0, 0)), hbm, hbm],
        out_specs=pl.BlockSpec((None, n_heads, hd), per_seq),
        scratch_shapes=[pltpu.VMEM((2, pg * rows, hd), F32), pltpu.VMEM((2, pg * rows, hd), F32),
                        pltpu.SemaphoreType.DMA((2, 2)),
                        pltpu.VMEM((pg * rows, hd), BF16), pltpu.VMEM((pg * rows, hd), BF16),
                        pltpu.VMEM((n_heads, 1), F32), pltpu.VMEM((n_heads, 1), F32),
                        pltpu.VMEM((n_heads, hd), F32)],
    )
    return pl.pallas_call(
        kern,
        grid_spec=grid_spec,
        out_shape=jax.ShapeDtypeStruct((n, n_heads, hd), F32),
        compiler_params=_cparams(2),
        name="sample_attn",
    )(page_table, q3, k_rep, v_rep, sel4, selself3, bias_s, rb0, jnp.asarray(own), cache_k, cache_v)


def _mix_kernel(ga_ref, gb_ref, co_ref, at_ref, o_ref):
    o_ref[...] = (ga_ref[...] * co_ref[...] + gb_ref[...] * at_ref[...]).astype(o_ref.dtype)


def gated_mix(gates, conv_out, attn):
    n, dm = conv_out.shape
    return pl.pallas_call(
        _mix_kernel,
        grid=(1,),
        in_specs=[pl.BlockSpec((n, dm), lambda i: (0, 0)), pl.BlockSpec((n, dm), lambda i: (0, 1)),
                  pl.BlockSpec((n, dm), lambda i: (0, 0)), pl.BlockSpec((n, dm), lambda i: (0, 0))],
        out_specs=pl.BlockSpec((n, dm), lambda i: (0, 0)),
        out_shape=jax.ShapeDtypeStruct((n, dm), BF16),
    )(gates, gates, conv_out, attn)


def _ffn_up_kernel(x_ref, xh_ref, wg_ref, wv_ref, dg_ref, dv_ref, bg_ref, bv_ref, pg_ref, pv_ref,
                   o_ref, sg_ref, sv_ref, eg_ref, ev_ref, wgb_ref, wvb_ref, *, tm, rs, width, tiles_per_seq):
    hist = width - 1
    pad = xh_ref.shape[0]
    first = pl.program_id(0) % tiles_per_seq == 0
    kc = 256
    for wf_ref, wb_ref in ((wg_ref, wgb_ref), (wv_ref, wvb_ref)):
        for k0 in range(0, wf_ref.shape[0], kc):
            wb_ref[k0:k0 + kc, :] = wf_ref[k0:k0 + kc, :].astype(BF16)
    branches = ((wgb_ref, dg_ref, bg_ref, pg_ref, eg_ref, sg_ref),
                (wvb_ref, dv_ref, bv_ref, pv_ref, ev_ref, sv_ref))
    for w_ref, _, _, prev_ref, e_ref, _ in branches:
        e_ref[0, pl.ds(0, pad), :] = jnp.dot(xh_ref[...], w_ref[...], preferred_element_type=F32)

        @pl.when(first)
        def _():
            e_ref[0, pl.ds(pad - hist, hist), :] = prev_ref[...]

    def project(r):
        for w_ref, _, _, _, e_ref, _ in branches:
            e_ref[r % 2, pl.ds(pad, rs), :] = jnp.dot(x_ref[pl.ds(r * rs, rs), :], w_ref[...],
                                                      preferred_element_type=F32)

    def hand_over(r):
        for _, _, _, _, e_ref, _ in branches:
            e_ref[(r + 1) % 2, pl.ds(0, pad), :] = e_ref[r % 2, pl.ds(rs, pad), :]

    def activate(r):
        outs = []
        for _, d_ref, b_ref, _, e_ref, _ in branches:
            acc = b_ref[...] + d_ref[pl.ds(hist, 1), :] * e_ref[r % 2, pl.ds(pad, rs), :]
            for j in range(hist):
                acc = acc + d_ref[pl.ds(j, 1), :] * e_ref[r % 2, pl.ds(pad - hist + j, rs), :]
            outs.append(acc)
        g, v = outs
        o_ref[pl.ds(r * rs, rs), :] = (g * _sigmoid(g) * v).astype(o_ref.dtype)

    n_sub = tm // rs
    project(0)
    hand_over(0)
    for r in range(1, n_sub):
        project(r)
        activate(r - 1)
        hand_over(r)
    activate(n_sub - 1)
    for _, _, _, _, e_ref, s_ref in branches:
        s_ref[...] = e_ref[(n_sub - 1) % 2, pl.ds(pad + rs - hist, hist), :]


def ffn_up_act(xn, w_up, prev, dw, bdw, n_seq, seq, tm, tn):
    m, k = xn.shape
    f2 = w_up.shape[1]
    f = f2 // 2
    width = dw.shape[0]
    ncb = f // tn
    pad = 16
    tps = seq // tm
    assert seq % tm == 0 and tm % pad == 0 and f % tn == 0 and width - 1 <= pad
    rs = min(128, tm // 2)
    assert tm % rs == 0 and rs % 8 == 0 and k % 256 == 0
    kern = functools.partial(_ffn_up_kernel, tm=tm, rs=rs, width=width, tiles_per_seq=tps)
    hb = tm // pad
    st_shape = jax.ShapeDtypeStruct((m // tm, width - 1, f), F32)
    act, sg, sv = pl.pallas_call(
        kern,
        grid=(m // tm, ncb),
        in_specs=[pl.BlockSpec((tm, k), lambda i, c: (i, 0)),
                  pl.BlockSpec((pad, k), lambda i, c: (jnp.maximum(i * hb - 1, 0), 0)),
                  pl.BlockSpec((k, tn), lambda i, c: (0, c)),
                  pl.BlockSpec((k, tn), lambda i, c: (0, c + ncb)),
                  pl.BlockSpec((width, tn), lambda i, c: (0, c)),
                  pl.BlockSpec((width, tn), lambda i, c: (0, c + ncb)),
                  pl.BlockSpec((1, tn), lambda i, c: (0, c)),
                  pl.BlockSpec((1, tn), lambda i, c: (0, c + ncb)),
                  pl.BlockSpec((None, width - 1, tn), lambda i, c: (i // tps, 0, c)),
                  pl.BlockSpec((None, width - 1, tn), lambda i, c: (i // tps, 0, c + ncb))],
        out_specs=[pl.BlockSpec((tm, tn), lambda i, c: (i, c)),
                   pl.BlockSpec((None, width - 1, tn), lambda i, c: (i, 0, c)),
                   pl.BlockSpec((None, width - 1, tn), lambda i, c: (i, 0, c))],
        out_shape=[jax.ShapeDtypeStruct((m, f), BF16), st_shape, st_shape],
        scratch_shapes=[pltpu.VMEM((2, pad + rs, tn), F32), pltpu.VMEM((2, pad + rs, tn), F32),
                        pltpu.VMEM((k, tn), BF16), pltpu.VMEM((k, tn), BF16)],
        compiler_params=_cparams(2),
        name="ffn_up_act",
    )(xn, xn, w_up, w_up, dw, dw, bdw.reshape(1, f2), bdw.reshape(1, f2), prev, prev)
    state = jnp.concatenate([sg, sv], axis=-1).reshape(n_seq, tps, width - 1, f2)[:, tps - 1]
    return act, state


def _ffn_act_step_kernel(ug_ref, uv_ref, pg_ref, pv_ref, wg_ref, wv_ref, bg_ref, bv_ref, o_ref, *, width):
    def conv(u_ref, prev_ref, w_ref, b_ref):
        acc = b_ref[...] + w_ref[pl.ds(width - 1, 1), :] * u_ref[...]
        for j in range(width - 1):
            acc = acc + w_ref[pl.ds(j, 1), :] * prev_ref[j]
        return acc
    g = conv(ug_ref, pg_ref, wg_ref, bg_ref)
    v = conv(uv_ref, pv_ref, wv_ref, bv_ref)
    o_ref[...] = (g * _sigmoid(g) * v).astype(o_ref.dtype)


def ffn_act_step(u, prev_t, dw, bdw, tc):
    n, f2 = u.shape
    f = f2 // 2
    width = dw.shape[0]
    ncb = f // tc
    kern = functools.partial(_ffn_act_step_kernel, width=width)
    return pl.pallas_call(
        kern,
        grid=(ncb,),
        in_specs=[pl.BlockSpec((n, tc), lambda c: (0, c)),
                  pl.BlockSpec((n, tc), lambda c: (0, c + ncb)),
                  pl.BlockSpec((width - 1, n, tc), lambda c: (0, 0, c)),
                  pl.BlockSpec((width - 1, n, tc), lambda c: (0, 0, c + ncb)),
                  pl.BlockSpec((width, tc), lambda c: (0, c)),
                  pl.BlockSpec((width, tc), lambda c: (0, c + ncb)),
                  pl.BlockSpec((1, tc), lambda c: (0, c)),
                  pl.BlockSpec((1, tc), lambda c: (0, c + ncb))],
        out_specs=pl.BlockSpec((n, tc), lambda c: (0, c)),
        out_shape=jax.ShapeDtypeStruct((n, f), BF16),
        compiler_params=_cparams(1),
    )(u, u, prev_t, prev_t, dw, dw, bdw.reshape(1, f2), bdw.reshape(1, f2))


def _row_tile(m, cap):
    tm = cap
    while m % tm:
        tm //= 2
    return tm if tm >= 16 else m


def _in_projection(xn, w_main, w_kw, w_g, sizes, tm):
    glu_w, q_w, k_w, v_w, qi_w, ki_w, wi_w, ga_w, gb_w = sizes
    offs = np.concatenate([[0], np.cumsum(sizes)])
    tn = 512
    t = dict(w_transposed=True)
    wide = lambda w, o: 2 * tn if w % (2 * tn) == 0 and o % (2 * tn) == 0 else tn
    glu = matmul_w(xn, w_main, int(offs[0]), glu_w, tm, wide(glu_w, int(offs[0])), name="mm_glu", **t)
    q = matmul_w(xn, w_main, int(offs[1]), q_w, tm, wide(q_w, int(offs[1])), name="mm_q", out_dtype=BF16, **t)
    k = matmul_w(xn, w_main, int(offs[2]), k_w, tm, min(tn, k_w), name="mm_k", **t)
    v = matmul_w(xn, w_main, int(offs[3]), v_w, tm, min(tn, v_w), name="mm_v", **t)
    qi = matmul_w(xn, w_main, int(offs[4]), qi_w, tm, min(tn, qi_w), name="mm_qi", out_dtype=BF16, **t)
    kw = matmul_w(xn, w_kw, 0, LANES, tm, LANES, name="mm_kw", **t)
    gates = matmul_w(xn, w_g, 0, ga_w + gb_w, tm, wide(ga_w + gb_w, 0), name="mm_gates", sigmoid=True, **t)
    return glu, q, k, v, qi, kw, gates


def kernel(x_prompt, x_sample, cache_k, cache_v, cache_kidx, state_conv, state_ffn, page_table, rel_bias,
           norm_attn, w_in, dw_conv, b_dw_conv, ln_conv_g, ln_conv_b, w_conv_out, w_o, norm_ffn, w_up, dw_ffn,
           b_dw_ffn, w_down, norm_final):
    bsz, seq, dm = x_prompt.shape
    nd, dec_seq, _ = x_sample.shape
    depth, n_pool, page, n_kv, hd = cache_k.shape
    idx_dim = cache_kidx.shape[-1]
    n_pages = page_table.shape[1]
    past = n_pages * page
    width, dconv = dw_conv.shape[1:]
    fwidth = dw_ffn.shape[1]
    f = w_down.shape[1]
    n_heads = w_o.shape[1] // hd
    d_attn = n_heads * hd
    d_kv = n_kv * hd
    n_in = w_in.shape[2]
    idx_heads = (n_in - 2 * dconv - d_attn - 2 * d_kv - idx_dim - 2 * dm) // (idx_dim + 1)
    sizes = (2 * dconv, d_attn, d_kv, d_kv, idx_heads * idx_dim, idx_dim, idx_heads, dm, dm)
    assert sum(sizes) == n_in and depth == 1 and dec_seq == 1 and page == LANES and d_attn == dm

    mp = bsz * seq
    xp = x_prompt.reshape(mp, dm)
    xs = x_sample.reshape(nd, dm)
    bias3, bias_s = bias_tables(rel_bias, page, n_kv)
    tmp = _row_tile(mp, 2048)
    drop = lambda a: a.reshape(a.shape[1:])
    (norm_attn, w_in, dw_conv, b_dw_conv, ln_conv_g, ln_conv_b, w_conv_out, w_o, norm_ffn, w_up, dw_ffn,
     b_dw_ffn, w_down, state_conv, state_ffn) = map(drop, (
         norm_attn, w_in, dw_conv, b_dw_conv, ln_conv_g, ln_conv_b, w_conv_out, w_o, norm_ffn, w_up, dw_ffn,
         b_dw_ffn, w_down, state_conv, state_ffn))
    kidx_pool = jnp.swapaxes(cache_kidx.reshape(n_pool, page, idx_dim), 1, 2)
    k_pool = cache_k.reshape(n_pool, page * n_kv, hd)
    v_pool = cache_v.reshape(n_pool, page * n_kv, hd)

    n_aligned = sum(sizes[:5])
    n_small = idx_dim + idx_heads
    w_in_t = jnp.swapaxes(w_in, 0, 1)
    w_main = cast_rows_bf16(w_in_t, 0, n_aligned)
    w_kw = cast_rows_bf16(w_in_t, n_aligned, LANES, n_small)
    w_g = cast_rows_bf16(w_in_t, n_aligned + n_small, 2 * dm)
    w_o, w_down = cast_bf16(w_o), cast_bf16(w_down)

    xn = rmsnorm_rows(xp, norm_attn, BF16, 512)
    glu, q, k, v, qi, kw, gates = _in_projection(xn, w_main, w_kw, w_g, sizes, tmp)
    conv0 = jnp.zeros((bsz, width - 1, dconv), F32)
    conv_out, conv_state_p = conv_branch_prompt(glu, conv0, dw_conv, b_dw_conv, ln_conv_g, ln_conv_b,
                                                w_conv_out, bsz, seq, 256)
    mixed = attn_prompt(qi, kw, q, k, v, conv_out, gates, bias3, bsz, seq, n_kv, idx_dim, idx_heads)
    x2, xn2 = matmul_res_norm(mixed, w_o, xp, norm_ffn, _row_tile(mp, 512))
    ffn0 = jnp.zeros((bsz, fwidth - 1, 2 * f), F32)
    act, ffn_state_p = ffn_up_act(xn2, w_up, ffn0, dw_ffn, b_dw_ffn, bsz, seq, _row_tile(seq, 2048), 512)
    tk_down = f // 4 if f % (4 * LANES) == 0 else f
    tk_p = 512 if f % 512 == 0 else tk_down
    y_prompt = matmul_res_norm_out(act, w_down, x2, norm_final, _row_tile(mp, 1024), tk_p).reshape(bsz, seq, dm)

    xns = rmsnorm_rows(xs, norm_attn, BF16, nd)
    glu_s, q_s, k_s, v_s, qi_s, kw_s, gates_s = _in_projection(xns, w_main, w_kw, w_g, sizes, nd)
    ki_s = kw_s[:, :idx_dim]
    wi_s = kw_s[:, idx_dim:idx_dim + idx_heads]
    sc_prev_t = jnp.swapaxes(state_conv, 0, 1)
    conv_out_s, u_conv_s = conv_branch_step(glu_s, sc_prev_t, dw_conv, b_dw_conv, ln_conv_g, ln_conv_b,
                                            w_conv_out)
    conv_state_s = jnp.concatenate([state_conv[:, 1:], u_conv_s[:, None, :]], axis=1)

    pg = 16 if n_pages % 16 == 0 else 8
    pg_idx = 32 if n_pages % 32 == 0 else pg
    scores3, sself3 = sample_scores(page_table, qi_s.reshape(nd, idx_heads, idx_dim),
                                    wi_s.reshape(nd, idx_heads, 1), ki_s.reshape(nd, 1, idx_dim), kidx_pool,
                                    pg_idx)
    topk_s = min(TOPK_MAX, (past + dec_seq) // 4)
    sel4, selself = sample_select(scores3.reshape(nd, past), sself3.reshape(nd, LANES), topk_s, n_kv)
    group = n_heads // n_kv
    k_rep = jnp.repeat(k_s.reshape(nd, n_kv, hd), group, axis=1)
    v_rep = jnp.repeat(v_s.reshape(nd, n_kv, hd), group, axis=1)
    attn_s = sample_attention(page_table, q_s.reshape(nd, n_heads, hd), k_rep, v_rep,
                              sel4.reshape(nd, n_pages // pg, 1, pg * page * n_kv), selself.reshape(nd, 1, LANES),
                              bias_s,
                              rel_bias[0].reshape(n_heads, 1), k_pool, v_pool, pg, n_kv)
    mixed_s = gated_mix(gates_s, conv_out_s, attn_s.reshape(nd, dm))
    x2s, xn2s = matmul_res_norm(mixed_s, w_o, xs, norm_ffn, nd)
    u_s = matmul_w(xn2s, w_up, 0, 2 * f, nd, 512)
    sf_prev_t = jnp.swapaxes(state_ffn, 0, 1)
    act_s = ffn_act_step(u_s, sf_prev_t, dw_ffn, b_dw_ffn, 512)
    y_sample = matmul_res_norm_out(act_s, w_down, x2s, norm_final, nd, tk_down).reshape(nd, dec_seq, dm)
    ffn_state_s = jnp.concatenate([state_ffn[:, 1:], u_s[:, None, :]], axis=1)

    return (y_prompt, y_sample,
            k.reshape(1, bsz, seq, n_kv, hd), v.reshape(1, bsz, seq, n_kv, hd),
            kw.reshape(bsz, seq, LANES)[None, :, :, :idx_dim],
            conv_state_p[None], ffn_state_p[None],
            k_s.reshape(1, nd, dec_seq, n_kv, hd), v_s.reshape(1, nd, dec_seq, n_kv, hd),
            ki_s.reshape(1, nd, dec_seq, idx_dim),
            conv_state_s[None], ffn_state_s[None])
```

```python
import functools
import math

import numpy as np
import jax
import jax.numpy as jnp
from jax import lax
from jax.experimental import pallas as pl
from jax.experimental.pallas import tpu as pltpu

F32 = jnp.float32
BF16 = jnp.bfloat16

EPS = 1e-6
TOPK_MAX = 256
N_BUCKETS = 32
MAX_DISTANCE = 128
QB = 128
CK = 256
LANES = 128
NEG = -1e30
LOG2E = math.log2(math.e)
VMEM_LIMIT = 56 * 1024 * 1024


def _cparams(n_axes, vmem=VMEM_LIMIT):
    return pltpu.CompilerParams(dimension_semantics=("arbitrary",) * n_axes, vmem_limit_bytes=vmem)


def _dot_nt(a, b):
    return lax.dot_general(a, b, (((1,), (1,)), ((), ())), preferred_element_type=F32)


def _sigmoid(x):
    return 1.0 / (1.0 + jnp.exp(-x))


def _fold_rows(x, op):
    while x.shape[0] > 8:
        half = x.shape[0] // 2
        x = op(x[:half], x[half:])
    return x


def _rel_bucket_np(dist):
    n = np.maximum(dist, 0)
    max_exact = N_BUCKETS // 2
    nf = np.maximum(n, 1).astype(np.float32)
    large = max_exact + (np.log(nf / np.float32(max_exact)) / np.float32(math.log(MAX_DISTANCE / max_exact))
                         * np.float32(N_BUCKETS - max_exact)).astype(np.int32)
    large = np.minimum(large, N_BUCKETS - 1)
    return np.where(n < max_exact, n, large).astype(np.int32)


def _rms_kernel(x_ref, g_ref, o_ref):
    x = x_ref[...]
    y = x * lax.rsqrt(jnp.mean(x * x, axis=-1, keepdims=True) + EPS) * g_ref[...]
    o_ref[...] = y.astype(o_ref.dtype)


def rmsnorm_rows(x, g, out_dtype, tm):
    m, d = x.shape
    return pl.pallas_call(
        _rms_kernel,
        grid=(m // tm,),
        in_specs=[pl.BlockSpec((tm, d), lambda i: (i, 0)), pl.BlockSpec((1, d), lambda i: (0, 0))],
        out_specs=pl.BlockSpec((tm, d), lambda i: (i, 0)),
        out_shape=jax.ShapeDtypeStruct((m, d), out_dtype),
        compiler_params=_cparams(1),
        name="rmsnorm",
    )(x, g.reshape(1, d))


def _cast_kernel(w_ref, o_ref):
    o_ref[...] = w_ref[...].astype(o_ref.dtype)


def cast_bf16(w, ncols=None):
    k, n = w.shape
    ncols = n if ncols is None else ncols
    tk = 512 if k % 512 == 0 else k
    tn = 1024 if ncols % 1024 == 0 else (512 if ncols % 512 == 0 else ncols)
    return pl.pallas_call(
        _cast_kernel,
        grid=(k // tk, ncols // tn),
        in_specs=[pl.BlockSpec((tk, tn), lambda i, j: (i, j))],
        out_specs=pl.BlockSpec((tk, tn), lambda i, j: (i, j)),
        out_shape=jax.ShapeDtypeStruct((k, ncols), BF16),
        compiler_params=_cparams(2),
        name="cast_bf16",
    )(w)


def _cast_rows_kernel(w_ref, o_ref, *, valid):
    w = w_ref[...]
    if valid < w.shape[0]:
        w = jnp.where(lax.broadcasted_iota(jnp.int32, w.shape, 0) < valid, w, 0.0)
    o_ref[...] = w.astype(o_ref.dtype)


def cast_rows_bf16(wt, row0, nrows, nvalid=None):
    n, k = wt.shape
    tr = 512 if nrows % 512 == 0 else nrows
    nvalid = nrows if nvalid is None else nvalid
    assert row0 % 8 == 0 and (nvalid == nrows or tr == nrows) and row0 + nrows <= n
    return pl.pallas_call(
        functools.partial(_cast_rows_kernel, valid=nvalid),
        grid=(nrows // tr,),
        in_specs=[pl.BlockSpec((pl.Element(tr), pl.Element(k)), lambda i: (pl.multiple_of(row0 + i * tr, 8), 0))],
        out_specs=pl.BlockSpec((tr, k), lambda i: (i, 0)),
        out_shape=jax.ShapeDtypeStruct((nrows, k), BF16),
        compiler_params=_cparams(1),
        name="cast_rows_bf16",
    )(wt)


def _mm_kernel(a_ref, w_ref, o_ref):
    o_ref[...] = jnp.dot(a_ref[...], w_ref[...].astype(BF16), preferred_element_type=F32).astype(o_ref.dtype)


def _mm_nt_kernel(a_ref, wt_ref, o_ref, *, sigmoid):
    y = _dot_nt(a_ref[...], wt_ref[...])
    o_ref[...] = (_sigmoid(y) if sigmoid else y).astype(o_ref.dtype)


def _mm_nt_heads_kernel(a_ref, wt_ref, o_ref, *, n_kv):
    y = _dot_nt(a_ref[...], wt_ref[...])
    tm, hd = y.shape[0], y.shape[1] // n_kv
    for g in range(n_kv):
        o_ref[pl.ds(g, tm, stride=n_kv), :] = y[:, g * hd:(g + 1) * hd]


def matmul_heads(a, wt, row0, n_kv, hd, tm, name):
    m, k = a.shape
    width = n_kv * hd
    assert row0 % width == 0 and m % tm == 0
    return pl.pallas_call(
        functools.partial(_mm_nt_heads_kernel, n_kv=n_kv),
        grid=(m // tm,),
        in_specs=[pl.BlockSpec((tm, k), lambda i: (i, 0)), pl.BlockSpec((width, k), lambda i: (row0 // width, 0))],
        out_specs=pl.BlockSpec((tm * n_kv, hd), lambda i: (i, 0)),
        out_shape=jax.ShapeDtypeStruct((m * n_kv, hd), F32),
        compiler_params=_cparams(1),
        name=name,
    )(a, wt)


def matmul_w(a, w, col0, ncols, tm, tn, name="matmul", out_dtype=F32, w_transposed=False, sigmoid=False):
    m, k = a.shape
    assert col0 % tn == 0 and ncols % tn == 0 and m % tm == 0 and (w_transposed or not sigmoid)
    cb = col0 // tn
    if w_transposed:
        w_spec = pl.BlockSpec((tn, k), lambda i, j: (j + cb, 0))
    else:
        w_spec = pl.BlockSpec((k, tn), lambda i, j: (0, j + cb))
    return pl.pallas_call(
        functools.partial(_mm_nt_kernel, sigmoid=sigmoid) if w_transposed else _mm_kernel,
        grid=(m // tm, ncols // tn),
        in_specs=[pl.BlockSpec((tm, k), lambda i, j: (i, 0)), w_spec],
        out_specs=pl.BlockSpec((tm, tn), lambda i, j: (i, j)),
        out_shape=jax.ShapeDtypeStruct((m, ncols), out_dtype),
        compiler_params=_cparams(2),
        name=name,
    )(a, w)


def _rms(x, g):
    return x * lax.rsqrt(jnp.mean(x * x, axis=-1, keepdims=True) + EPS) * g


def _mm_res_norm_kernel(a_ref, w_ref, r_ref, g_ref, o_ref, on_ref):
    x = r_ref[...] + jnp.dot(a_ref[...], w_ref[...], preferred_element_type=F32)
    o_ref[...] = x
    on_ref[...] = _rms(x, g_ref[...]).astype(on_ref.dtype)


def matmul_res_norm(a, w, res, g, tm):
    m, k = a.shape
    n = w.shape[1]
    return pl.pallas_call(
        _mm_res_norm_kernel,
        grid=(m // tm,),
        in_specs=[pl.BlockSpec((tm, k), lambda i: (i, 0)), pl.BlockSpec((k, n), lambda i: (0, 0)),
                  pl.BlockSpec((tm, n), lambda i: (i, 0)), pl.BlockSpec((1, n), lambda i: (0, 0))],
        out_specs=[pl.BlockSpec((tm, n), lambda i: (i, 0)), pl.BlockSpec((tm, n), lambda i: (i, 0))],
        out_shape=[jax.ShapeDtypeStruct((m, n), F32), jax.ShapeDtypeStruct((m, n), BF16)],
        compiler_params=_cparams(1),
        name="mm_o_norm",
    )(a, w, res, g.reshape(1, n))


def _mm_ksplit_norm_kernel(a_ref, w_ref, r_ref, g_ref, o_ref, acc_ref):
    kk = pl.program_id(1)

    @pl.when(kk == 0)
    def _():
        acc_ref[...] = r_ref[...]
    acc_ref[...] += jnp.dot(a_ref[...], w_ref[...], preferred_element_type=F32)

    @pl.when(kk == pl.num_programs(1) - 1)
    def _():
        o_ref[...] = _rms(acc_ref[...], g_ref[...])


def matmul_res_norm_out(a, w, res, g, tm, tk):
    m, k = a.shape
    n = w.shape[1]
    assert k % tk == 0 and m % tm == 0
    return pl.pallas_call(
        _mm_ksplit_norm_kernel,
        grid=(m // tm, k // tk),
        in_specs=[pl.BlockSpec((tm, tk), lambda i, kk: (i, kk)), pl.BlockSpec((tk, n), lambda i, kk: (kk, 0)),
                  pl.BlockSpec((tm, n), lambda i, kk: (i, 0)), pl.BlockSpec((1, n), lambda i, kk: (0, 0))],
        out_specs=pl.BlockSpec((tm, n), lambda i, kk: (i, 0)),
        out_shape=jax.ShapeDtypeStruct((m, n), F32),
        scratch_shapes=[pltpu.VMEM((tm, n), F32)],
        compiler_params=_cparams(2),
        name="mm_down_norm",
    )(a, w, res, g.reshape(1, n))


def _bias_kernel(rb_ref, bk3_ref, bks_ref, o3_ref, os_ref, *, n_heads):
    def head(h, carry):
        far = rb_ref[N_BUCKETS - 1, h]
        for t in range(3):
            bk = bk3_ref[t]
            acc = jnp.zeros(bk.shape, F32)
            for b in range(N_BUCKETS):
                acc = jnp.where(bk == b, rb_ref[b, h], acc)
            o3_ref[t, h] = (acc - far) * LOG2E
        for t in range(2):
            bk = bks_ref[t]
            acc = jnp.zeros(bk.shape, F32)
            for b in range(N_BUCKETS):
                acc = jnp.where(bk == b, rb_ref[b, h], acc)
            os_ref[t, pl.ds(h, 1), :] = acc
        return carry
    lax.fori_loop(0, n_heads, head, 0)


def bias_tables(rel_bias, page, rep):
    n_heads = rel_bias.shape[1]
    cols = page * rep
    i = np.arange(QB)[None, :]
    k = np.arange(QB)[:, None]
    bk3 = np.stack([_rel_bucket_np(i - k + 2 * QB), _rel_bucket_np(i - k + QB), _rel_bucket_np(i - k)])
    assert (_rel_bucket_np(np.arange(QB + 1, 1 << 20)) == N_BUCKETS - 1).all()
    assert (bk3[0] == N_BUCKETS - 1).all()
    assert page >= QB
    bks = np.stack([np.full((1, cols), N_BUCKETS - 1, np.int32),
                    _rel_bucket_np(page - np.arange(cols) // rep)[None, :]])
    return pl.pallas_call(
        functools.partial(_bias_kernel, n_heads=n_heads),
        in_specs=[pl.BlockSpec(memory_space=pltpu.SMEM),
                  pl.BlockSpec(memory_space=pltpu.VMEM), pl.BlockSpec(memory_space=pltpu.VMEM)],
        out_specs=[pl.BlockSpec(memory_space=pltpu.VMEM), pl.BlockSpec(memory_space=pltpu.VMEM)],
        out_shape=[jax.ShapeDtypeStruct((3, n_heads, QB, QB), F32),
                   jax.ShapeDtypeStruct((2, n_heads, cols), F32)],
    )(rel_bias, jnp.asarray(bk3), jnp.asarray(bks))


def _conv_kernel(glu_ref, prev_ref, dw_ref, bdw_ref, lng_ref, lnb_ref, wout_ref, o_ref, st_ref,
                 ext_ref, h_ref, wbf_ref, sh_ref, h2_ref, *, tt, width, dconv):
    b = pl.program_id(0)
    t = pl.program_id(1)
    pad = 32
    hist = width - 1

    @pl.when((b == 0) & (t == 0))
    def _():
        wbf_ref[...] = wout_ref[...].astype(BF16)

    @pl.when(t == 0)
    def _():
        ext_ref[pl.ds(pad - hist, hist), :] = prev_ref[...]

    @pl.when(t > 0)
    def _():
        ext_ref[pl.ds(0, pad), :] = ext_ref[pl.ds(tt, pad), :]

    glu = glu_ref[...]
    u = glu[:, :dconv] * _sigmoid(glu[:, dconv:])
    ext_ref[pl.ds(pad, tt), :] = u
    st_ref[...] = ext_ref[pl.ds(pad + tt - hist, hist), :]

    span = tt + pad - 8
    for s in range(1, 8):
        sh_ref[s - 1] = ext_ref[pl.ds(s, span), :]

    half = tt // 2
    for r, hh_ref in enumerate((h_ref, h2_ref)):
        r0 = r * half
        for c in range(dconv // LANES):
            cs = slice(c * LANES, (c + 1) * LANES)
            acc = jnp.zeros((half, LANES), F32) + bdw_ref[:, cs]
            for j in range(width):
                a, s = divmod(pad - hist + j, 8)
                src = ext_ref if s == 0 else sh_ref.at[s - 1]
                acc = acc + dw_ref[pl.ds(j, 1), cs] * src[pl.ds(8 * a + r0, half), cs]
            hh_ref[:, cs] = acc
        h = hh_ref[...]
        mu = jnp.mean(h, axis=-1, keepdims=True)
        var = jnp.mean(jnp.square(h - mu), axis=-1, keepdims=True)
        y = (h - mu) * lax.rsqrt(var + EPS) * lng_ref[...] + lnb_ref[...]
        y = y * _sigmoid(y)
        o_ref[pl.ds(r0, half), :] = jnp.dot(y.astype(BF16), wbf_ref[...], preferred_element_type=F32)


def conv_branch_prompt(glu_pre, prev, dw, bdw, lng, lnb, wout, n_seq, seq, tt):
    width, dconv = dw.shape
    dm = wout.shape[1]
    nt = seq // tt
    kern = functools.partial(_conv_kernel, tt=tt, width=width, dconv=dconv)
    return pl.pallas_call(
        kern,
        grid=(n_seq, nt),
        in_specs=[pl.BlockSpec((tt, 2 * dconv), lambda b, t: (b * nt + t, 0)),
                  pl.BlockSpec((None, width - 1, dconv), lambda b, t: (b, 0, 0)),
                  pl.BlockSpec((width, dconv), lambda b, t: (0, 0)),
                  pl.BlockSpec((1, dconv), lambda b, t: (0, 0)),
                  pl.BlockSpec((1, dconv), lambda b, t: (0, 0)),
                  pl.BlockSpec((1, dconv), lambda b, t: (0, 0)),
                  pl.BlockSpec((dconv, dm), lambda b, t: (0, 0))],
        out_specs=[pl.BlockSpec((tt, dm), lambda b, t: (b * nt + t, 0)),
                   pl.BlockSpec((None, width - 1, dconv), lambda b, t: (b, 0, 0))],
        out_shape=[jax.ShapeDtypeStruct((n_seq * seq, dm), F32),
                   jax.ShapeDtypeStruct((n_seq, width - 1, dconv), F32)],
        scratch_shapes=[pltpu.VMEM((32 + tt, dconv), F32), pltpu.VMEM((tt // 2, dconv), F32),
                        pltpu.VMEM((dconv, dm), BF16), pltpu.VMEM((7, 24 + tt, dconv), F32),
                        pltpu.VMEM((tt // 2, dconv), F32)],
        compiler_params=_cparams(2),
        name="conv_branch",
    )(glu_pre, prev, dw, bdw.reshape(1, dconv), lng.reshape(1, dconv), lnb.reshape(1, dconv), wout)


def _conv_step_kernel(glu_ref, prev_ref, dw_ref, bdw_ref, lng_ref, lnb_ref, wout_ref, o_ref, u_ref,
                      *, width, dconv):
    glu = glu_ref[...]
    u = glu[:, :dconv] * _sigmoid(glu[:, dconv:])
    u_ref[...] = u
    h = bdw_ref[...] + dw_ref[pl.ds(width - 1, 1), :] * u
    for j in range(width - 1):
        h = h + dw_ref[pl.ds(j, 1), :] * prev_ref[j]
    mu = jnp.mean(h, axis=-1, keepdims=True)
    var = jnp.mean(jnp.square(h - mu), axis=-1, keepdims=True)
    y = (h - mu) * lax.rsqrt(var + EPS) * lng_ref[...] + lnb_ref[...]
    y = y * _sigmoid(y)
    o_ref[...] = jnp.dot(y.astype(BF16), wout_ref[...].astype(BF16), preferred_element_type=F32)


def conv_branch_step(glu_pre, prev_t, dw, bdw, lng, lnb, wout):
    width, dconv = dw.shape
    n = glu_pre.shape[0]
    dm = wout.shape[1]
    kern = functools.partial(_conv_step_kernel, width=width, dconv=dconv)
    return pl.pallas_call(
        kern,
        out_shape=[jax.ShapeDtypeStruct((n, dm), F32), jax.ShapeDtypeStruct((n, dconv), F32)],
        compiler_params=pltpu.CompilerParams(vmem_limit_bytes=VMEM_LIMIT),
    )(glu_pre, prev_t, dw, bdw.reshape(1, dconv), lng.reshape(1, dconv), lnb.reshape(1, dconv), wout)


def _select_threshold(count_gt, row_min, row_max, n_adm, topk, any_fn):
    kf = jnp.float32(topk)
    full = n_adm <= kf
    lo0 = row_min - (1.0 + jnp.abs(row_min))
    hi0 = row_max
    flo0 = jnp.where(full, kf, n_adm)
    fhi0 = jnp.zeros_like(lo0)

    def active_rows(lo, hi, flo):
        mid = 0.5 * lo + 0.5 * hi
        return (flo != kf) & (lo < mid) & (mid < hi)

    def cond(st):
        lo, hi, flo, fhi = st
        return any_fn(active_rows(lo, hi, flo))

    def step(st):
        lo, hi, flo, fhi = st
        act = active_rows(lo, hi, flo)
        mid = 0.5 * lo + 0.5 * hi
        c = count_gt(mid)
        up = act & (c >= kf)
        dn = act & (c < kf)
        return (jnp.where(up, mid, lo), jnp.where(dn, mid, hi),
                jnp.where(up, c, flo), jnp.where(dn, c, fhi))

    def body(st):
        for _ in range(4):
            st = step(st)
        return st

    lo, hi, flo, fhi = lax.while_loop(cond, body, (lo0, hi0, flo0, fhi0))
    lo = jnp.where(full, -jnp.inf, lo)
    return lo, hi, flo, fhi


def _attn_prompt_kernel(qi_ref, wi_ref, kw_ref, q_ref, k_ref, v_ref, co_ref, ga_ref, gb_ref, bias_ref,
                        o_ref, kd_ref, kb_ref, vt_ref, wit_ref, qib_ref, sc_ref, sel_ref, qs_ref, *state,
                        seq, n_heads, n_kv, idx_heads, idx_dim, topk, hd):
    m_refs, l_refs, acc_refs = state[:n_kv], state[n_kv:2 * n_kv], state[2 * n_kv:3 * n_kv]
    st_refs = state[3 * n_kv:]
    j = pl.program_id(1)
    group = n_heads // n_kv
    nck = (j * QB + QB + CK - 1) // CK
    kf = jnp.float32(topk)

    @pl.when(j == 0)
    def _():
        kd_ref[...] = kw_ref[:, :idx_dim].astype(BF16)
        for g in range(n_kv):
            kb_ref[g] = k_ref[pl.ds(g, seq, stride=n_kv), :].astype(BF16)
            for c in range(seq // CK):
                vt_ref[g, c] = v_ref[pl.ds(c * CK * n_kv + g, CK, stride=n_kv), :].T.astype(BF16)

    wit_ref[...] = (wi_ref[...] * (idx_heads ** -0.5)).T
    qi = qi_ref[...].astype(F32) * (idx_dim ** -0.5)
    for h in range(idx_heads):
        qib_ref[h] = qi[:, h * idx_dim:(h + 1) * idx_dim].astype(BF16)
    qpos = j * QB + lax.broadcasted_iota(jnp.int32, (CK, QB), 1)
    krow = lax.broadcasted_iota(jnp.int32, (CK, QB), 0)
    per_dot = 2 * LANES // QB

    def score_chunk(c, carry):
        mn, mx = carry
        k0 = pl.multiple_of(c * CK, CK)
        kc = kd_ref[pl.ds(k0, CK), :]
        acc = jnp.zeros((CK, QB), F32)
        for p in range(idx_heads // per_dot):
            rhs = qib_ref[pl.ds(p * per_dot, per_dot)].reshape(per_dot * QB, idx_dim)
            s = _dot_nt(kc, rhs)
            for r in range(per_dot):
                acc = acc + (wit_ref[pl.ds(idx_dim + p * per_dot + r, 1), :]
                             * jnp.maximum(s[:, r * QB:(r + 1) * QB], 0.0))
        adm = (krow + c * CK) <= qpos
        sc_ref[c] = jnp.where(adm, acc, -jnp.inf)
        mn = jnp.minimum(mn, _fold_rows(jnp.where(adm, acc, jnp.inf), jnp.minimum))
        mx = jnp.maximum(mx, _fold_rows(jnp.where(adm, acc, -jnp.inf), jnp.maximum))
        return mn, mx

    mn8, mx8 = lax.fori_loop(0, nck, score_chunk,
                             (jnp.full((8, QB), jnp.inf, F32), jnp.full((8, QB), -jnp.inf, F32)))
    row_min = jnp.min(mn8, axis=0, keepdims=True)
    row_max = jnp.max(mx8, axis=0, keepdims=True)
    n_adm = (j * QB + 1 + lax.broadcasted_iota(jnp.int32, (1, QB), 1)).astype(F32)

    def count_gt(t):
        def cbody(c, acc):
            return acc + _fold_rows(jnp.where(sc_ref[c] > t, 1.0, 0.0), jnp.add)
        part = lax.fori_loop(0, nck, cbody, jnp.zeros((8, QB), F32))
        return jnp.sum(part, axis=0, keepdims=True)

    def any_fn(mask):
        return jnp.max(jnp.where(mask, 1.0, 0.0)) > 0.0

    lo, hi, flo, fhi = _select_threshold(count_gt, row_min, row_max, n_adm, topk, any_fn)
    tie = flo != kf

    def sel_chunk(c, carry):
        sel_ref[c] = jnp.where(sc_ref[c] > lo, 1.0, 0.0)
        return carry
    lax.fori_loop(0, nck, sel_chunk, 0)

    @pl.when(any_fn(tie))
    def _():
        need = kf - fhi
        lower = (lax.broadcasted_iota(jnp.int32, (CK, CK), 1)
                 < lax.broadcasted_iota(jnp.int32, (CK, CK), 0)).astype(BF16)

        def tie_chunk(c, before):
            s = sc_ref[c]
            eq = s == hi
            eqf = jnp.where(eq, 1.0, 0.0)
            rank = before + jnp.dot(lower, eqf.astype(BF16), preferred_element_type=F32)
            keep = (s > hi) | (eq & (rank < need))
            sel_ref[c] = jnp.where(tie, jnp.where(keep, 1.0, 0.0), sel_ref[c])
            return before + jnp.sum(eqf, axis=0, keepdims=True)
        lax.fori_loop(0, nck, tie_chunk, jnp.zeros((1, QB), F32))

    scale2 = hd ** -0.5 * LOG2E
    for h in range(n_heads):
        qs_ref[h] = q_ref[:, h * hd:(h + 1) * hd].astype(BF16)
    for g in range(n_kv):
        m_refs[g][...] = jnp.full(m_refs[g].shape, NEG, F32)
        l_refs[g][...] = jnp.zeros(l_refs[g].shape, F32)
        acc_refs[g][...] = jnp.zeros(acc_refs[g].shape, F32)

    def logits(g, c):
        kc = kb_ref[g, pl.ds(pl.multiple_of(c * CK, CK), CK), :]
        qg = qs_ref[pl.ds(g * group, group)].reshape(group * QB, hd)
        st_refs[g % 2][...] = _dot_nt(kc, qg)

    logits(0, 0)

    def att_chunk(c, carry, near):
        msk = jnp.concatenate([sel_ref[c]] * group, axis=1) > 0.0
        tis = [jnp.clip(c * (CK // QB) + s - j + 2, 0, 2) for s in range(CK // QB)]
        for g in range(n_kv):
            if g + 1 < n_kv:
                logits(g + 1, c)
            else:
                logits(0, jnp.minimum(c + 1, nck - 1))
            lg = st_refs[g % 2][...] * scale2
            if near:
                rows = [jnp.concatenate([bias_ref[ti, g * group + hh] for hh in range(group)], axis=1)
                        for ti in tis]
                lg = lg + jnp.concatenate(rows, axis=0)
            lg = jnp.where(msk, lg, -jnp.inf)
            m_old = m_refs[g][...]
            m_new = jnp.maximum(m_old, jnp.max(_fold_rows(lg, jnp.maximum), axis=0, keepdims=True))
            alpha = jnp.exp2(m_old - m_new)
            p = jnp.exp2(lg - m_new)
            l_refs[g][...] = alpha * l_refs[g][...] + jnp.sum(_fold_rows(p, jnp.add), axis=0, keepdims=True)
            pv = jnp.dot(vt_ref[g, c], p.astype(BF16), preferred_element_type=F32)
            acc_refs[g][...] = alpha * acc_refs[g][...] + pv
            m_refs[g][...] = m_new
        return carry
    n_far = jnp.maximum((j * (QB // LANES) - 1) // (CK // QB), 0)
    lax.fori_loop(0, n_far, functools.partial(att_chunk, near=False), 0)
    lax.fori_loop(n_far, nck, functools.partial(att_chunk, near=True), 0)

    for g in range(n_kv):
        ot = acc_refs[g][...] / l_refs[g][...]
        for hh in range(group):
            cs = slice((g * group + hh) * hd, (g * group + hh + 1) * hd)
            o = ot[:, hh * QB:(hh + 1) * QB].T
            mixed = ga_ref[:, cs] * co_ref[:, cs] + gb_ref[:, cs] * o
            o_ref[:, cs] = mixed.astype(o_ref.dtype)


def attn_prompt(qi, kw, q, k, v, conv_out, gates, bias3, n_seq, seq, n_kv, idx_dim, idx_heads):
    m, dm = q.shape
    hd = k.shape[1]
    n_heads = dm // hd
    group = n_heads // n_kv
    nb = seq // QB
    topk = min(TOPK_MAX, seq // 4)
    assert seq % CK == 0 and LANES % idx_dim == 0 and kw.shape[1] == LANES
    kern = functools.partial(_attn_prompt_kernel, seq=seq, n_heads=n_heads, n_kv=n_kv, idx_heads=idx_heads,
                             idx_dim=idx_dim, topk=topk, hd=hd)
    row = lambda b, j: (b * nb + j, 0)
    return pl.pallas_call(
        kern,
        grid=(n_seq, nb),
        in_specs=[pl.BlockSpec((QB, idx_heads * idx_dim), row),
                  pl.BlockSpec((QB, LANES), row),
                  pl.BlockSpec((seq, LANES), lambda b, j: (b, 0)),
                  pl.BlockSpec((QB, dm), row),
                  pl.BlockSpec((seq * n_kv, hd), lambda b, j: (b, 0)),
                  pl.BlockSpec((seq * n_kv, hd), lambda b, j: (b, 0)),
                  pl.BlockSpec((QB, dm), row),
                  pl.BlockSpec((QB, dm), lambda b, j: (b * nb + j, 0)),
                  pl.BlockSpec((QB, dm), lambda b, j: (b * nb + j, 1)),
                  pl.BlockSpec((3, n_heads, QB, QB), lambda b, j: (0, 0, 0, 0))],
        out_specs=pl.BlockSpec((QB, dm), row),
        out_shape=jax.ShapeDtypeStruct((m, dm), BF16),
        scratch_shapes=[pltpu.VMEM((seq, idx_dim), BF16),
                        pltpu.VMEM((n_kv, seq, hd), BF16),
                        pltpu.VMEM((n_kv, seq // CK, hd, CK), BF16),
                        pltpu.VMEM((LANES, QB), F32),
                        pltpu.VMEM((idx_heads, QB, idx_dim), BF16),
                        pltpu.VMEM((seq // CK, CK, QB), F32),
                        pltpu.VMEM((seq // CK, CK, QB), F32),
                        pltpu.VMEM((n_heads, QB, hd), BF16)]
                       + [pltpu.VMEM((1, group * QB), F32)] * (2 * n_kv)
                       + [pltpu.VMEM((hd, group * QB), F32)] * n_kv
                       + [pltpu.VMEM((CK, group * QB), F32)] * 2,
        compiler_params=_cparams(2),
        name="attn_prompt",
    )(qi, kw, kw, q, k, v, conv_out, gates, gates, bias3)


def _sample_score_kernel(pt_ref, qi_ref, wi_ref, kn_ref, *rest, pg, idx_heads, idx_dim):
    k_hbm, o_ref, self_ref, kbuf_ref, sem, kcat_ref = rest
    page = k_hbm.shape[2]
    b = pl.program_id(0)
    p = pl.program_id(1)
    n_steps = pl.num_programs(1)
    step = b * n_steps + p
    slot = step % 2

    def page_copies(bb, pp, sl):
        return [pltpu.make_async_copy(k_hbm.at[pt_ref[bb, pp * pg + i]], kbuf_ref.at[sl, i], sem.at[sl])
                for i in range(pg)]

    @pl.when(step == 0)
    def _():
        for c in page_copies(0, 0, 0):
            c.start()

    nxt = step + 1

    @pl.when(nxt < pl.num_programs(0) * n_steps)
    def _():
        for c in page_copies(nxt // n_steps, nxt % n_steps, 1 - slot):
            c.start()

    for c in page_copies(b, p, slot):
        c.wait()

    qi = qi_ref[...] * (idx_dim ** -0.5)
    wi = wi_ref[...] * (idx_heads ** -0.5)
    qb = qi.astype(BF16)
    for i in range(pg):
        kcat_ref[:, i * page:(i + 1) * page] = kbuf_ref[slot, i].astype(BF16)
    s = jnp.dot(qb, kcat_ref[...], preferred_element_type=F32)
    o_ref[...] = jnp.sum(wi * jnp.maximum(s, 0.0), axis=0, keepdims=True)

    @pl.when(pl.program_id(1) == 0)
    def _():
        kn = kn_ref[...].astype(BF16).astype(F32)
        s = jnp.sum(qb.astype(F32) * kn, axis=1, keepdims=True)
        sself = jnp.sum(wi * jnp.maximum(s, 0.0), axis=0, keepdims=True)
        self_ref[...] = jnp.broadcast_to(sself, self_ref.shape)


def sample_scores(page_table, qi3, wi3, ki_new3, cache_kidx_t, pg):
    n, n_pages = page_table.shape
    idx_heads, idx_dim = qi3.shape[1:]
    page = cache_kidx_t.shape[2]
    kern = functools.partial(_sample_score_kernel, pg=pg, idx_heads=idx_heads, idx_dim=idx_dim)
    grid_spec = pltpu.PrefetchScalarGridSpec(
        num_scalar_prefetch=1,
        grid=(n, n_pages // pg),
        in_specs=[pl.BlockSpec((None, idx_heads, idx_dim), lambda b, p, pt: (b, 0, 0)),
                  pl.BlockSpec((None, idx_heads, 1), lambda b, p, pt: (b, 0, 0)),
                  pl.BlockSpec((None, 1, idx_dim), lambda b, p, pt: (b, 0, 0)),
                  pl.BlockSpec(memory_space=pl.ANY)],
        out_specs=[pl.BlockSpec((None, None, 1, pg * page), lambda b, p, pt: (b, p, 0, 0)),
                   pl.BlockSpec((None, 1, LANES), lambda b, p, pt: (b, 0, 0))],
        scratch_shapes=[pltpu.VMEM((2, pg, idx_dim, page), F32), pltpu.SemaphoreType.DMA((2,)),
                        pltpu.VMEM((idx_dim, pg * page), BF16)],
    )
    return pl.pallas_call(
        kern,
        grid_spec=grid_spec,
        out_shape=[jax.ShapeDtypeStruct((n, n_pages // pg, 1, pg * page), F32),
                   jax.ShapeDtypeStruct((n, 1, LANES), F32)],
        compiler_params=_cparams(2),
        name="sample_scores",
    )(page_table, qi3, wi3, ki_new3, cache_kidx_t)


def _sample_select_kernel(sc_ref, self_ref, sel4_ref, selself_ref, sel_ref, *, topk, past, rep):
    sc = sc_ref[...]
    sself = self_ref[:, 0:1]
    n = sc.shape[0]
    kf = jnp.float32(topk)
    row_min = jnp.minimum(jnp.min(sc, axis=1, keepdims=True), sself)
    row_max = jnp.maximum(jnp.max(sc, axis=1, keepdims=True), sself)
    n_adm = jnp.full((n, 1), past + 1, F32)

    def count_gt(t):
        return (jnp.sum(jnp.where(sc > t, 1.0, 0.0), axis=1, keepdims=True)
                + jnp.where(sself > t, 1.0, 0.0))

    def any_fn(mask):
        return jnp.max(jnp.where(mask, 1.0, 0.0)) > 0.0

    lo, hi, flo, fhi = _select_threshold(count_gt, row_min, row_max, n_adm, topk, any_fn)
    tie = flo != kf
    sel_ref[...] = jnp.where(sc > lo, 1.0, 0.0)
    selself_ref[...] = jnp.broadcast_to(jnp.where(sself > lo, 1.0, 0.0), selself_ref.shape)

    @pl.when(any_fn(tie))
    def _():
        need = kf - fhi
        blk = 512
        tri = (lax.broadcasted_iota(jnp.int32, (blk, blk), 0)
               < lax.broadcasted_iota(jnp.int32, (blk, blk), 1)).astype(BF16)
        before = jnp.zeros((n, 1), F32)
        for c in range(past // blk):
            s = sc_ref[:, c * blk:(c + 1) * blk]
            eq = s == hi
            rank = before + jnp.dot(jnp.where(eq, 1.0, 0.0).astype(BF16), tri, preferred_element_type=F32)
            keep = (s > hi) | (eq & (rank < need))
            sel_ref[:, c * blk:(c + 1) * blk] = jnp.where(tie, jnp.where(keep, 1.0, 0.0),
                                                          sel_ref[:, c * blk:(c + 1) * blk])
            before = before + jnp.sum(jnp.where(eq, 1.0, 0.0), axis=1, keepdims=True)
        keep_self = (sself > hi) | ((sself == hi) & (before < need))
        selself_ref[...] = jnp.broadcast_to(
            jnp.where(tie, jnp.where(keep_self, 1.0, 0.0), jnp.where(sself > lo, 1.0, 0.0)), selself_ref.shape)

    blk = 512
    row_lo = lax.broadcasted_iota(jnp.int32, (blk, blk * rep), 0) * rep
    col = lax.broadcasted_iota(jnp.int32, (blk, blk * rep), 1)
    spread = jnp.where((col >= row_lo) & (col < row_lo + rep), 1.0, 0.0).astype(BF16)
    for c in range(past // blk):
        sel4_ref[:, c * blk * rep:(c + 1) * blk * rep] = jnp.dot(
            sel_ref[:, c * blk:(c + 1) * blk].astype(BF16), spread, preferred_element_type=F32)


def sample_select(scores, sself, topk, rep):
    n, past = scores.shape
    kern = functools.partial(_sample_select_kernel, topk=topk, past=past, rep=rep)
    return pl.pallas_call(
        kern,
        out_shape=[jax.ShapeDtypeStruct((n, past * rep), F32), jax.ShapeDtypeStruct((n, LANES), F32)],
        scratch_shapes=[pltpu.VMEM((n, past), F32)],
        compiler_params=pltpu.CompilerParams(vmem_limit_bytes=VMEM_LIMIT),
        name="sample_select",
    )(scores, sself)


def _sample_attn_kernel(pt_ref, q_ref, kn_ref, vn_ref, sel_ref, selself_ref, bias_ref, rb0_ref, own_ref, *rest,
                        pg, n_heads, n_kv, hd):
    k_hbm, v_hbm, o_ref, kbuf_ref, vbuf_ref, sem, kcat_ref, vcat_ref, m_ref, l_ref, acc_ref = rest
    b = pl.program_id(0)
    p = pl.program_id(1)
    n_steps = pl.num_programs(1)
    scale = hd ** -0.5
    rows = k_hbm.shape[1]
    step = b * n_steps + p
    slot = step % 2

    def page_copies(bb, pp, sl):
        copies = []
        for i in range(pg):
            pid = pt_ref[bb, pp * pg + i]
            dst = pl.ds(i * rows, rows)
            copies.append(pltpu.make_async_copy(k_hbm.at[pid], kbuf_ref.at[sl, dst], sem.at[0, sl]))
            copies.append(pltpu.make_async_copy(v_hbm.at[pid], vbuf_ref.at[sl, dst], sem.at[1, sl]))
        return copies

    @pl.when(step == 0)
    def _():
        for c in page_copies(0, 0, 0):
            c.start()

    nxt = step + 1

    @pl.when(nxt < pl.num_programs(0) * n_steps)
    def _():
        for c in page_copies(nxt // n_steps, nxt % n_steps, 1 - slot):
            c.start()

    for c in page_copies(b, p, slot):
        c.wait()

    @pl.when(p == 0)
    def _():
        m_ref[...] = jnp.full(m_ref.shape, NEG, F32)
        l_ref[...] = jnp.zeros(l_ref.shape, F32)
        acc_ref[...] = jnp.zeros(acc_ref.shape, F32)

    for i in range(pg):
        kcat_ref[i * rows:(i + 1) * rows, :] = kbuf_ref[slot, pl.ds(i * rows, rows), :].astype(BF16)
        vcat_ref[i * rows:(i + 1) * rows, :] = vbuf_ref[slot, pl.ds(i * rows, rows), :].astype(BF16)
    qb = q_ref[...].astype(BF16)
    last = p == n_steps - 1
    bias = jnp.concatenate([bias_ref[0]] * (pg - 1) + [jnp.where(last, bias_ref[1], bias_ref[0])], axis=1)
    msk = (own_ref[...] > 0.0) & (sel_ref[...] > 0.0)
    lg = jnp.where(msk, _dot_nt(qb, kcat_ref[...]) * scale + bias, -jnp.inf)
    m_old = m_ref[...]
    m_new = jnp.maximum(m_old, jnp.max(lg, axis=-1, keepdims=True))
    alpha = jnp.exp(m_old - m_new)
    pr = jnp.exp(lg - m_new)
    l_new = alpha * l_ref[...] + jnp.sum(pr, axis=-1, keepdims=True)
    acc = alpha * acc_ref[...] + jnp.dot(pr.astype(BF16), vcat_ref[...], preferred_element_type=F32)
    m_ref[...] = m_new
    l_ref[...] = l_new
    acc_ref[...] = acc

    @pl.when(last)
    def _():
        kn = kn_ref[...].astype(BF16).astype(F32)
        vn = vn_ref[...].astype(BF16).astype(F32)
        ls = jnp.sum(qb.astype(F32) * kn, axis=-1, keepdims=True) * scale + rb0_ref[...]
        on = selself_ref[:, 0:1] > 0.0
        ls = jnp.where(on, ls, NEG)
        m_f = jnp.maximum(m_new, ls)
        a2 = jnp.exp(m_new - m_f)
        ps = jnp.where(on, jnp.exp(ls - m_f), 0.0)
        l_f = a2 * l_new + ps
        acc_f = a2 * acc + ps.astype(BF16).astype(F32) * vn
        o_ref[...] = acc_f / l_f


def sample_attention(page_table, q3, k_rep, v_rep, sel4, selself3, bias_s, rb0, cache_k, cache_v, pg, n_kv):
    n, n_pages = page_table.shape
    n_heads, hd = q3.shape[1:]
    rows = cache_k.shape[1]
    kern = functools.partial(_sample_attn_kernel, pg=pg, n_heads=n_heads, n_kv=n_kv, hd=hd)
    own = (np.arange(pg * rows)[None, :] % n_kv
           == np.arange(n_heads)[:, None] // (n_heads // n_kv)).astype(np.float32)
    hbm = pl.BlockSpec(memory_space=pl.ANY)
    per_seq = lambda b, p, pt: (b, 0, 0)
    grid_spec = pltpu.PrefetchScalarGridSpec(
        num_scalar_prefetch=1,
        grid=(n, n_pages // pg),
        in_specs=[pl.BlockSpec((None, n_heads, hd), per_seq),
                  pl.BlockSpec((None, n_heads, hd), per_seq),
                  pl.BlockSpec((None, n_heads, hd), per_seq),
                  pl.BlockSpec((None, None, 1, pg * rows), lambda b, p, pt: (b, p, 0, 0)),
                  pl.BlockSpec((None, 1, LANES), per_seq),
                  pl.BlockSpec((2, n_heads, rows), lambda b, p, pt: (0, 0, 0)),
                  pl.BlockSpec((n_heads, 1), lambda b, p, pt: (0, 0)),
                  pl.BlockSpec((n_heads, pg * rows), lambda b, p, pt: (0, 0)), hbm, hbm],
        out_specs=pl.BlockSpec((None, n_heads, hd), per_seq),
        scratch_shapes=[pltpu.VMEM((2, pg * rows, hd), F32), pltpu.VMEM((2, pg * rows, hd), F32),
                        pltpu.SemaphoreType.DMA((2, 2)),
                        pltpu.VMEM((pg * rows, hd), BF16), pltpu.VMEM((pg * rows, hd), BF16),
                        pltpu.VMEM((n_heads, 1), F32), pltpu.VMEM((n_heads, 1), F32),
                        pltpu.VMEM((n_heads, hd), F32)],
    )
    return pl.pallas_call(
        kern,
        grid_spec=grid_spec,
        out_shape=jax.ShapeDtypeStruct((n, n_heads, hd), F32),
        compiler_params=_cparams(2),
        name="sample_attn",
    )(page_table, q3, k_rep, v_rep, sel4, selself3, bias_s, rb0, jnp.asarray(own), cache_k, cache_v)


def _mix_kernel(ga_ref, gb_ref, co_ref, at_ref, o_ref):
    o_ref[...] = (ga_ref[...] * co_ref[...] + gb_ref[...] * at_ref[...]).astype(o_ref.dtype)


def gated_mix(gates, conv_out, attn):
    n, dm = conv_out.shape
    return pl.pallas_call(
        _mix_kernel,
        grid=(1,),
        in_specs=[pl.BlockSpec((n, dm), lambda i: (0, 0)), pl.BlockSpec((n, dm), lambda i: (0, 1)),
                  pl.BlockSpec((n, dm), lambda i: (0, 0)), pl.BlockSpec((n, dm), lambda i: (0, 0))],
        out_specs=pl.BlockSpec((n, dm), lambda i: (0, 0)),
        out_shape=jax.ShapeDtypeStruct((n, dm), BF16),
    )(gates, gates, conv_out, attn)


def _ffn_up_kernel(x_ref, xh_ref, wg_ref, wv_ref, dg_ref, dv_ref, bg_ref, bv_ref, pg_ref, pv_ref,
                   o_ref, sg_ref, sv_ref, eg_ref, ev_ref, wgb_ref, wvb_ref, *, tm, rs, width, tiles_per_seq):
    hist = width - 1
    pad = xh_ref.shape[0]
    first = pl.program_id(0) % tiles_per_seq == 0
    kc = 256
    for wf_ref, wb_ref in ((wg_ref, wgb_ref), (wv_ref, wvb_ref)):
        for k0 in range(0, wf_ref.shape[0], kc):
            wb_ref[k0:k0 + kc, :] = wf_ref[k0:k0 + kc, :].astype(BF16)
    branches = ((wgb_ref, dg_ref, bg_ref, pg_ref, eg_ref, sg_ref),
                (wvb_ref, dv_ref, bv_ref, pv_ref, ev_ref, sv_ref))
    for w_ref, _, _, prev_ref, e_ref, _ in branches:
        e_ref[0, pl.ds(0, pad), :] = jnp.dot(xh_ref[...], w_ref[...], preferred_element_type=F32)

        @pl.when(first)
        def _():
            e_ref[0, pl.ds(pad - hist, hist), :] = prev_ref[...]

    def project(r):
        for w_ref, _, _, _, e_ref, _ in branches:
            e_ref[r % 2, pl.ds(pad, rs), :] = jnp.dot(x_ref[pl.ds(r * rs, rs), :], w_ref[...],
                                                      preferred_element_type=F32)

    def hand_over(r):
        for _, _, _, _, e_ref, _ in branches:
            e_ref[(r + 1) % 2, pl.ds(0, pad), :] = e_ref[r % 2, pl.ds(rs, pad), :]

    def activate(r):
        outs = []
        for _, d_ref, b_ref, _, e_ref, _ in branches:
            acc = b_ref[...] + d_ref[pl.ds(hist, 1), :] * e_ref[r % 2, pl.ds(pad, rs), :]
            for j in range(hist):
                acc = acc + d_ref[pl.ds(j, 1), :] * e_ref[r % 2, pl.ds(pad - hist + j, rs), :]
            outs.append(acc)
        g, v = outs
        o_ref[pl.ds(r * rs, rs), :] = (g * _sigmoid(g) * v).astype(o_ref.dtype)

    n_sub = tm // rs
    project(0)
    hand_over(0)
    for r in range(1, n_sub):
        project(r)
        activate(r - 1)
        hand_over(r)
    activate(n_sub - 1)
    for _, _, _, _, e_ref, s_ref in branches:
        s_ref[...] = e_ref[(n_sub - 1) % 2, pl.ds(pad + rs - hist, hist), :]


def ffn_up_act(xn, w_up, prev, dw, bdw, n_seq, seq, tm, tn):
    m, k = xn.shape
    f2 = w_up.shape[1]
    f = f2 // 2
    width = dw.shape[0]
    ncb = f // tn
    pad = 16
    tps = seq // tm
    assert seq % tm == 0 and tm % pad == 0 and f % tn == 0 and width - 1 <= pad
    rs = min(128, tm // 2)
    assert tm % rs == 0 and rs % 8 == 0 and k % 256 == 0
    kern = functools.partial(_ffn_up_kernel, tm=tm, rs=rs, width=width, tiles_per_seq=tps)
    hb = tm // pad
    st_shape = jax.ShapeDtypeStruct((m // tm, width - 1, f), F32)
    act, sg, sv = pl.pallas_call(
        kern,
        grid=(m // tm, ncb),
        in_specs=[pl.BlockSpec((tm, k), lambda i, c: (i, 0)),
                  pl.BlockSpec((pad, k), lambda i, c: (jnp.maximum(i * hb - 1, 0), 0)),
                  pl.BlockSpec((k, tn), lambda i, c: (0, c)),
                  pl.BlockSpec((k, tn), lambda i, c: (0, c + ncb)),
                  pl.BlockSpec((width, tn), lambda i, c: (0, c)),
                  pl.BlockSpec((width, tn), lambda i, c: (0, c + ncb)),
                  pl.BlockSpec((1, tn), lambda i, c: (0, c)),
                  pl.BlockSpec((1, tn), lambda i, c: (0, c + ncb)),
                  pl.BlockSpec((None, width - 1, tn), lambda i, c: (i // tps, 0, c)),
                  pl.BlockSpec((None, width - 1, tn), lambda i, c: (i // tps, 0, c + ncb))],
        out_specs=[pl.BlockSpec((tm, tn), lambda i, c: (i, c)),
                   pl.BlockSpec((None, width - 1, tn), lambda i, c: (i, 0, c)),
                   pl.BlockSpec((None, width - 1, tn), lambda i, c: (i, 0, c))],
        out_shape=[jax.ShapeDtypeStruct((m, f), BF16), st_shape, st_shape],
        scratch_shapes=[pltpu.VMEM((2, pad + rs, tn), F32), pltpu.VMEM((2, pad + rs, tn), F32),
                        pltpu.VMEM((k, tn), BF16), pltpu.VMEM((k, tn), BF16)],
        compiler_params=_cparams(2),
        name="ffn_up_act",
    )(xn, xn, w_up, w_up, dw, dw, bdw.reshape(1, f2), bdw.reshape(1, f2), prev, prev)
    state = jnp.concatenate([sg, sv], axis=-1).reshape(n_seq, tps, width - 1, f2)[:, tps - 1]
    return act, state


def _ffn_act_step_kernel(ug_ref, uv_ref, pg_ref, pv_ref, wg_ref, wv_ref, bg_ref, bv_ref, o_ref, *, width):
    def conv(u_ref, prev_ref, w_ref, b_ref):
        acc = b_ref[...] + w_ref[pl.ds(width - 1, 1), :] * u_ref[...]
        for j in range(width - 1):
            acc = acc + w_ref[pl.ds(j, 1), :] * prev_ref[j]
        return acc
    g = conv(ug_ref, pg_ref, wg_ref, bg_ref)
    v = conv(uv_ref, pv_ref, wv_ref, bv_ref)
    o_ref[...] = (g * _sigmoid(g) * v).astype(o_ref.dtype)


def ffn_act_step(u, prev_t, dw, bdw, tc):
    n, f2 = u.shape
    f = f2 // 2
    width = dw.shape[0]
    ncb = f // tc
    kern = functools.partial(_ffn_act_step_kernel, width=width)
    return pl.pallas_call(
        kern,
        grid=(ncb,),
        in_specs=[pl.BlockSpec((n, tc), lambda c: (0, c)),
                  pl.BlockSpec((n, tc), lambda c: (0, c + ncb)),
                  pl.BlockSpec((width - 1, n, tc), lambda c: (0, 0, c)),
                  pl.BlockSpec((width - 1, n, tc), lambda c: (0, 0, c + ncb)),
                  pl.BlockSpec((width, tc), lambda c: (0, c)),
                  pl.BlockSpec((width, tc), lambda c: (0, c + ncb)),
                  pl.BlockSpec((1, tc), lambda c: (0, c)),
                  pl.BlockSpec((1, tc), lambda c: (0, c + ncb))],
        out_specs=pl.BlockSpec((n, tc), lambda c: (0, c)),
        out_shape=jax.ShapeDtypeStruct((n, f), BF16),
        compiler_params=_cparams(1),
    )(u, u, prev_t, prev_t, dw, dw, bdw.reshape(1, f2), bdw.reshape(1, f2))


def _row_tile(m, cap):
    tm = cap
    while m % tm:
        tm //= 2
    return tm if tm >= 16 else m


def _in_projection(xn, w_main, w_kw, w_g, sizes, tm, hd):
    glu_w, q_w, k_w, v_w, qi_w, ki_w, wi_w, ga_w, gb_w = sizes
    offs = np.concatenate([[0], np.cumsum(sizes)])
    tn = 512
    t = dict(w_transposed=True)
    wide = lambda w, o: 2 * tn if w % (2 * tn) == 0 and o % (2 * tn) == 0 else tn
    glu = matmul_w(xn, w_main, int(offs[0]), glu_w, tm, wide(glu_w, int(offs[0])), name="mm_glu", **t)
    q = matmul_w(xn, w_main, int(offs[1]), q_w, tm, wide(q_w, int(offs[1])), name="mm_q", out_dtype=BF16, **t)
    k = matmul_heads(xn, w_main, int(offs[2]), k_w // hd, hd, tm, "mm_k")
    v = matmul_heads(xn, w_main, int(offs[3]), v_w // hd, hd, tm, "mm_v")
    qi = matmul_w(xn, w_main, int(offs[4]), qi_w, tm, min(tn, qi_w), name="mm_qi", out_dtype=BF16, **t)
    kw = matmul_w(xn, w_kw, 0, LANES, tm, LANES, name="mm_kw", **t)
    gates = matmul_w(xn, w_g, 0, ga_w + gb_w, tm, wide(ga_w + gb_w, 0), name="mm_gates", sigmoid=True, **t)
    return glu, q, k, v, qi, kw, gates


def kernel(x_prompt, x_sample, cache_k, cache_v, cache_kidx, state_conv, state_ffn, page_table, rel_bias,
           norm_attn, w_in, dw_conv, b_dw_conv, ln_conv_g, ln_conv_b, w_conv_out, w_o, norm_ffn, w_up, dw_ffn,
           b_dw_ffn, w_down, norm_final):
    bsz, seq, dm = x_prompt.shape
    nd, dec_seq, _ = x_sample.shape
    depth, n_pool, page, n_kv, hd = cache_k.shape
    idx_dim = cache_kidx.shape[-1]
    n_pages = page_table.shape[1]
    past = n_pages * page
    width, dconv = dw_conv.shape[1:]
    fwidth = dw_ffn.shape[1]
    f = w_down.shape[1]
    n_heads = w_o.shape[1] // hd
    d_attn = n_heads * hd
    d_kv = n_kv * hd
    n_in = w_in.shape[2]
    idx_heads = (n_in - 2 * dconv - d_attn - 2 * d_kv - idx_dim - 2 * dm) // (idx_dim + 1)
    sizes = (2 * dconv, d_attn, d_kv, d_kv, idx_heads * idx_dim, idx_dim, idx_heads, dm, dm)
    assert sum(sizes) == n_in and depth == 1 and dec_seq == 1 and page == LANES and d_attn == dm

    mp = bsz * seq
    xp = x_prompt.reshape(mp, dm)
    xs = x_sample.reshape(nd, dm)
    bias3, bias_s = bias_tables(rel_bias, page, n_kv)
    tmp = _row_tile(mp, 2048)
    drop = lambda a: a.reshape(a.shape[1:])
    (norm_attn, w_in, dw_conv, b_dw_conv, ln_conv_g, ln_conv_b, w_conv_out, w_o, norm_ffn, w_up, dw_ffn,
     b_dw_ffn, w_down, state_conv, state_ffn) = map(drop, (
         norm_attn, w_in, dw_conv, b_dw_conv, ln_conv_g, ln_conv_b, w_conv_out, w_o, norm_ffn, w_up, dw_ffn,
         b_dw_ffn, w_down, state_conv, state_ffn))
    kidx_pool = jnp.swapaxes(cache_kidx.reshape(n_pool, page, idx_dim), 1, 2)
    k_pool = cache_k.reshape(n_pool, page * n_kv, hd)
    v_pool = cache_v.reshape(n_pool, page * n_kv, hd)

    n_aligned = sum(sizes[:5])
    n_small = idx_dim + idx_heads
    w_in_t = jnp.swapaxes(w_in, 0, 1)
    w_main = cast_rows_bf16(w_in_t, 0, n_aligned)
    w_kw = cast_rows_bf16(w_in_t, n_aligned, LANES, n_small)
    w_g = cast_rows_bf16(w_in_t, n_aligned + n_small, 2 * dm)
    w_o, w_down = cast_bf16(w_o), cast_bf16(w_down)

    xn = rmsnorm_rows(xp, norm_attn, BF16, 512)
    glu, q, k, v, qi, kw, gates = _in_projection(xn, w_main, w_kw, w_g, sizes, tmp, hd)
    conv0 = jnp.zeros((bsz, width - 1, dconv), F32)
    conv_out, conv_state_p = conv_branch_prompt(glu, conv0, dw_conv, b_dw_conv, ln_conv_g, ln_conv_b,
                                                w_conv_out, bsz, seq, 256)
    mixed = attn_prompt(qi, kw, q, k, v, conv_out, gates, bias3, bsz, seq, n_kv, idx_dim, idx_heads)
    x2, xn2 = matmul_res_norm(mixed, w_o, xp, norm_ffn, _row_tile(mp, 512))
    ffn0 = jnp.zeros((bsz, fwidth - 1, 2 * f), F32)
    act, ffn_state_p = ffn_up_act(xn2, w_up, ffn0, dw_ffn, b_dw_ffn, bsz, seq, _row_tile(seq, 2048), 512)
    tk_down = f // 4 if f % (4 * LANES) == 0 else f
    tk_p = 512 if f % 512 == 0 else tk_down
    y_prompt = matmul_res_norm_out(act, w_down, x2, norm_final, _row_tile(mp, 1024), tk_p).reshape(bsz, seq, dm)

    xns = rmsnorm_rows(xs, norm_attn, BF16, nd)
    glu_s, q_s, k_s, v_s, qi_s, kw_s, gates_s = _in_projection(xns, w_main, w_kw, w_g, sizes, nd, hd)
    ki_s = kw_s[:, :idx_dim]
    wi_s = kw_s[:, idx_dim:idx_dim + idx_heads]
    sc_prev_t = jnp.swapaxes(state_conv, 0, 1)
    conv_out_s, u_conv_s = conv_branch_step(glu_s, sc_prev_t, dw_conv, b_dw_conv, ln_conv_g, ln_conv_b,
                                            w_conv_out)
    conv_state_s = jnp.concatenate([state_conv[:, 1:], u_conv_s[:, None, :]], axis=1)

    pg = 16 if n_pages % 16 == 0 else 8
    pg_idx = 32 if n_pages % 32 == 0 else pg
    scores3, sself3 = sample_scores(page_table, qi_s.reshape(nd, idx_heads, idx_dim),
                                    wi_s.reshape(nd, idx_heads, 1), ki_s.reshape(nd, 1, idx_dim), kidx_pool,
                                    pg_idx)
    topk_s = min(TOPK_MAX, (past + dec_seq) // 4)
    sel4, selself = sample_select(scores3.reshape(nd, past), sself3.reshape(nd, LANES), topk_s, n_kv)
    group = n_heads // n_kv
    k_rep = jnp.repeat(k_s.reshape(nd, n_kv, hd), group, axis=1)
    v_rep = jnp.repeat(v_s.reshape(nd, n_kv, hd), group, axis=1)
    attn_s = sample_attention(page_table, q_s.reshape(nd, n_heads, hd), k_rep, v_rep,
                              sel4.reshape(nd, n_pages // pg, 1, pg * page * n_kv), selself.reshape(nd, 1, LANES),
                              bias_s,
                              rel_bias[0].reshape(n_heads, 1), k_pool, v_pool, pg, n_kv)
    mixed_s = gated_mix(gates_s, conv_out_s, attn_s.reshape(nd, dm))
    x2s, xn2s = matmul_res_norm(mixed_s, w_o, xs, norm_ffn, nd)
    u_s = matmul_w(xn2s, w_up, 0, 2 * f, nd, 512)
    sf_prev_t = jnp.swapaxes(state_ffn, 0, 1)
    act_s = ffn_act_step(u_s, sf_prev_t, dw_ffn, b_dw_ffn, 512)
    y_sample = matmul_res_norm_out(act_s, w_down, x2s, norm_final, nd, tk_down).reshape(nd, dec_seq, dm)
    ffn_state_s = jnp.concatenate([state_ffn[:, 1:], u_s[:, None, :]], axis=1)

    return (y_prompt, y_sample,
            k.reshape(1, bsz, seq, n_kv, hd), v.reshape(1, bsz, seq, n_kv, hd),
            kw.reshape(bsz, seq, LANES)[None, :, :, :idx_dim],
            conv_state_p[None], ffn_state_p[None],
            k_s.reshape(1, nd, dec_seq, n_kv, hd), v_s.reshape(1, nd, dec_seq, n_kv, hd),
            ki_s.reshape(1, nd, dec_seq, idx_dim),
            conv_state_s[None], ffn_state_s[None])
```

```python
import functools
import math

import numpy as np
import jax
import jax.numpy as jnp
from jax import lax
from jax.experimental import pallas as pl
from jax.experimental.pallas import tpu as pltpu

F32 = jnp.float32
BF16 = jnp.bfloat16

EPS = 1e-6
TOPK_MAX = 256
N_BUCKETS = 32
MAX_DISTANCE = 128
QB = 128
CK = 256
LANES = 128
NEG = -1e30
LOG2E = math.log2(math.e)
VMEM_LIMIT = 56 * 1024 * 1024


def _cparams(n_axes, vmem=VMEM_LIMIT):
    return pltpu.CompilerParams(dimension_semantics=("arbitrary",) * n_axes, vmem_limit_bytes=vmem)


def _dot_nt(a, b):
    return lax.dot_general(a, b, (((1,), (1,)), ((), ())), preferred_element_type=F32)


def _sigmoid(x):
    return 1.0 / (1.0 + jnp.exp(-x))


def _fold_rows(x, op):
    while x.shape[0] > 8:
        half = x.shape[0] // 2
        x = op(x[:half], x[half:])
    return x


def _rel_bucket_np(dist):
    n = np.maximum(dist, 0)
    max_exact = N_BUCKETS // 2
    nf = np.maximum(n, 1).astype(np.float32)
    large = max_exact + (np.log(nf / np.float32(max_exact)) / np.float32(math.log(MAX_DISTANCE / max_exact))
                         * np.float32(N_BUCKETS - max_exact)).astype(np.int32)
    large = np.minimum(large, N_BUCKETS - 1)
    return np.where(n < max_exact, n, large).astype(np.int32)


def _rms_kernel(x_ref, g_ref, o_ref):
    x = x_ref[...]
    y = x * lax.rsqrt(jnp.mean(x * x, axis=-1, keepdims=True) + EPS) * g_ref[...]
    o_ref[...] = y.astype(o_ref.dtype)


def rmsnorm_rows(x, g, out_dtype, tm):
    m, d = x.shape
    return pl.pallas_call(
        _rms_kernel,
        grid=(m // tm,),
        in_specs=[pl.BlockSpec((tm, d), lambda i: (i, 0)), pl.BlockSpec((1, d), lambda i: (0, 0))],
        out_specs=pl.BlockSpec((tm, d), lambda i: (i, 0)),
        out_shape=jax.ShapeDtypeStruct((m, d), out_dtype),
        compiler_params=_cparams(1),
        name="rmsnorm",
    )(x, g.reshape(1, d))


def _cast_kernel(w_ref, o_ref):
    o_ref[...] = w_ref[...].astype(o_ref.dtype)


def cast_bf16(w, ncols=None):
    k, n = w.shape
    ncols = n if ncols is None else ncols
    tk = 512 if k % 512 == 0 else k
    tn = 1024 if ncols % 1024 == 0 else (512 if ncols % 512 == 0 else ncols)
    return pl.pallas_call(
        _cast_kernel,
        grid=(k // tk, ncols // tn),
        in_specs=[pl.BlockSpec((tk, tn), lambda i, j: (i, j))],
        out_specs=pl.BlockSpec((tk, tn), lambda i, j: (i, j)),
        out_shape=jax.ShapeDtypeStruct((k, ncols), BF16),
        compiler_params=_cparams(2),
        name="cast_bf16",
    )(w)


def _cast_rows_kernel(w_ref, o_ref, *, valid):
    w = w_ref[...]
    if valid < w.shape[0]:
        w = jnp.where(lax.broadcasted_iota(jnp.int32, w.shape, 0) < valid, w, 0.0)
    o_ref[...] = w.astype(o_ref.dtype)


def cast_rows_bf16(wt, row0, nrows, nvalid=None):
    n, k = wt.shape
    tr = 512 if nrows % 512 == 0 else nrows
    nvalid = nrows if nvalid is None else nvalid
    assert row0 % 8 == 0 and (nvalid == nrows or tr == nrows) and row0 + nrows <= n
    return pl.pallas_call(
        functools.partial(_cast_rows_kernel, valid=nvalid),
        grid=(nrows // tr,),
        in_specs=[pl.BlockSpec((pl.Element(tr), pl.Element(k)), lambda i: (pl.multiple_of(row0 + i * tr, 8), 0))],
        out_specs=pl.BlockSpec((tr, k), lambda i: (i, 0)),
        out_shape=jax.ShapeDtypeStruct((nrows, k), BF16),
        compiler_params=_cparams(1),
        name="cast_rows_bf16",
    )(wt)


def _mm_kernel(a_ref, w_ref, o_ref):
    o_ref[...] = jnp.dot(a_ref[...], w_ref[...].astype(BF16), preferred_element_type=F32).astype(o_ref.dtype)


def _mm_nt_kernel(a_ref, wt_ref, o_ref, *, sigmoid):
    y = _dot_nt(a_ref[...], wt_ref[...])
    o_ref[...] = (_sigmoid(y) if sigmoid else y).astype(o_ref.dtype)


def _mm_nt_heads_kernel(a_ref, wt_ref, o_ref, *, n_kv):
    y = _dot_nt(a_ref[...], wt_ref[...])
    tm, hd = y.shape[0], y.shape[1] // n_kv
    for g in range(n_kv):
        o_ref[pl.ds(g, tm, stride=n_kv), :] = y[:, g * hd:(g + 1) * hd]


def matmul_heads(a, wt, row0, n_kv, hd, tm, name):
    m, k = a.shape
    width = n_kv * hd
    assert row0 % width == 0 and m % tm == 0
    return pl.pallas_call(
        functools.partial(_mm_nt_heads_kernel, n_kv=n_kv),
        grid=(m // tm,),
        in_specs=[pl.BlockSpec((tm, k), lambda i: (i, 0)), pl.BlockSpec((width, k), lambda i: (row0 // width, 0))],
        out_specs=pl.BlockSpec((tm * n_kv, hd), lambda i: (i, 0)),
        out_shape=jax.ShapeDtypeStruct((m * n_kv, hd), F32),
        compiler_params=_cparams(1),
        name=name,
    )(a, wt)


def matmul_w(a, w, col0, ncols, tm, tn, name="matmul", out_dtype=F32, w_transposed=False, sigmoid=False):
    m, k = a.shape
    assert col0 % tn == 0 and ncols % tn == 0 and m % tm == 0 and (w_transposed or not sigmoid)
    cb = col0 // tn
    if w_transposed:
        w_spec = pl.BlockSpec((tn, k), lambda i, j: (j + cb, 0))
    else:
        w_spec = pl.BlockSpec((k, tn), lambda i, j: (0, j + cb))
    return pl.pallas_call(
        functools.partial(_mm_nt_kernel, sigmoid=sigmoid) if w_transposed else _mm_kernel,
        grid=(m // tm, ncols // tn),
        in_specs=[pl.BlockSpec((tm, k), lambda i, j: (i, 0)), w_spec],
        out_specs=pl.BlockSpec((tm, tn), lambda i, j: (i, j)),
        out_shape=jax.ShapeDtypeStruct((m, ncols), out_dtype),
        compiler_params=_cparams(2),
        name=name,
    )(a, w)


def _rms(x, g):
    return x * lax.rsqrt(jnp.mean(x * x, axis=-1, keepdims=True) + EPS) * g


def _mm_res_norm_kernel(a_ref, w_ref, r_ref, g_ref, o_ref, on_ref):
    x = r_ref[...] + jnp.dot(a_ref[...], w_ref[...], preferred_element_type=F32)
    o_ref[...] = x
    on_ref[...] = _rms(x, g_ref[...]).astype(on_ref.dtype)


def matmul_res_norm(a, w, res, g, tm):
    m, k = a.shape
    n = w.shape[1]
    return pl.pallas_call(
        _mm_res_norm_kernel,
        grid=(m // tm,),
        in_specs=[pl.BlockSpec((tm, k), lambda i: (i, 0)), pl.BlockSpec((k, n), lambda i: (0, 0)),
                  pl.BlockSpec((tm, n), lambda i: (i, 0)), pl.BlockSpec((1, n), lambda i: (0, 0))],
        out_specs=[pl.BlockSpec((tm, n), lambda i: (i, 0)), pl.BlockSpec((tm, n), lambda i: (i, 0))],
        out_shape=[jax.ShapeDtypeStruct((m, n), F32), jax.ShapeDtypeStruct((m, n), BF16)],
        compiler_params=_cparams(1),
        name="mm_o_norm",
    )(a, w, res, g.reshape(1, n))


def _mm_ksplit_norm_kernel(a_ref, w_ref, r_ref, g_ref, o_ref, acc_ref):
    kk = pl.program_id(1)

    @pl.when(kk == 0)
    def _():
        acc_ref[...] = r_ref[...]
    acc_ref[...] += jnp.dot(a_ref[...], w_ref[...], preferred_element_type=F32)

    @pl.when(kk == pl.num_programs(1) - 1)
    def _():
        o_ref[...] = _rms(acc_ref[...], g_ref[...])


def matmul_res_norm_out(a, w, res, g, tm, tk):
    m, k = a.shape
    n = w.shape[1]
    assert k % tk == 0 and m % tm == 0
    return pl.pallas_call(
        _mm_ksplit_norm_kernel,
        grid=(m // tm, k // tk),
        in_specs=[pl.BlockSpec((tm, tk), lambda i, kk: (i, kk)), pl.BlockSpec((tk, n), lambda i, kk: (kk, 0)),
                  pl.BlockSpec((tm, n), lambda i, kk: (i, 0)), pl.BlockSpec((1, n), lambda i, kk: (0, 0))],
        out_specs=pl.BlockSpec((tm, n), lambda i, kk: (i, 0)),
        out_shape=jax.ShapeDtypeStruct((m, n), F32),
        scratch_shapes=[pltpu.VMEM((tm, n), F32)],
        compiler_params=_cparams(2),
        name="mm_down_norm",
    )(a, w, res, g.reshape(1, n))


def _bias_kernel(rb_ref, bk3_ref, bks_ref, o3_ref, os_ref, *, n_heads):
    def head(h, carry):
        far = rb_ref[N_BUCKETS - 1, h]
        for t in range(3):
            bk = bk3_ref[t]
            acc = jnp.zeros(bk.shape, F32)
            for b in range(N_BUCKETS):
                acc = jnp.where(bk == b, rb_ref[b, h], acc)
            o3_ref[t, h] = (acc - far) * LOG2E
        for t in range(2):
            bk = bks_ref[t]
            acc = jnp.zeros(bk.shape, F32)
            for b in range(N_BUCKETS):
                acc = jnp.where(bk == b, rb_ref[b, h], acc)
            os_ref[t, pl.ds(h, 1), :] = acc
        return carry
    lax.fori_loop(0, n_heads, head, 0)


def bias_tables(rel_bias, page, rep):
    n_heads = rel_bias.shape[1]
    cols = page * rep
    i = np.arange(QB)[None, :]
    k = np.arange(QB)[:, None]
    bk3 = np.stack([_rel_bucket_np(i - k + 2 * QB), _rel_bucket_np(i - k + QB), _rel_bucket_np(i - k)])
    assert (_rel_bucket_np(np.arange(QB + 1, 1 << 20)) == N_BUCKETS - 1).all()
    assert (bk3[0] == N_BUCKETS - 1).all()
    assert page >= QB
    bks = np.stack([np.full((1, cols), N_BUCKETS - 1, np.int32),
                    _rel_bucket_np(page - np.arange(cols) // rep)[None, :]])
    return pl.pallas_call(
        functools.partial(_bias_kernel, n_heads=n_heads),
        in_specs=[pl.BlockSpec(memory_space=pltpu.SMEM),
                  pl.BlockSpec(memory_space=pltpu.VMEM), pl.BlockSpec(memory_space=pltpu.VMEM)],
        out_specs=[pl.BlockSpec(memory_space=pltpu.VMEM), pl.BlockSpec(memory_space=pltpu.VMEM)],
        out_shape=[jax.ShapeDtypeStruct((3, n_heads, QB, QB), F32),
                   jax.ShapeDtypeStruct((2, n_heads, cols), F32)],
    )(rel_bias, jnp.asarray(bk3), jnp.asarray(bks))


def _conv_kernel(glu_ref, prev_ref, dw_ref, bdw_ref, lng_ref, lnb_ref, wout_ref, o_ref, st_ref,
                 ext_ref, h_ref, wbf_ref, sh_ref, h2_ref, *, tt, width, dconv):
    b = pl.program_id(0)
    t = pl.program_id(1)
    pad = 32
    hist = width - 1

    @pl.when((b == 0) & (t == 0))
    def _():
        wbf_ref[...] = wout_ref[...].astype(BF16)

    @pl.when(t == 0)
    def _():
        ext_ref[pl.ds(pad - hist, hist), :] = prev_ref[...]

    @pl.when(t > 0)
    def _():
        ext_ref[pl.ds(0, pad), :] = ext_ref[pl.ds(tt, pad), :]

    glu = glu_ref[...]
    u = glu[:, :dconv] * _sigmoid(glu[:, dconv:])
    ext_ref[pl.ds(pad, tt), :] = u
    st_ref[...] = ext_ref[pl.ds(pad + tt - hist, hist), :]

    span = tt + pad - 8
    for s in range(1, 8):
        sh_ref[s - 1] = ext_ref[pl.ds(s, span), :]

    half = tt // 2
    for r, hh_ref in enumerate((h_ref, h2_ref)):
        r0 = r * half
        for c in range(dconv // LANES):
            cs = slice(c * LANES, (c + 1) * LANES)
            acc = jnp.zeros((half, LANES), F32) + bdw_ref[:, cs]
            for j in range(width):
                a, s = divmod(pad - hist + j, 8)
                src = ext_ref if s == 0 else sh_ref.at[s - 1]
                acc = acc + dw_ref[pl.ds(j, 1), cs] * src[pl.ds(8 * a + r0, half), cs]
            hh_ref[:, cs] = acc
        h = hh_ref[...]
        mu = jnp.mean(h, axis=-1, keepdims=True)
        var = jnp.mean(jnp.square(h - mu), axis=-1, keepdims=True)
        y = (h - mu) * lax.rsqrt(var + EPS) * lng_ref[...] + lnb_ref[...]
        y = y * _sigmoid(y)
        o_ref[pl.ds(r0, half), :] = jnp.dot(y.astype(BF16), wbf_ref[...], preferred_element_type=F32)


def conv_branch_prompt(glu_pre, prev, dw, bdw, lng, lnb, wout, n_seq, seq, tt):
    width, dconv = dw.shape
    dm = wout.shape[1]
    nt = seq // tt
    kern = functools.partial(_conv_kernel, tt=tt, width=width, dconv=dconv)
    return pl.pallas_call(
        kern,
        grid=(n_seq, nt),
        in_specs=[pl.BlockSpec((tt, 2 * dconv), lambda b, t: (b * nt + t, 0)),
                  pl.BlockSpec((None, width - 1, dconv), lambda b, t: (b, 0, 0)),
                  pl.BlockSpec((width, dconv), lambda b, t: (0, 0)),
                  pl.BlockSpec((1, dconv), lambda b, t: (0, 0)),
                  pl.BlockSpec((1, dconv), lambda b, t: (0, 0)),
                  pl.BlockSpec((1, dconv), lambda b, t: (0, 0)),
                  pl.BlockSpec((dconv, dm), lambda b, t: (0, 0))],
        out_specs=[pl.BlockSpec((tt, dm), lambda b, t: (b * nt + t, 0)),
                   pl.BlockSpec((None, width - 1, dconv), lambda b, t: (b, 0, 0))],
        out_shape=[jax.ShapeDtypeStruct((n_seq * seq, dm), F32),
                   jax.ShapeDtypeStruct((n_seq, width - 1, dconv), F32)],
        scratch_shapes=[pltpu.VMEM((32 + tt, dconv), F32), pltpu.VMEM((tt // 2, dconv), F32),
                        pltpu.VMEM((dconv, dm), BF16), pltpu.VMEM((7, 24 + tt, dconv), F32),
                        pltpu.VMEM((tt // 2, dconv), F32)],
        compiler_params=_cparams(2),
        name="conv_branch",
    )(glu_pre, prev, dw, bdw.reshape(1, dconv), lng.reshape(1, dconv), lnb.reshape(1, dconv), wout)


def _conv_step_kernel(glu_ref, prev_ref, dw_ref, bdw_ref, lng_ref, lnb_ref, wout_ref, o_ref, u_ref,
                      *, width, dconv):
    glu = glu_ref[...]
    u = glu[:, :dconv] * _sigmoid(glu[:, dconv:])
    u_ref[...] = u
    h = bdw_ref[...] + dw_ref[pl.ds(width - 1, 1), :] * u
    for j in range(width - 1):
        h = h + dw_ref[pl.ds(j, 1), :] * prev_ref[j]
    mu = jnp.mean(h, axis=-1, keepdims=True)
    var = jnp.mean(jnp.square(h - mu), axis=-1, keepdims=True)
    y = (h - mu) * lax.rsqrt(var + EPS) * lng_ref[...] + lnb_ref[...]
    y = y * _sigmoid(y)
    o_ref[...] = jnp.dot(y.astype(BF16), wout_ref[...].astype(BF16), preferred_element_type=F32)


def conv_branch_step(glu_pre, prev_t, dw, bdw, lng, lnb, wout):
    width, dconv = dw.shape
    n = glu_pre.shape[0]
    dm = wout.shape[1]
    kern = functools.partial(_conv_step_kernel, width=width, dconv=dconv)
    return pl.pallas_call(
        kern,
        out_shape=[jax.ShapeDtypeStruct((n, dm), F32), jax.ShapeDtypeStruct((n, dconv), F32)],
        compiler_params=pltpu.CompilerParams(vmem_limit_bytes=VMEM_LIMIT),
    )(glu_pre, prev_t, dw, bdw.reshape(1, dconv), lng.reshape(1, dconv), lnb.reshape(1, dconv), wout)


def _select_threshold(count_gt, row_min, row_max, n_adm, topk, any_fn):
    kf = jnp.float32(topk)
    full = n_adm <= kf
    lo0 = row_min - (1.0 + jnp.abs(row_min))
    hi0 = row_max
    flo0 = jnp.where(full, kf, n_adm)
    fhi0 = jnp.zeros_like(lo0)

    def active_rows(lo, hi, flo):
        mid = 0.5 * lo + 0.5 * hi
        return (flo != kf) & (lo < mid) & (mid < hi)

    def cond(st):
        lo, hi, flo, fhi = st
        return any_fn(active_rows(lo, hi, flo))

    def step(st):
        lo, hi, flo, fhi = st
        act = active_rows(lo, hi, flo)
        mid = 0.5 * lo + 0.5 * hi
        c = count_gt(mid)
        up = act & (c >= kf)
        dn = act & (c < kf)
        return (jnp.where(up, mid, lo), jnp.where(dn, mid, hi),
                jnp.where(up, c, flo), jnp.where(dn, c, fhi))

    def body(st):
        for _ in range(5):
            st = step(st)
        return st

    lo, hi, flo, fhi = lax.while_loop(cond, body, (lo0, hi0, flo0, fhi0))
    lo = jnp.where(full, -jnp.inf, lo)
    return lo, hi, flo, fhi


def _attn_prompt_kernel(qi_ref, wi_ref, kw_ref, q_ref, k_ref, v_ref, co_ref, ga_ref, gb_ref, bias_ref,
                        o_ref, kd_ref, kb_ref, vt_ref, wit_ref, qib_ref, sc_ref, sel_ref, qs_ref, *state,
                        seq, n_heads, n_kv, idx_heads, idx_dim, topk, hd):
    m_refs, l_refs, acc_refs = state[:n_kv], state[n_kv:2 * n_kv], state[2 * n_kv:3 * n_kv]
    st_refs = state[3 * n_kv:]
    j = pl.program_id(1)
    group = n_heads // n_kv
    nck = (j * QB + QB + CK - 1) // CK
    kf = jnp.float32(topk)

    @pl.when(j == 0)
    def _():
        kd_ref[...] = kw_ref[:, :idx_dim].astype(BF16)
        for g in range(n_kv):
            kb_ref[g] = k_ref[pl.ds(g, seq, stride=n_kv), :].astype(BF16)
            for c in range(seq // CK):
                vt_ref[g, c] = v_ref[pl.ds(c * CK * n_kv + g, CK, stride=n_kv), :].T.astype(BF16)

    wit_ref[...] = (wi_ref[...] * (idx_heads ** -0.5)).T
    qi = qi_ref[...].astype(F32) * (idx_dim ** -0.5)
    for h in range(idx_heads):
        qib_ref[h] = qi[:, h * idx_dim:(h + 1) * idx_dim].astype(BF16)
    qpos = j * QB + lax.broadcasted_iota(jnp.int32, (CK, QB), 1)
    krow = lax.broadcasted_iota(jnp.int32, (CK, QB), 0)
    per_dot = 2 * LANES // QB

    def score_chunk(c, carry):
        mn, mx = carry
        k0 = pl.multiple_of(c * CK, CK)
        kc = kd_ref[pl.ds(k0, CK), :]
        acc = jnp.zeros((CK, QB), F32)
        for p in range(idx_heads // per_dot):
            rhs = qib_ref[pl.ds(p * per_dot, per_dot)].reshape(per_dot * QB, idx_dim)
            s = _dot_nt(kc, rhs)
            for r in range(per_dot):
                acc = acc + (wit_ref[pl.ds(idx_dim + p * per_dot + r, 1), :]
                             * jnp.maximum(s[:, r * QB:(r + 1) * QB], 0.0))
        adm = (krow + c * CK) <= qpos
        sc_ref[c] = jnp.where(adm, acc, -jnp.inf)
        mn = jnp.minimum(mn, _fold_rows(jnp.where(adm, acc, jnp.inf), jnp.minimum))
        mx = jnp.maximum(mx, _fold_rows(jnp.where(adm, acc, -jnp.inf), jnp.maximum))
        return mn, mx

    mn8, mx8 = lax.fori_loop(0, nck, score_chunk,
                             (jnp.full((8, QB), jnp.inf, F32), jnp.full((8, QB), -jnp.inf, F32)))
    row_min = jnp.min(mn8, axis=0, keepdims=True)
    row_max = jnp.max(mx8, axis=0, keepdims=True)
    n_adm = (j * QB + 1 + lax.broadcasted_iota(jnp.int32, (1, QB), 1)).astype(F32)

    def count_gt(t):
        def cbody(c, acc):
            return acc + _fold_rows(jnp.where(sc_ref[c] > t, 1.0, 0.0), jnp.add)
        part = lax.fori_loop(0, nck, cbody, jnp.zeros((8, QB), F32))
        return jnp.sum(part, axis=0, keepdims=True)

    def any_fn(mask):
        return jnp.max(jnp.where(mask, 1.0, 0.0)) > 0.0

    lo, hi, flo, fhi = _select_threshold(count_gt, row_min, row_max, n_adm, topk, any_fn)
    tie = flo != kf

    def sel_chunk(c, carry):
        sel_ref[c] = jnp.where(sc_ref[c] > lo, 1.0, 0.0)
        return carry
    lax.fori_loop(0, nck, sel_chunk, 0)

    @pl.when(any_fn(tie))
    def _():
        need = kf - fhi
        lower = (lax.broadcasted_iota(jnp.int32, (CK, CK), 1)
                 < lax.broadcasted_iota(jnp.int32, (CK, CK), 0)).astype(BF16)

        def tie_chunk(c, before):
            s = sc_ref[c]
            eq = s == hi
            eqf = jnp.where(eq, 1.0, 0.0)
            rank = before + jnp.dot(lower, eqf.astype(BF16), preferred_element_type=F32)
            keep = (s > hi) | (eq & (rank < need))
            sel_ref[c] = jnp.where(tie, jnp.where(keep, 1.0, 0.0), sel_ref[c])
            return before + jnp.sum(eqf, axis=0, keepdims=True)
        lax.fori_loop(0, nck, tie_chunk, jnp.zeros((1, QB), F32))

    scale2 = hd ** -0.5 * LOG2E
    for h in range(n_heads):
        qs_ref[h] = q_ref[:, h * hd:(h + 1) * hd].astype(BF16)
    for g in range(n_kv):
        m_refs[g][...] = jnp.full(m_refs[g].shape, NEG, F32)
        l_refs[g][...] = jnp.zeros(l_refs[g].shape, F32)
        acc_refs[g][...] = jnp.zeros(acc_refs[g].shape, F32)

    def logits(g, c):
        kc = kb_ref[g, pl.ds(pl.multiple_of(c * CK, CK), CK), :]
        qg = qs_ref[pl.ds(g * group, group)].reshape(group * QB, hd)
        st_refs[g % 2][...] = _dot_nt(kc, qg)

    logits(0, 0)

    def att_chunk(c, carry, near):
        msk = jnp.concatenate([sel_ref[c]] * group, axis=1) > 0.0
        tis = [jnp.clip(c * (CK // QB) + s - j + 2, 0, 2) for s in range(CK // QB)]
        for g in range(n_kv):
            if g + 1 < n_kv:
                logits(g + 1, c)
            else:
                logits(0, jnp.minimum(c + 1, nck - 1))
            lg = st_refs[g % 2][...] * scale2
            if near:
                rows = [jnp.concatenate([bias_ref[ti, g * group + hh] for hh in range(group)], axis=1)
                        for ti in tis]
                lg = lg + jnp.concatenate(rows, axis=0)
            lg = jnp.where(msk, lg, -jnp.inf)
            m_old = m_refs[g][...]
            m_new = jnp.maximum(m_old, jnp.max(_fold_rows(lg, jnp.maximum), axis=0, keepdims=True))
            alpha = jnp.exp2(m_old - m_new)
            p = jnp.exp2(lg - m_new)
            l_refs[g][...] = alpha * l_refs[g][...] + jnp.sum(_fold_rows(p, jnp.add), axis=0, keepdims=True)
            pv = jnp.dot(vt_ref[g, c], p.astype(BF16), preferred_element_type=F32)
            acc_refs[g][...] = alpha * acc_refs[g][...] + pv
            m_refs[g][...] = m_new
        return carry
    n_far = jnp.maximum((j * (QB // LANES) - 1) // (CK // QB), 0)
    lax.fori_loop(0, n_far, functools.partial(att_chunk, near=False), 0)
    lax.fori_loop(n_far, nck, functools.partial(att_chunk, near=True), 0)

    for g in range(n_kv):
        ot = acc_refs[g][...] / l_refs[g][...]
        for hh in range(group):
            cs = slice((g * group + hh) * hd, (g * group + hh + 1) * hd)
            o = ot[:, hh * QB:(hh + 1) * QB].T
            mixed = ga_ref[:, cs] * co_ref[:, cs] + gb_ref[:, cs] * o
            o_ref[:, cs] = mixed.astype(o_ref.dtype)


def attn_prompt(qi, kw, q, k, v, conv_out, gates, bias3, n_seq, seq, n_kv, idx_dim, idx_heads):
    m, dm = q.shape
    hd = k.shape[1]
    n_heads = dm // hd
    group = n_heads // n_kv
    nb = seq // QB
    topk = min(TOPK_MAX, seq // 4)
    assert seq % CK == 0 and LANES % idx_dim == 0 and kw.shape[1] == LANES
    kern = functools.partial(_attn_prompt_kernel, seq=seq, n_heads=n_heads, n_kv=n_kv, idx_heads=idx_heads,
                             idx_dim=idx_dim, topk=topk, hd=hd)
    row = lambda b, j: (b * nb + j, 0)
    return pl.pallas_call(
        kern,
        grid=(n_seq, nb),
        in_specs=[pl.BlockSpec((QB, idx_heads * idx_dim), row),
                  pl.BlockSpec((QB, LANES), row),
                  pl.BlockSpec((seq, LANES), lambda b, j: (b, 0)),
                  pl.BlockSpec((QB, dm), row),
                  pl.BlockSpec((seq * n_kv, hd), lambda b, j: (b, 0)),
                  pl.BlockSpec((seq * n_kv, hd), lambda b, j: (b, 0)),
                  pl.BlockSpec((QB, dm), row),
                  pl.BlockSpec((QB, dm), lambda b, j: (b * nb + j, 0)),
                  pl.BlockSpec((QB, dm), lambda b, j: (b * nb + j, 1)),
                  pl.BlockSpec((3, n_heads, QB, QB), lambda b, j: (0, 0, 0, 0))],
        out_specs=pl.BlockSpec((QB, dm), row),
        out_shape=jax.ShapeDtypeStruct((m, dm), BF16),
        scratch_shapes=[pltpu.VMEM((seq, idx_dim), BF16),
                        pltpu.VMEM((n_kv, seq, hd), BF16),
                        pltpu.VMEM((n_kv, seq // CK, hd, CK), BF16),
                        pltpu.VMEM((LANES, QB), F32),
                        pltpu.VMEM((idx_heads, QB, idx_dim), BF16),
                        pltpu.VMEM((seq // CK, CK, QB), F32),
                        pltpu.VMEM((seq // CK, CK, QB), F32),
                        pltpu.VMEM((n_heads, QB, hd), BF16)]
                       + [pltpu.VMEM((1, group * QB), F32)] * (2 * n_kv)
                       + [pltpu.VMEM((hd, group * QB), F32)] * n_kv
                       + [pltpu.VMEM((CK, group * QB), F32)] * 2,
        compiler_params=_cparams(2),
        name="attn_prompt",
    )(qi, kw, kw, q, k, v, conv_out, gates, gates, bias3)


def _sample_score_kernel(pt_ref, qi_ref, wi_ref, kn_ref, *rest, pg, idx_heads, idx_dim):
    k_hbm, o_ref, self_ref, kbuf_ref, sem, kcat_ref = rest
    page = k_hbm.shape[2]
    b = pl.program_id(0)
    p = pl.program_id(1)
    n_steps = pl.num_programs(1)
    step = b * n_steps + p
    slot = step % 2

    def page_copies(bb, pp, sl):
        return [pltpu.make_async_copy(k_hbm.at[pt_ref[bb, pp * pg + i]], kbuf_ref.at[sl, i], sem.at[sl])
                for i in range(pg)]

    @pl.when(step == 0)
    def _():
        for c in page_copies(0, 0, 0):
            c.start()

    nxt = step + 1

    @pl.when(nxt < pl.num_programs(0) * n_steps)
    def _():
        for c in page_copies(nxt // n_steps, nxt % n_steps, 1 - slot):
            c.start()

    for c in page_copies(b, p, slot):
        c.wait()

    qi = qi_ref[...] * (idx_dim ** -0.5)
    wi = wi_ref[...] * (idx_heads ** -0.5)
    qb = qi.astype(BF16)
    for i in range(pg):
        kcat_ref[:, i * page:(i + 1) * page] = kbuf_ref[slot, i].astype(BF16)
    s = jnp.dot(qb, kcat_ref[...], preferred_element_type=F32)
    o_ref[...] = jnp.sum(wi * jnp.maximum(s, 0.0), axis=0, keepdims=True)

    @pl.when(pl.program_id(1) == 0)
    def _():
        kn = kn_ref[...].astype(BF16).astype(F32)
        s = jnp.sum(qb.astype(F32) * kn, axis=1, keepdims=True)
        sself = jnp.sum(wi * jnp.maximum(s, 0.0), axis=0, keepdims=True)
        self_ref[...] = jnp.broadcast_to(sself, self_ref.shape)


def sample_scores(page_table, qi3, wi3, ki_new3, cache_kidx_t, pg):
    n, n_pages = page_table.shape
    idx_heads, idx_dim = qi3.shape[1:]
    page = cache_kidx_t.shape[2]
    kern = functools.partial(_sample_score_kernel, pg=pg, idx_heads=idx_heads, idx_dim=idx_dim)
    grid_spec = pltpu.PrefetchScalarGridSpec(
        num_scalar_prefetch=1,
        grid=(n, n_pages // pg),
        in_specs=[pl.BlockSpec((None, idx_heads, idx_dim), lambda b, p, pt: (b, 0, 0)),
                  pl.BlockSpec((None, idx_heads, 1), lambda b, p, pt: (b, 0, 0)),
                  pl.BlockSpec((None, 1, idx_dim), lambda b, p, pt: (b, 0, 0)),
                  pl.BlockSpec(memory_space=pl.ANY)],
        out_specs=[pl.BlockSpec((None, None, 1, pg * page), lambda b, p, pt: (b, p, 0, 0)),
                   pl.BlockSpec((None, 1, LANES), lambda b, p, pt: (b, 0, 0))],
        scratch_shapes=[pltpu.VMEM((2, pg, idx_dim, page), F32), pltpu.SemaphoreType.DMA((2,)),
                        pltpu.VMEM((idx_dim, pg * page), BF16)],
    )
    return pl.pallas_call(
        kern,
        grid_spec=grid_spec,
        out_shape=[jax.ShapeDtypeStruct((n, n_pages // pg, 1, pg * page), F32),
                   jax.ShapeDtypeStruct((n, 1, LANES), F32)],
        compiler_params=_cparams(2),
        name="sample_scores",
    )(page_table, qi3, wi3, ki_new3, cache_kidx_t)


def _sample_select_kernel(sc_ref, self_ref, sel4_ref, selself_ref, sel_ref, *, topk, past, rep):
    sc = sc_ref[...]
    sself = self_ref[:, 0:1]
    n = sc.shape[0]
    kf = jnp.float32(topk)
    row_min = jnp.minimum(jnp.min(sc, axis=1, keepdims=True), sself)
    row_max = jnp.maximum(jnp.max(sc, axis=1, keepdims=True), sself)
    n_adm = jnp.full((n, 1), past + 1, F32)

    def count_gt(t):
        return (jnp.sum(jnp.where(sc > t, 1.0, 0.0), axis=1, keepdims=True)
                + jnp.where(sself > t, 1.0, 0.0))

    def any_fn(mask):
        return jnp.max(jnp.where(mask, 1.0, 0.0)) > 0.0

    lo, hi, flo, fhi = _select_threshold(count_gt, row_min, row_max, n_adm, topk, any_fn)
    tie = flo != kf
    sel_ref[...] = jnp.where(sc > lo, 1.0, 0.0)
    selself_ref[...] = jnp.broadcast_to(jnp.where(sself > lo, 1.0, 0.0), selself_ref.shape)

    @pl.when(any_fn(tie))
    def _():
        need = kf - fhi
        blk = 512
        tri = (lax.broadcasted_iota(jnp.int32, (blk, blk), 0)
               < lax.broadcasted_iota(jnp.int32, (blk, blk), 1)).astype(BF16)
        before = jnp.zeros((n, 1), F32)
        for c in range(past // blk):
            s = sc_ref[:, c * blk:(c + 1) * blk]
            eq = s == hi
            rank = before + jnp.dot(jnp.where(eq, 1.0, 0.0).astype(BF16), tri, preferred_element_type=F32)
            keep = (s > hi) | (eq & (rank < need))
            sel_ref[:, c * blk:(c + 1) * blk] = jnp.where(tie, jnp.where(keep, 1.0, 0.0),
                                                          sel_ref[:, c * blk:(c + 1) * blk])
            before = before + jnp.sum(jnp.where(eq, 1.0, 0.0), axis=1, keepdims=True)
        keep_self = (sself > hi) | ((sself == hi) & (before < need))
        selself_ref[...] = jnp.broadcast_to(
            jnp.where(tie, jnp.where(keep_self, 1.0, 0.0), jnp.where(sself > lo, 1.0, 0.0)), selself_ref.shape)

    blk = 512
    row_lo = lax.broadcasted_iota(jnp.int32, (blk, blk * rep), 0) * rep
    col = lax.broadcasted_iota(jnp.int32, (blk, blk * rep), 1)
    spread = jnp.where((col >= row_lo) & (col < row_lo + rep), 1.0, 0.0).astype(BF16)
    for c in range(past // blk):
        sel4_ref[:, c * blk * rep:(c + 1) * blk * rep] = jnp.dot(
            sel_ref[:, c * blk:(c + 1) * blk].astype(BF16), spread, preferred_element_type=F32)


def sample_select(scores, sself, topk, rep):
    n, past = scores.shape
    kern = functools.partial(_sample_select_kernel, topk=topk, past=past, rep=rep)
    return pl.pallas_call(
        kern,
        out_shape=[jax.ShapeDtypeStruct((n, past * rep), F32), jax.ShapeDtypeStruct((n, LANES), F32)],
        scratch_shapes=[pltpu.VMEM((n, past), F32)],
        compiler_params=pltpu.CompilerParams(vmem_limit_bytes=VMEM_LIMIT),
        name="sample_select",
    )(scores, sself)


def _sample_attn_kernel(pt_ref, q_ref, kn_ref, vn_ref, sel_ref, selself_ref, bias_ref, rb0_ref, own_ref, *rest,
                        pg, n_heads, n_kv, hd):
    k_hbm, v_hbm, o_ref, kbuf_ref, vbuf_ref, sem, kcat_ref, vcat_ref, m_ref, l_ref, acc_ref = rest
    b = pl.program_id(0)
    p = pl.program_id(1)
    n_steps = pl.num_programs(1)
    scale = hd ** -0.5
    rows = k_hbm.shape[1]
    step = b * n_steps + p
    slot = step % 2

    def page_copies(bb, pp, sl):
        copies = []
        for i in range(pg):
            pid = pt_ref[bb, pp * pg + i]
            dst = pl.ds(i * rows, rows)
            copies.append(pltpu.make_async_copy(k_hbm.at[pid], kbuf_ref.at[sl, dst], sem.at[0, sl]))
            copies.append(pltpu.make_async_copy(v_hbm.at[pid], vbuf_ref.at[sl, dst], sem.at[1, sl]))
        return copies

    @pl.when(step == 0)
    def _():
        for c in page_copies(0, 0, 0):
            c.start()

    nxt = step + 1

    @pl.when(nxt < pl.num_programs(0) * n_steps)
    def _():
        for c in page_copies(nxt // n_steps, nxt % n_steps, 1 - slot):
            c.start()

    for c in page_copies(b, p, slot):
        c.wait()

    @pl.when(p == 0)
    def _():
        m_ref[...] = jnp.full(m_ref.shape, NEG, F32)
        l_ref[...] = jnp.zeros(l_ref.shape, F32)
        acc_ref[...] = jnp.zeros(acc_ref.shape, F32)

    for i in range(pg):
        kcat_ref[i * rows:(i + 1) * rows, :] = kbuf_ref[slot, pl.ds(i * rows, rows), :].astype(BF16)
        vcat_ref[i * rows:(i + 1) * rows, :] = vbuf_ref[slot, pl.ds(i * rows, rows), :].astype(BF16)
    qb = q_ref[...].astype(BF16)
    last = p == n_steps - 1
    bias = jnp.concatenate([bias_ref[0]] * (pg - 1) + [jnp.where(last, bias_ref[1], bias_ref[0])], axis=1)
    msk = (own_ref[...] > 0.0) & (sel_ref[...] > 0.0)
    lg = jnp.where(msk, _dot_nt(qb, kcat_ref[...]) * scale + bias, -jnp.inf)
    m_old = m_ref[...]
    m_new = jnp.maximum(m_old, jnp.max(lg, axis=-1, keepdims=True))
    alpha = jnp.exp(m_old - m_new)
    pr = jnp.exp(lg - m_new)
    l_new = alpha * l_ref[...] + jnp.sum(pr, axis=-1, keepdims=True)
    acc = alpha * acc_ref[...] + jnp.dot(pr.astype(BF16), vcat_ref[...], preferred_element_type=F32)
    m_ref[...] = m_new
    l_ref[...] = l_new
    acc_ref[...] = acc

    @pl.when(last)
    def _():
        kn = kn_ref[...].astype(BF16).astype(F32)
        vn = vn_ref[...].astype(BF16).astype(F32)
        ls = jnp.sum(qb.astype(F32) * kn, axis=-1, keepdims=True) * scale + rb0_ref[...]
        on = selself_ref[:, 0:1] > 0.0
        ls = jnp.where(on, ls, NEG)
        m_f = jnp.maximum(m_new, ls)
        a2 = jnp.exp(m_new - m_f)
        ps = jnp.where(on, jnp.exp(ls - m_f), 0.0)
        l_f = a2 * l_new + ps
        acc_f = a2 * acc + ps.astype(BF16).astype(F32) * vn
        o_ref[...] = acc_f / l_f


def sample_attention(page_table, q3, k_rep, v_rep, sel4, selself3, bias_s, rb0, cache_k, cache_v, pg, n_kv):
    n, n_pages = page_table.shape
    n_heads, hd = q3.shape[1:]
    rows = cache_k.shape[1]
    kern = functools.partial(_sample_attn_kernel, pg=pg, n_heads=n_heads, n_kv=n_kv, hd=hd)
    own = (np.arange(pg * rows)[None, :] % n_kv
           == np.arange(n_heads)[:, None] // (n_heads // n_kv)).astype(np.float32)
    hbm = pl.BlockSpec(memory_space=pl.ANY)
    per_seq = lambda b, p, pt: (b, 0, 0)
    grid_spec = pltpu.PrefetchScalarGridSpec(
        num_scalar_prefetch=1,
        grid=(n, n_pages // pg),
        in_specs=[pl.BlockSpec((None, n_heads, hd), per_seq),
                  pl.BlockSpec((None, n_heads, hd), per_seq),
                  pl.BlockSpec((None, n_heads, hd), per_seq),
                  pl.BlockSpec((None, None, 1, pg * rows), lambda b, p, pt: (b, p, 0, 0)),
                  pl.BlockSpec((None, 1, LANES), per_seq),
                  pl.BlockSpec((2, n_heads, rows), lambda b, p, pt: (0, 0, 0)),
                  pl.BlockSpec((n_heads, 1), lambda b, p, pt: (0, 0)),
                  pl.BlockSpec((n_heads, pg * rows), lambda b, p, pt: (0, 0)), hbm, hbm],
        out_specs=pl.BlockSpec((None, n_heads, hd), per_seq),
        scratch_shapes=[pltpu.VMEM((2, pg * rows, hd), F32), pltpu.VMEM((2, pg * rows, hd), F32),
                        pltpu.SemaphoreType.DMA((2, 2)),
                        pltpu.VMEM((pg * rows, hd), BF16), pltpu.VMEM((pg * rows, hd), BF16),
                        pltpu.VMEM((n_heads, 1), F32), pltpu.VMEM((n_heads, 1), F32),
                        pltpu.VMEM((n_heads, hd), F32)],
    )
    return pl.pallas_call(
        kern,
        grid_spec=grid_spec,
        out_shape=jax.ShapeDtypeStruct((n, n_heads, hd), F32),
        compiler_params=_cparams(2),
        name="sample_attn",
    )(page_table, q3, k_rep, v_rep, sel4, selself3, bias_s, rb0, jnp.asarray(own), cache_k, cache_v)


def _mix_kernel(ga_ref, gb_ref, co_ref, at_ref, o_ref):
    o_ref[...] = (ga_ref[...] * co_ref[...] + gb_ref[...] * at_ref[...]).astype(o_ref.dtype)


def gated_mix(gates, conv_out, attn):
    n, dm = conv_out.shape
    return pl.pallas_call(
        _mix_kernel,
        grid=(1,),
        in_specs=[pl.BlockSpec((n, dm), lambda i: (0, 0)), pl.BlockSpec((n, dm), lambda i: (0, 1)),
                  pl.BlockSpec((n, dm), lambda i: (0, 0)), pl.BlockSpec((n, dm), lambda i: (0, 0))],
        out_specs=pl.BlockSpec((n, dm), lambda i: (0, 0)),
        out_shape=jax.ShapeDtypeStruct((n, dm), BF16),
    )(gates, gates, conv_out, attn)


def _ffn_up_kernel(x_ref, xh_ref, wg_ref, wv_ref, dg_ref, dv_ref, bg_ref, bv_ref, pg_ref, pv_ref,
                   o_ref, sg_ref, sv_ref, eg_ref, ev_ref, wgb_ref, wvb_ref, *, tm, rs, width, tiles_per_seq):
    hist = width - 1
    pad = xh_ref.shape[0]
    first = pl.program_id(0) % tiles_per_seq == 0
    kc = 256
    for wf_ref, wb_ref in ((wg_ref, wgb_ref), (wv_ref, wvb_ref)):
        for k0 in range(0, wf_ref.shape[0], kc):
            wb_ref[k0:k0 + kc, :] = wf_ref[k0:k0 + kc, :].astype(BF16)
    branches = ((wgb_ref, dg_ref, bg_ref, pg_ref, eg_ref, sg_ref),
                (wvb_ref, dv_ref, bv_ref, pv_ref, ev_ref, sv_ref))
    for w_ref, _, _, prev_ref, e_ref, _ in branches:
        e_ref[0, pl.ds(0, pad), :] = jnp.dot(xh_ref[...], w_ref[...], preferred_element_type=F32)

        @pl.when(first)
        def _():
            e_ref[0, pl.ds(pad - hist, hist), :] = prev_ref[...]

    def project(r):
        for w_ref, _, _, _, e_ref, _ in branches:
            e_ref[r % 2, pl.ds(pad, rs), :] = jnp.dot(x_ref[pl.ds(r * rs, rs), :], w_ref[...],
                                                      preferred_element_type=F32)

    def hand_over(r):
        for _, _, _, _, e_ref, _ in branches:
            e_ref[(r + 1) % 2, pl.ds(0, pad), :] = e_ref[r % 2, pl.ds(rs, pad), :]

    def activate(r):
        outs = []
        for _, d_ref, b_ref, _, e_ref, _ in branches:
            acc = b_ref[...] + d_ref[pl.ds(hist, 1), :] * e_ref[r % 2, pl.ds(pad, rs), :]
            for j in range(hist):
                acc = acc + d_ref[pl.ds(j, 1), :] * e_ref[r % 2, pl.ds(pad - hist + j, rs), :]
            outs.append(acc)
        g, v = outs
        o_ref[pl.ds(r * rs, rs), :] = (g * _sigmoid(g) * v).astype(o_ref.dtype)

    n_sub = tm // rs
    project(0)
    hand_over(0)
    for r in range(1, n_sub):
        project(r)
        activate(r - 1)
        hand_over(r)
    activate(n_sub - 1)
    for _, _, _, _, e_ref, s_ref in branches:
        s_ref[...] = e_ref[(n_sub - 1) % 2, pl.ds(pad + rs - hist, hist), :]


def ffn_up_act(xn, w_up, prev, dw, bdw, n_seq, seq, tm, tn):
    m, k = xn.shape
    f2 = w_up.shape[1]
    f = f2 // 2
    width = dw.shape[0]
    ncb = f // tn
    pad = 16
    tps = seq // tm
    assert seq % tm == 0 and tm % pad == 0 and f % tn == 0 and width - 1 <= pad
    rs = min(128, tm // 2)
    assert tm % rs == 0 and rs % 8 == 0 and k % 256 == 0
    kern = functools.partial(_ffn_up_kernel, tm=tm, rs=rs, width=width, tiles_per_seq=tps)
    hb = tm // pad
    st_shape = jax.ShapeDtypeStruct((m // tm, width - 1, f), F32)
    act, sg, sv = pl.pallas_call(
        kern,
        grid=(m // tm, ncb),
        in_specs=[pl.BlockSpec((tm, k), lambda i, c: (i, 0)),
                  pl.BlockSpec((pad, k), lambda i, c: (jnp.maximum(i * hb - 1, 0), 0)),
                  pl.BlockSpec((k, tn), lambda i, c: (0, c)),
                  pl.BlockSpec((k, tn), lambda i, c: (0, c + ncb)),
                  pl.BlockSpec((width, tn), lambda i, c: (0, c)),
                  pl.BlockSpec((width, tn), lambda i, c: (0, c + ncb)),
                  pl.BlockSpec((1, tn), lambda i, c: (0, c)),
                  pl.BlockSpec((1, tn), lambda i, c: (0, c + ncb)),
                  pl.BlockSpec((None, width - 1, tn), lambda i, c: (i // tps, 0, c)),
                  pl.BlockSpec((None, width - 1, tn), lambda i, c: (i // tps, 0, c + ncb))],
        out_specs=[pl.BlockSpec((tm, tn), lambda i, c: (i, c)),
                   pl.BlockSpec((None, width - 1, tn), lambda i, c: (i, 0, c)),
                   pl.BlockSpec((None, width - 1, tn), lambda i, c: (i, 0, c))],
        out_shape=[jax.ShapeDtypeStruct((m, f), BF16), st_shape, st_shape],
        scratch_shapes=[pltpu.VMEM((2, pad + rs, tn), F32), pltpu.VMEM((2, pad + rs, tn), F32),
                        pltpu.VMEM((k, tn), BF16), pltpu.VMEM((k, tn), BF16)],
        compiler_params=_cparams(2),
        name="ffn_up_act",
    )(xn, xn, w_up, w_up, dw, dw, bdw.reshape(1, f2), bdw.reshape(1, f2), prev, prev)
    state = jnp.concatenate([sg, sv], axis=-1).reshape(n_seq, tps, width - 1, f2)[:, tps - 1]
    return act, state


def _ffn_act_step_kernel(ug_ref, uv_ref, pg_ref, pv_ref, wg_ref, wv_ref, bg_ref, bv_ref, o_ref, *, width):
    def conv(u_ref, prev_ref, w_ref, b_ref):
        acc = b_ref[...] + w_ref[pl.ds(width - 1, 1), :] * u_ref[...]
        for j in range(width - 1):
            acc = acc + w_ref[pl.ds(j, 1), :] * prev_ref[j]
        return acc
    g = conv(ug_ref, pg_ref, wg_ref, bg_ref)
    v = conv(uv_ref, pv_ref, wv_ref, bv_ref)
    o_ref[...] = (g * _sigmoid(g) * v).astype(o_ref.dtype)


def ffn_act_step(u, prev_t, dw, bdw, tc):
    n, f2 = u.shape
    f = f2 // 2
    width = dw.shape[0]
    ncb = f // tc
    kern = functools.partial(_ffn_act_step_kernel, width=width)
    return pl.pallas_call(
        kern,
        grid=(ncb,),
        in_specs=[pl.BlockSpec((n, tc), lambda c: (0, c)),
                  pl.BlockSpec((n, tc), lambda c: (0, c + ncb)),
                  pl.BlockSpec((width - 1, n, tc), lambda c: (0, 0, c)),
                  pl.BlockSpec((width - 1, n, tc), lambda c: (0, 0, c + ncb)),
                  pl.BlockSpec((width, tc), lambda c: (0, c)),
                  pl.BlockSpec((width, tc), lambda c: (0, c + ncb)),
                  pl.BlockSpec((1, tc), lambda c: (0, c)),
                  pl.BlockSpec((1, tc), lambda c: (0, c + ncb))],
        out_specs=pl.BlockSpec((n, tc), lambda c: (0, c)),
        out_shape=jax.ShapeDtypeStruct((n, f), BF16),
        compiler_params=_cparams(1),
    )(u, u, prev_t, prev_t, dw, dw, bdw.reshape(1, f2), bdw.reshape(1, f2))


def _row_tile(m, cap):
    tm = cap
    while m % tm:
        tm //= 2
    return tm if tm >= 16 else m


def _in_projection(xn, w_main, w_kw, w_g, sizes, tm, hd):
    glu_w, q_w, k_w, v_w, qi_w, ki_w, wi_w, ga_w, gb_w = sizes
    offs = np.concatenate([[0], np.cumsum(sizes)])
    tn = 512
    t = dict(w_transposed=True)
    wide = lambda w, o: 2 * tn if w % (2 * tn) == 0 and o % (2 * tn) == 0 else tn
    glu = matmul_w(xn, w_main, int(offs[0]), glu_w, tm, wide(glu_w, int(offs[0])), name="mm_glu", **t)
    q = matmul_w(xn, w_main, int(offs[1]), q_w, tm, wide(q_w, int(offs[1])), name="mm_q", out_dtype=BF16, **t)
    k = matmul_heads(xn, w_main, int(offs[2]), k_w // hd, hd, tm, "mm_k")
    v = matmul_heads(xn, w_main, int(offs[3]), v_w // hd, hd, tm, "mm_v")
    qi = matmul_w(xn, w_main, int(offs[4]), qi_w, tm, min(wide(qi_w, int(offs[4])), qi_w), name="mm_qi",
                  out_dtype=BF16, **t)
    kw = matmul_w(xn, w_kw, 0, LANES, tm, LANES, name="mm_kw", **t)
    gates = matmul_w(xn, w_g, 0, ga_w + gb_w, tm, wide(ga_w + gb_w, 0), name="mm_gates", sigmoid=True, **t)
    return glu, q, k, v, qi, kw, gates


def kernel(x_prompt, x_sample, cache_k, cache_v, cache_kidx, state_conv, state_ffn, page_table, rel_bias,
           norm_attn, w_in, dw_conv, b_dw_conv, ln_conv_g, ln_conv_b, w_conv_out, w_o, norm_ffn, w_up, dw_ffn,
           b_dw_ffn, w_down, norm_final):
    bsz, seq, dm = x_prompt.shape
    nd, dec_seq, _ = x_sample.shape
    depth, n_pool, page, n_kv, hd = cache_k.shape
    idx_dim = cache_kidx.shape[-1]
    n_pages = page_table.shape[1]
    past = n_pages * page
    width, dconv = dw_conv.shape[1:]
    fwidth = dw_ffn.shape[1]
    f = w_down.shape[1]
    n_heads = w_o.shape[1] // hd
    d_attn = n_heads * hd
    d_kv = n_kv * hd
    n_in = w_in.shape[2]
    idx_heads = (n_in - 2 * dconv - d_attn - 2 * d_kv - idx_dim - 2 * dm) // (idx_dim + 1)
    sizes = (2 * dconv, d_attn, d_kv, d_kv, idx_heads * idx_dim, idx_dim, idx_heads, dm, dm)
    assert sum(sizes) == n_in and depth == 1 and dec_seq == 1 and page == LANES and d_attn == dm

    mp = bsz * seq
    xp = x_prompt.reshape(mp, dm)
    xs = x_sample.reshape(nd, dm)
    bias3, bias_s = bias_tables(rel_bias, page, n_kv)
    tmp = _row_tile(mp, 2048)
    drop = lambda a: a.reshape(a.shape[1:])
    (norm_attn, w_in, dw_conv, b_dw_conv, ln_conv_g, ln_conv_b, w_conv_out, w_o, norm_ffn, w_up, dw_ffn,
     b_dw_ffn, w_down, state_conv, state_ffn) = map(drop, (
         norm_attn, w_in, dw_conv, b_dw_conv, ln_conv_g, ln_conv_b, w_conv_out, w_o, norm_ffn, w_up, dw_ffn,
         b_dw_ffn, w_down, state_conv, state_ffn))
    kidx_pool = jnp.swapaxes(cache_kidx.reshape(n_pool, page, idx_dim), 1, 2)
    k_pool = cache_k.reshape(n_pool, page * n_kv, hd)
    v_pool = cache_v.reshape(n_pool, page * n_kv, hd)

    n_aligned = sum(sizes[:5])
    n_small = idx_dim + idx_heads
    w_in_t = jnp.swapaxes(w_in, 0, 1)
    w_main = cast_rows_bf16(w_in_t, 0, n_aligned)
    w_kw = cast_rows_bf16(w_in_t, n_aligned, LANES, n_small)
    w_g = cast_rows_bf16(w_in_t, n_aligned + n_small, 2 * dm)
    w_o, w_down = cast_bf16(w_o), cast_bf16(w_down)

    xn = rmsnorm_rows(xp, norm_attn, BF16, 512)
    glu, q, k, v, qi, kw, gates = _in_projection(xn, w_main, w_kw, w_g, sizes, tmp, hd)
    conv0 = jnp.zeros((bsz, width - 1, dconv), F32)
    conv_out, conv_state_p = conv_branch_prompt(glu, conv0, dw_conv, b_dw_conv, ln_conv_g, ln_conv_b,
                                                w_conv_out, bsz, seq, 256)
    mixed = attn_prompt(qi, kw, q, k, v, conv_out, gates, bias3, bsz, seq, n_kv, idx_dim, idx_heads)
    x2, xn2 = matmul_res_norm(mixed, w_o, xp, norm_ffn, _row_tile(mp, 512))
    ffn0 = jnp.zeros((bsz, fwidth - 1, 2 * f), F32)
    act, ffn_state_p = ffn_up_act(xn2, w_up, ffn0, dw_ffn, b_dw_ffn, bsz, seq, _row_tile(seq, 2048), 512)
    tk_down = f // 4 if f % (4 * LANES) == 0 else f
    tk_p = 512 if f % 512 == 0 else tk_down
    y_prompt = matmul_res_norm_out(act, w_down, x2, norm_final, _row_tile(mp, 1024), tk_p).reshape(bsz, seq, dm)

    xns = rmsnorm_rows(xs, norm_attn, BF16, nd)
    glu_s, q_s, k_s, v_s, qi_s, kw_s, gates_s = _in_projection(xns, w_main, w_kw, w_g, sizes, nd, hd)
    ki_s = kw_s[:, :idx_dim]
    wi_s = kw_s[:, idx_dim:idx_dim + idx_heads]
    sc_prev_t = jnp.swapaxes(state_conv, 0, 1)
    conv_out_s, u_conv_s = conv_branch_step(glu_s, sc_prev_t, dw_conv, b_dw_conv, ln_conv_g, ln_conv_b,
                                            w_conv_out)
    conv_state_s = jnp.concatenate([state_conv[:, 1:], u_conv_s[:, None, :]], axis=1)

    pg = 16 if n_pages % 16 == 0 else 8
    pg_idx = 32 if n_pages % 32 == 0 else pg
    scores3, sself3 = sample_scores(page_table, qi_s.reshape(nd, idx_heads, idx_dim),
                                    wi_s.reshape(nd, idx_heads, 1), ki_s.reshape(nd, 1, idx_dim), kidx_pool,
                                    pg_idx)
    topk_s = min(TOPK_MAX, (past + dec_seq) // 4)
    sel4, selself = sample_select(scores3.reshape(nd, past), sself3.reshape(nd, LANES), topk_s, n_kv)
    group = n_heads // n_kv
    k_rep = jnp.repeat(k_s.reshape(nd, n_kv, hd), group, axis=1)
    v_rep = jnp.repeat(v_s.reshape(nd, n_kv, hd), group, axis=1)
    attn_s = sample_attention(page_table, q_s.reshape(nd, n_heads, hd), k_rep, v_rep,
                              sel4.reshape(nd, n_pages // pg, 1, pg * page * n_kv), selself.reshape(nd, 1, LANES),
                              bias_s,
                              rel_bias[0].reshape(n_heads, 1), k_pool, v_pool, pg, n_kv)
    mixed_s = gated_mix(gates_s, conv_out_s, attn_s.reshape(nd, dm))
    x2s, xn2s = matmul_res_norm(mixed_s, w_o, xs, norm_ffn, nd)
    u_s = matmul_w(xn2s, w_up, 0, 2 * f, nd, 512)
    sf_prev_t = jnp.swapaxes(state_ffn, 0, 1)
    act_s = ffn_act_step(u_s, sf_prev_t, dw_ffn, b_dw_ffn, 512)
    y_sample = matmul_res_norm_out(act_s, w_down, x2s, norm_final, nd, tk_down).reshape(nd, dec_seq, dm)
    ffn_state_s = jnp.concatenate([state_ffn[:, 1:], u_s[:, None, :]], axis=1)

    return (y_prompt, y_sample,
            k.reshape(1, bsz, seq, n_kv, hd), v.reshape(1, bsz, seq, n_kv, hd),
            kw.reshape(bsz, seq, LANES)[None, :, :, :idx_dim],
            conv_state_p[None], ffn_state_p[None],
            k_s.reshape(1, nd, dec_seq, n_kv, hd), v_s.reshape(1, nd, dec_seq, n_kv, hd),
            ki_s.reshape(1, nd, dec_seq, idx_dim),
            conv_state_s[None], ffn_state_s[None])
```

```python
import functools
import math

import numpy as np
import jax
import jax.numpy as jnp
from jax import lax
from jax.experimental import pallas as pl
from jax.experimental.pallas import tpu as pltpu

F32 = jnp.float32
BF16 = jnp.bfloat16

EPS = 1e-6
TOPK_MAX = 256
N_BUCKETS = 32
MAX_DISTANCE = 128
QB = 128
CK = 256
LANES = 128
NEG = -1e30
LOG2E = math.log2(math.e)
VMEM_LIMIT = 56 * 1024 * 1024


def _cparams(n_axes, vmem=VMEM_LIMIT):
    return pltpu.CompilerParams(dimension_semantics=("arbitrary",) * n_axes, vmem_limit_bytes=vmem)


def _dot_nt(a, b):
    return lax.dot_general(a, b, (((1,), (1,)), ((), ())), preferred_element_type=F32)


def _sigmoid(x):
    return 1.0 / (1.0 + jnp.exp(-x))


def _fold_rows(x, op):
    while x.shape[0] > 8:
        half = x.shape[0] // 2
        x = op(x[:half], x[half:])
    return x


def _rel_bucket_np(dist):
    n = np.maximum(dist, 0)
    max_exact = N_BUCKETS // 2
    nf = np.maximum(n, 1).astype(np.float32)
    large = max_exact + (np.log(nf / np.float32(max_exact)) / np.float32(math.log(MAX_DISTANCE / max_exact))
                         * np.float32(N_BUCKETS - max_exact)).astype(np.int32)
    large = np.minimum(large, N_BUCKETS - 1)
    return np.where(n < max_exact, n, large).astype(np.int32)


def _rms_kernel(x_ref, g_ref, o_ref):
    x = x_ref[...]
    y = x * lax.rsqrt(jnp.mean(x * x, axis=-1, keepdims=True) + EPS) * g_ref[...]
    o_ref[...] = y.astype(o_ref.dtype)


def rmsnorm_rows(x, g, out_dtype, tm):
    m, d = x.shape
    return pl.pallas_call(
        _rms_kernel,
        grid=(m // tm,),
        in_specs=[pl.BlockSpec((tm, d), lambda i: (i, 0)), pl.BlockSpec((1, d), lambda i: (0, 0))],
        out_specs=pl.BlockSpec((tm, d), lambda i: (i, 0)),
        out_shape=jax.ShapeDtypeStruct((m, d), out_dtype),
        compiler_params=_cparams(1),
        name="rmsnorm",
    )(x, g.reshape(1, d))


def _cast_kernel(w_ref, o_ref):
    o_ref[...] = w_ref[...].astype(o_ref.dtype)


def cast_bf16(w, ncols=None):
    k, n = w.shape
    ncols = n if ncols is None else ncols
    tk = 512 if k % 512 == 0 else k
    tn = 1024 if ncols % 1024 == 0 else (512 if ncols % 512 == 0 else ncols)
    return pl.pallas_call(
        _cast_kernel,
        grid=(k // tk, ncols // tn),
        in_specs=[pl.BlockSpec((tk, tn), lambda i, j: (i, j))],
        out_specs=pl.BlockSpec((tk, tn), lambda i, j: (i, j)),
        out_shape=jax.ShapeDtypeStruct((k, ncols), BF16),
        compiler_params=_cparams(2),
        name="cast_bf16",
    )(w)


def _cast_rows_kernel(w_ref, o_ref, *, valid):
    w = w_ref[...]
    if valid < w.shape[0]:
        w = jnp.where(lax.broadcasted_iota(jnp.int32, w.shape, 0) < valid, w, 0.0)
    o_ref[...] = w.astype(o_ref.dtype)


def cast_rows_bf16(wt, row0, nrows, nvalid=None):
    n, k = wt.shape
    tr = 512 if nrows % 512 == 0 else nrows
    nvalid = nrows if nvalid is None else nvalid
    assert row0 % 8 == 0 and (nvalid == nrows or tr == nrows) and row0 + nrows <= n
    return pl.pallas_call(
        functools.partial(_cast_rows_kernel, valid=nvalid),
        grid=(nrows // tr,),
        in_specs=[pl.BlockSpec((pl.Element(tr), pl.Element(k)), lambda i: (pl.multiple_of(row0 + i * tr, 8), 0))],
        out_specs=pl.BlockSpec((tr, k), lambda i: (i, 0)),
        out_shape=jax.ShapeDtypeStruct((nrows, k), BF16),
        compiler_params=_cparams(1),
        name="cast_rows_bf16",
    )(wt)


def _mm_kernel(a_ref, w_ref, o_ref):
    o_ref[...] = jnp.dot(a_ref[...], w_ref[...].astype(BF16), preferred_element_type=F32).astype(o_ref.dtype)


def _mm_nt_kernel(a_ref, wt_ref, o_ref, *, sigmoid):
    y = _dot_nt(a_ref[...], wt_ref[...])
    o_ref[...] = (_sigmoid(y) if sigmoid else y).astype(o_ref.dtype)


def _mm_nt_heads_kernel(a_ref, wt_ref, o_ref, *, n_kv):
    y = _dot_nt(a_ref[...], wt_ref[...])
    tm, hd = y.shape[0], y.shape[1] // n_kv
    for g in range(n_kv):
        o_ref[pl.ds(g, tm, stride=n_kv), :] = y[:, g * hd:(g + 1) * hd]


def matmul_heads(a, wt, row0, n_kv, hd, tm, name):
    m, k = a.shape
    width = n_kv * hd
    assert row0 % width == 0 and m % tm == 0
    return pl.pallas_call(
        functools.partial(_mm_nt_heads_kernel, n_kv=n_kv),
        grid=(m // tm,),
        in_specs=[pl.BlockSpec((tm, k), lambda i: (i, 0)), pl.BlockSpec((width, k), lambda i: (row0 // width, 0))],
        out_specs=pl.BlockSpec((tm * n_kv, hd), lambda i: (i, 0)),
        out_shape=jax.ShapeDtypeStruct((m * n_kv, hd), F32),
        compiler_params=_cparams(1),
        name=name,
    )(a, wt)


def matmul_w(a, w, col0, ncols, tm, tn, name="matmul", out_dtype=F32, w_transposed=False, sigmoid=False):
    m, k = a.shape
    assert col0 % tn == 0 and ncols % tn == 0 and m % tm == 0 and (w_transposed or not sigmoid)
    cb = col0 // tn
    if w_transposed:
        w_spec = pl.BlockSpec((tn, k), lambda i, j: (j + cb, 0))
    else:
        w_spec = pl.BlockSpec((k, tn), lambda i, j: (0, j + cb))
    return pl.pallas_call(
        functools.partial(_mm_nt_kernel, sigmoid=sigmoid) if w_transposed else _mm_kernel,
        grid=(m // tm, ncols // tn),
        in_specs=[pl.BlockSpec((tm, k), lambda i, j: (i, 0)), w_spec],
        out_specs=pl.BlockSpec((tm, tn), lambda i, j: (i, j)),
        out_shape=jax.ShapeDtypeStruct((m, ncols), out_dtype),
        compiler_params=_cparams(2),
        name=name,
    )(a, w)


def _rms(x, g):
    return x * lax.rsqrt(jnp.mean(x * x, axis=-1, keepdims=True) + EPS) * g


def _mm_res_norm_kernel(a_ref, w_ref, r_ref, g_ref, o_ref, on_ref):
    x = r_ref[...] + jnp.dot(a_ref[...], w_ref[...], preferred_element_type=F32)
    o_ref[...] = x
    on_ref[...] = _rms(x, g_ref[...]).astype(on_ref.dtype)


def matmul_res_norm(a, w, res, g, tm):
    m, k = a.shape
    n = w.shape[1]
    return pl.pallas_call(
        _mm_res_norm_kernel,
        grid=(m // tm,),
        in_specs=[pl.BlockSpec((tm, k), lambda i: (i, 0)), pl.BlockSpec((k, n), lambda i: (0, 0)),
                  pl.BlockSpec((tm, n), lambda i: (i, 0)), pl.BlockSpec((1, n), lambda i: (0, 0))],
        out_specs=[pl.BlockSpec((tm, n), lambda i: (i, 0)), pl.BlockSpec((tm, n), lambda i: (i, 0))],
        out_shape=[jax.ShapeDtypeStruct((m, n), F32), jax.ShapeDtypeStruct((m, n), BF16)],
        compiler_params=_cparams(1),
        name="mm_o_norm",
    )(a, w, res, g.reshape(1, n))


def _mm_ksplit_norm_kernel(a_ref, w_ref, r_ref, g_ref, o_ref, acc_ref):
    kk = pl.program_id(1)

    @pl.when(kk == 0)
    def _():
        acc_ref[...] = r_ref[...]
    acc_ref[...] += jnp.dot(a_ref[...], w_ref[...], preferred_element_type=F32)

    @pl.when(kk == pl.num_programs(1) - 1)
    def _():
        o_ref[...] = _rms(acc_ref[...], g_ref[...])


def matmul_res_norm_out(a, w, res, g, tm, tk):
    m, k = a.shape
    n = w.shape[1]
    assert k % tk == 0 and m % tm == 0
    return pl.pallas_call(
        _mm_ksplit_norm_kernel,
        grid=(m // tm, k // tk),
        in_specs=[pl.BlockSpec((tm, tk), lambda i, kk: (i, kk)), pl.BlockSpec((tk, n), lambda i, kk: (kk, 0)),
                  pl.BlockSpec((tm, n), lambda i, kk: (i, 0)), pl.BlockSpec((1, n), lambda i, kk: (0, 0))],
        out_specs=pl.BlockSpec((tm, n), lambda i, kk: (i, 0)),
        out_shape=jax.ShapeDtypeStruct((m, n), F32),
        scratch_shapes=[pltpu.VMEM((tm, n), F32)],
        compiler_params=_cparams(2),
        name="mm_down_norm",
    )(a, w, res, g.reshape(1, n))


def _bias_kernel(rb_ref, bk3_ref, bks_ref, o3_ref, os_ref, *, n_heads):
    def head(h, carry):
        far = rb_ref[N_BUCKETS - 1, h]
        for t in range(3):
            bk = bk3_ref[t]
            acc = jnp.zeros(bk.shape, F32)
            for b in range(N_BUCKETS):
                acc = jnp.where(bk == b, rb_ref[b, h], acc)
            o3_ref[t, h] = (acc - far) * LOG2E
        for t in range(2):
            bk = bks_ref[t]
            acc = jnp.zeros(bk.shape, F32)
            for b in range(N_BUCKETS):
                acc = jnp.where(bk == b, rb_ref[b, h], acc)
            os_ref[t, pl.ds(h, 1), :] = acc
        return carry
    lax.fori_loop(0, n_heads, head, 0)


def bias_tables(rel_bias, page, rep):
    n_heads = rel_bias.shape[1]
    cols = page * rep
    i = np.arange(QB)[None, :]
    k = np.arange(QB)[:, None]
    bk3 = np.stack([_rel_bucket_np(i - k + 2 * QB), _rel_bucket_np(i - k + QB), _rel_bucket_np(i - k)])
    assert (_rel_bucket_np(np.arange(QB + 1, 1 << 20)) == N_BUCKETS - 1).all()
    assert (bk3[0] == N_BUCKETS - 1).all()
    assert page >= QB
    bks = np.stack([np.full((1, cols), N_BUCKETS - 1, np.int32),
                    _rel_bucket_np(page - np.arange(cols) // rep)[None, :]])
    return pl.pallas_call(
        functools.partial(_bias_kernel, n_heads=n_heads),
        in_specs=[pl.BlockSpec(memory_space=pltpu.SMEM),
                  pl.BlockSpec(memory_space=pltpu.VMEM), pl.BlockSpec(memory_space=pltpu.VMEM)],
        out_specs=[pl.BlockSpec(memory_space=pltpu.VMEM), pl.BlockSpec(memory_space=pltpu.VMEM)],
        out_shape=[jax.ShapeDtypeStruct((3, n_heads, QB, QB), F32),
                   jax.ShapeDtypeStruct((2, n_heads, cols), F32)],
    )(rel_bias, jnp.asarray(bk3), jnp.asarray(bks))


def _conv_kernel(glu_ref, prev_ref, dw_ref, bdw_ref, lng_ref, lnb_ref, wout_ref, o_ref, st_ref,
                 ext_ref, h_ref, wbf_ref, sh_ref, h2_ref, *, tt, width, dconv):
    b = pl.program_id(0)
    t = pl.program_id(1)
    pad = 32
    hist = width - 1

    @pl.when((b == 0) & (t == 0))
    def _():
        wbf_ref[...] = wout_ref[...].astype(BF16)

    @pl.when(t == 0)
    def _():
        ext_ref[pl.ds(pad - hist, hist), :] = prev_ref[...]

    @pl.when(t > 0)
    def _():
        ext_ref[pl.ds(0, pad), :] = ext_ref[pl.ds(tt, pad), :]

    glu = glu_ref[...]
    u = glu[:, :dconv] * _sigmoid(glu[:, dconv:])
    ext_ref[pl.ds(pad, tt), :] = u
    st_ref[...] = ext_ref[pl.ds(pad + tt - hist, hist), :]

    span = tt + pad - 8
    for s in range(1, 8):
        sh_ref[s - 1] = ext_ref[pl.ds(s, span), :]

    half = tt // 2
    for r, hh_ref in enumerate((h_ref, h2_ref)):
        r0 = r * half
        for c in range(dconv // LANES):
            cs = slice(c * LANES, (c + 1) * LANES)
            acc = jnp.zeros((half, LANES), F32) + bdw_ref[:, cs]
            for j in range(width):
                a, s = divmod(pad - hist + j, 8)
                src = ext_ref if s == 0 else sh_ref.at[s - 1]
                acc = acc + dw_ref[pl.ds(j, 1), cs] * src[pl.ds(8 * a + r0, half), cs]
            hh_ref[:, cs] = acc
        h = hh_ref[...]
        mu = jnp.mean(h, axis=-1, keepdims=True)
        var = jnp.mean(jnp.square(h - mu), axis=-1, keepdims=True)
        y = (h - mu) * lax.rsqrt(var + EPS) * lng_ref[...] + lnb_ref[...]
        y = y * _sigmoid(y)
        o_ref[pl.ds(r0, half), :] = jnp.dot(y.astype(BF16), wbf_ref[...], preferred_element_type=F32)


def conv_branch_prompt(glu_pre, prev, dw, bdw, lng, lnb, wout, n_seq, seq, tt):
    width, dconv = dw.shape
    dm = wout.shape[1]
    nt = seq // tt
    kern = functools.partial(_conv_kernel, tt=tt, width=width, dconv=dconv)
    return pl.pallas_call(
        kern,
        grid=(n_seq, nt),
        in_specs=[pl.BlockSpec((tt, 2 * dconv), lambda b, t: (b * nt + t, 0)),
                  pl.BlockSpec((None, width - 1, dconv), lambda b, t: (b, 0, 0)),
                  pl.BlockSpec((width, dconv), lambda b, t: (0, 0)),
                  pl.BlockSpec((1, dconv), lambda b, t: (0, 0)),
                  pl.BlockSpec((1, dconv), lambda b, t: (0, 0)),
                  pl.BlockSpec((1, dconv), lambda b, t: (0, 0)),
                  pl.BlockSpec((dconv, dm), lambda b, t: (0, 0))],
        out_specs=[pl.BlockSpec((tt, dm), lambda b, t: (b * nt + t, 0)),
                   pl.BlockSpec((None, width - 1, dconv), lambda b, t: (b, 0, 0))],
        out_shape=[jax.ShapeDtypeStruct((n_seq * seq, dm), F32),
                   jax.ShapeDtypeStruct((n_seq, width - 1, dconv), F32)],
        scratch_shapes=[pltpu.VMEM((32 + tt, dconv), F32), pltpu.VMEM((tt // 2, dconv), F32),
                        pltpu.VMEM((dconv, dm), BF16), pltpu.VMEM((7, 24 + tt, dconv), F32),
                        pltpu.VMEM((tt // 2, dconv), F32)],
        compiler_params=_cparams(2),
        name="conv_branch",
    )(glu_pre, prev, dw, bdw.reshape(1, dconv), lng.reshape(1, dconv), lnb.reshape(1, dconv), wout)


def _conv_step_kernel(glu_ref, prev_ref, dw_ref, bdw_ref, lng_ref, lnb_ref, wout_ref, o_ref, u_ref,
                      *, width, dconv):
    glu = glu_ref[...]
    u = glu[:, :dconv] * _sigmoid(glu[:, dconv:])
    u_ref[...] = u
    h = bdw_ref[...] + dw_ref[pl.ds(width - 1, 1), :] * u
    for j in range(width - 1):
        h = h + dw_ref[pl.ds(j, 1), :] * prev_ref[j]
    mu = jnp.mean(h, axis=-1, keepdims=True)
    var = jnp.mean(jnp.square(h - mu), axis=-1, keepdims=True)
    y = (h - mu) * lax.rsqrt(var + EPS) * lng_ref[...] + lnb_ref[...]
    y = y * _sigmoid(y)
    o_ref[...] = jnp.dot(y.astype(BF16), wout_ref[...].astype(BF16), preferred_element_type=F32)


def conv_branch_step(glu_pre, prev_t, dw, bdw, lng, lnb, wout):
    width, dconv = dw.shape
    n = glu_pre.shape[0]
    dm = wout.shape[1]
    kern = functools.partial(_conv_step_kernel, width=width, dconv=dconv)
    return pl.pallas_call(
        kern,
        out_shape=[jax.ShapeDtypeStruct((n, dm), F32), jax.ShapeDtypeStruct((n, dconv), F32)],
        compiler_params=pltpu.CompilerParams(vmem_limit_bytes=VMEM_LIMIT),
    )(glu_pre, prev_t, dw, bdw.reshape(1, dconv), lng.reshape(1, dconv), lnb.reshape(1, dconv), wout)


def _select_threshold(count_gt, row_min, row_max, n_adm, topk, any_fn):
    kf = jnp.float32(topk)
    full = n_adm <= kf
    lo0 = row_min - (1.0 + jnp.abs(row_min))
    hi0 = row_max
    flo0 = jnp.where(full, kf, n_adm)
    fhi0 = jnp.zeros_like(lo0)

    def active_rows(lo, hi, flo):
        mid = 0.5 * lo + 0.5 * hi
        return (flo != kf) & (lo < mid) & (mid < hi)

    def cond(st):
        lo, hi, flo, fhi = st
        return any_fn(active_rows(lo, hi, flo))

    def step(st):
        lo, hi, flo, fhi = st
        act = active_rows(lo, hi, flo)
        mid = 0.5 * lo + 0.5 * hi
        c = count_gt(mid)
        up = act & (c >= kf)
        dn = act & (c < kf)
        return (jnp.where(up, mid, lo), jnp.where(dn, mid, hi),
                jnp.where(up, c, flo), jnp.where(dn, c, fhi))

    def body(st):
        for _ in range(5):
            st = step(st)
        return st

    lo, hi, flo, fhi = lax.while_loop(cond, body, (lo0, hi0, flo0, fhi0))
    lo = jnp.where(full, -jnp.inf, lo)
    return lo, hi, flo, fhi


def _attn_prompt_kernel(qi_ref, wi_ref, kw_ref, q_ref, k_ref, v_ref, co_ref, ga_ref, gb_ref, bias_ref,
                        o_ref, kd_ref, kb_ref, vt_ref, wit_ref, qib_ref, sc_ref, sel_ref, qs_ref, *state,
                        seq, n_heads, n_kv, idx_heads, idx_dim, topk, hd):
    m_refs, l_refs, acc_refs = state[:n_kv], state[n_kv:2 * n_kv], state[2 * n_kv:3 * n_kv]
    st_refs = state[3 * n_kv:]
    j = pl.program_id(1)
    group = n_heads // n_kv
    nck = (j * QB + QB + CK - 1) // CK
    kf = jnp.float32(topk)

    @pl.when(j == 0)
    def _():
        kd_ref[...] = kw_ref[:, :idx_dim].astype(BF16)
        for g in range(n_kv):
            kb_ref[g] = k_ref[pl.ds(g, seq, stride=n_kv), :].astype(BF16)
            for c in range(seq // CK):
                vt_ref[g, c] = v_ref[pl.ds(c * CK * n_kv + g, CK, stride=n_kv), :].T.astype(BF16)

    wit_ref[...] = (wi_ref[...] * (idx_heads ** -0.5)).T
    qi = qi_ref[...].astype(F32) * (idx_dim ** -0.5)
    for h in range(idx_heads):
        qib_ref[h] = qi[:, h * idx_dim:(h + 1) * idx_dim].astype(BF16)
    qpos = j * QB + lax.broadcasted_iota(jnp.int32, (CK, QB), 1)
    krow = lax.broadcasted_iota(jnp.int32, (CK, QB), 0)
    per_dot = 2 * LANES // QB

    def score_chunk(c, carry):
        mn, mx = carry
        k0 = pl.multiple_of(c * CK, CK)
        kc = kd_ref[pl.ds(k0, CK), :]
        acc = jnp.zeros((CK, QB), F32)
        for p in range(idx_heads // per_dot):
            rhs = qib_ref[pl.ds(p * per_dot, per_dot)].reshape(per_dot * QB, idx_dim)
            s = _dot_nt(kc, rhs)
            for r in range(per_dot):
                acc = acc + (wit_ref[pl.ds(idx_dim + p * per_dot + r, 1), :]
                             * jnp.maximum(s[:, r * QB:(r + 1) * QB], 0.0))
        adm = (krow + c * CK) <= qpos
        sc_ref[c] = jnp.where(adm, acc, -jnp.inf)
        mn = jnp.minimum(mn, _fold_rows(jnp.where(adm, acc, jnp.inf), jnp.minimum))
        mx = jnp.maximum(mx, _fold_rows(jnp.where(adm, acc, -jnp.inf), jnp.maximum))
        return mn, mx

    mn8, mx8 = lax.fori_loop(0, nck, score_chunk,
                             (jnp.full((8, QB), jnp.inf, F32), jnp.full((8, QB), -jnp.inf, F32)))
    row_min = jnp.min(mn8, axis=0, keepdims=True)
    row_max = jnp.max(mx8, axis=0, keepdims=True)
    n_adm = (j * QB + 1 + lax.broadcasted_iota(jnp.int32, (1, QB), 1)).astype(F32)

    def count_gt(t):
        def cbody(c, acc):
            return acc + _fold_rows(jnp.where(sc_ref[c] > t, 1.0, 0.0), jnp.add)
        part = lax.fori_loop(0, nck, cbody, jnp.zeros((8, QB), F32))
        return jnp.sum(part, axis=0, keepdims=True)

    def any_fn(mask):
        return jnp.max(jnp.where(mask, 1.0, 0.0)) > 0.0

    lo, hi, flo, fhi = _select_threshold(count_gt, row_min, row_max, n_adm, topk, any_fn)
    tie = flo != kf

    def sel_chunk(c, carry):
        sel_ref[c] = jnp.where(sc_ref[c] > lo, 1.0, 0.0)
        return carry
    lax.fori_loop(0, nck, sel_chunk, 0)

    @pl.when(any_fn(tie))
    def _():
        need = kf - fhi
        lower = (lax.broadcasted_iota(jnp.int32, (CK, CK), 1)
                 < lax.broadcasted_iota(jnp.int32, (CK, CK), 0)).astype(BF16)

        def tie_chunk(c, before):
            s = sc_ref[c]
            eq = s == hi
            eqf = jnp.where(eq, 1.0, 0.0)
            rank = before + jnp.dot(lower, eqf.astype(BF16), preferred_element_type=F32)
            keep = (s > hi) | (eq & (rank < need))
            sel_ref[c] = jnp.where(tie, jnp.where(keep, 1.0, 0.0), sel_ref[c])
            return before + jnp.sum(eqf, axis=0, keepdims=True)
        lax.fori_loop(0, nck, tie_chunk, jnp.zeros((1, QB), F32))

    scale2 = hd ** -0.5 * LOG2E
    for h in range(n_heads):
        qs_ref[h] = q_ref[:, h * hd:(h + 1) * hd].astype(BF16)
    for g in range(n_kv):
        m_refs[g][...] = jnp.full(m_refs[g].shape, NEG, F32)
        l_refs[g][...] = jnp.zeros(l_refs[g].shape, F32)
        acc_refs[g][...] = jnp.zeros(acc_refs[g].shape, F32)

    def logits(g, c):
        kc = kb_ref[g, pl.ds(pl.multiple_of(c * CK, CK), CK), :]
        qg = qs_ref[pl.ds(g * group, group)].reshape(group * QB, hd)
        st_refs[g % 2][...] = _dot_nt(kc, qg)

    logits(0, 0)

    def att_chunk(c, carry, near):
        msk = jnp.concatenate([sel_ref[c]] * group, axis=1) > 0.0
        tis = [jnp.clip(c * (CK // QB) + s - j + 2, 0, 2) for s in range(CK // QB)]
        for g in range(n_kv):
            if g + 1 < n_kv:
                logits(g + 1, c)
            else:
                logits(0, jnp.minimum(c + 1, nck - 1))
            lg = st_refs[g % 2][...] * scale2
            if near:
                rows = [jnp.concatenate([bias_ref[ti, g * group + hh] for hh in range(group)], axis=1)
                        for ti in tis]
                lg = lg + jnp.concatenate(rows, axis=0)
            lg = jnp.where(msk, lg, -jnp.inf)
            m_old = m_refs[g][...]
            m_new = jnp.maximum(m_old, jnp.max(_fold_rows(lg, jnp.maximum), axis=0, keepdims=True))
            alpha = jnp.exp2(m_old - m_new)
            p = jnp.exp2(lg - m_new)
            l_refs[g][...] = alpha * l_refs[g][...] + jnp.sum(_fold_rows(p, jnp.add), axis=0, keepdims=True)
            pv = jnp.dot(vt_ref[g, c], p.astype(BF16), preferred_element_type=F32)
            acc_refs[g][...] = alpha * acc_refs[g][...] + pv
            m_refs[g][...] = m_new
        return carry
    n_far = jnp.maximum((j * (QB // LANES) - 1) // (CK // QB), 0)
    lax.fori_loop(0, n_far, functools.partial(att_chunk, near=False), 0)
    lax.fori_loop(n_far, nck, functools.partial(att_chunk, near=True), 0)

    for g in range(n_kv):
        ot = acc_refs[g][...] / l_refs[g][...]
        for hh in range(group):
            cs = slice((g * group + hh) * hd, (g * group + hh + 1) * hd)
            o = ot[:, hh * QB:(hh + 1) * QB].T
            mixed = ga_ref[:, cs] * co_ref[:, cs] + gb_ref[:, cs] * o
            o_ref[:, cs] = mixed.astype(o_ref.dtype)


def attn_prompt(qi, kw, q, k, v, conv_out, gates, bias3, n_seq, seq, n_kv, idx_dim, idx_heads):
    m, dm = q.shape
    hd = k.shape[1]
    n_heads = dm // hd
    group = n_heads // n_kv
    nb = seq // QB
    topk = min(TOPK_MAX, seq // 4)
    assert seq % CK == 0 and LANES % idx_dim == 0 and kw.shape[1] == LANES
    kern = functools.partial(_attn_prompt_kernel, seq=seq, n_heads=n_heads, n_kv=n_kv, idx_heads=idx_heads,
                             idx_dim=idx_dim, topk=topk, hd=hd)
    row = lambda b, j: (b * nb + j, 0)
    return pl.pallas_call(
        kern,
        grid=(n_seq, nb),
        in_specs=[pl.BlockSpec((QB, idx_heads * idx_dim), row),
                  pl.BlockSpec((QB, LANES), row),
                  pl.BlockSpec((seq, LANES), lambda b, j: (b, 0)),
                  pl.BlockSpec((QB, dm), row),
                  pl.BlockSpec((seq * n_kv, hd), lambda b, j: (b, 0)),
                  pl.BlockSpec((seq * n_kv, hd), lambda b, j: (b, 0)),
                  pl.BlockSpec((QB, dm), row),
                  pl.BlockSpec((QB, dm), lambda b, j: (b * nb + j, 0)),
                  pl.BlockSpec((QB, dm), lambda b, j: (b * nb + j, 1)),
                  pl.BlockSpec((3, n_heads, QB, QB), lambda b, j: (0, 0, 0, 0))],
        out_specs=pl.BlockSpec((QB, dm), row),
        out_shape=jax.ShapeDtypeStruct((m, dm), BF16),
        scratch_shapes=[pltpu.VMEM((seq, idx_dim), BF16),
                        pltpu.VMEM((n_kv, seq, hd), BF16),
                        pltpu.VMEM((n_kv, seq // CK, hd, CK), BF16),
                        pltpu.VMEM((LANES, QB), F32),
                        pltpu.VMEM((idx_heads, QB, idx_dim), BF16),
                        pltpu.VMEM((seq // CK, CK, QB), F32),
                        pltpu.VMEM((seq // CK, CK, QB), F32),
                        pltpu.VMEM((n_heads, QB, hd), BF16)]
                       + [pltpu.VMEM((1, group * QB), F32)] * (2 * n_kv)
                       + [pltpu.VMEM((hd, group * QB), F32)] * n_kv
                       + [pltpu.VMEM((CK, group * QB), F32)] * 2,
        compiler_params=_cparams(2),
        name="attn_prompt",
    )(qi, kw, kw, q, k, v, conv_out, gates, gates, bias3)


def _sample_score_kernel(pt_ref, qi_ref, wi_ref, kn_ref, *rest, pg, idx_heads, idx_dim):
    k_hbm, o_ref, self_ref, kbuf_ref, sem, kcat_ref = rest
    page = k_hbm.shape[2]
    b = pl.program_id(0)
    p = pl.program_id(1)
    n_steps = pl.num_programs(1)
    step = b * n_steps + p
    slot = step % 2

    def page_copies(bb, pp, sl):
        return [pltpu.make_async_copy(k_hbm.at[pt_ref[bb, pp * pg + i]], kbuf_ref.at[sl, i], sem.at[sl])
                for i in range(pg)]

    @pl.when(step == 0)
    def _():
        for i, c in enumerate(page_copies(0, 0, 0)):
            c.start(priority=i % 2)

    nxt = step + 1

    @pl.when(nxt < pl.num_programs(0) * n_steps)
    def _():
        for i, c in enumerate(page_copies(nxt // n_steps, nxt % n_steps, 1 - slot)):
            c.start(priority=i % 2)

    for c in page_copies(b, p, slot):
        c.wait()

    qi = qi_ref[...] * (idx_dim ** -0.5)
    wi = wi_ref[...] * (idx_heads ** -0.5)
    qb = qi.astype(BF16)
    for i in range(pg):
        kcat_ref[:, i * page:(i + 1) * page] = kbuf_ref[slot, i].astype(BF16)
    s = jnp.dot(qb, kcat_ref[...], preferred_element_type=F32)
    o_ref[...] = jnp.sum(wi * jnp.maximum(s, 0.0), axis=0, keepdims=True)

    @pl.when(pl.program_id(1) == 0)
    def _():
        kn = kn_ref[...].astype(BF16).astype(F32)
        s = jnp.sum(qb.astype(F32) * kn, axis=1, keepdims=True)
        sself = jnp.sum(wi * jnp.maximum(s, 0.0), axis=0, keepdims=True)
        self_ref[...] = jnp.broadcast_to(sself, self_ref.shape)


def sample_scores(page_table, qi3, wi3, ki_new3, cache_kidx_t, pg):
    n, n_pages = page_table.shape
    idx_heads, idx_dim = qi3.shape[1:]
    page = cache_kidx_t.shape[2]
    kern = functools.partial(_sample_score_kernel, pg=pg, idx_heads=idx_heads, idx_dim=idx_dim)
    grid_spec = pltpu.PrefetchScalarGridSpec(
        num_scalar_prefetch=1,
        grid=(n, n_pages // pg),
        in_specs=[pl.BlockSpec((None, idx_heads, idx_dim), lambda b, p, pt: (b, 0, 0)),
                  pl.BlockSpec((None, idx_heads, 1), lambda b, p, pt: (b, 0, 0)),
                  pl.BlockSpec((None, 1, idx_dim), lambda b, p, pt: (b, 0, 0)),
                  pl.BlockSpec(memory_space=pl.ANY)],
        out_specs=[pl.BlockSpec((None, None, 1, pg * page), lambda b, p, pt: (b, p, 0, 0)),
                   pl.BlockSpec((None, 1, LANES), lambda b, p, pt: (b, 0, 0))],
        scratch_shapes=[pltpu.VMEM((2, pg, idx_dim, page), F32), pltpu.SemaphoreType.DMA((2,)),
                        pltpu.VMEM((idx_dim, pg * page), BF16)],
    )
    return pl.pallas_call(
        kern,
        grid_spec=grid_spec,
        out_shape=[jax.ShapeDtypeStruct((n, n_pages // pg, 1, pg * page), F32),
                   jax.ShapeDtypeStruct((n, 1, LANES), F32)],
        compiler_params=_cparams(2),
        name="sample_scores",
    )(page_table, qi3, wi3, ki_new3, cache_kidx_t)


def _sample_select_kernel(sc_ref, self_ref, sel4_ref, selself_ref, sel_ref, *, topk, past, rep):
    sc = sc_ref[...]
    sself = self_ref[:, 0:1]
    n = sc.shape[0]
    kf = jnp.float32(topk)
    row_min = jnp.minimum(jnp.min(sc, axis=1, keepdims=True), sself)
    row_max = jnp.maximum(jnp.max(sc, axis=1, keepdims=True), sself)
    n_adm = jnp.full((n, 1), past + 1, F32)

    def count_gt(t):
        return (jnp.sum(jnp.where(sc > t, 1.0, 0.0), axis=1, keepdims=True)
                + jnp.where(sself > t, 1.0, 0.0))

    def any_fn(mask):
        return jnp.max(jnp.where(mask, 1.0, 0.0)) > 0.0

    lo, hi, flo, fhi = _select_threshold(count_gt, row_min, row_max, n_adm, topk, any_fn)
    tie = flo != kf
    sel_ref[...] = jnp.where(sc > lo, 1.0, 0.0)
    selself_ref[...] = jnp.broadcast_to(jnp.where(sself > lo, 1.0, 0.0), selself_ref.shape)

    @pl.when(any_fn(tie))
    def _():
        need = kf - fhi
        blk = 512
        tri = (lax.broadcasted_iota(jnp.int32, (blk, blk), 0)
               < lax.broadcasted_iota(jnp.int32, (blk, blk), 1)).astype(BF16)
        before = jnp.zeros((n, 1), F32)
        for c in range(past // blk):
            s = sc_ref[:, c * blk:(c + 1) * blk]
            eq = s == hi
            rank = before + jnp.dot(jnp.where(eq, 1.0, 0.0).astype(BF16), tri, preferred_element_type=F32)
            keep = (s > hi) | (eq & (rank < need))
            sel_ref[:, c * blk:(c + 1) * blk] = jnp.where(tie, jnp.where(keep, 1.0, 0.0),
                                                          sel_ref[:, c * blk:(c + 1) * blk])
            before = before + jnp.sum(jnp.where(eq, 1.0, 0.0), axis=1, keepdims=True)
        keep_self = (sself > hi) | ((sself == hi) & (before < need))
        selself_ref[...] = jnp.broadcast_to(
            jnp.where(tie, jnp.where(keep_self, 1.0, 0.0), jnp.where(sself > lo, 1.0, 0.0)), selself_ref.shape)

    blk = 512
    row_lo = lax.broadcasted_iota(jnp.int32, (blk, blk * rep), 0) * rep
    col = lax.broadcasted_iota(jnp.int32, (blk, blk * rep), 1)
    spread = jnp.where((col >= row_lo) & (col < row_lo + rep), 1.0, 0.0).astype(BF16)
    for c in range(past // blk):
        sel4_ref[:, c * blk * rep:(c + 1) * blk * rep] = jnp.dot(
            sel_ref[:, c * blk:(c + 1) * blk].astype(BF16), spread, preferred_element_type=F32)


def sample_select(scores, sself, topk, rep):
    n, past = scores.shape
    kern = functools.partial(_sample_select_kernel, topk=topk, past=past, rep=rep)
    return pl.pallas_call(
        kern,
        out_shape=[jax.ShapeDtypeStruct((n, past * rep), F32), jax.ShapeDtypeStruct((n, LANES), F32)],
        scratch_shapes=[pltpu.VMEM((n, past), F32)],
        compiler_params=pltpu.CompilerParams(vmem_limit_bytes=VMEM_LIMIT),
        name="sample_select",
    )(scores, sself)


def _sample_attn_kernel(pt_ref, q_ref, kn_ref, vn_ref, sel_ref, selself_ref, bias_ref, rb0_ref, own_ref, *rest,
                        pg, n_heads, n_kv, hd):
    k_hbm, v_hbm, o_ref, kbuf_ref, vbuf_ref, sem, kcat_ref, vcat_ref, m_ref, l_ref, acc_ref = rest
    b = pl.program_id(0)
    p = pl.program_id(1)
    n_steps = pl.num_programs(1)
    scale = hd ** -0.5
    rows = k_hbm.shape[1]
    step = b * n_steps + p
    slot = step % 2

    def page_copies(bb, pp, sl):
        copies = []
        for i in range(pg):
            pid = pt_ref[bb, pp * pg + i]
            dst = pl.ds(i * rows, rows)
            copies.append(pltpu.make_async_copy(k_hbm.at[pid], kbuf_ref.at[sl, dst], sem.at[0, sl]))
            copies.append(pltpu.make_async_copy(v_hbm.at[pid], vbuf_ref.at[sl, dst], sem.at[1, sl]))
        return copies

    @pl.when(step == 0)
    def _():
        for c in page_copies(0, 0, 0):
            c.start()

    nxt = step + 1

    @pl.when(nxt < pl.num_programs(0) * n_steps)
    def _():
        for c in page_copies(nxt // n_steps, nxt % n_steps, 1 - slot):
            c.start()

    for c in page_copies(b, p, slot):
        c.wait()

    @pl.when(p == 0)
    def _():
        m_ref[...] = jnp.full(m_ref.shape, NEG, F32)
        l_ref[...] = jnp.zeros(l_ref.shape, F32)
        acc_ref[...] = jnp.zeros(acc_ref.shape, F32)

    for i in range(pg):
        kcat_ref[i * rows:(i + 1) * rows, :] = kbuf_ref[slot, pl.ds(i * rows, rows), :].astype(BF16)
        vcat_ref[i * rows:(i + 1) * rows, :] = vbuf_ref[slot, pl.ds(i * rows, rows), :].astype(BF16)
    qb = q_ref[...].astype(BF16)
    last = p == n_steps - 1
    bias = jnp.concatenate([bias_ref[0]] * (pg - 1) + [jnp.where(last, bias_ref[1], bias_ref[0])], axis=1)
    msk = (own_ref[...] > 0.0) & (sel_ref[...] > 0.0)
    lg = jnp.where(msk, _dot_nt(qb, kcat_ref[...]) * scale + bias, -jnp.inf)
    m_old = m_ref[...]
    m_new = jnp.maximum(m_old, jnp.max(lg, axis=-1, keepdims=True))
    alpha = jnp.exp(m_old - m_new)
    pr = jnp.exp(lg - m_new)
    l_new = alpha * l_ref[...] + jnp.sum(pr, axis=-1, keepdims=True)
    acc = alpha * acc_ref[...] + jnp.dot(pr.astype(BF16), vcat_ref[...], preferred_element_type=F32)
    m_ref[...] = m_new
    l_ref[...] = l_new
    acc_ref[...] = acc

    @pl.when(last)
    def _():
        kn = kn_ref[...].astype(BF16).astype(F32)
        vn = vn_ref[...].astype(BF16).astype(F32)
        ls = jnp.sum(qb.astype(F32) * kn, axis=-1, keepdims=True) * scale + rb0_ref[...]
        on = selself_ref[:, 0:1] > 0.0
        ls = jnp.where(on, ls, NEG)
        m_f = jnp.maximum(m_new, ls)
        a2 = jnp.exp(m_new - m_f)
        ps = jnp.where(on, jnp.exp(ls - m_f), 0.0)
        l_f = a2 * l_new + ps
        acc_f = a2 * acc + ps.astype(BF16).astype(F32) * vn
        o_ref[...] = acc_f / l_f


def sample_attention(page_table, q3, k_rep, v_rep, sel4, selself3, bias_s, rb0, cache_k, cache_v, pg, n_kv):
    n, n_pages = page_table.shape
    n_heads, hd = q3.shape[1:]
    rows = cache_k.shape[1]
    kern = functools.partial(_sample_attn_kernel, pg=pg, n_heads=n_heads, n_kv=n_kv, hd=hd)
    own = (np.arange(pg * rows)[None, :] % n_kv
           == np.arange(n_heads)[:, None] // (n_heads // n_kv)).astype(np.float32)
    hbm = pl.BlockSpec(memory_space=pl.ANY)
    per_seq = lambda b, p, pt: (b, 0, 0)
    grid_spec = pltpu.PrefetchScalarGridSpec(
        num_scalar_prefetch=1,
        grid=(n, n_pages // pg),
        in_specs=[pl.BlockSpec((None, n_heads, hd), per_seq),
                  pl.BlockSpec((None, n_heads, hd), per_seq),
                  pl.BlockSpec((None, n_heads, hd), per_seq),
                  pl.BlockSpec((None, None, 1, pg * rows), lambda b, p, pt: (b, p, 0, 0)),
                  pl.BlockSpec((None, 1, LANES), per_seq),
                  pl.BlockSpec((2, n_heads, rows), lambda b, p, pt: (0, 0, 0)),
                  pl.BlockSpec((n_heads, 1), lambda b, p, pt: (0, 0)),
                  pl.BlockSpec((n_heads, pg * rows), lambda b, p, pt: (0, 0)), hbm, hbm],
        out_specs=pl.BlockSpec((None, n_heads, hd), per_seq),
        scratch_shapes=[pltpu.VMEM((2, pg * rows, hd), F32), pltpu.VMEM((2, pg * rows, hd), F32),
                        pltpu.SemaphoreType.DMA((2, 2)),
                        pltpu.VMEM((pg * rows, hd), BF16), pltpu.VMEM((pg * rows, hd), BF16),
                        pltpu.VMEM((n_heads, 1), F32), pltpu.VMEM((n_heads, 1), F32),
                        pltpu.VMEM((n_heads, hd), F32)],
    )
    return pl.pallas_call(
        kern,
        grid_spec=grid_spec,
        out_shape=jax.ShapeDtypeStruct((n, n_heads, hd), F32),
        compiler_params=_cparams(2),
        name="sample_attn",
    )(page_table, q3, k_rep, v_rep, sel4, selself3, bias_s, rb0, jnp.asarray(own), cache_k, cache_v)


def _mix_kernel(ga_ref, gb_ref, co_ref, at_ref, o_ref):
    o_ref[...] = (ga_ref[...] * co_ref[...] + gb_ref[...] * at_ref[...]).astype(o_ref.dtype)


def gated_mix(gates, conv_out, attn):
    n, dm = conv_out.shape
    return pl.pallas_call(
        _mix_kernel,
        grid=(1,),
        in_specs=[pl.BlockSpec((n, dm), lambda i: (0, 0)), pl.BlockSpec((n, dm), lambda i: (0, 1)),
                  pl.BlockSpec((n, dm), lambda i: (0, 0)), pl.BlockSpec((n, dm), lambda i: (0, 0))],
        out_specs=pl.BlockSpec((n, dm), lambda i: (0, 0)),
        out_shape=jax.ShapeDtypeStruct((n, dm), BF16),
    )(gates, gates, conv_out, attn)


def _ffn_up_kernel(x_ref, xh_ref, wg_ref, wv_ref, dg_ref, dv_ref, bg_ref, bv_ref, pg_ref, pv_ref,
                   o_ref, sg_ref, sv_ref, eg_ref, ev_ref, wgb_ref, wvb_ref, *, tm, rs, width, tiles_per_seq):
    hist = width - 1
    pad = xh_ref.shape[0]
    first = pl.program_id(0) % tiles_per_seq == 0
    kc = 256
    for wf_ref, wb_ref in ((wg_ref, wgb_ref), (wv_ref, wvb_ref)):
        for k0 in range(0, wf_ref.shape[0], kc):
            wb_ref[k0:k0 + kc, :] = wf_ref[k0:k0 + kc, :].astype(BF16)
    branches = ((wgb_ref, dg_ref, bg_ref, pg_ref, eg_ref, sg_ref),
                (wvb_ref, dv_ref, bv_ref, pv_ref, ev_ref, sv_ref))
    for w_ref, _, _, prev_ref, e_ref, _ in branches:
        e_ref[0, pl.ds(0, pad), :] = jnp.dot(xh_ref[...], w_ref[...], preferred_element_type=F32)

        @pl.when(first)
        def _():
            e_ref[0, pl.ds(pad - hist, hist), :] = prev_ref[...]

    def project(r):
        for w_ref, _, _, _, e_ref, _ in branches:
            e_ref[r % 2, pl.ds(pad, rs), :] = jnp.dot(x_ref[pl.ds(r * rs, rs), :], w_ref[...],
                                                      preferred_element_type=F32)

    def hand_over(r):
        for _, _, _, _, e_ref, _ in branches:
            e_ref[(r + 1) % 2, pl.ds(0, pad), :] = e_ref[r % 2, pl.ds(rs, pad), :]

    def activate(r):
        outs = []
        for _, d_ref, b_ref, _, e_ref, _ in branches:
            acc = b_ref[...] + d_ref[pl.ds(hist, 1), :] * e_ref[r % 2, pl.ds(pad, rs), :]
            for j in range(hist):
                acc = acc + d_ref[pl.ds(j, 1), :] * e_ref[r % 2, pl.ds(pad - hist + j, rs), :]
            outs.append(acc)
        g, v = outs
        o_ref[pl.ds(r * rs, rs), :] = (g * _sigmoid(g) * v).astype(o_ref.dtype)

    n_sub = tm // rs
    project(0)
    hand_over(0)
    for r in range(1, n_sub):
        project(r)
        activate(r - 1)
        hand_over(r)
    activate(n_sub - 1)
    for _, _, _, _, e_ref, s_ref in branches:
        s_ref[...] = e_ref[(n_sub - 1) % 2, pl.ds(pad + rs - hist, hist), :]


def ffn_up_act(xn, w_up, prev, dw, bdw, n_seq, seq, tm, tn):
    m, k = xn.shape
    f2 = w_up.shape[1]
    f = f2 // 2
    width = dw.shape[0]
    ncb = f // tn
    pad = 16
    tps = seq // tm
    assert seq % tm == 0 and tm % pad == 0 and f % tn == 0 and width - 1 <= pad
    rs = min(128, tm // 2)
    assert tm % rs == 0 and rs % 8 == 0 and k % 256 == 0
    kern = functools.partial(_ffn_up_kernel, tm=tm, rs=rs, width=width, tiles_per_seq=tps)
    hb = tm // pad
    st_shape = jax.ShapeDtypeStruct((m // tm, width - 1, f), F32)
    act, sg, sv = pl.pallas_call(
        kern,
        grid=(m // tm, ncb),
        in_specs=[pl.BlockSpec((tm, k), lambda i, c: (i, 0)),
                  pl.BlockSpec((pad, k), lambda i, c: (jnp.maximum(i * hb - 1, 0), 0)),
                  pl.BlockSpec((k, tn), lambda i, c: (0, c)),
                  pl.BlockSpec((k, tn), lambda i, c: (0, c + ncb)),
                  pl.BlockSpec((width, tn), lambda i, c: (0, c)),
                  pl.BlockSpec((width, tn), lambda i, c: (0, c + ncb)),
                  pl.BlockSpec((1, tn), lambda i, c: (0, c)),
                  pl.BlockSpec((1, tn), lambda i, c: (0, c + ncb)),
                  pl.BlockSpec((None, width - 1, tn), lambda i, c: (i // tps, 0, c)),
                  pl.BlockSpec((None, width - 1, tn), lambda i, c: (i // tps, 0, c + ncb))],
        out_specs=[pl.BlockSpec((tm, tn), lambda i, c: (i, c)),
                   pl.BlockSpec((None, width - 1, tn), lambda i, c: (i, 0, c)),
                   pl.BlockSpec((None, width - 1, tn), lambda i, c: (i, 0, c))],
        out_shape=[jax.ShapeDtypeStruct((m, f), BF16), st_shape, st_shape],
        scratch_shapes=[pltpu.VMEM((2, pad + rs, tn), F32), pltpu.VMEM((2, pad + rs, tn), F32),
                        pltpu.VMEM((k, tn), BF16), pltpu.VMEM((k, tn), BF16)],
        compiler_params=_cparams(2),
        name="ffn_up_act",
    )(xn, xn, w_up, w_up, dw, dw, bdw.reshape(1, f2), bdw.reshape(1, f2), prev, prev)
    state = jnp.concatenate([sg, sv], axis=-1).reshape(n_seq, tps, width - 1, f2)[:, tps - 1]
    return act, state


def _ffn_act_step_kernel(ug_ref, uv_ref, pg_ref, pv_ref, wg_ref, wv_ref, bg_ref, bv_ref, o_ref, *, width):
    def conv(u_ref, prev_ref, w_ref, b_ref):
        acc = b_ref[...] + w_ref[pl.ds(width - 1, 1), :] * u_ref[...]
        for j in range(width - 1):
            acc = acc + w_ref[pl.ds(j, 1), :] * prev_ref[j]
        return acc
    g = conv(ug_ref, pg_ref, wg_ref, bg_ref)
    v = conv(uv_ref, pv_ref, wv_ref, bv_ref)
    o_ref[...] = (g * _sigmoid(g) * v).astype(o_ref.dtype)


def ffn_act_step(u, prev_t, dw, bdw, tc):
    n, f2 = u.shape
    f = f2 // 2
    width = dw.shape[0]
    ncb = f // tc
    kern = functools.partial(_ffn_act_step_kernel, width=width)
    return pl.pallas_call(
        kern,
        grid=(ncb,),
        in_specs=[pl.BlockSpec((n, tc), lambda c: (0, c)),
                  pl.BlockSpec((n, tc), lambda c: (0, c + ncb)),
                  pl.BlockSpec((width - 1, n, tc), lambda c: (0, 0, c)),
                  pl.BlockSpec((width - 1, n, tc), lambda c: (0, 0, c + ncb)),
                  pl.BlockSpec((width, tc), lambda c: (0, c)),
                  pl.BlockSpec((width, tc), lambda c: (0, c + ncb)),
                  pl.BlockSpec((1, tc), lambda c: (0, c)),
                  pl.BlockSpec((1, tc), lambda c: (0, c + ncb))],
        out_specs=pl.BlockSpec((n, tc), lambda c: (0, c)),
        out_shape=jax.ShapeDtypeStruct((n, f), BF16),
        compiler_params=_cparams(1),
    )(u, u, prev_t, prev_t, dw, dw, bdw.reshape(1, f2), bdw.reshape(1, f2))


def _row_tile(m, cap):
    tm = cap
    while m % tm:
        tm //= 2
    return tm if tm >= 16 else m


def _in_projection(xn, w_main, w_kw, w_g, sizes, tm, hd):
    glu_w, q_w, k_w, v_w, qi_w, ki_w, wi_w, ga_w, gb_w = sizes
    offs = np.concatenate([[0], np.cumsum(sizes)])
    tn = 512
    t = dict(w_transposed=True)
    wide = lambda w, o: 2 * tn if w % (2 * tn) == 0 and o % (2 * tn) == 0 else tn
    glu = matmul_w(xn, w_main, int(offs[0]), glu_w, tm, wide(glu_w, int(offs[0])), name="mm_glu", **t)
    q = matmul_w(xn, w_main, int(offs[1]), q_w, tm, wide(q_w, int(offs[1])), name="mm_q", out_dtype=BF16, **t)
    k = matmul_heads(xn, w_main, int(offs[2]), k_w // hd, hd, tm, "mm_k")
    v = matmul_heads(xn, w_main, int(offs[3]), v_w // hd, hd, tm, "mm_v")
    qi = matmul_w(xn, w_main, int(offs[4]), qi_w, tm, min(wide(qi_w, int(offs[4])), qi_w), name="mm_qi",
                  out_dtype=BF16, **t)
    kw = matmul_w(xn, w_kw, 0, LANES, tm, LANES, name="mm_kw", **t)
    gates = matmul_w(xn, w_g, 0, ga_w + gb_w, tm, wide(ga_w + gb_w, 0), name="mm_gates", sigmoid=True, **t)
    return glu, q, k, v, qi, kw, gates


def kernel(x_prompt, x_sample, cache_k, cache_v, cache_kidx, state_conv, state_ffn, page_table, rel_bias,
           norm_attn, w_in, dw_conv, b_dw_conv, ln_conv_g, ln_conv_b, w_conv_out, w_o, norm_ffn, w_up, dw_ffn,
           b_dw_ffn, w_down, norm_final):
    bsz, seq, dm = x_prompt.shape
    nd, dec_seq, _ = x_sample.shape
    depth, n_pool, page, n_kv, hd = cache_k.shape
    idx_dim = cache_kidx.shape[-1]
    n_pages = page_table.shape[1]
    past = n_pages * page
    width, dconv = dw_conv.shape[1:]
    fwidth = dw_ffn.shape[1]
    f = w_down.shape[1]
    n_heads = w_o.shape[1] // hd
    d_attn = n_heads * hd
    d_kv = n_kv * hd
    n_in = w_in.shape[2]
    idx_heads = (n_in - 2 * dconv - d_attn - 2 * d_kv - idx_dim - 2 * dm) // (idx_dim + 1)
    sizes = (2 * dconv, d_attn, d_kv, d_kv, idx_heads * idx_dim, idx_dim, idx_heads, dm, dm)
    assert sum(sizes) == n_in and depth == 1 and dec_seq == 1 and page == LANES and d_attn == dm

    mp = bsz * seq
    xp = x_prompt.reshape(mp, dm)
    xs = x_sample.reshape(nd, dm)
    bias3, bias_s = bias_tables(rel_bias, page, n_kv)
    tmp = _row_tile(mp, 2048)
    drop = lambda a: a.reshape(a.shape[1:])
    (norm_attn, w_in, dw_conv, b_dw_conv, ln_conv_g, ln_conv_b, w_conv_out, w_o, norm_ffn, w_up, dw_ffn,
     b_dw_ffn, w_down, state_conv, state_ffn) = map(drop, (
         norm_attn, w_in, dw_conv, b_dw_conv, ln_conv_g, ln_conv_b, w_conv_out, w_o, norm_ffn, w_up, dw_ffn,
         b_dw_ffn, w_down, state_conv, state_ffn))
    kidx_pool = jnp.swapaxes(cache_kidx.reshape(n_pool, page, idx_dim), 1, 2)
    k_pool = cache_k.reshape(n_pool, page * n_kv, hd)
    v_pool = cache_v.reshape(n_pool, page * n_kv, hd)

    n_aligned = sum(sizes[:5])
    n_small = idx_dim + idx_heads
    w_in_t = jnp.swapaxes(w_in, 0, 1)
    w_main = cast_rows_bf16(w_in_t, 0, n_aligned)
    w_kw = cast_rows_bf16(w_in_t, n_aligned, LANES, n_small)
    w_g = cast_rows_bf16(w_in_t, n_aligned + n_small, 2 * dm)
    w_o, w_down = cast_bf16(w_o), cast_bf16(w_down)

    xn = rmsnorm_rows(xp, norm_attn, BF16, 512)
    glu, q, k, v, qi, kw, gates = _in_projection(xn, w_main, w_kw, w_g, sizes, tmp, hd)
    conv0 = jnp.zeros((bsz, width - 1, dconv), F32)
    conv_out, conv_state_p = conv_branch_prompt(glu, conv0, dw_conv, b_dw_conv, ln_conv_g, ln_conv_b,
                                                w_conv_out, bsz, seq, 256)
    mixed = attn_prompt(qi, kw, q, k, v, conv_out, gates, bias3, bsz, seq, n_kv, idx_dim, idx_heads)
    x2, xn2 = matmul_res_norm(mixed, w_o, xp, norm_ffn, _row_tile(mp, 512))
    ffn0 = jnp.zeros((bsz, fwidth - 1, 2 * f), F32)
    act, ffn_state_p = ffn_up_act(xn2, w_up, ffn0, dw_ffn, b_dw_ffn, bsz, seq, _row_tile(seq, 2048), 512)
    tk_down = f // 4 if f % (4 * LANES) == 0 else f
    tk_p = 512 if f % 512 == 0 else tk_down
    y_prompt = matmul_res_norm_out(act, w_down, x2, norm_final, _row_tile(mp, 1024), tk_p).reshape(bsz, seq, dm)

    xns = rmsnorm_rows(xs, norm_attn, BF16, nd)
    glu_s, q_s, k_s, v_s, qi_s, kw_s, gates_s = _in_projection(xns, w_main, w_kw, w_g, sizes, nd, hd)
    ki_s = kw_s[:, :idx_dim]
    wi_s = kw_s[:, idx_dim:idx_dim + idx_heads]
    sc_prev_t = jnp.swapaxes(state_conv, 0, 1)
    conv_out_s, u_conv_s = conv_branch_step(glu_s, sc_prev_t, dw_conv, b_dw_conv, ln_conv_g, ln_conv_b,
                                            w_conv_out)
    conv_state_s = jnp.concatenate([state_conv[:, 1:], u_conv_s[:, None, :]], axis=1)

    pg = 16 if n_pages % 16 == 0 else 8
    pg_idx = 32 if n_pages % 32 == 0 else pg
    scores3, sself3 = sample_scores(page_table, qi_s.reshape(nd, idx_heads, idx_dim),
                                    wi_s.reshape(nd, idx_heads, 1), ki_s.reshape(nd, 1, idx_dim), kidx_pool,
                                    pg_idx)
    topk_s = min(TOPK_MAX, (past + dec_seq) // 4)
    sel4, selself = sample_select(scores3.reshape(nd, past), sself3.reshape(nd, LANES), topk_s, n_kv)
    group = n_heads // n_kv
    k_rep = jnp.repeat(k_s.reshape(nd, n_kv, hd), group, axis=1)
    v_rep = jnp.repeat(v_s.reshape(nd, n_kv, hd), group, axis=1)
    attn_s = sample_attention(page_table, q_s.reshape(nd, n_heads, hd), k_rep, v_rep,
                              sel4.reshape(nd, n_pages // pg, 1, pg * page * n_kv), selself.reshape(nd, 1, LANES),
                              bias_s,
                              rel_bias[0].reshape(n_heads, 1), k_pool, v_pool, pg, n_kv)
    mixed_s = gated_mix(gates_s, conv_out_s, attn_s.reshape(nd, dm))
    x2s, xn2s = matmul_res_norm(mixed_s, w_o, xs, norm_ffn, nd)
    u_s = matmul_w(xn2s, w_up, 0, 2 * f, nd, 512)
    sf_prev_t = jnp.swapaxes(state_ffn, 0, 1)
    act_s = ffn_act_step(u_s, sf_prev_t, dw_ffn, b_dw_ffn, 512)
    y_sample = matmul_res_norm_out(act_s, w_down, x2s, norm_final, nd, tk_down).reshape(nd, dec_seq, dm)
    ffn_state_s = jnp.concatenate([state_ffn[:, 1:], u_s[:, None, :]], axis=1)

    return (y_prompt, y_sample,
            k.reshape(1, bsz, seq, n_kv, hd), v.reshape(1, bsz, seq, n_kv, hd),
            kw.reshape(bsz, seq, LANES)[None, :, :, :idx_dim],
            conv_state_p[None], ffn_state_p[None],
            k_s.reshape(1, nd, dec_seq, n_kv, hd), v_s.reshape(1, nd, dec_seq, n_kv, hd),
            ki_s.reshape(1, nd, dec_seq, idx_dim),
            conv_state_s[None], ffn_state_s[None])
```
